```python
import functools
import jax, jax.numpy as jnp
from jax import lax
import numpy as np

D_MODEL = 2048
BATCH = 1
SEQ = 8192
DEPTH = 1
DEC_BATCH = 128
DEC_SEQ = 1
PAST_LEN = 16384
PAGE_SIZE = 128

HEAD_DIM = 64
D_MIX = D_MODEL
D_ATTN = D_MIX // 2
D_RWKV = D_MIX - D_ATTN
N_Q_HEADS = D_ATTN // HEAD_DIM
N_KV_HEADS = N_Q_HEADS // 4
Q_PER_KV = N_Q_HEADS // N_KV_HEADS
D_KV = N_KV_HEADS * HEAD_DIM
WINDOW = 128
ATTN_BLOCK = WINDOW
ATTN_SCALE = HEAD_DIM ** -0.5
N_RWKV_HEADS = D_RWKV // HEAD_DIM
W_LORA = max(32, int(round(1.8 * D_RWKV ** 0.5 / 32)) * 32)
A_LORA = W_LORA
G_LORA = max(32, int(round(0.6 * D_RWKV ** 0.8 / 32)) * 32)
D_SHIFT = 3 * D_RWKV + W_LORA + A_LORA + G_LORA
D_IN = D_ATTN + 2 * D_KV + D_SHIFT
RWKV_SPLITS = (D_RWKV, 2 * D_RWKV, 3 * D_RWKV, 3 * D_RWKV + W_LORA, 3 * D_RWKV + W_LORA + A_LORA)
N_GROUPS = 4
EXPERTS_PER_GROUP = 8
N_EXPERTS = N_GROUPS * EXPERTS_PER_GROUP
TOP_K_INNER = 2
D_EXPERT = 512
MOE_BLOCK = 128
ALPHA = (2 * DEPTH) ** 0.25
BETA = (8 * DEPTH) ** -0.25
LN_EPS = 1e-5
GN_EPS = 64e-5

kernel_name = "hymba_swa_sink_rwkv7_hmoe_deepnorm_step"


def layer_norm(x, g, b):
    xf = x.astype(jnp.float32)
    mu = xf.mean(-1, keepdims=True)
    var = jnp.square(xf - mu).mean(-1, keepdims=True)
    return ((xf - mu) * lax.rsqrt(var + LN_EPS) * g.astype(jnp.float32) + b.astype(jnp.float32)).astype(x.dtype)


def sink_softmax(s, mask, sink):
    s = jnp.where(mask, s, -jnp.inf)
    m = jnp.maximum(s.max(-1, keepdims=True), sink)
    p = jnp.exp(s - m)
    return p / (p.sum(-1, keepdims=True) + jnp.exp(sink - m))


def prompt_attention(q, k, v, sinks):
    B, T = q.shape[:2]
    nb = T // ATTN_BLOCK
    qb = q.reshape(B, nb, ATTN_BLOCK, N_KV_HEADS, Q_PER_KV, HEAD_DIM)
    kb = k.reshape(B, nb, ATTN_BLOCK, N_KV_HEADS, HEAD_DIM)
    vb = v.reshape(B, nb, ATTN_BLOCK, N_KV_HEADS, HEAD_DIM)
    pad = ((0, 0), (1, 0), (0, 0), (0, 0), (0, 0))
    kpair = jnp.concatenate([jnp.pad(kb, pad)[:, :-1], kb], axis=2)
    vpair = jnp.concatenate([jnp.pad(vb, pad)[:, :-1], vb], axis=2)
    i = jnp.arange(ATTN_BLOCK)[:, None]
    j = jnp.arange(2 * ATTN_BLOCK)[None, :]
    diff = i + ATTN_BLOCK - j
    band = (diff >= 0) & (diff <= WINDOW)
    exists = (jnp.arange(nb)[:, None, None] > 0) | (j >= ATTN_BLOCK)[None]
    mask = (band[None] & exists)[None, :, None, None]
    s = jnp.einsum('bnqkgd,bnskd->bnkgqs', qb, kpair, preferred_element_type=jnp.float32) * ATTN_SCALE
    sink = sinks.astype(jnp.float32).reshape(N_KV_HEADS, Q_PER_KV)[:, :, None, None]
    p = sink_softmax(s, mask, sink)
    o = jnp.einsum('bnkgqs,bnskd->bnqkgd', p.astype(v.dtype), vpair)
    return o.reshape(B, T, D_ATTN), k[:, -WINDOW:], v[:, -WINDOW:]


def sample_attention(q, k, v, sinks, k_buf, v_buf):
    B, S = q.shape[:2]
    kc = jnp.concatenate([k_buf.astype(k.dtype), k], axis=1)
    vc = jnp.concatenate([v_buf.astype(v.dtype), v], axis=1)
    i = jnp.arange(S)[:, None]
    j = jnp.arange(WINDOW + S)[None, :]
    diff = i + WINDOW - j
    mask = ((diff >= 0) & (diff <= WINDOW) & (PAST_LEN - WINDOW + j >= 0))[None, None, None]
    qg = q.reshape(B, S, N_KV_HEADS, Q_PER_KV, HEAD_DIM)
    s = jnp.einsum('bqkgd,bskd->bkgqs', qg, kc, preferred_element_type=jnp.float32) * ATTN_SCALE
    sink = sinks.astype(jnp.float32).reshape(N_KV_HEADS, Q_PER_KV)[:, :, None, None]
    p = sink_softmax(s, mask, sink)
    o = jnp.einsum('bkgqs,bskd->bqkgd', p.astype(vc.dtype), vc)
    return o.reshape(B, S, D_ATTN), kc[:, -WINDOW:], vc[:, -WINDOW:]


def wkv_scan(S0, r, w, k, v, a, b):
    def step(S, inp):
        r_t, w_t, k_t, v_t, a_t, b_t = inp
        sa = jnp.einsum('bhij,bhj->bhi', S, a_t)
        S = S * w_t[:, :, None, :] + sa[..., None] * b_t[:, :, None, :] + v_t[..., None] * k_t[:, :, None, :]
        return S, jnp.einsum('bhij,bhj->bhi', S, r_t)
    xs = tuple(jnp.moveaxis(t, 1, 0) for t in (r, w, k, v, a, b))
    S_T, ys = lax.scan(step, S0, xs)
    return S_T, jnp.moveaxis(ys, 0, 1)


def rwkv_mixer(feat, shift_prev, S0, p):
    B, T, _ = feat.shape
    f32 = jnp.float32
    shifted = jnp.concatenate([shift_prev[:, None, :].astype(feat.dtype), feat[:, :-1]], axis=1)
    mixed = (feat + (shifted - feat) * p['shift_mu']).astype(f32)
    r, k, v, xw, xa, xg = jnp.split(mixed, RWKV_SPLITS, axis=-1)
    w = -jax.nn.softplus(-(p['w0'].astype(f32) + jnp.tanh(xw) @ p['w_decay_up'].astype(f32))) - 0.5
    decay = jnp.exp(-jnp.exp(w))
    a = jax.nn.sigmoid(p['a0'].astype(f32) + xa @ p['w_a_up'].astype(f32))
    g = jax.nn.sigmoid(xg) @ p['w_g_up'].astype(f32)
    heads = lambda t: t.reshape(B, T, N_RWKV_HEADS, HEAD_DIM)
    kk = heads(k * p['k_k'].astype(f32))
    kk = kk / jnp.maximum(jnp.linalg.norm(kk, axis=-1, keepdims=True), 1e-12)
    k = k * (1.0 + (a - 1.0) * p['k_a'].astype(f32))
    rh, kh, vh, ah = heads(r), heads(k), heads(v), heads(a)
    S_T, y = wkv_scan(S0.astype(f32), rh, heads(decay), kh, vh, -kk, kk * ah)
    mu = y.mean(-1, keepdims=True)
    var = jnp.square(y - mu).mean(-1, keepdims=True)
    y = ((y - mu) * lax.rsqrt(var + GN_EPS)).reshape(B, T, D_RWKV) * p['gn_g'].astype(f32) + p['gn_b'].astype(f32)
    bonus = (rh * kh * p['r_k'].astype(f32)).sum(-1, keepdims=True) * vh
    out = (y + bonus.reshape(B, T, D_RWKV)) * g
    return out.astype(feat.dtype), S_T, feat[:, -1]


def hier_moe(x, p):
    B, T, D = x.shape
    N = B * T
    f32 = jnp.float32
    x2d = x.reshape(N, D)
    lc = (x2d @ p['w_coarse']).astype(f32) + p['b_coarse'].astype(f32)
    g_sel = jnp.argmax(lc, axis=-1)
    p_group = jnp.take_along_axis(jax.nn.softmax(lc, axis=-1), g_sel[:, None], axis=-1)
    lf = ((x2d @ p['w_fine']).astype(f32) + p['b_fine'].astype(f32)).reshape(N, N_GROUPS, EXPERTS_PER_GROUP)
    lf_sel = jnp.take_along_axis(lf, g_sel[:, None, None], axis=1)[:, 0]
    top_v, top_i = lax.top_k(lf_sel, TOP_K_INNER)
    gates = p_group * jax.nn.softmax(top_v, axis=-1)
    expert_id = (g_sel[:, None] * EXPERTS_PER_GROUP + top_i).astype(jnp.int32)
    A = N * TOP_K_INNER
    flat_e = expert_id.reshape(A)
    order = jnp.argsort(flat_e)
    sorted_e = flat_e[order]
    token = (order // TOP_K_INNER).astype(jnp.int32)
    counts = jnp.bincount(flat_e, length=N_EXPERTS)
    padded = (counts + MOE_BLOCK - 1) // MOE_BLOCK * MOE_BLOCK
    start = jnp.cumsum(counts) - counts
    pend = jnp.cumsum(padded)
    pstart = pend - padded
    dest = pstart[sorted_e] + jnp.arange(A) - start[sorted_e]
    n_blocks = -(-(A + N_EXPERTS * (MOE_BLOCK - 1)) // MOE_BLOCK)
    rows = jnp.full((n_blocks * MOE_BLOCK,), N, jnp.int32).at[dest].set(token)
    block_e = jnp.minimum(jnp.searchsorted(pend, jnp.arange(n_blocks) * MOE_BLOCK, side='right'), N_EXPERTS - 1)
    x_pad = jnp.concatenate([x2d, jnp.zeros((1, D), x2d.dtype)], axis=0)

    def expert_block(args):
        idx, e = args
        xb = x_pad[idx]
        h = jax.nn.silu(xb @ p['w_exp_gate'][e]) * (xb @ p['w_exp_up'][e])
        return h @ p['w_exp_down'][e]

    y_blocks = lax.map(expert_block, (rows.reshape(n_blocks, MOE_BLOCK), block_e))
    y_rows = y_blocks.reshape(n_blocks * MOE_BLOCK, D)[dest].astype(f32) * gates.reshape(A)[order][:, None]
    y = jnp.zeros((N, D), f32).at[token].add(y_rows)
    return y.astype(x.dtype).reshape(B, T, D)


def layer_forward(x, attn_fn, shift_prev, wkv_prev, p):
    B, T, _ = x.shape
    h = x @ p['w_in']
    q, k, v, feat = jnp.split(h, (D_ATTN, D_ATTN + D_KV, D_ATTN + 2 * D_KV), axis=-1)
    q = q.reshape(B, T, N_Q_HEADS, HEAD_DIM)
    k = k.reshape(B, T, N_KV_HEADS, HEAD_DIM)
    v = v.reshape(B, T, N_KV_HEADS, HEAD_DIM)
    attn, k_win, v_win = attn_fn(q, k, v, p['attn_sinks'])
    rwkv, wkv_new, shift_new = rwkv_mixer(feat, shift_prev, wkv_prev, p)
    mix = jnp.concatenate([attn, rwkv], axis=-1) @ p['w_out']
    x1 = layer_norm(ALPHA * x + mix, p['ln1_g'], p['ln1_b'])
    x2 = layer_norm(ALPHA * x1 + hier_moe(x1, p), p['ln2_g'], p['ln2_b'])
    return x2, k_win, v_win, wkv_new, shift_new


def setup_inputs(seed: int = 0) -> dict:
    key = jax.random.key(seed)
    ks = jax.random.split(key, 32)
    f32 = jnp.float32
    nrm = lambda k, shape, scale: jax.random.normal(k, shape, f32) * scale
    col_scale = jnp.concatenate([
        jnp.ones((D_ATTN + D_KV,), f32), jnp.full((D_KV,), BETA, f32),
        jnp.ones((2 * D_RWKV,), f32), jnp.full((D_RWKV,), BETA, f32),
        jnp.ones((W_LORA + A_LORA + G_LORA,), f32)])
    return {
        'x_prompt': nrm(ks[0], (BATCH, SEQ, D_MODEL), 1.0),
        'x_sample': nrm(ks[1], (DEC_BATCH, DEC_SEQ, D_MODEL), 1.0),
        'cache_k_win': nrm(ks[2], (DEPTH, DEC_BATCH, WINDOW, N_KV_HEADS, HEAD_DIM), 1.0),
        'cache_v_win': nrm(ks[3], (DEPTH, DEC_BATCH, WINDOW, N_KV_HEADS, HEAD_DIM), BETA),
        'state_wkv': nrm(ks[4], (DEPTH, DEC_BATCH, N_RWKV_HEADS, HEAD_DIM, HEAD_DIM), 0.3),
        'state_shift': nrm(ks[5], (DEPTH, DEC_BATCH, D_SHIFT), 1.0),
        'w_in': nrm(ks[6], (DEPTH, D_MODEL, D_IN), D_MODEL ** -0.5) * col_scale,
        'attn_sinks': nrm(ks[7], (DEPTH, N_Q_HEADS), 0.5),
        'shift_mu': jax.random.uniform(ks[8], (DEPTH, D_SHIFT), f32),
        'w0': jax.random.uniform(ks[9], (DEPTH, D_RWKV), f32, -6.0, -1.0),
        'w_decay_up': nrm(ks[10], (DEPTH, W_LORA, D_RWKV), 0.1 * W_LORA ** -0.5),
        'a0': nrm(ks[11], (DEPTH, D_RWKV), 0.1),
        'w_a_up': nrm(ks[12], (DEPTH, A_LORA, D_RWKV), 0.5 * A_LORA ** -0.5),
        'w_g_up': nrm(ks[13], (DEPTH, G_LORA, D_RWKV), G_LORA ** -0.5),
        'k_k': 0.85 + nrm(ks[14], (DEPTH, D_RWKV), 0.05),
        'k_a': 1.0 + nrm(ks[15], (DEPTH, D_RWKV), 0.05),
        'r_k': nrm(ks[16], (DEPTH, N_RWKV_HEADS, HEAD_DIM), 0.1),
        'gn_g': 1.0 + nrm(ks[17], (DEPTH, D_RWKV), 0.05),
        'gn_b': nrm(ks[18], (DEPTH, D_RWKV), 0.01),
        'w_out': nrm(ks[19], (DEPTH, D_MIX, D_MODEL), BETA * D_MIX ** -0.5),
        'ln1_g': 1.0 + nrm(ks[20], (DEPTH, D_MODEL), 0.05),
        'ln1_b': nrm(ks[21], (DEPTH, D_MODEL), 0.01),
        'w_coarse': nrm(ks[22], (DEPTH, D_MODEL, N_GROUPS), D_MODEL ** -0.5),
        'b_coarse': nrm(ks[23], (DEPTH, N_GROUPS), 0.01),
        'w_fine': nrm(ks[24], (DEPTH, D_MODEL, N_EXPERTS), D_MODEL ** -0.5),
        'b_fine': nrm(ks[25], (DEPTH, N_EXPERTS), 0.01),
        'w_exp_gate': nrm(ks[26], (DEPTH, N_EXPERTS, D_MODEL, D_EXPERT), D_MODEL ** -0.5),
        'w_exp_up': nrm(ks[27], (DEPTH, N_EXPERTS, D_MODEL, D_EXPERT), D_MODEL ** -0.5),
        'w_exp_down': nrm(ks[28], (DEPTH, N_EXPERTS, D_EXPERT, D_MODEL), BETA * D_EXPERT ** -0.5),
        'ln2_g': 1.0 + nrm(ks[29], (DEPTH, D_MODEL), 0.05),
        'ln2_b': nrm(ks[30], (DEPTH, D_MODEL), 0.01),
    }


def reference(x_prompt, x_sample, cache_k_win, cache_v_win, state_wkv, state_shift,
              w_in, attn_sinks, shift_mu, w0, w_decay_up, a0, w_a_up, w_g_up, k_k, k_a, r_k,
              gn_g, gn_b, w_out, ln1_g, ln1_b, w_coarse, b_coarse, w_fine, b_fine,
              w_exp_gate, w_exp_up, w_exp_down, ln2_g, ln2_b):
    h_p, h_s = x_prompt, x_sample
    kp, vp, sp, shp = [], [], [], []
    kq, vq, sq, shq = [], [], [], []
    for l in range(DEPTH):
        p = dict(w_in=w_in[l], attn_sinks=attn_sinks[l], shift_mu=shift_mu[l], w0=w0[l],
                 w_decay_up=w_decay_up[l], a0=a0[l], w_a_up=w_a_up[l], w_g_up=w_g_up[l],
                 k_k=k_k[l], k_a=k_a[l], r_k=r_k[l], gn_g=gn_g[l], gn_b=gn_b[l], w_out=w_out[l],
                 ln1_g=ln1_g[l], ln1_b=ln1_b[l], w_coarse=w_coarse[l], b_coarse=b_coarse[l],
                 w_fine=w_fine[l], b_fine=b_fine[l], w_exp_gate=w_exp_gate[l], w_exp_up=w_exp_up[l],
                 w_exp_down=w_exp_down[l], ln2_g=ln2_g[l], ln2_b=ln2_b[l])
        zero_shift = jnp.zeros((h_p.shape[0], D_SHIFT), h_p.dtype)
        zero_wkv = jnp.zeros((h_p.shape[0], N_RWKV_HEADS, HEAD_DIM, HEAD_DIM), jnp.float32)
        h_p, k_w, v_w, s_w, sh_w = layer_forward(h_p, prompt_attention, zero_shift, zero_wkv, p)
        kp.append(k_w); vp.append(v_w); sp.append(s_w); shp.append(sh_w)
        attn_fn = functools.partial(sample_attention, k_buf=cache_k_win[l], v_buf=cache_v_win[l])
        h_s, k_w, v_w, s_w, sh_w = layer_forward(h_s, attn_fn, state_shift[l], state_wkv[l], p)
        kq.append(k_w); vq.append(v_w); sq.append(s_w); shq.append(sh_w)
    return (h_p, h_s, jnp.stack(kp), jnp.stack(vp), jnp.stack(sp), jnp.stack(shp),
            jnp.stack(kq), jnp.stack(vq), jnp.stack(sq), jnp.stack(shq))
```

```python
import functools
import math

import jax
import jax.numpy as jnp
from jax import lax
from jax.experimental import pallas as pl
from jax.experimental.pallas import tpu as pltpu

F32 = jnp.float32
BF16 = jnp.bfloat16

D_MODEL = 2048
SEQ = 8192
DEC_BATCH = 128
HEAD_DIM = 64
D_ATTN = 1024
D_RWKV = 1024
N_Q_HEADS = 16
N_KV_HEADS = 4
Q_PER_KV = 4
D_KV = 256
WINDOW = 128
ATTN_SCALE = HEAD_DIM ** -0.5
N_RWKV_HEADS = 16
W_LORA = 64
A_LORA = 64
G_LORA = 160
D_SHIFT = 3 * D_RWKV + W_LORA + A_LORA + G_LORA
D_QKV = D_ATTN + 2 * D_KV
N_GROUPS = 4
EXPERTS_PER_GROUP = 8
N_EXPERTS = 32
D_EXPERT = 512
ALPHA = 2.0 ** 0.25
LN_EPS = 1e-5
GN_EPS = 64e-5

SUBLANES = 8
LANES = 128
VMEM_LIMIT = 52 * 1024 * 1024

XW_PAD = LANES
XA_PAD = LANES
XG_PAD = 2 * LANES
OFF_XW = 3 * D_RWKV
OFF_XA = OFF_XW + XW_PAD
OFF_XG = OFF_XA + XA_PAD
D_FEAT = OFF_XG + XG_PAD

CHUNK = 64
HEADS_PER_TILE = LANES // HEAD_DIM
N_PAIRS = N_RWKV_HEADS // HEADS_PER_TILE
SOLVE_LEVELS = int(math.log2(CHUNK))

MOE_BLOCK = 256
ROUTE_FINE_OFF = N_GROUPS

NN = (((1,), (0,)), ((), ()))
NT = (((1,), (1,)), ((), ()))


def _dot(a, b, dims=NN):
    return lax.dot_general(a, b, dims, preferred_element_type=F32)


def _dot1(a, b, dims=NN):
    return _dot(a.astype(BF16), b.astype(BF16), dims)


def _split(x):
    hi = x.astype(BF16)
    lo = (x - hi.astype(F32)).astype(BF16)
    return hi, lo


def _dot3(a, b, dims=NN):
    ah, al = _split(a)
    bh, bl = _split(b)
    return _dot(ah, bh, dims) + (_dot(ah, bl, dims) + _dot(al, bh, dims))


def _split3(x):
    hi = x.astype(BF16)
    rem = x - hi.astype(F32)
    mid = rem.astype(BF16)
    lo = (rem - mid.astype(F32)).astype(BF16)
    return hi, mid, lo


def _dot_exact_lhs(a_bf16, b, dims=NN):
    bh, bm, bl = _split3(b)
    return _dot(a_bf16, bh, dims) + (_dot(a_bf16, bm, dims) + _dot(a_bf16, bl, dims))


def _dot_exact_rhs(a, b_bf16, dims=NN):
    ah, am, al = _split3(a)
    return _dot(ah, b_bf16, dims) + (_dot(am, b_bf16, dims) + _dot(al, b_bf16, dims))


def _div_pow2(x, d):
    return lax.shift_right_logical(x, jnp.int32(int(math.log2(d))))


def _mod_pow2(x, d):
    return lax.bitwise_and(x, jnp.int32(d - 1))


def _sigmoid(x):
    return 1.0 / (1.0 + jnp.exp(-x))


def _softplus(x):
    return jnp.maximum(x, 0.0) + jnp.log(1.0 + jnp.exp(-jnp.abs(x)))


def _layer_norm(z, g, b):
    mu = jnp.mean(z, axis=-1, keepdims=True)
    d = z - mu
    var = jnp.mean(d * d, axis=-1, keepdims=True)
    return d * lax.rsqrt(var + LN_EPS) * g + b


def _cparams(sem):
    return pltpu.CompilerParams(dimension_semantics=sem, vmem_limit_bytes=VMEM_LIMIT)


def _matmul_kernel(x_ref, w_ref, o_ref):
    o_ref[...] = _dot(x_ref[...].astype(BF16), w_ref[...].astype(BF16))


def _matmul(x, w, n_out, tm, tn, name):
    m, k = x.shape
    tm = min(tm, m)
    return pl.pallas_call(
        _matmul_kernel,
        out_shape=jax.ShapeDtypeStruct((m, n_out), F32),
        grid=(n_out // tn, m // tm),
        in_specs=[pl.BlockSpec((tm, k), lambda j, i: (i, 0)),
                  pl.BlockSpec((k, tn), lambda j, i: (0, j))],
        out_specs=pl.BlockSpec((tm, tn), lambda j, i: (i, j)),
        compiler_params=_cparams(("arbitrary", "arbitrary")),
        name=name,
    )(x, w)


def _prompt_attn_kernel(q_ref, kvp_ref, kvc_ref, sink_ref, o_ref):
    blk = pl.program_id(0)
    q = q_ref[...]
    kv_prev = kvp_ref[...]
    kv_cur = kvc_ref[...]
    qi = _mod_pow2(lax.broadcasted_iota(jnp.int32, (Q_PER_KV * WINDOW, 2 * WINDOW), 0), WINDOW)
    kj = lax.broadcasted_iota(jnp.int32, (Q_PER_KV * WINDOW, 2 * WINDOW), 1)
    diff = qi + WINDOW - kj
    mask = (diff >= 0) & (diff <= WINDOW) & ((blk > 0) | (kj >= WINDOW))
    row_head = _div_pow2(lax.broadcasted_iota(jnp.int32, (Q_PER_KV * WINDOW, 1), 0), WINDOW)
    outs = []
    for g in range(N_KV_HEADS):
        kc = jnp.concatenate([kv_prev[:, g * HEAD_DIM:(g + 1) * HEAD_DIM],
                              kv_cur[:, g * HEAD_DIM:(g + 1) * HEAD_DIM]], axis=0).astype(BF16)
        vc = jnp.concatenate([kv_prev[:, D_KV + g * HEAD_DIM:D_KV + (g + 1) * HEAD_DIM],
                              kv_cur[:, D_KV + g * HEAD_DIM:D_KV + (g + 1) * HEAD_DIM]], axis=0).astype(BF16)
        qs = jnp.concatenate(
            [q[:, (g * Q_PER_KV + h) * HEAD_DIM:(g * Q_PER_KV + h + 1) * HEAD_DIM] for h in range(Q_PER_KV)],
            axis=0).astype(BF16)
        s = _dot(qs, kc, NT) * ATTN_SCALE
        s = jnp.where(mask, s, -jnp.inf)
        sink = jnp.zeros((Q_PER_KV * WINDOW, 1), F32)
        for h in range(Q_PER_KV):
            sink = jnp.where(row_head == h, sink_ref[g * Q_PER_KV + h], sink)
        m = jnp.maximum(jnp.max(s, axis=-1, keepdims=True), sink)
        p = jnp.exp(s - m)
        denom = jnp.sum(p, axis=-1, keepdims=True) + jnp.exp(sink - m)
        p = p / denom
        o = _dot(p.astype(BF16), vc)
        for h in range(Q_PER_KV):
            outs.append(o[h * WINDOW:(h + 1) * WINDOW, :])
    o_ref[...] = jnp.concatenate(outs, axis=1)


def _prompt_attention(h_attn, sinks):
    nb = SEQ // WINDOW
    return pl.pallas_call(
        _prompt_attn_kernel,
        out_shape=jax.ShapeDtypeStruct((SEQ, D_ATTN), F32),
        grid=(nb,),
        in_specs=[pl.BlockSpec((WINDOW, D_ATTN), lambda i: (i, 0)),
                  pl.BlockSpec((WINDOW, 2 * D_KV), lambda i: (jnp.maximum(i - 1, 0), 2)),
                  pl.BlockSpec((WINDOW, 2 * D_KV), lambda i: (i, 2)),
                  pl.BlockSpec(memory_space=pltpu.SMEM)],
        out_specs=pl.BlockSpec((WINDOW, D_ATTN), lambda i: (i, 0)),
        compiler_params=_cparams(("arbitrary",)),
        name="prompt_attention",
    )(h_attn, h_attn, h_attn, sinks)


SAMPLE_ATTN_TILE = 8


def _sample_attn_kernel(q_ref, knew_ref, vnew_ref, ck_ref, cv_ref, sink_ref, o_ref, kwin_ref, vwin_ref):
    lane = lax.broadcasted_iota(jnp.int32, (N_Q_HEADS, D_KV), 1)
    head = lax.broadcasted_iota(jnp.int32, (N_Q_HEADS, D_KV), 0)
    group_mask = _div_pow2(lane, HEAD_DIM) == _div_pow2(head, Q_PER_KV)
    sink = sink_ref[...]
    row = lax.broadcasted_iota(jnp.int32, (WINDOW, D_KV), 0)
    for b in range(SAMPLE_ATTN_TILE):
        q = q_ref[b]
        qbd = jnp.where(group_mask, jnp.concatenate([q] * N_KV_HEADS, axis=1), 0.0).astype(BF16)
        kb = ck_ref[b]
        vb = cv_ref[b]
        kn = knew_ref[b]
        vn = vnew_ref[b]
        s = _dot1(qbd, kb, NT) * ATTN_SCALE
        s_new = jnp.sum(qbd.astype(F32) * kn.astype(BF16).astype(F32), axis=-1, keepdims=True) * ATTN_SCALE
        m = jnp.maximum(jnp.maximum(jnp.max(s, axis=-1, keepdims=True), s_new), sink)
        p = jnp.exp(s - m)
        p_new = jnp.exp(s_new - m)
        denom = jnp.sum(p, axis=-1, keepdims=True) + p_new + jnp.exp(sink - m)
        p = p / denom
        p_new = (p_new / denom).astype(BF16).astype(F32)
        o_full = _dot1(p, vb) + p_new * vn.astype(BF16).astype(F32)
        o_full = jnp.where(group_mask, o_full, 0.0)
        o = o_full[:, 0:HEAD_DIM]
        for g in range(1, N_KV_HEADS):
            o = o + o_full[:, g * HEAD_DIM:(g + 1) * HEAD_DIM]
        o_ref[b] = o
        kwin_ref[b] = jnp.where(row == WINDOW - 1, kn, pltpu.roll(kb, WINDOW - 1, axis=0))
        vwin_ref[b] = jnp.where(row == WINDOW - 1, vn, pltpu.roll(vb, WINDOW - 1, axis=0))


def _sample_attention(q, k_new, v_new, cache_k, cache_v, sinks):
    bt = SAMPLE_ATTN_TILE
    win_spec = pl.BlockSpec((bt, WINDOW, D_KV), lambda i: (i, 0, 0))
    new_spec = pl.BlockSpec((bt, 1, D_KV), lambda i: (i, 0, 0))
    return pl.pallas_call(
        _sample_attn_kernel,
        out_shape=(jax.ShapeDtypeStruct((DEC_BATCH, N_Q_HEADS, HEAD_DIM), F32),
                   jax.ShapeDtypeStruct((DEC_BATCH, WINDOW, D_KV), F32),
                   jax.ShapeDtypeStruct((DEC_BATCH, WINDOW, D_KV), F32)),
        grid=(DEC_BATCH // bt,),
        in_specs=[pl.BlockSpec((bt, N_Q_HEADS, HEAD_DIM), lambda i: (i, 0, 0)),
                  new_spec, new_spec, win_spec, win_spec,
                  pl.BlockSpec((N_Q_HEADS, 1), lambda i: (0, 0))],
        out_specs=(pl.BlockSpec((bt, N_Q_HEADS, HEAD_DIM), lambda i: (i, 0, 0)), win_spec, win_spec),
        compiler_params=_cparams(("arbitrary",)),
        name="sample_attention",
    )(q, k_new, v_new, cache_k, cache_v, sinks)


def _head_ones():
    r = _div_pow2(lax.broadcasted_iota(jnp.int32, (LANES, LANES), 0), HEAD_DIM)
    c = _div_pow2(lax.broadcasted_iota(jnp.int32, (LANES, LANES), 1), HEAD_DIM)
    return jnp.where(r == c, 1.0, 0.0).astype(BF16)


def _head_sum(x, ones):
    parts = [_dot_exact_rhs(x[:, p * LANES:(p + 1) * LANES], ones) for p in range(x.shape[1] // LANES)]
    return jnp.concatenate(parts, axis=1)


def _rwkv_prep(mixed, w0, a0, k_k, k_a, wd, wa, wg, ones):
    r = mixed[:, 0:D_RWKV]
    k = mixed[:, D_RWKV:2 * D_RWKV]
    v = mixed[:, 2 * D_RWKV:3 * D_RWKV]
    xw = mixed[:, OFF_XW:OFF_XW + XW_PAD]
    xa = mixed[:, OFF_XA:OFF_XA + XA_PAD]
    xg = mixed[:, OFF_XG:OFF_XG + XG_PAD]
    w_log = -_softplus(-(w0 + _dot1(jnp.tanh(xw), wd))) - 0.5
    log_decay = -jnp.exp(w_log)
    a = _sigmoid(a0 + _dot1(xa, wa))
    g = _dot1(_sigmoid(xg), wg)
    kk = k * k_k
    norm = jnp.sqrt(_head_sum(kk * kk, ones))
    kk = kk / jnp.maximum(norm, 1e-12)
    k2 = k * (1.0 + (a - 1.0) * k_a)
    return r, log_decay, k2, v, -kk, kk * a, g


def _rwkv_post(y, r, k2, v, g, r_k, gn_g, gn_b, ones):
    inv_n = 1.0 / HEAD_DIM
    mu = _head_sum(y, ones) * inv_n
    d = y - mu
    var = _head_sum(d * d, ones) * inv_n
    yn = d * lax.rsqrt(var + GN_EPS) * gn_g + gn_b
    bonus = _head_sum(r * k2 * r_k, ones) * v
    return (yn + bonus) * g


(OP_AABS, OP_RABS, OP_AN, OP_RN, OP_BN, OP_KN, OP_BH, OP_KH, OP_V) = range(9)
N_OPS = 9


def _prompt_rwkv_kernel(feat_ref, mu_ref, w0_ref, a0_ref, kk_ref, ka_ref, rk_ref, gng_ref, gnb_ref,
                        wd_ref, wa_ref, wg_ref, out_ref, state_ref,
                        prev_ref, s_ref, ops_ref, pc_ref, y_ref):
    c = pl.program_id(0)
    C = CHUNK

    @pl.when(c == 0)
    def _():
        prev_ref[...] = jnp.zeros_like(prev_ref)
        s_ref[...] = jnp.zeros_like(s_ref)

    ones = _head_ones()
    feat = feat_ref[...]
    row = lax.broadcasted_iota(jnp.int32, (C, 1), 0)
    shifted = jnp.where(row == 0, prev_ref[0:1, :], pltpu.roll(feat, 1, axis=0))
    prev_ref[0:1, :] = feat[C - 1:C, :]
    mixed = feat + (shifted - feat) * mu_ref[...]
    r, ld, k2, v, av, bv, g = _rwkv_prep(mixed, w0_ref[...], a0_ref[...], kk_ref[...], ka_ref[...],
                                         wd_ref[...], wa_ref[...], wg_ref[...], ones)

    ti = lax.broadcasted_iota(jnp.int32, (C, C), 0)
    tj = lax.broadcasted_iota(jnp.int32, (C, C), 1)
    tri_incl = jnp.where(tj <= ti, 1.0, 0.0).astype(BF16)
    cs = _dot_exact_lhs(tri_incl, ld)
    cs_ref = cs[C // 2 - 1:C // 2, :]
    cs_end = cs[C - 1:C, :]
    e_prev = jnp.exp(cs - ld)
    e_cur = jnp.exp(cs)
    n_prev = jnp.exp(cs - ld - cs_ref)
    n_cur = jnp.exp(cs - cs_ref)
    n_inv = jnp.exp(cs_ref - cs)
    e_tail = jnp.exp(cs_end - cs)
    ops = {OP_AABS: av * e_prev, OP_RABS: r * e_cur, OP_AN: av * n_prev, OP_RN: r * n_cur,
           OP_BN: bv * n_inv, OP_KN: k2 * n_inv, OP_BH: bv * e_tail, OP_KH: k2 * e_tail, OP_V: v}
    p_end = jnp.exp(cs_end)
    for p in range(N_PAIRS):
        sl = slice(p * LANES, (p + 1) * LANES)
        for idx, val in ops.items():
            ops_ref[p, idx] = val[:, sl]
        pc_ref[p] = jnp.broadcast_to(p_end[:, sl], (SUBLANES, LANES))

    lane1 = lax.broadcasted_iota(jnp.int32, (C, LANES), 1)
    head0 = lane1 < HEAD_DIM
    r2 = lax.broadcasted_iota(jnp.int32, (2 * C, 2 * C), 0)
    c2 = lax.broadcasted_iota(jnp.int32, (2 * C, 2 * C), 1)
    tq = _mod_pow2(r2, C)
    tk = _mod_pow2(c2, C)
    band = (tk < tq) | ((tk == tq) & (r2 >= C))
    blockdiag = _div_pow2(r2, HEAD_DIM) == _div_pow2(c2, HEAD_DIM)
    eye = r2 == c2

    def pair_body(p, carry):
        a_abs = ops_ref[p, OP_AABS]
        r_abs = ops_ref[p, OP_RABS]
        a_n = ops_ref[p, OP_AN]
        r_n = ops_ref[p, OP_RN]
        b_n = ops_ref[p, OP_BN]
        k_n = ops_ref[p, OP_KN]
        b_h = ops_ref[p, OP_BH]
        k_h = ops_ref[p, OP_KH]
        vv = ops_ref[p, OP_V]
        s0 = s_ref[p]

        gy = _dot3(jnp.concatenate([a_abs, r_abs], axis=0), s0)
        g0 = gy[0:C]
        ys0 = gy[C:2 * C]

        lhs0 = jnp.concatenate([jnp.where(head0, a_n, 0.0), jnp.where(head0, r_n, 0.0)], axis=0)
        lhs1 = jnp.concatenate([jnp.where(head0, 0.0, a_n), jnp.where(head0, 0.0, r_n)], axis=0)
        am0 = jnp.where(band, _dot3(lhs0, jnp.concatenate([b_n, k_n], axis=0), NT), 0.0)
        am1 = jnp.where(band, _dot3(lhs1, jnp.concatenate([k_n, b_n], axis=0), NT), 0.0)

        w_top = jnp.concatenate([am0[0:C], am1[0:C]], axis=0)
        n_mat = jnp.where(blockdiag, w_top, 0.0)
        ak = jnp.where(blockdiag, 0.0, w_top)
        v2 = jnp.concatenate([vv, vv], axis=0)
        m_bd = jnp.where(blockdiag, jnp.concatenate([g0, g0], axis=0) + _dot3(ak, v2), 0.0)

        u_bd = m_bd
        nk = n_mat
        for lvl in range(SOLVE_LEVELS):
            u_bd = u_bd + _dot3(nk, u_bd)
            if lvl + 1 < SOLVE_LEVELS:
                nk = _dot3(nk, nk)
        u = u_bd[0:C] + u_bd[C:2 * C]

        u0 = jnp.where(head0, u, 0.0)
        u1 = jnp.where(head0, 0.0, u)
        v0 = jnp.where(head0, vv, 0.0)
        v1 = jnp.where(head0, 0.0, vv)
        y_lhs = jnp.concatenate([am0[C:2 * C], am1[C:2 * C]], axis=1)
        y_rhs = jnp.concatenate([u0, v0, v1, u1], axis=0)
        y_ref[p] = ys0 + _dot3(y_lhs, y_rhs)

        dg = jnp.where(eye, pc_ref[p][0:1, :], 0.0)
        upd_lhs = jnp.concatenate([b_h, k_h, dg], axis=0).T
        upd_rhs = jnp.concatenate([u, vv, s0], axis=0)
        s_ref[p] = jnp.where(blockdiag, _dot3(upd_lhs, upd_rhs), 0.0)
        return carry

    lax.fori_loop(0, N_PAIRS, pair_body, 0)

    y = jnp.concatenate([y_ref[p] for p in range(N_PAIRS)], axis=1)
    out_ref[...] = _rwkv_post(y, r, k2, v, g, rk_ref[...], gng_ref[...], gnb_ref[...], ones)

    @pl.when(c == pl.num_programs(0) - 1)
    def _():
        state_ref[...] = s_ref[...]


def _prompt_rwkv(feat, prm):
    n_chunks = SEQ // CHUNK
    vec = pl.BlockSpec((1, D_RWKV), lambda c: (0, 0))
    full = lambda a: pl.BlockSpec(a.shape, lambda c: (0,) * a.ndim)
    return pl.pallas_call(
        _prompt_rwkv_kernel,
        out_shape=(jax.ShapeDtypeStruct((SEQ, D_RWKV), F32),
                   jax.ShapeDtypeStruct((N_PAIRS, LANES, LANES), F32)),
        grid=(n_chunks,),
        in_specs=[pl.BlockSpec((CHUNK, D_FEAT), lambda c: (c, 0)),
                  pl.BlockSpec((1, D_FEAT), lambda c: (0, 0)),
                  vec, vec, vec, vec, vec, vec, vec,
                  full(prm["wd"]), full(prm["wa"]), full(prm["wg"])],
        out_specs=(pl.BlockSpec((CHUNK, D_RWKV), lambda c: (c, 0)),
                   pl.BlockSpec((N_PAIRS, LANES, LANES), lambda c: (0, 0, 0))),
        scratch_shapes=[pltpu.VMEM((SUBLANES, D_FEAT), F32),
                        pltpu.VMEM((N_PAIRS, LANES, LANES), F32),
                        pltpu.VMEM((N_PAIRS, N_OPS, CHUNK, LANES), F32),
                        pltpu.VMEM((N_PAIRS, SUBLANES, LANES), F32),
                        pltpu.VMEM((N_PAIRS, CHUNK, LANES), F32)],
        compiler_params=_cparams(("arbitrary",)),
        name="prompt_rwkv",
    )(feat, prm["mu"], prm["w0"], prm["a0"], prm["k_k"], prm["k_a"], prm["r_k"], prm["gn_g"], prm["gn_b"],
      prm["wd"], prm["wa"], prm["wg"])


def _sample_prep_kernel(feat_ref, shift_ref, mu_ref, w0_ref, a0_ref, kk_ref, ka_ref, wd_ref, wa_ref, wg_ref,
                        r_ref, w_ref, k_ref, v_ref, a_ref, b_ref, g_ref):
    ones = _head_ones()
    feat = feat_ref[...]
    mixed = feat + (shift_ref[...] - feat) * mu_ref[...]
    r, ld, k2, v, av, bv, g = _rwkv_prep(mixed, w0_ref[...], a0_ref[...], kk_ref[...], ka_ref[...],
                                         wd_ref[...], wa_ref[...], wg_ref[...], ones)
    r_ref[...] = r
    w_ref[...] = jnp.exp(ld)
    k_ref[...] = k2
    v_ref[...] = v
    a_ref[...] = av
    b_ref[...] = bv
    g_ref[...] = g


def _sample_prep(feat, shift, prm):
    out = jax.ShapeDtypeStruct((DEC_BATCH, D_RWKV), F32)
    return pl.pallas_call(
        _sample_prep_kernel,
        out_shape=(out,) * 7,
        compiler_params=pltpu.CompilerParams(vmem_limit_bytes=VMEM_LIMIT),
        name="sample_rwkv_prep",
    )(feat, shift, prm["mu"], prm["w0"], prm["a0"], prm["k_k"], prm["k_a"], prm["wd"], prm["wa"], prm["wg"])


SAMPLE_STEP_TILE = 8


def _sample_step_kernel(s_ref, r_ref, w_ref, k_ref, a_ref, b_ref, vt_ref, yt_ref, snew_ref):
    head_lane = lax.broadcasted_iota(jnp.int32, (HEAD_DIM, N_RWKV_HEADS), 1)

    def seq_body(bi, carry):
        r = r_ref[bi]
        w = w_ref[bi]
        k = k_ref[bi]
        a = a_ref[bi]
        b = b_ref[bi]
        v_t = vt_ref[bi]
        y_t = jnp.zeros((HEAD_DIM, N_RWKV_HEADS), F32)
        for h in range(N_RWKV_HEADS):
            s = s_ref[bi, h]
            sa = jnp.sum(s * a[h:h + 1, :], axis=-1, keepdims=True)
            s_new = s * w[h:h + 1, :] + sa * b[h:h + 1, :] + v_t[:, h:h + 1] * k[h:h + 1, :]
            snew_ref[bi, h] = s_new
            y_t = jnp.where(head_lane == h, jnp.sum(s_new * r[h:h + 1, :], axis=-1, keepdims=True), y_t)
        yt_ref[bi] = y_t
        return carry

    lax.fori_loop(0, SAMPLE_STEP_TILE, seq_body, 0)


def _sample_step(state, r, w, k, a, b, v_t):
    bt = SAMPLE_STEP_TILE
    hs = pl.BlockSpec((bt, N_RWKV_HEADS, HEAD_DIM), lambda i: (i, 0, 0))
    ts = pl.BlockSpec((bt, HEAD_DIM, N_RWKV_HEADS), lambda i: (i, 0, 0))
    ss = pl.BlockSpec((bt, N_RWKV_HEADS, HEAD_DIM, HEAD_DIM), lambda i: (i, 0, 0, 0))
    return pl.pallas_call(
        _sample_step_kernel,
        out_shape=(jax.ShapeDtypeStruct((DEC_BATCH, HEAD_DIM, N_RWKV_HEADS), F32),
                   jax.ShapeDtypeStruct((DEC_BATCH, N_RWKV_HEADS, HEAD_DIM, HEAD_DIM), F32)),
        grid=(DEC_BATCH // bt,),
        in_specs=[ss] + [hs] * 5 + [ts],
        out_specs=(ts, ss),
        compiler_params=_cparams(("arbitrary",)),
        name="sample_rwkv_step",
    )(state, r, w, k, a, b, v_t)


def _sample_post_kernel(y_ref, r_ref, k_ref, v_ref, g_ref, rk_ref, gng_ref, gnb_ref, o_ref):
    o_ref[...] = _rwkv_post(y_ref[...], r_ref[...], k_ref[...], v_ref[...], g_ref[...], rk_ref[...], gng_ref[...],
                            gnb_ref[...], _head_ones())


def _sample_post(y, r, k, v, g, prm):
    return pl.pallas_call(
        _sample_post_kernel,
        out_shape=jax.ShapeDtypeStruct((DEC_BATCH, D_RWKV), F32),
        compiler_params=pltpu.CompilerParams(vmem_limit_bytes=VMEM_LIMIT),
        name="sample_rwkv_post",
    )(y, r, k, v, g, prm["r_k"], prm["gn_g"], prm["gn_b"])


def _outproj_router_kernel(attn_ref, rwkv_ref, x_ref, wo_ref, g_ref, b_ref, wr_ref, br_ref, x1_ref, route_ref):
    mix = _dot(attn_ref[...].astype(BF16), wo_ref[0:D_ATTN, :]) + _dot(rwkv_ref[...].astype(BF16),
                                                                        wo_ref[D_ATTN:D_ATTN + D_RWKV, :])
    x1 = _layer_norm(ALPHA * x_ref[...] + mix, g_ref[...], b_ref[...])
    x1_ref[...] = x1
    logits = _dot1(x1, wr_ref[...]) + br_ref[...]
    tm = logits.shape[0]
    lane = lax.broadcasted_iota(jnp.int32, (tm, LANES), 1).astype(F32)
    big = float(2 * LANES)
    neg = -jnp.inf
    lc = jnp.where(lane < N_GROUPS, logits, neg)
    mc = jnp.max(lc, axis=-1, keepdims=True)
    g_sel = jnp.min(jnp.where(lc == mc, lane, big), axis=-1, keepdims=True)
    p_group = 1.0 / jnp.sum(jnp.exp(lc - mc), axis=-1, keepdims=True)
    lo = ROUTE_FINE_OFF + g_sel * EXPERTS_PER_GROUP
    lf = jnp.where((lane >= lo) & (lane < lo + EXPERTS_PER_GROUP), logits, neg)
    v1 = jnp.max(lf, axis=-1, keepdims=True)
    i1 = jnp.min(jnp.where(lf == v1, lane, big), axis=-1, keepdims=True)
    lf2 = jnp.where(lane == i1, neg, lf)
    v2 = jnp.max(lf2, axis=-1, keepdims=True)
    i2 = jnp.min(jnp.where(lf2 == v2, lane, big), axis=-1, keepdims=True)
    e21 = jnp.exp(v2 - v1)
    gate1 = p_group / (1.0 + e21)
    gate2 = p_group * e21 / (1.0 + e21)
    route = jnp.where(lane == 0, i1 - ROUTE_FINE_OFF,
                      jnp.where(lane == 1, i2 - ROUTE_FINE_OFF,
                                jnp.where(lane == 2, gate1, jnp.where(lane == 3, gate2, 0.0))))
    route_ref[...] = route


def _outproj_router(attn, rwkv, x, wo_bf16, ln_g, ln_b, w_route, b_route, tm, name):
    m = x.shape[0]
    const = lambda shape: pl.BlockSpec(shape, lambda i: (0, 0))
    return pl.pallas_call(
        _outproj_router_kernel,
        out_shape=(jax.ShapeDtypeStruct((m, D_MODEL), F32), jax.ShapeDtypeStruct((m, LANES), F32)),
        grid=(m // tm,),
        in_specs=[pl.BlockSpec((tm, D_ATTN), lambda i: (i, 0)),
                  pl.BlockSpec((tm, D_RWKV), lambda i: (i, 0)),
                  pl.BlockSpec((tm, D_MODEL), lambda i: (i, 0)),
                  const((D_MODEL, D_MODEL)), const((1, D_MODEL)), const((1, D_MODEL)),
                  const((D_MODEL, LANES)), const((1, LANES))],
        out_specs=(pl.BlockSpec((tm, D_MODEL), lambda i: (i, 0)), pl.BlockSpec((tm, LANES), lambda i: (i, 0))),
        compiler_params=_cparams(("arbitrary",)),
        name=name,
    )(attn, rwkv, x, wo_bf16, ln_g, ln_b, w_route, b_route)


def _expert_kernel(be_ref, nb_ref, rows_ref, x_hbm, wg_ref, wu_ref, wd_ref, o_ref, xbuf, sem):
    blk = pl.program_id(0)

    def row_copy(slot):
        return pltpu.make_async_copy(x_hbm.at[pl.ds(rows_ref[0, 0, slot], 1)], xbuf.at[pl.ds(slot, 1)], sem)

    @pl.when(blk < nb_ref[0])
    def _():
        def issue(slot, carry):
            row_copy(slot).start()
            return carry

        def drain(slot, carry):
            row_copy(slot).wait()
            return carry

        lax.fori_loop(0, MOE_BLOCK, issue, 0)
        lax.fori_loop(0, MOE_BLOCK, drain, 0)
        xb = xbuf[...].astype(BF16)
        gate = _dot(xb, wg_ref[0].astype(BF16))
        up = _dot(xb, wu_ref[0].astype(BF16))
        h = gate * _sigmoid(gate) * up
        o_ref[...] = _dot(h.astype(BF16), wd_ref[0].astype(BF16))

    @pl.when(blk >= nb_ref[0])
    def _():
        o_ref[...] = jnp.zeros_like(o_ref)


def _expert_mlp(block_expert, n_used, rows, x_all, w_gate, w_up, w_down, n_blocks):
    grid_spec = pltpu.PrefetchScalarGridSpec(
        num_scalar_prefetch=2,
        grid=(n_blocks,),
        in_specs=[pl.BlockSpec((1, 1, MOE_BLOCK), lambda b, be, nb: (b, 0, 0), memory_space=pltpu.SMEM),
                  pl.BlockSpec(memory_space=pl.ANY),
                  pl.BlockSpec((1, D_MODEL, D_EXPERT), lambda b, be, nb: (be[b], 0, 0)),
                  pl.BlockSpec((1, D_MODEL, D_EXPERT), lambda b, be, nb: (be[b], 0, 0)),
                  pl.BlockSpec((1, D_EXPERT, D_MODEL), lambda b, be, nb: (be[b], 0, 0))],
        out_specs=pl.BlockSpec((MOE_BLOCK, D_MODEL), lambda b, be, nb: (b, 0)),
        scratch_shapes=[pltpu.VMEM((MOE_BLOCK, D_MODEL), F32), pltpu.SemaphoreType.DMA],
    )
    return pl.pallas_call(
        _expert_kernel,
        out_shape=jax.ShapeDtypeStruct((n_blocks * MOE_BLOCK, D_MODEL), F32),
        grid_spec=grid_spec,
        compiler_params=_cparams(("arbitrary",)),
        name="expert_mlp",
    )(block_expert, n_used, rows, x_all, w_gate, w_up, w_down)


COMBINE_TILE = 128


def _combine_kernel(dest_ref, y_hbm, x1_ref, route_ref, g_ref, b_ref, o_ref, ybuf, sem):
    def row_copy(slot):
        return pltpu.make_async_copy(y_hbm.at[pl.ds(dest_ref[0, 0, slot], 1)], ybuf.at[pl.ds(slot, 1)], sem)

    def issue(slot, carry):
        row_copy(slot).start()
        return carry

    def drain(slot, carry):
        row_copy(slot).wait()
        return carry

    lax.fori_loop(0, 2 * COMBINE_TILE, issue, 0)
    lax.fori_loop(0, 2 * COMBINE_TILE, drain, 0)
    route = route_ref[...]
    moe = route[:, 2:3] * ybuf[0:COMBINE_TILE, :] + route[:, 3:4] * ybuf[COMBINE_TILE:2 * COMBINE_TILE, :]
    o_ref[...] = _layer_norm(ALPHA * x1_ref[...] + moe, g_ref[...], b_ref[...])


def _combine(dest, y_slots, x1, route, ln_g, ln_b, name):
    m = x1.shape[0]
    tm = COMBINE_TILE
    return pl.pallas_call(
        _combine_kernel,
        out_shape=jax.ShapeDtypeStruct((m, D_MODEL), F32),
        grid=(m // tm,),
        in_specs=[pl.BlockSpec((1, 1, 2 * tm), lambda i: (i, 0, 0), memory_space=pltpu.SMEM),
                  pl.BlockSpec(memory_space=pl.ANY),
                  pl.BlockSpec((tm, D_MODEL), lambda i: (i, 0)),
                  pl.BlockSpec((tm, LANES), lambda i: (i, 0)),
                  pl.BlockSpec((1, D_MODEL), lambda i: (0, 0)),
                  pl.BlockSpec((1, D_MODEL), lambda i: (0, 0))],
        out_specs=pl.BlockSpec((tm, D_MODEL), lambda i: (i, 0)),
        scratch_shapes=[pltpu.VMEM((2 * tm, D_MODEL), F32), pltpu.SemaphoreType.DMA],
        compiler_params=_cparams(("arbitrary",)),
        name=name,
    )(dest, y_slots, x1, route, ln_g, ln_b)


def _feat_layout(t):
    pad = lambda a, n: jnp.pad(a, [(0, 0)] * (a.ndim - 1) + [(0, n - a.shape[-1])])
    o = 3 * D_RWKV
    return jnp.concatenate([t[..., :o], pad(t[..., o:o + W_LORA], XW_PAD),
                            pad(t[..., o + W_LORA:o + W_LORA + A_LORA], XA_PAD),
                            pad(t[..., o + W_LORA + A_LORA:], XG_PAD)], axis=-1)


def _feat_unlayout(t):
    return jnp.concatenate([t[..., :OFF_XW + W_LORA], t[..., OFF_XA:OFF_XA + A_LORA],
                            t[..., OFF_XG:OFF_XG + G_LORA]], axis=-1)


def _pad_rows(a, n):
    return jnp.pad(a, ((0, n - a.shape[0]), (0, 0)))


def _dispatch_plan(route_all, n_tokens, n_blocks):
    flat_e = route_all[:, 0:2].astype(jnp.int32).reshape(-1)
    n_assign = flat_e.shape[0]
    onehot = (flat_e[:, None] == jnp.arange(N_EXPERTS, dtype=jnp.int32)[None, :]).astype(jnp.int32)
    csum = jnp.cumsum(onehot, axis=0)
    rank = jnp.sum(onehot * csum, axis=1) - 1
    counts = csum[-1]
    padded = (counts + MOE_BLOCK - 1) // MOE_BLOCK * MOE_BLOCK
    pend = jnp.cumsum(padded)
    pstart = pend - padded
    dest = (pstart[flat_e] + rank).astype(jnp.int32)
    token = (jnp.arange(n_assign, dtype=jnp.int32) // 2)
    rows = jnp.full((n_blocks * MOE_BLOCK,), n_tokens, jnp.int32).at[dest].set(token)
    block_e = jnp.minimum(jnp.searchsorted(pend, jnp.arange(n_blocks, dtype=jnp.int32) * MOE_BLOCK, side="right"),
                          N_EXPERTS - 1).astype(jnp.int32)
    n_used = (pend[-1] // MOE_BLOCK).astype(jnp.int32).reshape(1)
    return dest, rows, block_e, n_used


def kernel(x_prompt, x_sample, cache_k_win, cache_v_win, state_wkv, state_shift, w_in, attn_sinks, shift_mu, w0,
           w_decay_up, a0, w_a_up, w_g_up, k_k, k_a, r_k, gn_g, gn_b, w_out, ln1_g, ln1_b, w_coarse, b_coarse,
           w_fine, b_fine, w_exp_gate, w_exp_up, w_exp_down, ln2_g, ln2_b):
    xp = x_prompt[0]
    xs = x_sample[:, 0]
    row = lambda a: a.reshape(1, -1)

    w_in0 = w_in[0]
    w_feat = _feat_layout(w_in0[:, D_QKV:])
    prm = dict(mu=row(_feat_layout(shift_mu[0])), w0=row(w0[0]), a0=row(a0[0]), k_k=row(k_k[0]), k_a=row(k_a[0]),
               r_k=row(r_k[0]), gn_g=row(gn_g[0]), gn_b=row(gn_b[0]),
               wd=_pad_rows(w_decay_up[0], XW_PAD), wa=_pad_rows(w_a_up[0], XA_PAD), wg=_pad_rows(w_g_up[0], XG_PAD))
    sinks = attn_sinks[0]
    wo_bf16 = w_out[0].astype(BF16)
    w_route = jnp.pad(jnp.concatenate([w_coarse[0], w_fine[0]], axis=1), ((0, 0), (0, LANES - N_GROUPS - N_EXPERTS)))
    b_route = jnp.pad(jnp.concatenate([b_coarse[0], b_fine[0]]), (0, LANES - N_GROUPS - N_EXPERTS)).reshape(1, LANES)

    hp_attn = _matmul(xp, w_in0, D_QKV, 1024, 768, "in_proj_prompt_attn")
    hp_feat = _matmul(xp, w_feat, D_FEAT, 1024, 896, "in_proj_prompt_feat")
    hs_attn = _matmul(xs, w_in0, D_QKV, DEC_BATCH, 768, "in_proj_sample_attn")
    hs_feat = _matmul(xs, w_feat, D_FEAT, DEC_BATCH, 896, "in_proj_sample_feat")

    attn_p = _prompt_attention(hp_attn, sinks)
    rwkv_p, state_p = _prompt_rwkv(hp_feat, prm)

    q_s = hs_attn[:, :D_ATTN].reshape(DEC_BATCH, N_Q_HEADS, HEAD_DIM)
    k_s = hs_attn[:, D_ATTN:D_ATTN + D_KV].reshape(DEC_BATCH, 1, D_KV)
    v_s = hs_attn[:, D_ATTN + D_KV:].reshape(DEC_BATCH, 1, D_KV)
    attn_s, kwin_s, vwin_s = _sample_attention(
        q_s, k_s, v_s, cache_k_win[0].reshape(DEC_BATCH, WINDOW, D_KV),
        cache_v_win[0].reshape(DEC_BATCH, WINDOW, D_KV), sinks.reshape(N_Q_HEADS, 1))
    r_s, w_s, k2_s, vv_s, a_s, b_s, g_s = _sample_prep(hs_feat, _feat_layout(state_shift[0]), prm)
    heads = lambda a: a.reshape(DEC_BATCH, N_RWKV_HEADS, HEAD_DIM)
    y_t, state_s = _sample_step(state_wkv[0], heads(r_s), heads(w_s), heads(k2_s), heads(a_s), heads(b_s),
                                heads(vv_s).transpose(0, 2, 1))
    rwkv_s = _sample_post(y_t.transpose(0, 2, 1).reshape(DEC_BATCH, D_RWKV), r_s, k2_s, vv_s, g_s, prm)

    x1_p, route_p = _outproj_router(attn_p, rwkv_p, xp, wo_bf16, row(ln1_g[0]), row(ln1_b[0]), w_route, b_route,
                                    256, "outproj_router_prompt")
    x1_s, route_s = _outproj_router(attn_s.reshape(DEC_BATCH, D_ATTN), rwkv_s, xs,
                                    wo_bf16, row(ln1_g[0]), row(ln1_b[0]), w_route, b_route,
                                    DEC_BATCH, "outproj_router_sample")

    n_tokens = SEQ + DEC_BATCH
    n_assign = 2 * n_tokens
    n_blocks = -(-(n_assign + N_EXPERTS * (MOE_BLOCK - 1)) // MOE_BLOCK)
    dest, rows, block_e, n_used = _dispatch_plan(jnp.concatenate([route_p, route_s], axis=0), n_tokens, n_blocks)
    x_all = jnp.concatenate([x1_p, x1_s, jnp.zeros((SUBLANES, D_MODEL), F32)], axis=0)
    y_slots = _expert_mlp(block_e, n_used, rows.reshape(n_blocks, 1, MOE_BLOCK), x_all,
                          w_exp_gate[0], w_exp_up[0], w_exp_down[0], n_blocks)

    def dest_tiles(d):
        d = d.reshape(-1, COMBINE_TILE, 2)
        return jnp.concatenate([d[:, :, 0], d[:, :, 1]], axis=1).reshape(-1, 1, 2 * COMBINE_TILE)

    y_p = _combine(dest_tiles(dest[:2 * SEQ]), y_slots, x1_p, route_p, row(ln2_g[0]), row(ln2_b[0]), "combine_prompt")
    y_s = _combine(dest_tiles(dest[2 * SEQ:]), y_slots, x1_s, route_s, row(ln2_g[0]), row(ln2_b[0]), "combine_sample")

    kv4 = lambda a: a.reshape(a.shape[0], N_KV_HEADS, HEAD_DIM)
    k_win_p = kv4(hp_attn[SEQ - WINDOW:, D_ATTN:D_ATTN + D_KV])[None, None]
    v_win_p = kv4(hp_attn[SEQ - WINDOW:, D_ATTN + D_KV:])[None, None]
    sp = state_p.reshape(N_PAIRS, HEADS_PER_TILE, HEAD_DIM, HEADS_PER_TILE, HEAD_DIM)
    wkv_p = jnp.stack([sp[:, i, :, i, :] for i in range(HEADS_PER_TILE)], axis=1)
    wkv_p = wkv_p.reshape(N_RWKV_HEADS, HEAD_DIM, HEAD_DIM).transpose(0, 2, 1)[None, None]
    shift_p = _feat_unlayout(hp_feat[SEQ - 1:SEQ])[None]
    shift_s = _feat_unlayout(hs_feat)[None]
    return (y_p[None], y_s[:, None, :], k_win_p, v_win_p, wkv_p, shift_p,
            kwin_s.reshape(1, DEC_BATCH, WINDOW, N_KV_HEADS, HEAD_DIM),
            vwin_s.reshape(1, DEC_BATCH, WINDOW, N_KV_HEADS, HEAD_DIM),
            state_s[None], shift_s)
```

```python
import functools
import math

import jax
import jax.numpy as jnp
from jax import lax
from jax.experimental import pallas as pl
from jax.experimental.pallas import tpu as pltpu

F32 = jnp.float32
BF16 = jnp.bfloat16

D_MODEL = 2048
SEQ = 8192
DEC_BATCH = 128
HEAD_DIM = 64
D_ATTN = 1024
D_RWKV = 1024
N_Q_HEADS = 16
N_KV_HEADS = 4
Q_PER_KV = 4
D_KV = 256
WINDOW = 128
ATTN_SCALE = HEAD_DIM ** -0.5
N_RWKV_HEADS = 16
W_LORA = 64
A_LORA = 64
G_LORA = 160
D_SHIFT = 3 * D_RWKV + W_LORA + A_LORA + G_LORA
D_QKV = D_ATTN + 2 * D_KV
N_GROUPS = 4
EXPERTS_PER_GROUP = 8
N_EXPERTS = 32
D_EXPERT = 512
ALPHA = 2.0 ** 0.25
LN_EPS = 1e-5
GN_EPS = 64e-5

SUBLANES = 8
LANES = 128
VMEM_LIMIT = 52 * 1024 * 1024

XW_PAD = LANES
XA_PAD = LANES
XG_PAD = 2 * LANES
OFF_XW = 3 * D_RWKV
OFF_XA = OFF_XW + XW_PAD
OFF_XG = OFF_XA + XA_PAD
D_FEAT = OFF_XG + XG_PAD

CHUNK = 64
HEADS_PER_TILE = LANES // HEAD_DIM
N_PAIRS = N_RWKV_HEADS // HEADS_PER_TILE
SOLVE_LEVELS = int(math.log2(CHUNK))

MOE_BLOCK = 256
ROUTE_FINE_OFF = N_GROUPS

NN = (((1,), (0,)), ((), ()))
NT = (((1,), (1,)), ((), ()))


def _dot(a, b, dims=NN):
    return lax.dot_general(a, b, dims, preferred_element_type=F32)


def _dot1(a, b, dims=NN):
    return _dot(a.astype(BF16), b.astype(BF16), dims)


def _split(x):
    hi = x.astype(BF16)
    lo = (x - hi.astype(F32)).astype(BF16)
    return hi, lo


def _dot3(a, b, dims=NN):
    ah, al = _split(a)
    bh, bl = _split(b)
    return _dot(ah, bh, dims) + (_dot(ah, bl, dims) + _dot(al, bh, dims))


def _dot_exact_lhs(a_bf16, b, dims=NN):
    bh, bl = _split(b)
    return _dot(a_bf16, bh, dims) + _dot(a_bf16, bl, dims)


def _dot_exact_rhs(a, b_bf16, dims=NN):
    ah, al = _split(a)
    return _dot(ah, b_bf16, dims) + _dot(al, b_bf16, dims)


def _div_pow2(x, d):
    return lax.shift_right_logical(x, jnp.int32(int(math.log2(d))))


def _mod_pow2(x, d):
    return lax.bitwise_and(x, jnp.int32(d - 1))


def _sigmoid(x):
    return 1.0 / (1.0 + jnp.exp(-x))


def _softplus(x):
    return jnp.maximum(x, 0.0) + jnp.log(1.0 + jnp.exp(-jnp.abs(x)))


def _layer_norm(z, g, b):
    mu = jnp.mean(z, axis=-1, keepdims=True)
    d = z - mu
    var = jnp.mean(d * d, axis=-1, keepdims=True)
    return d * lax.rsqrt(var + LN_EPS) * g + b


def _cparams(sem):
    return pltpu.CompilerParams(dimension_semantics=sem, vmem_limit_bytes=VMEM_LIMIT)


def _matmul_kernel(x_ref, w_ref, o_ref):
    o_ref[...] = _dot(x_ref[...].astype(BF16), w_ref[...].astype(BF16))


def _matmul(x, w, n_out, tm, tn, name):
    m, k = x.shape
    tm = min(tm, m)
    return pl.pallas_call(
        _matmul_kernel,
        out_shape=jax.ShapeDtypeStruct((m, n_out), F32),
        grid=(n_out // tn, m // tm),
        in_specs=[pl.BlockSpec((tm, k), lambda j, i: (i, 0)),
                  pl.BlockSpec((k, tn), lambda j, i: (0, j))],
        out_specs=pl.BlockSpec((tm, tn), lambda j, i: (i, j)),
        compiler_params=_cparams(("arbitrary", "arbitrary")),
        name=name,
    )(x, w)


def _prompt_attn_kernel(q_ref, kvp_ref, kvc_ref, sink_ref, o_ref):
    blk = pl.program_id(0)
    q = q_ref[...]
    kv_prev = kvp_ref[...]
    kv_cur = kvc_ref[...]
    qi = _mod_pow2(lax.broadcasted_iota(jnp.int32, (Q_PER_KV * WINDOW, 2 * WINDOW), 0), WINDOW)
    kj = lax.broadcasted_iota(jnp.int32, (Q_PER_KV * WINDOW, 2 * WINDOW), 1)
    diff = qi + WINDOW - kj
    mask = (diff >= 0) & (diff <= WINDOW) & ((blk > 0) | (kj >= WINDOW))
    row_head = _div_pow2(lax.broadcasted_iota(jnp.int32, (Q_PER_KV * WINDOW, 1), 0), WINDOW)
    outs = []
    for g in range(N_KV_HEADS):
        kc = jnp.concatenate([kv_prev[:, g * HEAD_DIM:(g + 1) * HEAD_DIM],
                              kv_cur[:, g * HEAD_DIM:(g + 1) * HEAD_DIM]], axis=0).astype(BF16)
        vc = jnp.concatenate([kv_prev[:, D_KV + g * HEAD_DIM:D_KV + (g + 1) * HEAD_DIM],
                              kv_cur[:, D_KV + g * HEAD_DIM:D_KV + (g + 1) * HEAD_DIM]], axis=0).astype(BF16)
        qs = jnp.concatenate(
            [q[:, (g * Q_PER_KV + h) * HEAD_DIM:(g * Q_PER_KV + h + 1) * HEAD_DIM] for h in range(Q_PER_KV)],
            axis=0).astype(BF16)
        s = _dot(qs, kc, NT) * ATTN_SCALE
        s = jnp.where(mask, s, -jnp.inf)
        sink = jnp.zeros((Q_PER_KV * WINDOW, 1), F32)
        for h in range(Q_PER_KV):
            sink = jnp.where(row_head == h, sink_ref[g * Q_PER_KV + h], sink)
        m = jnp.maximum(jnp.max(s, axis=-1, keepdims=True), sink)
        p = jnp.exp(s - m)
        denom = jnp.sum(p, axis=-1, keepdims=True) + jnp.exp(sink - m)
        p = p / denom
        o = _dot(p.astype(BF16), vc)
        for h in range(Q_PER_KV):
            outs.append(o[h * WINDOW:(h + 1) * WINDOW, :])
    o_ref[...] = jnp.concatenate(outs, axis=1)


def _prompt_attention(h_attn, sinks):
    nb = SEQ // WINDOW
    return pl.pallas_call(
        _prompt_attn_kernel,
        out_shape=jax.ShapeDtypeStruct((SEQ, D_ATTN), F32),
        grid=(nb,),
        in_specs=[pl.BlockSpec((WINDOW, D_ATTN), lambda i: (i, 0)),
                  pl.BlockSpec((WINDOW, 2 * D_KV), lambda i: (jnp.maximum(i - 1, 0), 2)),
                  pl.BlockSpec((WINDOW, 2 * D_KV), lambda i: (i, 2)),
                  pl.BlockSpec(memory_space=pltpu.SMEM)],
        out_specs=pl.BlockSpec((WINDOW, D_ATTN), lambda i: (i, 0)),
        compiler_params=_cparams(("arbitrary",)),
        name="prompt_attention",
    )(h_attn, h_attn, h_attn, sinks)


SAMPLE_ATTN_TILE = 8


def _sample_attn_kernel(q_ref, knew_ref, vnew_ref, ck_ref, cv_ref, sink_ref, o_ref, kwin_ref, vwin_ref):
    lane = lax.broadcasted_iota(jnp.int32, (N_Q_HEADS, D_KV), 1)
    head = lax.broadcasted_iota(jnp.int32, (N_Q_HEADS, D_KV), 0)
    group_mask = _div_pow2(lane, HEAD_DIM) == _div_pow2(head, Q_PER_KV)
    sink = sink_ref[...]
    row = lax.broadcasted_iota(jnp.int32, (WINDOW, D_KV), 0)
    for b in range(SAMPLE_ATTN_TILE):
        q = q_ref[b]
        qbd = jnp.where(group_mask, jnp.concatenate([q] * N_KV_HEADS, axis=1), 0.0).astype(BF16)
        kb = ck_ref[b]
        vb = cv_ref[b]
        kn = knew_ref[b]
        vn = vnew_ref[b]
        s = _dot1(qbd, kb, NT) * ATTN_SCALE
        s_new = jnp.sum(qbd.astype(F32) * kn.astype(BF16).astype(F32), axis=-1, keepdims=True) * ATTN_SCALE
        m = jnp.maximum(jnp.maximum(jnp.max(s, axis=-1, keepdims=True), s_new), sink)
        p = jnp.exp(s - m)
        p_new = jnp.exp(s_new - m)
        denom = jnp.sum(p, axis=-1, keepdims=True) + p_new + jnp.exp(sink - m)
        p = p / denom
        p_new = (p_new / denom).astype(BF16).astype(F32)
        o_full = _dot1(p, vb) + p_new * vn.astype(BF16).astype(F32)
        o_full = jnp.where(group_mask, o_full, 0.0)
        o = o_full[:, 0:HEAD_DIM]
        for g in range(1, N_KV_HEADS):
            o = o + o_full[:, g * HEAD_DIM:(g + 1) * HEAD_DIM]
        o_ref[b] = o
        kwin_ref[b] = jnp.where(row == WINDOW - 1, kn, pltpu.roll(kb, WINDOW - 1, axis=0))
        vwin_ref[b] = jnp.where(row == WINDOW - 1, vn, pltpu.roll(vb, WINDOW - 1, axis=0))


def _sample_attention(q, k_new, v_new, cache_k, cache_v, sinks):
    bt = SAMPLE_ATTN_TILE
    win_spec = pl.BlockSpec((bt, WINDOW, D_KV), lambda i: (i, 0, 0))
    new_spec = pl.BlockSpec((bt, 1, D_KV), lambda i: (i, 0, 0))
    return pl.pallas_call(
        _sample_attn_kernel,
        out_shape=(jax.ShapeDtypeStruct((DEC_BATCH, N_Q_HEADS, HEAD_DIM), F32),
                   jax.ShapeDtypeStruct((DEC_BATCH, WINDOW, D_KV), F32),
                   jax.ShapeDtypeStruct((DEC_BATCH, WINDOW, D_KV), F32)),
        grid=(DEC_BATCH // bt,),
        in_specs=[pl.BlockSpec((bt, N_Q_HEADS, HEAD_DIM), lambda i: (i, 0, 0)),
                  new_spec, new_spec, win_spec, win_spec,
                  pl.BlockSpec((N_Q_HEADS, 1), lambda i: (0, 0))],
        out_specs=(pl.BlockSpec((bt, N_Q_HEADS, HEAD_DIM), lambda i: (i, 0, 0)), win_spec, win_spec),
        compiler_params=_cparams(("arbitrary",)),
        name="sample_attention",
    )(q, k_new, v_new, cache_k, cache_v, sinks)


def _head_ones():
    r = _div_pow2(lax.broadcasted_iota(jnp.int32, (LANES, LANES), 0), HEAD_DIM)
    c = _div_pow2(lax.broadcasted_iota(jnp.int32, (LANES, LANES), 1), HEAD_DIM)
    return jnp.where(r == c, 1.0, 0.0).astype(BF16)


def _head_sum(x, ones):
    parts = [_dot_exact_rhs(x[:, p * LANES:(p + 1) * LANES], ones) for p in range(x.shape[1] // LANES)]
    return jnp.concatenate(parts, axis=1)


def _rwkv_prep(mixed, w0, a0, k_k, k_a, wd, wa, wg, ones):
    r = mixed[:, 0:D_RWKV]
    k = mixed[:, D_RWKV:2 * D_RWKV]
    v = mixed[:, 2 * D_RWKV:3 * D_RWKV]
    xw = mixed[:, OFF_XW:OFF_XW + XW_PAD]
    xa = mixed[:, OFF_XA:OFF_XA + XA_PAD]
    xg = mixed[:, OFF_XG:OFF_XG + XG_PAD]
    w_log = -_softplus(-(w0 + _dot1(jnp.tanh(xw), wd))) - 0.5
    log_decay = -jnp.exp(w_log)
    a = _sigmoid(a0 + _dot1(xa, wa))
    g = _dot1(_sigmoid(xg), wg)
    kk = k * k_k
    norm = jnp.sqrt(_head_sum(kk * kk, ones))
    kk = kk / jnp.maximum(norm, 1e-12)
    k2 = k * (1.0 + (a - 1.0) * k_a)
    return r, log_decay, k2, v, -kk, kk * a, g


def _rwkv_post(y, r, k2, v, g, r_k, gn_g, gn_b, ones):
    inv_n = 1.0 / HEAD_DIM
    mu = _head_sum(y, ones) * inv_n
    d = y - mu
    var = _head_sum(d * d, ones) * inv_n
    yn = d * lax.rsqrt(var + GN_EPS) * gn_g + gn_b
    bonus = _head_sum(r * k2 * r_k, ones) * v
    return (yn + bonus) * g


(OP_AABS, OP_RABS, OP_AN, OP_RN, OP_BN, OP_KN, OP_BH, OP_KH, OP_V) = range(9)
N_OPS = 9


def _prompt_rwkv_kernel(feat_ref, mu_ref, w0_ref, a0_ref, kk_ref, ka_ref, rk_ref, gng_ref, gnb_ref,
                        wd_ref, wa_ref, wg_ref, out_ref, state_ref,
                        prev_ref, s_ref, ops_ref, pc_ref, y_ref):
    c = pl.program_id(0)
    C = CHUNK

    @pl.when(c == 0)
    def _():
        prev_ref[...] = jnp.zeros_like(prev_ref)
        s_ref[...] = jnp.zeros_like(s_ref)

    ones = _head_ones()
    feat = feat_ref[...]
    row = lax.broadcasted_iota(jnp.int32, (C, 1), 0)
    shifted = jnp.where(row == 0, prev_ref[0:1, :], pltpu.roll(feat, 1, axis=0))
    prev_ref[0:1, :] = feat[C - 1:C, :]
    mixed = feat + (shifted - feat) * mu_ref[...]
    r, ld, k2, v, av, bv, g = _rwkv_prep(mixed, w0_ref[...], a0_ref[...], kk_ref[...], ka_ref[...],
                                         wd_ref[...], wa_ref[...], wg_ref[...], ones)

    ti = lax.broadcasted_iota(jnp.int32, (C, C), 0)
    tj = lax.broadcasted_iota(jnp.int32, (C, C), 1)
    tri_incl = jnp.where(tj <= ti, 1.0, 0.0).astype(BF16)
    cs = _dot_exact_lhs(tri_incl, ld)
    cs_ref = cs[C // 2 - 1:C // 2, :]
    cs_end = cs[C - 1:C, :]
    e_prev = jnp.exp(cs - ld)
    e_cur = jnp.exp(cs)
    n_prev = jnp.exp(cs - ld - cs_ref)
    n_cur = jnp.exp(cs - cs_ref)
    n_inv = jnp.exp(cs_ref - cs)
    e_tail = jnp.exp(cs_end - cs)
    ops = {OP_AABS: av * e_prev, OP_RABS: r * e_cur, OP_AN: av * n_prev, OP_RN: r * n_cur,
           OP_BN: bv * n_inv, OP_KN: k2 * n_inv, OP_BH: bv * e_tail, OP_KH: k2 * e_tail, OP_V: v}
    p_end = jnp.exp(cs_end)
    for p in range(N_PAIRS):
        sl = slice(p * LANES, (p + 1) * LANES)
        for idx, val in ops.items():
            ops_ref[p, idx] = val[:, sl]
        pc_ref[p] = jnp.broadcast_to(p_end[:, sl], (SUBLANES, LANES))

    lane1 = lax.broadcasted_iota(jnp.int32, (C, LANES), 1)
    head0 = lane1 < HEAD_DIM
    r2 = lax.broadcasted_iota(jnp.int32, (2 * C, 2 * C), 0)
    c2 = lax.broadcasted_iota(jnp.int32, (2 * C, 2 * C), 1)
    tq = _mod_pow2(r2, C)
    tk = _mod_pow2(c2, C)
    band = (tk < tq) | ((tk == tq) & (r2 >= C))
    blockdiag = _div_pow2(r2, HEAD_DIM) == _div_pow2(c2, HEAD_DIM)

    pairs = range(N_PAIRS)
    op = lambda p, idx: ops_ref[p, idx]

    zero_half = jnp.zeros((C, LANES), F32)

    gy = [_dot1(jnp.concatenate([op(p, OP_AABS), op(p, OP_RABS)], axis=0), s_ref[p]) for p in pairs]

    am0, am1 = [], []
    for p in pairs:
        a_n, r_n = op(p, OP_AN), op(p, OP_RN)
        b0, k0 = jnp.where(head0, op(p, OP_BN), 0.0), jnp.where(head0, op(p, OP_KN), 0.0)
        b1, k1 = jnp.where(head0, 0.0, op(p, OP_BN)), jnp.where(head0, 0.0, op(p, OP_KN))
        am = _dot1(jnp.concatenate([a_n, r_n], axis=0), jnp.concatenate([k0, b0, b1, k1], axis=0), NT)
        am0.append(jnp.where(band, am[:, 0:2 * C], 0.0))
        am1.append(jnp.where(band, am[:, 2 * C:4 * C], 0.0))

    w0, w1 = [], []
    for p in pairs:
        top0, top1 = am0[p][0:C], am1[p][0:C]
        ak = jnp.concatenate([jnp.where(head0, top0, 0.0), jnp.where(head0, 0.0, top1)], axis=0)
        vv = op(p, OP_V)
        g0 = gy[p][0:C]
        m = jnp.concatenate([g0, g0], axis=0) + _dot1(ak, jnp.concatenate([vv, vv], axis=0))
        w0.append(jnp.where(head0, m[0:C], top0))
        w1.append(jnp.where(head0, top1, m[C:2 * C]))

    for lvl in range(SOLVE_LEVELS):
        prod0 = [_dot3(w0[p], jnp.concatenate([zero_half, w0[p]], axis=0)) for p in pairs]
        prod1 = [_dot3(w1[p], jnp.concatenate([w1[p], zero_half], axis=0)) for p in pairs]
        w0 = [jnp.where(head0, w0[p] + prod0[p], prod0[p]) for p in pairs]
        w1 = [jnp.where(head0, prod1[p], w1[p] + prod1[p]) for p in pairs]
    u = [jnp.where(head0, w0[p], w1[p]) for p in pairs]

    for p in pairs:
        vv = op(p, OP_V)
        y_lhs = jnp.concatenate([am0[p][C:2 * C], am1[p][C:2 * C]], axis=1)
        y_rhs = jnp.concatenate([jnp.where(head0, vv, 0.0), jnp.where(head0, u[p], 0.0),
                                 jnp.where(head0, 0.0, u[p]), jnp.where(head0, 0.0, vv)], axis=0)
        y_ref[p] = gy[p][C:2 * C] + _dot1(y_lhs, y_rhs)

    for p in pairs:
        decay_rows = jnp.broadcast_to(pc_ref[p][0:1, :], (LANES, LANES)).T
        upd_lhs = jnp.concatenate([op(p, OP_BH), op(p, OP_KH)], axis=0).T
        upd_rhs = jnp.concatenate([u[p], op(p, OP_V)], axis=0)
        s_ref[p] = s_ref[p] * decay_rows + jnp.where(blockdiag, _dot1(upd_lhs, upd_rhs), 0.0)

    y = jnp.concatenate([y_ref[p] for p in range(N_PAIRS)], axis=1)
    out_ref[...] = _rwkv_post(y, r, k2, v, g, rk_ref[...], gng_ref[...], gnb_ref[...], ones)

    @pl.when(c == pl.num_programs(0) - 1)
    def _():
        state_ref[...] = s_ref[...]


def _prompt_rwkv(feat, prm):
    n_chunks = SEQ // CHUNK
    vec = pl.BlockSpec((1, D_RWKV), lambda c: (0, 0))
    full = lambda a: pl.BlockSpec(a.shape, lambda c: (0,) * a.ndim)
    return pl.pallas_call(
        _prompt_rwkv_kernel,
        out_shape=(jax.ShapeDtypeStruct((SEQ, D_RWKV), F32),
                   jax.ShapeDtypeStruct((N_PAIRS, LANES, LANES), F32)),
        grid=(n_chunks,),
        in_specs=[pl.BlockSpec((CHUNK, D_FEAT), lambda c: (c, 0)),
                  pl.BlockSpec((1, D_FEAT), lambda c: (0, 0)),
                  vec, vec, vec, vec, vec, vec, vec,
                  full(prm["wd"]), full(prm["wa"]), full(prm["wg"])],
        out_specs=(pl.BlockSpec((CHUNK, D_RWKV), lambda c: (c, 0)),
                   pl.BlockSpec((N_PAIRS, LANES, LANES), lambda c: (0, 0, 0))),
        scratch_shapes=[pltpu.VMEM((SUBLANES, D_FEAT), F32),
                        pltpu.VMEM((N_PAIRS, LANES, LANES), F32),
                        pltpu.VMEM((N_PAIRS, N_OPS, CHUNK, LANES), F32),
                        pltpu.VMEM((N_PAIRS, SUBLANES, LANES), F32),
                        pltpu.VMEM((N_PAIRS, CHUNK, LANES), F32)],
        compiler_params=_cparams(("arbitrary",)),
        name="prompt_rwkv",
    )(feat, prm["mu"], prm["w0"], prm["a0"], prm["k_k"], prm["k_a"], prm["r_k"], prm["gn_g"], prm["gn_b"],
      prm["wd"], prm["wa"], prm["wg"])


def _sample_prep_kernel(feat_ref, shift_ref, mu_ref, w0_ref, a0_ref, kk_ref, ka_ref, wd_ref, wa_ref, wg_ref,
                        r_ref, w_ref, k_ref, v_ref, a_ref, b_ref, g_ref):
    ones = _head_ones()
    feat = feat_ref[...]
    mixed = feat + (shift_ref[...] - feat) * mu_ref[...]
    r, ld, k2, v, av, bv, g = _rwkv_prep(mixed, w0_ref[...], a0_ref[...], kk_ref[...], ka_ref[...],
                                         wd_ref[...], wa_ref[...], wg_ref[...], ones)
    r_ref[...] = r
    w_ref[...] = jnp.exp(ld)
    k_ref[...] = k2
    v_ref[...] = v
    a_ref[...] = av
    b_ref[...] = bv
    g_ref[...] = g


def _sample_prep(feat, shift, prm):
    out = jax.ShapeDtypeStruct((DEC_BATCH, D_RWKV), F32)
    return pl.pallas_call(
        _sample_prep_kernel,
        out_shape=(out,) * 7,
        compiler_params=pltpu.CompilerParams(vmem_limit_bytes=VMEM_LIMIT),
        name="sample_rwkv_prep",
    )(feat, shift, prm["mu"], prm["w0"], prm["a0"], prm["k_k"], prm["k_a"], prm["wd"], prm["wa"], prm["wg"])


SAMPLE_STEP_TILE = 8


def _sample_step_kernel(s_ref, r_ref, w_ref, k_ref, a_ref, b_ref, vt_ref, yt_ref, snew_ref):
    head_lane = lax.broadcasted_iota(jnp.int32, (HEAD_DIM, N_RWKV_HEADS), 1)

    def seq_body(bi, carry):
        r = r_ref[bi]
        w = w_ref[bi]
        k = k_ref[bi]
        a = a_ref[bi]
        b = b_ref[bi]
        v_t = vt_ref[bi]
        y_t = jnp.zeros((HEAD_DIM, N_RWKV_HEADS), F32)
        for h in range(N_RWKV_HEADS):
            s = s_ref[bi, h]
            sa = jnp.sum(s * a[h:h + 1, :], axis=-1, keepdims=True)
            s_new = s * w[h:h + 1, :] + sa * b[h:h + 1, :] + v_t[:, h:h + 1] * k[h:h + 1, :]
            snew_ref[bi, h] = s_new
            y_t = jnp.where(head_lane == h, jnp.sum(s_new * r[h:h + 1, :], axis=-1, keepdims=True), y_t)
        yt_ref[bi] = y_t
        return carry

    lax.fori_loop(0, SAMPLE_STEP_TILE, seq_body, 0)


def _sample_step(state, r, w, k, a, b, v_t):
    bt = SAMPLE_STEP_TILE
    hs = pl.BlockSpec((bt, N_RWKV_HEADS, HEAD_DIM), lambda i: (i, 0, 0))
    ts = pl.BlockSpec((bt, HEAD_DIM, N_RWKV_HEADS), lambda i: (i, 0, 0))
    ss = pl.BlockSpec((bt, N_RWKV_HEADS, HEAD_DIM, HEAD_DIM), lambda i: (i, 0, 0, 0))
    return pl.pallas_call(
        _sample_step_kernel,
        out_shape=(jax.ShapeDtypeStruct((DEC_BATCH, HEAD_DIM, N_RWKV_HEADS), F32),
                   jax.ShapeDtypeStruct((DEC_BATCH, N_RWKV_HEADS, HEAD_DIM, HEAD_DIM), F32)),
        grid=(DEC_BATCH // bt,),
        in_specs=[ss] + [hs] * 5 + [ts],
        out_specs=(ts, ss),
        compiler_params=_cparams(("arbitrary",)),
        name="sample_rwkv_step",
    )(state, r, w, k, a, b, v_t)


def _sample_post_kernel(y_ref, r_ref, k_ref, v_ref, g_ref, rk_ref, gng_ref, gnb_ref, o_ref):
    o_ref[...] = _rwkv_post(y_ref[...], r_ref[...], k_ref[...], v_ref[...], g_ref[...], rk_ref[...], gng_ref[...],
                            gnb_ref[...], _head_ones())


def _sample_post(y, r, k, v, g, prm):
    return pl.pallas_call(
        _sample_post_kernel,
        out_shape=jax.ShapeDtypeStruct((DEC_BATCH, D_RWKV), F32),
        compiler_params=pltpu.CompilerParams(vmem_limit_bytes=VMEM_LIMIT),
        name="sample_rwkv_post",
    )(y, r, k, v, g, prm["r_k"], prm["gn_g"], prm["gn_b"])


def _outproj_router_kernel(attn_ref, rwkv_ref, x_ref, wo_ref, g_ref, b_ref, wr_ref, br_ref, x1_ref, route_ref):
    mix = _dot(attn_ref[...].astype(BF16), wo_ref[0:D_ATTN, :]) + _dot(rwkv_ref[...].astype(BF16),
                                                                        wo_ref[D_ATTN:D_ATTN + D_RWKV, :])
    x1 = _layer_norm(ALPHA * x_ref[...] + mix, g_ref[...], b_ref[...])
    x1_ref[...] = x1
    logits = _dot1(x1, wr_ref[...]) + br_ref[...]
    tm = logits.shape[0]
    lane = lax.broadcasted_iota(jnp.int32, (tm, LANES), 1).astype(F32)
    big = float(2 * LANES)
    neg = -jnp.inf
    lc = jnp.where(lane < N_GROUPS, logits, neg)
    mc = jnp.max(lc, axis=-1, keepdims=True)
    g_sel = jnp.min(jnp.where(lc == mc, lane, big), axis=-1, keepdims=True)
    p_group = 1.0 / jnp.sum(jnp.exp(lc - mc), axis=-1, keepdims=True)
    lo = ROUTE_FINE_OFF + g_sel * EXPERTS_PER_GROUP
    lf = jnp.where((lane >= lo) & (lane < lo + EXPERTS_PER_GROUP), logits, neg)
    v1 = jnp.max(lf, axis=-1, keepdims=True)
    i1 = jnp.min(jnp.where(lf == v1, lane, big), axis=-1, keepdims=True)
    lf2 = jnp.where(lane == i1, neg, lf)
    v2 = jnp.max(lf2, axis=-1, keepdims=True)
    i2 = jnp.min(jnp.where(lf2 == v2, lane, big), axis=-1, keepdims=True)
    e21 = jnp.exp(v2 - v1)
    gate1 = p_group / (1.0 + e21)
    gate2 = p_group * e21 / (1.0 + e21)
    route = jnp.where(lane == 0, i1 - ROUTE_FINE_OFF,
                      jnp.where(lane == 1, i2 - ROUTE_FINE_OFF,
                                jnp.where(lane == 2, gate1, jnp.where(lane == 3, gate2, 0.0))))
    route_ref[...] = route


def _outproj_router(attn, rwkv, x, wo_bf16, ln_g, ln_b, w_route, b_route, tm, name):
    m = x.shape[0]
    const = lambda shape: pl.BlockSpec(shape, lambda i: (0, 0))
    return pl.pallas_call(
        _outproj_router_kernel,
        out_shape=(jax.ShapeDtypeStruct((m, D_MODEL), F32), jax.ShapeDtypeStruct((m, LANES), F32)),
        grid=(m // tm,),
        in_specs=[pl.BlockSpec((tm, D_ATTN), lambda i: (i, 0)),
                  pl.BlockSpec((tm, D_RWKV), lambda i: (i, 0)),
                  pl.BlockSpec((tm, D_MODEL), lambda i: (i, 0)),
                  const((D_MODEL, D_MODEL)), const((1, D_MODEL)), const((1, D_MODEL)),
                  const((D_MODEL, LANES)), const((1, LANES))],
        out_specs=(pl.BlockSpec((tm, D_MODEL), lambda i: (i, 0)), pl.BlockSpec((tm, LANES), lambda i: (i, 0))),
        compiler_params=_cparams(("arbitrary",)),
        name=name,
    )(attn, rwkv, x, wo_bf16, ln_g, ln_b, w_route, b_route)


def _expert_kernel(be_ref, nb_ref, rows_ref, rows_next_ref, x_hbm, wg_ref, wu_ref, wd_ref, o_ref, xbuf, sem):
    blk = pl.program_id(0)
    n_used = nb_ref[0]
    cur = lax.rem(blk, 2)

    def gather(table_ref, buf):
        for slot in range(MOE_BLOCK):
            pltpu.make_async_copy(x_hbm.at[pl.ds(table_ref[0, 0, slot], 1)], xbuf.at[buf, pl.ds(slot, 1)],
                                  sem.at[buf]).start()

    def wait_gather(buf):
        pltpu.make_async_copy(x_hbm.at[pl.ds(0, MOE_BLOCK)], xbuf.at[buf], sem.at[buf]).wait()

    @pl.when(blk == 0)
    def _():
        gather(rows_ref, 0)

    @pl.when(blk < n_used)
    def _():
        gather(rows_next_ref, 1 - cur)
        wait_gather(cur)
        xb = xbuf[cur].astype(BF16)
        gate = _dot(xb, wg_ref[0].astype(BF16))
        up = _dot(xb, wu_ref[0].astype(BF16))
        h = gate * _sigmoid(gate) * up
        o_ref[...] = _dot(h.astype(BF16), wd_ref[0].astype(BF16))

    @pl.when(blk >= n_used)
    def _():
        o_ref[...] = jnp.zeros_like(o_ref)

    @pl.when(blk == n_used)
    def _():
        wait_gather(cur)

    @pl.when((blk < n_used) & (blk == pl.num_programs(0) - 1))
    def _():
        wait_gather(1 - cur)


def _expert_mlp(block_expert, n_used, rows, x_all, w_gate, w_up, w_down, n_blocks):
    rows_spec = lambda off: pl.BlockSpec((1, 1, MOE_BLOCK), lambda b, be, nb: (b + off, 0, 0),
                                         memory_space=pltpu.SMEM)
    grid_spec = pltpu.PrefetchScalarGridSpec(
        num_scalar_prefetch=2,
        grid=(n_blocks,),
        in_specs=[rows_spec(0), rows_spec(1),
                  pl.BlockSpec(memory_space=pl.ANY),
                  pl.BlockSpec((1, D_MODEL, D_EXPERT), lambda b, be, nb: (be[b], 0, 0)),
                  pl.BlockSpec((1, D_MODEL, D_EXPERT), lambda b, be, nb: (be[b], 0, 0)),
                  pl.BlockSpec((1, D_EXPERT, D_MODEL), lambda b, be, nb: (be[b], 0, 0))],
        out_specs=pl.BlockSpec((MOE_BLOCK, D_MODEL), lambda b, be, nb: (b, 0)),
        scratch_shapes=[pltpu.VMEM((2, MOE_BLOCK, D_MODEL), F32), pltpu.SemaphoreType.DMA((2,))],
    )
    return pl.pallas_call(
        _expert_kernel,
        out_shape=jax.ShapeDtypeStruct((n_blocks * MOE_BLOCK, D_MODEL), F32),
        grid_spec=grid_spec,
        compiler_params=_cparams(("arbitrary",)),
        name="expert_mlp",
    )(block_expert, n_used, rows, rows, x_all, w_gate, w_up, w_down)


COMBINE_TILE = 128


def _combine_kernel(dest_ref, dest_next_ref, y_hbm, x1_ref, route_ref, g_ref, b_ref, o_ref, ybuf, sem):
    i = pl.program_id(0)
    cur = lax.rem(i, 2)
    n_rows = 2 * COMBINE_TILE

    def gather(table_ref, buf):
        for slot in range(n_rows):
            pltpu.make_async_copy(y_hbm.at[pl.ds(table_ref[0, 0, slot], 1)], ybuf.at[buf, pl.ds(slot, 1)],
                                  sem.at[buf]).start()

    def wait_gather(buf):
        pltpu.make_async_copy(y_hbm.at[pl.ds(0, n_rows)], ybuf.at[buf], sem.at[buf]).wait()

    @pl.when(i == 0)
    def _():
        gather(dest_ref, 0)

    gather(dest_next_ref, 1 - cur)
    wait_gather(cur)
    route = route_ref[...]
    yb = ybuf[cur]
    moe = route[:, 2:3] * yb[0:COMBINE_TILE, :] + route[:, 3:4] * yb[COMBINE_TILE:n_rows, :]
    o_ref[...] = _layer_norm(ALPHA * x1_ref[...] + moe, g_ref[...], b_ref[...])

    @pl.when(i == pl.num_programs(0) - 1)
    def _():
        wait_gather(1 - cur)


def _combine(dest, y_slots, x1, route, ln_g, ln_b, name):
    m = x1.shape[0]
    tm = COMBINE_TILE
    return pl.pallas_call(
        _combine_kernel,
        out_shape=jax.ShapeDtypeStruct((m, D_MODEL), F32),
        grid=(m // tm,),
        in_specs=[pl.BlockSpec((1, 1, 2 * tm), lambda i: (i, 0, 0), memory_space=pltpu.SMEM),
                  pl.BlockSpec((1, 1, 2 * tm), lambda i: (i + 1, 0, 0), memory_space=pltpu.SMEM),
                  pl.BlockSpec(memory_space=pl.ANY),
                  pl.BlockSpec((tm, D_MODEL), lambda i: (i, 0)),
                  pl.BlockSpec((tm, LANES), lambda i: (i, 0)),
                  pl.BlockSpec((1, D_MODEL), lambda i: (0, 0)),
                  pl.BlockSpec((1, D_MODEL), lambda i: (0, 0))],
        out_specs=pl.BlockSpec((tm, D_MODEL), lambda i: (i, 0)),
        scratch_shapes=[pltpu.VMEM((2, 2 * tm, D_MODEL), F32), pltpu.SemaphoreType.DMA((2,))],
        compiler_params=_cparams(("arbitrary",)),
        name=name,
    )(dest, dest, y_slots, x1, route, ln_g, ln_b)


def _feat_layout(t):
    pad = lambda a, n: jnp.pad(a, [(0, 0)] * (a.ndim - 1) + [(0, n - a.shape[-1])])
    o = 3 * D_RWKV
    return jnp.concatenate([t[..., :o], pad(t[..., o:o + W_LORA], XW_PAD),
                            pad(t[..., o + W_LORA:o + W_LORA + A_LORA], XA_PAD),
                            pad(t[..., o + W_LORA + A_LORA:], XG_PAD)], axis=-1)


def _feat_unlayout(t):
    return jnp.concatenate([t[..., :OFF_XW + W_LORA], t[..., OFF_XA:OFF_XA + A_LORA],
                            t[..., OFF_XG:OFF_XG + G_LORA]], axis=-1)


def _pad_rows(a, n):
    return jnp.pad(a, ((0, n - a.shape[0]), (0, 0)))


def _dispatch_plan(route_all, n_tokens, n_blocks):
    flat_e = route_all[:, 0:2].astype(jnp.int32).reshape(-1)
    n_assign = flat_e.shape[0]
    onehot = (flat_e[:, None] == jnp.arange(N_EXPERTS, dtype=jnp.int32)[None, :]).astype(jnp.int32)
    csum = jnp.cumsum(onehot, axis=0)
    rank = jnp.sum(onehot * csum, axis=1) - 1
    counts = csum[-1]
    padded = (counts + MOE_BLOCK - 1) // MOE_BLOCK * MOE_BLOCK
    pend = jnp.cumsum(padded)
    pstart = pend - padded
    dest = (pstart[flat_e] + rank).astype(jnp.int32)
    token = (jnp.arange(n_assign, dtype=jnp.int32) // 2)
    rows = jnp.full(((n_blocks + 1) * MOE_BLOCK,), n_tokens, jnp.int32).at[dest].set(token)
    block_e = jnp.minimum(jnp.searchsorted(pend, jnp.arange(n_blocks, dtype=jnp.int32) * MOE_BLOCK, side="right"),
                          N_EXPERTS - 1).astype(jnp.int32)
    n_used = (pend[-1] // MOE_BLOCK).astype(jnp.int32).reshape(1)
    return dest, rows, block_e, n_used


def kernel(x_prompt, x_sample, cache_k_win, cache_v_win, state_wkv, state_shift, w_in, attn_sinks, shift_mu, w0,
           w_decay_up, a0, w_a_up, w_g_up, k_k, k_a, r_k, gn_g, gn_b, w_out, ln1_g, ln1_b, w_coarse, b_coarse,
           w_fine, b_fine, w_exp_gate, w_exp_up, w_exp_down, ln2_g, ln2_b):
    xp = x_prompt[0]
    xs = x_sample[:, 0]
    row = lambda a: a.reshape(1, -1)

    w_in0 = w_in[0]
    w_feat = _feat_layout(w_in0[:, D_QKV:])
    prm = dict(mu=row(_feat_layout(shift_mu[0])), w0=row(w0[0]), a0=row(a0[0]), k_k=row(k_k[0]), k_a=row(k_a[0]),
               r_k=row(r_k[0]), gn_g=row(gn_g[0]), gn_b=row(gn_b[0]),
               wd=_pad_rows(w_decay_up[0], XW_PAD), wa=_pad_rows(w_a_up[0], XA_PAD), wg=_pad_rows(w_g_up[0], XG_PAD))
    sinks = attn_sinks[0]
    wo_bf16 = w_out[0].astype(BF16)
    w_route = jnp.pad(jnp.concatenate([w_coarse[0], w_fine[0]], axis=1), ((0, 0), (0, LANES - N_GROUPS - N_EXPERTS)))
    b_route = jnp.pad(jnp.concatenate([b_coarse[0], b_fine[0]]), (0, LANES - N_GROUPS - N_EXPERTS)).reshape(1, LANES)

    hp_attn = _matmul(xp, w_in0, D_QKV, 1024, 768, "in_proj_prompt_attn")
    hp_feat = _matmul(xp, w_feat, D_FEAT, 1024, 896, "in_proj_prompt_feat")
    hs_attn = _matmul(xs, w_in0, D_QKV, DEC_BATCH, 768, "in_proj_sample_attn")
    hs_feat = _matmul(xs, w_feat, D_FEAT, DEC_BATCH, 896, "in_proj_sample_feat")

    attn_p = _prompt_attention(hp_attn, sinks)
    rwkv_p, state_p = _prompt_rwkv(hp_feat, prm)

    q_s = hs_attn[:, :D_ATTN].reshape(DEC_BATCH, N_Q_HEADS, HEAD_DIM)
    k_s = hs_attn[:, D_ATTN:D_ATTN + D_KV].reshape(DEC_BATCH, 1, D_KV)
    v_s = hs_attn[:, D_ATTN + D_KV:].reshape(DEC_BATCH, 1, D_KV)
    attn_s, kwin_s, vwin_s = _sample_attention(
        q_s, k_s, v_s, cache_k_win[0].reshape(DEC_BATCH, WINDOW, D_KV),
        cache_v_win[0].reshape(DEC_BATCH, WINDOW, D_KV), sinks.reshape(N_Q_HEADS, 1))
    r_s, w_s, k2_s, vv_s, a_s, b_s, g_s = _sample_prep(hs_feat, _feat_layout(state_shift[0]), prm)
    heads = lambda a: a.reshape(DEC_BATCH, N_RWKV_HEADS, HEAD_DIM)
    y_t, state_s = _sample_step(state_wkv[0], heads(r_s), heads(w_s), heads(k2_s), heads(a_s), heads(b_s),
                                heads(vv_s).transpose(0, 2, 1))
    rwkv_s = _sample_post(y_t.transpose(0, 2, 1).reshape(DEC_BATCH, D_RWKV), r_s, k2_s, vv_s, g_s, prm)

    x1_p, route_p = _outproj_router(attn_p, rwkv_p, xp, wo_bf16, row(ln1_g[0]), row(ln1_b[0]), w_route, b_route,
                                    256, "outproj_router_prompt")
    x1_s, route_s = _outproj_router(attn_s.reshape(DEC_BATCH, D_ATTN), rwkv_s, xs,
                                    wo_bf16, row(ln1_g[0]), row(ln1_b[0]), w_route, b_route,
                                    DEC_BATCH, "outproj_router_sample")

    n_tokens = SEQ + DEC_BATCH
    n_assign = 2 * n_tokens
    n_blocks = -(-(n_assign + N_EXPERTS * (MOE_BLOCK - 1)) // MOE_BLOCK)
    dest, rows, block_e, n_used = _dispatch_plan(jnp.concatenate([route_p, route_s], axis=0), n_tokens, n_blocks)
    x_all = jnp.concatenate([x1_p, x1_s, jnp.zeros((SUBLANES, D_MODEL), F32)], axis=0)
    y_slots = _expert_mlp(block_e, n_used, rows.reshape(n_blocks + 1, 1, MOE_BLOCK), x_all,
                          w_exp_gate[0], w_exp_up[0], w_exp_down[0], n_blocks)

    def dest_tiles(d):
        d = d.reshape(-1, COMBINE_TILE, 2)
        d = jnp.concatenate([d[:, :, 0], d[:, :, 1]], axis=1)
        return jnp.pad(d, ((0, 1), (0, 0))).reshape(-1, 1, 2 * COMBINE_TILE)

    y_p = _combine(dest_tiles(dest[:2 * SEQ]), y_slots, x1_p, route_p, row(ln2_g[0]), row(ln2_b[0]), "combine_prompt")
    y_s = _combine(dest_tiles(dest[2 * SEQ:]), y_slots, x1_s, route_s, row(ln2_g[0]), row(ln2_b[0]), "combine_sample")

    kv4 = lambda a: a.reshape(a.shape[0], N_KV_HEADS, HEAD_DIM)
    k_win_p = kv4(hp_attn[SEQ - WINDOW:, D_ATTN:D_ATTN + D_KV])[None, None]
    v_win_p = kv4(hp_attn[SEQ - WINDOW:, D_ATTN + D_KV:])[None, None]
    sp = state_p.reshape(N_PAIRS, HEADS_PER_TILE, HEAD_DIM, HEADS_PER_TILE, HEAD_DIM)
    wkv_p = jnp.stack([sp[:, i, :, i, :] for i in range(HEADS_PER_TILE)], axis=1)
    wkv_p = wkv_p.reshape(N_RWKV_HEADS, HEAD_DIM, HEAD_DIM).transpose(0, 2, 1)[None, None]
    shift_p = _feat_unlayout(hp_feat[SEQ - 1:SEQ])[None]
    shift_s = _feat_unlayout(hs_feat)[None]
    return (y_p[None], y_s[:, None, :], k_win_p, v_win_p, wkv_p, shift_p,
            kwin_s.reshape(1, DEC_BATCH, WINDOW, N_KV_HEADS, HEAD_DIM),
            vwin_s.reshape(1, DEC_BATCH, WINDOW, N_KV_HEADS, HEAD_DIM),
            state_s[None], shift_s)
```

```python
import functools
import math

import jax
import jax.numpy as jnp
from jax import lax
from jax.experimental import pallas as pl
from jax.experimental.pallas import tpu as pltpu

F32 = jnp.float32
BF16 = jnp.bfloat16

D_MODEL = 2048
SEQ = 8192
DEC_BATCH = 128
HEAD_DIM = 64
D_ATTN = 1024
D_RWKV = 1024
N_Q_HEADS = 16
N_KV_HEADS = 4
Q_PER_KV = 4
D_KV = 256
WINDOW = 128
ATTN_SCALE = HEAD_DIM ** -0.5
N_RWKV_HEADS = 16
W_LORA = 64
A_LORA = 64
G_LORA = 160
D_SHIFT = 3 * D_RWKV + W_LORA + A_LORA + G_LORA
D_QKV = D_ATTN + 2 * D_KV
N_GROUPS = 4
EXPERTS_PER_GROUP = 8
N_EXPERTS = 32
D_EXPERT = 512
ALPHA = 2.0 ** 0.25
LN_EPS = 1e-5
GN_EPS = 64e-5

SUBLANES = 8
LANES = 128
VMEM_LIMIT = 52 * 1024 * 1024

D_RKV = 3 * D_RWKV
D_TAIL = W_LORA + A_LORA + G_LORA
D_MAIN = D_QKV + D_RKV
MAIN_TN = 768

CHUNK = 64
HEADS_PER_TILE = LANES // HEAD_DIM
N_PAIRS = N_RWKV_HEADS // HEADS_PER_TILE
SOLVE_LEVELS = int(math.log2(CHUNK))

MOE_BLOCK = 256
ROUTE_FINE_OFF = N_GROUPS

NN = (((1,), (0,)), ((), ()))
NT = (((1,), (1,)), ((), ()))


def _dot(a, b, dims=NN):
    return lax.dot_general(a, b, dims, preferred_element_type=F32)


def _dot1(a, b, dims=NN):
    return _dot(a.astype(BF16), b.astype(BF16), dims)


def _split(x):
    hi = x.astype(BF16)
    lo = (x - hi.astype(F32)).astype(BF16)
    return hi, lo


def _dot3(a, b, dims=NN):
    ah, al = _split(a)
    bh, bl = _split(b)
    return _dot(ah, bh, dims) + (_dot(ah, bl, dims) + _dot(al, bh, dims))


def _dot_exact_lhs(a_bf16, b, dims=NN):
    bh, bl = _split(b)
    return _dot(a_bf16, bh, dims) + _dot(a_bf16, bl, dims)


def _dot_exact_rhs(a, b_bf16, dims=NN):
    ah, al = _split(a)
    return _dot(ah, b_bf16, dims) + _dot(al, b_bf16, dims)


def _div_pow2(x, d):
    return lax.shift_right_logical(x, jnp.int32(int(math.log2(d))))


def _mod_pow2(x, d):
    return lax.bitwise_and(x, jnp.int32(d - 1))


def _sigmoid(x):
    return 1.0 / (1.0 + jnp.exp(-x))


def _softplus(x):
    return jnp.maximum(x, 0.0) + jnp.log(1.0 + jnp.exp(-jnp.abs(x)))


def _layer_norm(z, g, b):
    mu = jnp.mean(z, axis=-1, keepdims=True)
    d = z - mu
    var = jnp.mean(d * d, axis=-1, keepdims=True)
    return d * lax.rsqrt(var + LN_EPS) * g + b


def _cparams(sem):
    return pltpu.CompilerParams(dimension_semantics=sem, vmem_limit_bytes=VMEM_LIMIT)


def _matmul_kernel(x_ref, w_ref, o_ref):
    o_ref[...] = _dot(x_ref[...].astype(BF16), w_ref[...].astype(BF16))


def _matmul(x, w, n_out, tm, tn, name):
    m, k = x.shape
    tm = min(tm, m)
    return pl.pallas_call(
        _matmul_kernel,
        out_shape=jax.ShapeDtypeStruct((m, n_out), F32),
        grid=(n_out // tn, m // tm),
        in_specs=[pl.BlockSpec((tm, k), lambda j, i: (i, 0)),
                  pl.BlockSpec((k, tn), lambda j, i: (0, j))],
        out_specs=pl.BlockSpec((tm, tn), lambda j, i: (i, j)),
        compiler_params=_cparams(("arbitrary", "arbitrary")),
        name=name,
    )(x, w)


def _prompt_attn_kernel(q_ref, kvp_ref, kvc_ref, sink_ref, o_ref):
    blk = pl.program_id(0)
    q = q_ref[...]
    kv_prev = kvp_ref[...]
    kv_cur = kvc_ref[...]
    qi = _mod_pow2(lax.broadcasted_iota(jnp.int32, (Q_PER_KV * WINDOW, 2 * WINDOW), 0), WINDOW)
    kj = lax.broadcasted_iota(jnp.int32, (Q_PER_KV * WINDOW, 2 * WINDOW), 1)
    diff = qi + WINDOW - kj
    mask = (diff >= 0) & (diff <= WINDOW) & ((blk > 0) | (kj >= WINDOW))
    row_head = _div_pow2(lax.broadcasted_iota(jnp.int32, (Q_PER_KV * WINDOW, 1), 0), WINDOW)
    outs = []
    for g in range(N_KV_HEADS):
        kc = jnp.concatenate([kv_prev[:, g * HEAD_DIM:(g + 1) * HEAD_DIM],
                              kv_cur[:, g * HEAD_DIM:(g + 1) * HEAD_DIM]], axis=0).astype(BF16)
        vc = jnp.concatenate([kv_prev[:, D_KV + g * HEAD_DIM:D_KV + (g + 1) * HEAD_DIM],
                              kv_cur[:, D_KV + g * HEAD_DIM:D_KV + (g + 1) * HEAD_DIM]], axis=0).astype(BF16)
        qs = jnp.concatenate(
            [q[:, (g * Q_PER_KV + h) * HEAD_DIM:(g * Q_PER_KV + h + 1) * HEAD_DIM] for h in range(Q_PER_KV)],
            axis=0).astype(BF16)
        s = _dot(qs, kc, NT) * ATTN_SCALE
        s = jnp.where(mask, s, -jnp.inf)
        sink = jnp.zeros((Q_PER_KV * WINDOW, 1), F32)
        for h in range(Q_PER_KV):
            sink = jnp.where(row_head == h, sink_ref[g * Q_PER_KV + h], sink)
        m = jnp.maximum(jnp.max(s, axis=-1, keepdims=True), sink)
        p = jnp.exp(s - m)
        denom = jnp.sum(p, axis=-1, keepdims=True) + jnp.exp(sink - m)
        p = p / denom
        o = _dot(p.astype(BF16), vc)
        for h in range(Q_PER_KV):
            outs.append(o[h * WINDOW:(h + 1) * WINDOW, :])
    o_ref[...] = jnp.concatenate(outs, axis=1)


def _prompt_attention(h_attn, sinks):
    nb = SEQ // WINDOW
    return pl.pallas_call(
        _prompt_attn_kernel,
        out_shape=jax.ShapeDtypeStruct((SEQ, D_ATTN), F32),
        grid=(nb,),
        in_specs=[pl.BlockSpec((WINDOW, D_ATTN), lambda i: (i, 0)),
                  pl.BlockSpec((WINDOW, 2 * D_KV), lambda i: (jnp.maximum(i - 1, 0), 2)),
                  pl.BlockSpec((WINDOW, 2 * D_KV), lambda i: (i, 2)),
                  pl.BlockSpec(memory_space=pltpu.SMEM)],
        out_specs=pl.BlockSpec((WINDOW, D_ATTN), lambda i: (i, 0)),
        compiler_params=_cparams(("arbitrary",)),
        name="prompt_attention",
    )(h_attn, h_attn, h_attn, sinks)


SAMPLE_ATTN_TILE = 8


def _sample_attn_kernel(q_ref, knew_ref, vnew_ref, ck_ref, cv_ref, sink_ref, o_ref, kwin_ref, vwin_ref):
    lane = lax.broadcasted_iota(jnp.int32, (N_Q_HEADS, D_KV), 1)
    head = lax.broadcasted_iota(jnp.int32, (N_Q_HEADS, D_KV), 0)
    group_mask = _div_pow2(lane, HEAD_DIM) == _div_pow2(head, Q_PER_KV)
    sink = sink_ref[...]
    row = lax.broadcasted_iota(jnp.int32, (WINDOW, D_KV), 0)
    for b in range(SAMPLE_ATTN_TILE):
        q = q_ref[b]
        qbd = jnp.where(group_mask, jnp.concatenate([q] * N_KV_HEADS, axis=1), 0.0).astype(BF16)
        kb = ck_ref[b]
        vb = cv_ref[b]
        kn = knew_ref[b]
        vn = vnew_ref[b]
        s = _dot1(qbd, kb, NT) * ATTN_SCALE
        s_new = jnp.sum(qbd.astype(F32) * kn.astype(BF16).astype(F32), axis=-1, keepdims=True) * ATTN_SCALE
        m = jnp.maximum(jnp.maximum(jnp.max(s, axis=-1, keepdims=True), s_new), sink)
        p = jnp.exp(s - m)
        p_new = jnp.exp(s_new - m)
        denom = jnp.sum(p, axis=-1, keepdims=True) + p_new + jnp.exp(sink - m)
        p = p / denom
        p_new = (p_new / denom).astype(BF16).astype(F32)
        o_full = _dot1(p, vb) + p_new * vn.astype(BF16).astype(F32)
        o_full = jnp.where(group_mask, o_full, 0.0)
        o = o_full[:, 0:HEAD_DIM]
        for g in range(1, N_KV_HEADS):
            o = o + o_full[:, g * HEAD_DIM:(g + 1) * HEAD_DIM]
        o_ref[b] = o
        kwin_ref[b] = jnp.where(row == WINDOW - 1, kn, pltpu.roll(kb, WINDOW - 1, axis=0))
        vwin_ref[b] = jnp.where(row == WINDOW - 1, vn, pltpu.roll(vb, WINDOW - 1, axis=0))


def _sample_attention(q, k_new, v_new, cache_k, cache_v, sinks):
    bt = SAMPLE_ATTN_TILE
    win_spec = pl.BlockSpec((bt, WINDOW, D_KV), lambda i: (i, 0, 0))
    new_spec = pl.BlockSpec((bt, 1, D_KV), lambda i: (i, 0, 0))
    return pl.pallas_call(
        _sample_attn_kernel,
        out_shape=(jax.ShapeDtypeStruct((DEC_BATCH, N_Q_HEADS, HEAD_DIM), F32),
                   jax.ShapeDtypeStruct((DEC_BATCH, WINDOW, D_KV), F32),
                   jax.ShapeDtypeStruct((DEC_BATCH, WINDOW, D_KV), F32)),
        grid=(DEC_BATCH // bt,),
        in_specs=[pl.BlockSpec((bt, N_Q_HEADS, HEAD_DIM), lambda i: (i, 0, 0)),
                  new_spec, new_spec, win_spec, win_spec,
                  pl.BlockSpec((N_Q_HEADS, 1), lambda i: (0, 0))],
        out_specs=(pl.BlockSpec((bt, N_Q_HEADS, HEAD_DIM), lambda i: (i, 0, 0)), win_spec, win_spec),
        compiler_params=_cparams(("arbitrary",)),
        name="sample_attention",
    )(q, k_new, v_new, cache_k, cache_v, sinks)


def _head_ones():
    r = _div_pow2(lax.broadcasted_iota(jnp.int32, (LANES, LANES), 0), HEAD_DIM)
    c = _div_pow2(lax.broadcasted_iota(jnp.int32, (LANES, LANES), 1), HEAD_DIM)
    return jnp.where(r == c, 1.0, 0.0).astype(BF16)


def _head_sum(x, ones):
    parts = [_dot_exact_rhs(x[:, p * LANES:(p + 1) * LANES], ones) for p in range(x.shape[1] // LANES)]
    return jnp.concatenate(parts, axis=1)


def _token_mix(feat, shifted, mu):
    return feat + (shifted - feat) * mu


def _rwkv_prep(mixed, mixed_tail, w0, a0, k_k, k_a, wd, wa, wg, ones):
    r = mixed[:, 0:D_RWKV]
    k = mixed[:, D_RWKV:2 * D_RWKV]
    v = mixed[:, 2 * D_RWKV:3 * D_RWKV]
    xw = mixed_tail[:, 0:W_LORA]
    xa = mixed_tail[:, W_LORA:W_LORA + A_LORA]
    xg = mixed_tail[:, W_LORA + A_LORA:D_TAIL]
    w_log = -_softplus(-(w0 + _dot1(jnp.tanh(xw), wd))) - 0.5
    log_decay = -jnp.exp(w_log)
    a = _sigmoid(a0 + _dot1(xa, wa))
    g = _dot1(_sigmoid(xg), wg)
    kk = k * k_k
    norm = jnp.sqrt(_head_sum(kk * kk, ones))
    kk = kk / jnp.maximum(norm, 1e-12)
    k2 = k * (1.0 + (a - 1.0) * k_a)
    return r, log_decay, k2, v, -kk, kk * a, g


def _rwkv_post(y, r, k2, v, g, r_k, gn_g, gn_b, ones):
    inv_n = 1.0 / HEAD_DIM
    mu = _head_sum(y, ones) * inv_n
    d = y - mu
    var = _head_sum(d * d, ones) * inv_n
    yn = d * lax.rsqrt(var + GN_EPS) * gn_g + gn_b
    bonus = _head_sum(r * k2 * r_k, ones) * v
    return (yn + bonus) * g


(OP_AABS, OP_RABS, OP_AN, OP_RN, OP_BN, OP_KN, OP_BH, OP_KH, OP_V) = range(9)
N_OPS = 9


def _prompt_rwkv_kernel(f1_ref, f2_ref, x_ref, wt_ref, mu_ref, mut_ref, w0_ref, a0_ref, kk_ref, ka_ref, rk_ref,
                        gng_ref, gnb_ref, wd_ref, wa_ref, wg_ref, out_ref, state_ref, tail_ref,
                        prev_ref, prevt_ref, s_ref, ops_ref, pc_ref, y_ref):
    c = pl.program_id(0)
    C = CHUNK

    @pl.when(c == 0)
    def _():
        prev_ref[...] = jnp.zeros_like(prev_ref)
        prevt_ref[...] = jnp.zeros_like(prevt_ref)
        s_ref[...] = jnp.zeros_like(s_ref)

    ones = _head_ones()
    row = lax.broadcasted_iota(jnp.int32, (C, 1), 0)

    def token_shift(feat, carry_ref):
        shifted = jnp.where(row == 0, carry_ref[0:1, :], pltpu.roll(feat, 1, axis=0))
        carry_ref[0:1, :] = feat[C - 1:C, :]
        return shifted

    feat = jnp.concatenate([f1_ref[...], f2_ref[...]], axis=1)
    tail = _dot1(x_ref[...], wt_ref[...])
    mixed = _token_mix(feat, token_shift(feat, prev_ref), mu_ref[...])
    mixed_tail = _token_mix(tail, token_shift(tail, prevt_ref), mut_ref[...])
    tail_ref[...] = prevt_ref[...]
    r, ld, k2, v, av, bv, g = _rwkv_prep(mixed, mixed_tail, w0_ref[...], a0_ref[...], kk_ref[...], ka_ref[...],
                                         wd_ref[...], wa_ref[...], wg_ref[...], ones)

    ti = lax.broadcasted_iota(jnp.int32, (C, C), 0)
    tj = lax.broadcasted_iota(jnp.int32, (C, C), 1)
    tri_incl = jnp.where(tj <= ti, 1.0, 0.0).astype(BF16)
    cs = _dot_exact_lhs(tri_incl, ld)
    cs_ref = cs[C // 2 - 1:C // 2, :]
    cs_end = cs[C - 1:C, :]
    e_prev = jnp.exp(cs - ld)
    e_cur = jnp.exp(cs)
    n_prev = jnp.exp(cs - ld - cs_ref)
    n_cur = jnp.exp(cs - cs_ref)
    n_inv = jnp.exp(cs_ref - cs)
    e_tail = jnp.exp(cs_end - cs)
    ops = {OP_AABS: av * e_prev, OP_RABS: r * e_cur, OP_AN: av * n_prev, OP_RN: r * n_cur,
           OP_BN: bv * n_inv, OP_KN: k2 * n_inv, OP_BH: bv * e_tail, OP_KH: k2 * e_tail, OP_V: v}
    p_end = jnp.exp(cs_end)
    for p in range(N_PAIRS):
        sl = slice(p * LANES, (p + 1) * LANES)
        for idx, val in ops.items():
            ops_ref[p, idx] = val[:, sl]
        pc_ref[p] = jnp.broadcast_to(p_end[:, sl], (SUBLANES, LANES))

    lane1 = lax.broadcasted_iota(jnp.int32, (C, LANES), 1)
    head0 = lane1 < HEAD_DIM
    r2 = lax.broadcasted_iota(jnp.int32, (2 * C, 2 * C), 0)
    c2 = lax.broadcasted_iota(jnp.int32, (2 * C, 2 * C), 1)
    tq = _mod_pow2(r2, C)
    tk = _mod_pow2(c2, C)
    band = (tk < tq) | ((tk == tq) & (r2 >= C))
    blockdiag = _div_pow2(r2, HEAD_DIM) == _div_pow2(c2, HEAD_DIM)

    pairs = range(N_PAIRS)
    op = lambda p, idx: ops_ref[p, idx]

    zero_half = jnp.zeros((C, LANES), F32)

    gy = [_dot1(jnp.concatenate([op(p, OP_AABS), op(p, OP_RABS)], axis=0), s_ref[p]) for p in pairs]

    am0, am1 = [], []
    for p in pairs:
        a_n, r_n = op(p, OP_AN), op(p, OP_RN)
        b0, k0 = jnp.where(head0, op(p, OP_BN), 0.0), jnp.where(head0, op(p, OP_KN), 0.0)
        b1, k1 = jnp.where(head0, 0.0, op(p, OP_BN)), jnp.where(head0, 0.0, op(p, OP_KN))
        am = _dot1(jnp.concatenate([a_n, r_n], axis=0), jnp.concatenate([k0, b0, b1, k1], axis=0), NT)
        am0.append(jnp.where(band, am[:, 0:2 * C], 0.0))
        am1.append(jnp.where(band, am[:, 2 * C:4 * C], 0.0))

    w0, w1 = [], []
    for p in pairs:
        top0, top1 = am0[p][0:C], am1[p][0:C]
        ak = jnp.concatenate([jnp.where(head0, top0, 0.0), jnp.where(head0, 0.0, top1)], axis=0)
        vv = op(p, OP_V)
        g0 = gy[p][0:C]
        m = jnp.concatenate([g0, g0], axis=0) + _dot1(ak, jnp.concatenate([vv, vv], axis=0))
        w0.append(jnp.where(head0, m[0:C], top0))
        w1.append(jnp.where(head0, top1, m[C:2 * C]))

    for lvl in range(SOLVE_LEVELS):
        prod0 = [_dot3(w0[p], jnp.concatenate([zero_half, w0[p]], axis=0)) for p in pairs]
        prod1 = [_dot3(w1[p], jnp.concatenate([w1[p], zero_half], axis=0)) for p in pairs]
        w0 = [jnp.where(head0, w0[p] + prod0[p], prod0[p]) for p in pairs]
        w1 = [jnp.where(head0, prod1[p], w1[p] + prod1[p]) for p in pairs]
    u = [jnp.where(head0, w0[p], w1[p]) for p in pairs]

    for p in pairs:
        vv = op(p, OP_V)
        y_lhs = jnp.concatenate([am0[p][C:2 * C], am1[p][C:2 * C]], axis=1)
        y_rhs = jnp.concatenate([jnp.where(head0, vv, 0.0), jnp.where(head0, u[p], 0.0),
                                 jnp.where(head0, 0.0, u[p]), jnp.where(head0, 0.0, vv)], axis=0)
        y_ref[p] = gy[p][C:2 * C] + _dot1(y_lhs, y_rhs)

    for p in pairs:
        decay_rows = jnp.broadcast_to(pc_ref[p][0:1, :], (LANES, LANES)).T
        upd_lhs = jnp.concatenate([op(p, OP_BH), op(p, OP_KH)], axis=0).T
        upd_rhs = jnp.concatenate([u[p], op(p, OP_V)], axis=0)
        s_ref[p] = s_ref[p] * decay_rows + jnp.where(blockdiag, _dot1(upd_lhs, upd_rhs), 0.0)

    y = jnp.concatenate([y_ref[p] for p in range(N_PAIRS)], axis=1)
    out_ref[...] = _rwkv_post(y, r, k2, v, g, rk_ref[...], gng_ref[...], gnb_ref[...], ones)

    @pl.when(c == pl.num_programs(0) - 1)
    def _():
        state_ref[...] = s_ref[...]


def _prompt_rwkv(h_main, x, prm):
    n_chunks = SEQ // CHUNK
    half = D_RKV // 2
    assert D_QKV == half
    vec = pl.BlockSpec((1, D_RWKV), lambda c: (0, 0))
    full = lambda a: pl.BlockSpec(a.shape, lambda c: (0,) * a.ndim)
    return pl.pallas_call(
        _prompt_rwkv_kernel,
        out_shape=(jax.ShapeDtypeStruct((SEQ, D_RWKV), F32),
                   jax.ShapeDtypeStruct((N_PAIRS, LANES, LANES), F32),
                   jax.ShapeDtypeStruct((SUBLANES, D_TAIL), F32)),
        grid=(n_chunks,),
        in_specs=[pl.BlockSpec((CHUNK, half), lambda c: (c, 1)),
                  pl.BlockSpec((CHUNK, half), lambda c: (c, 2)),
                  pl.BlockSpec((CHUNK, D_MODEL), lambda c: (c, 0)),
                  full(prm["w_tail"]), full(prm["mu"]), full(prm["mu_tail"]),
                  vec, vec, vec, vec, vec, vec, vec,
                  full(prm["wd"]), full(prm["wa"]), full(prm["wg"])],
        out_specs=(pl.BlockSpec((CHUNK, D_RWKV), lambda c: (c, 0)),
                   pl.BlockSpec((N_PAIRS, LANES, LANES), lambda c: (0, 0, 0)),
                   pl.BlockSpec((SUBLANES, D_TAIL), lambda c: (0, 0))),
        scratch_shapes=[pltpu.VMEM((SUBLANES, D_RKV), F32),
                        pltpu.VMEM((SUBLANES, D_TAIL), F32),
                        pltpu.VMEM((N_PAIRS, LANES, LANES), F32),
                        pltpu.VMEM((N_PAIRS, N_OPS, CHUNK, LANES), F32),
                        pltpu.VMEM((N_PAIRS, SUBLANES, LANES), F32),
                        pltpu.VMEM((N_PAIRS, CHUNK, LANES), F32)],
        compiler_params=_cparams(("arbitrary",)),
        name="prompt_rwkv",
    )(h_main, h_main, x, prm["w_tail"], prm["mu"], prm["mu_tail"], prm["w0"], prm["a0"], prm["k_k"], prm["k_a"],
      prm["r_k"], prm["gn_g"], prm["gn_b"], prm["wd"], prm["wa"], prm["wg"])


def _sample_prep_kernel(h_ref, x_ref, wt_ref, shift_ref, mu_ref, mut_ref, w0_ref, a0_ref, kk_ref, ka_ref,
                        wd_ref, wa_ref, wg_ref, r_ref, w_ref, k_ref, v_ref, a_ref, b_ref, g_ref, tail_ref):
    ones = _head_ones()
    feat = h_ref[:, D_QKV:D_MAIN]
    tail = _dot1(x_ref[...], wt_ref[...])
    tail_ref[...] = tail
    mixed = _token_mix(feat, shift_ref[:, 0:D_RKV], mu_ref[...])
    mixed_tail = _token_mix(tail, shift_ref[:, D_RKV:D_SHIFT], mut_ref[...])
    r, ld, k2, v, av, bv, g = _rwkv_prep(mixed, mixed_tail, w0_ref[...], a0_ref[...], kk_ref[...], ka_ref[...],
                                         wd_ref[...], wa_ref[...], wg_ref[...], ones)
    r_ref[...] = r
    w_ref[...] = jnp.exp(ld)
    k_ref[...] = k2
    v_ref[...] = v
    a_ref[...] = av
    b_ref[...] = bv
    g_ref[...] = g


def _sample_prep(h_main, x, shift, prm):
    out = jax.ShapeDtypeStruct((DEC_BATCH, D_RWKV), F32)
    return pl.pallas_call(
        _sample_prep_kernel,
        out_shape=(out,) * 7 + (jax.ShapeDtypeStruct((DEC_BATCH, D_TAIL), F32),),
        compiler_params=pltpu.CompilerParams(vmem_limit_bytes=VMEM_LIMIT),
        name="sample_rwkv_prep",
    )(h_main, x, prm["w_tail"], shift, prm["mu"], prm["mu_tail"], prm["w0"], prm["a0"], prm["k_k"], prm["k_a"],
      prm["wd"], prm["wa"], prm["wg"])


SAMPLE_STEP_TILE = 8


def _sample_step_kernel(s_ref, r_ref, w_ref, k_ref, a_ref, b_ref, vt_ref, yt_ref, snew_ref):
    head_lane = lax.broadcasted_iota(jnp.int32, (HEAD_DIM, N_RWKV_HEADS), 1)

    def seq_body(bi, carry):
        r = r_ref[bi]
        w = w_ref[bi]
        k = k_ref[bi]
        a = a_ref[bi]
        b = b_ref[bi]
        v_t = vt_ref[bi]
        heads = range(N_RWKV_HEADS)
        sa = [jnp.sum(s_ref[bi, h] * a[h:h + 1, :], axis=-1, keepdims=True) for h in heads]
        s_new = [s_ref[bi, h] * w[h:h + 1, :] + sa[h] * b[h:h + 1, :] + v_t[:, h:h + 1] * k[h:h + 1, :]
                 for h in heads]
        for h in heads:
            snew_ref[bi, h] = s_new[h]
        y_cols = [jnp.sum(s_new[h] * r[h:h + 1, :], axis=-1, keepdims=True) for h in heads]
        y_t = jnp.zeros((HEAD_DIM, N_RWKV_HEADS), F32)
        for h in heads:
            y_t = jnp.where(head_lane == h, y_cols[h], y_t)
        yt_ref[bi] = y_t
        return carry

    lax.fori_loop(0, SAMPLE_STEP_TILE, seq_body, 0)


def _sample_step(state, r, w, k, a, b, v_t):
    bt = SAMPLE_STEP_TILE
    hs = pl.BlockSpec((bt, N_RWKV_HEADS, HEAD_DIM), lambda i: (i, 0, 0))
    ts = pl.BlockSpec((bt, HEAD_DIM, N_RWKV_HEADS), lambda i: (i, 0, 0))
    ss = pl.BlockSpec((bt, N_RWKV_HEADS, HEAD_DIM, HEAD_DIM), lambda i: (i, 0, 0, 0))
    return pl.pallas_call(
        _sample_step_kernel,
        out_shape=(jax.ShapeDtypeStruct((DEC_BATCH, HEAD_DIM, N_RWKV_HEADS), F32),
                   jax.ShapeDtypeStruct((DEC_BATCH, N_RWKV_HEADS, HEAD_DIM, HEAD_DIM), F32)),
        grid=(DEC_BATCH // bt,),
        in_specs=[ss] + [hs] * 5 + [ts],
        out_specs=(ts, ss),
        compiler_params=_cparams(("arbitrary",)),
        name="sample_rwkv_step",
    )(state, r, w, k, a, b, v_t)


def _sample_post_kernel(y_ref, r_ref, k_ref, v_ref, g_ref, rk_ref, gng_ref, gnb_ref, o_ref):
    o_ref[...] = _rwkv_post(y_ref[...], r_ref[...], k_ref[...], v_ref[...], g_ref[...], rk_ref[...], gng_ref[...],
                            gnb_ref[...], _head_ones())


def _sample_post(y, r, k, v, g, prm):
    return pl.pallas_call(
        _sample_post_kernel,
        out_shape=jax.ShapeDtypeStruct((DEC_BATCH, D_RWKV), F32),
        compiler_params=pltpu.CompilerParams(vmem_limit_bytes=VMEM_LIMIT),
        name="sample_rwkv_post",
    )(y, r, k, v, g, prm["r_k"], prm["gn_g"], prm["gn_b"])


def _outproj_router_kernel(attn_ref, rwkv_ref, x_ref, wo_ref, g_ref, b_ref, wr_ref, br_ref, x1_ref, route_ref):
    mix = _dot(attn_ref[...].astype(BF16), wo_ref[0:D_ATTN, :]) + _dot(rwkv_ref[...].astype(BF16),
                                                                        wo_ref[D_ATTN:D_ATTN + D_RWKV, :])
    x1 = _layer_norm(ALPHA * x_ref[...] + mix, g_ref[...], b_ref[...])
    x1_ref[...] = x1
    logits = _dot1(x1, wr_ref[...]) + br_ref[...]
    tm = logits.shape[0]
    lane = lax.broadcasted_iota(jnp.int32, (tm, LANES), 1).astype(F32)
    big = float(2 * LANES)
    neg = -jnp.inf
    lc = jnp.where(lane < N_GROUPS, logits, neg)
    mc = jnp.max(lc, axis=-1, keepdims=True)
    g_sel = jnp.min(jnp.where(lc == mc, lane, big), axis=-1, keepdims=True)
    p_group = 1.0 / jnp.sum(jnp.exp(lc - mc), axis=-1, keepdims=True)
    lo = ROUTE_FINE_OFF + g_sel * EXPERTS_PER_GROUP
    lf = jnp.where((lane >= lo) & (lane < lo + EXPERTS_PER_GROUP), logits, neg)
    v1 = jnp.max(lf, axis=-1, keepdims=True)
    i1 = jnp.min(jnp.where(lf == v1, lane, big), axis=-1, keepdims=True)
    lf2 = jnp.where(lane == i1, neg, lf)
    v2 = jnp.max(lf2, axis=-1, keepdims=True)
    i2 = jnp.min(jnp.where(lf2 == v2, lane, big), axis=-1, keepdims=True)
    e21 = jnp.exp(v2 - v1)
    gate1 = p_group / (1.0 + e21)
    gate2 = p_group * e21 / (1.0 + e21)
    route = jnp.where(lane == 0, i1 - ROUTE_FINE_OFF,
                      jnp.where(lane == 1, i2 - ROUTE_FINE_OFF,
                                jnp.where(lane == 2, gate1, jnp.where(lane == 3, gate2, 0.0))))
    route_ref[...] = route


def _outproj_router_into_kernel(attn_ref, rwkv_ref, x_ref, wo_ref, g_ref, b_ref, wr_ref, br_ref, x1_in, route_in,
                                x1_ref, route_ref):
    del x1_in, route_in
    _outproj_router_kernel(attn_ref, rwkv_ref, x_ref, wo_ref, g_ref, b_ref, wr_ref, br_ref, x1_ref, route_ref)


def _outproj_router_fill_kernel(n_steps, *refs):
    x1_ref, route_ref = refs[-2:]

    @pl.when(pl.program_id(0) < n_steps)
    def _():
        _outproj_router_kernel(*refs)

    @pl.when(pl.program_id(0) >= n_steps)
    def _():
        x1_ref[...] = jnp.zeros_like(x1_ref)
        route_ref[...] = jnp.zeros_like(route_ref)


def _outproj_router(attn, rwkv, x, wo_bf16, ln_g, ln_b, w_route, b_route, tm, n_total, row_block, into, name):
    m = x.shape[0]
    n_steps = m // tm
    const = lambda shape: pl.BlockSpec(shape, lambda i: (0, 0))
    rows = lambda width: pl.BlockSpec((tm, width), lambda i: (jnp.minimum(i, n_steps - 1), 0))
    in_specs = [rows(D_ATTN), rows(D_RWKV), rows(D_MODEL),
                const((D_MODEL, D_MODEL)), const((1, D_MODEL)), const((1, D_MODEL)),
                const((D_MODEL, LANES)), const((1, LANES))]
    args = [attn, rwkv, x, wo_bf16, ln_g, ln_b, w_route, b_route]
    aliases = {}
    if into is not None:
        in_specs += [pl.BlockSpec(memory_space=pl.ANY)] * 2
        aliases = {len(args): 0, len(args) + 1: 1}
        args += list(into)
        body, grid_steps = _outproj_router_into_kernel, n_steps
    else:
        body, grid_steps = functools.partial(_outproj_router_fill_kernel, n_steps), pl.cdiv(n_total, tm)
    return pl.pallas_call(
        body,
        out_shape=(jax.ShapeDtypeStruct((n_total, D_MODEL), F32), jax.ShapeDtypeStruct((n_total, LANES), F32)),
        grid=(grid_steps,),
        in_specs=in_specs,
        out_specs=(pl.BlockSpec((tm, D_MODEL), lambda i: (i + row_block, 0)),
                   pl.BlockSpec((tm, LANES), lambda i: (i + row_block, 0))),
        input_output_aliases=aliases,
        compiler_params=_cparams(("arbitrary",)),
        name=name,
    )(*args)


def _expert_kernel(be_ref, nb_ref, rows_ref, rows_next_ref, x_hbm, wg_ref, wu_ref, wd_ref, o_ref, xbuf, sem):
    blk = pl.program_id(0)
    n_used = nb_ref[0]
    cur = lax.rem(blk, 2)

    def gather(table_ref, buf):
        for slot in range(MOE_BLOCK):
            pltpu.make_async_copy(x_hbm.at[pl.ds(table_ref[0, 0, slot], 1)], xbuf.at[buf, pl.ds(slot, 1)],
                                  sem.at[buf]).start()

    def wait_gather(buf):
        pltpu.make_async_copy(x_hbm.at[pl.ds(0, MOE_BLOCK)], xbuf.at[buf], sem.at[buf]).wait()

    @pl.when(blk == 0)
    def _():
        gather(rows_ref, 0)

    @pl.when(blk < n_used)
    def _():
        gather(rows_next_ref, 1 - cur)
        wait_gather(cur)
        xb = xbuf[cur].astype(BF16)
        gate = _dot(xb, wg_ref[0].astype(BF16))
        up = _dot(xb, wu_ref[0].astype(BF16))
        h = gate * _sigmoid(gate) * up
        o_ref[...] = _dot(h.astype(BF16), wd_ref[0].astype(BF16))

    @pl.when(blk >= n_used)
    def _():
        o_ref[...] = jnp.zeros_like(o_ref)

    @pl.when(blk == n_used)
    def _():
        wait_gather(cur)

    @pl.when((blk < n_used) & (blk == pl.num_programs(0) - 1))
    def _():
        wait_gather(1 - cur)


def _expert_mlp(block_expert, n_used, rows, x_all, w_gate, w_up, w_down, n_blocks):
    rows_spec = lambda off: pl.BlockSpec((1, 1, MOE_BLOCK), lambda b, be, nb: (b + off, 0, 0),
                                         memory_space=pltpu.SMEM)
    grid_spec = pltpu.PrefetchScalarGridSpec(
        num_scalar_prefetch=2,
        grid=(n_blocks,),
        in_specs=[rows_spec(0), rows_spec(1),
                  pl.BlockSpec(memory_space=pl.ANY),
                  pl.BlockSpec((1, D_MODEL, D_EXPERT), lambda b, be, nb: (be[b], 0, 0)),
                  pl.BlockSpec((1, D_MODEL, D_EXPERT), lambda b, be, nb: (be[b], 0, 0)),
                  pl.BlockSpec((1, D_EXPERT, D_MODEL), lambda b, be, nb: (be[b], 0, 0))],
        out_specs=pl.BlockSpec((MOE_BLOCK, D_MODEL), lambda b, be, nb: (b, 0)),
        scratch_shapes=[pltpu.VMEM((2, MOE_BLOCK, D_MODEL), F32), pltpu.SemaphoreType.DMA((2,))],
    )
    return pl.pallas_call(
        _expert_kernel,
        out_shape=jax.ShapeDtypeStruct((n_blocks * MOE_BLOCK, D_MODEL), F32),
        grid_spec=grid_spec,
        compiler_params=_cparams(("arbitrary",)),
        name="expert_mlp",
    )(block_expert, n_used, rows, rows, x_all, w_gate, w_up, w_down)


COMBINE_TILE = 128


def _combine_kernel(dest_ref, dest_next_ref, y_hbm, x1_ref, route_ref, g_ref, b_ref, o_ref, ybuf, sem):
    i = pl.program_id(0)
    cur = lax.rem(i, 2)
    n_rows = 2 * COMBINE_TILE

    def gather(table_ref, buf):
        for slot in range(n_rows):
            pltpu.make_async_copy(y_hbm.at[pl.ds(table_ref[0, 0, slot], 1)], ybuf.at[buf, pl.ds(slot, 1)],
                                  sem.at[buf]).start()

    def wait_gather(buf):
        pltpu.make_async_copy(y_hbm.at[pl.ds(0, n_rows)], ybuf.at[buf], sem.at[buf]).wait()

    @pl.when(i == 0)
    def _():
        gather(dest_ref, 0)

    gather(dest_next_ref, 1 - cur)
    wait_gather(cur)
    route = route_ref[...]
    yb = ybuf[cur]
    moe = route[:, 2:3] * yb[0:COMBINE_TILE, :] + route[:, 3:4] * yb[COMBINE_TILE:n_rows, :]
    o_ref[...] = _layer_norm(ALPHA * x1_ref[...] + moe, g_ref[...], b_ref[...])

    @pl.when(i == pl.num_programs(0) - 1)
    def _():
        wait_gather(1 - cur)


def _combine(dest, y_slots, x1_all, route_all, m, row_block, ln_g, ln_b, name):
    tm = COMBINE_TILE
    return pl.pallas_call(
        _combine_kernel,
        out_shape=jax.ShapeDtypeStruct((m, D_MODEL), F32),
        grid=(m // tm,),
        in_specs=[pl.BlockSpec((1, 1, 2 * tm), lambda i: (i, 0, 0), memory_space=pltpu.SMEM),
                  pl.BlockSpec((1, 1, 2 * tm), lambda i: (i + 1, 0, 0), memory_space=pltpu.SMEM),
                  pl.BlockSpec(memory_space=pl.ANY),
                  pl.BlockSpec((tm, D_MODEL), lambda i: (i + row_block, 0)),
                  pl.BlockSpec((tm, LANES), lambda i: (i + row_block, 0)),
                  pl.BlockSpec((1, D_MODEL), lambda i: (0, 0)),
                  pl.BlockSpec((1, D_MODEL), lambda i: (0, 0))],
        out_specs=pl.BlockSpec((tm, D_MODEL), lambda i: (i, 0)),
        scratch_shapes=[pltpu.VMEM((2, 2 * tm, D_MODEL), F32), pltpu.SemaphoreType.DMA((2,))],
        compiler_params=_cparams(("arbitrary",)),
        name=name,
    )(dest, dest, y_slots, x1_all, route_all, ln_g, ln_b)


def _dispatch_plan(route_all, n_blocks):
    flat_e = route_all[:, 0:2].astype(jnp.int32).reshape(-1)
    n_assign = flat_e.shape[0]
    onehot = (flat_e[:, None] == jnp.arange(N_EXPERTS, dtype=jnp.int32)[None, :]).astype(jnp.int32)
    csum = jnp.cumsum(onehot, axis=0)
    rank = jnp.sum(onehot * csum, axis=1) - 1
    counts = csum[-1]
    padded = (counts + MOE_BLOCK - 1) // MOE_BLOCK * MOE_BLOCK
    pend = jnp.cumsum(padded)
    pstart = pend - padded
    dest = (pstart[flat_e] + rank).astype(jnp.int32)
    token = (jnp.arange(n_assign, dtype=jnp.int32) // 2)
    rows = jnp.zeros(((n_blocks + 1) * MOE_BLOCK,), jnp.int32).at[dest].set(token, unique_indices=True)
    block_e = jnp.minimum(jnp.searchsorted(pend, jnp.arange(n_blocks, dtype=jnp.int32) * MOE_BLOCK, side="right"),
                          N_EXPERTS - 1).astype(jnp.int32)
    n_used = (pend[-1] // MOE_BLOCK).astype(jnp.int32).reshape(1)
    return dest, rows, block_e, n_used


def kernel(x_prompt, x_sample, cache_k_win, cache_v_win, state_wkv, state_shift, w_in, attn_sinks, shift_mu, w0,
           w_decay_up, a0, w_a_up, w_g_up, k_k, k_a, r_k, gn_g, gn_b, w_out, ln1_g, ln1_b, w_coarse, b_coarse,
           w_fine, b_fine, w_exp_gate, w_exp_up, w_exp_down, ln2_g, ln2_b):
    xp = x_prompt[0]
    xs = x_sample[:, 0]
    row = lambda a: a.reshape(1, -1)

    w_in0 = w_in[0]
    prm = dict(mu=row(shift_mu[0, :D_RKV]), mu_tail=row(shift_mu[0, D_RKV:]), w_tail=w_in0[:, D_MAIN:].astype(BF16),
               w0=row(w0[0]), a0=row(a0[0]), k_k=row(k_k[0]), k_a=row(k_a[0]),
               r_k=row(r_k[0]), gn_g=row(gn_g[0]), gn_b=row(gn_b[0]),
               wd=w_decay_up[0], wa=w_a_up[0], wg=w_g_up[0])
    sinks = attn_sinks[0]
    wo_bf16 = w_out[0].astype(BF16)
    w_route = jnp.pad(jnp.concatenate([w_coarse[0], w_fine[0]], axis=1), ((0, 0), (0, LANES - N_GROUPS - N_EXPERTS)))
    b_route = jnp.pad(jnp.concatenate([b_coarse[0], b_fine[0]]), (0, LANES - N_GROUPS - N_EXPERTS)).reshape(1, LANES)

    hp = _matmul(xp, w_in0, D_MAIN, 1024, MAIN_TN, "in_proj_prompt")
    hs = _matmul(xs, w_in0, D_MAIN, DEC_BATCH, MAIN_TN, "in_proj_sample")

    attn_p = _prompt_attention(hp, sinks)
    rwkv_p, state_p, tail_p = _prompt_rwkv(hp, xp, prm)

    q_s = hs[:, :D_ATTN].reshape(DEC_BATCH, N_Q_HEADS, HEAD_DIM)
    k_s = hs[:, D_ATTN:D_ATTN + D_KV].reshape(DEC_BATCH, 1, D_KV)
    v_s = hs[:, D_ATTN + D_KV:D_QKV].reshape(DEC_BATCH, 1, D_KV)
    attn_s, kwin_s, vwin_s = _sample_attention(
        q_s, k_s, v_s, cache_k_win[0].reshape(DEC_BATCH, WINDOW, D_KV),
        cache_v_win[0].reshape(DEC_BATCH, WINDOW, D_KV), sinks.reshape(N_Q_HEADS, 1))
    r_s, w_s, k2_s, vv_s, a_s, b_s, g_s, tail_s = _sample_prep(hs, xs, state_shift[0], prm)
    heads = lambda a: a.reshape(DEC_BATCH, N_RWKV_HEADS, HEAD_DIM)
    y_t, state_s = _sample_step(state_wkv[0], heads(r_s), heads(w_s), heads(k2_s), heads(a_s), heads(b_s),
                                heads(vv_s).transpose(0, 2, 1))
    rwkv_s = _sample_post(y_t.transpose(0, 2, 1).reshape(DEC_BATCH, D_RWKV), r_s, k2_s, vv_s, g_s, prm)

    n_tokens = SEQ + DEC_BATCH
    x1_pr, route_pr = _outproj_router(attn_p, rwkv_p, xp, wo_bf16, row(ln1_g[0]), row(ln1_b[0]), w_route, b_route,
                                      256, n_tokens, 0, None, "outproj_router_prompt")
    x1_all, route_all = _outproj_router(attn_s.reshape(DEC_BATCH, D_ATTN), rwkv_s, xs,
                                        wo_bf16, row(ln1_g[0]), row(ln1_b[0]), w_route, b_route,
                                        DEC_BATCH, n_tokens, SEQ // DEC_BATCH, (x1_pr, route_pr),
                                        "outproj_router_sample")

    n_assign = 2 * n_tokens
    n_blocks = -(-(n_assign + N_EXPERTS * (MOE_BLOCK - 1)) // MOE_BLOCK)
    dest, rows, block_e, n_used = _dispatch_plan(route_all, n_blocks)
    y_slots = _expert_mlp(block_e, n_used, rows.reshape(n_blocks + 1, 1, MOE_BLOCK), x1_all,
                          w_exp_gate[0], w_exp_up[0], w_exp_down[0], n_blocks)

    def dest_tiles(d):
        d = d.reshape(-1, COMBINE_TILE, 2)
        d = jnp.concatenate([d[:, :, 0], d[:, :, 1]], axis=1)
        return jnp.pad(d, ((0, 1), (0, 0))).reshape(-1, 1, 2 * COMBINE_TILE)

    y_p = _combine(dest_tiles(dest[:2 * SEQ]), y_slots, x1_all, route_all, SEQ, 0, row(ln2_g[0]), row(ln2_b[0]),
                   "combine_prompt")
    y_s = _combine(dest_tiles(dest[2 * SEQ:]), y_slots, x1_all, route_all, DEC_BATCH, SEQ // COMBINE_TILE,
                   row(ln2_g[0]), row(ln2_b[0]), "combine_sample")

    kv4 = lambda a: a.reshape(a.shape[0], N_KV_HEADS, HEAD_DIM)
    k_win_p = kv4(hp[SEQ - WINDOW:, D_ATTN:D_ATTN + D_KV])[None, None]
    v_win_p = kv4(hp[SEQ - WINDOW:, D_ATTN + D_KV:D_QKV])[None, None]
    sp = state_p.reshape(N_PAIRS, HEADS_PER_TILE, HEAD_DIM, HEADS_PER_TILE, HEAD_DIM)
    wkv_p = jnp.stack([sp[:, i, :, i, :] for i in range(HEADS_PER_TILE)], axis=1)
    wkv_p = wkv_p.reshape(N_RWKV_HEADS, HEAD_DIM, HEAD_DIM).transpose(0, 2, 1)[None, None]
    shift_p = jnp.concatenate([hp[SEQ - 1:SEQ, D_QKV:], tail_p[0:1]], axis=1)[None]
    shift_s = jnp.concatenate([hs[:, D_QKV:], tail_s], axis=1)[None]
    return (y_p[None], y_s[:, None, :], k_win_p, v_win_p, wkv_p, shift_p,
            kwin_s.reshape(1, DEC_BATCH, WINDOW, N_KV_HEADS, HEAD_DIM),
            vwin_s.reshape(1, DEC_BATCH, WINDOW, N_KV_HEADS, HEAD_DIM),
            state_s[None], shift_s)
```

```python
import functools
import math

import jax
import jax.numpy as jnp
from jax import lax
from jax.experimental import pallas as pl
from jax.experimental.pallas import tpu as pltpu

F32 = jnp.float32
BF16 = jnp.bfloat16

D_MODEL = 2048
SEQ = 8192
DEC_BATCH = 128
HEAD_DIM = 64
D_ATTN = 1024
D_RWKV = 1024
N_Q_HEADS = 16
N_KV_HEADS = 4
Q_PER_KV = 4
D_KV = 256
WINDOW = 128
ATTN_SCALE = HEAD_DIM ** -0.5
N_RWKV_HEADS = 16
W_LORA = 64
A_LORA = 64
G_LORA = 160
D_SHIFT = 3 * D_RWKV + W_LORA + A_LORA + G_LORA
D_QKV = D_ATTN + 2 * D_KV
N_GROUPS = 4
EXPERTS_PER_GROUP = 8
N_EXPERTS = 32
D_EXPERT = 512
ALPHA = 2.0 ** 0.25
LN_EPS = 1e-5
GN_EPS = 64e-5

SUBLANES = 8
LANES = 128
VMEM_LIMIT = 52 * 1024 * 1024

D_RKV = 3 * D_RWKV
D_TAIL = W_LORA + A_LORA + G_LORA
D_MAIN = D_QKV + D_RKV
MAIN_TN = 768

CHUNK = 64
HEADS_PER_TILE = LANES // HEAD_DIM
N_PAIRS = N_RWKV_HEADS // HEADS_PER_TILE
SOLVE_LEVELS = int(math.log2(CHUNK))

MOE_BLOCK = 256
ROUTE_FINE_OFF = N_GROUPS

NN = (((1,), (0,)), ((), ()))
NT = (((1,), (1,)), ((), ()))


def _dot(a, b, dims=NN):
    return lax.dot_general(a, b, dims, preferred_element_type=F32)


def _dot1(a, b, dims=NN):
    return _dot(a.astype(BF16), b.astype(BF16), dims)


def _split(x):
    hi = x.astype(BF16)
    lo = (x - hi.astype(F32)).astype(BF16)
    return hi, lo


def _dot3(a, b, dims=NN):
    ah, al = _split(a)
    bh, bl = _split(b)
    return _dot(ah, bh, dims) + (_dot(ah, bl, dims) + _dot(al, bh, dims))


def _dot_exact_lhs(a_bf16, b, dims=NN):
    bh, bl = _split(b)
    return _dot(a_bf16, bh, dims) + _dot(a_bf16, bl, dims)


def _dot_exact_rhs(a, b_bf16, dims=NN):
    ah, al = _split(a)
    return _dot(ah, b_bf16, dims) + _dot(al, b_bf16, dims)


def _div_pow2(x, d):
    return lax.shift_right_logical(x, jnp.int32(int(math.log2(d))))


def _mod_pow2(x, d):
    return lax.bitwise_and(x, jnp.int32(d - 1))


def _pack_bf16_halves(x_bf16):
    n = x_bf16.shape[1] // 2
    bits = lax.bitcast_convert_type(x_bf16.astype(F32), jnp.uint32)
    return lax.bitwise_or(bits[:, 0:n], lax.shift_right_logical(bits[:, n:2 * n], jnp.uint32(16)))


def _unpack_bf16_halves(packed):
    hi = lax.bitcast_convert_type(lax.bitwise_and(packed, jnp.uint32(0xFFFF0000)), F32)
    lo = lax.bitcast_convert_type(lax.shift_left(packed, jnp.uint32(16)), F32)
    return hi.astype(BF16), lo.astype(BF16)


def _sigmoid(x):
    return 1.0 / (1.0 + jnp.exp(-x))


def _softplus(x):
    return jnp.maximum(x, 0.0) + jnp.log(1.0 + jnp.exp(-jnp.abs(x)))


def _layer_norm(z, g, b):
    mu = jnp.mean(z, axis=-1, keepdims=True)
    d = z - mu
    var = jnp.mean(d * d, axis=-1, keepdims=True)
    return d * lax.rsqrt(var + LN_EPS) * g + b


def _cparams(sem):
    return pltpu.CompilerParams(dimension_semantics=sem, vmem_limit_bytes=VMEM_LIMIT)


def _matmul_kernel(x_ref, wt_ref, o_ref):
    o_ref[...] = _dot(x_ref[...].astype(BF16), wt_ref[...].astype(BF16), NT)


def _matmul(x, w_t, n_out, tm, tn, name):
    m, k = x.shape
    tm = min(tm, m)
    return pl.pallas_call(
        _matmul_kernel,
        out_shape=jax.ShapeDtypeStruct((m, n_out), F32),
        grid=(n_out // tn, m // tm),
        in_specs=[pl.BlockSpec((tm, k), lambda j, i: (i, 0)),
                  pl.BlockSpec((tn, k), lambda j, i: (j, 0))],
        out_specs=pl.BlockSpec((tm, tn), lambda j, i: (i, j)),
        compiler_params=_cparams(("arbitrary", "arbitrary")),
        name=name,
    )(x, w_t)


def _prompt_attn_kernel(q_ref, kvp_ref, kvc_ref, sink_ref, o_ref):
    blk = pl.program_id(0)
    q = q_ref[...]
    kv_prev = kvp_ref[...]
    kv_cur = kvc_ref[...]
    qi = _mod_pow2(lax.broadcasted_iota(jnp.int32, (Q_PER_KV * WINDOW, 2 * WINDOW), 0), WINDOW)
    kj = lax.broadcasted_iota(jnp.int32, (Q_PER_KV * WINDOW, 2 * WINDOW), 1)
    diff = qi + WINDOW - kj
    mask = (diff >= 0) & (diff <= WINDOW) & ((blk > 0) | (kj >= WINDOW))
    row_head = _div_pow2(lax.broadcasted_iota(jnp.int32, (Q_PER_KV * WINDOW, 1), 0), WINDOW)
    outs = []
    for g in range(N_KV_HEADS):
        kc = jnp.concatenate([kv_prev[:, g * HEAD_DIM:(g + 1) * HEAD_DIM],
                              kv_cur[:, g * HEAD_DIM:(g + 1) * HEAD_DIM]], axis=0).astype(BF16)
        vc = jnp.concatenate([kv_prev[:, D_KV + g * HEAD_DIM:D_KV + (g + 1) * HEAD_DIM],
                              kv_cur[:, D_KV + g * HEAD_DIM:D_KV + (g + 1) * HEAD_DIM]], axis=0).astype(BF16)
        qs = jnp.concatenate(
            [q[:, (g * Q_PER_KV + h) * HEAD_DIM:(g * Q_PER_KV + h + 1) * HEAD_DIM] for h in range(Q_PER_KV)],
            axis=0).astype(BF16)
        s = _dot(qs, kc, NT) * ATTN_SCALE
        s = jnp.where(mask, s, -jnp.inf)
        sink = jnp.zeros((Q_PER_KV * WINDOW, 1), F32)
        for h in range(Q_PER_KV):
            sink = jnp.where(row_head == h, sink_ref[g * Q_PER_KV + h], sink)
        m = jnp.maximum(jnp.max(s, axis=-1, keepdims=True), sink)
        p = jnp.exp(s - m)
        denom = jnp.sum(p, axis=-1, keepdims=True) + jnp.exp(sink - m)
        p = p / denom
        o = _dot(p.astype(BF16), vc)
        for h in range(Q_PER_KV):
            outs.append(o[h * WINDOW:(h + 1) * WINDOW, :])
    o_ref[...] = jnp.concatenate(outs, axis=1)


def _prompt_attention(h_attn, sinks):
    nb = SEQ // WINDOW
    return pl.pallas_call(
        _prompt_attn_kernel,
        out_shape=jax.ShapeDtypeStruct((SEQ, D_ATTN), F32),
        grid=(nb,),
        in_specs=[pl.BlockSpec((WINDOW, D_ATTN), lambda i: (i, 0)),
                  pl.BlockSpec((WINDOW, 2 * D_KV), lambda i: (jnp.maximum(i - 1, 0), 2)),
                  pl.BlockSpec((WINDOW, 2 * D_KV), lambda i: (i, 2)),
                  pl.BlockSpec(memory_space=pltpu.SMEM)],
        out_specs=pl.BlockSpec((WINDOW, D_ATTN), lambda i: (i, 0)),
        compiler_params=_cparams(("arbitrary",)),
        name="prompt_attention",
    )(h_attn, h_attn, h_attn, sinks)


SAMPLE_ATTN_TILE = 8


def _sample_attn_kernel(q_ref, knew_ref, vnew_ref, ck_ref, cv_ref, sink_ref, o_ref, kwin_ref, vwin_ref):
    lane = lax.broadcasted_iota(jnp.int32, (N_Q_HEADS, D_KV), 1)
    head = lax.broadcasted_iota(jnp.int32, (N_Q_HEADS, D_KV), 0)
    group_mask = _div_pow2(lane, HEAD_DIM) == _div_pow2(head, Q_PER_KV)
    sink = sink_ref[...]
    row = lax.broadcasted_iota(jnp.int32, (WINDOW, D_KV), 0)
    for b in range(SAMPLE_ATTN_TILE):
        q = q_ref[b]
        qbd = jnp.where(group_mask, jnp.concatenate([q] * N_KV_HEADS, axis=1), 0.0).astype(BF16)
        k_t = ck_ref[b]
        kb = k_t.T
        vb = cv_ref[b].T
        kn = knew_ref[b]
        vn = vnew_ref[b]
        s = _dot1(qbd, k_t) * ATTN_SCALE
        s_new = jnp.sum(qbd.astype(F32) * kn.astype(BF16).astype(F32), axis=-1, keepdims=True) * ATTN_SCALE
        m = jnp.maximum(jnp.maximum(jnp.max(s, axis=-1, keepdims=True), s_new), sink)
        p = jnp.exp(s - m)
        p_new = jnp.exp(s_new - m)
        denom = jnp.sum(p, axis=-1, keepdims=True) + p_new + jnp.exp(sink - m)
        p = p / denom
        p_new = (p_new / denom).astype(BF16).astype(F32)
        o_full = _dot1(p, vb) + p_new * vn.astype(BF16).astype(F32)
        o_full = jnp.where(group_mask, o_full, 0.0)
        o = o_full[:, 0:HEAD_DIM]
        for g in range(1, N_KV_HEADS):
            o = o + o_full[:, g * HEAD_DIM:(g + 1) * HEAD_DIM]
        o_ref[b] = o
        kwin_ref[b] = jnp.where(row == WINDOW - 1, kn, pltpu.roll(kb, WINDOW - 1, axis=0))
        vwin_ref[b] = jnp.where(row == WINDOW - 1, vn, pltpu.roll(vb, WINDOW - 1, axis=0))


def _sample_attention(q, k_new, v_new, cache_k_t, cache_v_t, sinks):
    bt = SAMPLE_ATTN_TILE
    win_spec = pl.BlockSpec((bt, WINDOW, D_KV), lambda i: (i, 0, 0))
    win_t_spec = pl.BlockSpec((bt, D_KV, WINDOW), lambda i: (i, 0, 0))
    new_spec = pl.BlockSpec((bt, 1, D_KV), lambda i: (i, 0, 0))
    return pl.pallas_call(
        _sample_attn_kernel,
        out_shape=(jax.ShapeDtypeStruct((DEC_BATCH, N_Q_HEADS, HEAD_DIM), F32),
                   jax.ShapeDtypeStruct((DEC_BATCH, WINDOW, D_KV), F32),
                   jax.ShapeDtypeStruct((DEC_BATCH, WINDOW, D_KV), F32)),
        grid=(DEC_BATCH // bt,),
        in_specs=[pl.BlockSpec((bt, N_Q_HEADS, HEAD_DIM), lambda i: (i, 0, 0)),
                  new_spec, new_spec, win_t_spec, win_t_spec,
                  pl.BlockSpec((N_Q_HEADS, 1), lambda i: (0, 0))],
        out_specs=(pl.BlockSpec((bt, N_Q_HEADS, HEAD_DIM), lambda i: (i, 0, 0)), win_spec, win_spec),
        compiler_params=_cparams(("arbitrary",)),
        name="sample_attention",
    )(q, k_new, v_new, cache_k_t, cache_v_t, sinks)


def _head_ones():
    r = _div_pow2(lax.broadcasted_iota(jnp.int32, (LANES, LANES), 0), HEAD_DIM)
    c = _div_pow2(lax.broadcasted_iota(jnp.int32, (LANES, LANES), 1), HEAD_DIM)
    return jnp.where(r == c, 1.0, 0.0).astype(BF16)


def _head_sum(x, ones):
    parts = [_dot_exact_rhs(x[:, p * LANES:(p + 1) * LANES], ones) for p in range(x.shape[1] // LANES)]
    return jnp.concatenate(parts, axis=1)


def _token_mix(feat, shifted, mu):
    return feat + (shifted - feat) * mu


def _rwkv_prep(mixed, mixed_tail, w0, a0, k_k, k_a, wd, wa, wg, ones):
    r = mixed[:, 0:D_RWKV]
    k = mixed[:, D_RWKV:2 * D_RWKV]
    v = mixed[:, 2 * D_RWKV:3 * D_RWKV]
    xw = mixed_tail[:, 0:W_LORA]
    xa = mixed_tail[:, W_LORA:W_LORA + A_LORA]
    xg = mixed_tail[:, W_LORA + A_LORA:D_TAIL]
    w_log = -_softplus(-(w0 + _dot1(jnp.tanh(xw), wd))) - 0.5
    log_decay = -jnp.exp(w_log)
    a = _sigmoid(a0 + _dot1(xa, wa))
    g = _dot1(_sigmoid(xg), wg)
    kk = k * k_k
    norm = jnp.sqrt(_head_sum(kk * kk, ones))
    kk = kk / jnp.maximum(norm, 1e-12)
    k2 = k * (1.0 + (a - 1.0) * k_a)
    return r, log_decay, k2, v, -kk, kk * a, g


def _rwkv_post(y, r, k2, v, g, r_k, gn_g, gn_b, ones):
    inv_n = 1.0 / HEAD_DIM
    mu = _head_sum(y, ones) * inv_n
    d = y - mu
    var = _head_sum(d * d, ones) * inv_n
    yn = d * lax.rsqrt(var + GN_EPS) * gn_g + gn_b
    bonus = _head_sum(r * k2 * r_k, ones) * v
    return (yn + bonus) * g


(OP_AABS, OP_RABS, OP_AN, OP_RN, OP_BN, OP_KN, OP_BH, OP_KH, OP_V) = range(9)
N_OPS = 9


def _prompt_rwkv_kernel(f1_ref, f2_ref, x_ref, wt_ref, mu_ref, mut_ref, w0_ref, a0_ref, kk_ref, ka_ref, rk_ref,
                        gng_ref, gnb_ref, wd_ref, wa_ref, wg_ref, out_ref, state_ref, tail_ref,
                        prev_ref, prevt_ref, s_ref, ops_ref, pc_ref, y_ref):
    c = pl.program_id(0)
    C = CHUNK

    @pl.when(c == 0)
    def _():
        prev_ref[...] = jnp.zeros_like(prev_ref)
        prevt_ref[...] = jnp.zeros_like(prevt_ref)
        s_ref[...] = jnp.zeros_like(s_ref)

    ones = _head_ones()
    row = lax.broadcasted_iota(jnp.int32, (C, 1), 0)

    def token_shift(feat, carry_ref):
        shifted = jnp.where(row == 0, carry_ref[0:1, :], pltpu.roll(feat, 1, axis=0))
        carry_ref[0:1, :] = feat[C - 1:C, :]
        return shifted

    feat = jnp.concatenate([f1_ref[...], f2_ref[...]], axis=1)
    tail = _dot1(x_ref[...], wt_ref[...], NT)
    mixed = _token_mix(feat, token_shift(feat, prev_ref), mu_ref[...])
    mixed_tail = _token_mix(tail, token_shift(tail, prevt_ref), mut_ref[...])
    tail_ref[...] = prevt_ref[...]
    r, ld, k2, v, av, bv, g = _rwkv_prep(mixed, mixed_tail, w0_ref[...], a0_ref[...], kk_ref[...], ka_ref[...],
                                         wd_ref[...], wa_ref[...], wg_ref[...], ones)

    ti = lax.broadcasted_iota(jnp.int32, (C, C), 0)
    tj = lax.broadcasted_iota(jnp.int32, (C, C), 1)
    tri_incl = jnp.where(tj <= ti, 1.0, 0.0).astype(BF16)
    cs = _dot_exact_lhs(tri_incl, ld)
    cs_ref = cs[C // 2 - 1:C // 2, :]
    cs_end = cs[C - 1:C, :]
    e_prev = jnp.exp(cs - ld)
    e_cur = jnp.exp(cs)
    n_prev = jnp.exp(cs - ld - cs_ref)
    n_cur = jnp.exp(cs - cs_ref)
    n_inv = jnp.exp(cs_ref - cs)
    e_tail = jnp.exp(cs_end - cs)
    ops = {OP_AABS: av * e_prev, OP_RABS: r * e_cur, OP_AN: av * n_prev, OP_RN: r * n_cur,
           OP_BN: bv * n_inv, OP_KN: k2 * n_inv, OP_BH: bv * e_tail, OP_KH: k2 * e_tail, OP_V: v}
    p_end = jnp.exp(cs_end)
    for p in range(N_PAIRS):
        sl = slice(p * LANES, (p + 1) * LANES)
        for idx, val in ops.items():
            ops_ref[p, idx] = val[:, sl]
        pc_ref[p] = jnp.broadcast_to(p_end[:, sl], (SUBLANES, LANES))

    lane1 = lax.broadcasted_iota(jnp.int32, (C, LANES), 1)
    head0 = lane1 < HEAD_DIM
    r2 = lax.broadcasted_iota(jnp.int32, (2 * C, 2 * C), 0)
    c2 = lax.broadcasted_iota(jnp.int32, (2 * C, 2 * C), 1)
    tq = _mod_pow2(r2, C)
    tk = _mod_pow2(c2, C)
    band = (tk < tq) | ((tk == tq) & (r2 >= C))
    blockdiag = _div_pow2(r2, HEAD_DIM) == _div_pow2(c2, HEAD_DIM)

    pairs = range(N_PAIRS)
    op = lambda p, idx: ops_ref[p, idx]

    zero_half = jnp.zeros((C, LANES), F32)

    gy = [_dot1(jnp.concatenate([op(p, OP_AABS), op(p, OP_RABS)], axis=0), s_ref[p]) for p in pairs]

    am0, am1 = [], []
    for p in pairs:
        a_n, r_n = op(p, OP_AN), op(p, OP_RN)
        b0, k0 = jnp.where(head0, op(p, OP_BN), 0.0), jnp.where(head0, op(p, OP_KN), 0.0)
        b1, k1 = jnp.where(head0, 0.0, op(p, OP_BN)), jnp.where(head0, 0.0, op(p, OP_KN))
        am = _dot1(jnp.concatenate([a_n, r_n], axis=0), jnp.concatenate([k0, b0, b1, k1], axis=0), NT)
        am0.append(jnp.where(band, am[:, 0:2 * C], 0.0))
        am1.append(jnp.where(band, am[:, 2 * C:4 * C], 0.0))

    w0, w1 = [], []
    for p in pairs:
        top0, top1 = am0[p][0:C], am1[p][0:C]
        ak = jnp.concatenate([jnp.where(head0, top0, 0.0), jnp.where(head0, 0.0, top1)], axis=0)
        vv = op(p, OP_V)
        g0 = gy[p][0:C]
        m = jnp.concatenate([g0, g0], axis=0) + _dot1(ak, jnp.concatenate([vv, vv], axis=0))
        w0.append(jnp.where(head0, m[0:C], top0))
        w1.append(jnp.where(head0, top1, m[C:2 * C]))

    for lvl in range(SOLVE_LEVELS):
        prod0 = [_dot3(w0[p], jnp.concatenate([zero_half, w0[p]], axis=0)) for p in pairs]
        prod1 = [_dot3(w1[p], jnp.concatenate([w1[p], zero_half], axis=0)) for p in pairs]
        w0 = [jnp.where(head0, w0[p] + prod0[p], prod0[p]) for p in pairs]
        w1 = [jnp.where(head0, prod1[p], w1[p] + prod1[p]) for p in pairs]
    u = [jnp.where(head0, w0[p], w1[p]) for p in pairs]

    for p in pairs:
        vv = op(p, OP_V)
        y_lhs = jnp.concatenate([am0[p][C:2 * C], am1[p][C:2 * C]], axis=1)
        y_rhs = jnp.concatenate([jnp.where(head0, vv, 0.0), jnp.where(head0, u[p], 0.0),
                                 jnp.where(head0, 0.0, u[p]), jnp.where(head0, 0.0, vv)], axis=0)
        y_ref[p] = gy[p][C:2 * C] + _dot1(y_lhs, y_rhs)

    for p in pairs:
        decay_rows = jnp.broadcast_to(pc_ref[p][0:1, :], (LANES, LANES)).T
        upd_lhs = jnp.concatenate([op(p, OP_BH), op(p, OP_KH)], axis=0).T
        upd_rhs = jnp.concatenate([u[p], op(p, OP_V)], axis=0)
        s_ref[p] = s_ref[p] * decay_rows + jnp.where(blockdiag, _dot1(upd_lhs, upd_rhs), 0.0)

    y = jnp.concatenate([y_ref[p] for p in range(N_PAIRS)], axis=1)
    out_ref[...] = _rwkv_post(y, r, k2, v, g, rk_ref[...], gng_ref[...], gnb_ref[...], ones)

    @pl.when(c == pl.num_programs(0) - 1)
    def _():
        state_ref[...] = s_ref[...]


def _prompt_rwkv(h_main, x, prm):
    n_chunks = SEQ // CHUNK
    half = D_RKV // 2
    assert D_QKV == half
    vec = pl.BlockSpec((1, D_RWKV), lambda c: (0, 0))
    full = lambda a: pl.BlockSpec(a.shape, lambda c: (0,) * a.ndim)
    return pl.pallas_call(
        _prompt_rwkv_kernel,
        out_shape=(jax.ShapeDtypeStruct((SEQ, D_RWKV), F32),
                   jax.ShapeDtypeStruct((N_PAIRS, LANES, LANES), F32),
                   jax.ShapeDtypeStruct((SUBLANES, D_TAIL), F32)),
        grid=(n_chunks,),
        in_specs=[pl.BlockSpec((CHUNK, half), lambda c: (c, 1)),
                  pl.BlockSpec((CHUNK, half), lambda c: (c, 2)),
                  pl.BlockSpec((CHUNK, D_MODEL), lambda c: (c, 0)),
                  full(prm["w_tail"]), full(prm["mu"]), full(prm["mu_tail"]),
                  vec, vec, vec, vec, vec, vec, vec,
                  full(prm["wd"]), full(prm["wa"]), full(prm["wg"])],
        out_specs=(pl.BlockSpec((CHUNK, D_RWKV), lambda c: (c, 0)),
                   pl.BlockSpec((N_PAIRS, LANES, LANES), lambda c: (0, 0, 0)),
                   pl.BlockSpec((SUBLANES, D_TAIL), lambda c: (0, 0))),
        scratch_shapes=[pltpu.VMEM((SUBLANES, D_RKV), F32),
                        pltpu.VMEM((SUBLANES, D_TAIL), F32),
                        pltpu.VMEM((N_PAIRS, LANES, LANES), F32),
                        pltpu.VMEM((N_PAIRS, N_OPS, CHUNK, LANES), F32),
                        pltpu.VMEM((N_PAIRS, SUBLANES, LANES), F32),
                        pltpu.VMEM((N_PAIRS, CHUNK, LANES), F32)],
        compiler_params=_cparams(("arbitrary",)),
        name="prompt_rwkv",
    )(h_main, h_main, x, prm["w_tail"], prm["mu"], prm["mu_tail"], prm["w0"], prm["a0"], prm["k_k"], prm["k_a"],
      prm["r_k"], prm["gn_g"], prm["gn_b"], prm["wd"], prm["wa"], prm["wg"])


def _sample_prep_kernel(h_ref, x_ref, wt_ref, shift_ref, mu_ref, mut_ref, w0_ref, a0_ref, kk_ref, ka_ref,
                        wd_ref, wa_ref, wg_ref, r_ref, k_ref, v_ref, g_ref, tail_ref,
                        rt_ref, wtr_ref, kt_ref, vt_ref, at_ref, bt_ref):
    ones = _head_ones()
    feat = h_ref[:, D_QKV:D_MAIN]
    tail = _dot1(x_ref[...], wt_ref[...], NT)
    tail_ref[...] = tail
    mixed = _token_mix(feat, shift_ref[:, 0:D_RKV], mu_ref[...])
    mixed_tail = _token_mix(tail, shift_ref[:, D_RKV:D_SHIFT], mut_ref[...])
    r, ld, k2, v, av, bv, g = _rwkv_prep(mixed, mixed_tail, w0_ref[...], a0_ref[...], kk_ref[...], ka_ref[...],
                                         wd_ref[...], wa_ref[...], wg_ref[...], ones)
    r_ref[...] = r
    k_ref[...] = k2
    v_ref[...] = v
    g_ref[...] = g
    rt_ref[...] = r.T
    wtr_ref[...] = jnp.exp(ld).T
    kt_ref[...] = k2.T
    vt_ref[...] = v.T
    at_ref[...] = av.T
    bt_ref[...] = bv.T


def _sample_prep(h_main, x, shift, prm):
    tok = jax.ShapeDtypeStruct((DEC_BATCH, D_RWKV), F32)
    chan = jax.ShapeDtypeStruct((D_RWKV, DEC_BATCH), F32)
    return pl.pallas_call(
        _sample_prep_kernel,
        out_shape=(tok,) * 4 + (jax.ShapeDtypeStruct((DEC_BATCH, D_TAIL), F32),) + (chan,) * 6,
        compiler_params=pltpu.CompilerParams(vmem_limit_bytes=VMEM_LIMIT),
        name="sample_rwkv_prep",
    )(h_main, x, prm["w_tail"], shift, prm["mu"], prm["mu_tail"], prm["w0"], prm["a0"], prm["k_k"], prm["k_a"],
      prm["wd"], prm["wa"], prm["wg"])


STEP_GROUP = 4


def _sample_step_kernel(s_ref, r_ref, w_ref, k_ref, a_ref, b_ref, v_ref, y_ref, snew_ref):
    r, w, k, a, b = r_ref[...], w_ref[...], k_ref[...], a_ref[...], b_ref[...]
    for g0 in range(0, HEAD_DIM, 2 * STEP_GROUP):
        chans = range(g0, g0 + 2 * STEP_GROUP)
        sa = {i: jnp.sum(s_ref[0, i] * a, axis=0, keepdims=True) for i in chans}
        s_new = {i: s_ref[0, i] * w + sa[i] * b + v_ref[i:i + 1, :] * k for i in chans}
        for i in chans:
            y_ref[i:i + 1, :] = jnp.sum(s_new[i] * r, axis=0, keepdims=True)
        for i in range(g0, g0 + 2 * STEP_GROUP, 2):
            pair = jnp.concatenate([s_new[i], s_new[i + 1]], axis=0)
            snew_ref[:, i * HEAD_DIM:(i + 2) * HEAD_DIM] = pair.T


def _sample_step(state_t, r_t, w_t, k_t, a_t, b_t, v_t):
    head_rows = pl.BlockSpec((HEAD_DIM, DEC_BATCH), lambda h: (h, 0))
    return pl.pallas_call(
        _sample_step_kernel,
        out_shape=(jax.ShapeDtypeStruct((D_RWKV, DEC_BATCH), F32),
                   jax.ShapeDtypeStruct((DEC_BATCH, N_RWKV_HEADS * HEAD_DIM * HEAD_DIM), F32)),
        grid=(N_RWKV_HEADS,),
        in_specs=[pl.BlockSpec((1, HEAD_DIM, HEAD_DIM, DEC_BATCH), lambda h: (h, 0, 0, 0))] + [head_rows] * 6,
        out_specs=(head_rows, pl.BlockSpec((DEC_BATCH, HEAD_DIM * HEAD_DIM), lambda h: (0, h))),
        compiler_params=_cparams(("arbitrary",)),
        name="sample_rwkv_step",
    )(state_t, r_t, w_t, k_t, a_t, b_t, v_t)


def _sample_post_kernel(yt_ref, r_ref, k_ref, v_ref, g_ref, rk_ref, gng_ref, gnb_ref, o_ref):
    o_ref[...] = _rwkv_post(yt_ref[...].T, r_ref[...], k_ref[...], v_ref[...], g_ref[...], rk_ref[...],
                            gng_ref[...], gnb_ref[...], _head_ones())


def _sample_post(y, r, k, v, g, prm):
    return pl.pallas_call(
        _sample_post_kernel,
        out_shape=jax.ShapeDtypeStruct((DEC_BATCH, D_RWKV), F32),
        compiler_params=pltpu.CompilerParams(vmem_limit_bytes=VMEM_LIMIT),
        name="sample_rwkv_post",
    )(y, r, k, v, g, prm["r_k"], prm["gn_g"], prm["gn_b"])


def _outproj_router_kernel(attn_ref, rwkv_ref, x_ref, wo_ref, g_ref, b_ref, wr_ref, br_ref,
                           x1_ref, x1b_ref, route_ref):
    mix = _dot(attn_ref[...].astype(BF16), wo_ref[0:D_ATTN, :]) + _dot(rwkv_ref[...].astype(BF16),
                                                                        wo_ref[D_ATTN:D_ATTN + D_RWKV, :])
    x1 = _layer_norm(ALPHA * x_ref[...] + mix, g_ref[...], b_ref[...])
    x1_ref[...] = x1
    x1b = x1.astype(BF16)
    x1b_ref[...] = _pack_bf16_halves(x1b)
    logits = _dot(x1b, wr_ref[...].astype(BF16)) + br_ref[...]
    tm = logits.shape[0]
    lane = lax.broadcasted_iota(jnp.int32, (tm, LANES), 1).astype(F32)
    big = float(2 * LANES)
    neg = -jnp.inf
    lc = jnp.where(lane < N_GROUPS, logits, neg)
    mc = jnp.max(lc, axis=-1, keepdims=True)
    g_sel = jnp.min(jnp.where(lc == mc, lane, big), axis=-1, keepdims=True)
    p_group = 1.0 / jnp.sum(jnp.exp(lc - mc), axis=-1, keepdims=True)
    lo = ROUTE_FINE_OFF + g_sel * EXPERTS_PER_GROUP
    lf = jnp.where((lane >= lo) & (lane < lo + EXPERTS_PER_GROUP), logits, neg)
    v1 = jnp.max(lf, axis=-1, keepdims=True)
    i1 = jnp.min(jnp.where(lf == v1, lane, big), axis=-1, keepdims=True)
    lf2 = jnp.where(lane == i1, neg, lf)
    v2 = jnp.max(lf2, axis=-1, keepdims=True)
    i2 = jnp.min(jnp.where(lf2 == v2, lane, big), axis=-1, keepdims=True)
    e21 = jnp.exp(v2 - v1)
    gate1 = p_group / (1.0 + e21)
    gate2 = p_group * e21 / (1.0 + e21)
    route = jnp.where(lane == 0, i1 - ROUTE_FINE_OFF,
                      jnp.where(lane == 1, i2 - ROUTE_FINE_OFF,
                                jnp.where(lane == 2, gate1, jnp.where(lane == 3, gate2, 0.0))))
    route_ref[...] = route


N_ROUTER_OUTS = 3


def _outproj_router_into_kernel(*refs):
    _outproj_router_kernel(*refs[:-2 * N_ROUTER_OUTS], *refs[-N_ROUTER_OUTS:])


def _outproj_router_fill_kernel(n_steps, *refs):
    @pl.when(pl.program_id(0) < n_steps)
    def _():
        _outproj_router_kernel(*refs)

    @pl.when(pl.program_id(0) >= n_steps)
    def _():
        for out_ref in refs[-N_ROUTER_OUTS:]:
            out_ref[...] = jnp.zeros_like(out_ref)


def _outproj_router(attn, rwkv, x, wo_bf16, ln_g, ln_b, w_route, b_route, tm, n_total, row_block, into, name):
    m = x.shape[0]
    n_steps = m // tm
    const = lambda shape: pl.BlockSpec(shape, lambda i: (0, 0))
    rows = lambda width: pl.BlockSpec((tm, width), lambda i: (jnp.minimum(i, n_steps - 1), 0))
    in_specs = [rows(D_ATTN), rows(D_RWKV), rows(D_MODEL),
                const((D_MODEL, D_MODEL)), const((1, D_MODEL)), const((1, D_MODEL)),
                const((D_MODEL, LANES)), const((1, LANES))]
    args = [attn, rwkv, x, wo_bf16, ln_g, ln_b, w_route, b_route]
    aliases = {}
    if into is not None:
        in_specs += [pl.BlockSpec(memory_space=pl.ANY)] * N_ROUTER_OUTS
        aliases = {len(args) + k: k for k in range(N_ROUTER_OUTS)}
        args += list(into)
        body, grid_steps = _outproj_router_into_kernel, n_steps
    else:
        body, grid_steps = functools.partial(_outproj_router_fill_kernel, n_steps), pl.cdiv(n_total, tm)
    out_rows = lambda width: pl.BlockSpec((tm, width), lambda i: (i + row_block, 0))
    return pl.pallas_call(
        body,
        out_shape=(jax.ShapeDtypeStruct((n_total, D_MODEL), F32),
                   jax.ShapeDtypeStruct((n_total, D_MODEL // 2), jnp.uint32),
                   jax.ShapeDtypeStruct((n_total, LANES), F32)),
        grid=(grid_steps,),
        in_specs=in_specs,
        out_specs=(out_rows(D_MODEL), out_rows(D_MODEL // 2), out_rows(LANES)),
        input_output_aliases=aliases,
        compiler_params=_cparams(("arbitrary",)),
        name=name,
    )(*args)


DISPATCH_TILE = 128


def _dispatch_kernel(zoff_ref, dest_ref, x_hbm, o_hbm, zbuf, zsem, sem):
    i = pl.program_id(0)
    n_rows = 2 * DISPATCH_TILE
    n_blocks = o_hbm.shape[0] // MOE_BLOCK
    n_used = zoff_ref[N_EXPERTS]

    def zero_fill(start_row):
        start_row = pl.multiple_of(start_row, MOE_BLOCK)
        return pltpu.make_async_copy(zbuf, o_hbm.at[pl.ds(start_row, MOE_BLOCK)], zsem)

    def zero_fills(action):
        for e in range(N_EXPERTS):
            @pl.when(zoff_ref[e] >= 0)
            def _():
                action(zero_fill(zoff_ref[e]))
        for b in range(n_blocks):
            @pl.when(b >= n_used)
            def _():
                action(zero_fill(b * MOE_BLOCK))

    @pl.when(i == 0)
    def _():
        zbuf[...] = jnp.zeros_like(zbuf)
        zero_fills(lambda copy: copy.start())
        zero_fills(lambda copy: copy.wait())

    for t in range(DISPATCH_TILE):
        src = x_hbm.at[pl.ds(i * DISPATCH_TILE + t, 1)]
        for k in range(2):
            pltpu.make_async_copy(src, o_hbm.at[pl.ds(dest_ref[0, 0, 2 * t + k], 1)], sem).start()
    pltpu.make_async_copy(x_hbm.at[pl.ds(0, n_rows)], o_hbm.at[pl.ds(0, n_rows)], sem).wait()


def _dispatch(zero_offsets, dest, x_packed, n_blocks):
    n_tokens, width = x_packed.shape
    grid_spec = pltpu.PrefetchScalarGridSpec(
        num_scalar_prefetch=1,
        grid=(n_tokens // DISPATCH_TILE,),
        in_specs=[pl.BlockSpec((1, 1, 2 * DISPATCH_TILE), lambda i, z: (i, 0, 0), memory_space=pltpu.SMEM),
                  pl.BlockSpec(memory_space=pl.ANY)],
        out_specs=pl.BlockSpec(memory_space=pl.ANY),
        scratch_shapes=[pltpu.VMEM((MOE_BLOCK, width), x_packed.dtype), pltpu.SemaphoreType.DMA,
                        pltpu.SemaphoreType.DMA],
    )
    return pl.pallas_call(
        _dispatch_kernel,
        out_shape=jax.ShapeDtypeStruct((n_blocks * MOE_BLOCK, width), x_packed.dtype),
        grid_spec=grid_spec,
        compiler_params=_cparams(("arbitrary",)),
        name="moe_dispatch",
    )(zero_offsets, dest.reshape(-1, 1, 2 * DISPATCH_TILE), x_packed)


def _expert_kernel(be_ref, nb_ref, x_ref, wg_ref, wu_ref, wd_ref, o_ref):
    blk = pl.program_id(0)

    @pl.when(blk < nb_ref[0])
    def _():
        half = D_MODEL // 2
        x_head, x_tail = _unpack_bf16_halves(x_ref[...])
        proj = lambda w_ref: (_dot(x_head, w_ref[0, 0:half, :].astype(BF16))
                              + _dot(x_tail, w_ref[0, half:D_MODEL, :].astype(BF16)))
        gate = proj(wg_ref)
        up = proj(wu_ref)
        h = gate * _sigmoid(gate) * up
        o_ref[...] = _dot(h.astype(BF16), wd_ref[0].astype(BF16))

    @pl.when(blk >= nb_ref[0])
    def _():
        o_ref[...] = jnp.zeros_like(o_ref)


def _expert_mlp(block_expert, n_used, x_sorted, w_gate, w_up, w_down, n_blocks):
    grid_spec = pltpu.PrefetchScalarGridSpec(
        num_scalar_prefetch=2,
        grid=(n_blocks,),
        in_specs=[pl.BlockSpec((MOE_BLOCK, D_MODEL // 2), lambda b, be, nb: (jnp.minimum(b, nb[0] - 1), 0)),
                  pl.BlockSpec((1, D_MODEL, D_EXPERT), lambda b, be, nb: (be[b], 0, 0)),
                  pl.BlockSpec((1, D_MODEL, D_EXPERT), lambda b, be, nb: (be[b], 0, 0)),
                  pl.BlockSpec((1, D_EXPERT, D_MODEL), lambda b, be, nb: (be[b], 0, 0))],
        out_specs=pl.BlockSpec((MOE_BLOCK, D_MODEL), lambda b, be, nb: (b, 0)),
    )
    return pl.pallas_call(
        _expert_kernel,
        out_shape=jax.ShapeDtypeStruct((n_blocks * MOE_BLOCK, D_MODEL), F32),
        grid_spec=grid_spec,
        compiler_params=_cparams(("arbitrary",)),
        name="expert_mlp",
    )(block_expert, n_used, x_sorted, w_gate, w_up, w_down)


COMBINE_TILE = 128


def _combine_kernel(dest_ref, dest_next_ref, y_hbm, x1_ref, route_ref, g_ref, b_ref, o_ref, ybuf, sem):
    i = pl.program_id(0)
    cur = lax.rem(i, 2)
    n_rows = 2 * COMBINE_TILE

    def gather(table_ref, buf):
        for slot in range(n_rows):
            pltpu.make_async_copy(y_hbm.at[pl.ds(table_ref[0, 0, slot], 1)], ybuf.at[buf, pl.ds(slot, 1)],
                                  sem.at[buf]).start()

    def wait_gather(buf):
        pltpu.make_async_copy(y_hbm.at[pl.ds(0, n_rows)], ybuf.at[buf], sem.at[buf]).wait()

    @pl.when(i == 0)
    def _():
        gather(dest_ref, 0)

    gather(dest_next_ref, 1 - cur)
    wait_gather(cur)
    route = route_ref[...]
    yb = ybuf[cur]
    moe = route[:, 2:3] * yb[0:COMBINE_TILE, :] + route[:, 3:4] * yb[COMBINE_TILE:n_rows, :]
    o_ref[...] = _layer_norm(ALPHA * x1_ref[...] + moe, g_ref[...], b_ref[...])

    @pl.when(i == pl.num_programs(0) - 1)
    def _():
        wait_gather(1 - cur)


def _combine(dest, y_slots, x1_all, route_all, m, row_block, ln_g, ln_b, name):
    tm = COMBINE_TILE
    return pl.pallas_call(
        _combine_kernel,
        out_shape=jax.ShapeDtypeStruct((m, D_MODEL), F32),
        grid=(m // tm,),
        in_specs=[pl.BlockSpec((1, 1, 2 * tm), lambda i: (i, 0, 0), memory_space=pltpu.SMEM),
                  pl.BlockSpec((1, 1, 2 * tm), lambda i: (i + 1, 0, 0), memory_space=pltpu.SMEM),
                  pl.BlockSpec(memory_space=pl.ANY),
                  pl.BlockSpec((tm, D_MODEL), lambda i: (i + row_block, 0)),
                  pl.BlockSpec((tm, LANES), lambda i: (i + row_block, 0)),
                  pl.BlockSpec((1, D_MODEL), lambda i: (0, 0)),
                  pl.BlockSpec((1, D_MODEL), lambda i: (0, 0))],
        out_specs=pl.BlockSpec((tm, D_MODEL), lambda i: (i, 0)),
        scratch_shapes=[pltpu.VMEM((2, 2 * tm, D_MODEL), F32), pltpu.SemaphoreType.DMA((2,))],
        compiler_params=_cparams(("arbitrary",)),
        name=name,
    )(dest, dest, y_slots, x1_all, route_all, ln_g, ln_b)


def _dispatch_plan(route_all, n_blocks):
    flat_e = route_all[:, 0:2].astype(jnp.int32).reshape(-1)
    onehot = (flat_e[:, None] == jnp.arange(N_EXPERTS, dtype=jnp.int32)[None, :]).astype(jnp.int32)
    csum = jnp.cumsum(onehot, axis=0)
    rank = jnp.sum(onehot * csum, axis=1) - 1
    counts = csum[-1]
    padded = (counts + MOE_BLOCK - 1) // MOE_BLOCK * MOE_BLOCK
    pend = jnp.cumsum(padded)
    pstart = pend - padded
    dest = (pstart[flat_e] + rank).astype(jnp.int32)
    zero_offsets = jnp.where(counts > 0, pend - MOE_BLOCK, -1).astype(jnp.int32)
    n_used = (pend[-1] // MOE_BLOCK).astype(jnp.int32)
    block_start = jnp.minimum(jnp.arange(n_blocks, dtype=jnp.int32), n_used - 1) * MOE_BLOCK
    block_e = jnp.minimum(jnp.searchsorted(pend, block_start, side="right"), N_EXPERTS - 1).astype(jnp.int32)
    return dest, jnp.concatenate([zero_offsets, n_used.reshape(1)]), block_e, n_used.reshape(1)


def kernel(x_prompt, x_sample, cache_k_win, cache_v_win, state_wkv, state_shift, w_in, attn_sinks, shift_mu, w0,
           w_decay_up, a0, w_a_up, w_g_up, k_k, k_a, r_k, gn_g, gn_b, w_out, ln1_g, ln1_b, w_coarse, b_coarse,
           w_fine, b_fine, w_exp_gate, w_exp_up, w_exp_down, ln2_g, ln2_b):
    xp = x_prompt[0]
    xs = x_sample[:, 0]
    row = lambda a: a.reshape(1, -1)

    w_in_t = jnp.swapaxes(w_in[0], 0, 1)
    prm = dict(mu=row(shift_mu[0, :D_RKV]), mu_tail=row(shift_mu[0, D_RKV:]), w_tail=w_in_t[D_MAIN:].astype(BF16),
               w0=row(w0[0]), a0=row(a0[0]), k_k=row(k_k[0]), k_a=row(k_a[0]),
               r_k=row(r_k[0]), gn_g=row(gn_g[0]), gn_b=row(gn_b[0]),
               wd=w_decay_up[0], wa=w_a_up[0], wg=w_g_up[0])
    sinks = attn_sinks[0]
    wo_bf16 = w_out[0].astype(BF16)
    w_route = jnp.pad(jnp.concatenate([w_coarse[0], w_fine[0]], axis=1), ((0, 0), (0, LANES - N_GROUPS - N_EXPERTS)))
    b_route = jnp.pad(jnp.concatenate([b_coarse[0], b_fine[0]]), (0, LANES - N_GROUPS - N_EXPERTS)).reshape(1, LANES)

    hp = _matmul(xp, w_in_t, D_MAIN, 1024, MAIN_TN, "in_proj_prompt")
    hs = _matmul(xs, w_in_t, D_MAIN, DEC_BATCH, MAIN_TN, "in_proj_sample")

    attn_p = _prompt_attention(hp, sinks)
    rwkv_p, state_p, tail_p = _prompt_rwkv(hp, xp, prm)

    q_s = hs[:, :D_ATTN].reshape(DEC_BATCH, N_Q_HEADS, HEAD_DIM)
    k_s = hs[:, D_ATTN:D_ATTN + D_KV].reshape(DEC_BATCH, 1, D_KV)
    v_s = hs[:, D_ATTN + D_KV:D_QKV].reshape(DEC_BATCH, 1, D_KV)
    window_t = lambda c: jnp.transpose(c, (0, 2, 3, 1)).reshape(DEC_BATCH, D_KV, WINDOW)
    attn_s, kwin_s, vwin_s = _sample_attention(
        q_s, k_s, v_s, window_t(cache_k_win[0]), window_t(cache_v_win[0]), sinks.reshape(N_Q_HEADS, 1))
    r_s, k2_s, vv_s, g_s, tail_s, r_t, w_t, k_t, v_t, a_t, b_t = _sample_prep(hs, xs, state_shift[0], prm)
    y_t, state_s = _sample_step(jnp.transpose(state_wkv[0], (1, 2, 3, 0)), r_t, w_t, k_t, a_t, b_t, v_t)
    state_s = state_s.reshape(DEC_BATCH, N_RWKV_HEADS, HEAD_DIM, HEAD_DIM)
    rwkv_s = _sample_post(y_t, r_s, k2_s, vv_s, g_s, prm)

    n_tokens = SEQ + DEC_BATCH
    outs_pr = _outproj_router(attn_p, rwkv_p, xp, wo_bf16, row(ln1_g[0]), row(ln1_b[0]), w_route, b_route,
                              256, n_tokens, 0, None, "outproj_router_prompt")
    x1_all, x1b_all, route_all = _outproj_router(attn_s.reshape(DEC_BATCH, D_ATTN), rwkv_s, xs,
                                                 wo_bf16, row(ln1_g[0]), row(ln1_b[0]), w_route, b_route,
                                                 DEC_BATCH, n_tokens, SEQ // DEC_BATCH, outs_pr,
                                                 "outproj_router_sample")

    n_assign = 2 * n_tokens
    n_blocks = -(-(n_assign + N_EXPERTS * (MOE_BLOCK - 1)) // MOE_BLOCK)
    dest, zero_offsets, block_e, n_used = _dispatch_plan(route_all, n_blocks)
    x_sorted = _dispatch(zero_offsets, dest, x1b_all, n_blocks)
    y_slots = _expert_mlp(block_e, n_used, x_sorted, w_exp_gate[0], w_exp_up[0], w_exp_down[0], n_blocks)

    def dest_tiles(d):
        d = d.reshape(-1, COMBINE_TILE, 2)
        d = jnp.concatenate([d[:, :, 0], d[:, :, 1]], axis=1)
        return jnp.pad(d, ((0, 1), (0, 0))).reshape(-1, 1, 2 * COMBINE_TILE)

    y_p = _combine(dest_tiles(dest[:2 * SEQ]), y_slots, x1_all, route_all, SEQ, 0, row(ln2_g[0]), row(ln2_b[0]),
                   "combine_prompt")
    y_s = _combine(dest_tiles(dest[2 * SEQ:]), y_slots, x1_all, route_all, DEC_BATCH, SEQ // COMBINE_TILE,
                   row(ln2_g[0]), row(ln2_b[0]), "combine_sample")

    kv4 = lambda a: a.reshape(a.shape[0], N_KV_HEADS, HEAD_DIM)
    k_win_p = kv4(hp[SEQ - WINDOW:, D_ATTN:D_ATTN + D_KV])[None, None]
    v_win_p = kv4(hp[SEQ - WINDOW:, D_ATTN + D_KV:D_QKV])[None, None]
    sp = state_p.reshape(N_PAIRS, HEADS_PER_TILE, HEAD_DIM, HEADS_PER_TILE, HEAD_DIM)
    wkv_p = jnp.stack([sp[:, i, :, i, :] for i in range(HEADS_PER_TILE)], axis=1)
    wkv_p = wkv_p.reshape(N_RWKV_HEADS, HEAD_DIM, HEAD_DIM).transpose(0, 2, 1)[None, None]
    shift_p = jnp.concatenate([hp[SEQ - 1:SEQ, D_QKV:], tail_p[0:1]], axis=1)[None]
    shift_s = jnp.concatenate([hs[:, D_QKV:], tail_s], axis=1)[None]
    return (y_p[None], y_s[:, None, :], k_win_p, v_win_p, wkv_p, shift_p,
            kwin_s.reshape(1, DEC_BATCH, WINDOW, N_KV_HEADS, HEAD_DIM),
            vwin_s.reshape(1, DEC_BATCH, WINDOW, N_KV_HEADS, HEAD_DIM),
            state_s[None], shift_s)
```

```python
import functools
import math

import jax
import jax.numpy as jnp
from jax import lax
from jax.experimental import pallas as pl
from jax.experimental.pallas import tpu as pltpu

F32 = jnp.float32
BF16 = jnp.bfloat16

D_MODEL = 2048
SEQ = 8192
DEC_BATCH = 128
HEAD_DIM = 64
D_ATTN = 1024
D_RWKV = 1024
N_Q_HEADS = 16
N_KV_HEADS = 4
Q_PER_KV = 4
D_KV = 256
WINDOW = 128
ATTN_SCALE = HEAD_DIM ** -0.5
N_RWKV_HEADS = 16
W_LORA = 64
A_LORA = 64
G_LORA = 160
D_SHIFT = 3 * D_RWKV + W_LORA + A_LORA + G_LORA
D_QKV = D_ATTN + 2 * D_KV
N_GROUPS = 4
EXPERTS_PER_GROUP = 8
N_EXPERTS = 32
D_EXPERT = 512
ALPHA = 2.0 ** 0.25
LN_EPS = 1e-5
GN_EPS = 64e-5

SUBLANES = 8
LANES = 128
VMEM_LIMIT = 52 * 1024 * 1024

D_RKV = 3 * D_RWKV
D_TAIL = W_LORA + A_LORA + G_LORA
D_MAIN = D_QKV + D_RKV
MAIN_TN = 768

CHUNK = 64
HEADS_PER_TILE = LANES // HEAD_DIM
N_PAIRS = N_RWKV_HEADS // HEADS_PER_TILE
SOLVE_LEVELS = int(math.log2(CHUNK))

MOE_BLOCK = 256
ROUTE_FINE_OFF = N_GROUPS

NN = (((1,), (0,)), ((), ()))
NT = (((1,), (1,)), ((), ()))


def _dot(a, b, dims=NN):
    return lax.dot_general(a, b, dims, preferred_element_type=F32)


def _dot1(a, b, dims=NN):
    return _dot(a.astype(BF16), b.astype(BF16), dims)


def _split(x):
    hi = x.astype(BF16)
    lo = (x - hi.astype(F32)).astype(BF16)
    return hi, lo


def _dot3(a, b, dims=NN):
    ah, al = _split(a)
    bh, bl = _split(b)
    return _dot(ah, bh, dims) + (_dot(ah, bl, dims) + _dot(al, bh, dims))


def _dot_exact_lhs(a_bf16, b, dims=NN):
    bh, bl = _split(b)
    return _dot(a_bf16, bh, dims) + _dot(a_bf16, bl, dims)


def _dot_exact_rhs(a, b_bf16, dims=NN):
    ah, al = _split(a)
    return _dot(ah, b_bf16, dims) + _dot(al, b_bf16, dims)


def _div_pow2(x, d):
    return lax.shift_right_logical(x, jnp.int32(int(math.log2(d))))


def _mod_pow2(x, d):
    return lax.bitwise_and(x, jnp.int32(d - 1))


def _pack_bf16_halves(x_bf16):
    n = x_bf16.shape[1] // 2
    bits = lax.bitcast_convert_type(x_bf16.astype(F32), jnp.uint32)
    return lax.bitwise_or(bits[:, 0:n], lax.shift_right_logical(bits[:, n:2 * n], jnp.uint32(16)))


def _unpack_bf16_halves(packed):
    hi = lax.bitcast_convert_type(lax.bitwise_and(packed, jnp.uint32(0xFFFF0000)), F32)
    lo = lax.bitcast_convert_type(lax.shift_left(packed, jnp.uint32(16)), F32)
    return hi.astype(BF16), lo.astype(BF16)


def _sigmoid(x):
    return 1.0 / (1.0 + jnp.exp(-x))


def _softplus(x):
    return jnp.maximum(x, 0.0) + jnp.log(1.0 + jnp.exp(-jnp.abs(x)))


def _layer_norm(z, g, b):
    mu = jnp.mean(z, axis=-1, keepdims=True)
    d = z - mu
    var = jnp.mean(d * d, axis=-1, keepdims=True)
    return d * lax.rsqrt(var + LN_EPS) * g + b


def _cparams(sem):
    return pltpu.CompilerParams(dimension_semantics=sem, vmem_limit_bytes=VMEM_LIMIT)


def _matmul_kernel(x_ref, wt_ref, o_ref):
    o_ref[...] = _dot(x_ref[...].astype(BF16), wt_ref[...].astype(BF16), NT)


def _matmul(x, w_t, n_out, tm, tn, name):
    m, k = x.shape
    tm = min(tm, m)
    return pl.pallas_call(
        _matmul_kernel,
        out_shape=jax.ShapeDtypeStruct((m, n_out), F32),
        grid=(n_out // tn, m // tm),
        in_specs=[pl.BlockSpec((tm, k), lambda j, i: (i, 0)),
                  pl.BlockSpec((tn, k), lambda j, i: (j, 0))],
        out_specs=pl.BlockSpec((tm, tn), lambda j, i: (i, j)),
        compiler_params=_cparams(("arbitrary", "arbitrary")),
        name=name,
    )(x, w_t)


def _prompt_attn_kernel(q_ref, kvp_ref, kvc_ref, sink_ref, o_ref):
    blk = pl.program_id(0)
    q = q_ref[...]
    kv_prev = kvp_ref[...]
    kv_cur = kvc_ref[...]
    qi = _mod_pow2(lax.broadcasted_iota(jnp.int32, (Q_PER_KV * WINDOW, 2 * WINDOW), 0), WINDOW)
    kj = lax.broadcasted_iota(jnp.int32, (Q_PER_KV * WINDOW, 2 * WINDOW), 1)
    diff = qi + WINDOW - kj
    mask = (diff >= 0) & (diff <= WINDOW) & ((blk > 0) | (kj >= WINDOW))
    row_head = _div_pow2(lax.broadcasted_iota(jnp.int32, (Q_PER_KV * WINDOW, 1), 0), WINDOW)
    outs = []
    for g in range(N_KV_HEADS):
        kc = jnp.concatenate([kv_prev[:, g * HEAD_DIM:(g + 1) * HEAD_DIM],
                              kv_cur[:, g * HEAD_DIM:(g + 1) * HEAD_DIM]], axis=0).astype(BF16)
        vc = jnp.concatenate([kv_prev[:, D_KV + g * HEAD_DIM:D_KV + (g + 1) * HEAD_DIM],
                              kv_cur[:, D_KV + g * HEAD_DIM:D_KV + (g + 1) * HEAD_DIM]], axis=0).astype(BF16)
        qs = jnp.concatenate(
            [q[:, (g * Q_PER_KV + h) * HEAD_DIM:(g * Q_PER_KV + h + 1) * HEAD_DIM] for h in range(Q_PER_KV)],
            axis=0).astype(BF16)
        s = _dot(qs, kc, NT) * ATTN_SCALE
        s = jnp.where(mask, s, -jnp.inf)
        sink = jnp.zeros((Q_PER_KV * WINDOW, 1), F32)
        for h in range(Q_PER_KV):
            sink = jnp.where(row_head == h, sink_ref[g * Q_PER_KV + h], sink)
        m = jnp.maximum(jnp.max(s, axis=-1, keepdims=True), sink)
        p = jnp.exp(s - m)
        denom = jnp.sum(p, axis=-1, keepdims=True) + jnp.exp(sink - m)
        p = p / denom
        o = _dot(p.astype(BF16), vc)
        for h in range(Q_PER_KV):
            outs.append(o[h * WINDOW:(h + 1) * WINDOW, :])
    o_ref[...] = jnp.concatenate(outs, axis=1)


def _prompt_attention(h_attn, sinks):
    nb = SEQ // WINDOW
    return pl.pallas_call(
        _prompt_attn_kernel,
        out_shape=jax.ShapeDtypeStruct((SEQ, D_ATTN), F32),
        grid=(nb,),
        in_specs=[pl.BlockSpec((WINDOW, D_ATTN), lambda i: (i, 0)),
                  pl.BlockSpec((WINDOW, 2 * D_KV), lambda i: (jnp.maximum(i - 1, 0), 2)),
                  pl.BlockSpec((WINDOW, 2 * D_KV), lambda i: (i, 2)),
                  pl.BlockSpec(memory_space=pltpu.SMEM)],
        out_specs=pl.BlockSpec((WINDOW, D_ATTN), lambda i: (i, 0)),
        compiler_params=_cparams(("arbitrary",)),
        name="prompt_attention",
    )(h_attn, h_attn, h_attn, sinks)


SAMPLE_ATTN_TILE = 8


def _sample_attn_kernel(q_ref, knew_ref, vnew_ref, ck_ref, cv_ref, sink_ref, o_ref, kwin_ref, vwin_ref):
    lane = lax.broadcasted_iota(jnp.int32, (N_Q_HEADS, D_KV), 1)
    head = lax.broadcasted_iota(jnp.int32, (N_Q_HEADS, D_KV), 0)
    group_mask = _div_pow2(lane, HEAD_DIM) == _div_pow2(head, Q_PER_KV)
    sink = sink_ref[...]
    row = lax.broadcasted_iota(jnp.int32, (WINDOW, D_KV), 0)
    for b in range(SAMPLE_ATTN_TILE):
        q = q_ref[b]
        qbd = jnp.where(group_mask, jnp.concatenate([q] * N_KV_HEADS, axis=1), 0.0).astype(BF16)
        k_t = ck_ref[b]
        kb = k_t.T
        vb = cv_ref[b].T
        kn = knew_ref[b]
        vn = vnew_ref[b]
        s = _dot1(qbd, k_t) * ATTN_SCALE
        s_new = jnp.sum(qbd.astype(F32) * kn.astype(BF16).astype(F32), axis=-1, keepdims=True) * ATTN_SCALE
        m = jnp.maximum(jnp.maximum(jnp.max(s, axis=-1, keepdims=True), s_new), sink)
        p = jnp.exp(s - m)
        p_new = jnp.exp(s_new - m)
        denom = jnp.sum(p, axis=-1, keepdims=True) + p_new + jnp.exp(sink - m)
        p = p / denom
        p_new = (p_new / denom).astype(BF16).astype(F32)
        o_full = _dot1(p, vb) + p_new * vn.astype(BF16).astype(F32)
        o_full = jnp.where(group_mask, o_full, 0.0)
        o = o_full[:, 0:HEAD_DIM]
        for g in range(1, N_KV_HEADS):
            o = o + o_full[:, g * HEAD_DIM:(g + 1) * HEAD_DIM]
        o_ref[b] = o
        kwin_ref[b] = jnp.where(row == WINDOW - 1, kn, pltpu.roll(kb, WINDOW - 1, axis=0))
        vwin_ref[b] = jnp.where(row == WINDOW - 1, vn, pltpu.roll(vb, WINDOW - 1, axis=0))


def _sample_attention(q, k_new, v_new, cache_k_t, cache_v_t, sinks):
    bt = SAMPLE_ATTN_TILE
    win_spec = pl.BlockSpec((bt, WINDOW, D_KV), lambda i: (i, 0, 0))
    win_t_spec = pl.BlockSpec((bt, D_KV, WINDOW), lambda i: (i, 0, 0))
    new_spec = pl.BlockSpec((bt, 1, D_KV), lambda i: (i, 0, 0))
    return pl.pallas_call(
        _sample_attn_kernel,
        out_shape=(jax.ShapeDtypeStruct((DEC_BATCH, N_Q_HEADS, HEAD_DIM), F32),
                   jax.ShapeDtypeStruct((DEC_BATCH, WINDOW, D_KV), F32),
                   jax.ShapeDtypeStruct((DEC_BATCH, WINDOW, D_KV), F32)),
        grid=(DEC_BATCH // bt,),
        in_specs=[pl.BlockSpec((bt, N_Q_HEADS, HEAD_DIM), lambda i: (i, 0, 0)),
                  new_spec, new_spec, win_t_spec, win_t_spec,
                  pl.BlockSpec((N_Q_HEADS, 1), lambda i: (0, 0))],
        out_specs=(pl.BlockSpec((bt, N_Q_HEADS, HEAD_DIM), lambda i: (i, 0, 0)), win_spec, win_spec),
        compiler_params=_cparams(("arbitrary",)),
        name="sample_attention",
    )(q, k_new, v_new, cache_k_t, cache_v_t, sinks)


def _head_ones():
    r = _div_pow2(lax.broadcasted_iota(jnp.int32, (LANES, LANES), 0), HEAD_DIM)
    c = _div_pow2(lax.broadcasted_iota(jnp.int32, (LANES, LANES), 1), HEAD_DIM)
    return jnp.where(r == c, 1.0, 0.0).astype(BF16)


def _head_sum(x, ones):
    parts = [_dot_exact_rhs(x[:, p * LANES:(p + 1) * LANES], ones) for p in range(x.shape[1] // LANES)]
    return jnp.concatenate(parts, axis=1)


def _token_mix(feat, shifted, mu):
    return feat + (shifted - feat) * mu


def _rwkv_prep(mixed, mixed_tail, w0, a0, k_k, k_a, wd, wa, wg, ones):
    r = mixed[:, 0:D_RWKV]
    k = mixed[:, D_RWKV:2 * D_RWKV]
    v = mixed[:, 2 * D_RWKV:3 * D_RWKV]
    xw = mixed_tail[:, 0:W_LORA]
    xa = mixed_tail[:, W_LORA:W_LORA + A_LORA]
    xg = mixed_tail[:, W_LORA + A_LORA:D_TAIL]
    w_log = -_softplus(-(w0 + _dot1(jnp.tanh(xw), wd))) - 0.5
    log_decay = -jnp.exp(w_log)
    a = _sigmoid(a0 + _dot1(xa, wa))
    g = _dot1(_sigmoid(xg), wg)
    kk = k * k_k
    norm = jnp.sqrt(_head_sum(kk * kk, ones))
    kk = kk / jnp.maximum(norm, 1e-12)
    k2 = k * (1.0 + (a - 1.0) * k_a)
    return r, log_decay, k2, v, -kk, kk * a, g


def _rwkv_post(y, r, k2, v, g, r_k, gn_g, gn_b, ones):
    inv_n = 1.0 / HEAD_DIM
    mu = _head_sum(y, ones) * inv_n
    d = y - mu
    var = _head_sum(d * d, ones) * inv_n
    yn = d * lax.rsqrt(var + GN_EPS) * gn_g + gn_b
    bonus = _head_sum(r * k2 * r_k, ones) * v
    return (yn + bonus) * g


(OP_AABS, OP_RABS, OP_AN, OP_RN, OP_BN, OP_KN, OP_BH, OP_KH, OP_V) = range(9)
N_OPS = 9


def _prompt_rwkv_kernel(f1_ref, f2_ref, x_ref, wt_ref, mu_ref, mut_ref, w0_ref, a0_ref, kk_ref, ka_ref, rk_ref,
                        gng_ref, gnb_ref, wd_ref, wa_ref, wg_ref, out_ref, state_ref, tail_ref,
                        prev_ref, prevt_ref, s_ref, ops_ref, pc_ref, y_ref):
    c = pl.program_id(0)
    C = CHUNK

    @pl.when(c == 0)
    def _():
        prev_ref[...] = jnp.zeros_like(prev_ref)
        prevt_ref[...] = jnp.zeros_like(prevt_ref)
        s_ref[...] = jnp.zeros_like(s_ref)

    ones = _head_ones()
    row = lax.broadcasted_iota(jnp.int32, (C, 1), 0)

    def token_shift(feat, carry_ref):
        shifted = jnp.where(row == 0, carry_ref[0:1, :], pltpu.roll(feat, 1, axis=0))
        carry_ref[0:1, :] = feat[C - 1:C, :]
        return shifted

    feat = jnp.concatenate([f1_ref[...], f2_ref[...]], axis=1)
    tail = _dot1(x_ref[...], wt_ref[...], NT)
    mixed = _token_mix(feat, token_shift(feat, prev_ref), mu_ref[...])
    mixed_tail = _token_mix(tail, token_shift(tail, prevt_ref), mut_ref[...])
    tail_ref[...] = prevt_ref[...]
    r, ld, k2, v, av, bv, g = _rwkv_prep(mixed, mixed_tail, w0_ref[...], a0_ref[...], kk_ref[...], ka_ref[...],
                                         wd_ref[...], wa_ref[...], wg_ref[...], ones)

    ti = lax.broadcasted_iota(jnp.int32, (C, C), 0)
    tj = lax.broadcasted_iota(jnp.int32, (C, C), 1)
    tri_incl = jnp.where(tj <= ti, 1.0, 0.0).astype(BF16)
    cs = _dot_exact_lhs(tri_incl, ld)
    cs_ref = cs[C // 2 - 1:C // 2, :]
    cs_end = cs[C - 1:C, :]
    e_prev = jnp.exp(cs - ld)
    e_cur = jnp.exp(cs)
    n_prev = jnp.exp(cs - ld - cs_ref)
    n_cur = jnp.exp(cs - cs_ref)
    n_inv = jnp.exp(cs_ref - cs)
    e_tail = jnp.exp(cs_end - cs)
    ops = {OP_AABS: av * e_prev, OP_RABS: r * e_cur, OP_AN: av * n_prev, OP_RN: r * n_cur,
           OP_BN: bv * n_inv, OP_KN: k2 * n_inv, OP_BH: bv * e_tail, OP_KH: k2 * e_tail, OP_V: v}
    p_end = jnp.exp(cs_end)
    for p in range(N_PAIRS):
        sl = slice(p * LANES, (p + 1) * LANES)
        for idx, val in ops.items():
            ops_ref[p, idx] = val[:, sl]
        pc_ref[p] = jnp.broadcast_to(p_end[:, sl], (SUBLANES, LANES))

    lane1 = lax.broadcasted_iota(jnp.int32, (C, LANES), 1)
    head0 = lane1 < HEAD_DIM
    r2 = lax.broadcasted_iota(jnp.int32, (2 * C, 2 * C), 0)
    c2 = lax.broadcasted_iota(jnp.int32, (2 * C, 2 * C), 1)
    tq = _mod_pow2(r2, C)
    tk = _mod_pow2(c2, C)
    band = (tk < tq) | ((tk == tq) & (r2 >= C))
    blockdiag = _div_pow2(r2, HEAD_DIM) == _div_pow2(c2, HEAD_DIM)

    pairs = range(N_PAIRS)
    op = lambda p, idx: ops_ref[p, idx]

    zero_half = jnp.zeros((C, LANES), F32)

    gy = [_dot1(jnp.concatenate([op(p, OP_AABS), op(p, OP_RABS)], axis=0), s_ref[p]) for p in pairs]

    am0, am1 = [], []
    for p in pairs:
        a_n, r_n = op(p, OP_AN), op(p, OP_RN)
        b0, k0 = jnp.where(head0, op(p, OP_BN), 0.0), jnp.where(head0, op(p, OP_KN), 0.0)
        b1, k1 = jnp.where(head0, 0.0, op(p, OP_BN)), jnp.where(head0, 0.0, op(p, OP_KN))
        am = _dot1(jnp.concatenate([a_n, r_n], axis=0), jnp.concatenate([k0, b0, b1, k1], axis=0), NT)
        am0.append(jnp.where(band, am[:, 0:2 * C], 0.0))
        am1.append(jnp.where(band, am[:, 2 * C:4 * C], 0.0))

    w0, w1 = [], []
    for p in pairs:
        top0, top1 = am0[p][0:C], am1[p][0:C]
        ak = jnp.concatenate([jnp.where(head0, top0, 0.0), jnp.where(head0, 0.0, top1)], axis=0)
        vv = op(p, OP_V)
        g0 = gy[p][0:C]
        m = jnp.concatenate([g0, g0], axis=0) + _dot1(ak, jnp.concatenate([vv, vv], axis=0))
        w0.append(jnp.where(head0, m[0:C], top0))
        w1.append(jnp.where(head0, top1, m[C:2 * C]))

    for lvl in range(SOLVE_LEVELS):
        prod0 = [_dot3(w0[p], jnp.concatenate([zero_half, w0[p]], axis=0)) for p in pairs]
        prod1 = [_dot3(w1[p], jnp.concatenate([w1[p], zero_half], axis=0)) for p in pairs]
        w0 = [jnp.where(head0, w0[p] + prod0[p], prod0[p]) for p in pairs]
        w1 = [jnp.where(head0, prod1[p], w1[p] + prod1[p]) for p in pairs]
    u = [jnp.where(head0, w0[p], w1[p]) for p in pairs]

    for p in pairs:
        vv = op(p, OP_V)
        y_lhs = jnp.concatenate([am0[p][C:2 * C], am1[p][C:2 * C]], axis=1)
        y_rhs = jnp.concatenate([jnp.where(head0, vv, 0.0), jnp.where(head0, u[p], 0.0),
                                 jnp.where(head0, 0.0, u[p]), jnp.where(head0, 0.0, vv)], axis=0)
        y_ref[p] = gy[p][C:2 * C] + _dot1(y_lhs, y_rhs)

    for p in pairs:
        decay_rows = jnp.broadcast_to(pc_ref[p][0:1, :], (LANES, LANES)).T
        upd_lhs = jnp.concatenate([op(p, OP_BH), op(p, OP_KH)], axis=0).T
        upd_rhs = jnp.concatenate([u[p], op(p, OP_V)], axis=0)
        s_ref[p] = s_ref[p] * decay_rows + jnp.where(blockdiag, _dot1(upd_lhs, upd_rhs), 0.0)

    y = jnp.concatenate([y_ref[p] for p in range(N_PAIRS)], axis=1)
    out_ref[...] = _rwkv_post(y, r, k2, v, g, rk_ref[...], gng_ref[...], gnb_ref[...], ones)

    @pl.when(c == pl.num_programs(0) - 1)
    def _():
        state_ref[...] = s_ref[...]


def _prompt_rwkv(h_main, x, prm):
    n_chunks = SEQ // CHUNK
    half = D_RKV // 2
    assert D_QKV == half
    vec = pl.BlockSpec((1, D_RWKV), lambda c: (0, 0))
    full = lambda a: pl.BlockSpec(a.shape, lambda c: (0,) * a.ndim)
    return pl.pallas_call(
        _prompt_rwkv_kernel,
        out_shape=(jax.ShapeDtypeStruct((SEQ, D_RWKV), F32),
                   jax.ShapeDtypeStruct((N_PAIRS, LANES, LANES), F32),
                   jax.ShapeDtypeStruct((SUBLANES, D_TAIL), F32)),
        grid=(n_chunks,),
        in_specs=[pl.BlockSpec((CHUNK, half), lambda c: (c, 1)),
                  pl.BlockSpec((CHUNK, half), lambda c: (c, 2)),
                  pl.BlockSpec((CHUNK, D_MODEL), lambda c: (c, 0)),
                  full(prm["w_tail"]), full(prm["mu"]), full(prm["mu_tail"]),
                  vec, vec, vec, vec, vec, vec, vec,
                  full(prm["wd"]), full(prm["wa"]), full(prm["wg"])],
        out_specs=(pl.BlockSpec((CHUNK, D_RWKV), lambda c: (c, 0)),
                   pl.BlockSpec((N_PAIRS, LANES, LANES), lambda c: (0, 0, 0)),
                   pl.BlockSpec((SUBLANES, D_TAIL), lambda c: (0, 0))),
        scratch_shapes=[pltpu.VMEM((SUBLANES, D_RKV), F32),
                        pltpu.VMEM((SUBLANES, D_TAIL), F32),
                        pltpu.VMEM((N_PAIRS, LANES, LANES), F32),
                        pltpu.VMEM((N_PAIRS, N_OPS, CHUNK, LANES), F32),
                        pltpu.VMEM((N_PAIRS, SUBLANES, LANES), F32),
                        pltpu.VMEM((N_PAIRS, CHUNK, LANES), F32)],
        compiler_params=_cparams(("arbitrary",)),
        name="prompt_rwkv",
    )(h_main, h_main, x, prm["w_tail"], prm["mu"], prm["mu_tail"], prm["w0"], prm["a0"], prm["k_k"], prm["k_a"],
      prm["r_k"], prm["gn_g"], prm["gn_b"], prm["wd"], prm["wa"], prm["wg"])


def _sample_prep_kernel(h_ref, x_ref, wt_ref, shift_ref, mu_ref, mut_ref, w0_ref, a0_ref, kk_ref, ka_ref,
                        wd_ref, wa_ref, wg_ref, r_ref, k_ref, v_ref, g_ref, tail_ref,
                        rt_ref, wtr_ref, kt_ref, vt_ref, at_ref, bt_ref):
    ones = _head_ones()
    feat = h_ref[:, D_QKV:D_MAIN]
    tail = _dot1(x_ref[...], wt_ref[...], NT)
    tail_ref[...] = tail
    mixed = _token_mix(feat, shift_ref[:, 0:D_RKV], mu_ref[...])
    mixed_tail = _token_mix(tail, shift_ref[:, D_RKV:D_SHIFT], mut_ref[...])
    r, ld, k2, v, av, bv, g = _rwkv_prep(mixed, mixed_tail, w0_ref[...], a0_ref[...], kk_ref[...], ka_ref[...],
                                         wd_ref[...], wa_ref[...], wg_ref[...], ones)
    r_ref[...] = r
    k_ref[...] = k2
    v_ref[...] = v
    g_ref[...] = g
    rt_ref[...] = r.T
    wtr_ref[...] = jnp.exp(ld).T
    kt_ref[...] = k2.T
    vt_ref[...] = v.T
    at_ref[...] = av.T
    bt_ref[...] = bv.T


def _sample_prep(h_main, x, shift, prm):
    tok = jax.ShapeDtypeStruct((DEC_BATCH, D_RWKV), F32)
    chan = jax.ShapeDtypeStruct((D_RWKV, DEC_BATCH), F32)
    return pl.pallas_call(
        _sample_prep_kernel,
        out_shape=(tok,) * 4 + (jax.ShapeDtypeStruct((DEC_BATCH, D_TAIL), F32),) + (chan,) * 6,
        compiler_params=pltpu.CompilerParams(vmem_limit_bytes=VMEM_LIMIT),
        name="sample_rwkv_prep",
    )(h_main, x, prm["w_tail"], shift, prm["mu"], prm["mu_tail"], prm["w0"], prm["a0"], prm["k_k"], prm["k_a"],
      prm["wd"], prm["wa"], prm["wg"])


STEP_GROUP = 4


def _sample_step_kernel(s_ref, r_ref, w_ref, k_ref, a_ref, b_ref, v_ref, y_ref, snew_ref):
    r, w, k, a, b = r_ref[...], w_ref[...], k_ref[...], a_ref[...], b_ref[...]
    for g0 in range(0, HEAD_DIM, 2 * STEP_GROUP):
        chans = range(g0, g0 + 2 * STEP_GROUP)
        sa = {i: jnp.sum(s_ref[0, i] * a, axis=0, keepdims=True) for i in chans}
        s_new = {i: s_ref[0, i] * w + sa[i] * b + v_ref[i:i + 1, :] * k for i in chans}
        for i in chans:
            y_ref[i:i + 1, :] = jnp.sum(s_new[i] * r, axis=0, keepdims=True)
        for i in range(g0, g0 + 2 * STEP_GROUP, 2):
            pair = jnp.concatenate([s_new[i], s_new[i + 1]], axis=0)
            snew_ref[:, i * HEAD_DIM:(i + 2) * HEAD_DIM] = pair.T


def _sample_step(state_t, r_t, w_t, k_t, a_t, b_t, v_t):
    head_rows = pl.BlockSpec((HEAD_DIM, DEC_BATCH), lambda h: (h, 0))
    return pl.pallas_call(
        _sample_step_kernel,
        out_shape=(jax.ShapeDtypeStruct((D_RWKV, DEC_BATCH), F32),
                   jax.ShapeDtypeStruct((DEC_BATCH, N_RWKV_HEADS * HEAD_DIM * HEAD_DIM), F32)),
        grid=(N_RWKV_HEADS,),
        in_specs=[pl.BlockSpec((1, HEAD_DIM, HEAD_DIM, DEC_BATCH), lambda h: (h, 0, 0, 0))] + [head_rows] * 6,
        out_specs=(head_rows, pl.BlockSpec((DEC_BATCH, HEAD_DIM * HEAD_DIM), lambda h: (0, h))),
        compiler_params=_cparams(("arbitrary",)),
        name="sample_rwkv_step",
    )(state_t, r_t, w_t, k_t, a_t, b_t, v_t)


def _sample_post_kernel(yt_ref, r_ref, k_ref, v_ref, g_ref, rk_ref, gng_ref, gnb_ref, o_ref):
    o_ref[...] = _rwkv_post(yt_ref[...].T, r_ref[...], k_ref[...], v_ref[...], g_ref[...], rk_ref[...],
                            gng_ref[...], gnb_ref[...], _head_ones())


def _sample_post(y, r, k, v, g, prm):
    return pl.pallas_call(
        _sample_post_kernel,
        out_shape=jax.ShapeDtypeStruct((DEC_BATCH, D_RWKV), F32),
        compiler_params=pltpu.CompilerParams(vmem_limit_bytes=VMEM_LIMIT),
        name="sample_rwkv_post",
    )(y, r, k, v, g, prm["r_k"], prm["gn_g"], prm["gn_b"])


def _outproj_router_kernel(attn_ref, rwkv_ref, x_ref, wo_ref, g_ref, b_ref, wr_ref, br_ref,
                           x1_ref, x1b_ref, route_ref):
    mix = _dot(attn_ref[...].astype(BF16), wo_ref[0:D_ATTN, :]) + _dot(rwkv_ref[...].astype(BF16),
                                                                        wo_ref[D_ATTN:D_ATTN + D_RWKV, :])
    x1 = _layer_norm(ALPHA * x_ref[...] + mix, g_ref[...], b_ref[...])
    x1_ref[...] = x1
    x1b = x1.astype(BF16)
    x1b_ref[...] = _pack_bf16_halves(x1b)
    logits = _dot(x1b, wr_ref[...].astype(BF16)) + br_ref[...]
    tm = logits.shape[0]
    lane = lax.broadcasted_iota(jnp.int32, (tm, LANES), 1).astype(F32)
    big = float(2 * LANES)
    neg = -jnp.inf
    lc = jnp.where(lane < N_GROUPS, logits, neg)
    mc = jnp.max(lc, axis=-1, keepdims=True)
    g_sel = jnp.min(jnp.where(lc == mc, lane, big), axis=-1, keepdims=True)
    p_group = 1.0 / jnp.sum(jnp.exp(lc - mc), axis=-1, keepdims=True)
    lo = ROUTE_FINE_OFF + g_sel * EXPERTS_PER_GROUP
    lf = jnp.where((lane >= lo) & (lane < lo + EXPERTS_PER_GROUP), logits, neg)
    v1 = jnp.max(lf, axis=-1, keepdims=True)
    i1 = jnp.min(jnp.where(lf == v1, lane, big), axis=-1, keepdims=True)
    lf2 = jnp.where(lane == i1, neg, lf)
    v2 = jnp.max(lf2, axis=-1, keepdims=True)
    i2 = jnp.min(jnp.where(lf2 == v2, lane, big), axis=-1, keepdims=True)
    e21 = jnp.exp(v2 - v1)
    gate1 = p_group / (1.0 + e21)
    gate2 = p_group * e21 / (1.0 + e21)
    route = jnp.where(lane == 0, i1 - ROUTE_FINE_OFF,
                      jnp.where(lane == 1, i2 - ROUTE_FINE_OFF,
                                jnp.where(lane == 2, gate1, jnp.where(lane == 3, gate2, 0.0))))
    route_ref[...] = route


N_ROUTER_OUTS = 3


def _outproj_router_into_kernel(*refs):
    _outproj_router_kernel(*refs[:-2 * N_ROUTER_OUTS], *refs[-N_ROUTER_OUTS:])


def _outproj_router_fill_kernel(n_steps, *refs):
    @pl.when(pl.program_id(0) < n_steps)
    def _():
        _outproj_router_kernel(*refs)

    @pl.when(pl.program_id(0) >= n_steps)
    def _():
        for out_ref in refs[-N_ROUTER_OUTS:]:
            out_ref[...] = jnp.zeros_like(out_ref)


def _outproj_router(attn, rwkv, x, wo_bf16, ln_g, ln_b, w_route, b_route, tm, n_total, row_block, into, name):
    m = x.shape[0]
    n_steps = m // tm
    const = lambda shape: pl.BlockSpec(shape, lambda i: (0, 0))
    rows = lambda width: pl.BlockSpec((tm, width), lambda i: (jnp.minimum(i, n_steps - 1), 0))
    in_specs = [rows(D_ATTN), rows(D_RWKV), rows(D_MODEL),
                const((D_MODEL, D_MODEL)), const((1, D_MODEL)), const((1, D_MODEL)),
                const((D_MODEL, LANES)), const((1, LANES))]
    args = [attn, rwkv, x, wo_bf16, ln_g, ln_b, w_route, b_route]
    aliases = {}
    if into is not None:
        in_specs += [pl.BlockSpec(memory_space=pl.ANY)] * N_ROUTER_OUTS
        aliases = {len(args) + k: k for k in range(N_ROUTER_OUTS)}
        args += list(into)
        body, grid_steps = _outproj_router_into_kernel, n_steps
    else:
        body, grid_steps = functools.partial(_outproj_router_fill_kernel, n_steps), pl.cdiv(n_total, tm)
    out_rows = lambda width: pl.BlockSpec((tm, width), lambda i: (i + row_block, 0))
    return pl.pallas_call(
        body,
        out_shape=(jax.ShapeDtypeStruct((n_total, D_MODEL), F32),
                   jax.ShapeDtypeStruct((n_total, D_MODEL // 2), jnp.uint32),
                   jax.ShapeDtypeStruct((n_total, LANES), F32)),
        grid=(grid_steps,),
        in_specs=in_specs,
        out_specs=(out_rows(D_MODEL), out_rows(D_MODEL // 2), out_rows(LANES)),
        input_output_aliases=aliases,
        compiler_params=_cparams(("arbitrary",)),
        name=name,
    )(*args)


DISPATCH_TILE = 128


def _dispatch_kernel(zoff_ref, dest_ref, x_ref, o_hbm, zbuf, zsem, sem):
    i = pl.program_id(0)
    n_blocks = o_hbm.shape[0] // MOE_BLOCK
    n_used = zoff_ref[N_EXPERTS]

    def zero_fill(start_row):
        start_row = pl.multiple_of(start_row, MOE_BLOCK)
        return pltpu.make_async_copy(zbuf, o_hbm.at[pl.ds(start_row, MOE_BLOCK)], zsem)

    def zero_fills(action):
        for e in range(N_EXPERTS):
            @pl.when(zoff_ref[e] >= 0)
            def _():
                action(zero_fill(zoff_ref[e]))
        for b in range(n_blocks):
            @pl.when(b >= n_used)
            def _():
                action(zero_fill(b * MOE_BLOCK))

    @pl.when(i == 0)
    def _():
        zbuf[...] = jnp.zeros_like(zbuf)
        zero_fills(lambda copy: copy.start())
        zero_fills(lambda copy: copy.wait())

    for t in range(DISPATCH_TILE):
        for k in range(2):
            pltpu.make_async_copy(x_ref.at[pl.ds(t, 1)], o_hbm.at[pl.ds(dest_ref[0, 0, 2 * t + k], 1)], sem).start()
    for k in range(2):
        pltpu.make_async_copy(x_ref, o_hbm.at[pl.ds(0, DISPATCH_TILE)], sem).wait()


def _dispatch(zero_offsets, dest, x_packed, n_blocks):
    n_tokens, width = x_packed.shape
    grid_spec = pltpu.PrefetchScalarGridSpec(
        num_scalar_prefetch=1,
        grid=(n_tokens // DISPATCH_TILE,),
        in_specs=[pl.BlockSpec((1, 1, 2 * DISPATCH_TILE), lambda i, z: (i, 0, 0), memory_space=pltpu.SMEM),
                  pl.BlockSpec((DISPATCH_TILE, width), lambda i, z: (i, 0))],
        out_specs=pl.BlockSpec(memory_space=pl.ANY),
        scratch_shapes=[pltpu.VMEM((MOE_BLOCK, width), x_packed.dtype), pltpu.SemaphoreType.DMA,
                        pltpu.SemaphoreType.DMA],
    )
    return pl.pallas_call(
        _dispatch_kernel,
        out_shape=jax.ShapeDtypeStruct((n_blocks * MOE_BLOCK, width), x_packed.dtype),
        grid_spec=grid_spec,
        compiler_params=_cparams(("arbitrary",)),
        name="moe_dispatch",
    )(zero_offsets, dest.reshape(-1, 1, 2 * DISPATCH_TILE), x_packed)


def _expert_kernel(be_ref, nb_ref, x_ref, wg_ref, wu_ref, wd_ref, o_ref):
    blk = pl.program_id(0)

    @pl.when(blk < nb_ref[0])
    def _():
        half = D_MODEL // 2
        x_head, x_tail = _unpack_bf16_halves(x_ref[...])
        proj = lambda w_ref: (_dot(x_head, w_ref[0, 0:half, :].astype(BF16))
                              + _dot(x_tail, w_ref[0, half:D_MODEL, :].astype(BF16)))
        gate = proj(wg_ref)
        up = proj(wu_ref)
        h = gate * _sigmoid(gate) * up
        o_ref[...] = _dot(h.astype(BF16), wd_ref[0].astype(BF16))

    @pl.when(blk >= nb_ref[0])
    def _():
        o_ref[...] = jnp.zeros_like(o_ref)


def _expert_mlp(block_expert, n_used, x_sorted, w_gate, w_up, w_down, n_blocks):
    grid_spec = pltpu.PrefetchScalarGridSpec(
        num_scalar_prefetch=2,
        grid=(n_blocks,),
        in_specs=[pl.BlockSpec((MOE_BLOCK, D_MODEL // 2), lambda b, be, nb: (jnp.minimum(b, nb[0] - 1), 0)),
                  pl.BlockSpec((1, D_MODEL, D_EXPERT), lambda b, be, nb: (be[b], 0, 0)),
                  pl.BlockSpec((1, D_MODEL, D_EXPERT), lambda b, be, nb: (be[b], 0, 0)),
                  pl.BlockSpec((1, D_EXPERT, D_MODEL), lambda b, be, nb: (be[b], 0, 0))],
        out_specs=pl.BlockSpec((MOE_BLOCK, D_MODEL), lambda b, be, nb: (b, 0)),
    )
    return pl.pallas_call(
        _expert_kernel,
        out_shape=jax.ShapeDtypeStruct((n_blocks * MOE_BLOCK, D_MODEL), F32),
        grid_spec=grid_spec,
        compiler_params=_cparams(("arbitrary",)),
        name="expert_mlp",
    )(block_expert, n_used, x_sorted, w_gate, w_up, w_down)


COMBINE_TILE = 128


def _combine_kernel(dest_ref, dest_next_ref, y_hbm, x1_ref, route_ref, g_ref, b_ref, o_ref, ybuf, sem):
    i = pl.program_id(0)
    cur = lax.rem(i, 2)
    n_rows = 2 * COMBINE_TILE

    def gather(table_ref, buf):
        for slot in range(n_rows):
            pltpu.make_async_copy(y_hbm.at[pl.ds(table_ref[0, 0, slot], 1)], ybuf.at[buf, pl.ds(slot, 1)],
                                  sem.at[buf]).start()

    def wait_gather(buf):
        pltpu.make_async_copy(y_hbm.at[pl.ds(0, n_rows)], ybuf.at[buf], sem.at[buf]).wait()

    @pl.when(i == 0)
    def _():
        gather(dest_ref, 0)

    gather(dest_next_ref, 1 - cur)
    wait_gather(cur)
    route = route_ref[...]
    yb = ybuf[cur]
    moe = route[:, 2:3] * yb[0:COMBINE_TILE, :] + route[:, 3:4] * yb[COMBINE_TILE:n_rows, :]
    o_ref[...] = _layer_norm(ALPHA * x1_ref[...] + moe, g_ref[...], b_ref[...])

    @pl.when(i == pl.num_programs(0) - 1)
    def _():
        wait_gather(1 - cur)


def _combine(dest, y_slots, x1_all, route_all, m, row_block, ln_g, ln_b, name):
    tm = COMBINE_TILE
    return pl.pallas_call(
        _combine_kernel,
        out_shape=jax.ShapeDtypeStruct((m, D_MODEL), F32),
        grid=(m // tm,),
        in_specs=[pl.BlockSpec((1, 1, 2 * tm), lambda i: (i, 0, 0), memory_space=pltpu.SMEM),
                  pl.BlockSpec((1, 1, 2 * tm), lambda i: (i + 1, 0, 0), memory_space=pltpu.SMEM),
                  pl.BlockSpec(memory_space=pl.ANY),
                  pl.BlockSpec((tm, D_MODEL), lambda i: (i + row_block, 0)),
                  pl.BlockSpec((tm, LANES), lambda i: (i + row_block, 0)),
                  pl.BlockSpec((1, D_MODEL), lambda i: (0, 0)),
                  pl.BlockSpec((1, D_MODEL), lambda i: (0, 0))],
        out_specs=pl.BlockSpec((tm, D_MODEL), lambda i: (i, 0)),
        scratch_shapes=[pltpu.VMEM((2, 2 * tm, D_MODEL), F32), pltpu.SemaphoreType.DMA((2,))],
        compiler_params=_cparams(("arbitrary",)),
        name=name,
    )(dest, dest, y_slots, x1_all, route_all, ln_g, ln_b)


def _dispatch_plan(route_all, n_blocks):
    flat_e = route_all[:, 0:2].astype(jnp.int32).reshape(-1)
    onehot = (flat_e[:, None] == jnp.arange(N_EXPERTS, dtype=jnp.int32)[None, :]).astype(jnp.int32)
    csum = jnp.cumsum(onehot, axis=0)
    rank = jnp.sum(onehot * csum, axis=1) - 1
    counts = csum[-1]
    padded = (counts + MOE_BLOCK - 1) // MOE_BLOCK * MOE_BLOCK
    pend = jnp.cumsum(padded)
    pstart = pend - padded
    dest = (pstart[flat_e] + rank).astype(jnp.int32)
    zero_offsets = jnp.where(counts > 0, pend - MOE_BLOCK, -1).astype(jnp.int32)
    n_used = (pend[-1] // MOE_BLOCK).astype(jnp.int32)
    block_start = jnp.minimum(jnp.arange(n_blocks, dtype=jnp.int32), n_used - 1) * MOE_BLOCK
    block_e = jnp.minimum(jnp.searchsorted(pend, block_start, side="right"), N_EXPERTS - 1).astype(jnp.int32)
    return dest, jnp.concatenate([zero_offsets, n_used.reshape(1)]), block_e, n_used.reshape(1)


def kernel(x_prompt, x_sample, cache_k_win, cache_v_win, state_wkv, state_shift, w_in, attn_sinks, shift_mu, w0,
           w_decay_up, a0, w_a_up, w_g_up, k_k, k_a, r_k, gn_g, gn_b, w_out, ln1_g, ln1_b, w_coarse, b_coarse,
           w_fine, b_fine, w_exp_gate, w_exp_up, w_exp_down, ln2_g, ln2_b):
    xp = x_prompt[0]
    xs = x_sample[:, 0]
    row = lambda a: a.reshape(1, -1)

    w_in_t = jnp.swapaxes(w_in[0], 0, 1)
    prm = dict(mu=row(shift_mu[0, :D_RKV]), mu_tail=row(shift_mu[0, D_RKV:]), w_tail=w_in_t[D_MAIN:].astype(BF16),
               w0=row(w0[0]), a0=row(a0[0]), k_k=row(k_k[0]), k_a=row(k_a[0]),
               r_k=row(r_k[0]), gn_g=row(gn_g[0]), gn_b=row(gn_b[0]),
               wd=w_decay_up[0], wa=w_a_up[0], wg=w_g_up[0])
    sinks = attn_sinks[0]
    wo_bf16 = w_out[0].astype(BF16)
    w_route = jnp.pad(jnp.concatenate([w_coarse[0], w_fine[0]], axis=1), ((0, 0), (0, LANES - N_GROUPS - N_EXPERTS)))
    b_route = jnp.pad(jnp.concatenate([b_coarse[0], b_fine[0]]), (0, LANES - N_GROUPS - N_EXPERTS)).reshape(1, LANES)

    hp = _matmul(xp, w_in_t, D_MAIN, 1024, MAIN_TN, "in_proj_prompt")
    hs = _matmul(xs, w_in_t, D_MAIN, DEC_BATCH, MAIN_TN, "in_proj_sample")

    attn_p = _prompt_attention(hp, sinks)
    rwkv_p, state_p, tail_p = _prompt_rwkv(hp, xp, prm)

    q_s = hs[:, :D_ATTN].reshape(DEC_BATCH, N_Q_HEADS, HEAD_DIM)
    k_s = hs[:, D_ATTN:D_ATTN + D_KV].reshape(DEC_BATCH, 1, D_KV)
    v_s = hs[:, D_ATTN + D_KV:D_QKV].reshape(DEC_BATCH, 1, D_KV)
    window_t = lambda c: jnp.transpose(c, (0, 2, 3, 1)).reshape(DEC_BATCH, D_KV, WINDOW)
    attn_s, kwin_s, vwin_s = _sample_attention(
        q_s, k_s, v_s, window_t(cache_k_win[0]), window_t(cache_v_win[0]), sinks.reshape(N_Q_HEADS, 1))
    r_s, k2_s, vv_s, g_s, tail_s, r_t, w_t, k_t, v_t, a_t, b_t = _sample_prep(hs, xs, state_shift[0], prm)
    y_t, state_s = _sample_step(jnp.transpose(state_wkv[0], (1, 2, 3, 0)), r_t, w_t, k_t, a_t, b_t, v_t)
    state_s = state_s.reshape(DEC_BATCH, N_RWKV_HEADS, HEAD_DIM, HEAD_DIM)
    rwkv_s = _sample_post(y_t, r_s, k2_s, vv_s, g_s, prm)

    n_tokens = SEQ + DEC_BATCH
    outs_pr = _outproj_router(attn_p, rwkv_p, xp, wo_bf16, row(ln1_g[0]), row(ln1_b[0]), w_route, b_route,
                              256, n_tokens, 0, None, "outproj_router_prompt")
    x1_all, x1b_all, route_all = _outproj_router(attn_s.reshape(DEC_BATCH, D_ATTN), rwkv_s, xs,
                                                 wo_bf16, row(ln1_g[0]), row(ln1_b[0]), w_route, b_route,
                                                 DEC_BATCH, n_tokens, SEQ // DEC_BATCH, outs_pr,
                                                 "outproj_router_sample")

    n_assign = 2 * n_tokens
    n_blocks = -(-(n_assign + N_EXPERTS * (MOE_BLOCK - 1)) // MOE_BLOCK)
    dest, zero_offsets, block_e, n_used = _dispatch_plan(route_all, n_blocks)
    x_sorted = _dispatch(zero_offsets, dest, x1b_all, n_blocks)
    y_slots = _expert_mlp(block_e, n_used, x_sorted, w_exp_gate[0], w_exp_up[0], w_exp_down[0], n_blocks)

    def dest_tiles(d):
        d = d.reshape(-1, COMBINE_TILE, 2)
        d = jnp.concatenate([d[:, :, 0], d[:, :, 1]], axis=1)
        return jnp.pad(d, ((0, 1), (0, 0))).reshape(-1, 1, 2 * COMBINE_TILE)

    y_p = _combine(dest_tiles(dest[:2 * SEQ]), y_slots, x1_all, route_all, SEQ, 0, row(ln2_g[0]), row(ln2_b[0]),
                   "combine_prompt")
    y_s = _combine(dest_tiles(dest[2 * SEQ:]), y_slots, x1_all, route_all, DEC_BATCH, SEQ // COMBINE_TILE,
                   row(ln2_g[0]), row(ln2_b[0]), "combine_sample")

    kv4 = lambda a: a.reshape(a.shape[0], N_KV_HEADS, HEAD_DIM)
    k_win_p = kv4(hp[SEQ - WINDOW:, D_ATTN:D_ATTN + D_KV])[None, None]
    v_win_p = kv4(hp[SEQ - WINDOW:, D_ATTN + D_KV:D_QKV])[None, None]
    sp = state_p.reshape(N_PAIRS, HEADS_PER_TILE, HEAD_DIM, HEADS_PER_TILE, HEAD_DIM)
    wkv_p = jnp.stack([sp[:, i, :, i, :] for i in range(HEADS_PER_TILE)], axis=1)
    wkv_p = wkv_p.reshape(N_RWKV_HEADS, HEAD_DIM, HEAD_DIM).transpose(0, 2, 1)[None, None]
    shift_p = jnp.concatenate([hp[SEQ - 1:SEQ, D_QKV:], tail_p[0:1]], axis=1)[None]
    shift_s = jnp.concatenate([hs[:, D_QKV:], tail_s], axis=1)[None]
    return (y_p[None], y_s[:, None, :], k_win_p, v_win_p, wkv_p, shift_p,
            kwin_s.reshape(1, DEC_BATCH, WINDOW, N_KV_HEADS, HEAD_DIM),
            vwin_s.reshape(1, DEC_BATCH, WINDOW, N_KV_HEADS, HEAD_DIM),
            state_s[None], shift_s)
```

```python
import functools
import math

import jax
import jax.numpy as jnp
from jax import lax
from jax.experimental import pallas as pl
from jax.experimental.pallas import tpu as pltpu

F32 = jnp.float32
BF16 = jnp.bfloat16

D_MODEL = 2048
SEQ = 8192
DEC_BATCH = 128
HEAD_DIM = 64
D_ATTN = 1024
D_RWKV = 1024
N_Q_HEADS = 16
N_KV_HEADS = 4
Q_PER_KV = 4
D_KV = 256
WINDOW = 128
ATTN_SCALE = HEAD_DIM ** -0.5
N_RWKV_HEADS = 16
W_LORA = 64
A_LORA = 64
G_LORA = 160
D_SHIFT = 3 * D_RWKV + W_LORA + A_LORA + G_LORA
D_QKV = D_ATTN + 2 * D_KV
N_GROUPS = 4
EXPERTS_PER_GROUP = 8
N_EXPERTS = 32
D_EXPERT = 512
ALPHA = 2.0 ** 0.25
LN_EPS = 1e-5
GN_EPS = 64e-5

SUBLANES = 8
LANES = 128
VMEM_LIMIT = 52 * 1024 * 1024

D_RKV = 3 * D_RWKV
D_TAIL = W_LORA + A_LORA + G_LORA
D_MAIN = D_QKV + D_RKV
MAIN_TN = 1536
MAIN_TM = 512

CHUNK = 64
HEADS_PER_TILE = LANES // HEAD_DIM
N_PAIRS = N_RWKV_HEADS // HEADS_PER_TILE
SOLVE_LEVELS = int(math.log2(CHUNK))
PAIR_GROUP = 8

MOE_BLOCK = 256
ROUTE_FINE_OFF = N_GROUPS

NN = (((1,), (0,)), ((), ()))
NT = (((1,), (1,)), ((), ()))


def _dot(a, b, dims=NN):
    return lax.dot_general(a, b, dims, preferred_element_type=F32)


def _dot1(a, b, dims=NN):
    return _dot(a.astype(BF16), b.astype(BF16), dims)


def _split(x):
    hi = x.astype(BF16)
    lo = (x - hi.astype(F32)).astype(BF16)
    return hi, lo


def _dot3(a, b, dims=NN):
    ah, al = _split(a)
    bh, bl = _split(b)
    return _dot(ah, bh, dims) + (_dot(ah, bl, dims) + _dot(al, bh, dims))


def _dot_exact_lhs(a_bf16, b, dims=NN):
    bh, bl = _split(b)
    return _dot(a_bf16, bh, dims) + _dot(a_bf16, bl, dims)


def _dot_exact_rhs(a, b_bf16, dims=NN):
    ah, al = _split(a)
    return _dot(ah, b_bf16, dims) + _dot(al, b_bf16, dims)


def _div_pow2(x, d):
    return lax.shift_right_logical(x, jnp.int32(int(math.log2(d))))


def _mod_pow2(x, d):
    return lax.bitwise_and(x, jnp.int32(d - 1))


def _pack_bf16_halves(x_bf16):
    n = x_bf16.shape[1] // 2
    bits = lax.bitcast_convert_type(x_bf16.astype(F32), jnp.uint32)
    return lax.bitwise_or(bits[:, 0:n], lax.shift_right_logical(bits[:, n:2 * n], jnp.uint32(16)))


def _unpack_bf16_halves(packed):
    hi = lax.bitcast_convert_type(lax.bitwise_and(packed, jnp.uint32(0xFFFF0000)), F32)
    lo = lax.bitcast_convert_type(lax.shift_left(packed, jnp.uint32(16)), F32)
    return hi.astype(BF16), lo.astype(BF16)


def _sigmoid(x):
    return 1.0 / (1.0 + jnp.exp(-x))


def _softplus(x):
    return jnp.maximum(x, 0.0) + jnp.log(1.0 + jnp.exp(-jnp.abs(x)))


def _layer_norm(z, g, b):
    mu = jnp.mean(z, axis=-1, keepdims=True)
    d = z - mu
    var = jnp.mean(d * d, axis=-1, keepdims=True)
    return d * lax.rsqrt(var + LN_EPS) * g + b


def _cparams(sem):
    return pltpu.CompilerParams(dimension_semantics=sem, vmem_limit_bytes=VMEM_LIMIT)


def _matmul_kernel(x_ref, wt_ref, o_ref):
    o_ref[...] = _dot(x_ref[...].astype(BF16), wt_ref[...].astype(BF16), NT)


def _matmul(x, w_t, n_out, tm, tn, name):
    m, k = x.shape
    tm = min(tm, m)
    return pl.pallas_call(
        _matmul_kernel,
        out_shape=jax.ShapeDtypeStruct((m, n_out), F32),
        grid=(n_out // tn, m // tm),
        in_specs=[pl.BlockSpec((tm, k), lambda j, i: (i, 0)),
                  pl.BlockSpec((tn, k), lambda j, i: (j, 0))],
        out_specs=pl.BlockSpec((tm, tn), lambda j, i: (i, j)),
        compiler_params=_cparams(("arbitrary", "arbitrary")),
        name=name,
    )(x, w_t)


def _prompt_attn_kernel(q_ref, kvp_ref, kvc_ref, sink_ref, o_ref):
    blk = pl.program_id(0)
    q = q_ref[...]
    kv_prev = kvp_ref[...]
    kv_cur = kvc_ref[...]
    qi = _mod_pow2(lax.broadcasted_iota(jnp.int32, (Q_PER_KV * WINDOW, 2 * WINDOW), 0), WINDOW)
    kj = lax.broadcasted_iota(jnp.int32, (Q_PER_KV * WINDOW, 2 * WINDOW), 1)
    diff = qi + WINDOW - kj
    mask = (diff >= 0) & (diff <= WINDOW) & ((blk > 0) | (kj >= WINDOW))
    row_head = _div_pow2(lax.broadcasted_iota(jnp.int32, (Q_PER_KV * WINDOW, 1), 0), WINDOW)
    groups = range(N_KV_HEADS)
    kv_cols = lambda off, g: jnp.concatenate([kv_prev[:, off + g * HEAD_DIM:off + (g + 1) * HEAD_DIM],
                                              kv_cur[:, off + g * HEAD_DIM:off + (g + 1) * HEAD_DIM]],
                                             axis=0).astype(BF16)
    q_rows = lambda g: jnp.concatenate(
        [q[:, (g * Q_PER_KV + h) * HEAD_DIM:(g * Q_PER_KV + h + 1) * HEAD_DIM] for h in range(Q_PER_KV)],
        axis=0).astype(BF16)
    s = [jnp.where(mask, _dot(q_rows(g), kv_cols(0, g), NT) * ATTN_SCALE, -jnp.inf) for g in groups]
    sink = []
    for g in groups:
        col = jnp.zeros((Q_PER_KV * WINDOW, 1), F32)
        for h in range(Q_PER_KV):
            col = jnp.where(row_head == h, sink_ref[g * Q_PER_KV + h], col)
        sink.append(col)
    m = [jnp.maximum(jnp.max(s[g], axis=-1, keepdims=True), sink[g]) for g in groups]
    p = [jnp.exp(s[g] - m[g]) for g in groups]
    denom = [jnp.sum(p[g], axis=-1, keepdims=True) + jnp.exp(sink[g] - m[g]) for g in groups]
    o = [_dot((p[g] / denom[g]).astype(BF16), kv_cols(D_KV, g)) for g in groups]
    o_ref[...] = jnp.concatenate([o[g][h * WINDOW:(h + 1) * WINDOW, :] for g in groups for h in range(Q_PER_KV)],
                                 axis=1)


def _prompt_attention(h_attn, sinks):
    nb = SEQ // WINDOW
    return pl.pallas_call(
        _prompt_attn_kernel,
        out_shape=jax.ShapeDtypeStruct((SEQ, D_ATTN), F32),
        grid=(nb,),
        in_specs=[pl.BlockSpec((WINDOW, D_ATTN), lambda i: (i, 0)),
                  pl.BlockSpec((WINDOW, 2 * D_KV), lambda i: (jnp.maximum(i - 1, 0), 2)),
                  pl.BlockSpec((WINDOW, 2 * D_KV), lambda i: (i, 2)),
                  pl.BlockSpec(memory_space=pltpu.SMEM)],
        out_specs=pl.BlockSpec((WINDOW, D_ATTN), lambda i: (i, 0)),
        compiler_params=_cparams(("arbitrary",)),
        name="prompt_attention",
    )(h_attn, h_attn, h_attn, sinks)


SAMPLE_ATTN_TILE = 8


def _sample_attn_kernel(q_ref, knew_ref, vnew_ref, ck_ref, cv_ref, sink_ref, o_ref, kwin_ref, vwin_ref):
    lane = lax.broadcasted_iota(jnp.int32, (N_Q_HEADS, D_KV), 1)
    head = lax.broadcasted_iota(jnp.int32, (N_Q_HEADS, D_KV), 0)
    group_mask = _div_pow2(lane, HEAD_DIM) == _div_pow2(head, Q_PER_KV)
    sink = sink_ref[...]
    row = lax.broadcasted_iota(jnp.int32, (WINDOW, D_KV), 0)
    for b in range(SAMPLE_ATTN_TILE):
        q = q_ref[b]
        qbd = jnp.where(group_mask, jnp.concatenate([q] * N_KV_HEADS, axis=1), 0.0).astype(BF16)
        k_t = ck_ref[b]
        kb = k_t.T
        vb = cv_ref[b].T
        kn = knew_ref[b]
        vn = vnew_ref[b]
        s = _dot1(qbd, k_t) * ATTN_SCALE
        s_new = jnp.sum(qbd.astype(F32) * kn.astype(BF16).astype(F32), axis=-1, keepdims=True) * ATTN_SCALE
        m = jnp.maximum(jnp.maximum(jnp.max(s, axis=-1, keepdims=True), s_new), sink)
        p = jnp.exp(s - m)
        p_new = jnp.exp(s_new - m)
        denom = jnp.sum(p, axis=-1, keepdims=True) + p_new + jnp.exp(sink - m)
        p = p / denom
        p_new = (p_new / denom).astype(BF16).astype(F32)
        o_full = _dot1(p, vb) + p_new * vn.astype(BF16).astype(F32)
        o_full = jnp.where(group_mask, o_full, 0.0)
        o = o_full[:, 0:HEAD_DIM]
        for g in range(1, N_KV_HEADS):
            o = o + o_full[:, g * HEAD_DIM:(g + 1) * HEAD_DIM]
        o_ref[b] = o
        kwin_ref[b] = jnp.where(row == WINDOW - 1, kn, pltpu.roll(kb, WINDOW - 1, axis=0))
        vwin_ref[b] = jnp.where(row == WINDOW - 1, vn, pltpu.roll(vb, WINDOW - 1, axis=0))


def _sample_attention(q, k_new, v_new, cache_k_t, cache_v_t, sinks):
    bt = SAMPLE_ATTN_TILE
    win_spec = pl.BlockSpec((bt, WINDOW, D_KV), lambda i: (i, 0, 0))
    win_t_spec = pl.BlockSpec((bt, D_KV, WINDOW), lambda i: (i, 0, 0))
    new_spec = pl.BlockSpec((bt, 1, D_KV), lambda i: (i, 0, 0))
    return pl.pallas_call(
        _sample_attn_kernel,
        out_shape=(jax.ShapeDtypeStruct((DEC_BATCH, N_Q_HEADS, HEAD_DIM), F32),
                   jax.ShapeDtypeStruct((DEC_BATCH, WINDOW, D_KV), F32),
                   jax.ShapeDtypeStruct((DEC_BATCH, WINDOW, D_KV), F32)),
        grid=(DEC_BATCH // bt,),
        in_specs=[pl.BlockSpec((bt, N_Q_HEADS, HEAD_DIM), lambda i: (i, 0, 0)),
                  new_spec, new_spec, win_t_spec, win_t_spec,
                  pl.BlockSpec((N_Q_HEADS, 1), lambda i: (0, 0))],
        out_specs=(pl.BlockSpec((bt, N_Q_HEADS, HEAD_DIM), lambda i: (i, 0, 0)), win_spec, win_spec),
        compiler_params=_cparams(("arbitrary",)),
        name="sample_attention",
    )(q, k_new, v_new, cache_k_t, cache_v_t, sinks)


def _head_ones():
    r = _div_pow2(lax.broadcasted_iota(jnp.int32, (LANES, LANES), 0), HEAD_DIM)
    c = _div_pow2(lax.broadcasted_iota(jnp.int32, (LANES, LANES), 1), HEAD_DIM)
    return jnp.where(r == c, 1.0, 0.0).astype(BF16)


def _head_sum(x, ones):
    parts = [_dot_exact_rhs(x[:, p * LANES:(p + 1) * LANES], ones) for p in range(x.shape[1] // LANES)]
    return jnp.concatenate(parts, axis=1)


def _token_mix(feat, shifted, mu):
    return feat + (shifted - feat) * mu


def _rwkv_prep(mixed, mixed_tail, w0, a0, k_k, k_a, wd, wa, wg, ones):
    r = mixed[:, 0:D_RWKV]
    k = mixed[:, D_RWKV:2 * D_RWKV]
    v = mixed[:, 2 * D_RWKV:3 * D_RWKV]
    xw = mixed_tail[:, 0:W_LORA]
    xa = mixed_tail[:, W_LORA:W_LORA + A_LORA]
    xg = mixed_tail[:, W_LORA + A_LORA:D_TAIL]
    w_log = -_softplus(-(w0 + _dot1(jnp.tanh(xw), wd))) - 0.5
    log_decay = -jnp.exp(w_log)
    a = _sigmoid(a0 + _dot1(xa, wa))
    g = _dot1(_sigmoid(xg), wg)
    kk = k * k_k
    kk = kk * lax.rsqrt(jnp.maximum(_head_sum(kk * kk, ones), 1e-24))
    k2 = k * (1.0 + (a - 1.0) * k_a)
    return r, log_decay, k2, v, -kk, kk * a, g


def _rwkv_post(y, r, k2, v, g, r_k, gn_g, gn_b, ones):
    inv_n = 1.0 / HEAD_DIM
    mu = _head_sum(y, ones) * inv_n
    d = y - mu
    var = _head_sum(d * d, ones) * inv_n
    yn = d * lax.rsqrt(var + GN_EPS) * gn_g + gn_b
    bonus = _head_sum(r * k2 * r_k, ones) * v
    return (yn + bonus) * g


(OP_AABS, OP_RABS, OP_AN, OP_RN, OP_BN, OP_KN, OP_BH, OP_KH, OP_V) = range(9)
N_OPS = 9


def _prompt_rwkv_kernel(f1_ref, f2_ref, x_ref, wt_ref, mu_ref, mut_ref, w0_ref, a0_ref, kk_ref, ka_ref, rk_ref,
                        gng_ref, gnb_ref, wd_ref, wa_ref, wg_ref, out_ref, state_ref, tail_ref,
                        prev_ref, prevt_ref, s_ref, ops_ref, pc_ref, y_ref):
    c = pl.program_id(0)
    C = CHUNK

    @pl.when(c == 0)
    def _():
        prev_ref[...] = jnp.zeros_like(prev_ref)
        prevt_ref[...] = jnp.zeros_like(prevt_ref)
        s_ref[...] = jnp.zeros_like(s_ref)

    ones = _head_ones()
    row = lax.broadcasted_iota(jnp.int32, (C, 1), 0)

    def token_shift(feat, carry_ref):
        shifted = jnp.where(row == 0, carry_ref[0:1, :], pltpu.roll(feat, 1, axis=0))
        carry_ref[0:1, :] = feat[C - 1:C, :]
        return shifted

    feat = jnp.concatenate([f1_ref[...], f2_ref[...]], axis=1)
    tail = _dot1(x_ref[...], wt_ref[...], NT)
    mixed = _token_mix(feat, token_shift(feat, prev_ref), mu_ref[...])
    mixed_tail = _token_mix(tail, token_shift(tail, prevt_ref), mut_ref[...])
    tail_ref[...] = prevt_ref[...]
    r, ld, k2, v, av, bv, g = _rwkv_prep(mixed, mixed_tail, w0_ref[...], a0_ref[...], kk_ref[...], ka_ref[...],
                                         wd_ref[...], wa_ref[...], wg_ref[...], ones)

    ti = lax.broadcasted_iota(jnp.int32, (C, C), 0)
    tj = lax.broadcasted_iota(jnp.int32, (C, C), 1)
    tri_incl = jnp.where(tj <= ti, 1.0, 0.0).astype(BF16)
    cs = _dot_exact_lhs(tri_incl, ld)
    cs_ref = cs[C // 2 - 1:C // 2, :]
    cs_end = cs[C - 1:C, :]
    e_prev = jnp.exp(cs - ld)
    e_cur = jnp.exp(cs)
    n_prev = jnp.exp(cs - ld - cs_ref)
    n_cur = jnp.exp(cs - cs_ref)
    n_inv = jnp.exp(cs_ref - cs)
    e_tail = jnp.exp(cs_end - cs)
    ops = {OP_AABS: av * e_prev, OP_RABS: r * e_cur, OP_AN: av * n_prev, OP_RN: r * n_cur,
           OP_BN: bv * n_inv, OP_KN: k2 * n_inv, OP_BH: bv * e_tail, OP_KH: k2 * e_tail, OP_V: v}
    p_end = jnp.exp(cs_end)
    for p in range(N_PAIRS):
        sl = slice(p * LANES, (p + 1) * LANES)
        for idx, val in ops.items():
            ops_ref[p, idx] = val[:, sl]
        pc_ref[p] = jnp.broadcast_to(p_end[:, sl], (SUBLANES, LANES))

    lane1 = lax.broadcasted_iota(jnp.int32, (C, LANES), 1)
    head0 = lane1 < HEAD_DIM
    r2 = lax.broadcasted_iota(jnp.int32, (2 * C, 2 * C), 0)
    c2 = lax.broadcasted_iota(jnp.int32, (2 * C, 2 * C), 1)
    tq = _mod_pow2(r2, C)
    tk = _mod_pow2(c2, C)
    band = (tk < tq) | ((tk == tq) & (r2 >= C))
    blockdiag = _div_pow2(r2, HEAD_DIM) == _div_pow2(c2, HEAD_DIM)

    op = lambda p, idx: ops_ref[p, idx]
    zero_half = jnp.zeros((C, LANES), F32)
    for pairs in [range(g, g + PAIR_GROUP) for g in range(0, N_PAIRS, PAIR_GROUP)]:
        gy = {p: _dot1(jnp.concatenate([op(p, OP_AABS), op(p, OP_RABS)], axis=0), s_ref[p]) for p in pairs}

        am0, am1 = {}, {}
        for p in pairs:
            a_n, r_n = op(p, OP_AN), op(p, OP_RN)
            b0, k0 = jnp.where(head0, op(p, OP_BN), 0.0), jnp.where(head0, op(p, OP_KN), 0.0)
            b1, k1 = jnp.where(head0, 0.0, op(p, OP_BN)), jnp.where(head0, 0.0, op(p, OP_KN))
            am = _dot1(jnp.concatenate([a_n, r_n], axis=0), jnp.concatenate([k0, b0, b1, k1], axis=0), NT)
            am0[p] = jnp.where(band, am[:, 0:2 * C], 0.0)
            am1[p] = jnp.where(band, am[:, 2 * C:4 * C], 0.0)

        w0, w1 = {}, {}
        for p in pairs:
            top0, top1 = am0[p][0:C], am1[p][0:C]
            ak = jnp.concatenate([jnp.where(head0, top0, 0.0), jnp.where(head0, 0.0, top1)], axis=0)
            vv = op(p, OP_V)
            g0 = gy[p][0:C]
            m = jnp.concatenate([g0, g0], axis=0) + _dot1(ak, jnp.concatenate([vv, vv], axis=0))
            w0[p] = jnp.where(head0, m[0:C], top0)
            w1[p] = jnp.where(head0, top1, m[C:2 * C])

        for lvl in range(SOLVE_LEVELS):
            prod0 = {p: _dot1(w0[p], jnp.concatenate([zero_half, w0[p]], axis=0)) for p in pairs}
            prod1 = {p: _dot1(w1[p], jnp.concatenate([w1[p], zero_half], axis=0)) for p in pairs}
            w0 = {p: jnp.where(head0, w0[p] + prod0[p], prod0[p]) for p in pairs}
            w1 = {p: jnp.where(head0, prod1[p], w1[p] + prod1[p]) for p in pairs}
        u = {p: jnp.where(head0, w0[p], w1[p]) for p in pairs}

        for p in pairs:
            vv = op(p, OP_V)
            y_lhs = jnp.concatenate([am0[p][C:2 * C], am1[p][C:2 * C]], axis=1)
            y_rhs = jnp.concatenate([jnp.where(head0, vv, 0.0), jnp.where(head0, u[p], 0.0),
                                     jnp.where(head0, 0.0, u[p]), jnp.where(head0, 0.0, vv)], axis=0)
            y_ref[p] = gy[p][C:2 * C] + _dot1(y_lhs, y_rhs)

        for p in pairs:
            decay_rows = jnp.broadcast_to(pc_ref[p][0:1, :], (LANES, LANES)).T
            upd_lhs = jnp.concatenate([op(p, OP_BH), op(p, OP_KH)], axis=0).T
            upd_rhs = jnp.concatenate([u[p], op(p, OP_V)], axis=0)
            s_ref[p] = s_ref[p] * decay_rows + jnp.where(blockdiag, _dot1(upd_lhs, upd_rhs), 0.0)

    y = jnp.concatenate([y_ref[p] for p in range(N_PAIRS)], axis=1)
    out_ref[...] = _rwkv_post(y, r, k2, v, g, rk_ref[...], gng_ref[...], gnb_ref[...], ones)

    @pl.when(c == pl.num_programs(0) - 1)
    def _():
        state_ref[...] = s_ref[...]


def _prompt_rwkv(h_main, x, prm):
    n_chunks = SEQ // CHUNK
    half = D_RKV // 2
    assert D_QKV == half
    vec = pl.BlockSpec((1, D_RWKV), lambda c: (0, 0))
    full = lambda a: pl.BlockSpec(a.shape, lambda c: (0,) * a.ndim)
    return pl.pallas_call(
        _prompt_rwkv_kernel,
        out_shape=(jax.ShapeDtypeStruct((SEQ, D_RWKV), F32),
                   jax.ShapeDtypeStruct((N_PAIRS, LANES, LANES), F32),
                   jax.ShapeDtypeStruct((SUBLANES, D_TAIL), F32)),
        grid=(n_chunks,),
        in_specs=[pl.BlockSpec((CHUNK, half), lambda c: (c, 1)),
                  pl.BlockSpec((CHUNK, half), lambda c: (c, 2)),
                  pl.BlockSpec((CHUNK, D_MODEL), lambda c: (c, 0)),
                  full(prm["w_tail"]), full(prm["mu"]), full(prm["mu_tail"]),
                  vec, vec, vec, vec, vec, vec, vec,
                  full(prm["wd"]), full(prm["wa"]), full(prm["wg"])],
        out_specs=(pl.BlockSpec((CHUNK, D_RWKV), lambda c: (c, 0)),
                   pl.BlockSpec((N_PAIRS, LANES, LANES), lambda c: (0, 0, 0)),
                   pl.BlockSpec((SUBLANES, D_TAIL), lambda c: (0, 0))),
        scratch_shapes=[pltpu.VMEM((SUBLANES, D_RKV), F32),
                        pltpu.VMEM((SUBLANES, D_TAIL), F32),
                        pltpu.VMEM((N_PAIRS, LANES, LANES), F32),
                        pltpu.VMEM((N_PAIRS, N_OPS, CHUNK, LANES), F32),
                        pltpu.VMEM((N_PAIRS, SUBLANES, LANES), F32),
                        pltpu.VMEM((N_PAIRS, CHUNK, LANES), F32)],
        compiler_params=_cparams(("arbitrary",)),
        name="prompt_rwkv",
    )(h_main, h_main, x, prm["w_tail"], prm["mu"], prm["mu_tail"], prm["w0"], prm["a0"], prm["k_k"], prm["k_a"],
      prm["r_k"], prm["gn_g"], prm["gn_b"], prm["wd"], prm["wa"], prm["wg"])


def _sample_prep_kernel(h_ref, x_ref, wt_ref, shift_ref, mu_ref, mut_ref, w0_ref, a0_ref, kk_ref, ka_ref,
                        wd_ref, wa_ref, wg_ref, r_ref, k_ref, v_ref, g_ref, tail_ref,
                        rt_ref, wtr_ref, kt_ref, vt_ref, at_ref, bt_ref):
    ones = _head_ones()
    feat = h_ref[:, D_QKV:D_MAIN]
    tail = _dot1(x_ref[...], wt_ref[...], NT)
    tail_ref[...] = tail
    mixed = _token_mix(feat, shift_ref[:, 0:D_RKV], mu_ref[...])
    mixed_tail = _token_mix(tail, shift_ref[:, D_RKV:D_SHIFT], mut_ref[...])
    r, ld, k2, v, av, bv, g = _rwkv_prep(mixed, mixed_tail, w0_ref[...], a0_ref[...], kk_ref[...], ka_ref[...],
                                         wd_ref[...], wa_ref[...], wg_ref[...], ones)
    r_ref[...] = r
    k_ref[...] = k2
    v_ref[...] = v
    g_ref[...] = g
    rt_ref[...] = r.T
    wtr_ref[...] = jnp.exp(ld).T
    kt_ref[...] = k2.T
    vt_ref[...] = v.T
    at_ref[...] = av.T
    bt_ref[...] = bv.T


def _sample_prep(h_main, x, shift, prm):
    tok = jax.ShapeDtypeStruct((DEC_BATCH, D_RWKV), F32)
    chan = jax.ShapeDtypeStruct((D_RWKV, DEC_BATCH), F32)
    return pl.pallas_call(
        _sample_prep_kernel,
        out_shape=(tok,) * 4 + (jax.ShapeDtypeStruct((DEC_BATCH, D_TAIL), F32),) + (chan,) * 6,
        compiler_params=pltpu.CompilerParams(vmem_limit_bytes=VMEM_LIMIT),
        name="sample_rwkv_prep",
    )(h_main, x, prm["w_tail"], shift, prm["mu"], prm["mu_tail"], prm["w0"], prm["a0"], prm["k_k"], prm["k_a"],
      prm["wd"], prm["wa"], prm["wg"])


STEP_GROUP = 4


def _sample_step_kernel(s_ref, r_ref, w_ref, k_ref, a_ref, b_ref, v_ref, y_ref, snew_ref):
    r, w, k, a, b = r_ref[...], w_ref[...], k_ref[...], a_ref[...], b_ref[...]
    for g0 in range(0, HEAD_DIM, 2 * STEP_GROUP):
        chans = range(g0, g0 + 2 * STEP_GROUP)
        sa = {i: jnp.sum(s_ref[0, i] * a, axis=0, keepdims=True) for i in chans}
        s_new = {i: s_ref[0, i] * w + sa[i] * b + v_ref[i:i + 1, :] * k for i in chans}
        for i in chans:
            y_ref[i:i + 1, :] = jnp.sum(s_new[i] * r, axis=0, keepdims=True)
        for i in range(g0, g0 + 2 * STEP_GROUP, 2):
            pair = jnp.concatenate([s_new[i], s_new[i + 1]], axis=0)
            snew_ref[:, i * HEAD_DIM:(i + 2) * HEAD_DIM] = pair.T


def _sample_step(state_t, r_t, w_t, k_t, a_t, b_t, v_t):
    head_rows = pl.BlockSpec((HEAD_DIM, DEC_BATCH), lambda h: (h, 0))
    return pl.pallas_call(
        _sample_step_kernel,
        out_shape=(jax.ShapeDtypeStruct((D_RWKV, DEC_BATCH), F32),
                   jax.ShapeDtypeStruct((DEC_BATCH, N_RWKV_HEADS * HEAD_DIM * HEAD_DIM), F32)),
        grid=(N_RWKV_HEADS,),
        in_specs=[pl.BlockSpec((1, HEAD_DIM, HEAD_DIM, DEC_BATCH), lambda h: (h, 0, 0, 0))] + [head_rows] * 6,
        out_specs=(head_rows, pl.BlockSpec((DEC_BATCH, HEAD_DIM * HEAD_DIM), lambda h: (0, h))),
        compiler_params=_cparams(("arbitrary",)),
        name="sample_rwkv_step",
    )(state_t, r_t, w_t, k_t, a_t, b_t, v_t)


def _sample_post_kernel(yt_ref, r_ref, k_ref, v_ref, g_ref, rk_ref, gng_ref, gnb_ref, o_ref):
    o_ref[...] = _rwkv_post(yt_ref[...].T, r_ref[...], k_ref[...], v_ref[...], g_ref[...], rk_ref[...],
                            gng_ref[...], gnb_ref[...], _head_ones())


def _sample_post(y, r, k, v, g, prm):
    return pl.pallas_call(
        _sample_post_kernel,
        out_shape=jax.ShapeDtypeStruct((DEC_BATCH, D_RWKV), F32),
        compiler_params=pltpu.CompilerParams(vmem_limit_bytes=VMEM_LIMIT),
        name="sample_rwkv_post",
    )(y, r, k, v, g, prm["r_k"], prm["gn_g"], prm["gn_b"])


def _outproj_router_kernel(attn_ref, rwkv_ref, x_ref, wo_ref, g_ref, b_ref, wr_ref, br_ref,
                           x1_ref, x1b_ref, route_ref):
    mix = _dot(attn_ref[...].astype(BF16), wo_ref[0:D_ATTN, :]) + _dot(rwkv_ref[...].astype(BF16),
                                                                        wo_ref[D_ATTN:D_ATTN + D_RWKV, :])
    x1 = _layer_norm(ALPHA * x_ref[...] + mix, g_ref[...], b_ref[...])
    x1_ref[...] = x1
    x1b = x1.astype(BF16)
    x1b_ref[...] = _pack_bf16_halves(x1b)
    logits = _dot(x1b, wr_ref[...].astype(BF16)) + br_ref[...]
    tm = logits.shape[0]
    lane = lax.broadcasted_iota(jnp.int32, (tm, LANES), 1).astype(F32)
    big = float(2 * LANES)
    neg = -jnp.inf
    lc = jnp.where(lane < N_GROUPS, logits, neg)
    mc = jnp.max(lc, axis=-1, keepdims=True)
    g_sel = jnp.min(jnp.where(lc == mc, lane, big), axis=-1, keepdims=True)
    p_group = 1.0 / jnp.sum(jnp.exp(lc - mc), axis=-1, keepdims=True)
    lo = ROUTE_FINE_OFF + g_sel * EXPERTS_PER_GROUP
    lf = jnp.where((lane >= lo) & (lane < lo + EXPERTS_PER_GROUP), logits, neg)
    v1 = jnp.max(lf, axis=-1, keepdims=True)
    i1 = jnp.min(jnp.where(lf == v1, lane, big), axis=-1, keepdims=True)
    lf2 = jnp.where(lane == i1, neg, lf)
    v2 = jnp.max(lf2, axis=-1, keepdims=True)
    i2 = jnp.min(jnp.where(lf2 == v2, lane, big), axis=-1, keepdims=True)
    e21 = jnp.exp(v2 - v1)
    gate1 = p_group / (1.0 + e21)
    gate2 = p_group * e21 / (1.0 + e21)
    route = jnp.where(lane == 0, i1 - ROUTE_FINE_OFF,
                      jnp.where(lane == 1, i2 - ROUTE_FINE_OFF,
                                jnp.where(lane == 2, gate1, jnp.where(lane == 3, gate2, 0.0))))
    route_ref[...] = route


N_ROUTER_OUTS = 3


def _outproj_router_into_kernel(*refs):
    _outproj_router_kernel(*refs[:-2 * N_ROUTER_OUTS], *refs[-N_ROUTER_OUTS:])


def _outproj_router_fill_kernel(n_steps, *refs):
    @pl.when(pl.program_id(0) < n_steps)
    def _():
        _outproj_router_kernel(*refs)

    @pl.when(pl.program_id(0) >= n_steps)
    def _():
        for out_ref in refs[-N_ROUTER_OUTS:]:
            out_ref[...] = jnp.zeros_like(out_ref)


def _outproj_router(attn, rwkv, x, wo_bf16, ln_g, ln_b, w_route, b_route, tm, n_total, row_block, into, name):
    m = x.shape[0]
    n_steps = m // tm
    const = lambda shape: pl.BlockSpec(shape, lambda i: (0, 0))
    rows = lambda width: pl.BlockSpec((tm, width), lambda i: (jnp.minimum(i, n_steps - 1), 0))
    in_specs = [rows(D_ATTN), rows(D_RWKV), rows(D_MODEL),
                const((D_MODEL, D_MODEL)), const((1, D_MODEL)), const((1, D_MODEL)),
                const((D_MODEL, LANES)), const((1, LANES))]
    args = [attn, rwkv, x, wo_bf16, ln_g, ln_b, w_route, b_route]
    aliases = {}
    if into is not None:
        in_specs += [pl.BlockSpec(memory_space=pl.ANY)] * N_ROUTER_OUTS
        aliases = {len(args) + k: k for k in range(N_ROUTER_OUTS)}
        args += list(into)
        body, grid_steps = _outproj_router_into_kernel, n_steps
    else:
        body, grid_steps = functools.partial(_outproj_router_fill_kernel, n_steps), pl.cdiv(n_total, tm)
    out_rows = lambda width: pl.BlockSpec((tm, width), lambda i: (i + row_block, 0))
    return pl.pallas_call(
        body,
        out_shape=(jax.ShapeDtypeStruct((n_total, D_MODEL), F32),
                   jax.ShapeDtypeStruct((n_total, D_MODEL // 2), jnp.uint32),
                   jax.ShapeDtypeStruct((n_total, LANES), F32)),
        grid=(grid_steps,),
        in_specs=in_specs,
        out_specs=(out_rows(D_MODEL), out_rows(D_MODEL // 2), out_rows(LANES)),
        input_output_aliases=aliases,
        compiler_params=_cparams(("arbitrary",)),
        name=name,
    )(*args)


DISPATCH_TILE = 128


def _dispatch_kernel(zoff_ref, dest_ref, x_ref, o_hbm, zbuf, zsem, sem):
    i = pl.program_id(0)
    n_blocks = o_hbm.shape[0] // MOE_BLOCK
    n_used = zoff_ref[N_EXPERTS]

    def zero_fill(start_row):
        start_row = pl.multiple_of(start_row, MOE_BLOCK)
        return pltpu.make_async_copy(zbuf, o_hbm.at[pl.ds(start_row, MOE_BLOCK)], zsem)

    def zero_fills(action):
        for e in range(N_EXPERTS):
            @pl.when(zoff_ref[e] >= 0)
            def _():
                action(zero_fill(zoff_ref[e]))
        for b in range(n_blocks):
            @pl.when(b >= n_used)
            def _():
                action(zero_fill(b * MOE_BLOCK))

    @pl.when(i == 0)
    def _():
        zbuf[...] = jnp.zeros_like(zbuf)
        zero_fills(lambda copy: copy.start())
        zero_fills(lambda copy: copy.wait())

    for t in range(DISPATCH_TILE):
        for k in range(2):
            pltpu.make_async_copy(x_ref.at[pl.ds(t, 1)], o_hbm.at[pl.ds(dest_ref[0, 0, 2 * t + k], 1)], sem).start()
    for k in range(2):
        pltpu.make_async_copy(x_ref, o_hbm.at[pl.ds(0, DISPATCH_TILE)], sem).wait()


def _dispatch(zero_offsets, dest, x_packed, n_blocks):
    n_tokens, width = x_packed.shape
    grid_spec = pltpu.PrefetchScalarGridSpec(
        num_scalar_prefetch=1,
        grid=(n_tokens // DISPATCH_TILE,),
        in_specs=[pl.BlockSpec((1, 1, 2 * DISPATCH_TILE), lambda i, z: (i, 0, 0), memory_space=pltpu.SMEM),
                  pl.BlockSpec((DISPATCH_TILE, width), lambda i, z: (i, 0))],
        out_specs=pl.BlockSpec(memory_space=pl.ANY),
        scratch_shapes=[pltpu.VMEM((MOE_BLOCK, width), x_packed.dtype), pltpu.SemaphoreType.DMA,
                        pltpu.SemaphoreType.DMA],
    )
    return pl.pallas_call(
        _dispatch_kernel,
        out_shape=jax.ShapeDtypeStruct((n_blocks * MOE_BLOCK, width), x_packed.dtype),
        grid_spec=grid_spec,
        compiler_params=_cparams(("arbitrary",)),
        name="moe_dispatch",
    )(zero_offsets, dest.reshape(-1, 1, 2 * DISPATCH_TILE), x_packed)


def _expert_kernel(be_ref, nb_ref, x_ref, wg_ref, wu_ref, wd_ref, o_ref):
    blk = pl.program_id(0)

    @pl.when(blk < nb_ref[0])
    def _():
        half = D_MODEL // 2
        x_head, x_tail = _unpack_bf16_halves(x_ref[...])
        proj = lambda w_ref: (_dot(x_head, w_ref[0, 0:half, :].astype(BF16))
                              + _dot(x_tail, w_ref[0, half:D_MODEL, :].astype(BF16)))
        gate = proj(wg_ref)
        up = proj(wu_ref)
        h = gate * _sigmoid(gate) * up
        o_ref[...] = _dot(h.astype(BF16), wd_ref[0].astype(BF16))

    @pl.when(blk >= nb_ref[0])
    def _():
        o_ref[...] = jnp.zeros_like(o_ref)


def _expert_mlp(block_expert, n_used, x_sorted, w_gate, w_up, w_down, n_blocks):
    grid_spec = pltpu.PrefetchScalarGridSpec(
        num_scalar_prefetch=2,
        grid=(n_blocks,),
        in_specs=[pl.BlockSpec((MOE_BLOCK, D_MODEL // 2), lambda b, be, nb: (jnp.minimum(b, nb[0] - 1), 0)),
                  pl.BlockSpec((1, D_MODEL, D_EXPERT), lambda b, be, nb: (be[b], 0, 0)),
                  pl.BlockSpec((1, D_MODEL, D_EXPERT), lambda b, be, nb: (be[b], 0, 0)),
                  pl.BlockSpec((1, D_EXPERT, D_MODEL), lambda b, be, nb: (be[b], 0, 0))],
        out_specs=pl.BlockSpec((MOE_BLOCK, D_MODEL), lambda b, be, nb: (b, 0)),
    )
    return pl.pallas_call(
        _expert_kernel,
        out_shape=jax.ShapeDtypeStruct((n_blocks * MOE_BLOCK, D_MODEL), F32),
        grid_spec=grid_spec,
        compiler_params=_cparams(("arbitrary",)),
        name="expert_mlp",
    )(block_expert, n_used, x_sorted, w_gate, w_up, w_down)


COMBINE_TILE = 128


def _combine_kernel(dest_ref, dest_next_ref, y_hbm, x1_ref, route_ref, g_ref, b_ref, o_ref, ybuf, sem):
    i = pl.program_id(0)
    cur = lax.rem(i, 2)
    n_rows = 2 * COMBINE_TILE

    def gather(table_ref, buf):
        for slot in range(n_rows):
            pltpu.make_async_copy(y_hbm.at[pl.ds(table_ref[0, 0, slot], 1)], ybuf.at[buf, pl.ds(slot, 1)],
                                  sem.at[buf]).start()

    def wait_gather(buf):
        pltpu.make_async_copy(y_hbm.at[pl.ds(0, n_rows)], ybuf.at[buf], sem.at[buf]).wait()

    @pl.when(i == 0)
    def _():
        gather(dest_ref, 0)

    gather(dest_next_ref, 1 - cur)
    wait_gather(cur)
    route = route_ref[...]
    yb = ybuf[cur]
    moe = route[:, 2:3] * yb[0:COMBINE_TILE, :] + route[:, 3:4] * yb[COMBINE_TILE:n_rows, :]
    o_ref[...] = _layer_norm(ALPHA * x1_ref[...] + moe, g_ref[...], b_ref[...])

    @pl.when(i == pl.num_programs(0) - 1)
    def _():
        wait_gather(1 - cur)


def _combine(dest, y_slots, x1_all, route_all, m, row_block, ln_g, ln_b, name):
    tm = COMBINE_TILE
    return pl.pallas_call(
        _combine_kernel,
        out_shape=jax.ShapeDtypeStruct((m, D_MODEL), F32),
        grid=(m // tm,),
        in_specs=[pl.BlockSpec((1, 1, 2 * tm), lambda i: (i, 0, 0), memory_space=pltpu.SMEM),
                  pl.BlockSpec((1, 1, 2 * tm), lambda i: (i + 1, 0, 0), memory_space=pltpu.SMEM),
                  pl.BlockSpec(memory_space=pl.ANY),
                  pl.BlockSpec((tm, D_MODEL), lambda i: (i + row_block, 0)),
                  pl.BlockSpec((tm, LANES), lambda i: (i + row_block, 0)),
                  pl.BlockSpec((1, D_MODEL), lambda i: (0, 0)),
                  pl.BlockSpec((1, D_MODEL), lambda i: (0, 0))],
        out_specs=pl.BlockSpec((tm, D_MODEL), lambda i: (i, 0)),
        scratch_shapes=[pltpu.VMEM((2, 2 * tm, D_MODEL), F32), pltpu.SemaphoreType.DMA((2,))],
        compiler_params=_cparams(("arbitrary",)),
        name=name,
    )(dest, dest, y_slots, x1_all, route_all, ln_g, ln_b)


def _dispatch_plan(route_all, n_blocks):
    flat_e = route_all[:, 0:2].astype(jnp.int32).reshape(-1)
    onehot = (flat_e[:, None] == jnp.arange(N_EXPERTS, dtype=jnp.int32)[None, :]).astype(jnp.int32)
    csum = jnp.cumsum(onehot, axis=0)
    rank = jnp.sum(onehot * csum, axis=1) - 1
    counts = csum[-1]
    padded = (counts + MOE_BLOCK - 1) // MOE_BLOCK * MOE_BLOCK
    pend = jnp.cumsum(padded)
    pstart = pend - padded
    dest = (pstart[flat_e] + rank).astype(jnp.int32)
    zero_offsets = jnp.where(counts > 0, pend - MOE_BLOCK, -1).astype(jnp.int32)
    n_used = (pend[-1] // MOE_BLOCK).astype(jnp.int32)
    block_start = jnp.minimum(jnp.arange(n_blocks, dtype=jnp.int32), n_used - 1) * MOE_BLOCK
    block_e = jnp.minimum(jnp.searchsorted(pend, block_start, side="right"), N_EXPERTS - 1).astype(jnp.int32)
    return dest, jnp.concatenate([zero_offsets, n_used.reshape(1)]), block_e, n_used.reshape(1)


def kernel(x_prompt, x_sample, cache_k_win, cache_v_win, state_wkv, state_shift, w_in, attn_sinks, shift_mu, w0,
           w_decay_up, a0, w_a_up, w_g_up, k_k, k_a, r_k, gn_g, gn_b, w_out, ln1_g, ln1_b, w_coarse, b_coarse,
           w_fine, b_fine, w_exp_gate, w_exp_up, w_exp_down, ln2_g, ln2_b):
    xp = x_prompt[0]
    xs = x_sample[:, 0]
    row = lambda a: a.reshape(1, -1)

    w_in_t = jnp.swapaxes(w_in[0], 0, 1)
    prm = dict(mu=row(shift_mu[0, :D_RKV]), mu_tail=row(shift_mu[0, D_RKV:]), w_tail=w_in_t[D_MAIN:].astype(BF16),
               w0=row(w0[0]), a0=row(a0[0]), k_k=row(k_k[0]), k_a=row(k_a[0]),
               r_k=row(r_k[0]), gn_g=row(gn_g[0]), gn_b=row(gn_b[0]),
               wd=w_decay_up[0], wa=w_a_up[0], wg=w_g_up[0])
    sinks = attn_sinks[0]
    wo_bf16 = w_out[0].astype(BF16)
    w_route = jnp.pad(jnp.concatenate([w_coarse[0], w_fine[0]], axis=1), ((0, 0), (0, LANES - N_GROUPS - N_EXPERTS)))
    b_route = jnp.pad(jnp.concatenate([b_coarse[0], b_fine[0]]), (0, LANES - N_GROUPS - N_EXPERTS)).reshape(1, LANES)

    hp = _matmul(xp, w_in_t, D_MAIN, MAIN_TM, MAIN_TN, "in_proj_prompt")
    hs = _matmul(xs, w_in_t, D_MAIN, DEC_BATCH, MAIN_TN, "in_proj_sample")

    attn_p = _prompt_attention(hp, sinks)
    rwkv_p, state_p, tail_p = _prompt_rwkv(hp, xp, prm)

    q_s = hs[:, :D_ATTN].reshape(DEC_BATCH, N_Q_HEADS, HEAD_DIM)
    k_s = hs[:, D_ATTN:D_ATTN + D_KV].reshape(DEC_BATCH, 1, D_KV)
    v_s = hs[:, D_ATTN + D_KV:D_QKV].reshape(DEC_BATCH, 1, D_KV)
    window_t = lambda c: jnp.transpose(c, (0, 2, 3, 1)).reshape(DEC_BATCH, D_KV, WINDOW)
    attn_s, kwin_s, vwin_s = _sample_attention(
        q_s, k_s, v_s, window_t(cache_k_win[0]), window_t(cache_v_win[0]), sinks.reshape(N_Q_HEADS, 1))
    r_s, k2_s, vv_s, g_s, tail_s, r_t, w_t, k_t, v_t, a_t, b_t = _sample_prep(hs, xs, state_shift[0], prm)
    y_t, state_s = _sample_step(jnp.transpose(state_wkv[0], (1, 2, 3, 0)), r_t, w_t, k_t, a_t, b_t, v_t)
    state_s = state_s.reshape(DEC_BATCH, N_RWKV_HEADS, HEAD_DIM, HEAD_DIM)
    rwkv_s = _sample_post(y_t, r_s, k2_s, vv_s, g_s, prm)

    n_tokens = SEQ + DEC_BATCH
    outs_pr = _outproj_router(attn_p, rwkv_p, xp, wo_bf16, row(ln1_g[0]), row(ln1_b[0]), w_route, b_route,
                              256, n_tokens, 0, None, "outproj_router_prompt")
    x1_all, x1b_all, route_all = _outproj_router(attn_s.reshape(DEC_BATCH, D_ATTN), rwkv_s, xs,
                                                 wo_bf16, row(ln1_g[0]), row(ln1_b[0]), w_route, b_route,
                                                 DEC_BATCH, n_tokens, SEQ // DEC_BATCH, outs_pr,
                                                 "outproj_router_sample")

    n_assign = 2 * n_tokens
    n_blocks = -(-(n_assign + N_EXPERTS * (MOE_BLOCK - 1)) // MOE_BLOCK)
    dest, zero_offsets, block_e, n_used = _dispatch_plan(route_all, n_blocks)
    x_sorted = _dispatch(zero_offsets, dest, x1b_all, n_blocks)
    y_slots = _expert_mlp(block_e, n_used, x_sorted, w_exp_gate[0], w_exp_up[0], w_exp_down[0], n_blocks)

    def dest_tiles(d):
        d = d.reshape(-1, COMBINE_TILE, 2)
        d = jnp.concatenate([d[:, :, 0], d[:, :, 1]], axis=1)
        return jnp.pad(d, ((0, 1), (0, 0))).reshape(-1, 1, 2 * COMBINE_TILE)

    y_p = _combine(dest_tiles(dest[:2 * SEQ]), y_slots, x1_all, route_all, SEQ, 0, row(ln2_g[0]), row(ln2_b[0]),
                   "combine_prompt")
    y_s = _combine(dest_tiles(dest[2 * SEQ:]), y_slots, x1_all, route_all, DEC_BATCH, SEQ // COMBINE_TILE,
                   row(ln2_g[0]), row(ln2_b[0]), "combine_sample")

    kv4 = lambda a: a.reshape(a.shape[0], N_KV_HEADS, HEAD_DIM)
    k_win_p = kv4(hp[SEQ - WINDOW:, D_ATTN:D_ATTN + D_KV])[None, None]
    v_win_p = kv4(hp[SEQ - WINDOW:, D_ATTN + D_KV:D_QKV])[None, None]
    sp = state_p.reshape(N_PAIRS, HEADS_PER_TILE, HEAD_DIM, HEADS_PER_TILE, HEAD_DIM)
    wkv_p = jnp.stack([sp[:, i, :, i, :] for i in range(HEADS_PER_TILE)], axis=1)
    wkv_p = wkv_p.reshape(N_RWKV_HEADS, HEAD_DIM, HEAD_DIM).transpose(0, 2, 1)[None, None]
    shift_p = jnp.concatenate([hp[SEQ - 1:SEQ, D_QKV:], tail_p[0:1]], axis=1)[None]
    shift_s = jnp.concatenate([hs[:, D_QKV:], tail_s], axis=1)[None]
    return (y_p[None], y_s[:, None, :], k_win_p, v_win_p, wkv_p, shift_p,
            kwin_s.reshape(1, DEC_BATCH, WINDOW, N_KV_HEADS, HEAD_DIM),
            vwin_s.reshape(1, DEC_BATCH, WINDOW, N_KV_HEADS, HEAD_DIM),
            state_s[None], shift_s)
```

```python
import functools
import math

import jax
import jax.numpy as jnp
from jax import lax
from jax.experimental import pallas as pl
from jax.experimental.pallas import tpu as pltpu

F32 = jnp.float32
BF16 = jnp.bfloat16

D_MODEL = 2048
SEQ = 8192
DEC_BATCH = 128
HEAD_DIM = 64
D_ATTN = 1024
D_RWKV = 1024
N_Q_HEADS = 16
N_KV_HEADS = 4
Q_PER_KV = 4
D_KV = 256
WINDOW = 128
ATTN_SCALE = HEAD_DIM ** -0.5
N_RWKV_HEADS = 16
W_LORA = 64
A_LORA = 64
G_LORA = 160
D_SHIFT = 3 * D_RWKV + W_LORA + A_LORA + G_LORA
D_QKV = D_ATTN + 2 * D_KV
N_GROUPS = 4
EXPERTS_PER_GROUP = 8
N_EXPERTS = 32
D_EXPERT = 512
ALPHA = 2.0 ** 0.25
LN_EPS = 1e-5
GN_EPS = 64e-5

SUBLANES = 8
LANES = 128
VMEM_LIMIT = 52 * 1024 * 1024

D_RKV = 3 * D_RWKV
D_TAIL = W_LORA + A_LORA + G_LORA
D_MAIN = D_QKV + D_RKV
MAIN_TN = 1536
MAIN_TM = 512

CHUNK = 64
HEADS_PER_TILE = LANES // HEAD_DIM
N_PAIRS = N_RWKV_HEADS // HEADS_PER_TILE
SOLVE_LEVELS = int(math.log2(CHUNK))
PAIR_GROUP = 8

MOE_BLOCK = 256
ROUTE_FINE_OFF = N_GROUPS

NN = (((1,), (0,)), ((), ()))
NT = (((1,), (1,)), ((), ()))


def _dot(a, b, dims=NN):
    return lax.dot_general(a, b, dims, preferred_element_type=F32)


def _dot1(a, b, dims=NN):
    return _dot(a.astype(BF16), b.astype(BF16), dims)


def _split(x):
    hi = x.astype(BF16)
    lo = (x - hi.astype(F32)).astype(BF16)
    return hi, lo


def _dot3(a, b, dims=NN):
    ah, al = _split(a)
    bh, bl = _split(b)
    return _dot(ah, bh, dims) + (_dot(ah, bl, dims) + _dot(al, bh, dims))


def _dot_exact_lhs(a_bf16, b, dims=NN):
    bh, bl = _split(b)
    return _dot(a_bf16, bh, dims) + _dot(a_bf16, bl, dims)


def _dot_exact_rhs(a, b_bf16, dims=NN):
    ah, al = _split(a)
    return _dot(ah, b_bf16, dims) + _dot(al, b_bf16, dims)


def _div_pow2(x, d):
    return lax.shift_right_logical(x, jnp.int32(int(math.log2(d))))


def _mod_pow2(x, d):
    return lax.bitwise_and(x, jnp.int32(d - 1))


def _pack_bf16_halves(x_bf16):
    n = x_bf16.shape[1] // 2
    bits = lax.bitcast_convert_type(x_bf16.astype(F32), jnp.uint32)
    return lax.bitwise_or(bits[:, 0:n], lax.shift_right_logical(bits[:, n:2 * n], jnp.uint32(16)))


def _unpack_bf16_halves(packed):
    hi = lax.bitcast_convert_type(lax.bitwise_and(packed, jnp.uint32(0xFFFF0000)), F32)
    lo = lax.bitcast_convert_type(lax.shift_left(packed, jnp.uint32(16)), F32)
    return hi.astype(BF16), lo.astype(BF16)


def _sigmoid(x):
    return 1.0 / (1.0 + jnp.exp(-x))


def _softplus(x):
    return jnp.maximum(x, 0.0) + jnp.log(1.0 + jnp.exp(-jnp.abs(x)))


def _layer_norm(z, g, b):
    mu = jnp.mean(z, axis=-1, keepdims=True)
    d = z - mu
    var = jnp.mean(d * d, axis=-1, keepdims=True)
    return d * lax.rsqrt(var + LN_EPS) * g + b


def _cparams(sem):
    return pltpu.CompilerParams(dimension_semantics=sem, vmem_limit_bytes=VMEM_LIMIT)


def _matmul_kernel(x_ref, wt_ref, o_ref):
    o_ref[...] = _dot(x_ref[...].astype(BF16), wt_ref[...].astype(BF16), NT)


def _matmul(x, w_t, n_out, tm, tn, name):
    m, k = x.shape
    tm = min(tm, m)
    return pl.pallas_call(
        _matmul_kernel,
        out_shape=jax.ShapeDtypeStruct((m, n_out), F32),
        grid=(n_out // tn, m // tm),
        in_specs=[pl.BlockSpec((tm, k), lambda j, i: (i, 0)),
                  pl.BlockSpec((tn, k), lambda j, i: (j, 0))],
        out_specs=pl.BlockSpec((tm, tn), lambda j, i: (i, j)),
        compiler_params=_cparams(("arbitrary", "arbitrary")),
        name=name,
    )(x, w_t)


def _prompt_attn_kernel(q_ref, kvp_ref, kvc_ref, sink_ref, o_ref):
    blk = pl.program_id(0)
    q = q_ref[...]
    kv_prev = kvp_ref[...]
    kv_cur = kvc_ref[...]
    qi = _mod_pow2(lax.broadcasted_iota(jnp.int32, (Q_PER_KV * WINDOW, 2 * WINDOW), 0), WINDOW)
    kj = lax.broadcasted_iota(jnp.int32, (Q_PER_KV * WINDOW, 2 * WINDOW), 1)
    diff = qi + WINDOW - kj
    mask = (diff >= 0) & (diff <= WINDOW) & ((blk > 0) | (kj >= WINDOW))
    row_head = _div_pow2(lax.broadcasted_iota(jnp.int32, (Q_PER_KV * WINDOW, 1), 0), WINDOW)
    groups = range(N_KV_HEADS)
    kv_cols = lambda off, g: jnp.concatenate([kv_prev[:, off + g * HEAD_DIM:off + (g + 1) * HEAD_DIM],
                                              kv_cur[:, off + g * HEAD_DIM:off + (g + 1) * HEAD_DIM]],
                                             axis=0).astype(BF16)
    q_rows = lambda g: jnp.concatenate(
        [q[:, (g * Q_PER_KV + h) * HEAD_DIM:(g * Q_PER_KV + h + 1) * HEAD_DIM] for h in range(Q_PER_KV)],
        axis=0).astype(BF16)
    s = [jnp.where(mask, _dot(q_rows(g), kv_cols(0, g), NT) * ATTN_SCALE, -jnp.inf) for g in groups]
    sink = []
    for g in groups:
        col = jnp.zeros((Q_PER_KV * WINDOW, 1), F32)
        for h in range(Q_PER_KV):
            col = jnp.where(row_head == h, sink_ref[g * Q_PER_KV + h], col)
        sink.append(col)
    m = [jnp.maximum(jnp.max(s[g], axis=-1, keepdims=True), sink[g]) for g in groups]
    p = [jnp.exp(s[g] - m[g]) for g in groups]
    denom = [jnp.sum(p[g], axis=-1, keepdims=True) + jnp.exp(sink[g] - m[g]) for g in groups]
    o = [_dot((p[g] / denom[g]).astype(BF16), kv_cols(D_KV, g)) for g in groups]
    o_ref[...] = jnp.concatenate([o[g][h * WINDOW:(h + 1) * WINDOW, :] for g in groups for h in range(Q_PER_KV)],
                                 axis=1)


def _prompt_attention(h_attn, sinks):
    nb = SEQ // WINDOW
    return pl.pallas_call(
        _prompt_attn_kernel,
        out_shape=jax.ShapeDtypeStruct((SEQ, D_ATTN), F32),
        grid=(nb,),
        in_specs=[pl.BlockSpec((WINDOW, D_ATTN), lambda i: (i, 0)),
                  pl.BlockSpec((WINDOW, 2 * D_KV), lambda i: (jnp.maximum(i - 1, 0), 2)),
                  pl.BlockSpec((WINDOW, 2 * D_KV), lambda i: (i, 2)),
                  pl.BlockSpec(memory_space=pltpu.SMEM)],
        out_specs=pl.BlockSpec((WINDOW, D_ATTN), lambda i: (i, 0)),
        compiler_params=_cparams(("arbitrary",)),
        name="prompt_attention",
    )(h_attn, h_attn, h_attn, sinks)


SAMPLE_ATTN_TILE = 8


def _sample_attn_kernel(q_ref, knew_ref, vnew_ref, ck_ref, cv_ref, sink_ref, o_ref, kwin_ref, vwin_ref):
    lane = lax.broadcasted_iota(jnp.int32, (N_Q_HEADS, D_KV), 1)
    head = lax.broadcasted_iota(jnp.int32, (N_Q_HEADS, D_KV), 0)
    group_mask = _div_pow2(lane, HEAD_DIM) == _div_pow2(head, Q_PER_KV)
    sink = sink_ref[...]
    row = lax.broadcasted_iota(jnp.int32, (WINDOW, D_KV), 0)
    for b in range(SAMPLE_ATTN_TILE):
        q = q_ref[b]
        qbd = jnp.where(group_mask, jnp.concatenate([q] * N_KV_HEADS, axis=1), 0.0).astype(BF16)
        k_t = ck_ref[b]
        kb = k_t.T
        vb = cv_ref[b].T
        kn = knew_ref[b]
        vn = vnew_ref[b]
        s = _dot1(qbd, k_t) * ATTN_SCALE
        s_new = jnp.sum(qbd.astype(F32) * kn.astype(BF16).astype(F32), axis=-1, keepdims=True) * ATTN_SCALE
        m = jnp.maximum(jnp.maximum(jnp.max(s, axis=-1, keepdims=True), s_new), sink)
        p = jnp.exp(s - m)
        p_new = jnp.exp(s_new - m)
        denom = jnp.sum(p, axis=-1, keepdims=True) + p_new + jnp.exp(sink - m)
        p = p / denom
        p_new = (p_new / denom).astype(BF16).astype(F32)
        o_full = _dot1(p, vb) + p_new * vn.astype(BF16).astype(F32)
        o_full = jnp.where(group_mask, o_full, 0.0)
        o = o_full[:, 0:HEAD_DIM]
        for g in range(1, N_KV_HEADS):
            o = o + o_full[:, g * HEAD_DIM:(g + 1) * HEAD_DIM]
        o_ref[b] = o
        kwin_ref[b] = jnp.where(row == WINDOW - 1, kn, pltpu.roll(kb, WINDOW - 1, axis=0))
        vwin_ref[b] = jnp.where(row == WINDOW - 1, vn, pltpu.roll(vb, WINDOW - 1, axis=0))


def _sample_attention(q, k_new, v_new, cache_k_t, cache_v_t, sinks):
    bt = SAMPLE_ATTN_TILE
    win_spec = pl.BlockSpec((bt, WINDOW, D_KV), lambda i: (i, 0, 0))
    win_t_spec = pl.BlockSpec((bt, D_KV, WINDOW), lambda i: (i, 0, 0))
    new_spec = pl.BlockSpec((bt, 1, D_KV), lambda i: (i, 0, 0))
    return pl.pallas_call(
        _sample_attn_kernel,
        out_shape=(jax.ShapeDtypeStruct((DEC_BATCH, N_Q_HEADS, HEAD_DIM), F32),
                   jax.ShapeDtypeStruct((DEC_BATCH, WINDOW, D_KV), F32),
                   jax.ShapeDtypeStruct((DEC_BATCH, WINDOW, D_KV), F32)),
        grid=(DEC_BATCH // bt,),
        in_specs=[pl.BlockSpec((bt, N_Q_HEADS, HEAD_DIM), lambda i: (i, 0, 0)),
                  new_spec, new_spec, win_t_spec, win_t_spec,
                  pl.BlockSpec((N_Q_HEADS, 1), lambda i: (0, 0))],
        out_specs=(pl.BlockSpec((bt, N_Q_HEADS, HEAD_DIM), lambda i: (i, 0, 0)), win_spec, win_spec),
        compiler_params=_cparams(("arbitrary",)),
        name="sample_attention",
    )(q, k_new, v_new, cache_k_t, cache_v_t, sinks)


def _head_ones():
    r = _div_pow2(lax.broadcasted_iota(jnp.int32, (LANES, LANES), 0), HEAD_DIM)
    c = _div_pow2(lax.broadcasted_iota(jnp.int32, (LANES, LANES), 1), HEAD_DIM)
    return jnp.where(r == c, 1.0, 0.0).astype(BF16)


def _head_sum(x, ones):
    parts = [_dot_exact_rhs(x[:, p * LANES:(p + 1) * LANES], ones) for p in range(x.shape[1] // LANES)]
    return jnp.concatenate(parts, axis=1)


def _token_mix(feat, shifted, mu):
    return feat + (shifted - feat) * mu


def _rwkv_prep(mixed, mixed_tail, w0, a0, k_k, k_a, wd, wa, wg, ones):
    r = mixed[:, 0:D_RWKV]
    k = mixed[:, D_RWKV:2 * D_RWKV]
    v = mixed[:, 2 * D_RWKV:3 * D_RWKV]
    xw = mixed_tail[:, 0:W_LORA]
    xa = mixed_tail[:, W_LORA:W_LORA + A_LORA]
    xg = mixed_tail[:, W_LORA + A_LORA:D_TAIL]
    w_log = -_softplus(-(w0 + _dot1(jnp.tanh(xw), wd))) - 0.5
    log_decay = -jnp.exp(w_log)
    a = _sigmoid(a0 + _dot1(xa, wa))
    g = _dot1(_sigmoid(xg), wg)
    kk = k * k_k
    kk = kk * lax.rsqrt(jnp.maximum(_head_sum(kk * kk, ones), 1e-24))
    k2 = k * (1.0 + (a - 1.0) * k_a)
    return r, log_decay, k2, v, -kk, kk * a, g


def _rwkv_post(y, r, k2, v, g, r_k, gn_g, gn_b, ones):
    inv_n = 1.0 / HEAD_DIM
    mu = _head_sum(y, ones) * inv_n
    d = y - mu
    var = _head_sum(d * d, ones) * inv_n
    yn = d * lax.rsqrt(var + GN_EPS) * gn_g + gn_b
    bonus = _head_sum(r * k2 * r_k, ones) * v
    return (yn + bonus) * g


(OP_AABS, OP_RABS, OP_AN, OP_RN, OP_BN, OP_KN, OP_BH, OP_KH, OP_V) = range(9)
N_OPS = 9


def _prompt_rwkv_kernel(f1_ref, f2_ref, x_ref, wt_ref, mu_ref, mut_ref, w0_ref, a0_ref, kk_ref, ka_ref, rk_ref,
                        gng_ref, gnb_ref, wd_ref, wa_ref, wg_ref, out_ref, state_ref, tail_ref,
                        prev_ref, prevt_ref, s_ref, ops_ref, pc_ref, y_ref):
    c = pl.program_id(0)
    C = CHUNK

    @pl.when(c == 0)
    def _():
        prev_ref[...] = jnp.zeros_like(prev_ref)
        prevt_ref[...] = jnp.zeros_like(prevt_ref)
        s_ref[...] = jnp.zeros_like(s_ref)

    ones = _head_ones()
    row = lax.broadcasted_iota(jnp.int32, (C, 1), 0)

    def token_shift(feat, carry_ref):
        shifted = jnp.where(row == 0, carry_ref[0:1, :], pltpu.roll(feat, 1, axis=0))
        carry_ref[0:1, :] = feat[C - 1:C, :]
        return shifted

    feat = jnp.concatenate([f1_ref[...], f2_ref[...]], axis=1)
    tail = _dot1(x_ref[...], wt_ref[...], NT)
    mixed = _token_mix(feat, token_shift(feat, prev_ref), mu_ref[...])
    mixed_tail = _token_mix(tail, token_shift(tail, prevt_ref), mut_ref[...])
    tail_ref[...] = prevt_ref[...]
    r, ld, k2, v, av, bv, g = _rwkv_prep(mixed, mixed_tail, w0_ref[...], a0_ref[...], kk_ref[...], ka_ref[...],
                                         wd_ref[...], wa_ref[...], wg_ref[...], ones)

    ti = lax.broadcasted_iota(jnp.int32, (C, C), 0)
    tj = lax.broadcasted_iota(jnp.int32, (C, C), 1)
    tri_incl = jnp.where(tj <= ti, 1.0, 0.0).astype(BF16)
    cs = _dot_exact_lhs(tri_incl, ld)
    cs_ref = cs[C // 2 - 1:C // 2, :]
    cs_end = cs[C - 1:C, :]
    e_prev = jnp.exp(cs - ld)
    e_cur = jnp.exp(cs)
    n_prev = jnp.exp(cs - ld - cs_ref)
    n_cur = jnp.exp(cs - cs_ref)
    n_inv = jnp.exp(cs_ref - cs)
    e_tail = jnp.exp(cs_end - cs)
    ops = {OP_AABS: av * e_prev, OP_RABS: r * e_cur, OP_AN: av * n_prev, OP_RN: r * n_cur,
           OP_BN: bv * n_inv, OP_KN: k2 * n_inv, OP_BH: bv * e_tail, OP_KH: k2 * e_tail, OP_V: v}
    p_end = jnp.exp(cs_end)
    for p in range(N_PAIRS):
        sl = slice(p * LANES, (p + 1) * LANES)
        for idx, val in ops.items():
            ops_ref[p, idx] = val[:, sl]
        pc_ref[p] = jnp.broadcast_to(p_end[:, sl], (SUBLANES, LANES))

    lane1 = lax.broadcasted_iota(jnp.int32, (C, LANES), 1)
    head0 = lane1 < HEAD_DIM
    r2 = lax.broadcasted_iota(jnp.int32, (2 * C, 2 * C), 0)
    c2 = lax.broadcasted_iota(jnp.int32, (2 * C, 2 * C), 1)
    tq = _mod_pow2(r2, C)
    tk = _mod_pow2(c2, C)
    band = (tk < tq) | ((tk == tq) & (r2 >= C))
    blockdiag = _div_pow2(r2, HEAD_DIM) == _div_pow2(c2, HEAD_DIM)

    op = lambda p, idx: ops_ref[p, idx]
    zero_half = jnp.zeros((C, LANES), F32)
    for pairs in [range(g, g + PAIR_GROUP) for g in range(0, N_PAIRS, PAIR_GROUP)]:
        gy = {p: _dot1(jnp.concatenate([op(p, OP_AABS), op(p, OP_RABS)], axis=0), s_ref[p]) for p in pairs}

        am0, am1 = {}, {}
        for p in pairs:
            a_n, r_n = op(p, OP_AN), op(p, OP_RN)
            b0, k0 = jnp.where(head0, op(p, OP_BN), 0.0), jnp.where(head0, op(p, OP_KN), 0.0)
            b1, k1 = jnp.where(head0, 0.0, op(p, OP_BN)), jnp.where(head0, 0.0, op(p, OP_KN))
            am = _dot1(jnp.concatenate([a_n, r_n], axis=0), jnp.concatenate([k0, b0, b1, k1], axis=0), NT)
            am0[p] = jnp.where(band, am[:, 0:2 * C], 0.0)
            am1[p] = jnp.where(band, am[:, 2 * C:4 * C], 0.0)

        w0, w1 = {}, {}
        for p in pairs:
            top0, top1 = am0[p][0:C], am1[p][0:C]
            ak = jnp.concatenate([jnp.where(head0, top0, 0.0), jnp.where(head0, 0.0, top1)], axis=0)
            vv = op(p, OP_V)
            g0 = gy[p][0:C]
            m = jnp.concatenate([g0, g0], axis=0) + _dot1(ak, jnp.concatenate([vv, vv], axis=0))
            w0[p] = jnp.where(head0, m[0:C], top0)
            w1[p] = jnp.where(head0, top1, m[C:2 * C])

        for lvl in range(SOLVE_LEVELS):
            prod0 = {p: _dot1(w0[p], jnp.concatenate([zero_half, w0[p]], axis=0)) for p in pairs}
            prod1 = {p: _dot1(w1[p], jnp.concatenate([w1[p], zero_half], axis=0)) for p in pairs}
            w0 = {p: jnp.where(head0, w0[p] + prod0[p], prod0[p]) for p in pairs}
            w1 = {p: jnp.where(head0, prod1[p], w1[p] + prod1[p]) for p in pairs}
        u = {p: jnp.where(head0, w0[p], w1[p]) for p in pairs}

        for p in pairs:
            vv = op(p, OP_V)
            y_lhs = jnp.concatenate([am0[p][C:2 * C], am1[p][C:2 * C]], axis=1)
            y_rhs = jnp.concatenate([jnp.where(head0, vv, 0.0), jnp.where(head0, u[p], 0.0),
                                     jnp.where(head0, 0.0, u[p]), jnp.where(head0, 0.0, vv)], axis=0)
            y_ref[p] = gy[p][C:2 * C] + _dot1(y_lhs, y_rhs)

        for p in pairs:
            decay_rows = jnp.broadcast_to(pc_ref[p][0:1, :], (LANES, LANES)).T
            upd_lhs = jnp.concatenate([op(p, OP_BH), op(p, OP_KH)], axis=0).T
            upd_rhs = jnp.concatenate([u[p], op(p, OP_V)], axis=0)
            s_ref[p] = s_ref[p] * decay_rows + jnp.where(blockdiag, _dot1(upd_lhs, upd_rhs), 0.0)

    y = jnp.concatenate([y_ref[p] for p in range(N_PAIRS)], axis=1)
    out_ref[...] = _rwkv_post(y, r, k2, v, g, rk_ref[...], gng_ref[...], gnb_ref[...], ones)

    @pl.when(c == pl.num_programs(0) - 1)
    def _():
        state_ref[...] = s_ref[...]


def _prompt_rwkv(h_main, x, prm):
    n_chunks = SEQ // CHUNK
    half = D_RKV // 2
    assert D_QKV == half
    vec = pl.BlockSpec((1, D_RWKV), lambda c: (0, 0))
    full = lambda a: pl.BlockSpec(a.shape, lambda c: (0,) * a.ndim)
    return pl.pallas_call(
        _prompt_rwkv_kernel,
        out_shape=(jax.ShapeDtypeStruct((SEQ, D_RWKV), F32),
                   jax.ShapeDtypeStruct((N_PAIRS, LANES, LANES), F32),
                   jax.ShapeDtypeStruct((SUBLANES, D_TAIL), F32)),
        grid=(n_chunks,),
        in_specs=[pl.BlockSpec((CHUNK, half), lambda c: (c, 1)),
                  pl.BlockSpec((CHUNK, half), lambda c: (c, 2)),
                  pl.BlockSpec((CHUNK, D_MODEL), lambda c: (c, 0)),
                  full(prm["w_tail"]), full(prm["mu"]), full(prm["mu_tail"]),
                  vec, vec, vec, vec, vec, vec, vec,
                  full(prm["wd"]), full(prm["wa"]), full(prm["wg"])],
        out_specs=(pl.BlockSpec((CHUNK, D_RWKV), lambda c: (c, 0)),
                   pl.BlockSpec((N_PAIRS, LANES, LANES), lambda c: (0, 0, 0)),
                   pl.BlockSpec((SUBLANES, D_TAIL), lambda c: (0, 0))),
        scratch_shapes=[pltpu.VMEM((SUBLANES, D_RKV), F32),
                        pltpu.VMEM((SUBLANES, D_TAIL), F32),
                        pltpu.VMEM((N_PAIRS, LANES, LANES), F32),
                        pltpu.VMEM((N_PAIRS, N_OPS, CHUNK, LANES), F32),
                        pltpu.VMEM((N_PAIRS, SUBLANES, LANES), F32),
                        pltpu.VMEM((N_PAIRS, CHUNK, LANES), F32)],
        compiler_params=_cparams(("arbitrary",)),
        name="prompt_rwkv",
    )(h_main, h_main, x, prm["w_tail"], prm["mu"], prm["mu_tail"], prm["w0"], prm["a0"], prm["k_k"], prm["k_a"],
      prm["r_k"], prm["gn_g"], prm["gn_b"], prm["wd"], prm["wa"], prm["wg"])


def _sample_prep_kernel(h_ref, x_ref, wt_ref, shift_ref, mu_ref, mut_ref, w0_ref, a0_ref, kk_ref, ka_ref,
                        wd_ref, wa_ref, wg_ref, r_ref, k_ref, v_ref, g_ref, tail_ref,
                        rt_ref, wtr_ref, kt_ref, vt_ref, at_ref, bt_ref):
    ones = _head_ones()
    feat = h_ref[:, D_QKV:D_MAIN]
    tail = _dot1(x_ref[...], wt_ref[...], NT)
    tail_ref[...] = tail
    mixed = _token_mix(feat, shift_ref[:, 0:D_RKV], mu_ref[...])
    mixed_tail = _token_mix(tail, shift_ref[:, D_RKV:D_SHIFT], mut_ref[...])
    r, ld, k2, v, av, bv, g = _rwkv_prep(mixed, mixed_tail, w0_ref[...], a0_ref[...], kk_ref[...], ka_ref[...],
                                         wd_ref[...], wa_ref[...], wg_ref[...], ones)
    r_ref[...] = r
    k_ref[...] = k2
    v_ref[...] = v
    g_ref[...] = g
    rt_ref[...] = r.T
    wtr_ref[...] = jnp.exp(ld).T
    kt_ref[...] = k2.T
    vt_ref[...] = v.T
    at_ref[...] = av.T
    bt_ref[...] = bv.T


def _sample_prep(h_main, x, shift, prm):
    tok = jax.ShapeDtypeStruct((DEC_BATCH, D_RWKV), F32)
    chan = jax.ShapeDtypeStruct((D_RWKV, DEC_BATCH), F32)
    return pl.pallas_call(
        _sample_prep_kernel,
        out_shape=(tok,) * 4 + (jax.ShapeDtypeStruct((DEC_BATCH, D_TAIL), F32),) + (chan,) * 6,
        compiler_params=pltpu.CompilerParams(vmem_limit_bytes=VMEM_LIMIT),
        name="sample_rwkv_prep",
    )(h_main, x, prm["w_tail"], shift, prm["mu"], prm["mu_tail"], prm["w0"], prm["a0"], prm["k_k"], prm["k_a"],
      prm["wd"], prm["wa"], prm["wg"])


STEP_GROUP = 4


def _sample_step_kernel(s_ref, r_ref, w_ref, k_ref, a_ref, b_ref, v_ref, y_ref, snew_ref):
    r, w, k, a, b = r_ref[...], w_ref[...], k_ref[...], a_ref[...], b_ref[...]
    for g0 in range(0, HEAD_DIM, 2 * STEP_GROUP):
        chans = range(g0, g0 + 2 * STEP_GROUP)
        sa = {i: jnp.sum(s_ref[0, i] * a, axis=0, keepdims=True) for i in chans}
        s_new = {i: s_ref[0, i] * w + sa[i] * b + v_ref[i:i + 1, :] * k for i in chans}
        for i in chans:
            y_ref[i:i + 1, :] = jnp.sum(s_new[i] * r, axis=0, keepdims=True)
        for i in range(g0, g0 + 2 * STEP_GROUP, 2):
            pair = jnp.concatenate([s_new[i], s_new[i + 1]], axis=0)
            snew_ref[:, i * HEAD_DIM:(i + 2) * HEAD_DIM] = pair.T


def _sample_step(state_t, r_t, w_t, k_t, a_t, b_t, v_t):
    head_rows = pl.BlockSpec((HEAD_DIM, DEC_BATCH), lambda h: (h, 0))
    return pl.pallas_call(
        _sample_step_kernel,
        out_shape=(jax.ShapeDtypeStruct((D_RWKV, DEC_BATCH), F32),
                   jax.ShapeDtypeStruct((DEC_BATCH, N_RWKV_HEADS * HEAD_DIM * HEAD_DIM), F32)),
        grid=(N_RWKV_HEADS,),
        in_specs=[pl.BlockSpec((1, HEAD_DIM, HEAD_DIM, DEC_BATCH), lambda h: (h, 0, 0, 0))] + [head_rows] * 6,
        out_specs=(head_rows, pl.BlockSpec((DEC_BATCH, HEAD_DIM * HEAD_DIM), lambda h: (0, h))),
        compiler_params=_cparams(("arbitrary",)),
        name="sample_rwkv_step",
    )(state_t, r_t, w_t, k_t, a_t, b_t, v_t)


def _sample_post_kernel(yt_ref, r_ref, k_ref, v_ref, g_ref, rk_ref, gng_ref, gnb_ref, o_ref):
    o_ref[...] = _rwkv_post(yt_ref[...].T, r_ref[...], k_ref[...], v_ref[...], g_ref[...], rk_ref[...],
                            gng_ref[...], gnb_ref[...], _head_ones())


def _sample_post(y, r, k, v, g, prm):
    return pl.pallas_call(
        _sample_post_kernel,
        out_shape=jax.ShapeDtypeStruct((DEC_BATCH, D_RWKV), F32),
        compiler_params=pltpu.CompilerParams(vmem_limit_bytes=VMEM_LIMIT),
        name="sample_rwkv_post",
    )(y, r, k, v, g, prm["r_k"], prm["gn_g"], prm["gn_b"])


def _outproj_router_kernel(attn_ref, rwkv_ref, x_ref, wo_ref, g_ref, b_ref, wr_ref, br_ref,
                           x1_ref, x1b_ref, route_ref):
    mix = _dot(attn_ref[...].astype(BF16), wo_ref[0:D_ATTN, :]) + _dot(rwkv_ref[...].astype(BF16),
                                                                        wo_ref[D_ATTN:D_ATTN + D_RWKV, :])
    x1 = _layer_norm(ALPHA * x_ref[...] + mix, g_ref[...], b_ref[...])
    x1_ref[...] = x1
    x1b = x1.astype(BF16)
    x1b_ref[...] = _pack_bf16_halves(x1b)
    logits = _dot(x1b, wr_ref[...].astype(BF16)) + br_ref[...]
    tm = logits.shape[0]
    lane = lax.broadcasted_iota(jnp.int32, (tm, LANES), 1).astype(F32)
    big = float(2 * LANES)
    neg = -jnp.inf
    lc = jnp.where(lane < N_GROUPS, logits, neg)
    mc = jnp.max(lc, axis=-1, keepdims=True)
    g_sel = jnp.min(jnp.where(lc == mc, lane, big), axis=-1, keepdims=True)
    p_group = 1.0 / jnp.sum(jnp.exp(lc - mc), axis=-1, keepdims=True)
    lo = ROUTE_FINE_OFF + g_sel * EXPERTS_PER_GROUP
    lf = jnp.where((lane >= lo) & (lane < lo + EXPERTS_PER_GROUP), logits, neg)
    v1 = jnp.max(lf, axis=-1, keepdims=True)
    i1 = jnp.min(jnp.where(lf == v1, lane, big), axis=-1, keepdims=True)
    lf2 = jnp.where(lane == i1, neg, lf)
    v2 = jnp.max(lf2, axis=-1, keepdims=True)
    i2 = jnp.min(jnp.where(lf2 == v2, lane, big), axis=-1, keepdims=True)
    e21 = jnp.exp(v2 - v1)
    gate1 = p_group / (1.0 + e21)
    gate2 = p_group * e21 / (1.0 + e21)
    route = jnp.where(lane == 0, i1 - ROUTE_FINE_OFF,
                      jnp.where(lane == 1, i2 - ROUTE_FINE_OFF,
                                jnp.where(lane == 2, gate1, jnp.where(lane == 3, gate2, 0.0))))
    route_ref[...] = route


N_ROUTER_OUTS = 3


def _outproj_router_into_kernel(*refs):
    _outproj_router_kernel(*refs[:-2 * N_ROUTER_OUTS], *refs[-N_ROUTER_OUTS:])


def _outproj_router_fill_kernel(n_steps, *refs):
    @pl.when(pl.program_id(0) < n_steps)
    def _():
        _outproj_router_kernel(*refs)

    @pl.when(pl.program_id(0) >= n_steps)
    def _():
        for out_ref in refs[-N_ROUTER_OUTS:]:
            out_ref[...] = jnp.zeros_like(out_ref)


def _outproj_router(attn, rwkv, x, wo_bf16, ln_g, ln_b, w_route, b_route, tm, n_total, row_block, into, name):
    m = x.shape[0]
    n_steps = m // tm
    const = lambda shape: pl.BlockSpec(shape, lambda i: (0, 0))
    rows = lambda width: pl.BlockSpec((tm, width), lambda i: (jnp.minimum(i, n_steps - 1), 0))
    in_specs = [rows(D_ATTN), rows(D_RWKV), rows(D_MODEL),
                const((D_MODEL, D_MODEL)), const((1, D_MODEL)), const((1, D_MODEL)),
                const((D_MODEL, LANES)), const((1, LANES))]
    args = [attn, rwkv, x, wo_bf16, ln_g, ln_b, w_route, b_route]
    aliases = {}
    if into is not None:
        in_specs += [pl.BlockSpec(memory_space=pl.ANY)] * N_ROUTER_OUTS
        aliases = {len(args) + k: k for k in range(N_ROUTER_OUTS)}
        args += list(into)
        body, grid_steps = _outproj_router_into_kernel, n_steps
    else:
        body, grid_steps = functools.partial(_outproj_router_fill_kernel, n_steps), pl.cdiv(n_total, tm)
    out_rows = lambda width: pl.BlockSpec((tm, width), lambda i: (i + row_block, 0))
    return pl.pallas_call(
        body,
        out_shape=(jax.ShapeDtypeStruct((n_total, D_MODEL), F32),
                   jax.ShapeDtypeStruct((n_total, D_MODEL // 2), jnp.uint32),
                   jax.ShapeDtypeStruct((n_total, LANES), F32)),
        grid=(grid_steps,),
        in_specs=in_specs,
        out_specs=(out_rows(D_MODEL), out_rows(D_MODEL // 2), out_rows(LANES)),
        input_output_aliases=aliases,
        compiler_params=_cparams(("arbitrary",)),
        name=name,
    )(*args)


DISPATCH_TILE = 128


def _dispatch_kernel(zoff_ref, dest_ref, x_ref, o_hbm, zbuf, ring, zsem, sem):
    i = pl.program_id(0)
    n_blocks = o_hbm.shape[0] // MOE_BLOCK
    n_used = zoff_ref[N_EXPERTS]

    def zero_fill(start_row):
        start_row = pl.multiple_of(start_row, MOE_BLOCK)
        return pltpu.make_async_copy(zbuf, o_hbm.at[pl.ds(start_row, MOE_BLOCK)], zsem)

    def zero_fills(action):
        for e in range(N_EXPERTS):
            @pl.when(zoff_ref[e] >= 0)
            def _():
                action(zero_fill(zoff_ref[e]))
        for b in range(n_blocks):
            @pl.when(b >= n_used)
            def _():
                action(zero_fill(b * MOE_BLOCK))

    @pl.when(i == 0)
    def _():
        zbuf[...] = jnp.zeros_like(zbuf)
        zero_fills(lambda copy: copy.start())
        zero_fills(lambda copy: copy.wait())

    cur = lax.rem(i, 2)

    def wait_rows(slot):
        for k in range(2):
            pltpu.make_async_copy(ring.at[slot], o_hbm.at[pl.ds(0, DISPATCH_TILE)], sem.at[slot]).wait()

    @pl.when(i >= 2)
    def _():
        wait_rows(cur)

    ring[cur] = x_ref[...]
    for t in range(DISPATCH_TILE):
        for k in range(2):
            pltpu.make_async_copy(ring.at[cur, pl.ds(t, 1)], o_hbm.at[pl.ds(dest_ref[0, 0, 2 * t + k], 1)],
                                  sem.at[cur]).start()

    @pl.when(i == pl.num_programs(0) - 1)
    def _():
        wait_rows(cur)

        @pl.when(i >= 1)
        def _():
            wait_rows(1 - cur)


def _dispatch(zero_offsets, dest, x_packed, n_blocks):
    n_tokens, width = x_packed.shape
    grid_spec = pltpu.PrefetchScalarGridSpec(
        num_scalar_prefetch=1,
        grid=(n_tokens // DISPATCH_TILE,),
        in_specs=[pl.BlockSpec((1, 1, 2 * DISPATCH_TILE), lambda i, z: (i, 0, 0), memory_space=pltpu.SMEM),
                  pl.BlockSpec((DISPATCH_TILE, width), lambda i, z: (i, 0))],
        out_specs=pl.BlockSpec(memory_space=pl.ANY),
        scratch_shapes=[pltpu.VMEM((MOE_BLOCK, width), x_packed.dtype),
                        pltpu.VMEM((2, DISPATCH_TILE, width), x_packed.dtype),
                        pltpu.SemaphoreType.DMA, pltpu.SemaphoreType.DMA((2,))],
    )
    return pl.pallas_call(
        _dispatch_kernel,
        out_shape=jax.ShapeDtypeStruct((n_blocks * MOE_BLOCK, width), x_packed.dtype),
        grid_spec=grid_spec,
        compiler_params=_cparams(("arbitrary",)),
        name="moe_dispatch",
    )(zero_offsets, dest.reshape(-1, 1, 2 * DISPATCH_TILE), x_packed)


def _expert_kernel(be_ref, nb_ref, x_ref, wg_hbm, wu_hbm, wd_hbm, o_ref, wg_buf, wu_buf, wd_buf, slot_ref, sem):
    blk = pl.program_id(0)
    n_used = nb_ref[0]
    expert = be_ref[blk]
    is_first = (blk == 0) | (be_ref[jnp.maximum(blk - 1, 0)] != expert)

    def fetch(e, slot):
        return [pltpu.make_async_copy(hbm.at[e], buf.at[slot], sem.at[slot, i])
                for i, (hbm, buf) in enumerate(((wg_hbm, wg_buf), (wu_hbm, wu_buf), (wd_hbm, wd_buf)))]

    @pl.when((blk < n_used) & is_first)
    def _():
        @pl.when(blk == 0)
        def _():
            slot_ref[0] = 1
            for copy in fetch(expert, 0):
                copy.start()

        slot = 1 - slot_ref[0]
        slot_ref[0] = slot
        for copy in fetch(expert, slot):
            copy.wait()
        nxt = lax.while_loop(lambda j: (j < n_used) & (be_ref[jnp.minimum(j, n_used - 1)] == expert),
                             lambda j: j + 1, blk + 1)

        @pl.when(nxt < n_used)
        def _():
            for copy in fetch(be_ref[jnp.minimum(nxt, n_used - 1)], 1 - slot):
                copy.start()

    @pl.when(blk < n_used)
    def _():
        slot = slot_ref[0]
        half = D_MODEL // 2
        x_head, x_tail = _unpack_bf16_halves(x_ref[...])
        proj = lambda w_buf: (_dot(x_head, w_buf[slot, 0:half, :].astype(BF16))
                              + _dot(x_tail, w_buf[slot, half:D_MODEL, :].astype(BF16)))
        gate = proj(wg_buf)
        up = proj(wu_buf)
        h = gate * _sigmoid(gate) * up
        o_ref[...] = _dot(h.astype(BF16), wd_buf[slot].astype(BF16))

    @pl.when(blk >= n_used)
    def _():
        o_ref[...] = jnp.zeros_like(o_ref)


def _expert_mlp(block_expert, n_used, x_sorted, w_gate, w_up, w_down, n_blocks):
    grid_spec = pltpu.PrefetchScalarGridSpec(
        num_scalar_prefetch=2,
        grid=(n_blocks,),
        in_specs=[pl.BlockSpec((MOE_BLOCK, D_MODEL // 2), lambda b, be, nb: (jnp.minimum(b, nb[0] - 1), 0)),
                  pl.BlockSpec(memory_space=pl.ANY), pl.BlockSpec(memory_space=pl.ANY),
                  pl.BlockSpec(memory_space=pl.ANY)],
        out_specs=pl.BlockSpec((MOE_BLOCK, D_MODEL), lambda b, be, nb: (b, 0)),
        scratch_shapes=[pltpu.VMEM((2, D_MODEL, D_EXPERT), F32), pltpu.VMEM((2, D_MODEL, D_EXPERT), F32),
                        pltpu.VMEM((2, D_EXPERT, D_MODEL), F32), pltpu.SMEM((1,), jnp.int32),
                        pltpu.SemaphoreType.DMA((2, 3))],
    )
    return pl.pallas_call(
        _expert_kernel,
        out_shape=jax.ShapeDtypeStruct((n_blocks * MOE_BLOCK, D_MODEL), F32),
        grid_spec=grid_spec,
        compiler_params=_cparams(("arbitrary",)),
        name="expert_mlp",
    )(block_expert, n_used, x_sorted, w_gate, w_up, w_down)


COMBINE_TILE = 128


def _combine_kernel(dest_ref, dest_next_ref, y_hbm, x1_ref, route_ref, g_ref, b_ref, o_ref, ybuf, sem):
    i = pl.program_id(0)
    cur = lax.rem(i, 2)
    n_rows = 2 * COMBINE_TILE

    def gather(table_ref, buf):
        for slot in range(n_rows):
            pltpu.make_async_copy(y_hbm.at[pl.ds(table_ref[0, 0, slot], 1)], ybuf.at[buf, pl.ds(slot, 1)],
                                  sem.at[buf]).start()

    def wait_gather(buf):
        pltpu.make_async_copy(y_hbm.at[pl.ds(0, n_rows)], ybuf.at[buf], sem.at[buf]).wait()

    @pl.when(i == 0)
    def _():
        gather(dest_ref, 0)

    gather(dest_next_ref, 1 - cur)
    wait_gather(cur)
    route = route_ref[...]
    yb = ybuf[cur]
    moe = route[:, 2:3] * yb[0:COMBINE_TILE, :] + route[:, 3:4] * yb[COMBINE_TILE:n_rows, :]
    o_ref[...] = _layer_norm(ALPHA * x1_ref[...] + moe, g_ref[...], b_ref[...])

    @pl.when(i == pl.num_programs(0) - 1)
    def _():
        wait_gather(1 - cur)


def _combine(dest, y_slots, x1_all, route_all, m, row_block, ln_g, ln_b, name):
    tm = COMBINE_TILE
    return pl.pallas_call(
        _combine_kernel,
        out_shape=jax.ShapeDtypeStruct((m, D_MODEL), F32),
        grid=(m // tm,),
        in_specs=[pl.BlockSpec((1, 1, 2 * tm), lambda i: (i, 0, 0), memory_space=pltpu.SMEM),
                  pl.BlockSpec((1, 1, 2 * tm), lambda i: (i + 1, 0, 0), memory_space=pltpu.SMEM),
                  pl.BlockSpec(memory_space=pl.ANY),
                  pl.BlockSpec((tm, D_MODEL), lambda i: (i + row_block, 0)),
                  pl.BlockSpec((tm, LANES), lambda i: (i + row_block, 0)),
                  pl.BlockSpec((1, D_MODEL), lambda i: (0, 0)),
                  pl.BlockSpec((1, D_MODEL), lambda i: (0, 0))],
        out_specs=pl.BlockSpec((tm, D_MODEL), lambda i: (i, 0)),
        scratch_shapes=[pltpu.VMEM((2, 2 * tm, D_MODEL), F32), pltpu.SemaphoreType.DMA((2,))],
        compiler_params=_cparams(("arbitrary",)),
        name=name,
    )(dest, dest, y_slots, x1_all, route_all, ln_g, ln_b)


def _dispatch_plan(route_all, n_blocks):
    flat_e = route_all[:, 0:2].astype(jnp.int32).reshape(-1)
    onehot = (flat_e[:, None] == jnp.arange(N_EXPERTS, dtype=jnp.int32)[None, :]).astype(jnp.int32)
    csum = jnp.cumsum(onehot, axis=0)
    rank = jnp.sum(onehot * csum, axis=1) - 1
    counts = csum[-1]
    padded = (counts + MOE_BLOCK - 1) // MOE_BLOCK * MOE_BLOCK
    pend = jnp.cumsum(padded)
    pstart = pend - padded
    dest = (pstart[flat_e] + rank).astype(jnp.int32)
    zero_offsets = jnp.where(counts > 0, pend - MOE_BLOCK, -1).astype(jnp.int32)
    n_used = (pend[-1] // MOE_BLOCK).astype(jnp.int32)
    block_start = jnp.minimum(jnp.arange(n_blocks, dtype=jnp.int32), n_used - 1) * MOE_BLOCK
    block_e = jnp.minimum(jnp.searchsorted(pend, block_start, side="right"), N_EXPERTS - 1).astype(jnp.int32)
    return dest, jnp.concatenate([zero_offsets, n_used.reshape(1)]), block_e, n_used.reshape(1)


def kernel(x_prompt, x_sample, cache_k_win, cache_v_win, state_wkv, state_shift, w_in, attn_sinks, shift_mu, w0,
           w_decay_up, a0, w_a_up, w_g_up, k_k, k_a, r_k, gn_g, gn_b, w_out, ln1_g, ln1_b, w_coarse, b_coarse,
           w_fine, b_fine, w_exp_gate, w_exp_up, w_exp_down, ln2_g, ln2_b):
    xp = x_prompt[0]
    xs = x_sample[:, 0]
    row = lambda a: a.reshape(1, -1)

    w_in_t = jnp.swapaxes(w_in[0], 0, 1)
    prm = dict(mu=row(shift_mu[0, :D_RKV]), mu_tail=row(shift_mu[0, D_RKV:]), w_tail=w_in_t[D_MAIN:].astype(BF16),
               w0=row(w0[0]), a0=row(a0[0]), k_k=row(k_k[0]), k_a=row(k_a[0]),
               r_k=row(r_k[0]), gn_g=row(gn_g[0]), gn_b=row(gn_b[0]),
               wd=w_decay_up[0], wa=w_a_up[0], wg=w_g_up[0])
    sinks = attn_sinks[0]
    wo_bf16 = w_out[0].astype(BF16)
    w_route = jnp.pad(jnp.concatenate([w_coarse[0], w_fine[0]], axis=1), ((0, 0), (0, LANES - N_GROUPS - N_EXPERTS)))
    b_route = jnp.pad(jnp.concatenate([b_coarse[0], b_fine[0]]), (0, LANES - N_GROUPS - N_EXPERTS)).reshape(1, LANES)

    hp = _matmul(xp, w_in_t, D_MAIN, MAIN_TM, MAIN_TN, "in_proj_prompt")
    hs = _matmul(xs, w_in_t, D_MAIN, DEC_BATCH, MAIN_TN, "in_proj_sample")

    attn_p = _prompt_attention(hp, sinks)
    rwkv_p, state_p, tail_p = _prompt_rwkv(hp, xp, prm)

    q_s = hs[:, :D_ATTN].reshape(DEC_BATCH, N_Q_HEADS, HEAD_DIM)
    k_s = hs[:, D_ATTN:D_ATTN + D_KV].reshape(DEC_BATCH, 1, D_KV)
    v_s = hs[:, D_ATTN + D_KV:D_QKV].reshape(DEC_BATCH, 1, D_KV)
    window_t = lambda c: jnp.transpose(c, (0, 2, 3, 1)).reshape(DEC_BATCH, D_KV, WINDOW)
    attn_s, kwin_s, vwin_s = _sample_attention(
        q_s, k_s, v_s, window_t(cache_k_win[0]), window_t(cache_v_win[0]), sinks.reshape(N_Q_HEADS, 1))
    r_s, k2_s, vv_s, g_s, tail_s, r_t, w_t, k_t, v_t, a_t, b_t = _sample_prep(hs, xs, state_shift[0], prm)
    y_t, state_s = _sample_step(jnp.transpose(state_wkv[0], (1, 2, 3, 0)), r_t, w_t, k_t, a_t, b_t, v_t)
    state_s = state_s.reshape(DEC_BATCH, N_RWKV_HEADS, HEAD_DIM, HEAD_DIM)
    rwkv_s = _sample_post(y_t, r_s, k2_s, vv_s, g_s, prm)

    n_tokens = SEQ + DEC_BATCH
    outs_pr = _outproj_router(attn_p, rwkv_p, xp, wo_bf16, row(ln1_g[0]), row(ln1_b[0]), w_route, b_route,
                              256, n_tokens, 0, None, "outproj_router_prompt")
    x1_all, x1b_all, route_all = _outproj_router(attn_s.reshape(DEC_BATCH, D_ATTN), rwkv_s, xs,
                                                 wo_bf16, row(ln1_g[0]), row(ln1_b[0]), w_route, b_route,
                                                 DEC_BATCH, n_tokens, SEQ // DEC_BATCH, outs_pr,
                                                 "outproj_router_sample")

    n_assign = 2 * n_tokens
    n_blocks = -(-(n_assign + N_EXPERTS * (MOE_BLOCK - 1)) // MOE_BLOCK)
    dest, zero_offsets, block_e, n_used = _dispatch_plan(route_all, n_blocks)
    x_sorted = _dispatch(zero_offsets, dest, x1b_all, n_blocks)
    y_slots = _expert_mlp(block_e, n_used, x_sorted, w_exp_gate[0], w_exp_up[0], w_exp_down[0], n_blocks)

    def dest_tiles(d):
        d = d.reshape(-1, COMBINE_TILE, 2)
        d = jnp.concatenate([d[:, :, 0], d[:, :, 1]], axis=1)
        return jnp.pad(d, ((0, 1), (0, 0))).reshape(-1, 1, 2 * COMBINE_TILE)

    y_p = _combine(dest_tiles(dest[:2 * SEQ]), y_slots, x1_all, route_all, SEQ, 0, row(ln2_g[0]), row(ln2_b[0]),
                   "combine_prompt")
    y_s = _combine(dest_tiles(dest[2 * SEQ:]), y_slots, x1_all, route_all, DEC_BATCH, SEQ // COMBINE_TILE,
                   row(ln2_g[0]), row(ln2_b[0]), "combine_sample")

    kv4 = lambda a: a.reshape(a.shape[0], N_KV_HEADS, HEAD_DIM)
    k_win_p = kv4(hp[SEQ - WINDOW:, D_ATTN:D_ATTN + D_KV])[None, None]
    v_win_p = kv4(hp[SEQ - WINDOW:, D_ATTN + D_KV:D_QKV])[None, None]
    sp = state_p.reshape(N_PAIRS, HEADS_PER_TILE, HEAD_DIM, HEADS_PER_TILE, HEAD_DIM)
    wkv_p = jnp.stack([sp[:, i, :, i, :] for i in range(HEADS_PER_TILE)], axis=1)
    wkv_p = wkv_p.reshape(N_RWKV_HEADS, HEAD_DIM, HEAD_DIM).transpose(0, 2, 1)[None, None]
    shift_p = jnp.concatenate([hp[SEQ - 1:SEQ, D_QKV:], tail_p[0:1]], axis=1)[None]
    shift_s = jnp.concatenate([hs[:, D_QKV:], tail_s], axis=1)[None]
    return (y_p[None], y_s[:, None, :], k_win_p, v_win_p, wkv_p, shift_p,
            kwin_s.reshape(1, DEC_BATCH, WINDOW, N_KV_HEADS, HEAD_DIM),
            vwin_s.reshape(1, DEC_BATCH, WINDOW, N_KV_HEADS, HEAD_DIM),
            state_s[None], shift_s)
```

```python
import functools
import math

import jax
import jax.numpy as jnp
from jax import lax
from jax.experimental import pallas as pl
from jax.experimental.pallas import tpu as pltpu

F32 = jnp.float32
BF16 = jnp.bfloat16

D_MODEL = 2048
SEQ = 8192
DEC_BATCH = 128
HEAD_DIM = 64
D_ATTN = 1024
D_RWKV = 1024
N_Q_HEADS = 16
N_KV_HEADS = 4
Q_PER_KV = 4
D_KV = 256
WINDOW = 128
ATTN_SCALE = HEAD_DIM ** -0.5
N_RWKV_HEADS = 16
W_LORA = 64
A_LORA = 64
G_LORA = 160
D_SHIFT = 3 * D_RWKV + W_LORA + A_LORA + G_LORA
D_QKV = D_ATTN + 2 * D_KV
N_GROUPS = 4
EXPERTS_PER_GROUP = 8
N_EXPERTS = 32
D_EXPERT = 512
ALPHA = 2.0 ** 0.25
LN_EPS = 1e-5
GN_EPS = 64e-5

SUBLANES = 8
LANES = 128
VMEM_LIMIT = 52 * 1024 * 1024

D_RKV = 3 * D_RWKV
D_TAIL = W_LORA + A_LORA + G_LORA
D_MAIN = D_QKV + D_RKV
MAIN_TN = 1536
MAIN_TM = 512

CHUNK = 64
HEADS_PER_TILE = LANES // HEAD_DIM
N_PAIRS = N_RWKV_HEADS // HEADS_PER_TILE
SOLVE_LEVELS = int(math.log2(CHUNK))
PAIR_GROUP = 8

MOE_BLOCK = 256
ROUTE_FINE_OFF = N_GROUPS

NN = (((1,), (0,)), ((), ()))
NT = (((1,), (1,)), ((), ()))


def _dot(a, b, dims=NN):
    return lax.dot_general(a, b, dims, preferred_element_type=F32)


def _dot1(a, b, dims=NN):
    return _dot(a.astype(BF16), b.astype(BF16), dims)


def _split(x):
    hi = x.astype(BF16)
    lo = (x - hi.astype(F32)).astype(BF16)
    return hi, lo


def _dot3(a, b, dims=NN):
    ah, al = _split(a)
    bh, bl = _split(b)
    return _dot(ah, bh, dims) + (_dot(ah, bl, dims) + _dot(al, bh, dims))


def _dot_exact_lhs(a_bf16, b, dims=NN):
    bh, bl = _split(b)
    return _dot(a_bf16, bh, dims) + _dot(a_bf16, bl, dims)


def _dot_exact_rhs(a, b_bf16, dims=NN):
    ah, al = _split(a)
    return _dot(ah, b_bf16, dims) + _dot(al, b_bf16, dims)


def _div_pow2(x, d):
    return lax.shift_right_logical(x, jnp.int32(int(math.log2(d))))


def _mod_pow2(x, d):
    return lax.bitwise_and(x, jnp.int32(d - 1))


def _pack_bf16_halves(x_bf16):
    n = x_bf16.shape[1] // 2
    bits = lax.bitcast_convert_type(x_bf16.astype(F32), jnp.uint32)
    return lax.bitwise_or(bits[:, 0:n], lax.shift_right_logical(bits[:, n:2 * n], jnp.uint32(16)))


def _unpack_bf16_halves(packed):
    hi = lax.bitcast_convert_type(lax.bitwise_and(packed, jnp.uint32(0xFFFF0000)), F32)
    lo = lax.bitcast_convert_type(lax.shift_left(packed, jnp.uint32(16)), F32)
    return hi.astype(BF16), lo.astype(BF16)


def _sigmoid(x):
    return 1.0 / (1.0 + jnp.exp(-x))


def _softplus(x):
    return jnp.maximum(x, 0.0) + jnp.log(1.0 + jnp.exp(-jnp.abs(x)))


def _layer_norm(z, g, b):
    mu = jnp.mean(z, axis=-1, keepdims=True)
    d = z - mu
    var = jnp.mean(d * d, axis=-1, keepdims=True)
    return d * lax.rsqrt(var + LN_EPS) * g + b


def _cparams(sem):
    return pltpu.CompilerParams(dimension_semantics=sem, vmem_limit_bytes=VMEM_LIMIT)


def _matmul_kernel(x_ref, wt_ref, o_ref):
    o_ref[...] = _dot(x_ref[...].astype(BF16), wt_ref[...].astype(BF16), NT)


def _matmul(x, w_t, n_out, tm, tn, name):
    m, k = x.shape
    tm = min(tm, m)
    return pl.pallas_call(
        _matmul_kernel,
        out_shape=jax.ShapeDtypeStruct((m, n_out), F32),
        grid=(n_out // tn, m // tm),
        in_specs=[pl.BlockSpec((tm, k), lambda j, i: (i, 0)),
                  pl.BlockSpec((tn, k), lambda j, i: (j, 0))],
        out_specs=pl.BlockSpec((tm, tn), lambda j, i: (i, j)),
        compiler_params=_cparams(("arbitrary", "arbitrary")),
        name=name,
    )(x, w_t)


def _prompt_attn_kernel(q_ref, kvp_ref, kvc_ref, sink_ref, o_ref):
    blk = pl.program_id(0)
    q = q_ref[...]
    kv_prev = kvp_ref[...]
    kv_cur = kvc_ref[...]
    qi = _mod_pow2(lax.broadcasted_iota(jnp.int32, (Q_PER_KV * WINDOW, 2 * WINDOW), 0), WINDOW)
    kj = lax.broadcasted_iota(jnp.int32, (Q_PER_KV * WINDOW, 2 * WINDOW), 1)
    diff = qi + WINDOW - kj
    mask = (diff >= 0) & (diff <= WINDOW) & ((blk > 0) | (kj >= WINDOW))
    row_head = _div_pow2(lax.broadcasted_iota(jnp.int32, (Q_PER_KV * WINDOW, 1), 0), WINDOW)
    groups = range(N_KV_HEADS)
    kv_cols = lambda off, g: jnp.concatenate([kv_prev[:, off + g * HEAD_DIM:off + (g + 1) * HEAD_DIM],
                                              kv_cur[:, off + g * HEAD_DIM:off + (g + 1) * HEAD_DIM]],
                                             axis=0).astype(BF16)
    q_rows = lambda g: jnp.concatenate(
        [q[:, (g * Q_PER_KV + h) * HEAD_DIM:(g * Q_PER_KV + h + 1) * HEAD_DIM] for h in range(Q_PER_KV)],
        axis=0).astype(BF16)
    s = [jnp.where(mask, _dot(q_rows(g), kv_cols(0, g), NT) * ATTN_SCALE, -jnp.inf) for g in groups]
    sink = []
    for g in groups:
        col = jnp.zeros((Q_PER_KV * WINDOW, 1), F32)
        for h in range(Q_PER_KV):
            col = jnp.where(row_head == h, sink_ref[g * Q_PER_KV + h], col)
        sink.append(col)
    m = [jnp.maximum(jnp.max(s[g], axis=-1, keepdims=True), sink[g]) for g in groups]
    p = [jnp.exp(s[g] - m[g]) for g in groups]
    denom = [jnp.sum(p[g], axis=-1, keepdims=True) + jnp.exp(sink[g] - m[g]) for g in groups]
    o = [_dot((p[g] / denom[g]).astype(BF16), kv_cols(D_KV, g)) for g in groups]
    o_ref[...] = jnp.concatenate([o[g][h * WINDOW:(h + 1) * WINDOW, :] for g in groups for h in range(Q_PER_KV)],
                                 axis=1)


def _prompt_attention(h_attn, sinks):
    nb = SEQ // WINDOW
    return pl.pallas_call(
        _prompt_attn_kernel,
        out_shape=jax.ShapeDtypeStruct((SEQ, D_ATTN), F32),
        grid=(nb,),
        in_specs=[pl.BlockSpec((WINDOW, D_ATTN), lambda i: (i, 0)),
                  pl.BlockSpec((WINDOW, 2 * D_KV), lambda i: (jnp.maximum(i - 1, 0), 2)),
                  pl.BlockSpec((WINDOW, 2 * D_KV), lambda i: (i, 2)),
                  pl.BlockSpec(memory_space=pltpu.SMEM)],
        out_specs=pl.BlockSpec((WINDOW, D_ATTN), lambda i: (i, 0)),
        compiler_params=_cparams(("arbitrary",)),
        name="prompt_attention",
    )(h_attn, h_attn, h_attn, sinks)


SAMPLE_ATTN_TILE = 8


def _sample_attn_kernel(q_ref, knew_ref, vnew_ref, ck_ref, cv_ref, sink_ref, o_ref, kwin_ref, vwin_ref):
    lane = lax.broadcasted_iota(jnp.int32, (N_Q_HEADS, D_KV), 1)
    head = lax.broadcasted_iota(jnp.int32, (N_Q_HEADS, D_KV), 0)
    group_mask = _div_pow2(lane, HEAD_DIM) == _div_pow2(head, Q_PER_KV)
    sink = sink_ref[...]
    row = lax.broadcasted_iota(jnp.int32, (WINDOW, D_KV), 0)
    seqs = range(SAMPLE_ATTN_TILE)
    qbd = [jnp.where(group_mask, jnp.concatenate([q_ref[b]] * N_KV_HEADS, axis=1), 0.0).astype(BF16) for b in seqs]
    s = [_dot1(qbd[b], ck_ref[b]) * ATTN_SCALE for b in seqs]
    s_new = [jnp.sum(qbd[b].astype(F32) * knew_ref[b].astype(BF16).astype(F32), axis=-1, keepdims=True) * ATTN_SCALE
             for b in seqs]
    m = [jnp.maximum(jnp.maximum(jnp.max(s[b], axis=-1, keepdims=True), s_new[b]), sink) for b in seqs]
    p = [jnp.exp(s[b] - m[b]) for b in seqs]
    p_new = [jnp.exp(s_new[b] - m[b]) for b in seqs]
    denom = [jnp.sum(p[b], axis=-1, keepdims=True) + p_new[b] + jnp.exp(sink - m[b]) for b in seqs]
    for b in seqs:
        kb = ck_ref[b].T
        vb = cv_ref[b].T
        kn = knew_ref[b]
        vn = vnew_ref[b]
        o_full = (_dot1(p[b] / denom[b], vb)
                  + (p_new[b] / denom[b]).astype(BF16).astype(F32) * vn.astype(BF16).astype(F32))
        o_full = jnp.where(group_mask, o_full, 0.0)
        o = o_full[:, 0:HEAD_DIM]
        for g in range(1, N_KV_HEADS):
            o = o + o_full[:, g * HEAD_DIM:(g + 1) * HEAD_DIM]
        o_ref[b] = o
        kwin_ref[b] = jnp.where(row == WINDOW - 1, kn, pltpu.roll(kb, WINDOW - 1, axis=0))
        vwin_ref[b] = jnp.where(row == WINDOW - 1, vn, pltpu.roll(vb, WINDOW - 1, axis=0))


def _sample_attention(q, k_new, v_new, cache_k_t, cache_v_t, sinks):
    bt = SAMPLE_ATTN_TILE
    win_spec = pl.BlockSpec((bt, WINDOW, D_KV), lambda i: (i, 0, 0))
    win_t_spec = pl.BlockSpec((bt, D_KV, WINDOW), lambda i: (i, 0, 0))
    new_spec = pl.BlockSpec((bt, 1, D_KV), lambda i: (i, 0, 0))
    return pl.pallas_call(
        _sample_attn_kernel,
        out_shape=(jax.ShapeDtypeStruct((DEC_BATCH, N_Q_HEADS, HEAD_DIM), F32),
                   jax.ShapeDtypeStruct((DEC_BATCH, WINDOW, D_KV), F32),
                   jax.ShapeDtypeStruct((DEC_BATCH, WINDOW, D_KV), F32)),
        grid=(DEC_BATCH // bt,),
        in_specs=[pl.BlockSpec((bt, N_Q_HEADS, HEAD_DIM), lambda i: (i, 0, 0)),
                  new_spec, new_spec, win_t_spec, win_t_spec,
                  pl.BlockSpec((N_Q_HEADS, 1), lambda i: (0, 0))],
        out_specs=(pl.BlockSpec((bt, N_Q_HEADS, HEAD_DIM), lambda i: (i, 0, 0)), win_spec, win_spec),
        compiler_params=_cparams(("arbitrary",)),
        name="sample_attention",
    )(q, k_new, v_new, cache_k_t, cache_v_t, sinks)


def _head_ones():
    r = _div_pow2(lax.broadcasted_iota(jnp.int32, (LANES, LANES), 0), HEAD_DIM)
    c = _div_pow2(lax.broadcasted_iota(jnp.int32, (LANES, LANES), 1), HEAD_DIM)
    return jnp.where(r == c, 1.0, 0.0).astype(BF16)


def _head_sum(x, ones):
    parts = [_dot_exact_rhs(x[:, p * LANES:(p + 1) * LANES], ones) for p in range(x.shape[1] // LANES)]
    return jnp.concatenate(parts, axis=1)


def _token_mix(feat, shifted, mu):
    return feat + (shifted - feat) * mu


def _rwkv_prep(mixed, mixed_tail, w0, a0, k_k, k_a, wd, wa, wg, ones):
    r = mixed[:, 0:D_RWKV]
    k = mixed[:, D_RWKV:2 * D_RWKV]
    v = mixed[:, 2 * D_RWKV:3 * D_RWKV]
    xw = mixed_tail[:, 0:W_LORA]
    xa = mixed_tail[:, W_LORA:W_LORA + A_LORA]
    xg = mixed_tail[:, W_LORA + A_LORA:D_TAIL]
    w_log = -_softplus(-(w0 + _dot1(jnp.tanh(xw), wd))) - 0.5
    log_decay = -jnp.exp(w_log)
    a = _sigmoid(a0 + _dot1(xa, wa))
    g = _dot1(_sigmoid(xg), wg)
    kk = k * k_k
    kk = kk * lax.rsqrt(jnp.maximum(_head_sum(kk * kk, ones), 1e-24))
    k2 = k * (1.0 + (a - 1.0) * k_a)
    return r, log_decay, k2, v, -kk, kk * a, g


def _rwkv_post(y, r, k2, v, g, r_k, gn_g, gn_b, ones):
    inv_n = 1.0 / HEAD_DIM
    mu = _head_sum(y, ones) * inv_n
    d = y - mu
    var = _head_sum(d * d, ones) * inv_n
    yn = d * lax.rsqrt(var + GN_EPS) * gn_g + gn_b
    bonus = _head_sum(r * k2 * r_k, ones) * v
    return (yn + bonus) * g


(OP_AABS, OP_RABS, OP_AN, OP_RN, OP_BN, OP_KN, OP_BH, OP_KH, OP_V) = range(9)
N_OPS = 9


def _prompt_rwkv_kernel(f1_ref, f2_ref, x_ref, wt_ref, mu_ref, mut_ref, w0_ref, a0_ref, kk_ref, ka_ref, rk_ref,
                        gng_ref, gnb_ref, wd_ref, wa_ref, wg_ref, out_ref, state_ref, tail_ref,
                        prev_ref, prevt_ref, s_ref, ops_ref, pc_ref, y_ref):
    c = pl.program_id(0)
    C = CHUNK

    @pl.when(c == 0)
    def _():
        prev_ref[...] = jnp.zeros_like(prev_ref)
        prevt_ref[...] = jnp.zeros_like(prevt_ref)
        s_ref[...] = jnp.zeros_like(s_ref)

    ones = _head_ones()
    row = lax.broadcasted_iota(jnp.int32, (C, 1), 0)

    def token_shift(feat, carry_ref):
        shifted = jnp.where(row == 0, carry_ref[0:1, :], pltpu.roll(feat, 1, axis=0))
        carry_ref[0:1, :] = feat[C - 1:C, :]
        return shifted

    feat = jnp.concatenate([f1_ref[...], f2_ref[...]], axis=1)
    tail = _dot1(x_ref[...], wt_ref[...], NT)
    mixed = _token_mix(feat, token_shift(feat, prev_ref), mu_ref[...])
    mixed_tail = _token_mix(tail, token_shift(tail, prevt_ref), mut_ref[...])
    tail_ref[...] = prevt_ref[...]
    r, ld, k2, v, av, bv, g = _rwkv_prep(mixed, mixed_tail, w0_ref[...], a0_ref[...], kk_ref[...], ka_ref[...],
                                         wd_ref[...], wa_ref[...], wg_ref[...], ones)

    ti = lax.broadcasted_iota(jnp.int32, (C, C), 0)
    tj = lax.broadcasted_iota(jnp.int32, (C, C), 1)
    tri_incl = jnp.where(tj <= ti, 1.0, 0.0).astype(BF16)
    cs = _dot_exact_lhs(tri_incl, ld)
    cs_ref = cs[C // 2 - 1:C // 2, :]
    cs_end = cs[C - 1:C, :]
    e_prev = jnp.exp(cs - ld)
    e_cur = jnp.exp(cs)
    n_prev = jnp.exp(cs - ld - cs_ref)
    n_cur = jnp.exp(cs - cs_ref)
    n_inv = jnp.exp(cs_ref - cs)
    e_tail = jnp.exp(cs_end - cs)
    ops = {OP_AABS: av * e_prev, OP_RABS: r * e_cur, OP_AN: av * n_prev, OP_RN: r * n_cur,
           OP_BN: bv * n_inv, OP_KN: k2 * n_inv, OP_BH: bv * e_tail, OP_KH: k2 * e_tail, OP_V: v}
    p_end = jnp.exp(cs_end)
    for p in range(N_PAIRS):
        sl = slice(p * LANES, (p + 1) * LANES)
        for idx, val in ops.items():
            ops_ref[p, idx] = val[:, sl]
        pc_ref[p] = jnp.broadcast_to(p_end[:, sl], (SUBLANES, LANES))

    lane1 = lax.broadcasted_iota(jnp.int32, (C, LANES), 1)
    head0 = lane1 < HEAD_DIM
    r2 = lax.broadcasted_iota(jnp.int32, (2 * C, 2 * C), 0)
    c2 = lax.broadcasted_iota(jnp.int32, (2 * C, 2 * C), 1)
    tq = _mod_pow2(r2, C)
    tk = _mod_pow2(c2, C)
    band = (tk < tq) | ((tk == tq) & (r2 >= C))
    blockdiag = _div_pow2(r2, HEAD_DIM) == _div_pow2(c2, HEAD_DIM)

    op = lambda p, idx: ops_ref[p, idx]
    zero_half = jnp.zeros((C, LANES), F32)
    for pairs in [range(g, g + PAIR_GROUP) for g in range(0, N_PAIRS, PAIR_GROUP)]:
        gy = {p: _dot1(jnp.concatenate([op(p, OP_AABS), op(p, OP_RABS)], axis=0), s_ref[p]) for p in pairs}

        am0, am1 = {}, {}
        for p in pairs:
            a_n, r_n = op(p, OP_AN), op(p, OP_RN)
            b0, k0 = jnp.where(head0, op(p, OP_BN), 0.0), jnp.where(head0, op(p, OP_KN), 0.0)
            b1, k1 = jnp.where(head0, 0.0, op(p, OP_BN)), jnp.where(head0, 0.0, op(p, OP_KN))
            am = _dot1(jnp.concatenate([a_n, r_n], axis=0), jnp.concatenate([k0, b0, b1, k1], axis=0), NT)
            am0[p] = jnp.where(band, am[:, 0:2 * C], 0.0)
            am1[p] = jnp.where(band, am[:, 2 * C:4 * C], 0.0)

        w0, w1 = {}, {}
        for p in pairs:
            top0, top1 = am0[p][0:C], am1[p][0:C]
            ak = jnp.concatenate([jnp.where(head0, top0, 0.0), jnp.where(head0, 0.0, top1)], axis=0)
            vv = op(p, OP_V)
            g0 = gy[p][0:C]
            m = jnp.concatenate([g0, g0], axis=0) + _dot1(ak, jnp.concatenate([vv, vv], axis=0))
            w0[p] = jnp.where(head0, m[0:C], top0)
            w1[p] = jnp.where(head0, top1, m[C:2 * C])

        for lvl in range(SOLVE_LEVELS):
            prod0 = {p: _dot1(w0[p], jnp.concatenate([zero_half, w0[p]], axis=0)) for p in pairs}
            prod1 = {p: _dot1(w1[p], jnp.concatenate([w1[p], zero_half], axis=0)) for p in pairs}
            w0 = {p: jnp.where(head0, w0[p] + prod0[p], prod0[p]) for p in pairs}
            w1 = {p: jnp.where(head0, prod1[p], w1[p] + prod1[p]) for p in pairs}
        u = {p: jnp.where(head0, w0[p], w1[p]) for p in pairs}

        for p in pairs:
            vv = op(p, OP_V)
            y_lhs = jnp.concatenate([am0[p][C:2 * C], am1[p][C:2 * C]], axis=1)
            y_rhs = jnp.concatenate([jnp.where(head0, vv, 0.0), jnp.where(head0, u[p], 0.0),
                                     jnp.where(head0, 0.0, u[p]), jnp.where(head0, 0.0, vv)], axis=0)
            y_ref[p] = gy[p][C:2 * C] + _dot1(y_lhs, y_rhs)

        for p in pairs:
            decay_rows = jnp.broadcast_to(pc_ref[p][0:1, :], (LANES, LANES)).T
            upd_lhs = jnp.concatenate([op(p, OP_BH), op(p, OP_KH)], axis=0).T
            upd_rhs = jnp.concatenate([u[p], op(p, OP_V)], axis=0)
            s_ref[p] = s_ref[p] * decay_rows + jnp.where(blockdiag, _dot1(upd_lhs, upd_rhs), 0.0)

    y = jnp.concatenate([y_ref[p] for p in range(N_PAIRS)], axis=1)
    out_ref[...] = _rwkv_post(y, r, k2, v, g, rk_ref[...], gng_ref[...], gnb_ref[...], ones)

    @pl.when(c == pl.num_programs(0) - 1)
    def _():
        state_ref[...] = s_ref[...]


def _prompt_rwkv(h_main, x, prm):
    n_chunks = SEQ // CHUNK
    half = D_RKV // 2
    assert D_QKV == half
    vec = pl.BlockSpec((1, D_RWKV), lambda c: (0, 0))
    full = lambda a: pl.BlockSpec(a.shape, lambda c: (0,) * a.ndim)
    return pl.pallas_call(
        _prompt_rwkv_kernel,
        out_shape=(jax.ShapeDtypeStruct((SEQ, D_RWKV), F32),
                   jax.ShapeDtypeStruct((N_PAIRS, LANES, LANES), F32),
                   jax.ShapeDtypeStruct((SUBLANES, D_TAIL), F32)),
        grid=(n_chunks,),
        in_specs=[pl.BlockSpec((CHUNK, half), lambda c: (c, 1)),
                  pl.BlockSpec((CHUNK, half), lambda c: (c, 2)),
                  pl.BlockSpec((CHUNK, D_MODEL), lambda c: (c, 0)),
                  full(prm["w_tail"]), full(prm["mu"]), full(prm["mu_tail"]),
                  vec, vec, vec, vec, vec, vec, vec,
                  full(prm["wd"]), full(prm["wa"]), full(prm["wg"])],
        out_specs=(pl.BlockSpec((CHUNK, D_RWKV), lambda c: (c, 0)),
                   pl.BlockSpec((N_PAIRS, LANES, LANES), lambda c: (0, 0, 0)),
                   pl.BlockSpec((SUBLANES, D_TAIL), lambda c: (0, 0))),
        scratch_shapes=[pltpu.VMEM((SUBLANES, D_RKV), F32),
                        pltpu.VMEM((SUBLANES, D_TAIL), F32),
                        pltpu.VMEM((N_PAIRS, LANES, LANES), F32),
                        pltpu.VMEM((N_PAIRS, N_OPS, CHUNK, LANES), F32),
                        pltpu.VMEM((N_PAIRS, SUBLANES, LANES), F32),
                        pltpu.VMEM((N_PAIRS, CHUNK, LANES), F32)],
        compiler_params=_cparams(("arbitrary",)),
        name="prompt_rwkv",
    )(h_main, h_main, x, prm["w_tail"], prm["mu"], prm["mu_tail"], prm["w0"], prm["a0"], prm["k_k"], prm["k_a"],
      prm["r_k"], prm["gn_g"], prm["gn_b"], prm["wd"], prm["wa"], prm["wg"])


def _sample_prep_kernel(h_ref, x_ref, wt_ref, shift_ref, mu_ref, mut_ref, w0_ref, a0_ref, kk_ref, ka_ref,
                        wd_ref, wa_ref, wg_ref, r_ref, k_ref, v_ref, g_ref, tail_ref,
                        rt_ref, wtr_ref, kt_ref, vt_ref, at_ref, bt_ref):
    ones = _head_ones()
    feat = h_ref[:, D_QKV:D_MAIN]
    tail = _dot1(x_ref[...], wt_ref[...], NT)
    tail_ref[...] = tail
    mixed = _token_mix(feat, shift_ref[:, 0:D_RKV], mu_ref[...])
    mixed_tail = _token_mix(tail, shift_ref[:, D_RKV:D_SHIFT], mut_ref[...])
    r, ld, k2, v, av, bv, g = _rwkv_prep(mixed, mixed_tail, w0_ref[...], a0_ref[...], kk_ref[...], ka_ref[...],
                                         wd_ref[...], wa_ref[...], wg_ref[...], ones)
    r_ref[...] = r
    k_ref[...] = k2
    v_ref[...] = v
    g_ref[...] = g
    rt_ref[...] = r.T
    wtr_ref[...] = jnp.exp(ld).T
    kt_ref[...] = k2.T
    vt_ref[...] = v.T
    at_ref[...] = av.T
    bt_ref[...] = bv.T


def _sample_prep(h_main, x, shift, prm):
    tok = jax.ShapeDtypeStruct((DEC_BATCH, D_RWKV), F32)
    chan = jax.ShapeDtypeStruct((D_RWKV, DEC_BATCH), F32)
    return pl.pallas_call(
        _sample_prep_kernel,
        out_shape=(tok,) * 4 + (jax.ShapeDtypeStruct((DEC_BATCH, D_TAIL), F32),) + (chan,) * 6,
        compiler_params=pltpu.CompilerParams(vmem_limit_bytes=VMEM_LIMIT),
        name="sample_rwkv_prep",
    )(h_main, x, prm["w_tail"], shift, prm["mu"], prm["mu_tail"], prm["w0"], prm["a0"], prm["k_k"], prm["k_a"],
      prm["wd"], prm["wa"], prm["wg"])


STEP_GROUP = 4


def _sample_step_kernel(s_ref, r_ref, w_ref, k_ref, a_ref, b_ref, v_ref, y_ref, snew_ref):
    r, w, k, a, b = r_ref[...], w_ref[...], k_ref[...], a_ref[...], b_ref[...]
    for g0 in range(0, HEAD_DIM, 2 * STEP_GROUP):
        chans = range(g0, g0 + 2 * STEP_GROUP)
        sa = {i: jnp.sum(s_ref[0, i] * a, axis=0, keepdims=True) for i in chans}
        s_new = {i: s_ref[0, i] * w + sa[i] * b + v_ref[i:i + 1, :] * k for i in chans}
        for i in chans:
            y_ref[i:i + 1, :] = jnp.sum(s_new[i] * r, axis=0, keepdims=True)
        for i in range(g0, g0 + 2 * STEP_GROUP, 2):
            pair = jnp.concatenate([s_new[i], s_new[i + 1]], axis=0)
            snew_ref[:, i * HEAD_DIM:(i + 2) * HEAD_DIM] = pair.T


def _sample_step(state_t, r_t, w_t, k_t, a_t, b_t, v_t):
    head_rows = pl.BlockSpec((HEAD_DIM, DEC_BATCH), lambda h: (h, 0))
    return pl.pallas_call(
        _sample_step_kernel,
        out_shape=(jax.ShapeDtypeStruct((D_RWKV, DEC_BATCH), F32),
                   jax.ShapeDtypeStruct((DEC_BATCH, N_RWKV_HEADS * HEAD_DIM * HEAD_DIM), F32)),
        grid=(N_RWKV_HEADS,),
        in_specs=[pl.BlockSpec((1, HEAD_DIM, HEAD_DIM, DEC_BATCH), lambda h: (h, 0, 0, 0))] + [head_rows] * 6,
        out_specs=(head_rows, pl.BlockSpec((DEC_BATCH, HEAD_DIM * HEAD_DIM), lambda h: (0, h))),
        compiler_params=_cparams(("arbitrary",)),
        name="sample_rwkv_step",
    )(state_t, r_t, w_t, k_t, a_t, b_t, v_t)


def _sample_post_kernel(yt_ref, r_ref, k_ref, v_ref, g_ref, rk_ref, gng_ref, gnb_ref, o_ref):
    o_ref[...] = _rwkv_post(yt_ref[...].T, r_ref[...], k_ref[...], v_ref[...], g_ref[...], rk_ref[...],
                            gng_ref[...], gnb_ref[...], _head_ones())


def _sample_post(y, r, k, v, g, prm):
    return pl.pallas_call(
        _sample_post_kernel,
        out_shape=jax.ShapeDtypeStruct((DEC_BATCH, D_RWKV), F32),
        compiler_params=pltpu.CompilerParams(vmem_limit_bytes=VMEM_LIMIT),
        name="sample_rwkv_post",
    )(y, r, k, v, g, prm["r_k"], prm["gn_g"], prm["gn_b"])


def _project_mix(attn_ref, rwkv_ref, wo_ref):
    return (_dot(attn_ref[...].astype(BF16), wo_ref[0:D_ATTN, :])
            + _dot(rwkv_ref[...].astype(BF16), wo_ref[D_ATTN:D_ATTN + D_RWKV, :]))


def _norm_and_route(mix, x_ref, g_ref, b_ref, wr_ref, br_ref, x1_ref, x1b_ref, route_ref):
    x1 = _layer_norm(ALPHA * x_ref[...] + mix, g_ref[...], b_ref[...])
    x1_ref[...] = x1
    x1b = x1.astype(BF16)
    x1b_ref[...] = _pack_bf16_halves(x1b)
    logits = _dot(x1b, wr_ref[...].astype(BF16)) + br_ref[...]
    tm = logits.shape[0]
    lane = lax.broadcasted_iota(jnp.int32, (tm, LANES), 1).astype(F32)
    big = float(2 * LANES)
    neg = -jnp.inf
    lc = jnp.where(lane < N_GROUPS, logits, neg)
    mc = jnp.max(lc, axis=-1, keepdims=True)
    g_sel = jnp.min(jnp.where(lc == mc, lane, big), axis=-1, keepdims=True)
    p_group = 1.0 / jnp.sum(jnp.exp(lc - mc), axis=-1, keepdims=True)
    lo = ROUTE_FINE_OFF + g_sel * EXPERTS_PER_GROUP
    lf = jnp.where((lane >= lo) & (lane < lo + EXPERTS_PER_GROUP), logits, neg)
    v1 = jnp.max(lf, axis=-1, keepdims=True)
    i1 = jnp.min(jnp.where(lf == v1, lane, big), axis=-1, keepdims=True)
    lf2 = jnp.where(lane == i1, neg, lf)
    v2 = jnp.max(lf2, axis=-1, keepdims=True)
    i2 = jnp.min(jnp.where(lf2 == v2, lane, big), axis=-1, keepdims=True)
    e21 = jnp.exp(v2 - v1)
    gate1 = p_group / (1.0 + e21)
    gate2 = p_group * e21 / (1.0 + e21)
    route = jnp.where(lane == 0, i1 - ROUTE_FINE_OFF,
                      jnp.where(lane == 1, i2 - ROUTE_FINE_OFF,
                                jnp.where(lane == 2, gate1, jnp.where(lane == 3, gate2, 0.0))))
    route_ref[...] = route


N_ROUTER_OUTS = 3


def _outproj_router_kernel(n_tiles, n_aliased, attn_ref, rwkv_ref, x_ref, wo_ref, g_ref, b_ref, wr_ref, br_ref,
                           *rest):
    outs = rest[n_aliased:n_aliased + N_ROUTER_OUTS]
    mix_ref = rest[-1]
    i = pl.program_id(0)
    finish = lambda mix: _norm_and_route(mix, x_ref, g_ref, b_ref, wr_ref, br_ref, *outs)

    @pl.when(i == 0)
    def _():
        mix_ref[...] = _project_mix(attn_ref, rwkv_ref, wo_ref)

    @pl.when((i >= 1) & (i < n_tiles))
    def _():
        finish(mix_ref[...])
        mix_ref[...] = _project_mix(attn_ref, rwkv_ref, wo_ref)

    @pl.when(i == n_tiles)
    def _():
        finish(mix_ref[...])

    @pl.when(i > n_tiles)
    def _():
        for out_ref in outs:
            out_ref[...] = jnp.zeros_like(out_ref)


def _outproj_router(attn, rwkv, x, wo_bf16, ln_g, ln_b, w_route, b_route, tm, n_total, row_block, into, name):
    m = x.shape[0]
    n_tiles = m // tm
    const = lambda shape: pl.BlockSpec(shape, lambda i: (0, 0))
    ahead = lambda width: pl.BlockSpec((tm, width), lambda i: (jnp.minimum(i, n_tiles - 1), 0))
    behind = lambda width: pl.BlockSpec((tm, width), lambda i: (jnp.clip(i - 1, 0, n_tiles - 1), 0))
    in_specs = [ahead(D_ATTN), ahead(D_RWKV), behind(D_MODEL),
                const((D_MODEL, D_MODEL)), const((1, D_MODEL)), const((1, D_MODEL)),
                const((D_MODEL, LANES)), const((1, LANES))]
    args = [attn, rwkv, x, wo_bf16, ln_g, ln_b, w_route, b_route]
    aliases, n_aliased, fill_steps = {}, 0, pl.cdiv(n_total - m, tm)
    if into is not None:
        n_aliased, fill_steps = N_ROUTER_OUTS, 0
        in_specs += [pl.BlockSpec(memory_space=pl.ANY)] * N_ROUTER_OUTS
        aliases = {len(args) + k: k for k in range(N_ROUTER_OUTS)}
        args += list(into)
    out_rows = lambda width: pl.BlockSpec((tm, width), lambda i: (jnp.maximum(i - 1, 0) + row_block, 0))
    return pl.pallas_call(
        functools.partial(_outproj_router_kernel, n_tiles, n_aliased),
        out_shape=(jax.ShapeDtypeStruct((n_total, D_MODEL), F32),
                   jax.ShapeDtypeStruct((n_total, D_MODEL // 2), jnp.uint32),
                   jax.ShapeDtypeStruct((n_total, LANES), F32)),
        grid=(n_tiles + 1 + fill_steps,),
        in_specs=in_specs,
        out_specs=(out_rows(D_MODEL), out_rows(D_MODEL // 2), out_rows(LANES)),
        scratch_shapes=[pltpu.VMEM((tm, D_MODEL), F32)],
        input_output_aliases=aliases,
        compiler_params=_cparams(("arbitrary",)),
        name=name,
    )(*args)


DISPATCH_TILE = 128


def _dispatch_kernel(zoff_ref, dest_ref, x_ref, o_hbm, zbuf, ring, zsem, sem):
    i = pl.program_id(0)
    n_blocks = o_hbm.shape[0] // MOE_BLOCK
    n_used = zoff_ref[N_EXPERTS]

    def zero_fill(start_row):
        start_row = pl.multiple_of(start_row, MOE_BLOCK)
        return pltpu.make_async_copy(zbuf, o_hbm.at[pl.ds(start_row, MOE_BLOCK)], zsem)

    def zero_fills(action):
        for e in range(N_EXPERTS):
            @pl.when(zoff_ref[e] >= 0)
            def _():
                action(zero_fill(zoff_ref[e]))
        for b in range(n_blocks):
            @pl.when(b >= n_used)
            def _():
                action(zero_fill(b * MOE_BLOCK))

    @pl.when(i == 0)
    def _():
        zbuf[...] = jnp.zeros_like(zbuf)
        zero_fills(lambda copy: copy.start())
        zero_fills(lambda copy: copy.wait())

    cur = lax.rem(i, 2)

    def wait_rows(slot):
        for k in range(2):
            pltpu.make_async_copy(ring.at[slot], o_hbm.at[pl.ds(0, DISPATCH_TILE)], sem.at[slot]).wait()

    @pl.when(i >= 2)
    def _():
        wait_rows(cur)

    ring[cur] = x_ref[...]
    for t in range(DISPATCH_TILE):
        for k in range(2):
            pltpu.make_async_copy(ring.at[cur, pl.ds(t, 1)], o_hbm.at[pl.ds(dest_ref[0, 0, 2 * t + k], 1)],
                                  sem.at[cur]).start()

    @pl.when(i == pl.num_programs(0) - 1)
    def _():
        wait_rows(cur)

        @pl.when(i >= 1)
        def _():
            wait_rows(1 - cur)


def _dispatch(zero_offsets, dest, x_packed, n_blocks):
    n_tokens, width = x_packed.shape
    grid_spec = pltpu.PrefetchScalarGridSpec(
        num_scalar_prefetch=1,
        grid=(n_tokens // DISPATCH_TILE,),
        in_specs=[pl.BlockSpec((1, 1, 2 * DISPATCH_TILE), lambda i, z: (i, 0, 0), memory_space=pltpu.SMEM),
                  pl.BlockSpec((DISPATCH_TILE, width), lambda i, z: (i, 0))],
        out_specs=pl.BlockSpec(memory_space=pl.ANY),
        scratch_shapes=[pltpu.VMEM((MOE_BLOCK, width), x_packed.dtype),
                        pltpu.VMEM((2, DISPATCH_TILE, width), x_packed.dtype),
                        pltpu.SemaphoreType.DMA, pltpu.SemaphoreType.DMA((2,))],
    )
    return pl.pallas_call(
        _dispatch_kernel,
        out_shape=jax.ShapeDtypeStruct((n_blocks * MOE_BLOCK, width), x_packed.dtype),
        grid_spec=grid_spec,
        compiler_params=_cparams(("arbitrary",)),
        name="moe_dispatch",
    )(zero_offsets, dest.reshape(-1, 1, 2 * DISPATCH_TILE), x_packed)


def _expert_kernel(be_ref, nb_ref, x_ref, wg_hbm, wu_hbm, wd_hbm, o_ref, wg_buf, wu_buf, wd_buf, slot_ref, sem):
    blk = pl.program_id(0)
    n_used = nb_ref[0]
    expert = be_ref[blk]
    is_first = (blk == 0) | (be_ref[jnp.maximum(blk - 1, 0)] != expert)

    def fetch(e, slot):
        return [pltpu.make_async_copy(hbm.at[e], buf.at[slot], sem.at[slot, i])
                for i, (hbm, buf) in enumerate(((wg_hbm, wg_buf), (wu_hbm, wu_buf), (wd_hbm, wd_buf)))]

    @pl.when((blk < n_used) & is_first)
    def _():
        @pl.when(blk == 0)
        def _():
            slot_ref[0] = 1
            for copy in fetch(expert, 0):
                copy.start()

        slot = 1 - slot_ref[0]
        slot_ref[0] = slot
        for copy in fetch(expert, slot):
            copy.wait()
        nxt = lax.while_loop(lambda j: (j < n_used) & (be_ref[jnp.minimum(j, n_used - 1)] == expert),
                             lambda j: j + 1, blk + 1)

        @pl.when(nxt < n_used)
        def _():
            for copy in fetch(be_ref[jnp.minimum(nxt, n_used - 1)], 1 - slot):
                copy.start()

    @pl.when(blk < n_used)
    def _():
        slot = slot_ref[0]
        half = D_MODEL // 2
        x_head, x_tail = _unpack_bf16_halves(x_ref[...])
        proj = lambda w_buf: (_dot(x_head, w_buf[slot, 0:half, :].astype(BF16))
                              + _dot(x_tail, w_buf[slot, half:D_MODEL, :].astype(BF16)))
        gate = proj(wg_buf)
        up = proj(wu_buf)
        h = gate * _sigmoid(gate) * up
        o_ref[...] = _dot(h.astype(BF16), wd_buf[slot].astype(BF16))

    @pl.when(blk >= n_used)
    def _():
        o_ref[...] = jnp.zeros_like(o_ref)


def _expert_mlp(block_expert, n_used, x_sorted, w_gate, w_up, w_down, n_blocks):
    grid_spec = pltpu.PrefetchScalarGridSpec(
        num_scalar_prefetch=2,
        grid=(n_blocks,),
        in_specs=[pl.BlockSpec((MOE_BLOCK, D_MODEL // 2), lambda b, be, nb: (jnp.minimum(b, nb[0] - 1), 0)),
                  pl.BlockSpec(memory_space=pl.ANY), pl.BlockSpec(memory_space=pl.ANY),
                  pl.BlockSpec(memory_space=pl.ANY)],
        out_specs=pl.BlockSpec((MOE_BLOCK, D_MODEL), lambda b, be, nb: (b, 0)),
        scratch_shapes=[pltpu.VMEM((2, D_MODEL, D_EXPERT), F32), pltpu.VMEM((2, D_MODEL, D_EXPERT), F32),
                        pltpu.VMEM((2, D_EXPERT, D_MODEL), F32), pltpu.SMEM((1,), jnp.int32),
                        pltpu.SemaphoreType.DMA((2, 3))],
    )
    return pl.pallas_call(
        _expert_kernel,
        out_shape=jax.ShapeDtypeStruct((n_blocks * MOE_BLOCK, D_MODEL), F32),
        grid_spec=grid_spec,
        compiler_params=_cparams(("arbitrary",)),
        name="expert_mlp",
    )(block_expert, n_used, x_sorted, w_gate, w_up, w_down)


COMBINE_TILE = 128


def _combine_kernel(dest_ref, dest_next_ref, y_hbm, x1_ref, route_ref, g_ref, b_ref, o_ref, ybuf, sem):
    i = pl.program_id(0)
    cur = lax.rem(i, 2)
    n_rows = 2 * COMBINE_TILE

    def gather(table_ref, buf):
        for slot in range(n_rows):
            pltpu.make_async_copy(y_hbm.at[pl.ds(table_ref[0, 0, slot], 1)], ybuf.at[buf, pl.ds(slot, 1)],
                                  sem.at[buf]).start()

    def wait_gather(buf):
        pltpu.make_async_copy(y_hbm.at[pl.ds(0, n_rows)], ybuf.at[buf], sem.at[buf]).wait()

    @pl.when(i == 0)
    def _():
        gather(dest_ref, 0)

    gather(dest_next_ref, 1 - cur)
    wait_gather(cur)
    route = route_ref[...]
    yb = ybuf[cur]
    moe = route[:, 2:3] * yb[0:COMBINE_TILE, :] + route[:, 3:4] * yb[COMBINE_TILE:n_rows, :]
    o_ref[...] = _layer_norm(ALPHA * x1_ref[...] + moe, g_ref[...], b_ref[...])

    @pl.when(i == pl.num_programs(0) - 1)
    def _():
        wait_gather(1 - cur)


def _combine(dest, y_slots, x1_all, route_all, m, row_block, ln_g, ln_b, name):
    tm = COMBINE_TILE
    return pl.pallas_call(
        _combine_kernel,
        out_shape=jax.ShapeDtypeStruct((m, D_MODEL), F32),
        grid=(m // tm,),
        in_specs=[pl.BlockSpec((1, 1, 2 * tm), lambda i: (i, 0, 0), memory_space=pltpu.SMEM),
                  pl.BlockSpec((1, 1, 2 * tm), lambda i: (i + 1, 0, 0), memory_space=pltpu.SMEM),
                  pl.BlockSpec(memory_space=pl.ANY),
                  pl.BlockSpec((tm, D_MODEL), lambda i: (i + row_block, 0)),
                  pl.BlockSpec((tm, LANES), lambda i: (i + row_block, 0)),
                  pl.BlockSpec((1, D_MODEL), lambda i: (0, 0)),
                  pl.BlockSpec((1, D_MODEL), lambda i: (0, 0))],
        out_specs=pl.BlockSpec((tm, D_MODEL), lambda i: (i, 0)),
        scratch_shapes=[pltpu.VMEM((2, 2 * tm, D_MODEL), F32), pltpu.SemaphoreType.DMA((2,))],
        compiler_params=_cparams(("arbitrary",)),
        name=name,
    )(dest, dest, y_slots, x1_all, route_all, ln_g, ln_b)


def _dispatch_plan(route_all, n_blocks):
    flat_e = route_all[:, 0:2].astype(jnp.int32).reshape(-1)
    onehot = (flat_e[:, None] == jnp.arange(N_EXPERTS, dtype=jnp.int32)[None, :]).astype(jnp.int32)
    csum = jnp.cumsum(onehot, axis=0)
    rank = jnp.sum(onehot * csum, axis=1) - 1
    counts = csum[-1]
    padded = (counts + MOE_BLOCK - 1) // MOE_BLOCK * MOE_BLOCK
    pend = jnp.cumsum(padded)
    pstart = pend - padded
    dest = (pstart[flat_e] + rank).astype(jnp.int32)
    zero_offsets = jnp.where(counts > 0, pend - MOE_BLOCK, -1).astype(jnp.int32)
    n_used = (pend[-1] // MOE_BLOCK).astype(jnp.int32)
    block_start = jnp.minimum(jnp.arange(n_blocks, dtype=jnp.int32), n_used - 1) * MOE_BLOCK
    block_e = jnp.minimum(jnp.searchsorted(pend, block_start, side="right"), N_EXPERTS - 1).astype(jnp.int32)
    return dest, jnp.concatenate([zero_offsets, n_used.reshape(1)]), block_e, n_used.reshape(1)


def kernel(x_prompt, x_sample, cache_k_win, cache_v_win, state_wkv, state_shift, w_in, attn_sinks, shift_mu, w0,
           w_decay_up, a0, w_a_up, w_g_up, k_k, k_a, r_k, gn_g, gn_b, w_out, ln1_g, ln1_b, w_coarse, b_coarse,
           w_fine, b_fine, w_exp_gate, w_exp_up, w_exp_down, ln2_g, ln2_b):
    xp = x_prompt[0]
    xs = x_sample[:, 0]
    row = lambda a: a.reshape(1, -1)

    w_in_t = jnp.swapaxes(w_in[0], 0, 1)
    prm = dict(mu=row(shift_mu[0, :D_RKV]), mu_tail=row(shift_mu[0, D_RKV:]), w_tail=w_in_t[D_MAIN:].astype(BF16),
               w0=row(w0[0]), a0=row(a0[0]), k_k=row(k_k[0]), k_a=row(k_a[0]),
               r_k=row(r_k[0]), gn_g=row(gn_g[0]), gn_b=row(gn_b[0]),
               wd=w_decay_up[0], wa=w_a_up[0], wg=w_g_up[0])
    sinks = attn_sinks[0]
    wo_bf16 = w_out[0].astype(BF16)
    w_route = jnp.pad(jnp.concatenate([w_coarse[0], w_fine[0]], axis=1), ((0, 0), (0, LANES - N_GROUPS - N_EXPERTS)))
    b_route = jnp.pad(jnp.concatenate([b_coarse[0], b_fine[0]]), (0, LANES - N_GROUPS - N_EXPERTS)).reshape(1, LANES)

    hp = _matmul(xp, w_in_t, D_MAIN, MAIN_TM, MAIN_TN, "in_proj_prompt")
    hs = _matmul(xs, w_in_t, D_MAIN, DEC_BATCH, MAIN_TN, "in_proj_sample")

    attn_p = _prompt_attention(hp, sinks)
    rwkv_p, state_p, tail_p = _prompt_rwkv(hp, xp, prm)

    q_s = hs[:, :D_ATTN].reshape(DEC_BATCH, N_Q_HEADS, HEAD_DIM)
    k_s = hs[:, D_ATTN:D_ATTN + D_KV].reshape(DEC_BATCH, 1, D_KV)
    v_s = hs[:, D_ATTN + D_KV:D_QKV].reshape(DEC_BATCH, 1, D_KV)
    window_t = lambda c: jnp.transpose(c, (0, 2, 3, 1)).reshape(DEC_BATCH, D_KV, WINDOW)
    attn_s, kwin_s, vwin_s = _sample_attention(
        q_s, k_s, v_s, window_t(cache_k_win[0]), window_t(cache_v_win[0]), sinks.reshape(N_Q_HEADS, 1))
    r_s, k2_s, vv_s, g_s, tail_s, r_t, w_t, k_t, v_t, a_t, b_t = _sample_prep(hs, xs, state_shift[0], prm)
    y_t, state_s = _sample_step(jnp.transpose(state_wkv[0], (1, 2, 3, 0)), r_t, w_t, k_t, a_t, b_t, v_t)
    state_s = state_s.reshape(DEC_BATCH, N_RWKV_HEADS, HEAD_DIM, HEAD_DIM)
    rwkv_s = _sample_post(y_t, r_s, k2_s, vv_s, g_s, prm)

    n_tokens = SEQ + DEC_BATCH
    outs_pr = _outproj_router(attn_p, rwkv_p, xp, wo_bf16, row(ln1_g[0]), row(ln1_b[0]), w_route, b_route,
                              256, n_tokens, 0, None, "outproj_router_prompt")
    x1_all, x1b_all, route_all = _outproj_router(attn_s.reshape(DEC_BATCH, D_ATTN), rwkv_s, xs,
                                                 wo_bf16, row(ln1_g[0]), row(ln1_b[0]), w_route, b_route,
                                                 DEC_BATCH, n_tokens, SEQ // DEC_BATCH, outs_pr,
                                                 "outproj_router_sample")

    n_assign = 2 * n_tokens
    n_blocks = -(-(n_assign + N_EXPERTS * (MOE_BLOCK - 1)) // MOE_BLOCK)
    dest, zero_offsets, block_e, n_used = _dispatch_plan(route_all, n_blocks)
    x_sorted = _dispatch(zero_offsets, dest, x1b_all, n_blocks)
    y_slots = _expert_mlp(block_e, n_used, x_sorted, w_exp_gate[0], w_exp_up[0], w_exp_down[0], n_blocks)

    def dest_tiles(d):
        d = d.reshape(-1, COMBINE_TILE, 2)
        d = jnp.concatenate([d[:, :, 0], d[:, :, 1]], axis=1)
        return jnp.pad(d, ((0, 1), (0, 0))).reshape(-1, 1, 2 * COMBINE_TILE)

    y_p = _combine(dest_tiles(dest[:2 * SEQ]), y_slots, x1_all, route_all, SEQ, 0, row(ln2_g[0]), row(ln2_b[0]),
                   "combine_prompt")
    y_s = _combine(dest_tiles(dest[2 * SEQ:]), y_slots, x1_all, route_all, DEC_BATCH, SEQ // COMBINE_TILE,
                   row(ln2_g[0]), row(ln2_b[0]), "combine_sample")

    kv4 = lambda a: a.reshape(a.shape[0], N_KV_HEADS, HEAD_DIM)
    k_win_p = kv4(hp[SEQ - WINDOW:, D_ATTN:D_ATTN + D_KV])[None, None]
    v_win_p = kv4(hp[SEQ - WINDOW:, D_ATTN + D_KV:D_QKV])[None, None]
    sp = state_p.reshape(N_PAIRS, HEADS_PER_TILE, HEAD_DIM, HEADS_PER_TILE, HEAD_DIM)
    wkv_p = jnp.stack([sp[:, i, :, i, :] for i in range(HEADS_PER_TILE)], axis=1)
    wkv_p = wkv_p.reshape(N_RWKV_HEADS, HEAD_DIM, HEAD_DIM).transpose(0, 2, 1)[None, None]
    shift_p = jnp.concatenate([hp[SEQ - 1:SEQ, D_QKV:], tail_p[0:1]], axis=1)[None]
    shift_s = jnp.concatenate([hs[:, D_QKV:], tail_s], axis=1)[None]
    return (y_p[None], y_s[:, None, :], k_win_p, v_win_p, wkv_p, shift_p,
            kwin_s.reshape(1, DEC_BATCH, WINDOW, N_KV_HEADS, HEAD_DIM),
            vwin_s.reshape(1, DEC_BATCH, WINDOW, N_KV_HEADS, HEAD_DIM),
            state_s[None], shift_s)
```

```python
import functools
import math

import jax
import jax.numpy as jnp
from jax import lax
from jax.experimental import pallas as pl
from jax.experimental.pallas import tpu as pltpu

F32 = jnp.float32
BF16 = jnp.bfloat16

D_MODEL = 2048
SEQ = 8192
DEC_BATCH = 128
HEAD_DIM = 64
D_ATTN = 1024
D_RWKV = 1024
N_Q_HEADS = 16
N_KV_HEADS = 4
Q_PER_KV = 4
D_KV = 256
WINDOW = 128
ATTN_SCALE = HEAD_DIM ** -0.5
N_RWKV_HEADS = 16
W_LORA = 64
A_LORA = 64
G_LORA = 160
D_SHIFT = 3 * D_RWKV + W_LORA + A_LORA + G_LORA
D_QKV = D_ATTN + 2 * D_KV
N_GROUPS = 4
EXPERTS_PER_GROUP = 8
N_EXPERTS = 32
D_EXPERT = 512
ALPHA = 2.0 ** 0.25
LN_EPS = 1e-5
GN_EPS = 64e-5

SUBLANES = 8
LANES = 128
VMEM_LIMIT = 52 * 1024 * 1024

D_RKV = 3 * D_RWKV
D_TAIL = W_LORA + A_LORA + G_LORA
D_MAIN = D_QKV + D_RKV
MAIN_TN = 1536
MAIN_TM = 512

CHUNK = 64
HEADS_PER_TILE = LANES // HEAD_DIM
N_PAIRS = N_RWKV_HEADS // HEADS_PER_TILE
SOLVE_LEVELS = int(math.log2(CHUNK))
PAIR_GROUP = 8

MOE_BLOCK = 256
ROUTE_FINE_OFF = N_GROUPS

NN = (((1,), (0,)), ((), ()))
NT = (((1,), (1,)), ((), ()))


def _dot(a, b, dims=NN):
    return lax.dot_general(a, b, dims, preferred_element_type=F32)


def _dot1(a, b, dims=NN):
    return _dot(a.astype(BF16), b.astype(BF16), dims)


def _split(x):
    hi = x.astype(BF16)
    lo = (x - hi.astype(F32)).astype(BF16)
    return hi, lo


def _dot3(a, b, dims=NN):
    ah, al = _split(a)
    bh, bl = _split(b)
    return _dot(ah, bh, dims) + (_dot(ah, bl, dims) + _dot(al, bh, dims))


def _dot_exact_lhs(a_bf16, b, dims=NN):
    bh, bl = _split(b)
    return _dot(a_bf16, bh, dims) + _dot(a_bf16, bl, dims)


def _dot_exact_rhs(a, b_bf16, dims=NN):
    ah, al = _split(a)
    return _dot(ah, b_bf16, dims) + _dot(al, b_bf16, dims)


def _div_pow2(x, d):
    return lax.shift_right_logical(x, jnp.int32(int(math.log2(d))))


def _mod_pow2(x, d):
    return lax.bitwise_and(x, jnp.int32(d - 1))


def _pack_bf16_halves(x_bf16):
    n = x_bf16.shape[1] // 2
    bits = lax.bitcast_convert_type(x_bf16.astype(F32), jnp.uint32)
    return lax.bitwise_or(bits[:, 0:n], lax.shift_right_logical(bits[:, n:2 * n], jnp.uint32(16)))


def _unpack_bf16_halves(packed):
    hi = lax.bitcast_convert_type(lax.bitwise_and(packed, jnp.uint32(0xFFFF0000)), F32)
    lo = lax.bitcast_convert_type(lax.shift_left(packed, jnp.uint32(16)), F32)
    return hi.astype(BF16), lo.astype(BF16)


def _sigmoid(x):
    return 1.0 / (1.0 + jnp.exp(-x))


def _softplus(x):
    return jnp.maximum(x, 0.0) + jnp.log(1.0 + jnp.exp(-jnp.abs(x)))


def _layer_norm(z, g, b):
    mu = jnp.mean(z, axis=-1, keepdims=True)
    d = z - mu
    var = jnp.mean(d * d, axis=-1, keepdims=True)
    return d * lax.rsqrt(var + LN_EPS) * g + b


def _cparams(sem):
    return pltpu.CompilerParams(dimension_semantics=sem, vmem_limit_bytes=VMEM_LIMIT)


def _matmul_kernel(x_ref, wt_ref, o_ref):
    o_ref[...] = _dot(x_ref[...].astype(BF16), wt_ref[...].astype(BF16), NT)


def _matmul(x, w_t, n_out, tm, tn, name):
    m, k = x.shape
    tm = min(tm, m)
    return pl.pallas_call(
        _matmul_kernel,
        out_shape=jax.ShapeDtypeStruct((m, n_out), F32),
        grid=(n_out // tn, m // tm),
        in_specs=[pl.BlockSpec((tm, k), lambda j, i: (i, 0)),
                  pl.BlockSpec((tn, k), lambda j, i: (j, 0))],
        out_specs=pl.BlockSpec((tm, tn), lambda j, i: (i, j)),
        compiler_params=_cparams(("arbitrary", "arbitrary")),
        name=name,
    )(x, w_t)


def _prompt_attn_kernel(q_ref, kvp_ref, kvc_ref, sink_ref, o_ref):
    blk = pl.program_id(0)
    q = q_ref[...]
    kv_prev = kvp_ref[...]
    kv_cur = kvc_ref[...]
    qi = _mod_pow2(lax.broadcasted_iota(jnp.int32, (Q_PER_KV * WINDOW, 2 * WINDOW), 0), WINDOW)
    kj = lax.broadcasted_iota(jnp.int32, (Q_PER_KV * WINDOW, 2 * WINDOW), 1)
    diff = qi + WINDOW - kj
    mask = (diff >= 0) & (diff <= WINDOW) & ((blk > 0) | (kj >= WINDOW))
    row_head = _div_pow2(lax.broadcasted_iota(jnp.int32, (Q_PER_KV * WINDOW, 1), 0), WINDOW)
    groups = range(N_KV_HEADS)
    kv_cols = lambda off, g: jnp.concatenate([kv_prev[:, off + g * HEAD_DIM:off + (g + 1) * HEAD_DIM],
                                              kv_cur[:, off + g * HEAD_DIM:off + (g + 1) * HEAD_DIM]],
                                             axis=0).astype(BF16)
    q_rows = lambda g: jnp.concatenate(
        [q[:, (g * Q_PER_KV + h) * HEAD_DIM:(g * Q_PER_KV + h + 1) * HEAD_DIM] for h in range(Q_PER_KV)],
        axis=0).astype(BF16)
    s = [jnp.where(mask, _dot(q_rows(g), kv_cols(0, g), NT) * ATTN_SCALE, -jnp.inf) for g in groups]
    sink = []
    for g in groups:
        col = jnp.zeros((Q_PER_KV * WINDOW, 1), F32)
        for h in range(Q_PER_KV):
            col = jnp.where(row_head == h, sink_ref[g * Q_PER_KV + h], col)
        sink.append(col)
    m = [jnp.maximum(jnp.max(s[g], axis=-1, keepdims=True), sink[g]) for g in groups]
    p = [jnp.exp(s[g] - m[g]) for g in groups]
    denom = [jnp.sum(p[g], axis=-1, keepdims=True) + jnp.exp(sink[g] - m[g]) for g in groups]
    o = [_dot((p[g] / denom[g]).astype(BF16), kv_cols(D_KV, g)) for g in groups]
    o_ref[...] = jnp.concatenate([o[g][h * WINDOW:(h + 1) * WINDOW, :] for g in groups for h in range(Q_PER_KV)],
                                 axis=1)


def _prompt_attention(h_attn, sinks):
    nb = SEQ // WINDOW
    return pl.pallas_call(
        _prompt_attn_kernel,
        out_shape=jax.ShapeDtypeStruct((SEQ, D_ATTN), F32),
        grid=(nb,),
        in_specs=[pl.BlockSpec((WINDOW, D_ATTN), lambda i: (i, 0)),
                  pl.BlockSpec((WINDOW, 2 * D_KV), lambda i: (jnp.maximum(i - 1, 0), 2)),
                  pl.BlockSpec((WINDOW, 2 * D_KV), lambda i: (i, 2)),
                  pl.BlockSpec(memory_space=pltpu.SMEM)],
        out_specs=pl.BlockSpec((WINDOW, D_ATTN), lambda i: (i, 0)),
        compiler_params=_cparams(("arbitrary",)),
        name="prompt_attention",
    )(h_attn, h_attn, h_attn, sinks)


SAMPLE_ATTN_TILE = 8


def _sample_attn_kernel(q_ref, knew_ref, vnew_ref, ck_ref, cv_ref, sink_ref, o_ref, kwin_ref, vwin_ref):
    lane = lax.broadcasted_iota(jnp.int32, (N_Q_HEADS, D_KV), 1)
    head = lax.broadcasted_iota(jnp.int32, (N_Q_HEADS, D_KV), 0)
    group_mask = _div_pow2(lane, HEAD_DIM) == _div_pow2(head, Q_PER_KV)
    sink = sink_ref[...]
    row = lax.broadcasted_iota(jnp.int32, (WINDOW, D_KV), 0)
    seqs = range(SAMPLE_ATTN_TILE)
    qbd = [jnp.where(group_mask, jnp.concatenate([q_ref[b]] * N_KV_HEADS, axis=1), 0.0).astype(BF16) for b in seqs]
    s = [_dot1(qbd[b], ck_ref[b]) * ATTN_SCALE for b in seqs]
    s_new = [jnp.sum(qbd[b].astype(F32) * knew_ref[b].astype(BF16).astype(F32), axis=-1, keepdims=True) * ATTN_SCALE
             for b in seqs]
    m = [jnp.maximum(jnp.maximum(jnp.max(s[b], axis=-1, keepdims=True), s_new[b]), sink) for b in seqs]
    p = [jnp.exp(s[b] - m[b]) for b in seqs]
    p_new = [jnp.exp(s_new[b] - m[b]) for b in seqs]
    denom = [jnp.sum(p[b], axis=-1, keepdims=True) + p_new[b] + jnp.exp(sink - m[b]) for b in seqs]
    for b in seqs:
        kb = ck_ref[b].T
        vb = cv_ref[b].T
        kn = knew_ref[b]
        vn = vnew_ref[b]
        o_full = (_dot1(p[b] / denom[b], vb)
                  + (p_new[b] / denom[b]).astype(BF16).astype(F32) * vn.astype(BF16).astype(F32))
        o_full = jnp.where(group_mask, o_full, 0.0)
        o = o_full[:, 0:HEAD_DIM]
        for g in range(1, N_KV_HEADS):
            o = o + o_full[:, g * HEAD_DIM:(g + 1) * HEAD_DIM]
        o_ref[b] = o
        kwin_ref[b] = jnp.where(row == WINDOW - 1, kn, pltpu.roll(kb, WINDOW - 1, axis=0))
        vwin_ref[b] = jnp.where(row == WINDOW - 1, vn, pltpu.roll(vb, WINDOW - 1, axis=0))


def _sample_attention(q, k_new, v_new, cache_k_t, cache_v_t, sinks):
    bt = SAMPLE_ATTN_TILE
    win_spec = pl.BlockSpec((bt, WINDOW, D_KV), lambda i: (i, 0, 0))
    win_t_spec = pl.BlockSpec((bt, D_KV, WINDOW), lambda i: (i, 0, 0))
    new_spec = pl.BlockSpec((bt, 1, D_KV), lambda i: (i, 0, 0))
    return pl.pallas_call(
        _sample_attn_kernel,
        out_shape=(jax.ShapeDtypeStruct((DEC_BATCH, N_Q_HEADS, HEAD_DIM), F32),
                   jax.ShapeDtypeStruct((DEC_BATCH, WINDOW, D_KV), F32),
                   jax.ShapeDtypeStruct((DEC_BATCH, WINDOW, D_KV), F32)),
        grid=(DEC_BATCH // bt,),
        in_specs=[pl.BlockSpec((bt, N_Q_HEADS, HEAD_DIM), lambda i: (i, 0, 0)),
                  new_spec, new_spec, win_t_spec, win_t_spec,
                  pl.BlockSpec((N_Q_HEADS, 1), lambda i: (0, 0))],
        out_specs=(pl.BlockSpec((bt, N_Q_HEADS, HEAD_DIM), lambda i: (i, 0, 0)), win_spec, win_spec),
        compiler_params=_cparams(("arbitrary",)),
        name="sample_attention",
    )(q, k_new, v_new, cache_k_t, cache_v_t, sinks)


def _head_ones():
    r = _div_pow2(lax.broadcasted_iota(jnp.int32, (LANES, LANES), 0), HEAD_DIM)
    c = _div_pow2(lax.broadcasted_iota(jnp.int32, (LANES, LANES), 1), HEAD_DIM)
    return jnp.where(r == c, 1.0, 0.0).astype(BF16)


def _head_sum(x, ones):
    parts = [_dot_exact_rhs(x[:, p * LANES:(p + 1) * LANES], ones) for p in range(x.shape[1] // LANES)]
    return jnp.concatenate(parts, axis=1)


def _token_mix(feat, shifted, mu):
    return feat + (shifted - feat) * mu


def _rwkv_prep(mixed, mixed_tail, w0, a0, k_k, k_a, wd, wa, wg, ones):
    r = mixed[:, 0:D_RWKV]
    k = mixed[:, D_RWKV:2 * D_RWKV]
    v = mixed[:, 2 * D_RWKV:3 * D_RWKV]
    xw = mixed_tail[:, 0:W_LORA]
    xa = mixed_tail[:, W_LORA:W_LORA + A_LORA]
    xg = mixed_tail[:, W_LORA + A_LORA:D_TAIL]
    w_log = -_softplus(-(w0 + _dot1(jnp.tanh(xw), wd))) - 0.5
    log_decay = -jnp.exp(w_log)
    a = _sigmoid(a0 + _dot1(xa, wa))
    g = _dot1(_sigmoid(xg), wg)
    kk = k * k_k
    kk = kk * lax.rsqrt(jnp.maximum(_head_sum(kk * kk, ones), 1e-24))
    k2 = k * (1.0 + (a - 1.0) * k_a)
    return r, log_decay, k2, v, -kk, kk * a, g


def _rwkv_post(y, r, k2, v, g, r_k, gn_g, gn_b, ones):
    inv_n = 1.0 / HEAD_DIM
    mu = _head_sum(y, ones) * inv_n
    d = y - mu
    var = _head_sum(d * d, ones) * inv_n
    yn = d * lax.rsqrt(var + GN_EPS) * gn_g + gn_b
    bonus = _head_sum(r * k2 * r_k, ones) * v
    return (yn + bonus) * g


(OP_AABS, OP_RABS, OP_AN, OP_RN, OP_BN, OP_KN, OP_BH, OP_KH, OP_V) = range(9)
N_OPS = 9


def _prompt_rwkv_kernel(f1_ref, f2_ref, x_ref, wt_ref, mu_ref, mut_ref, w0_ref, a0_ref, kk_ref, ka_ref, rk_ref,
                        gng_ref, gnb_ref, wd_ref, wa_ref, wg_ref, out_ref, state_ref, tail_ref,
                        prev_ref, prevt_ref, s_ref, ops_ref, pc_ref, y_ref):
    c = pl.program_id(0)
    C = CHUNK

    @pl.when(c == 0)
    def _():
        prev_ref[...] = jnp.zeros_like(prev_ref)
        prevt_ref[...] = jnp.zeros_like(prevt_ref)
        s_ref[...] = jnp.zeros_like(s_ref)

    ones = _head_ones()
    row = lax.broadcasted_iota(jnp.int32, (C, 1), 0)

    def token_shift(feat, carry_ref):
        shifted = jnp.where(row == 0, carry_ref[0:1, :], pltpu.roll(feat, 1, axis=0))
        carry_ref[0:1, :] = feat[C - 1:C, :]
        return shifted

    feat = jnp.concatenate([f1_ref[...], f2_ref[...]], axis=1)
    tail = _dot1(x_ref[...], wt_ref[...], NT)
    mixed = _token_mix(feat, token_shift(feat, prev_ref), mu_ref[...])
    mixed_tail = _token_mix(tail, token_shift(tail, prevt_ref), mut_ref[...])
    tail_ref[...] = prevt_ref[...]
    r, ld, k2, v, av, bv, g = _rwkv_prep(mixed, mixed_tail, w0_ref[...], a0_ref[...], kk_ref[...], ka_ref[...],
                                         wd_ref[...], wa_ref[...], wg_ref[...], ones)

    ti = lax.broadcasted_iota(jnp.int32, (C, C), 0)
    tj = lax.broadcasted_iota(jnp.int32, (C, C), 1)
    tri_incl = jnp.where(tj <= ti, 1.0, 0.0).astype(BF16)
    cs = _dot_exact_lhs(tri_incl, ld)
    cs_ref = cs[C // 2 - 1:C // 2, :]
    cs_end = cs[C - 1:C, :]
    e_prev = jnp.exp(cs - ld)
    e_cur = jnp.exp(cs)
    n_prev = jnp.exp(cs - ld - cs_ref)
    n_cur = jnp.exp(cs - cs_ref)
    n_inv = jnp.exp(cs_ref - cs)
    e_tail = jnp.exp(cs_end - cs)
    ops = {OP_AABS: av * e_prev, OP_RABS: r * e_cur, OP_AN: av * n_prev, OP_RN: r * n_cur,
           OP_BN: bv * n_inv, OP_KN: k2 * n_inv, OP_BH: bv * e_tail, OP_KH: k2 * e_tail, OP_V: v}
    p_end = jnp.exp(cs_end)
    for p in range(N_PAIRS):
        sl = slice(p * LANES, (p + 1) * LANES)
        for idx, val in ops.items():
            ops_ref[p, idx] = val[:, sl]
        pc_ref[p] = jnp.broadcast_to(p_end[:, sl], (SUBLANES, LANES))

    lane1 = lax.broadcasted_iota(jnp.int32, (C, LANES), 1)
    head0 = lane1 < HEAD_DIM
    r2 = lax.broadcasted_iota(jnp.int32, (2 * C, 2 * C), 0)
    c2 = lax.broadcasted_iota(jnp.int32, (2 * C, 2 * C), 1)
    tq = _mod_pow2(r2, C)
    tk = _mod_pow2(c2, C)
    band = (tk < tq) | ((tk == tq) & (r2 >= C))
    blockdiag = _div_pow2(r2, HEAD_DIM) == _div_pow2(c2, HEAD_DIM)

    op = lambda p, idx: ops_ref[p, idx]
    zero_half = jnp.zeros((C, LANES), F32)
    for pairs in [range(g, g + PAIR_GROUP) for g in range(0, N_PAIRS, PAIR_GROUP)]:
        gy = {p: _dot1(jnp.concatenate([op(p, OP_AABS), op(p, OP_RABS)], axis=0), s_ref[p]) for p in pairs}

        am0, am1 = {}, {}
        for p in pairs:
            a_n, r_n = op(p, OP_AN), op(p, OP_RN)
            b0, k0 = jnp.where(head0, op(p, OP_BN), 0.0), jnp.where(head0, op(p, OP_KN), 0.0)
            b1, k1 = jnp.where(head0, 0.0, op(p, OP_BN)), jnp.where(head0, 0.0, op(p, OP_KN))
            am = _dot1(jnp.concatenate([a_n, r_n], axis=0), jnp.concatenate([k0, b0, b1, k1], axis=0), NT)
            am0[p] = jnp.where(band, am[:, 0:2 * C], 0.0)
            am1[p] = jnp.where(band, am[:, 2 * C:4 * C], 0.0)

        w0, w1 = {}, {}
        for p in pairs:
            top0, top1 = am0[p][0:C], am1[p][0:C]
            ak = jnp.concatenate([jnp.where(head0, top0, 0.0), jnp.where(head0, 0.0, top1)], axis=0)
            vv = op(p, OP_V)
            g0 = gy[p][0:C]
            m = jnp.concatenate([g0, g0], axis=0) + _dot1(ak, jnp.concatenate([vv, vv], axis=0))
            w0[p] = jnp.where(head0, m[0:C], top0)
            w1[p] = jnp.where(head0, top1, m[C:2 * C])

        for lvl in range(SOLVE_LEVELS):
            prod0 = {p: _dot1(w0[p], jnp.concatenate([zero_half, w0[p]], axis=0)) for p in pairs}
            prod1 = {p: _dot1(w1[p], jnp.concatenate([w1[p], zero_half], axis=0)) for p in pairs}
            w0 = {p: jnp.where(head0, w0[p] + prod0[p], prod0[p]) for p in pairs}
            w1 = {p: jnp.where(head0, prod1[p], w1[p] + prod1[p]) for p in pairs}
        u = {p: jnp.where(head0, w0[p], w1[p]) for p in pairs}

        for p in pairs:
            vv = op(p, OP_V)
            y_lhs = jnp.concatenate([am0[p][C:2 * C], am1[p][C:2 * C]], axis=1)
            y_rhs = jnp.concatenate([jnp.where(head0, vv, 0.0), jnp.where(head0, u[p], 0.0),
                                     jnp.where(head0, 0.0, u[p]), jnp.where(head0, 0.0, vv)], axis=0)
            y_ref[p] = gy[p][C:2 * C] + _dot1(y_lhs, y_rhs)

        for p in pairs:
            decay_rows = jnp.broadcast_to(pc_ref[p][0:1, :], (LANES, LANES)).T
            upd_lhs = jnp.concatenate([op(p, OP_BH), op(p, OP_KH)], axis=0).T
            upd_rhs = jnp.concatenate([u[p], op(p, OP_V)], axis=0)
            s_ref[p] = s_ref[p] * decay_rows + jnp.where(blockdiag, _dot1(upd_lhs, upd_rhs), 0.0)

    y = jnp.concatenate([y_ref[p] for p in range(N_PAIRS)], axis=1)
    out_ref[...] = _rwkv_post(y, r, k2, v, g, rk_ref[...], gng_ref[...], gnb_ref[...], ones)

    @pl.when(c == pl.num_programs(0) - 1)
    def _():
        state_ref[...] = s_ref[...]


def _prompt_rwkv(h_main, x, prm):
    n_chunks = SEQ // CHUNK
    half = D_RKV // 2
    assert D_QKV == half
    vec = pl.BlockSpec((1, D_RWKV), lambda c: (0, 0))
    full = lambda a: pl.BlockSpec(a.shape, lambda c: (0,) * a.ndim)
    return pl.pallas_call(
        _prompt_rwkv_kernel,
        out_shape=(jax.ShapeDtypeStruct((SEQ, D_RWKV), F32),
                   jax.ShapeDtypeStruct((N_PAIRS, LANES, LANES), F32),
                   jax.ShapeDtypeStruct((SUBLANES, D_TAIL), F32)),
        grid=(n_chunks,),
        in_specs=[pl.BlockSpec((CHUNK, half), lambda c: (c, 1)),
                  pl.BlockSpec((CHUNK, half), lambda c: (c, 2)),
                  pl.BlockSpec((CHUNK, D_MODEL), lambda c: (c, 0)),
                  full(prm["w_tail"]), full(prm["mu"]), full(prm["mu_tail"]),
                  vec, vec, vec, vec, vec, vec, vec,
                  full(prm["wd"]), full(prm["wa"]), full(prm["wg"])],
        out_specs=(pl.BlockSpec((CHUNK, D_RWKV), lambda c: (c, 0)),
                   pl.BlockSpec((N_PAIRS, LANES, LANES), lambda c: (0, 0, 0)),
                   pl.BlockSpec((SUBLANES, D_TAIL), lambda c: (0, 0))),
        scratch_shapes=[pltpu.VMEM((SUBLANES, D_RKV), F32),
                        pltpu.VMEM((SUBLANES, D_TAIL), F32),
                        pltpu.VMEM((N_PAIRS, LANES, LANES), F32),
                        pltpu.VMEM((N_PAIRS, N_OPS, CHUNK, LANES), F32),
                        pltpu.VMEM((N_PAIRS, SUBLANES, LANES), F32),
                        pltpu.VMEM((N_PAIRS, CHUNK, LANES), F32)],
        compiler_params=_cparams(("arbitrary",)),
        name="prompt_rwkv",
    )(h_main, h_main, x, prm["w_tail"], prm["mu"], prm["mu_tail"], prm["w0"], prm["a0"], prm["k_k"], prm["k_a"],
      prm["r_k"], prm["gn_g"], prm["gn_b"], prm["wd"], prm["wa"], prm["wg"])


def _sample_prep_kernel(h_ref, x_ref, wt_ref, shift_ref, mu_ref, mut_ref, w0_ref, a0_ref, kk_ref, ka_ref,
                        wd_ref, wa_ref, wg_ref, r_ref, k_ref, v_ref, g_ref, tail_ref,
                        rt_ref, wtr_ref, kt_ref, vt_ref, at_ref, bt_ref):
    ones = _head_ones()
    feat = h_ref[:, D_QKV:D_MAIN]
    tail = _dot1(x_ref[...], wt_ref[...], NT)
    tail_ref[...] = tail
    mixed = _token_mix(feat, shift_ref[:, 0:D_RKV], mu_ref[...])
    mixed_tail = _token_mix(tail, shift_ref[:, D_RKV:D_SHIFT], mut_ref[...])
    r, ld, k2, v, av, bv, g = _rwkv_prep(mixed, mixed_tail, w0_ref[...], a0_ref[...], kk_ref[...], ka_ref[...],
                                         wd_ref[...], wa_ref[...], wg_ref[...], ones)
    r_ref[...] = r
    k_ref[...] = k2
    v_ref[...] = v
    g_ref[...] = g
    rt_ref[...] = r.T
    wtr_ref[...] = jnp.exp(ld).T
    kt_ref[...] = k2.T
    vt_ref[...] = v.T
    at_ref[...] = av.T
    bt_ref[...] = bv.T


def _sample_prep(h_main, x, shift, prm):
    tok = jax.ShapeDtypeStruct((DEC_BATCH, D_RWKV), F32)
    chan = jax.ShapeDtypeStruct((D_RWKV, DEC_BATCH), F32)
    return pl.pallas_call(
        _sample_prep_kernel,
        out_shape=(tok,) * 4 + (jax.ShapeDtypeStruct((DEC_BATCH, D_TAIL), F32),) + (chan,) * 6,
        compiler_params=pltpu.CompilerParams(vmem_limit_bytes=VMEM_LIMIT),
        name="sample_rwkv_prep",
    )(h_main, x, prm["w_tail"], shift, prm["mu"], prm["mu_tail"], prm["w0"], prm["a0"], prm["k_k"], prm["k_a"],
      prm["wd"], prm["wa"], prm["wg"])


STEP_GROUP = 4


def _sample_step_kernel(s_ref, r_ref, w_ref, k_ref, a_ref, b_ref, v_ref, y_ref, snew_ref):
    r, w, k, a, b = r_ref[...], w_ref[...], k_ref[...], a_ref[...], b_ref[...]
    for g0 in range(0, HEAD_DIM, 2 * STEP_GROUP):
        chans = range(g0, g0 + 2 * STEP_GROUP)
        sa = {i: jnp.sum(s_ref[0, i] * a, axis=0, keepdims=True) for i in chans}
        s_new = {i: s_ref[0, i] * w + sa[i] * b + v_ref[i:i + 1, :] * k for i in chans}
        for i in chans:
            y_ref[i:i + 1, :] = jnp.sum(s_new[i] * r, axis=0, keepdims=True)
        for i in range(g0, g0 + 2 * STEP_GROUP, 2):
            pair = jnp.concatenate([s_new[i], s_new[i + 1]], axis=0)
            snew_ref[:, i * HEAD_DIM:(i + 2) * HEAD_DIM] = pair.T


def _sample_step(state_t, r_t, w_t, k_t, a_t, b_t, v_t):
    head_rows = pl.BlockSpec((HEAD_DIM, DEC_BATCH), lambda h: (h, 0))
    return pl.pallas_call(
        _sample_step_kernel,
        out_shape=(jax.ShapeDtypeStruct((D_RWKV, DEC_BATCH), F32),
                   jax.ShapeDtypeStruct((DEC_BATCH, N_RWKV_HEADS * HEAD_DIM * HEAD_DIM), F32)),
        grid=(N_RWKV_HEADS,),
        in_specs=[pl.BlockSpec((1, HEAD_DIM, HEAD_DIM, DEC_BATCH), lambda h: (h, 0, 0, 0))] + [head_rows] * 6,
        out_specs=(head_rows, pl.BlockSpec((DEC_BATCH, HEAD_DIM * HEAD_DIM), lambda h: (0, h))),
        compiler_params=_cparams(("arbitrary",)),
        name="sample_rwkv_step",
    )(state_t, r_t, w_t, k_t, a_t, b_t, v_t)


def _sample_post_kernel(yt_ref, r_ref, k_ref, v_ref, g_ref, rk_ref, gng_ref, gnb_ref, o_ref):
    o_ref[...] = _rwkv_post(yt_ref[...].T, r_ref[...], k_ref[...], v_ref[...], g_ref[...], rk_ref[...],
                            gng_ref[...], gnb_ref[...], _head_ones())


def _sample_post(y, r, k, v, g, prm):
    return pl.pallas_call(
        _sample_post_kernel,
        out_shape=jax.ShapeDtypeStruct((DEC_BATCH, D_RWKV), F32),
        compiler_params=pltpu.CompilerParams(vmem_limit_bytes=VMEM_LIMIT),
        name="sample_rwkv_post",
    )(y, r, k, v, g, prm["r_k"], prm["gn_g"], prm["gn_b"])


def _project_mix(attn_ref, rwkv_ref, wo_ref):
    return (_dot(attn_ref[...].astype(BF16), wo_ref[0:D_ATTN, :])
            + _dot(rwkv_ref[...].astype(BF16), wo_ref[D_ATTN:D_ATTN + D_RWKV, :]))


def _norm_and_route(mix, x_ref, g_ref, b_ref, wr_ref, br_ref, x1_ref, x1b_ref, route_ref, counts_ref, base_ref):
    x1 = _layer_norm(ALPHA * x_ref[...] + mix, g_ref[...], b_ref[...])
    x1_ref[...] = x1
    x1b = x1.astype(BF16)
    x1b_ref[...] = _pack_bf16_halves(x1b)
    logits = _dot(x1b, wr_ref[...].astype(BF16)) + br_ref[...]
    tm = logits.shape[0]
    lane = lax.broadcasted_iota(jnp.int32, (tm, LANES), 1).astype(F32)
    big = float(2 * LANES)
    neg = -jnp.inf
    lc = jnp.where(lane < N_GROUPS, logits, neg)
    mc = jnp.max(lc, axis=-1, keepdims=True)
    g_sel = jnp.min(jnp.where(lc == mc, lane, big), axis=-1, keepdims=True)
    p_group = 1.0 / jnp.sum(jnp.exp(lc - mc), axis=-1, keepdims=True)
    lo = ROUTE_FINE_OFF + g_sel * EXPERTS_PER_GROUP
    lf = jnp.where((lane >= lo) & (lane < lo + EXPERTS_PER_GROUP), logits, neg)
    v1 = jnp.max(lf, axis=-1, keepdims=True)
    i1 = jnp.min(jnp.where(lf == v1, lane, big), axis=-1, keepdims=True)
    lf2 = jnp.where(lane == i1, neg, lf)
    v2 = jnp.max(lf2, axis=-1, keepdims=True)
    i2 = jnp.min(jnp.where(lf2 == v2, lane, big), axis=-1, keepdims=True)
    e21 = jnp.exp(v2 - v1)
    gate1 = p_group / (1.0 + e21)
    gate2 = p_group * e21 / (1.0 + e21)
    pick1 = jnp.where(lane == i1, 1.0, 0.0)
    pick2 = jnp.where(lane == i2, 1.0, 0.0)
    picks = pick1 + pick2
    ti = lax.broadcasted_iota(jnp.int32, (tm, tm), 0)
    tj = lax.broadcasted_iota(jnp.int32, (tm, tm), 1)
    earlier = jnp.where(tj < ti, 1.0, 0.0).astype(BF16)
    before = _dot(earlier, picks.astype(BF16)) + base_ref[0:1, :]
    rank1 = jnp.sum(pick1 * before, axis=-1, keepdims=True)
    rank2 = jnp.sum(pick2 * before, axis=-1, keepdims=True)
    base_ref[0:1, :] = base_ref[0:1, :] + jnp.sum(picks, axis=0, keepdims=True)
    counts_ref[...] = base_ref[...]
    route = jnp.where(lane == 0, i1 - ROUTE_FINE_OFF,
                      jnp.where(lane == 1, i2 - ROUTE_FINE_OFF,
                                jnp.where(lane == 2, gate1,
                                          jnp.where(lane == 3, gate2,
                                                    jnp.where(lane == 4, rank1, jnp.where(lane == 5, rank2, 0.0))))))
    route_ref[...] = route


N_ROUTER_OUTS = 3


def _outproj_router_kernel(n_tiles, n_aliased, attn_ref, rwkv_ref, x_ref, wo_ref, g_ref, b_ref, wr_ref, br_ref,
                           base0_ref, *rest):
    outs = rest[n_aliased:n_aliased + N_ROUTER_OUTS]
    counts_ref, mix_ref, base_ref = rest[-3:]
    i = pl.program_id(0)
    finish = lambda mix: _norm_and_route(mix, x_ref, g_ref, b_ref, wr_ref, br_ref, *outs, counts_ref, base_ref)

    @pl.when(i == 0)
    def _():
        base_ref[...] = base0_ref[...]
        mix_ref[...] = _project_mix(attn_ref, rwkv_ref, wo_ref)

    @pl.when((i >= 1) & (i < n_tiles))
    def _():
        finish(mix_ref[...])
        mix_ref[...] = _project_mix(attn_ref, rwkv_ref, wo_ref)

    @pl.when(i == n_tiles)
    def _():
        finish(mix_ref[...])

    @pl.when(i > n_tiles)
    def _():
        for out_ref in outs:
            out_ref[...] = jnp.zeros_like(out_ref)


def _outproj_router(attn, rwkv, x, wo_bf16, ln_g, ln_b, w_route, b_route, counts_before, tm, n_total, row_block,
                    into, name):
    m = x.shape[0]
    n_tiles = m // tm
    const = lambda shape: pl.BlockSpec(shape, lambda i: (0, 0))
    ahead = lambda width: pl.BlockSpec((tm, width), lambda i: (jnp.minimum(i, n_tiles - 1), 0))
    behind = lambda width: pl.BlockSpec((tm, width), lambda i: (jnp.clip(i - 1, 0, n_tiles - 1), 0))
    in_specs = [ahead(D_ATTN), ahead(D_RWKV), behind(D_MODEL),
                const((D_MODEL, D_MODEL)), const((1, D_MODEL)), const((1, D_MODEL)),
                const((D_MODEL, LANES)), const((1, LANES)), const((SUBLANES, LANES))]
    args = [attn, rwkv, x, wo_bf16, ln_g, ln_b, w_route, b_route, counts_before]
    aliases, n_aliased, fill_steps = {}, 0, pl.cdiv(n_total - m, tm)
    if into is not None:
        n_aliased, fill_steps = N_ROUTER_OUTS, 0
        in_specs += [pl.BlockSpec(memory_space=pl.ANY)] * N_ROUTER_OUTS
        aliases = {len(args) + k: k for k in range(N_ROUTER_OUTS)}
        args += list(into)
    out_rows = lambda width: pl.BlockSpec((tm, width), lambda i: (jnp.maximum(i - 1, 0) + row_block, 0))
    return pl.pallas_call(
        functools.partial(_outproj_router_kernel, n_tiles, n_aliased),
        out_shape=(jax.ShapeDtypeStruct((n_total, D_MODEL), F32),
                   jax.ShapeDtypeStruct((n_total, D_MODEL // 2), jnp.uint32),
                   jax.ShapeDtypeStruct((n_total, LANES), F32),
                   jax.ShapeDtypeStruct((SUBLANES, LANES), F32)),
        grid=(n_tiles + 1 + fill_steps,),
        in_specs=in_specs,
        out_specs=(out_rows(D_MODEL), out_rows(D_MODEL // 2), out_rows(LANES), const((SUBLANES, LANES))),
        scratch_shapes=[pltpu.VMEM((tm, D_MODEL), F32), pltpu.VMEM((SUBLANES, LANES), F32)],
        input_output_aliases=aliases,
        compiler_params=_cparams(("arbitrary",)),
        name=name,
    )(*args)


DISPATCH_TILE = 128


def _dispatch_kernel(zoff_ref, dest_ref, x_ref, o_hbm, zbuf, ring, zsem, sem):
    i = pl.program_id(0)
    n_blocks = o_hbm.shape[0] // MOE_BLOCK
    n_used = zoff_ref[N_EXPERTS]

    def zero_fill(start_row):
        start_row = pl.multiple_of(start_row, MOE_BLOCK)
        return pltpu.make_async_copy(zbuf, o_hbm.at[pl.ds(start_row, MOE_BLOCK)], zsem)

    def zero_fills(action):
        for e in range(N_EXPERTS):
            @pl.when(zoff_ref[e] >= 0)
            def _():
                action(zero_fill(zoff_ref[e]))
        for b in range(n_blocks):
            @pl.when(b >= n_used)
            def _():
                action(zero_fill(b * MOE_BLOCK))

    @pl.when(i == 0)
    def _():
        zbuf[...] = jnp.zeros_like(zbuf)
        zero_fills(lambda copy: copy.start())
        zero_fills(lambda copy: copy.wait())

    cur = lax.rem(i, 2)

    def wait_rows(slot):
        for k in range(2):
            pltpu.make_async_copy(ring.at[slot], o_hbm.at[pl.ds(0, DISPATCH_TILE)], sem.at[slot]).wait()

    @pl.when(i >= 2)
    def _():
        wait_rows(cur)

    ring[cur] = x_ref[...]
    for t in range(DISPATCH_TILE):
        for k in range(2):
            pltpu.make_async_copy(ring.at[cur, pl.ds(t, 1)], o_hbm.at[pl.ds(dest_ref[0, 0, 2 * t + k], 1)],
                                  sem.at[cur]).start()

    @pl.when(i == pl.num_programs(0) - 1)
    def _():
        wait_rows(cur)

        @pl.when(i >= 1)
        def _():
            wait_rows(1 - cur)


def _dispatch(zero_offsets, dest, x_packed, n_blocks):
    n_tokens, width = x_packed.shape
    grid_spec = pltpu.PrefetchScalarGridSpec(
        num_scalar_prefetch=1,
        grid=(n_tokens // DISPATCH_TILE,),
        in_specs=[pl.BlockSpec((1, 1, 2 * DISPATCH_TILE), lambda i, z: (i, 0, 0), memory_space=pltpu.SMEM),
                  pl.BlockSpec((DISPATCH_TILE, width), lambda i, z: (i, 0))],
        out_specs=pl.BlockSpec(memory_space=pl.ANY),
        scratch_shapes=[pltpu.VMEM((MOE_BLOCK, width), x_packed.dtype),
                        pltpu.VMEM((2, DISPATCH_TILE, width), x_packed.dtype),
                        pltpu.SemaphoreType.DMA, pltpu.SemaphoreType.DMA((2,))],
    )
    return pl.pallas_call(
        _dispatch_kernel,
        out_shape=jax.ShapeDtypeStruct((n_blocks * MOE_BLOCK, width), x_packed.dtype),
        grid_spec=grid_spec,
        compiler_params=_cparams(("arbitrary",)),
        name="moe_dispatch",
    )(zero_offsets, dest.reshape(-1, 1, 2 * DISPATCH_TILE), x_packed)


def _expert_kernel(be_ref, nb_ref, x_ref, wg_hbm, wu_hbm, wd_hbm, o_ref, wg_buf, wu_buf, wd_buf, slot_ref, sem):
    blk = pl.program_id(0)
    n_used = nb_ref[0]
    expert = be_ref[blk]
    is_first = (blk == 0) | (be_ref[jnp.maximum(blk - 1, 0)] != expert)

    def fetch(e, slot):
        return [pltpu.make_async_copy(hbm.at[e], buf.at[slot], sem.at[slot, i])
                for i, (hbm, buf) in enumerate(((wg_hbm, wg_buf), (wu_hbm, wu_buf), (wd_hbm, wd_buf)))]

    @pl.when((blk < n_used) & is_first)
    def _():
        @pl.when(blk == 0)
        def _():
            slot_ref[0] = 1
            for copy in fetch(expert, 0):
                copy.start()

        slot = 1 - slot_ref[0]
        slot_ref[0] = slot
        for copy in fetch(expert, slot):
            copy.wait()
        nxt = lax.while_loop(lambda j: (j < n_used) & (be_ref[jnp.minimum(j, n_used - 1)] == expert),
                             lambda j: j + 1, blk + 1)

        @pl.when(nxt < n_used)
        def _():
            for copy in fetch(be_ref[jnp.minimum(nxt, n_used - 1)], 1 - slot):
                copy.start()

    @pl.when(blk < n_used)
    def _():
        slot = slot_ref[0]
        half = D_MODEL // 2
        x_head, x_tail = _unpack_bf16_halves(x_ref[...])
        proj = lambda w_buf: (_dot(x_head, w_buf[slot, 0:half, :].astype(BF16))
                              + _dot(x_tail, w_buf[slot, half:D_MODEL, :].astype(BF16)))
        gate = proj(wg_buf)
        up = proj(wu_buf)
        h = gate * _sigmoid(gate) * up
        o_ref[...] = _dot(h.astype(BF16), wd_buf[slot].astype(BF16))

    @pl.when(blk >= n_used)
    def _():
        o_ref[...] = jnp.zeros_like(o_ref)


def _expert_mlp(block_expert, n_used, x_sorted, w_gate, w_up, w_down, n_blocks):
    grid_spec = pltpu.PrefetchScalarGridSpec(
        num_scalar_prefetch=2,
        grid=(n_blocks,),
        in_specs=[pl.BlockSpec((MOE_BLOCK, D_MODEL // 2), lambda b, be, nb: (jnp.minimum(b, nb[0] - 1), 0)),
                  pl.BlockSpec(memory_space=pl.ANY), pl.BlockSpec(memory_space=pl.ANY),
                  pl.BlockSpec(memory_space=pl.ANY)],
        out_specs=pl.BlockSpec((MOE_BLOCK, D_MODEL), lambda b, be, nb: (b, 0)),
        scratch_shapes=[pltpu.VMEM((2, D_MODEL, D_EXPERT), F32), pltpu.VMEM((2, D_MODEL, D_EXPERT), F32),
                        pltpu.VMEM((2, D_EXPERT, D_MODEL), F32), pltpu.SMEM((1,), jnp.int32),
                        pltpu.SemaphoreType.DMA((2, 3))],
    )
    return pl.pallas_call(
        _expert_kernel,
        out_shape=jax.ShapeDtypeStruct((n_blocks * MOE_BLOCK, D_MODEL), F32),
        grid_spec=grid_spec,
        compiler_params=_cparams(("arbitrary",)),
        name="expert_mlp",
    )(block_expert, n_used, x_sorted, w_gate, w_up, w_down)


COMBINE_TILE = 128


def _combine_kernel(dest_ref, dest_next_ref, y_hbm, x1_ref, route_ref, g_ref, b_ref, o_ref, ybuf, sem):
    i = pl.program_id(0)
    cur = lax.rem(i, 2)
    n_rows = 2 * COMBINE_TILE

    def gather(table_ref, buf):
        for slot in range(n_rows):
            pltpu.make_async_copy(y_hbm.at[pl.ds(table_ref[0, 0, slot], 1)], ybuf.at[buf, pl.ds(slot, 1)],
                                  sem.at[buf]).start()

    def wait_gather(buf):
        pltpu.make_async_copy(y_hbm.at[pl.ds(0, n_rows)], ybuf.at[buf], sem.at[buf]).wait()

    @pl.when(i == 0)
    def _():
        gather(dest_ref, 0)

    gather(dest_next_ref, 1 - cur)
    wait_gather(cur)
    route = route_ref[...]
    yb = ybuf[cur]
    moe = route[:, 2:3] * yb[0:COMBINE_TILE, :] + route[:, 3:4] * yb[COMBINE_TILE:n_rows, :]
    o_ref[...] = _layer_norm(ALPHA * x1_ref[...] + moe, g_ref[...], b_ref[...])

    @pl.when(i == pl.num_programs(0) - 1)
    def _():
        wait_gather(1 - cur)


def _combine(dest, y_slots, x1_all, route_all, m, row_block, ln_g, ln_b, name):
    tm = COMBINE_TILE
    return pl.pallas_call(
        _combine_kernel,
        out_shape=jax.ShapeDtypeStruct((m, D_MODEL), F32),
        grid=(m // tm,),
        in_specs=[pl.BlockSpec((1, 1, 2 * tm), lambda i: (i, 0, 0), memory_space=pltpu.SMEM),
                  pl.BlockSpec((1, 1, 2 * tm), lambda i: (i + 1, 0, 0), memory_space=pltpu.SMEM),
                  pl.BlockSpec(memory_space=pl.ANY),
                  pl.BlockSpec((tm, D_MODEL), lambda i: (i + row_block, 0)),
                  pl.BlockSpec((tm, LANES), lambda i: (i + row_block, 0)),
                  pl.BlockSpec((1, D_MODEL), lambda i: (0, 0)),
                  pl.BlockSpec((1, D_MODEL), lambda i: (0, 0))],
        out_specs=pl.BlockSpec((tm, D_MODEL), lambda i: (i, 0)),
        scratch_shapes=[pltpu.VMEM((2, 2 * tm, D_MODEL), F32), pltpu.SemaphoreType.DMA((2,))],
        compiler_params=_cparams(("arbitrary",)),
        name=name,
    )(dest, dest, y_slots, x1_all, route_all, ln_g, ln_b)


def _dispatch_plan(route_all, counts_lanes, n_blocks):
    flat_e = route_all[:, 0:2].astype(jnp.int32).reshape(-1)
    rank = route_all[:, 4:6].astype(jnp.int32).reshape(-1)
    counts = counts_lanes[0, ROUTE_FINE_OFF:ROUTE_FINE_OFF + N_EXPERTS].astype(jnp.int32)
    padded = (counts + MOE_BLOCK - 1) // MOE_BLOCK * MOE_BLOCK
    pend = jnp.cumsum(padded)
    pstart = pend - padded
    dest = pstart[flat_e] + rank
    zero_offsets = jnp.where(counts > 0, pend - MOE_BLOCK, -1).astype(jnp.int32)
    n_used = (pend[-1] // MOE_BLOCK).astype(jnp.int32)
    block_start = jnp.minimum(jnp.arange(n_blocks, dtype=jnp.int32), n_used - 1) * MOE_BLOCK
    block_e = jnp.minimum(jnp.sum((pend[None, :] <= block_start[:, None]).astype(jnp.int32), axis=1), N_EXPERTS - 1)
    return dest, jnp.concatenate([zero_offsets, n_used.reshape(1)]), block_e, n_used.reshape(1)


def kernel(x_prompt, x_sample, cache_k_win, cache_v_win, state_wkv, state_shift, w_in, attn_sinks, shift_mu, w0,
           w_decay_up, a0, w_a_up, w_g_up, k_k, k_a, r_k, gn_g, gn_b, w_out, ln1_g, ln1_b, w_coarse, b_coarse,
           w_fine, b_fine, w_exp_gate, w_exp_up, w_exp_down, ln2_g, ln2_b):
    xp = x_prompt[0]
    xs = x_sample[:, 0]
    row = lambda a: a.reshape(1, -1)

    w_in_t = jnp.swapaxes(w_in[0], 0, 1)
    prm = dict(mu=row(shift_mu[0, :D_RKV]), mu_tail=row(shift_mu[0, D_RKV:]), w_tail=w_in_t[D_MAIN:].astype(BF16),
               w0=row(w0[0]), a0=row(a0[0]), k_k=row(k_k[0]), k_a=row(k_a[0]),
               r_k=row(r_k[0]), gn_g=row(gn_g[0]), gn_b=row(gn_b[0]),
               wd=w_decay_up[0], wa=w_a_up[0], wg=w_g_up[0])
    sinks = attn_sinks[0]
    wo_bf16 = w_out[0].astype(BF16)
    w_route = jnp.pad(jnp.concatenate([w_coarse[0], w_fine[0]], axis=1), ((0, 0), (0, LANES - N_GROUPS - N_EXPERTS)))
    b_route = jnp.pad(jnp.concatenate([b_coarse[0], b_fine[0]]), (0, LANES - N_GROUPS - N_EXPERTS)).reshape(1, LANES)

    hp = _matmul(xp, w_in_t, D_MAIN, MAIN_TM, MAIN_TN, "in_proj_prompt")
    hs = _matmul(xs, w_in_t, D_MAIN, DEC_BATCH, MAIN_TN, "in_proj_sample")

    attn_p = _prompt_attention(hp, sinks)
    rwkv_p, state_p, tail_p = _prompt_rwkv(hp, xp, prm)

    q_s = hs[:, :D_ATTN].reshape(DEC_BATCH, N_Q_HEADS, HEAD_DIM)
    k_s = hs[:, D_ATTN:D_ATTN + D_KV].reshape(DEC_BATCH, 1, D_KV)
    v_s = hs[:, D_ATTN + D_KV:D_QKV].reshape(DEC_BATCH, 1, D_KV)
    window_t = lambda c: jnp.transpose(c, (0, 2, 3, 1)).reshape(DEC_BATCH, D_KV, WINDOW)
    attn_s, kwin_s, vwin_s = _sample_attention(
        q_s, k_s, v_s, window_t(cache_k_win[0]), window_t(cache_v_win[0]), sinks.reshape(N_Q_HEADS, 1))
    r_s, k2_s, vv_s, g_s, tail_s, r_t, w_t, k_t, v_t, a_t, b_t = _sample_prep(hs, xs, state_shift[0], prm)
    y_t, state_s = _sample_step(jnp.transpose(state_wkv[0], (1, 2, 3, 0)), r_t, w_t, k_t, a_t, b_t, v_t)
    state_s = state_s.reshape(DEC_BATCH, N_RWKV_HEADS, HEAD_DIM, HEAD_DIM)
    rwkv_s = _sample_post(y_t, r_s, k2_s, vv_s, g_s, prm)

    n_tokens = SEQ + DEC_BATCH
    *outs_pr, counts_pr = _outproj_router(attn_p, rwkv_p, xp, wo_bf16, row(ln1_g[0]), row(ln1_b[0]), w_route,
                                          b_route, jnp.zeros((SUBLANES, LANES), F32), 256, n_tokens, 0, None,
                                          "outproj_router_prompt")
    x1_all, x1b_all, route_all, counts = _outproj_router(
        attn_s.reshape(DEC_BATCH, D_ATTN), rwkv_s, xs, wo_bf16, row(ln1_g[0]), row(ln1_b[0]), w_route, b_route,
        counts_pr, DEC_BATCH, n_tokens, SEQ // DEC_BATCH, outs_pr, "outproj_router_sample")

    n_assign = 2 * n_tokens
    n_blocks = -(-(n_assign + N_EXPERTS * (MOE_BLOCK - 1)) // MOE_BLOCK)
    dest, zero_offsets, block_e, n_used = _dispatch_plan(route_all, counts, n_blocks)
    x_sorted = _dispatch(zero_offsets, dest, x1b_all, n_blocks)
    y_slots = _expert_mlp(block_e, n_used, x_sorted, w_exp_gate[0], w_exp_up[0], w_exp_down[0], n_blocks)

    def dest_tiles(d):
        d = d.reshape(-1, COMBINE_TILE, 2)
        d = jnp.concatenate([d[:, :, 0], d[:, :, 1]], axis=1)
        return jnp.pad(d, ((0, 1), (0, 0))).reshape(-1, 1, 2 * COMBINE_TILE)

    y_p = _combine(dest_tiles(dest[:2 * SEQ]), y_slots, x1_all, route_all, SEQ, 0, row(ln2_g[0]), row(ln2_b[0]),
                   "combine_prompt")
    y_s = _combine(dest_tiles(dest[2 * SEQ:]), y_slots, x1_all, route_all, DEC_BATCH, SEQ // COMBINE_TILE,
                   row(ln2_g[0]), row(ln2_b[0]), "combine_sample")

    kv4 = lambda a: a.reshape(a.shape[0], N_KV_HEADS, HEAD_DIM)
    k_win_p = kv4(hp[SEQ - WINDOW:, D_ATTN:D_ATTN + D_KV])[None, None]
    v_win_p = kv4(hp[SEQ - WINDOW:, D_ATTN + D_KV:D_QKV])[None, None]
    sp = state_p.reshape(N_PAIRS, HEADS_PER_TILE, HEAD_DIM, HEADS_PER_TILE, HEAD_DIM)
    wkv_p = jnp.stack([sp[:, i, :, i, :] for i in range(HEADS_PER_TILE)], axis=1)
    wkv_p = wkv_p.reshape(N_RWKV_HEADS, HEAD_DIM, HEAD_DIM).transpose(0, 2, 1)[None, None]
    shift_p = jnp.concatenate([hp[SEQ - 1:SEQ, D_QKV:], tail_p[0:1]], axis=1)[None]
    shift_s = jnp.concatenate([hs[:, D_QKV:], tail_s], axis=1)[None]
    return (y_p[None], y_s[:, None, :], k_win_p, v_win_p, wkv_p, shift_p,
            kwin_s.reshape(1, DEC_BATCH, WINDOW, N_KV_HEADS, HEAD_DIM),
            vwin_s.reshape(1, DEC_BATCH, WINDOW, N_KV_HEADS, HEAD_DIM),
            state_s[None], shift_s)
```

```python
import functools
import math

import jax
import jax.numpy as jnp
from jax import lax
from jax.experimental import pallas as pl
from jax.experimental.pallas import tpu as pltpu

F32 = jnp.float32
BF16 = jnp.bfloat16

D_MODEL = 2048
SEQ = 8192
DEC_BATCH = 128
HEAD_DIM = 64
D_ATTN = 1024
D_RWKV = 1024
N_Q_HEADS = 16
N_KV_HEADS = 4
Q_PER_KV = 4
D_KV = 256
WINDOW = 128
ATTN_SCALE = HEAD_DIM ** -0.5
N_RWKV_HEADS = 16
W_LORA = 64
A_LORA = 64
G_LORA = 160
D_SHIFT = 3 * D_RWKV + W_LORA + A_LORA + G_LORA
D_QKV = D_ATTN + 2 * D_KV
N_GROUPS = 4
EXPERTS_PER_GROUP = 8
N_EXPERTS = 32
D_EXPERT = 512
ALPHA = 2.0 ** 0.25
LN_EPS = 1e-5
GN_EPS = 64e-5

SUBLANES = 8
LANES = 128
VMEM_LIMIT = 52 * 1024 * 1024

D_RKV = 3 * D_RWKV
D_TAIL = W_LORA + A_LORA + G_LORA
D_MAIN = D_QKV + D_RKV
MAIN_TN = 1536
MAIN_TM = 512

CHUNK = 64
HEADS_PER_TILE = LANES // HEAD_DIM
N_PAIRS = N_RWKV_HEADS // HEADS_PER_TILE
SOLVE_LEVELS = int(math.log2(CHUNK))
PAIR_GROUP = 8

MOE_BLOCK = 256
ROUTE_FINE_OFF = N_GROUPS

NN = (((1,), (0,)), ((), ()))
NT = (((1,), (1,)), ((), ()))


def _dot(a, b, dims=NN):
    return lax.dot_general(a, b, dims, preferred_element_type=F32)


def _dot1(a, b, dims=NN):
    return _dot(a.astype(BF16), b.astype(BF16), dims)


def _split(x):
    hi = x.astype(BF16)
    lo = (x - hi.astype(F32)).astype(BF16)
    return hi, lo


def _dot3(a, b, dims=NN):
    ah, al = _split(a)
    bh, bl = _split(b)
    return _dot(ah, bh, dims) + (_dot(ah, bl, dims) + _dot(al, bh, dims))


def _dot_exact_lhs(a_bf16, b, dims=NN):
    bh, bl = _split(b)
    return _dot(a_bf16, bh, dims) + _dot(a_bf16, bl, dims)


def _dot_exact_rhs(a, b_bf16, dims=NN):
    ah, al = _split(a)
    return _dot(ah, b_bf16, dims) + _dot(al, b_bf16, dims)


def _div_pow2(x, d):
    return lax.shift_right_logical(x, jnp.int32(int(math.log2(d))))


def _mod_pow2(x, d):
    return lax.bitwise_and(x, jnp.int32(d - 1))


def _pack_bf16_halves(x_bf16):
    n = x_bf16.shape[1] // 2
    bits = lax.bitcast_convert_type(x_bf16.astype(F32), jnp.uint32)
    return lax.bitwise_or(bits[:, 0:n], lax.shift_right_logical(bits[:, n:2 * n], jnp.uint32(16)))


def _unpack_bf16_halves(packed):
    hi = lax.bitcast_convert_type(lax.bitwise_and(packed, jnp.uint32(0xFFFF0000)), F32)
    lo = lax.bitcast_convert_type(lax.shift_left(packed, jnp.uint32(16)), F32)
    return hi.astype(BF16), lo.astype(BF16)


def _sigmoid(x):
    return 1.0 / (1.0 + jnp.exp(-x))


def _softplus(x):
    return jnp.maximum(x, 0.0) + jnp.log(1.0 + jnp.exp(-jnp.abs(x)))


def _layer_norm(z, g, b):
    mu = jnp.mean(z, axis=-1, keepdims=True)
    d = z - mu
    var = jnp.mean(d * d, axis=-1, keepdims=True)
    return d * lax.rsqrt(var + LN_EPS) * g + b


def _cparams(sem):
    return pltpu.CompilerParams(dimension_semantics=sem, vmem_limit_bytes=VMEM_LIMIT)


def _matmul_kernel(x_ref, wt_ref, o_ref):
    o_ref[...] = _dot(x_ref[...].astype(BF16), wt_ref[...].astype(BF16), NT)


def _matmul(x, w_t, n_out, tm, tn, name):
    m, k = x.shape
    tm = min(tm, m)
    return pl.pallas_call(
        _matmul_kernel,
        out_shape=jax.ShapeDtypeStruct((m, n_out), F32),
        grid=(n_out // tn, m // tm),
        in_specs=[pl.BlockSpec((tm, k), lambda j, i: (i, 0)),
                  pl.BlockSpec((tn, k), lambda j, i: (j, 0))],
        out_specs=pl.BlockSpec((tm, tn), lambda j, i: (i, j)),
        compiler_params=_cparams(("arbitrary", "arbitrary")),
        name=name,
    )(x, w_t)


def _band_bias():
    qi = jnp.arange(Q_PER_KV * WINDOW)[:, None] % WINDOW
    kj = jnp.arange(2 * WINDOW)[None, :]
    diff = qi + WINDOW - kj
    band = (diff >= 0) & (diff <= WINDOW)
    keep = jnp.stack([band & (kj >= WINDOW), band])
    return jnp.where(keep, 0.0, -jnp.inf).astype(F32)


def _prompt_attn_kernel(q_ref, kvp_ref, kvc_ref, bias_ref, sink_ref, o_ref):
    q = q_ref[...]
    kv_prev = kvp_ref[...]
    kv_cur = kvc_ref[...]
    bias = bias_ref[0]
    row_head = _div_pow2(lax.broadcasted_iota(jnp.int32, (Q_PER_KV * WINDOW, 1), 0), WINDOW)
    groups = range(N_KV_HEADS)
    kv_cols = lambda off, g: jnp.concatenate([kv_prev[:, off + g * HEAD_DIM:off + (g + 1) * HEAD_DIM],
                                              kv_cur[:, off + g * HEAD_DIM:off + (g + 1) * HEAD_DIM]],
                                             axis=0).astype(BF16)
    q_rows = lambda g: jnp.concatenate(
        [q[:, (g * Q_PER_KV + h) * HEAD_DIM:(g * Q_PER_KV + h + 1) * HEAD_DIM] for h in range(Q_PER_KV)],
        axis=0).astype(BF16)
    s = [_dot(q_rows(g), kv_cols(0, g), NT) * ATTN_SCALE + bias for g in groups]
    sink = []
    for g in groups:
        col = jnp.zeros((Q_PER_KV * WINDOW, 1), F32)
        for h in range(Q_PER_KV):
            col = jnp.where(row_head == h, sink_ref[g * Q_PER_KV + h], col)
        sink.append(col)
    m = [jnp.maximum(jnp.max(s[g], axis=-1, keepdims=True), sink[g]) for g in groups]
    p = [jnp.exp(s[g] - m[g]) for g in groups]
    denom = [jnp.sum(p[g], axis=-1, keepdims=True) + jnp.exp(sink[g] - m[g]) for g in groups]
    o = [_dot((p[g] / denom[g]).astype(BF16), kv_cols(D_KV, g)) for g in groups]
    o_ref[...] = jnp.concatenate([o[g][h * WINDOW:(h + 1) * WINDOW, :] for g in groups for h in range(Q_PER_KV)],
                                 axis=1)


def _prompt_attention(h_attn, sinks):
    nb = SEQ // WINDOW
    return pl.pallas_call(
        _prompt_attn_kernel,
        out_shape=jax.ShapeDtypeStruct((SEQ, D_ATTN), F32),
        grid=(nb,),
        in_specs=[pl.BlockSpec((WINDOW, D_ATTN), lambda i: (i, 0)),
                  pl.BlockSpec((WINDOW, 2 * D_KV), lambda i: (jnp.maximum(i - 1, 0), 2)),
                  pl.BlockSpec((WINDOW, 2 * D_KV), lambda i: (i, 2)),
                  pl.BlockSpec((1, Q_PER_KV * WINDOW, 2 * WINDOW), lambda i: (jnp.minimum(i, 1), 0, 0)),
                  pl.BlockSpec(memory_space=pltpu.SMEM)],
        out_specs=pl.BlockSpec((WINDOW, D_ATTN), lambda i: (i, 0)),
        compiler_params=_cparams(("arbitrary",)),
        name="prompt_attention",
    )(h_attn, h_attn, h_attn, _band_bias(), sinks)


SAMPLE_ATTN_TILE = 8


def _sample_attn_kernel(q_ref, knew_ref, vnew_ref, ck_ref, cv_ref, sink_ref, o_ref, kwin_ref, vwin_ref):
    lane = lax.broadcasted_iota(jnp.int32, (N_Q_HEADS, D_KV), 1)
    head = lax.broadcasted_iota(jnp.int32, (N_Q_HEADS, D_KV), 0)
    group_mask = _div_pow2(lane, HEAD_DIM) == _div_pow2(head, Q_PER_KV)
    sink = sink_ref[...]
    row = lax.broadcasted_iota(jnp.int32, (WINDOW, D_KV), 0)
    seqs = range(SAMPLE_ATTN_TILE)
    qbd = [jnp.where(group_mask, jnp.concatenate([q_ref[b]] * N_KV_HEADS, axis=1), 0.0).astype(BF16) for b in seqs]
    s = [_dot1(qbd[b], ck_ref[b]) * ATTN_SCALE for b in seqs]
    s_new = [jnp.sum(qbd[b].astype(F32) * knew_ref[b].astype(BF16).astype(F32), axis=-1, keepdims=True) * ATTN_SCALE
             for b in seqs]
    m = [jnp.maximum(jnp.maximum(jnp.max(s[b], axis=-1, keepdims=True), s_new[b]), sink) for b in seqs]
    p = [jnp.exp(s[b] - m[b]) for b in seqs]
    p_new = [jnp.exp(s_new[b] - m[b]) for b in seqs]
    denom = [jnp.sum(p[b], axis=-1, keepdims=True) + p_new[b] + jnp.exp(sink - m[b]) for b in seqs]
    for b in seqs:
        kb = ck_ref[b].T
        vb = cv_ref[b].T
        kn = knew_ref[b]
        vn = vnew_ref[b]
        o_full = (_dot1(p[b] / denom[b], vb)
                  + (p_new[b] / denom[b]).astype(BF16).astype(F32) * vn.astype(BF16).astype(F32))
        o_full = jnp.where(group_mask, o_full, 0.0)
        o = o_full[:, 0:HEAD_DIM]
        for g in range(1, N_KV_HEADS):
            o = o + o_full[:, g * HEAD_DIM:(g + 1) * HEAD_DIM]
        o_ref[b] = o
        kwin_ref[b] = jnp.where(row == WINDOW - 1, kn, pltpu.roll(kb, WINDOW - 1, axis=0))
        vwin_ref[b] = jnp.where(row == WINDOW - 1, vn, pltpu.roll(vb, WINDOW - 1, axis=0))


def _sample_attention(q, k_new, v_new, cache_k_t, cache_v_t, sinks):
    bt = SAMPLE_ATTN_TILE
    win_spec = pl.BlockSpec((bt, WINDOW, D_KV), lambda i: (i, 0, 0))
    win_t_spec = pl.BlockSpec((bt, D_KV, WINDOW), lambda i: (i, 0, 0))
    new_spec = pl.BlockSpec((bt, 1, D_KV), lambda i: (i, 0, 0))
    return pl.pallas_call(
        _sample_attn_kernel,
        out_shape=(jax.ShapeDtypeStruct((DEC_BATCH, N_Q_HEADS, HEAD_DIM), F32),
                   jax.ShapeDtypeStruct((DEC_BATCH, WINDOW, D_KV), F32),
                   jax.ShapeDtypeStruct((DEC_BATCH, WINDOW, D_KV), F32)),
        grid=(DEC_BATCH // bt,),
        in_specs=[pl.BlockSpec((bt, N_Q_HEADS, HEAD_DIM), lambda i: (i, 0, 0)),
                  new_spec, new_spec, win_t_spec, win_t_spec,
                  pl.BlockSpec((N_Q_HEADS, 1), lambda i: (0, 0))],
        out_specs=(pl.BlockSpec((bt, N_Q_HEADS, HEAD_DIM), lambda i: (i, 0, 0)), win_spec, win_spec),
        compiler_params=_cparams(("arbitrary",)),
        name="sample_attention",
    )(q, k_new, v_new, cache_k_t, cache_v_t, sinks)


def _head_ones():
    r = _div_pow2(lax.broadcasted_iota(jnp.int32, (LANES, LANES), 0), HEAD_DIM)
    c = _div_pow2(lax.broadcasted_iota(jnp.int32, (LANES, LANES), 1), HEAD_DIM)
    return jnp.where(r == c, 1.0, 0.0).astype(BF16)


def _head_sum(x, ones):
    parts = [_dot_exact_rhs(x[:, p * LANES:(p + 1) * LANES], ones) for p in range(x.shape[1] // LANES)]
    return jnp.concatenate(parts, axis=1)


def _token_mix(feat, shifted, mu):
    return feat + (shifted - feat) * mu


def _rwkv_prep(mixed, mixed_tail, w0, a0, k_k, k_a, wd, wa, wg, ones):
    r = mixed[:, 0:D_RWKV]
    k = mixed[:, D_RWKV:2 * D_RWKV]
    v = mixed[:, 2 * D_RWKV:3 * D_RWKV]
    xw = mixed_tail[:, 0:W_LORA]
    xa = mixed_tail[:, W_LORA:W_LORA + A_LORA]
    xg = mixed_tail[:, W_LORA + A_LORA:D_TAIL]
    w_log = -_softplus(-(w0 + _dot1(jnp.tanh(xw), wd))) - 0.5
    log_decay = -jnp.exp(w_log)
    a = _sigmoid(a0 + _dot1(xa, wa))
    g = _dot1(_sigmoid(xg), wg)
    kk = k * k_k
    kk = kk * lax.rsqrt(jnp.maximum(_head_sum(kk * kk, ones), 1e-24))
    k2 = k * (1.0 + (a - 1.0) * k_a)
    return r, log_decay, k2, v, -kk, kk * a, g


def _rwkv_post(y, r, k2, v, g, r_k, gn_g, gn_b, ones):
    inv_n = 1.0 / HEAD_DIM
    mu = _head_sum(y, ones) * inv_n
    d = y - mu
    var = _head_sum(d * d, ones) * inv_n
    yn = d * lax.rsqrt(var + GN_EPS) * gn_g + gn_b
    bonus = _head_sum(r * k2 * r_k, ones) * v
    return (yn + bonus) * g


(OP_AABS, OP_RABS, OP_AN, OP_RN, OP_BN, OP_KN, OP_BH, OP_KH, OP_V) = range(9)
N_OPS = 9


def _prompt_rwkv_kernel(f1_ref, f2_ref, x_ref, wt_ref, mu_ref, mut_ref, w0_ref, a0_ref, kk_ref, ka_ref, rk_ref,
                        gng_ref, gnb_ref, wd_ref, wa_ref, wg_ref, out_ref, state_ref, tail_ref,
                        prev_ref, prevt_ref, s_ref, ops_ref, pc_ref, y_ref):
    c = pl.program_id(0)
    C = CHUNK

    @pl.when(c == 0)
    def _():
        prev_ref[...] = jnp.zeros_like(prev_ref)
        prevt_ref[...] = jnp.zeros_like(prevt_ref)
        s_ref[...] = jnp.zeros_like(s_ref)

    ones = _head_ones()
    row = lax.broadcasted_iota(jnp.int32, (C, 1), 0)

    def token_shift(feat, carry_ref):
        shifted = jnp.where(row == 0, carry_ref[0:1, :], pltpu.roll(feat, 1, axis=0))
        carry_ref[0:1, :] = feat[C - 1:C, :]
        return shifted

    feat = jnp.concatenate([f1_ref[...], f2_ref[...]], axis=1)
    tail = _dot1(x_ref[...], wt_ref[...], NT)
    mixed = _token_mix(feat, token_shift(feat, prev_ref), mu_ref[...])
    mixed_tail = _token_mix(tail, token_shift(tail, prevt_ref), mut_ref[...])
    tail_ref[...] = prevt_ref[...]
    r, ld, k2, v, av, bv, g = _rwkv_prep(mixed, mixed_tail, w0_ref[...], a0_ref[...], kk_ref[...], ka_ref[...],
                                         wd_ref[...], wa_ref[...], wg_ref[...], ones)

    ti = lax.broadcasted_iota(jnp.int32, (C, C), 0)
    tj = lax.broadcasted_iota(jnp.int32, (C, C), 1)
    tri_incl = jnp.where(tj <= ti, 1.0, 0.0).astype(BF16)
    cs = _dot_exact_lhs(tri_incl, ld)
    cs_ref = cs[C // 2 - 1:C // 2, :]
    cs_end = cs[C - 1:C, :]
    e_prev = jnp.exp(cs - ld)
    e_cur = jnp.exp(cs)
    n_prev = jnp.exp(cs - ld - cs_ref)
    n_cur = jnp.exp(cs - cs_ref)
    n_inv = jnp.exp(cs_ref - cs)
    e_tail = jnp.exp(cs_end - cs)
    ops = {OP_AABS: av * e_prev, OP_RABS: r * e_cur, OP_AN: av * n_prev, OP_RN: r * n_cur,
           OP_BN: bv * n_inv, OP_KN: k2 * n_inv, OP_BH: bv * e_tail, OP_KH: k2 * e_tail, OP_V: v}
    p_end = jnp.exp(cs_end)
    for p in range(N_PAIRS):
        sl = slice(p * LANES, (p + 1) * LANES)
        for idx, val in ops.items():
            ops_ref[p, idx] = val[:, sl]
        pc_ref[p] = jnp.broadcast_to(p_end[:, sl], (SUBLANES, LANES))

    lane1 = lax.broadcasted_iota(jnp.int32, (C, LANES), 1)
    head0 = lane1 < HEAD_DIM
    r2 = lax.broadcasted_iota(jnp.int32, (2 * C, 2 * C), 0)
    c2 = lax.broadcasted_iota(jnp.int32, (2 * C, 2 * C), 1)
    tq = _mod_pow2(r2, C)
    tk = _mod_pow2(c2, C)
    band = (tk < tq) | ((tk == tq) & (r2 >= C))
    blockdiag = _div_pow2(r2, HEAD_DIM) == _div_pow2(c2, HEAD_DIM)

    op = lambda p, idx: ops_ref[p, idx]
    zero_half = jnp.zeros((C, LANES), F32)
    for pairs in [range(g, g + PAIR_GROUP) for g in range(0, N_PAIRS, PAIR_GROUP)]:
        gy = {p: _dot1(jnp.concatenate([op(p, OP_AABS), op(p, OP_RABS)], axis=0), s_ref[p]) for p in pairs}

        am0, am1 = {}, {}
        for p in pairs:
            a_n, r_n = op(p, OP_AN), op(p, OP_RN)
            b0, k0 = jnp.where(head0, op(p, OP_BN), 0.0), jnp.where(head0, op(p, OP_KN), 0.0)
            b1, k1 = jnp.where(head0, 0.0, op(p, OP_BN)), jnp.where(head0, 0.0, op(p, OP_KN))
            am = _dot1(jnp.concatenate([a_n, r_n], axis=0), jnp.concatenate([k0, b0, b1, k1], axis=0), NT)
            am0[p] = jnp.where(band, am[:, 0:2 * C], 0.0)
            am1[p] = jnp.where(band, am[:, 2 * C:4 * C], 0.0)

        w0, w1 = {}, {}
        for p in pairs:
            top0, top1 = am0[p][0:C], am1[p][0:C]
            ak = jnp.concatenate([jnp.where(head0, top0, 0.0), jnp.where(head0, 0.0, top1)], axis=0)
            vv = op(p, OP_V)
            g0 = gy[p][0:C]
            m = jnp.concatenate([g0, g0], axis=0) + _dot1(ak, jnp.concatenate([vv, vv], axis=0))
            w0[p] = jnp.where(head0, m[0:C], top0)
            w1[p] = jnp.where(head0, top1, m[C:2 * C])

        for lvl in range(SOLVE_LEVELS):
            prod0 = {p: _dot1(w0[p], jnp.concatenate([zero_half, w0[p]], axis=0)) for p in pairs}
            prod1 = {p: _dot1(w1[p], jnp.concatenate([w1[p], zero_half], axis=0)) for p in pairs}
            w0 = {p: jnp.where(head0, w0[p] + prod0[p], prod0[p]) for p in pairs}
            w1 = {p: jnp.where(head0, prod1[p], w1[p] + prod1[p]) for p in pairs}
        u = {p: jnp.where(head0, w0[p], w1[p]) for p in pairs}

        for p in pairs:
            vv = op(p, OP_V)
            y_lhs = jnp.concatenate([am0[p][C:2 * C], am1[p][C:2 * C]], axis=1)
            y_rhs = jnp.concatenate([jnp.where(head0, vv, 0.0), jnp.where(head0, u[p], 0.0),
                                     jnp.where(head0, 0.0, u[p]), jnp.where(head0, 0.0, vv)], axis=0)
            y_ref[p] = gy[p][C:2 * C] + _dot1(y_lhs, y_rhs)

        for p in pairs:
            decay_rows = jnp.broadcast_to(pc_ref[p][0:1, :], (LANES, LANES)).T
            upd_lhs = jnp.concatenate([op(p, OP_BH), op(p, OP_KH)], axis=0).T
            upd_rhs = jnp.concatenate([u[p], op(p, OP_V)], axis=0)
            s_ref[p] = s_ref[p] * decay_rows + jnp.where(blockdiag, _dot1(upd_lhs, upd_rhs), 0.0)

    y = jnp.concatenate([y_ref[p] for p in range(N_PAIRS)], axis=1)
    out_ref[...] = _rwkv_post(y, r, k2, v, g, rk_ref[...], gng_ref[...], gnb_ref[...], ones)

    @pl.when(c == pl.num_programs(0) - 1)
    def _():
        state_ref[...] = s_ref[...]


def _prompt_rwkv(h_main, x, prm):
    n_chunks = SEQ // CHUNK
    half = D_RKV // 2
    assert D_QKV == half
    vec = pl.BlockSpec((1, D_RWKV), lambda c: (0, 0))
    full = lambda a: pl.BlockSpec(a.shape, lambda c: (0,) * a.ndim)
    return pl.pallas_call(
        _prompt_rwkv_kernel,
        out_shape=(jax.ShapeDtypeStruct((SEQ, D_RWKV), F32),
                   jax.ShapeDtypeStruct((N_PAIRS, LANES, LANES), F32),
                   jax.ShapeDtypeStruct((SUBLANES, D_TAIL), F32)),
        grid=(n_chunks,),
        in_specs=[pl.BlockSpec((CHUNK, half), lambda c: (c, 1)),
                  pl.BlockSpec((CHUNK, half), lambda c: (c, 2)),
                  pl.BlockSpec((CHUNK, D_MODEL), lambda c: (c, 0)),
                  full(prm["w_tail"]), full(prm["mu"]), full(prm["mu_tail"]),
                  vec, vec, vec, vec, vec, vec, vec,
                  full(prm["wd"]), full(prm["wa"]), full(prm["wg"])],
        out_specs=(pl.BlockSpec((CHUNK, D_RWKV), lambda c: (c, 0)),
                   pl.BlockSpec((N_PAIRS, LANES, LANES), lambda c: (0, 0, 0)),
                   pl.BlockSpec((SUBLANES, D_TAIL), lambda c: (0, 0))),
        scratch_shapes=[pltpu.VMEM((SUBLANES, D_RKV), F32),
                        pltpu.VMEM((SUBLANES, D_TAIL), F32),
                        pltpu.VMEM((N_PAIRS, LANES, LANES), F32),
                        pltpu.VMEM((N_PAIRS, N_OPS, CHUNK, LANES), F32),
                        pltpu.VMEM((N_PAIRS, SUBLANES, LANES), F32),
                        pltpu.VMEM((N_PAIRS, CHUNK, LANES), F32)],
        compiler_params=_cparams(("arbitrary",)),
        name="prompt_rwkv",
    )(h_main, h_main, x, prm["w_tail"], prm["mu"], prm["mu_tail"], prm["w0"], prm["a0"], prm["k_k"], prm["k_a"],
      prm["r_k"], prm["gn_g"], prm["gn_b"], prm["wd"], prm["wa"], prm["wg"])


def _sample_prep_kernel(h_ref, x_ref, wt_ref, shift_ref, mu_ref, mut_ref, w0_ref, a0_ref, kk_ref, ka_ref,
                        wd_ref, wa_ref, wg_ref, r_ref, k_ref, v_ref, g_ref, tail_ref,
                        rt_ref, wtr_ref, kt_ref, vt_ref, at_ref, bt_ref):
    ones = _head_ones()
    feat = h_ref[:, D_QKV:D_MAIN]
    tail = _dot1(x_ref[...], wt_ref[...], NT)
    tail_ref[...] = tail
    mixed = _token_mix(feat, shift_ref[:, 0:D_RKV], mu_ref[...])
    mixed_tail = _token_mix(tail, shift_ref[:, D_RKV:D_SHIFT], mut_ref[...])
    r, ld, k2, v, av, bv, g = _rwkv_prep(mixed, mixed_tail, w0_ref[...], a0_ref[...], kk_ref[...], ka_ref[...],
                                         wd_ref[...], wa_ref[...], wg_ref[...], ones)
    r_ref[...] = r
    k_ref[...] = k2
    v_ref[...] = v
    g_ref[...] = g
    rt_ref[...] = r.T
    wtr_ref[...] = jnp.exp(ld).T
    kt_ref[...] = k2.T
    vt_ref[...] = v.T
    at_ref[...] = av.T
    bt_ref[...] = bv.T


def _sample_prep(h_main, x, shift, prm):
    tok = jax.ShapeDtypeStruct((DEC_BATCH, D_RWKV), F32)
    chan = jax.ShapeDtypeStruct((D_RWKV, DEC_BATCH), F32)
    return pl.pallas_call(
        _sample_prep_kernel,
        out_shape=(tok,) * 4 + (jax.ShapeDtypeStruct((DEC_BATCH, D_TAIL), F32),) + (chan,) * 6,
        compiler_params=pltpu.CompilerParams(vmem_limit_bytes=VMEM_LIMIT),
        name="sample_rwkv_prep",
    )(h_main, x, prm["w_tail"], shift, prm["mu"], prm["mu_tail"], prm["w0"], prm["a0"], prm["k_k"], prm["k_a"],
      prm["wd"], prm["wa"], prm["wg"])


STEP_GROUP = 4


def _sample_step_kernel(s_ref, r_ref, w_ref, k_ref, a_ref, b_ref, v_ref, y_ref, snew_ref):
    r, w, k, a, b = r_ref[...], w_ref[...], k_ref[...], a_ref[...], b_ref[...]
    for g0 in range(0, HEAD_DIM, 2 * STEP_GROUP):
        chans = range(g0, g0 + 2 * STEP_GROUP)
        sa = {i: jnp.sum(s_ref[0, i] * a, axis=0, keepdims=True) for i in chans}
        s_new = {i: s_ref[0, i] * w + sa[i] * b + v_ref[i:i + 1, :] * k for i in chans}
        for i in chans:
            y_ref[i:i + 1, :] = jnp.sum(s_new[i] * r, axis=0, keepdims=True)
        for i in range(g0, g0 + 2 * STEP_GROUP, 2):
            pair = jnp.concatenate([s_new[i], s_new[i + 1]], axis=0)
            snew_ref[:, i * HEAD_DIM:(i + 2) * HEAD_DIM] = pair.T


def _sample_step(state_t, r_t, w_t, k_t, a_t, b_t, v_t):
    head_rows = pl.BlockSpec((HEAD_DIM, DEC_BATCH), lambda h: (h, 0))
    return pl.pallas_call(
        _sample_step_kernel,
        out_shape=(jax.ShapeDtypeStruct((D_RWKV, DEC_BATCH), F32),
                   jax.ShapeDtypeStruct((DEC_BATCH, N_RWKV_HEADS * HEAD_DIM * HEAD_DIM), F32)),
        grid=(N_RWKV_HEADS,),
        in_specs=[pl.BlockSpec((1, HEAD_DIM, HEAD_DIM, DEC_BATCH), lambda h: (h, 0, 0, 0))] + [head_rows] * 6,
        out_specs=(head_rows, pl.BlockSpec((DEC_BATCH, HEAD_DIM * HEAD_DIM), lambda h: (0, h))),
        compiler_params=_cparams(("arbitrary",)),
        name="sample_rwkv_step",
    )(state_t, r_t, w_t, k_t, a_t, b_t, v_t)


def _sample_post_kernel(yt_ref, r_ref, k_ref, v_ref, g_ref, rk_ref, gng_ref, gnb_ref, o_ref):
    o_ref[...] = _rwkv_post(yt_ref[...].T, r_ref[...], k_ref[...], v_ref[...], g_ref[...], rk_ref[...],
                            gng_ref[...], gnb_ref[...], _head_ones())


def _sample_post(y, r, k, v, g, prm):
    return pl.pallas_call(
        _sample_post_kernel,
        out_shape=jax.ShapeDtypeStruct((DEC_BATCH, D_RWKV), F32),
        compiler_params=pltpu.CompilerParams(vmem_limit_bytes=VMEM_LIMIT),
        name="sample_rwkv_post",
    )(y, r, k, v, g, prm["r_k"], prm["gn_g"], prm["gn_b"])


def _project_mix(attn_ref, rwkv_ref, wo_ref):
    return (_dot(attn_ref[...].astype(BF16), wo_ref[0:D_ATTN, :])
            + _dot(rwkv_ref[...].astype(BF16), wo_ref[D_ATTN:D_ATTN + D_RWKV, :]))


def _norm_and_route(mix, x_ref, g_ref, b_ref, wr_ref, br_ref, x1_ref, x1b_ref, route_ref):
    x1 = _layer_norm(ALPHA * x_ref[...] + mix, g_ref[...], b_ref[...])
    x1_ref[...] = x1
    x1b = x1.astype(BF16)
    x1b_ref[...] = _pack_bf16_halves(x1b)
    logits = _dot(x1b, wr_ref[...].astype(BF16)) + br_ref[...]
    tm = logits.shape[0]
    lane = lax.broadcasted_iota(jnp.int32, (tm, LANES), 1).astype(F32)
    big = float(2 * LANES)
    neg = -jnp.inf
    lc = jnp.where(lane < N_GROUPS, logits, neg)
    mc = jnp.max(lc, axis=-1, keepdims=True)
    g_sel = jnp.min(jnp.where(lc == mc, lane, big), axis=-1, keepdims=True)
    p_group = 1.0 / jnp.sum(jnp.exp(lc - mc), axis=-1, keepdims=True)
    lo = ROUTE_FINE_OFF + g_sel * EXPERTS_PER_GROUP
    lf = jnp.where((lane >= lo) & (lane < lo + EXPERTS_PER_GROUP), logits, neg)
    v1 = jnp.max(lf, axis=-1, keepdims=True)
    i1 = jnp.min(jnp.where(lf == v1, lane, big), axis=-1, keepdims=True)
    lf2 = jnp.where(lane == i1, neg, lf)
    v2 = jnp.max(lf2, axis=-1, keepdims=True)
    i2 = jnp.min(jnp.where(lf2 == v2, lane, big), axis=-1, keepdims=True)
    e21 = jnp.exp(v2 - v1)
    gate1 = p_group / (1.0 + e21)
    gate2 = p_group * e21 / (1.0 + e21)
    route = jnp.where(lane == 0, i1 - ROUTE_FINE_OFF,
                      jnp.where(lane == 1, i2 - ROUTE_FINE_OFF,
                                jnp.where(lane == 2, gate1, jnp.where(lane == 3, gate2, 0.0))))
    route_ref[...] = route


N_ROUTER_OUTS = 3


def _outproj_router_kernel(n_tiles, n_aliased, attn_ref, rwkv_ref, x_ref, wo_ref, g_ref, b_ref, wr_ref, br_ref,
                           *rest):
    outs = rest[n_aliased:n_aliased + N_ROUTER_OUTS]
    mix_ref = rest[-1]
    i = pl.program_id(0)
    finish = lambda mix: _norm_and_route(mix, x_ref, g_ref, b_ref, wr_ref, br_ref, *outs)

    @pl.when(i == 0)
    def _():
        mix_ref[...] = _project_mix(attn_ref, rwkv_ref, wo_ref)

    @pl.when((i >= 1) & (i < n_tiles))
    def _():
        finish(mix_ref[...])
        mix_ref[...] = _project_mix(attn_ref, rwkv_ref, wo_ref)

    @pl.when(i == n_tiles)
    def _():
        finish(mix_ref[...])

    @pl.when(i > n_tiles)
    def _():
        for out_ref in outs:
            out_ref[...] = jnp.zeros_like(out_ref)


def _outproj_router(attn, rwkv, x, wo_bf16, ln_g, ln_b, w_route, b_route, tm, n_total, row_block, into, name):
    m = x.shape[0]
    n_tiles = m // tm
    const = lambda shape: pl.BlockSpec(shape, lambda i: (0, 0))
    ahead = lambda width: pl.BlockSpec((tm, width), lambda i: (jnp.minimum(i, n_tiles - 1), 0))
    behind = lambda width: pl.BlockSpec((tm, width), lambda i: (jnp.clip(i - 1, 0, n_tiles - 1), 0))
    in_specs = [ahead(D_ATTN), ahead(D_RWKV), behind(D_MODEL),
                const((D_MODEL, D_MODEL)), const((1, D_MODEL)), const((1, D_MODEL)),
                const((D_MODEL, LANES)), const((1, LANES))]
    args = [attn, rwkv, x, wo_bf16, ln_g, ln_b, w_route, b_route]
    aliases, n_aliased, fill_steps = {}, 0, pl.cdiv(n_total - m, tm)
    if into is not None:
        n_aliased, fill_steps = N_ROUTER_OUTS, 0
        in_specs += [pl.BlockSpec(memory_space=pl.ANY)] * N_ROUTER_OUTS
        aliases = {len(args) + k: k for k in range(N_ROUTER_OUTS)}
        args += list(into)
    out_rows = lambda width: pl.BlockSpec((tm, width), lambda i: (jnp.maximum(i - 1, 0) + row_block, 0))
    return pl.pallas_call(
        functools.partial(_outproj_router_kernel, n_tiles, n_aliased),
        out_shape=(jax.ShapeDtypeStruct((n_total, D_MODEL), F32),
                   jax.ShapeDtypeStruct((n_total, D_MODEL // 2), jnp.uint32),
                   jax.ShapeDtypeStruct((n_total, LANES), F32)),
        grid=(n_tiles + 1 + fill_steps,),
        in_specs=in_specs,
        out_specs=(out_rows(D_MODEL), out_rows(D_MODEL // 2), out_rows(LANES)),
        scratch_shapes=[pltpu.VMEM((tm, D_MODEL), F32)],
        input_output_aliases=aliases,
        compiler_params=_cparams(("arbitrary",)),
        name=name,
    )(*args)


DISPATCH_TILE = 128


def _dispatch_kernel(zoff_ref, dest_ref, x_ref, o_hbm, zbuf, ring, zsem, sem):
    i = pl.program_id(0)
    n_blocks = o_hbm.shape[0] // MOE_BLOCK
    n_used = zoff_ref[N_EXPERTS]

    def zero_fill(start_row):
        start_row = pl.multiple_of(start_row, MOE_BLOCK)
        return pltpu.make_async_copy(zbuf, o_hbm.at[pl.ds(start_row, MOE_BLOCK)], zsem)

    def zero_fills(action):
        for e in range(N_EXPERTS):
            @pl.when(zoff_ref[e] >= 0)
            def _():
                action(zero_fill(zoff_ref[e]))
        for b in range(n_blocks):
            @pl.when(b >= n_used)
            def _():
                action(zero_fill(b * MOE_BLOCK))

    @pl.when(i == 0)
    def _():
        zbuf[...] = jnp.zeros_like(zbuf)
        zero_fills(lambda copy: copy.start())
        zero_fills(lambda copy: copy.wait())

    cur = lax.rem(i, 2)

    def wait_rows(slot):
        for k in range(2):
            pltpu.make_async_copy(ring.at[slot], o_hbm.at[pl.ds(0, DISPATCH_TILE)], sem.at[slot]).wait()

    @pl.when(i >= 2)
    def _():
        wait_rows(cur)

    ring[cur] = x_ref[...]
    for t in range(DISPATCH_TILE):
        for k in range(2):
            pltpu.make_async_copy(ring.at[cur, pl.ds(t, 1)], o_hbm.at[pl.ds(dest_ref[0, 0, 2 * t + k], 1)],
                                  sem.at[cur]).start()

    @pl.when(i == pl.num_programs(0) - 1)
    def _():
        wait_rows(cur)

        @pl.when(i >= 1)
        def _():
            wait_rows(1 - cur)


def _dispatch(zero_offsets, dest, x_packed, n_blocks):
    n_tokens, width = x_packed.shape
    grid_spec = pltpu.PrefetchScalarGridSpec(
        num_scalar_prefetch=1,
        grid=(n_tokens // DISPATCH_TILE,),
        in_specs=[pl.BlockSpec((1, 1, 2 * DISPATCH_TILE), lambda i, z: (i, 0, 0), memory_space=pltpu.SMEM),
                  pl.BlockSpec((DISPATCH_TILE, width), lambda i, z: (i, 0))],
        out_specs=pl.BlockSpec(memory_space=pl.ANY),
        scratch_shapes=[pltpu.VMEM((MOE_BLOCK, width), x_packed.dtype),
                        pltpu.VMEM((2, DISPATCH_TILE, width), x_packed.dtype),
                        pltpu.SemaphoreType.DMA, pltpu.SemaphoreType.DMA((2,))],
    )
    return pl.pallas_call(
        _dispatch_kernel,
        out_shape=jax.ShapeDtypeStruct((n_blocks * MOE_BLOCK, width), x_packed.dtype),
        grid_spec=grid_spec,
        compiler_params=_cparams(("arbitrary",)),
        name="moe_dispatch",
    )(zero_offsets, dest.reshape(-1, 1, 2 * DISPATCH_TILE), x_packed)


def _expert_kernel(be_ref, nb_ref, x_ref, wg_hbm, wu_hbm, wd_hbm, o_ref, wg_buf, wu_buf, wd_buf, slot_ref, sem):
    blk = pl.program_id(0)
    n_used = nb_ref[0]
    expert = be_ref[blk]
    is_first = (blk == 0) | (be_ref[jnp.maximum(blk - 1, 0)] != expert)

    def fetch(e, slot):
        return [pltpu.make_async_copy(hbm.at[e], buf.at[slot], sem.at[slot, i])
                for i, (hbm, buf) in enumerate(((wg_hbm, wg_buf), (wu_hbm, wu_buf), (wd_hbm, wd_buf)))]

    @pl.when((blk < n_used) & is_first)
    def _():
        @pl.when(blk == 0)
        def _():
            slot_ref[0] = 1
            for copy in fetch(expert, 0):
                copy.start()

        slot = 1 - slot_ref[0]
        slot_ref[0] = slot
        for copy in fetch(expert, slot):
            copy.wait()
        nxt = lax.while_loop(lambda j: (j < n_used) & (be_ref[jnp.minimum(j, n_used - 1)] == expert),
                             lambda j: j + 1, blk + 1)

        @pl.when(nxt < n_used)
        def _():
            for copy in fetch(be_ref[jnp.minimum(nxt, n_used - 1)], 1 - slot):
                copy.start()

    @pl.when(blk < n_used)
    def _():
        slot = slot_ref[0]
        half = D_MODEL // 2
        x_head, x_tail = _unpack_bf16_halves(x_ref[...])
        proj = lambda w_buf: (_dot(x_head, w_buf[slot, 0:half, :].astype(BF16))
                              + _dot(x_tail, w_buf[slot, half:D_MODEL, :].astype(BF16)))
        gate = proj(wg_buf)
        up = proj(wu_buf)
        h = gate * _sigmoid(gate) * up
        o_ref[...] = _dot(h.astype(BF16), wd_buf[slot].astype(BF16))

    @pl.when(blk >= n_used)
    def _():
        o_ref[...] = jnp.zeros_like(o_ref)


def _expert_mlp(block_expert, n_used, x_sorted, w_gate, w_up, w_down, n_blocks):
    grid_spec = pltpu.PrefetchScalarGridSpec(
        num_scalar_prefetch=2,
        grid=(n_blocks,),
        in_specs=[pl.BlockSpec((MOE_BLOCK, D_MODEL // 2), lambda b, be, nb: (jnp.minimum(b, nb[0] - 1), 0)),
                  pl.BlockSpec(memory_space=pl.ANY), pl.BlockSpec(memory_space=pl.ANY),
                  pl.BlockSpec(memory_space=pl.ANY)],
        out_specs=pl.BlockSpec((MOE_BLOCK, D_MODEL), lambda b, be, nb: (b, 0)),
        scratch_shapes=[pltpu.VMEM((2, D_MODEL, D_EXPERT), F32), pltpu.VMEM((2, D_MODEL, D_EXPERT), F32),
                        pltpu.VMEM((2, D_EXPERT, D_MODEL), F32), pltpu.SMEM((1,), jnp.int32),
                        pltpu.SemaphoreType.DMA((2, 3))],
    )
    return pl.pallas_call(
        _expert_kernel,
        out_shape=jax.ShapeDtypeStruct((n_blocks * MOE_BLOCK, D_MODEL), F32),
        grid_spec=grid_spec,
        compiler_params=_cparams(("arbitrary",)),
        name="expert_mlp",
    )(block_expert, n_used, x_sorted, w_gate, w_up, w_down)


COMBINE_TILE = 256


def _combine_kernel(dest_ref, dest_next_ref, y_hbm, x1_ref, route_ref, g_ref, b_ref, o_ref, ybuf, sem):
    i = pl.program_id(0)
    cur = lax.rem(i, 2)
    tile = x1_ref.shape[0]
    n_rows = 2 * tile

    def gather(table_ref, buf):
        for slot in range(n_rows):
            pltpu.make_async_copy(y_hbm.at[pl.ds(table_ref[0, 0, slot], 1)], ybuf.at[buf, pl.ds(slot, 1)],
                                  sem.at[buf]).start()

    def wait_gather(buf):
        pltpu.make_async_copy(y_hbm.at[pl.ds(0, n_rows)], ybuf.at[buf], sem.at[buf]).wait()

    @pl.when(i == 0)
    def _():
        gather(dest_ref, 0)

    gather(dest_next_ref, 1 - cur)
    wait_gather(cur)
    route = route_ref[...]
    yb = ybuf[cur]
    moe = route[:, 2:3] * yb[0:tile, :] + route[:, 3:4] * yb[tile:n_rows, :]
    o_ref[...] = _layer_norm(ALPHA * x1_ref[...] + moe, g_ref[...], b_ref[...])

    @pl.when(i == pl.num_programs(0) - 1)
    def _():
        wait_gather(1 - cur)


def _combine(dest, y_slots, x1_all, route_all, m, tm, row_block, ln_g, ln_b, name):
    return pl.pallas_call(
        _combine_kernel,
        out_shape=jax.ShapeDtypeStruct((m, D_MODEL), F32),
        grid=(m // tm,),
        in_specs=[pl.BlockSpec((1, 1, 2 * tm), lambda i: (i, 0, 0), memory_space=pltpu.SMEM),
                  pl.BlockSpec((1, 1, 2 * tm), lambda i: (i + 1, 0, 0), memory_space=pltpu.SMEM),
                  pl.BlockSpec(memory_space=pl.ANY),
                  pl.BlockSpec((tm, D_MODEL), lambda i: (i + row_block, 0)),
                  pl.BlockSpec((tm, LANES), lambda i: (i + row_block, 0)),
                  pl.BlockSpec((1, D_MODEL), lambda i: (0, 0)),
                  pl.BlockSpec((1, D_MODEL), lambda i: (0, 0))],
        out_specs=pl.BlockSpec((tm, D_MODEL), lambda i: (i, 0)),
        scratch_shapes=[pltpu.VMEM((2, 2 * tm, D_MODEL), F32), pltpu.SemaphoreType.DMA((2,))],
        compiler_params=_cparams(("arbitrary",)),
        name=name,
    )(dest, dest, y_slots, x1_all, route_all, ln_g, ln_b)


def _dispatch_plan(route_all, n_blocks):
    flat_e = route_all[:, 0:2].astype(jnp.int32).reshape(-1)
    onehot = (flat_e[:, None] == jnp.arange(N_EXPERTS, dtype=jnp.int32)[None, :]).astype(jnp.int32)
    csum = jnp.cumsum(onehot, axis=0)
    rank = jnp.sum(onehot * csum, axis=1) - 1
    counts = csum[-1]
    padded = (counts + MOE_BLOCK - 1) // MOE_BLOCK * MOE_BLOCK
    pend = jnp.cumsum(padded)
    pstart = pend - padded
    dest = (pstart[flat_e] + rank).astype(jnp.int32)
    zero_offsets = jnp.where(counts > 0, pend - MOE_BLOCK, -1).astype(jnp.int32)
    n_used = (pend[-1] // MOE_BLOCK).astype(jnp.int32)
    block_start = jnp.minimum(jnp.arange(n_blocks, dtype=jnp.int32), n_used - 1) * MOE_BLOCK
    block_e = jnp.minimum(jnp.searchsorted(pend, block_start, side="right"), N_EXPERTS - 1).astype(jnp.int32)
    return dest, jnp.concatenate([zero_offsets, n_used.reshape(1)]), block_e, n_used.reshape(1)


def kernel(x_prompt, x_sample, cache_k_win, cache_v_win, state_wkv, state_shift, w_in, attn_sinks, shift_mu, w0,
           w_decay_up, a0, w_a_up, w_g_up, k_k, k_a, r_k, gn_g, gn_b, w_out, ln1_g, ln1_b, w_coarse, b_coarse,
           w_fine, b_fine, w_exp_gate, w_exp_up, w_exp_down, ln2_g, ln2_b):
    xp = x_prompt[0]
    xs = x_sample[:, 0]
    row = lambda a: a.reshape(1, -1)

    w_in_t = jnp.swapaxes(w_in[0], 0, 1)
    prm = dict(mu=row(shift_mu[0, :D_RKV]), mu_tail=row(shift_mu[0, D_RKV:]), w_tail=w_in_t[D_MAIN:].astype(BF16),
               w0=row(w0[0]), a0=row(a0[0]), k_k=row(k_k[0]), k_a=row(k_a[0]),
               r_k=row(r_k[0]), gn_g=row(gn_g[0]), gn_b=row(gn_b[0]),
               wd=w_decay_up[0], wa=w_a_up[0], wg=w_g_up[0])
    sinks = attn_sinks[0]
    wo_bf16 = w_out[0].astype(BF16)
    w_route = jnp.pad(jnp.concatenate([w_coarse[0], w_fine[0]], axis=1), ((0, 0), (0, LANES - N_GROUPS - N_EXPERTS)))
    b_route = jnp.pad(jnp.concatenate([b_coarse[0], b_fine[0]]), (0, LANES - N_GROUPS - N_EXPERTS)).reshape(1, LANES)

    hp = _matmul(xp, w_in_t, D_MAIN, MAIN_TM, MAIN_TN, "in_proj_prompt")
    hs = _matmul(xs, w_in_t, D_MAIN, DEC_BATCH, MAIN_TN, "in_proj_sample")

    attn_p = _prompt_attention(hp, sinks)
    rwkv_p, state_p, tail_p = _prompt_rwkv(hp, xp, prm)

    q_s = hs[:, :D_ATTN].reshape(DEC_BATCH, N_Q_HEADS, HEAD_DIM)
    k_s = hs[:, D_ATTN:D_ATTN + D_KV].reshape(DEC_BATCH, 1, D_KV)
    v_s = hs[:, D_ATTN + D_KV:D_QKV].reshape(DEC_BATCH, 1, D_KV)
    window_t = lambda c: jnp.transpose(c, (0, 2, 3, 1)).reshape(DEC_BATCH, D_KV, WINDOW)
    attn_s, kwin_s, vwin_s = _sample_attention(
        q_s, k_s, v_s, window_t(cache_k_win[0]), window_t(cache_v_win[0]), sinks.reshape(N_Q_HEADS, 1))
    r_s, k2_s, vv_s, g_s, tail_s, r_t, w_t, k_t, v_t, a_t, b_t = _sample_prep(hs, xs, state_shift[0], prm)
    y_t, state_s = _sample_step(jnp.transpose(state_wkv[0], (1, 2, 3, 0)), r_t, w_t, k_t, a_t, b_t, v_t)
    state_s = state_s.reshape(DEC_BATCH, N_RWKV_HEADS, HEAD_DIM, HEAD_DIM)
    rwkv_s = _sample_post(y_t, r_s, k2_s, vv_s, g_s, prm)

    n_tokens = SEQ + DEC_BATCH
    outs_pr = _outproj_router(attn_p, rwkv_p, xp, wo_bf16, row(ln1_g[0]), row(ln1_b[0]), w_route, b_route,
                              256, n_tokens, 0, None, "outproj_router_prompt")
    x1_all, x1b_all, route_all = _outproj_router(attn_s.reshape(DEC_BATCH, D_ATTN), rwkv_s, xs,
                                                 wo_bf16, row(ln1_g[0]), row(ln1_b[0]), w_route, b_route,
                                                 DEC_BATCH, n_tokens, SEQ // DEC_BATCH, outs_pr,
                                                 "outproj_router_sample")

    n_assign = 2 * n_tokens
    n_blocks = -(-(n_assign + N_EXPERTS * (MOE_BLOCK - 1)) // MOE_BLOCK)
    dest, zero_offsets, block_e, n_used = _dispatch_plan(route_all, n_blocks)
    x_sorted = _dispatch(zero_offsets, dest, x1b_all, n_blocks)
    y_slots = _expert_mlp(block_e, n_used, x_sorted, w_exp_gate[0], w_exp_up[0], w_exp_down[0], n_blocks)

    def dest_tiles(d, tile):
        d = d.reshape(-1, tile, 2)
        d = jnp.concatenate([d[:, :, 0], d[:, :, 1]], axis=1)
        return jnp.pad(d, ((0, 1), (0, 0))).reshape(-1, 1, 2 * tile)

    y_p = _combine(dest_tiles(dest[:2 * SEQ], COMBINE_TILE), y_slots, x1_all, route_all, SEQ, COMBINE_TILE, 0,
                   row(ln2_g[0]), row(ln2_b[0]), "combine_prompt")
    y_s = _combine(dest_tiles(dest[2 * SEQ:], DEC_BATCH), y_slots, x1_all, route_all, DEC_BATCH, DEC_BATCH,
                   SEQ // DEC_BATCH, row(ln2_g[0]), row(ln2_b[0]), "combine_sample")

    kv4 = lambda a: a.reshape(a.shape[0], N_KV_HEADS, HEAD_DIM)
    k_win_p = kv4(hp[SEQ - WINDOW:, D_ATTN:D_ATTN + D_KV])[None, None]
    v_win_p = kv4(hp[SEQ - WINDOW:, D_ATTN + D_KV:D_QKV])[None, None]
    sp = state_p.reshape(N_PAIRS, HEADS_PER_TILE, HEAD_DIM, HEADS_PER_TILE, HEAD_DIM)
    wkv_p = jnp.stack([sp[:, i, :, i, :] for i in range(HEADS_PER_TILE)], axis=1)
    wkv_p = wkv_p.reshape(N_RWKV_HEADS, HEAD_DIM, HEAD_DIM).transpose(0, 2, 1)[None, None]
    shift_p = jnp.concatenate([hp[SEQ - 1:SEQ, D_QKV:], tail_p[0:1]], axis=1)[None]
    shift_s = jnp.concatenate([hs[:, D_QKV:], tail_s], axis=1)[None]
    return (y_p[None], y_s[:, None, :], k_win_p, v_win_p, wkv_p, shift_p,
            kwin_s.reshape(1, DEC_BATCH, WINDOW, N_KV_HEADS, HEAD_DIM),
            vwin_s.reshape(1, DEC_BATCH, WINDOW, N_KV_HEADS, HEAD_DIM),
            state_s[None], shift_s)
```

```python
import functools
import math

import jax
import jax.numpy as jnp
from jax import lax
from jax.experimental import pallas as pl
from jax.experimental.pallas import tpu as pltpu

F32 = jnp.float32
BF16 = jnp.bfloat16

D_MODEL = 2048
SEQ = 8192
DEC_BATCH = 128
HEAD_DIM = 64
D_ATTN = 1024
D_RWKV = 1024
N_Q_HEADS = 16
N_KV_HEADS = 4
Q_PER_KV = 4
D_KV = 256
WINDOW = 128
ATTN_SCALE = HEAD_DIM ** -0.5
N_RWKV_HEADS = 16
W_LORA = 64
A_LORA = 64
G_LORA = 160
D_SHIFT = 3 * D_RWKV + W_LORA + A_LORA + G_LORA
D_QKV = D_ATTN + 2 * D_KV
N_GROUPS = 4
EXPERTS_PER_GROUP = 8
N_EXPERTS = 32
D_EXPERT = 512
ALPHA = 2.0 ** 0.25
LN_EPS = 1e-5
GN_EPS = 64e-5

SUBLANES = 8
LANES = 128
VMEM_LIMIT = 52 * 1024 * 1024

D_RKV = 3 * D_RWKV
D_TAIL = W_LORA + A_LORA + G_LORA
D_MAIN = D_QKV + D_RKV
MAIN_TN = 1536
MAIN_TM = 512
TAIL_TM = 1024

CHUNK = 64
HEADS_PER_TILE = LANES // HEAD_DIM
N_PAIRS = N_RWKV_HEADS // HEADS_PER_TILE
SOLVE_LEVELS = int(math.log2(CHUNK))
PAIR_GROUP = 8

MOE_BLOCK = 256
ROUTE_FINE_OFF = N_GROUPS

NN = (((1,), (0,)), ((), ()))
NT = (((1,), (1,)), ((), ()))


def _dot(a, b, dims=NN):
    return lax.dot_general(a, b, dims, preferred_element_type=F32)


def _dot1(a, b, dims=NN):
    return _dot(a.astype(BF16), b.astype(BF16), dims)


def _split(x):
    hi = x.astype(BF16)
    lo = (x - hi.astype(F32)).astype(BF16)
    return hi, lo


def _dot3(a, b, dims=NN):
    ah, al = _split(a)
    bh, bl = _split(b)
    return _dot(ah, bh, dims) + (_dot(ah, bl, dims) + _dot(al, bh, dims))


def _dot_exact_lhs(a_bf16, b, dims=NN):
    bh, bl = _split(b)
    return _dot(a_bf16, bh, dims) + _dot(a_bf16, bl, dims)


def _dot_exact_rhs(a, b_bf16, dims=NN):
    ah, al = _split(a)
    return _dot(ah, b_bf16, dims) + _dot(al, b_bf16, dims)


def _div_pow2(x, d):
    return lax.shift_right_logical(x, jnp.int32(int(math.log2(d))))


def _mod_pow2(x, d):
    return lax.bitwise_and(x, jnp.int32(d - 1))


def _pack_bf16_halves(x_bf16):
    n = x_bf16.shape[1] // 2
    bits = lax.bitcast_convert_type(x_bf16.astype(F32), jnp.uint32)
    return lax.bitwise_or(bits[:, 0:n], lax.shift_right_logical(bits[:, n:2 * n], jnp.uint32(16)))


def _unpack_bf16_halves(packed):
    hi = lax.bitcast_convert_type(lax.bitwise_and(packed, jnp.uint32(0xFFFF0000)), F32)
    lo = lax.bitcast_convert_type(lax.shift_left(packed, jnp.uint32(16)), F32)
    return hi.astype(BF16), lo.astype(BF16)


def _sigmoid(x):
    return 1.0 / (1.0 + jnp.exp(-x))


def _softplus(x):
    return jnp.maximum(x, 0.0) + jnp.log(1.0 + jnp.exp(-jnp.abs(x)))


def _layer_norm(z, g, b):
    mu = jnp.mean(z, axis=-1, keepdims=True)
    d = z - mu
    var = jnp.mean(d * d, axis=-1, keepdims=True)
    return d * lax.rsqrt(var + LN_EPS) * g + b


def _cparams(sem):
    return pltpu.CompilerParams(dimension_semantics=sem, vmem_limit_bytes=VMEM_LIMIT)


def _matmul_kernel(x_ref, wt_ref, o_ref):
    o_ref[...] = _dot(x_ref[...].astype(BF16), wt_ref[...].astype(BF16), NT)


def _matmul(x, w_t, n_out, tm, tn, name):
    m, k = x.shape
    tm = min(tm, m)
    return pl.pallas_call(
        _matmul_kernel,
        out_shape=jax.ShapeDtypeStruct((m, n_out), F32),
        grid=(n_out // tn, m // tm),
        in_specs=[pl.BlockSpec((tm, k), lambda j, i: (i, 0)),
                  pl.BlockSpec((tn, k), lambda j, i: (j, 0))],
        out_specs=pl.BlockSpec((tm, tn), lambda j, i: (i, j)),
        compiler_params=_cparams(("arbitrary", "arbitrary")),
        name=name,
    )(x, w_t)


def _band_bias():
    qi = jnp.arange(Q_PER_KV * WINDOW)[:, None] % WINDOW
    kj = jnp.arange(2 * WINDOW)[None, :]
    diff = qi + WINDOW - kj
    band = (diff >= 0) & (diff <= WINDOW)
    keep = jnp.stack([band & (kj >= WINDOW), band])
    return jnp.where(keep, 0.0, -jnp.inf).astype(F32)


def _prompt_attn_kernel(q_ref, kvp_ref, kvc_ref, bias_ref, sink_ref, o_ref):
    q = q_ref[...]
    kv_prev = kvp_ref[...]
    kv_cur = kvc_ref[...]
    bias = bias_ref[0]
    row_head = _div_pow2(lax.broadcasted_iota(jnp.int32, (Q_PER_KV * WINDOW, 1), 0), WINDOW)
    groups = range(N_KV_HEADS)
    kv_cols = lambda off, g: jnp.concatenate([kv_prev[:, off + g * HEAD_DIM:off + (g + 1) * HEAD_DIM],
                                              kv_cur[:, off + g * HEAD_DIM:off + (g + 1) * HEAD_DIM]],
                                             axis=0).astype(BF16)
    q_rows = lambda g: jnp.concatenate(
        [q[:, (g * Q_PER_KV + h) * HEAD_DIM:(g * Q_PER_KV + h + 1) * HEAD_DIM] for h in range(Q_PER_KV)],
        axis=0).astype(BF16)
    s = [_dot(q_rows(g), kv_cols(0, g), NT) * ATTN_SCALE + bias for g in groups]
    sink = []
    for g in groups:
        col = jnp.zeros((Q_PER_KV * WINDOW, 1), F32)
        for h in range(Q_PER_KV):
            col = jnp.where(row_head == h, sink_ref[g * Q_PER_KV + h], col)
        sink.append(col)
    m = [jnp.maximum(jnp.max(s[g], axis=-1, keepdims=True), sink[g]) for g in groups]
    p = [jnp.exp(s[g] - m[g]) for g in groups]
    denom = [jnp.sum(p[g], axis=-1, keepdims=True) + jnp.exp(sink[g] - m[g]) for g in groups]
    o = [_dot((p[g] / denom[g]).astype(BF16), kv_cols(D_KV, g)) for g in groups]
    o_ref[...] = jnp.concatenate([o[g][h * WINDOW:(h + 1) * WINDOW, :] for g in groups for h in range(Q_PER_KV)],
                                 axis=1)


def _prompt_attention(h_attn, sinks):
    nb = SEQ // WINDOW
    return pl.pallas_call(
        _prompt_attn_kernel,
        out_shape=jax.ShapeDtypeStruct((SEQ, D_ATTN), F32),
        grid=(nb,),
        in_specs=[pl.BlockSpec((WINDOW, D_ATTN), lambda i: (i, 0)),
                  pl.BlockSpec((WINDOW, 2 * D_KV), lambda i: (jnp.maximum(i - 1, 0), 2)),
                  pl.BlockSpec((WINDOW, 2 * D_KV), lambda i: (i, 2)),
                  pl.BlockSpec((1, Q_PER_KV * WINDOW, 2 * WINDOW), lambda i: (jnp.minimum(i, 1), 0, 0)),
                  pl.BlockSpec(memory_space=pltpu.SMEM)],
        out_specs=pl.BlockSpec((WINDOW, D_ATTN), lambda i: (i, 0)),
        compiler_params=_cparams(("arbitrary",)),
        name="prompt_attention",
    )(h_attn, h_attn, h_attn, _band_bias(), sinks)


SAMPLE_ATTN_TILE = 8


def _sample_attn_kernel(q_ref, knew_ref, vnew_ref, ck_ref, cv_ref, sink_ref, o_ref, kwin_ref, vwin_ref):
    lane = lax.broadcasted_iota(jnp.int32, (N_Q_HEADS, D_KV), 1)
    head = lax.broadcasted_iota(jnp.int32, (N_Q_HEADS, D_KV), 0)
    group_mask = _div_pow2(lane, HEAD_DIM) == _div_pow2(head, Q_PER_KV)
    sink = sink_ref[...]
    row = lax.broadcasted_iota(jnp.int32, (WINDOW, D_KV), 0)
    seqs = range(SAMPLE_ATTN_TILE)
    qbd = [jnp.where(group_mask, jnp.concatenate([q_ref[b]] * N_KV_HEADS, axis=1), 0.0).astype(BF16) for b in seqs]
    s = [_dot1(qbd[b], ck_ref[b]) * ATTN_SCALE for b in seqs]
    s_new = [jnp.sum(qbd[b].astype(F32) * knew_ref[b].astype(BF16).astype(F32), axis=-1, keepdims=True) * ATTN_SCALE
             for b in seqs]
    m = [jnp.maximum(jnp.maximum(jnp.max(s[b], axis=-1, keepdims=True), s_new[b]), sink) for b in seqs]
    p = [jnp.exp(s[b] - m[b]) for b in seqs]
    p_new = [jnp.exp(s_new[b] - m[b]) for b in seqs]
    denom = [jnp.sum(p[b], axis=-1, keepdims=True) + p_new[b] + jnp.exp(sink - m[b]) for b in seqs]
    for b in seqs:
        kb = ck_ref[b].T
        vb = cv_ref[b].T
        kn = knew_ref[b]
        vn = vnew_ref[b]
        o_full = (_dot1(p[b] / denom[b], vb)
                  + (p_new[b] / denom[b]).astype(BF16).astype(F32) * vn.astype(BF16).astype(F32))
        o_full = jnp.where(group_mask, o_full, 0.0)
        o = o_full[:, 0:HEAD_DIM]
        for g in range(1, N_KV_HEADS):
            o = o + o_full[:, g * HEAD_DIM:(g + 1) * HEAD_DIM]
        o_ref[b] = o
        kwin_ref[b] = jnp.where(row == WINDOW - 1, kn, pltpu.roll(kb, WINDOW - 1, axis=0))
        vwin_ref[b] = jnp.where(row == WINDOW - 1, vn, pltpu.roll(vb, WINDOW - 1, axis=0))


def _sample_attention(q, k_new, v_new, cache_k_t, cache_v_t, sinks):
    bt = SAMPLE_ATTN_TILE
    win_spec = pl.BlockSpec((bt, WINDOW, D_KV), lambda i: (i, 0, 0))
    win_t_spec = pl.BlockSpec((bt, D_KV, WINDOW), lambda i: (i, 0, 0))
    new_spec = pl.BlockSpec((bt, 1, D_KV), lambda i: (i, 0, 0))
    return pl.pallas_call(
        _sample_attn_kernel,
        out_shape=(jax.ShapeDtypeStruct((DEC_BATCH, N_Q_HEADS, HEAD_DIM), F32),
                   jax.ShapeDtypeStruct((DEC_BATCH, WINDOW, D_KV), F32),
                   jax.ShapeDtypeStruct((DEC_BATCH, WINDOW, D_KV), F32)),
        grid=(DEC_BATCH // bt,),
        in_specs=[pl.BlockSpec((bt, N_Q_HEADS, HEAD_DIM), lambda i: (i, 0, 0)),
                  new_spec, new_spec, win_t_spec, win_t_spec,
                  pl.BlockSpec((N_Q_HEADS, 1), lambda i: (0, 0))],
        out_specs=(pl.BlockSpec((bt, N_Q_HEADS, HEAD_DIM), lambda i: (i, 0, 0)), win_spec, win_spec),
        compiler_params=_cparams(("arbitrary",)),
        name="sample_attention",
    )(q, k_new, v_new, cache_k_t, cache_v_t, sinks)


def _head_ones():
    r = _div_pow2(lax.broadcasted_iota(jnp.int32, (LANES, LANES), 0), HEAD_DIM)
    c = _div_pow2(lax.broadcasted_iota(jnp.int32, (LANES, LANES), 1), HEAD_DIM)
    return jnp.where(r == c, 1.0, 0.0).astype(BF16)


def _head_sum(x, ones):
    parts = [_dot_exact_rhs(x[:, p * LANES:(p + 1) * LANES], ones) for p in range(x.shape[1] // LANES)]
    return jnp.concatenate(parts, axis=1)


def _token_mix(feat, shifted, mu):
    return feat + (shifted - feat) * mu


def _rwkv_prep(mixed, mixed_tail, w0, a0, k_k, k_a, wd, wa, wg, ones):
    r = mixed[:, 0:D_RWKV]
    k = mixed[:, D_RWKV:2 * D_RWKV]
    v = mixed[:, 2 * D_RWKV:3 * D_RWKV]
    xw = mixed_tail[:, 0:W_LORA]
    xa = mixed_tail[:, W_LORA:W_LORA + A_LORA]
    xg = mixed_tail[:, W_LORA + A_LORA:D_TAIL]
    w_log = -_softplus(-(w0 + _dot1(jnp.tanh(xw), wd))) - 0.5
    log_decay = -jnp.exp(w_log)
    a = _sigmoid(a0 + _dot1(xa, wa))
    g = _dot1(_sigmoid(xg), wg)
    kk = k * k_k
    kk = kk * lax.rsqrt(jnp.maximum(_head_sum(kk * kk, ones), 1e-24))
    k2 = k * (1.0 + (a - 1.0) * k_a)
    return r, log_decay, k2, v, -kk, kk * a, g


def _rwkv_post(y, r, k2, v, g, r_k, gn_g, gn_b, ones):
    inv_n = 1.0 / HEAD_DIM
    mu = _head_sum(y, ones) * inv_n
    d = y - mu
    var = _head_sum(d * d, ones) * inv_n
    yn = d * lax.rsqrt(var + GN_EPS) * gn_g + gn_b
    bonus = _head_sum(r * k2 * r_k, ones) * v
    return (yn + bonus) * g


(OP_AABS, OP_RABS, OP_AN, OP_RN, OP_BN, OP_KN, OP_BH, OP_KH, OP_V) = range(9)
N_OPS = 9


def _prompt_rwkv_kernel(f1_ref, f2_ref, tail_in_ref, mu_ref, mut_ref, w0_ref, a0_ref, kk_ref, ka_ref, rk_ref,
                        gng_ref, gnb_ref, wd_ref, wa_ref, wg_ref, out_ref, state_ref,
                        prev_ref, prevt_ref, s_ref, ops_ref, pc_ref, y_ref):
    c = pl.program_id(0)
    C = CHUNK

    @pl.when(c == 0)
    def _():
        prev_ref[...] = jnp.zeros_like(prev_ref)
        prevt_ref[...] = jnp.zeros_like(prevt_ref)
        s_ref[...] = jnp.zeros_like(s_ref)

    ones = _head_ones()
    row = lax.broadcasted_iota(jnp.int32, (C, 1), 0)

    def token_shift(feat, carry_ref):
        shifted = jnp.where(row == 0, carry_ref[0:1, :], pltpu.roll(feat, 1, axis=0))
        carry_ref[0:1, :] = feat[C - 1:C, :]
        return shifted

    feat = jnp.concatenate([f1_ref[...], f2_ref[...]], axis=1)
    tail = tail_in_ref[...]
    mixed = _token_mix(feat, token_shift(feat, prev_ref), mu_ref[...])
    mixed_tail = _token_mix(tail, token_shift(tail, prevt_ref), mut_ref[...])
    r, ld, k2, v, av, bv, g = _rwkv_prep(mixed, mixed_tail, w0_ref[...], a0_ref[...], kk_ref[...], ka_ref[...],
                                         wd_ref[...], wa_ref[...], wg_ref[...], ones)

    ti = lax.broadcasted_iota(jnp.int32, (C, C), 0)
    tj = lax.broadcasted_iota(jnp.int32, (C, C), 1)
    tri_incl = jnp.where(tj <= ti, 1.0, 0.0).astype(BF16)
    cs = _dot_exact_lhs(tri_incl, ld)
    cs_ref = cs[C // 2 - 1:C // 2, :]
    cs_end = cs[C - 1:C, :]
    e_prev = jnp.exp(cs - ld)
    e_cur = jnp.exp(cs)
    n_prev = jnp.exp(cs - ld - cs_ref)
    n_cur = jnp.exp(cs - cs_ref)
    n_inv = jnp.exp(cs_ref - cs)
    e_tail = jnp.exp(cs_end - cs)
    ops = {OP_AABS: av * e_prev, OP_RABS: r * e_cur, OP_AN: av * n_prev, OP_RN: r * n_cur,
           OP_BN: bv * n_inv, OP_KN: k2 * n_inv, OP_BH: bv * e_tail, OP_KH: k2 * e_tail, OP_V: v}
    p_end = jnp.exp(cs_end)
    for p in range(N_PAIRS):
        sl = slice(p * LANES, (p + 1) * LANES)
        for idx, val in ops.items():
            ops_ref[p, idx] = val[:, sl]
        pc_ref[p] = jnp.broadcast_to(p_end[:, sl], (SUBLANES, LANES))

    lane1 = lax.broadcasted_iota(jnp.int32, (C, LANES), 1)
    head0 = lane1 < HEAD_DIM
    r2 = lax.broadcasted_iota(jnp.int32, (2 * C, 2 * C), 0)
    c2 = lax.broadcasted_iota(jnp.int32, (2 * C, 2 * C), 1)
    tq = _mod_pow2(r2, C)
    tk = _mod_pow2(c2, C)
    band = (tk < tq) | ((tk == tq) & (r2 >= C))
    blockdiag = _div_pow2(r2, HEAD_DIM) == _div_pow2(c2, HEAD_DIM)

    op = lambda p, idx: ops_ref[p, idx]
    zero_half = jnp.zeros((C, LANES), F32)
    for pairs in [range(g, g + PAIR_GROUP) for g in range(0, N_PAIRS, PAIR_GROUP)]:
        gy = {p: _dot1(jnp.concatenate([op(p, OP_AABS), op(p, OP_RABS)], axis=0), s_ref[p]) for p in pairs}

        am0, am1 = {}, {}
        for p in pairs:
            a_n, r_n = op(p, OP_AN), op(p, OP_RN)
            b0, k0 = jnp.where(head0, op(p, OP_BN), 0.0), jnp.where(head0, op(p, OP_KN), 0.0)
            b1, k1 = jnp.where(head0, 0.0, op(p, OP_BN)), jnp.where(head0, 0.0, op(p, OP_KN))
            am = _dot1(jnp.concatenate([a_n, r_n], axis=0), jnp.concatenate([k0, b0, b1, k1], axis=0), NT)
            am0[p] = jnp.where(band, am[:, 0:2 * C], 0.0)
            am1[p] = jnp.where(band, am[:, 2 * C:4 * C], 0.0)

        w0, w1 = {}, {}
        for p in pairs:
            top0, top1 = am0[p][0:C], am1[p][0:C]
            ak = jnp.concatenate([jnp.where(head0, top0, 0.0), jnp.where(head0, 0.0, top1)], axis=0)
            vv = op(p, OP_V)
            g0 = gy[p][0:C]
            m = jnp.concatenate([g0, g0], axis=0) + _dot1(ak, jnp.concatenate([vv, vv], axis=0))
            w0[p] = jnp.where(head0, m[0:C], top0)
            w1[p] = jnp.where(head0, top1, m[C:2 * C])

        for lvl in range(SOLVE_LEVELS):
            prod0 = {p: _dot1(w0[p], jnp.concatenate([zero_half, w0[p]], axis=0)) for p in pairs}
            prod1 = {p: _dot1(w1[p], jnp.concatenate([w1[p], zero_half], axis=0)) for p in pairs}
            w0 = {p: jnp.where(head0, w0[p] + prod0[p], prod0[p]) for p in pairs}
            w1 = {p: jnp.where(head0, prod1[p], w1[p] + prod1[p]) for p in pairs}
        u = {p: jnp.where(head0, w0[p], w1[p]) for p in pairs}

        for p in pairs:
            vv = op(p, OP_V)
            y_lhs = jnp.concatenate([am0[p][C:2 * C], am1[p][C:2 * C]], axis=1)
            y_rhs = jnp.concatenate([jnp.where(head0, vv, 0.0), jnp.where(head0, u[p], 0.0),
                                     jnp.where(head0, 0.0, u[p]), jnp.where(head0, 0.0, vv)], axis=0)
            y_ref[p] = gy[p][C:2 * C] + _dot1(y_lhs, y_rhs)

        for p in pairs:
            decay_rows = jnp.broadcast_to(pc_ref[p][0:1, :], (LANES, LANES)).T
            upd_lhs = jnp.concatenate([op(p, OP_BH), op(p, OP_KH)], axis=0).T
            upd_rhs = jnp.concatenate([u[p], op(p, OP_V)], axis=0)
            s_ref[p] = s_ref[p] * decay_rows + jnp.where(blockdiag, _dot1(upd_lhs, upd_rhs), 0.0)

    y = jnp.concatenate([y_ref[p] for p in range(N_PAIRS)], axis=1)
    out_ref[...] = _rwkv_post(y, r, k2, v, g, rk_ref[...], gng_ref[...], gnb_ref[...], ones)

    @pl.when(c == pl.num_programs(0) - 1)
    def _():
        state_ref[...] = s_ref[...]


def _prompt_rwkv(h_main, tail, prm):
    n_chunks = SEQ // CHUNK
    half = D_RKV // 2
    assert D_QKV == half
    vec = pl.BlockSpec((1, D_RWKV), lambda c: (0, 0))
    full = lambda a: pl.BlockSpec(a.shape, lambda c: (0,) * a.ndim)
    return pl.pallas_call(
        _prompt_rwkv_kernel,
        out_shape=(jax.ShapeDtypeStruct((SEQ, D_RWKV), F32),
                   jax.ShapeDtypeStruct((N_PAIRS, LANES, LANES), F32)),
        grid=(n_chunks,),
        in_specs=[pl.BlockSpec((CHUNK, half), lambda c: (c, 1)),
                  pl.BlockSpec((CHUNK, half), lambda c: (c, 2)),
                  pl.BlockSpec((CHUNK, D_TAIL), lambda c: (c, 0)),
                  full(prm["mu"]), full(prm["mu_tail"]),
                  vec, vec, vec, vec, vec, vec, vec,
                  full(prm["wd"]), full(prm["wa"]), full(prm["wg"])],
        out_specs=(pl.BlockSpec((CHUNK, D_RWKV), lambda c: (c, 0)),
                   pl.BlockSpec((N_PAIRS, LANES, LANES), lambda c: (0, 0, 0))),
        scratch_shapes=[pltpu.VMEM((SUBLANES, D_RKV), F32),
                        pltpu.VMEM((SUBLANES, D_TAIL), F32),
                        pltpu.VMEM((N_PAIRS, LANES, LANES), F32),
                        pltpu.VMEM((N_PAIRS, N_OPS, CHUNK, LANES), F32),
                        pltpu.VMEM((N_PAIRS, SUBLANES, LANES), F32),
                        pltpu.VMEM((N_PAIRS, CHUNK, LANES), F32)],
        compiler_params=_cparams(("arbitrary",)),
        name="prompt_rwkv",
    )(h_main, h_main, tail, prm["mu"], prm["mu_tail"], prm["w0"], prm["a0"], prm["k_k"], prm["k_a"],
      prm["r_k"], prm["gn_g"], prm["gn_b"], prm["wd"], prm["wa"], prm["wg"])


def _sample_prep_kernel(h_ref, x_ref, wt_ref, shift_ref, mu_ref, mut_ref, w0_ref, a0_ref, kk_ref, ka_ref,
                        wd_ref, wa_ref, wg_ref, r_ref, k_ref, v_ref, g_ref, tail_ref,
                        rt_ref, wtr_ref, kt_ref, vt_ref, at_ref, bt_ref):
    ones = _head_ones()
    feat = h_ref[:, D_QKV:D_MAIN]
    tail = _dot1(x_ref[...], wt_ref[...], NT)
    tail_ref[...] = tail
    mixed = _token_mix(feat, shift_ref[:, 0:D_RKV], mu_ref[...])
    mixed_tail = _token_mix(tail, shift_ref[:, D_RKV:D_SHIFT], mut_ref[...])
    r, ld, k2, v, av, bv, g = _rwkv_prep(mixed, mixed_tail, w0_ref[...], a0_ref[...], kk_ref[...], ka_ref[...],
                                         wd_ref[...], wa_ref[...], wg_ref[...], ones)
    r_ref[...] = r
    k_ref[...] = k2
    v_ref[...] = v
    g_ref[...] = g
    rt_ref[...] = r.T
    wtr_ref[...] = jnp.exp(ld).T
    kt_ref[...] = k2.T
    vt_ref[...] = v.T
    at_ref[...] = av.T
    bt_ref[...] = bv.T


def _sample_prep(h_main, x, shift, prm):
    tok = jax.ShapeDtypeStruct((DEC_BATCH, D_RWKV), F32)
    chan = jax.ShapeDtypeStruct((D_RWKV, DEC_BATCH), F32)
    return pl.pallas_call(
        _sample_prep_kernel,
        out_shape=(tok,) * 4 + (jax.ShapeDtypeStruct((DEC_BATCH, D_TAIL), F32),) + (chan,) * 6,
        compiler_params=pltpu.CompilerParams(vmem_limit_bytes=VMEM_LIMIT),
        name="sample_rwkv_prep",
    )(h_main, x, prm["w_tail"], shift, prm["mu"], prm["mu_tail"], prm["w0"], prm["a0"], prm["k_k"], prm["k_a"],
      prm["wd"], prm["wa"], prm["wg"])


STEP_GROUP = 4


def _sample_step_kernel(s_ref, r_ref, w_ref, k_ref, a_ref, b_ref, v_ref, y_ref, snew_ref):
    r, w, k, a, b = r_ref[...], w_ref[...], k_ref[...], a_ref[...], b_ref[...]
    for g0 in range(0, HEAD_DIM, 2 * STEP_GROUP):
        chans = range(g0, g0 + 2 * STEP_GROUP)
        sa = {i: jnp.sum(s_ref[0, i] * a, axis=0, keepdims=True) for i in chans}
        s_new = {i: s_ref[0, i] * w + sa[i] * b + v_ref[i:i + 1, :] * k for i in chans}
        for i in chans:
            y_ref[i:i + 1, :] = jnp.sum(s_new[i] * r, axis=0, keepdims=True)
        for i in range(g0, g0 + 2 * STEP_GROUP, 2):
            pair = jnp.concatenate([s_new[i], s_new[i + 1]], axis=0)
            snew_ref[:, i * HEAD_DIM:(i + 2) * HEAD_DIM] = pair.T


def _sample_step(state_t, r_t, w_t, k_t, a_t, b_t, v_t):
    head_rows = pl.BlockSpec((HEAD_DIM, DEC_BATCH), lambda h: (h, 0))
    return pl.pallas_call(
        _sample_step_kernel,
        out_shape=(jax.ShapeDtypeStruct((D_RWKV, DEC_BATCH), F32),
                   jax.ShapeDtypeStruct((DEC_BATCH, N_RWKV_HEADS * HEAD_DIM * HEAD_DIM), F32)),
        grid=(N_RWKV_HEADS,),
        in_specs=[pl.BlockSpec((1, HEAD_DIM, HEAD_DIM, DEC_BATCH), lambda h: (h, 0, 0, 0))] + [head_rows] * 6,
        out_specs=(head_rows, pl.BlockSpec((DEC_BATCH, HEAD_DIM * HEAD_DIM), lambda h: (0, h))),
        compiler_params=_cparams(("arbitrary",)),
        name="sample_rwkv_step",
    )(state_t, r_t, w_t, k_t, a_t, b_t, v_t)


def _sample_post_kernel(yt_ref, r_ref, k_ref, v_ref, g_ref, rk_ref, gng_ref, gnb_ref, o_ref):
    o_ref[...] = _rwkv_post(yt_ref[...].T, r_ref[...], k_ref[...], v_ref[...], g_ref[...], rk_ref[...],
                            gng_ref[...], gnb_ref[...], _head_ones())


def _sample_post(y, r, k, v, g, prm):
    return pl.pallas_call(
        _sample_post_kernel,
        out_shape=jax.ShapeDtypeStruct((DEC_BATCH, D_RWKV), F32),
        compiler_params=pltpu.CompilerParams(vmem_limit_bytes=VMEM_LIMIT),
        name="sample_rwkv_post",
    )(y, r, k, v, g, prm["r_k"], prm["gn_g"], prm["gn_b"])


def _project_mix(attn_ref, rwkv_ref, wo_ref):
    return (_dot(attn_ref[...].astype(BF16), wo_ref[0:D_ATTN, :])
            + _dot(rwkv_ref[...].astype(BF16), wo_ref[D_ATTN:D_ATTN + D_RWKV, :]))


def _norm_and_route(mix, x_ref, g_ref, b_ref, wr_ref, br_ref, x1_ref, x1b_ref, route_ref):
    x1 = _layer_norm(ALPHA * x_ref[...] + mix, g_ref[...], b_ref[...])
    x1_ref[...] = x1
    x1b = x1.astype(BF16)
    x1b_ref[...] = _pack_bf16_halves(x1b)
    logits = _dot(x1b, wr_ref[...].astype(BF16)) + br_ref[...]
    tm = logits.shape[0]
    lane = lax.broadcasted_iota(jnp.int32, (tm, LANES), 1).astype(F32)
    big = float(2 * LANES)
    neg = -jnp.inf
    lc = jnp.where(lane < N_GROUPS, logits, neg)
    mc = jnp.max(lc, axis=-1, keepdims=True)
    g_sel = jnp.min(jnp.where(lc == mc, lane, big), axis=-1, keepdims=True)
    p_group = 1.0 / jnp.sum(jnp.exp(lc - mc), axis=-1, keepdims=True)
    lo = ROUTE_FINE_OFF + g_sel * EXPERTS_PER_GROUP
    lf = jnp.where((lane >= lo) & (lane < lo + EXPERTS_PER_GROUP), logits, neg)
    v1 = jnp.max(lf, axis=-1, keepdims=True)
    i1 = jnp.min(jnp.where(lf == v1, lane, big), axis=-1, keepdims=True)
    lf2 = jnp.where(lane == i1, neg, lf)
    v2 = jnp.max(lf2, axis=-1, keepdims=True)
    i2 = jnp.min(jnp.where(lf2 == v2, lane, big), axis=-1, keepdims=True)
    e21 = jnp.exp(v2 - v1)
    gate1 = p_group / (1.0 + e21)
    gate2 = p_group * e21 / (1.0 + e21)
    route = jnp.where(lane == 0, i1 - ROUTE_FINE_OFF,
                      jnp.where(lane == 1, i2 - ROUTE_FINE_OFF,
                                jnp.where(lane == 2, gate1, jnp.where(lane == 3, gate2, 0.0))))
    route_ref[...] = route


N_ROUTER_OUTS = 3


def _outproj_router_kernel(n_tiles, n_aliased, attn_ref, rwkv_ref, x_ref, wo_ref, g_ref, b_ref, wr_ref, br_ref,
                           *rest):
    outs = rest[n_aliased:n_aliased + N_ROUTER_OUTS]
    mix_ref = rest[-1]
    i = pl.program_id(0)
    finish = lambda mix: _norm_and_route(mix, x_ref, g_ref, b_ref, wr_ref, br_ref, *outs)

    @pl.when(i == 0)
    def _():
        mix_ref[...] = _project_mix(attn_ref, rwkv_ref, wo_ref)

    @pl.when((i >= 1) & (i < n_tiles))
    def _():
        finish(mix_ref[...])
        mix_ref[...] = _project_mix(attn_ref, rwkv_ref, wo_ref)

    @pl.when(i == n_tiles)
    def _():
        finish(mix_ref[...])

    @pl.when(i > n_tiles)
    def _():
        for out_ref in outs:
            out_ref[...] = jnp.zeros_like(out_ref)


def _outproj_router(attn, rwkv, x, wo_bf16, ln_g, ln_b, w_route, b_route, tm, n_total, row_block, into, name):
    m = x.shape[0]
    n_tiles = m // tm
    const = lambda shape: pl.BlockSpec(shape, lambda i: (0, 0))
    ahead = lambda width: pl.BlockSpec((tm, width), lambda i: (jnp.minimum(i, n_tiles - 1), 0))
    behind = lambda width: pl.BlockSpec((tm, width), lambda i: (jnp.clip(i - 1, 0, n_tiles - 1), 0))
    in_specs = [ahead(D_ATTN), ahead(D_RWKV), behind(D_MODEL),
                const((D_MODEL, D_MODEL)), const((1, D_MODEL)), const((1, D_MODEL)),
                const((D_MODEL, LANES)), const((1, LANES))]
    args = [attn, rwkv, x, wo_bf16, ln_g, ln_b, w_route, b_route]
    aliases, n_aliased, fill_steps = {}, 0, pl.cdiv(n_total - m, tm)
    if into is not None:
        n_aliased, fill_steps = N_ROUTER_OUTS, 0
        in_specs += [pl.BlockSpec(memory_space=pl.ANY)] * N_ROUTER_OUTS
        aliases = {len(args) + k: k for k in range(N_ROUTER_OUTS)}
        args += list(into)
    out_rows = lambda width: pl.BlockSpec((tm, width), lambda i: (jnp.maximum(i - 1, 0) + row_block, 0))
    return pl.pallas_call(
        functools.partial(_outproj_router_kernel, n_tiles, n_aliased),
        out_shape=(jax.ShapeDtypeStruct((n_total, D_MODEL), F32),
                   jax.ShapeDtypeStruct((n_total, D_MODEL // 2), jnp.uint32),
                   jax.ShapeDtypeStruct((n_total, LANES), F32)),
        grid=(n_tiles + 1 + fill_steps,),
        in_specs=in_specs,
        out_specs=(out_rows(D_MODEL), out_rows(D_MODEL // 2), out_rows(LANES)),
        scratch_shapes=[pltpu.VMEM((tm, D_MODEL), F32)],
        input_output_aliases=aliases,
        compiler_params=_cparams(("arbitrary",)),
        name=name,
    )(*args)


DISPATCH_TILE = 128


def _dispatch_kernel(zoff_ref, dest_ref, x_ref, o_hbm, zbuf, ring, zsem, sem):
    i = pl.program_id(0)
    n_blocks = o_hbm.shape[0] // MOE_BLOCK
    n_used = zoff_ref[N_EXPERTS]

    def zero_fill(start_row):
        start_row = pl.multiple_of(start_row, MOE_BLOCK)
        return pltpu.make_async_copy(zbuf, o_hbm.at[pl.ds(start_row, MOE_BLOCK)], zsem)

    def zero_fills(action):
        for e in range(N_EXPERTS):
            @pl.when(zoff_ref[e] >= 0)
            def _():
                action(zero_fill(zoff_ref[e]))
        for b in range(n_blocks):
            @pl.when(b >= n_used)
            def _():
                action(zero_fill(b * MOE_BLOCK))

    @pl.when(i == 0)
    def _():
        zbuf[...] = jnp.zeros_like(zbuf)
        zero_fills(lambda copy: copy.start())
        zero_fills(lambda copy: copy.wait())

    cur = lax.rem(i, 2)

    def wait_rows(slot):
        for k in range(2):
            pltpu.make_async_copy(ring.at[slot], o_hbm.at[pl.ds(0, DISPATCH_TILE)], sem.at[slot]).wait()

    @pl.when(i >= 2)
    def _():
        wait_rows(cur)

    ring[cur] = x_ref[...]
    for t in range(DISPATCH_TILE):
        for k in range(2):
            pltpu.make_async_copy(ring.at[cur, pl.ds(t, 1)], o_hbm.at[pl.ds(dest_ref[0, 0, 2 * t + k], 1)],
                                  sem.at[cur]).start()

    @pl.when(i == pl.num_programs(0) - 1)
    def _():
        wait_rows(cur)

        @pl.when(i >= 1)
        def _():
            wait_rows(1 - cur)


def _dispatch(zero_offsets, dest, x_packed, n_blocks):
    n_tokens, width = x_packed.shape
    grid_spec = pltpu.PrefetchScalarGridSpec(
        num_scalar_prefetch=1,
        grid=(n_tokens // DISPATCH_TILE,),
        in_specs=[pl.BlockSpec((1, 1, 2 * DISPATCH_TILE), lambda i, z: (i, 0, 0), memory_space=pltpu.SMEM),
                  pl.BlockSpec((DISPATCH_TILE, width), lambda i, z: (i, 0))],
        out_specs=pl.BlockSpec(memory_space=pl.ANY),
        scratch_shapes=[pltpu.VMEM((MOE_BLOCK, width), x_packed.dtype),
                        pltpu.VMEM((2, DISPATCH_TILE, width), x_packed.dtype),
                        pltpu.SemaphoreType.DMA, pltpu.SemaphoreType.DMA((2,))],
    )
    return pl.pallas_call(
        _dispatch_kernel,
        out_shape=jax.ShapeDtypeStruct((n_blocks * MOE_BLOCK, width), x_packed.dtype),
        grid_spec=grid_spec,
        compiler_params=_cparams(("arbitrary",)),
        name="moe_dispatch",
    )(zero_offsets, dest.reshape(-1, 1, 2 * DISPATCH_TILE), x_packed)


def _expert_kernel(be_ref, nb_ref, x_ref, wg_hbm, wu_hbm, wd_hbm, o_ref, wg_buf, wu_buf, wd_buf, slot_ref, sem):
    blk = pl.program_id(0)
    n_used = nb_ref[0]
    expert = be_ref[blk]
    is_first = (blk == 0) | (be_ref[jnp.maximum(blk - 1, 0)] != expert)

    def fetch(e, slot):
        return [pltpu.make_async_copy(hbm.at[e], buf.at[slot], sem.at[slot, i])
                for i, (hbm, buf) in enumerate(((wg_hbm, wg_buf), (wu_hbm, wu_buf), (wd_hbm, wd_buf)))]

    @pl.when((blk < n_used) & is_first)
    def _():
        @pl.when(blk == 0)
        def _():
            slot_ref[0] = 1
            for copy in fetch(expert, 0):
                copy.start()

        slot = 1 - slot_ref[0]
        slot_ref[0] = slot
        for copy in fetch(expert, slot):
            copy.wait()
        nxt = lax.while_loop(lambda j: (j < n_used) & (be_ref[jnp.minimum(j, n_used - 1)] == expert),
                             lambda j: j + 1, blk + 1)

        @pl.when(nxt < n_used)
        def _():
            for copy in fetch(be_ref[jnp.minimum(nxt, n_used - 1)], 1 - slot):
                copy.start()

    @pl.when(blk < n_used)
    def _():
        slot = slot_ref[0]
        half = D_MODEL // 2
        x_head, x_tail = _unpack_bf16_halves(x_ref[...])
        proj = lambda w_buf: (_dot(x_head, w_buf[slot, 0:half, :].astype(BF16))
                              + _dot(x_tail, w_buf[slot, half:D_MODEL, :].astype(BF16)))
        gate = proj(wg_buf)
        up = proj(wu_buf)
        h = gate * _sigmoid(gate) * up
        o_ref[...] = _dot(h.astype(BF16), wd_buf[slot].astype(BF16))

    @pl.when(blk >= n_used)
    def _():
        o_ref[...] = jnp.zeros_like(o_ref)


def _expert_mlp(block_expert, n_used, x_sorted, w_gate, w_up, w_down, n_blocks):
    grid_spec = pltpu.PrefetchScalarGridSpec(
        num_scalar_prefetch=2,
        grid=(n_blocks,),
        in_specs=[pl.BlockSpec((MOE_BLOCK, D_MODEL // 2), lambda b, be, nb: (jnp.minimum(b, nb[0] - 1), 0)),
                  pl.BlockSpec(memory_space=pl.ANY), pl.BlockSpec(memory_space=pl.ANY),
                  pl.BlockSpec(memory_space=pl.ANY)],
        out_specs=pl.BlockSpec((MOE_BLOCK, D_MODEL), lambda b, be, nb: (b, 0)),
        scratch_shapes=[pltpu.VMEM((2, D_MODEL, D_EXPERT), F32), pltpu.VMEM((2, D_MODEL, D_EXPERT), F32),
                        pltpu.VMEM((2, D_EXPERT, D_MODEL), F32), pltpu.SMEM((1,), jnp.int32),
                        pltpu.SemaphoreType.DMA((2, 3))],
    )
    return pl.pallas_call(
        _expert_kernel,
        out_shape=jax.ShapeDtypeStruct((n_blocks * MOE_BLOCK, D_MODEL), F32),
        grid_spec=grid_spec,
        compiler_params=_cparams(("arbitrary",)),
        name="expert_mlp",
    )(block_expert, n_used, x_sorted, w_gate, w_up, w_down)


COMBINE_TILE = 256


def _combine_kernel(dest_ref, dest_next_ref, y_hbm, x1_ref, route_ref, g_ref, b_ref, o_ref, ybuf, sem):
    i = pl.program_id(0)
    cur = lax.rem(i, 2)
    tile = x1_ref.shape[0]
    n_rows = 2 * tile

    def gather(table_ref, buf):
        for slot in range(n_rows):
            pltpu.make_async_copy(y_hbm.at[pl.ds(table_ref[0, 0, slot], 1)], ybuf.at[buf, pl.ds(slot, 1)],
                                  sem.at[buf]).start()

    def wait_gather(buf):
        pltpu.make_async_copy(y_hbm.at[pl.ds(0, n_rows)], ybuf.at[buf], sem.at[buf]).wait()

    @pl.when(i == 0)
    def _():
        gather(dest_ref, 0)

    gather(dest_next_ref, 1 - cur)
    wait_gather(cur)
    route = route_ref[...]
    yb = ybuf[cur]
    moe = route[:, 2:3] * yb[0:tile, :] + route[:, 3:4] * yb[tile:n_rows, :]
    o_ref[...] = _layer_norm(ALPHA * x1_ref[...] + moe, g_ref[...], b_ref[...])

    @pl.when(i == pl.num_programs(0) - 1)
    def _():
        wait_gather(1 - cur)


def _combine(dest, y_slots, x1_all, route_all, m, tm, row_block, ln_g, ln_b, name):
    return pl.pallas_call(
        _combine_kernel,
        out_shape=jax.ShapeDtypeStruct((m, D_MODEL), F32),
        grid=(m // tm,),
        in_specs=[pl.BlockSpec((1, 1, 2 * tm), lambda i: (i, 0, 0), memory_space=pltpu.SMEM),
                  pl.BlockSpec((1, 1, 2 * tm), lambda i: (i + 1, 0, 0), memory_space=pltpu.SMEM),
                  pl.BlockSpec(memory_space=pl.ANY),
                  pl.BlockSpec((tm, D_MODEL), lambda i: (i + row_block, 0)),
                  pl.BlockSpec((tm, LANES), lambda i: (i + row_block, 0)),
                  pl.BlockSpec((1, D_MODEL), lambda i: (0, 0)),
                  pl.BlockSpec((1, D_MODEL), lambda i: (0, 0))],
        out_specs=pl.BlockSpec((tm, D_MODEL), lambda i: (i, 0)),
        scratch_shapes=[pltpu.VMEM((2, 2 * tm, D_MODEL), F32), pltpu.SemaphoreType.DMA((2,))],
        compiler_params=_cparams(("arbitrary",)),
        name=name,
    )(dest, dest, y_slots, x1_all, route_all, ln_g, ln_b)


def _dispatch_plan(route_all, n_blocks):
    flat_e = route_all[:, 0:2].astype(jnp.int32).reshape(-1)
    onehot = (flat_e[:, None] == jnp.arange(N_EXPERTS, dtype=jnp.int32)[None, :]).astype(jnp.int32)
    csum = jnp.cumsum(onehot, axis=0)
    rank = jnp.sum(onehot * csum, axis=1) - 1
    counts = csum[-1]
    padded = (counts + MOE_BLOCK - 1) // MOE_BLOCK * MOE_BLOCK
    pend = jnp.cumsum(padded)
    pstart = pend - padded
    dest = (pstart[flat_e] + rank).astype(jnp.int32)
    zero_offsets = jnp.where(counts > 0, pend - MOE_BLOCK, -1).astype(jnp.int32)
    n_used = (pend[-1] // MOE_BLOCK).astype(jnp.int32)
    block_start = jnp.minimum(jnp.arange(n_blocks, dtype=jnp.int32), n_used - 1) * MOE_BLOCK
    block_e = jnp.minimum(jnp.searchsorted(pend, block_start, side="right"), N_EXPERTS - 1).astype(jnp.int32)
    return dest, jnp.concatenate([zero_offsets, n_used.reshape(1)]), block_e, n_used.reshape(1)


def kernel(x_prompt, x_sample, cache_k_win, cache_v_win, state_wkv, state_shift, w_in, attn_sinks, shift_mu, w0,
           w_decay_up, a0, w_a_up, w_g_up, k_k, k_a, r_k, gn_g, gn_b, w_out, ln1_g, ln1_b, w_coarse, b_coarse,
           w_fine, b_fine, w_exp_gate, w_exp_up, w_exp_down, ln2_g, ln2_b):
    xp = x_prompt[0]
    xs = x_sample[:, 0]
    row = lambda a: a.reshape(1, -1)

    w_in_t = jnp.swapaxes(w_in[0], 0, 1)
    prm = dict(mu=row(shift_mu[0, :D_RKV]), mu_tail=row(shift_mu[0, D_RKV:]), w_tail=w_in_t[D_MAIN:].astype(BF16),
               w0=row(w0[0]), a0=row(a0[0]), k_k=row(k_k[0]), k_a=row(k_a[0]),
               r_k=row(r_k[0]), gn_g=row(gn_g[0]), gn_b=row(gn_b[0]),
               wd=w_decay_up[0], wa=w_a_up[0], wg=w_g_up[0])
    sinks = attn_sinks[0]
    wo_bf16 = w_out[0].astype(BF16)
    w_route = jnp.pad(jnp.concatenate([w_coarse[0], w_fine[0]], axis=1), ((0, 0), (0, LANES - N_GROUPS - N_EXPERTS)))
    b_route = jnp.pad(jnp.concatenate([b_coarse[0], b_fine[0]]), (0, LANES - N_GROUPS - N_EXPERTS)).reshape(1, LANES)

    hp = _matmul(xp, w_in_t, D_MAIN, MAIN_TM, MAIN_TN, "in_proj_prompt")
    tail_p = _matmul(xp, prm["w_tail"], D_TAIL, TAIL_TM, D_TAIL, "in_proj_prompt_tail")
    hs = _matmul(xs, w_in_t, D_MAIN, DEC_BATCH, MAIN_TN, "in_proj_sample")

    attn_p = _prompt_attention(hp, sinks)
    rwkv_p, state_p = _prompt_rwkv(hp, tail_p, prm)

    q_s = hs[:, :D_ATTN].reshape(DEC_BATCH, N_Q_HEADS, HEAD_DIM)
    k_s = hs[:, D_ATTN:D_ATTN + D_KV].reshape(DEC_BATCH, 1, D_KV)
    v_s = hs[:, D_ATTN + D_KV:D_QKV].reshape(DEC_BATCH, 1, D_KV)
    window_t = lambda c: jnp.transpose(c, (0, 2, 3, 1)).reshape(DEC_BATCH, D_KV, WINDOW)
    attn_s, kwin_s, vwin_s = _sample_attention(
        q_s, k_s, v_s, window_t(cache_k_win[0]), window_t(cache_v_win[0]), sinks.reshape(N_Q_HEADS, 1))
    r_s, k2_s, vv_s, g_s, tail_s, r_t, w_t, k_t, v_t, a_t, b_t = _sample_prep(hs, xs, state_shift[0], prm)
    y_t, state_s = _sample_step(jnp.transpose(state_wkv[0], (1, 2, 3, 0)), r_t, w_t, k_t, a_t, b_t, v_t)
    state_s = state_s.reshape(DEC_BATCH, N_RWKV_HEADS, HEAD_DIM, HEAD_DIM)
    rwkv_s = _sample_post(y_t, r_s, k2_s, vv_s, g_s, prm)

    n_tokens = SEQ + DEC_BATCH
    outs_pr = _outproj_router(attn_p, rwkv_p, xp, wo_bf16, row(ln1_g[0]), row(ln1_b[0]), w_route, b_route,
                              256, n_tokens, 0, None, "outproj_router_prompt")
    x1_all, x1b_all, route_all = _outproj_router(attn_s.reshape(DEC_BATCH, D_ATTN), rwkv_s, xs,
                                                 wo_bf16, row(ln1_g[0]), row(ln1_b[0]), w_route, b_route,
                                                 DEC_BATCH, n_tokens, SEQ // DEC_BATCH, outs_pr,
                                                 "outproj_router_sample")

    n_assign = 2 * n_tokens
    n_blocks = -(-(n_assign + N_EXPERTS * (MOE_BLOCK - 1)) // MOE_BLOCK)
    dest, zero_offsets, block_e, n_used = _dispatch_plan(route_all, n_blocks)
    x_sorted = _dispatch(zero_offsets, dest, x1b_all, n_blocks)
    y_slots = _expert_mlp(block_e, n_used, x_sorted, w_exp_gate[0], w_exp_up[0], w_exp_down[0], n_blocks)

    def dest_tiles(d, tile):
        d = d.reshape(-1, tile, 2)
        d = jnp.concatenate([d[:, :, 0], d[:, :, 1]], axis=1)
        return jnp.pad(d, ((0, 1), (0, 0))).reshape(-1, 1, 2 * tile)

    y_p = _combine(dest_tiles(dest[:2 * SEQ], COMBINE_TILE), y_slots, x1_all, route_all, SEQ, COMBINE_TILE, 0,
                   row(ln2_g[0]), row(ln2_b[0]), "combine_prompt")
    y_s = _combine(dest_tiles(dest[2 * SEQ:], DEC_BATCH), y_slots, x1_all, route_all, DEC_BATCH, DEC_BATCH,
                   SEQ // DEC_BATCH, row(ln2_g[0]), row(ln2_b[0]), "combine_sample")

    kv4 = lambda a: a.reshape(a.shape[0], N_KV_HEADS, HEAD_DIM)
    k_win_p = kv4(hp[SEQ - WINDOW:, D_ATTN:D_ATTN + D_KV])[None, None]
    v_win_p = kv4(hp[SEQ - WINDOW:, D_ATTN + D_KV:D_QKV])[None, None]
    sp = state_p.reshape(N_PAIRS, HEADS_PER_TILE, HEAD_DIM, HEADS_PER_TILE, HEAD_DIM)
    wkv_p = jnp.stack([sp[:, i, :, i, :] for i in range(HEADS_PER_TILE)], axis=1)
    wkv_p = wkv_p.reshape(N_RWKV_HEADS, HEAD_DIM, HEAD_DIM).transpose(0, 2, 1)[None, None]
    shift_p = jnp.concatenate([hp[SEQ - 1:SEQ, D_QKV:], tail_p[SEQ - 1:SEQ]], axis=1)[None]
    shift_s = jnp.concatenate([hs[:, D_QKV:], tail_s], axis=1)[None]
    return (y_p[None], y_s[:, None, :], k_win_p, v_win_p, wkv_p, shift_p,
            kwin_s.reshape(1, DEC_BATCH, WINDOW, N_KV_HEADS, HEAD_DIM),
            vwin_s.reshape(1, DEC_BATCH, WINDOW, N_KV_HEADS, HEAD_DIM),
            state_s[None], shift_s)
```

```python
import functools
import math

import jax
import jax.numpy as jnp
from jax import lax
from jax.experimental import pallas as pl
from jax.experimental.pallas import tpu as pltpu

F32 = jnp.float32
BF16 = jnp.bfloat16

D_MODEL = 2048
SEQ = 8192
DEC_BATCH = 128
HEAD_DIM = 64
D_ATTN = 1024
D_RWKV = 1024
N_Q_HEADS = 16
N_KV_HEADS = 4
Q_PER_KV = 4
D_KV = 256
WINDOW = 128
ATTN_SCALE = HEAD_DIM ** -0.5
N_RWKV_HEADS = 16
W_LORA = 64
A_LORA = 64
G_LORA = 160
D_SHIFT = 3 * D_RWKV + W_LORA + A_LORA + G_LORA
D_QKV = D_ATTN + 2 * D_KV
N_GROUPS = 4
EXPERTS_PER_GROUP = 8
N_EXPERTS = 32
D_EXPERT = 512
ALPHA = 2.0 ** 0.25
LN_EPS = 1e-5
GN_EPS = 64e-5

SUBLANES = 8
LANES = 128
VMEM_LIMIT = 52 * 1024 * 1024

D_RKV = 3 * D_RWKV
D_TAIL = W_LORA + A_LORA + G_LORA
D_MAIN = D_QKV + D_RKV
MAIN_TN = 1536
MAIN_TM = 512
TAIL_TM = 1024
OUTPROJ_TM = 256

CHUNK = 64
HEADS_PER_TILE = LANES // HEAD_DIM
N_PAIRS = N_RWKV_HEADS // HEADS_PER_TILE
SOLVE_LEVELS = int(math.log2(CHUNK))
PAIR_GROUP = 8

MOE_BLOCK = 256
ROUTE_FINE_OFF = N_GROUPS

NN = (((1,), (0,)), ((), ()))
NT = (((1,), (1,)), ((), ()))


def _dot(a, b, dims=NN):
    return lax.dot_general(a, b, dims, preferred_element_type=F32)


def _dot1(a, b, dims=NN):
    return _dot(a.astype(BF16), b.astype(BF16), dims)


def _split(x):
    hi = x.astype(BF16)
    lo = (x - hi.astype(F32)).astype(BF16)
    return hi, lo


def _dot_exact_lhs(a_bf16, b, dims=NN):
    bh, bl = _split(b)
    return _dot(a_bf16, bh, dims) + _dot(a_bf16, bl, dims)


def _dot_exact_rhs(a, b_bf16, dims=NN):
    ah, al = _split(a)
    return _dot(ah, b_bf16, dims) + _dot(al, b_bf16, dims)


def _div_pow2(x, d):
    return lax.shift_right_logical(x, jnp.int32(int(math.log2(d))))


def _mod_pow2(x, d):
    return lax.bitwise_and(x, jnp.int32(d - 1))


def _pack_bf16_halves(x_bf16):
    n = x_bf16.shape[1] // 2
    bits = lax.bitcast_convert_type(x_bf16.astype(F32), jnp.uint32)
    return lax.bitwise_or(bits[:, 0:n], lax.shift_right_logical(bits[:, n:2 * n], jnp.uint32(16)))


def _unpack_bf16_halves(packed):
    hi = lax.bitcast_convert_type(lax.bitwise_and(packed, jnp.uint32(0xFFFF0000)), F32)
    lo = lax.bitcast_convert_type(lax.shift_left(packed, jnp.uint32(16)), F32)
    return hi.astype(BF16), lo.astype(BF16)


def _sigmoid(x):
    return 1.0 / (1.0 + jnp.exp(-x))


def _softplus(x):
    return jnp.maximum(x, 0.0) + jnp.log(1.0 + jnp.exp(-jnp.abs(x)))


def _layer_norm(z, g, b):
    mu = jnp.mean(z, axis=-1, keepdims=True)
    d = z - mu
    var = jnp.mean(d * d, axis=-1, keepdims=True)
    return d * lax.rsqrt(var + LN_EPS) * g + b


def _cparams(sem):
    return pltpu.CompilerParams(dimension_semantics=sem, vmem_limit_bytes=VMEM_LIMIT)


def _matmul_kernel(x_ref, wt_ref, o_ref):
    o_ref[...] = _dot(x_ref[...].astype(BF16), wt_ref[...].astype(BF16), NT)


def _matmul(x, w_t, n_out, tm, tn, name):
    m, k = x.shape
    tm = min(tm, m)
    return pl.pallas_call(
        _matmul_kernel,
        out_shape=jax.ShapeDtypeStruct((m, n_out), F32),
        grid=(n_out // tn, m // tm),
        in_specs=[pl.BlockSpec((tm, k), lambda j, i: (i, 0)),
                  pl.BlockSpec((tn, k), lambda j, i: (j, 0))],
        out_specs=pl.BlockSpec((tm, tn), lambda j, i: (i, j)),
        compiler_params=_cparams(("arbitrary", "arbitrary")),
        name=name,
    )(x, w_t)


def _band_bias():
    qi = jnp.arange(Q_PER_KV * WINDOW)[:, None] % WINDOW
    kj = jnp.arange(2 * WINDOW)[None, :]
    diff = qi + WINDOW - kj
    band = (diff >= 0) & (diff <= WINDOW)
    keep = jnp.stack([band & (kj >= WINDOW), band])
    return jnp.where(keep, 0.0, -jnp.inf).astype(F32)


def _prompt_attn_kernel(q_ref, kvp_ref, kvc_ref, bias_ref, sink_ref, o_ref):
    q = q_ref[...]
    kv_prev = kvp_ref[...]
    kv_cur = kvc_ref[...]
    bias = bias_ref[0]
    row_head = _div_pow2(lax.broadcasted_iota(jnp.int32, (Q_PER_KV * WINDOW, 1), 0), WINDOW)
    groups = range(N_KV_HEADS)
    kv_cols = lambda off, g: jnp.concatenate([kv_prev[:, off + g * HEAD_DIM:off + (g + 1) * HEAD_DIM],
                                              kv_cur[:, off + g * HEAD_DIM:off + (g + 1) * HEAD_DIM]],
                                             axis=0).astype(BF16)
    q_rows = lambda g: jnp.concatenate(
        [q[:, (g * Q_PER_KV + h) * HEAD_DIM:(g * Q_PER_KV + h + 1) * HEAD_DIM] for h in range(Q_PER_KV)],
        axis=0).astype(BF16)
    s = [_dot(q_rows(g), kv_cols(0, g), NT) * ATTN_SCALE + bias for g in groups]
    sink = []
    for g in groups:
        col = jnp.zeros((Q_PER_KV * WINDOW, 1), F32)
        for h in range(Q_PER_KV):
            col = jnp.where(row_head == h, sink_ref[g * Q_PER_KV + h], col)
        sink.append(col)
    m = [jnp.maximum(jnp.max(s[g], axis=-1, keepdims=True), sink[g]) for g in groups]
    p = [jnp.exp(s[g] - m[g]) for g in groups]
    denom = [jnp.sum(p[g], axis=-1, keepdims=True) + jnp.exp(sink[g] - m[g]) for g in groups]
    o = [_dot((p[g] / denom[g]).astype(BF16), kv_cols(D_KV, g)) for g in groups]
    o_ref[...] = jnp.concatenate([o[g][h * WINDOW:(h + 1) * WINDOW, :] for g in groups for h in range(Q_PER_KV)],
                                 axis=1)


def _prompt_attention(h_attn, sinks):
    nb = SEQ // WINDOW
    return pl.pallas_call(
        _prompt_attn_kernel,
        out_shape=jax.ShapeDtypeStruct((SEQ, D_ATTN), F32),
        grid=(nb,),
        in_specs=[pl.BlockSpec((WINDOW, D_ATTN), lambda i: (i, 0)),
                  pl.BlockSpec((WINDOW, 2 * D_KV), lambda i: (jnp.maximum(i - 1, 0), 2)),
                  pl.BlockSpec((WINDOW, 2 * D_KV), lambda i: (i, 2)),
                  pl.BlockSpec((1, Q_PER_KV * WINDOW, 2 * WINDOW), lambda i: (jnp.minimum(i, 1), 0, 0)),
                  pl.BlockSpec(memory_space=pltpu.SMEM)],
        out_specs=pl.BlockSpec((WINDOW, D_ATTN), lambda i: (i, 0)),
        compiler_params=_cparams(("arbitrary",)),
        name="prompt_attention",
    )(h_attn, h_attn, h_attn, _band_bias(), sinks)


SAMPLE_ATTN_TILE = 8


def _sample_attn_kernel(q_ref, knew_ref, vnew_ref, ck_ref, cv_ref, sink_ref, o_ref, kwin_ref, vwin_ref):
    lane = lax.broadcasted_iota(jnp.int32, (N_Q_HEADS, D_KV), 1)
    head = lax.broadcasted_iota(jnp.int32, (N_Q_HEADS, D_KV), 0)
    group_mask = _div_pow2(lane, HEAD_DIM) == _div_pow2(head, Q_PER_KV)
    sink = sink_ref[...]
    row = lax.broadcasted_iota(jnp.int32, (WINDOW, D_KV), 0)
    seqs = range(SAMPLE_ATTN_TILE)
    qbd = [jnp.where(group_mask, jnp.concatenate([q_ref[b]] * N_KV_HEADS, axis=1), 0.0).astype(BF16) for b in seqs]
    s = [_dot1(qbd[b], ck_ref[b]) * ATTN_SCALE for b in seqs]
    s_new = [jnp.sum(qbd[b].astype(F32) * knew_ref[b].astype(BF16).astype(F32), axis=-1, keepdims=True) * ATTN_SCALE
             for b in seqs]
    m = [jnp.maximum(jnp.maximum(jnp.max(s[b], axis=-1, keepdims=True), s_new[b]), sink) for b in seqs]
    p = [jnp.exp(s[b] - m[b]) for b in seqs]
    p_new = [jnp.exp(s_new[b] - m[b]) for b in seqs]
    denom = [jnp.sum(p[b], axis=-1, keepdims=True) + p_new[b] + jnp.exp(sink - m[b]) for b in seqs]
    for b in seqs:
        kb = ck_ref[b].T
        vb = cv_ref[b].T
        kn = knew_ref[b]
        vn = vnew_ref[b]
        o_full = (_dot1(p[b] / denom[b], vb)
                  + (p_new[b] / denom[b]).astype(BF16).astype(F32) * vn.astype(BF16).astype(F32))
        o_full = jnp.where(group_mask, o_full, 0.0)
        o = o_full[:, 0:HEAD_DIM]
        for g in range(1, N_KV_HEADS):
            o = o + o_full[:, g * HEAD_DIM:(g + 1) * HEAD_DIM]
        o_ref[b] = o
        kwin_ref[b] = jnp.where(row == WINDOW - 1, kn, pltpu.roll(kb, WINDOW - 1, axis=0))
        vwin_ref[b] = jnp.where(row == WINDOW - 1, vn, pltpu.roll(vb, WINDOW - 1, axis=0))


def _sample_attention(q, k_new, v_new, cache_k_t, cache_v_t, sinks):
    bt = SAMPLE_ATTN_TILE
    win_spec = pl.BlockSpec((bt, WINDOW, D_KV), lambda i: (i, 0, 0))
    win_t_spec = pl.BlockSpec((bt, D_KV, WINDOW), lambda i: (i, 0, 0))
    new_spec = pl.BlockSpec((bt, 1, D_KV), lambda i: (i, 0, 0))
    return pl.pallas_call(
        _sample_attn_kernel,
        out_shape=(jax.ShapeDtypeStruct((DEC_BATCH, N_Q_HEADS, HEAD_DIM), F32),
                   jax.ShapeDtypeStruct((DEC_BATCH, WINDOW, D_KV), F32),
                   jax.ShapeDtypeStruct((DEC_BATCH, WINDOW, D_KV), F32)),
        grid=(DEC_BATCH // bt,),
        in_specs=[pl.BlockSpec((bt, N_Q_HEADS, HEAD_DIM), lambda i: (i, 0, 0)),
                  new_spec, new_spec, win_t_spec, win_t_spec,
                  pl.BlockSpec((N_Q_HEADS, 1), lambda i: (0, 0))],
        out_specs=(pl.BlockSpec((bt, N_Q_HEADS, HEAD_DIM), lambda i: (i, 0, 0)), win_spec, win_spec),
        compiler_params=_cparams(("arbitrary",)),
        name="sample_attention",
    )(q, k_new, v_new, cache_k_t, cache_v_t, sinks)


def _head_ones():
    r = _div_pow2(lax.broadcasted_iota(jnp.int32, (LANES, LANES), 0), HEAD_DIM)
    c = _div_pow2(lax.broadcasted_iota(jnp.int32, (LANES, LANES), 1), HEAD_DIM)
    return jnp.where(r == c, 1.0, 0.0).astype(BF16)


def _head_sum(x, ones, passes=2):
    dot = _dot_exact_rhs if passes == 2 else _dot1
    parts = [dot(x[:, p * LANES:(p + 1) * LANES], ones) for p in range(x.shape[1] // LANES)]
    return jnp.concatenate(parts, axis=1)


def _token_mix(feat, shifted, mu):
    return feat + (shifted - feat) * mu


def _rwkv_prep(mixed, mixed_tail, w0, a0, k_k, k_a, wd, wa, wg, ones):
    r = mixed[:, 0:D_RWKV]
    k = mixed[:, D_RWKV:2 * D_RWKV]
    v = mixed[:, 2 * D_RWKV:3 * D_RWKV]
    xw = mixed_tail[:, 0:W_LORA]
    xa = mixed_tail[:, W_LORA:W_LORA + A_LORA]
    xg = mixed_tail[:, W_LORA + A_LORA:D_TAIL]
    w_log = -_softplus(-(w0 + _dot1(jnp.tanh(xw), wd))) - 0.5
    log_decay = -jnp.exp(w_log)
    a = _sigmoid(a0 + _dot1(xa, wa))
    g = _dot1(_sigmoid(xg), wg)
    kk = k * k_k
    kk = kk * lax.rsqrt(jnp.maximum(_head_sum(kk * kk, ones), 1e-24))
    k2 = k * (1.0 + (a - 1.0) * k_a)
    return r, log_decay, k2, v, -kk, kk * a, g


def _rwkv_post(y, r, k2, v, g, r_k, gn_g, gn_b, ones, passes=2):
    inv_n = 1.0 / HEAD_DIM
    mu = _head_sum(y, ones, passes) * inv_n
    d = y - mu
    var = _head_sum(d * d, ones, passes) * inv_n
    yn = d * lax.rsqrt(var + GN_EPS) * gn_g + gn_b
    bonus = _head_sum(r * k2 * r_k, ones, passes) * v
    return (yn + bonus) * g


(OP_AABS, OP_RABS, OP_AN, OP_RN, OP_BN, OP_KN, OP_BH, OP_KH, OP_V) = range(9)
N_OPS = 9


def _prompt_rwkv_kernel(f1_ref, f2_ref, tail_in_ref, mu_ref, mut_ref, w0_ref, a0_ref, kk_ref, ka_ref, rk_ref,
                        gng_ref, gnb_ref, wd_ref, wa_ref, wg_ref, out_ref, state_ref,
                        prev_ref, prevt_ref, s_ref, ops_ref, pc_ref, y_ref):
    c = pl.program_id(0)
    C = CHUNK

    @pl.when(c == 0)
    def _():
        prev_ref[...] = jnp.zeros_like(prev_ref)
        prevt_ref[...] = jnp.zeros_like(prevt_ref)
        s_ref[...] = jnp.zeros_like(s_ref)

    ones = _head_ones()
    row = lax.broadcasted_iota(jnp.int32, (C, 1), 0)

    def token_shift(feat, carry_ref):
        shifted = jnp.where(row == 0, carry_ref[0:1, :], pltpu.roll(feat, 1, axis=0))
        carry_ref[0:1, :] = feat[C - 1:C, :]
        return shifted

    feat = jnp.concatenate([f1_ref[...], f2_ref[...]], axis=1)
    tail = tail_in_ref[...]
    mixed = _token_mix(feat, token_shift(feat, prev_ref), mu_ref[...])
    mixed_tail = _token_mix(tail, token_shift(tail, prevt_ref), mut_ref[...])
    r, ld, k2, v, av, bv, g = _rwkv_prep(mixed, mixed_tail, w0_ref[...], a0_ref[...], kk_ref[...], ka_ref[...],
                                         wd_ref[...], wa_ref[...], wg_ref[...], ones)

    ti = lax.broadcasted_iota(jnp.int32, (C, C), 0)
    tj = lax.broadcasted_iota(jnp.int32, (C, C), 1)
    tri_incl = jnp.where(tj <= ti, 1.0, 0.0).astype(BF16)
    cs = _dot_exact_lhs(tri_incl, ld)
    cs_ref = cs[C // 2 - 1:C // 2, :]
    cs_end = cs[C - 1:C, :]
    e_prev = jnp.exp(cs - ld)
    e_cur = jnp.exp(cs)
    n_prev = jnp.exp(cs - ld - cs_ref)
    n_cur = jnp.exp(cs - cs_ref)
    n_inv = jnp.exp(cs_ref - cs)
    e_tail = jnp.exp(cs_end - cs)
    ops = {OP_AABS: av * e_prev, OP_RABS: r * e_cur, OP_AN: av * n_prev, OP_RN: r * n_cur,
           OP_BN: bv * n_inv, OP_KN: k2 * n_inv, OP_BH: bv * e_tail, OP_KH: k2 * e_tail, OP_V: v}
    p_end = jnp.exp(cs_end)
    for p in range(N_PAIRS):
        sl = slice(p * LANES, (p + 1) * LANES)
        for idx, val in ops.items():
            ops_ref[p, idx] = val[:, sl]
        pc_ref[p] = jnp.broadcast_to(p_end[:, sl], (SUBLANES, LANES))

    lane1 = lax.broadcasted_iota(jnp.int32, (C, LANES), 1)
    head0 = lane1 < HEAD_DIM
    r2 = lax.broadcasted_iota(jnp.int32, (2 * C, 2 * C), 0)
    c2 = lax.broadcasted_iota(jnp.int32, (2 * C, 2 * C), 1)
    tq = _mod_pow2(r2, C)
    tk = _mod_pow2(c2, C)
    band = (tk < tq) | ((tk == tq) & (r2 >= C))
    blockdiag = _div_pow2(r2, HEAD_DIM) == _div_pow2(c2, HEAD_DIM)

    op = lambda p, idx: ops_ref[p, idx]
    zero_half = jnp.zeros((C, LANES), F32)
    for pairs in [range(g, g + PAIR_GROUP) for g in range(0, N_PAIRS, PAIR_GROUP)]:
        gy = {p: _dot1(jnp.concatenate([op(p, OP_AABS), op(p, OP_RABS)], axis=0), s_ref[p]) for p in pairs}

        am0, am1 = {}, {}
        for p in pairs:
            a_n, r_n = op(p, OP_AN), op(p, OP_RN)
            b0, k0 = jnp.where(head0, op(p, OP_BN), 0.0), jnp.where(head0, op(p, OP_KN), 0.0)
            b1, k1 = jnp.where(head0, 0.0, op(p, OP_BN)), jnp.where(head0, 0.0, op(p, OP_KN))
            am = _dot1(jnp.concatenate([a_n, r_n], axis=0), jnp.concatenate([k0, b0, b1, k1], axis=0), NT)
            am0[p] = jnp.where(band, am[:, 0:2 * C], 0.0)
            am1[p] = jnp.where(band, am[:, 2 * C:4 * C], 0.0)

        w0, w1 = {}, {}
        for p in pairs:
            top0, top1 = am0[p][0:C], am1[p][0:C]
            ak = jnp.concatenate([jnp.where(head0, top0, 0.0), jnp.where(head0, 0.0, top1)], axis=0)
            vv = op(p, OP_V)
            g0 = gy[p][0:C]
            m = jnp.concatenate([g0, g0], axis=0) + _dot1(ak, jnp.concatenate([vv, vv], axis=0))
            w0[p] = jnp.where(head0, m[0:C], top0)
            w1[p] = jnp.where(head0, top1, m[C:2 * C])

        for lvl in range(SOLVE_LEVELS):
            prod0 = {p: _dot1(w0[p], jnp.concatenate([zero_half, w0[p]], axis=0)) for p in pairs}
            prod1 = {p: _dot1(w1[p], jnp.concatenate([w1[p], zero_half], axis=0)) for p in pairs}
            w0 = {p: jnp.where(head0, w0[p] + prod0[p], prod0[p]) for p in pairs}
            w1 = {p: jnp.where(head0, prod1[p], w1[p] + prod1[p]) for p in pairs}
        u = {p: jnp.where(head0, w0[p], w1[p]) for p in pairs}

        for p in pairs:
            vv = op(p, OP_V)
            y_lhs = jnp.concatenate([am0[p][C:2 * C], am1[p][C:2 * C]], axis=1)
            y_rhs = jnp.concatenate([jnp.where(head0, vv, 0.0), jnp.where(head0, u[p], 0.0),
                                     jnp.where(head0, 0.0, u[p]), jnp.where(head0, 0.0, vv)], axis=0)
            y_ref[p] = gy[p][C:2 * C] + _dot1(y_lhs, y_rhs)

        for p in pairs:
            decay_rows = jnp.broadcast_to(pc_ref[p][0:1, :], (LANES, LANES)).T
            upd_lhs = jnp.concatenate([op(p, OP_BH), op(p, OP_KH)], axis=0).T
            upd_rhs = jnp.concatenate([u[p], op(p, OP_V)], axis=0)
            s_ref[p] = s_ref[p] * decay_rows + jnp.where(blockdiag, _dot1(upd_lhs, upd_rhs), 0.0)

    y = jnp.concatenate([y_ref[p] for p in range(N_PAIRS)], axis=1)
    out_ref[...] = _rwkv_post(y, r, k2, v, g, rk_ref[...], gng_ref[...], gnb_ref[...], ones, passes=1)

    @pl.when(c == pl.num_programs(0) - 1)
    def _():
        state_ref[...] = s_ref[...]


def _prompt_rwkv(h_main, tail, prm):
    n_chunks = SEQ // CHUNK
    half = D_RKV // 2
    assert D_QKV == half
    vec = pl.BlockSpec((1, D_RWKV), lambda c: (0, 0))
    full = lambda a: pl.BlockSpec(a.shape, lambda c: (0,) * a.ndim)
    return pl.pallas_call(
        _prompt_rwkv_kernel,
        out_shape=(jax.ShapeDtypeStruct((SEQ, D_RWKV), F32),
                   jax.ShapeDtypeStruct((N_PAIRS, LANES, LANES), F32)),
        grid=(n_chunks,),
        in_specs=[pl.BlockSpec((CHUNK, half), lambda c: (c, 1)),
                  pl.BlockSpec((CHUNK, half), lambda c: (c, 2)),
                  pl.BlockSpec((CHUNK, D_TAIL), lambda c: (c, 0)),
                  full(prm["mu"]), full(prm["mu_tail"]),
                  vec, vec, vec, vec, vec, vec, vec,
                  full(prm["wd"]), full(prm["wa"]), full(prm["wg"])],
        out_specs=(pl.BlockSpec((CHUNK, D_RWKV), lambda c: (c, 0)),
                   pl.BlockSpec((N_PAIRS, LANES, LANES), lambda c: (0, 0, 0))),
        scratch_shapes=[pltpu.VMEM((SUBLANES, D_RKV), F32),
                        pltpu.VMEM((SUBLANES, D_TAIL), F32),
                        pltpu.VMEM((N_PAIRS, LANES, LANES), F32),
                        pltpu.VMEM((N_PAIRS, N_OPS, CHUNK, LANES), F32),
                        pltpu.VMEM((N_PAIRS, SUBLANES, LANES), F32),
                        pltpu.VMEM((N_PAIRS, CHUNK, LANES), F32)],
        compiler_params=_cparams(("arbitrary",)),
        name="prompt_rwkv",
    )(h_main, h_main, tail, prm["mu"], prm["mu_tail"], prm["w0"], prm["a0"], prm["k_k"], prm["k_a"],
      prm["r_k"], prm["gn_g"], prm["gn_b"], prm["wd"], prm["wa"], prm["wg"])


def _sample_prep_kernel(h_ref, x_ref, wt_ref, shift_ref, mu_ref, mut_ref, w0_ref, a0_ref, kk_ref, ka_ref,
                        wd_ref, wa_ref, wg_ref, r_ref, k_ref, v_ref, g_ref, tail_ref,
                        rt_ref, wtr_ref, kt_ref, vt_ref, at_ref, bt_ref):
    ones = _head_ones()
    feat = h_ref[:, D_QKV:D_MAIN]
    tail = _dot1(x_ref[...], wt_ref[...], NT)
    tail_ref[...] = tail
    mixed = _token_mix(feat, shift_ref[:, 0:D_RKV], mu_ref[...])
    mixed_tail = _token_mix(tail, shift_ref[:, D_RKV:D_SHIFT], mut_ref[...])
    r, ld, k2, v, av, bv, g = _rwkv_prep(mixed, mixed_tail, w0_ref[...], a0_ref[...], kk_ref[...], ka_ref[...],
                                         wd_ref[...], wa_ref[...], wg_ref[...], ones)
    r_ref[...] = r
    k_ref[...] = k2
    v_ref[...] = v
    g_ref[...] = g
    rt_ref[...] = r.T
    wtr_ref[...] = jnp.exp(ld).T
    kt_ref[...] = k2.T
    vt_ref[...] = v.T
    at_ref[...] = av.T
    bt_ref[...] = bv.T


def _sample_prep(h_main, x, shift, prm):
    tok = jax.ShapeDtypeStruct((DEC_BATCH, D_RWKV), F32)
    chan = jax.ShapeDtypeStruct((D_RWKV, DEC_BATCH), F32)
    return pl.pallas_call(
        _sample_prep_kernel,
        out_shape=(tok,) * 4 + (jax.ShapeDtypeStruct((DEC_BATCH, D_TAIL), F32),) + (chan,) * 6,
        compiler_params=pltpu.CompilerParams(vmem_limit_bytes=VMEM_LIMIT),
        name="sample_rwkv_prep",
    )(h_main, x, prm["w_tail"], shift, prm["mu"], prm["mu_tail"], prm["w0"], prm["a0"], prm["k_k"], prm["k_a"],
      prm["wd"], prm["wa"], prm["wg"])


STEP_GROUP = 4


def _sample_step_kernel(s_ref, r_ref, w_ref, k_ref, a_ref, b_ref, v_ref, y_ref, snew_ref):
    r, w, k, a, b = r_ref[...], w_ref[...], k_ref[...], a_ref[...], b_ref[...]
    for g0 in range(0, HEAD_DIM, 2 * STEP_GROUP):
        chans = range(g0, g0 + 2 * STEP_GROUP)
        sa = {i: jnp.sum(s_ref[0, i] * a, axis=0, keepdims=True) for i in chans}
        s_new = {i: s_ref[0, i] * w + sa[i] * b + v_ref[i:i + 1, :] * k for i in chans}
        for i in chans:
            y_ref[i:i + 1, :] = jnp.sum(s_new[i] * r, axis=0, keepdims=True)
        for i in range(g0, g0 + 2 * STEP_GROUP, 2):
            pair = jnp.concatenate([s_new[i], s_new[i + 1]], axis=0)
            snew_ref[:, i * HEAD_DIM:(i + 2) * HEAD_DIM] = pair.T


def _sample_step(state_t, r_t, w_t, k_t, a_t, b_t, v_t):
    head_rows = pl.BlockSpec((HEAD_DIM, DEC_BATCH), lambda h: (h, 0))
    return pl.pallas_call(
        _sample_step_kernel,
        out_shape=(jax.ShapeDtypeStruct((D_RWKV, DEC_BATCH), F32),
                   jax.ShapeDtypeStruct((DEC_BATCH, N_RWKV_HEADS * HEAD_DIM * HEAD_DIM), F32)),
        grid=(N_RWKV_HEADS,),
        in_specs=[pl.BlockSpec((1, HEAD_DIM, HEAD_DIM, DEC_BATCH), lambda h: (h, 0, 0, 0))] + [head_rows] * 6,
        out_specs=(head_rows, pl.BlockSpec((DEC_BATCH, HEAD_DIM * HEAD_DIM), lambda h: (0, h))),
        compiler_params=_cparams(("arbitrary",)),
        name="sample_rwkv_step",
    )(state_t, r_t, w_t, k_t, a_t, b_t, v_t)


def _sample_post_kernel(yt_ref, r_ref, k_ref, v_ref, g_ref, rk_ref, gng_ref, gnb_ref, o_ref):
    o_ref[...] = _rwkv_post(yt_ref[...].T, r_ref[...], k_ref[...], v_ref[...], g_ref[...], rk_ref[...],
                            gng_ref[...], gnb_ref[...], _head_ones())


def _sample_post(y, r, k, v, g, prm):
    return pl.pallas_call(
        _sample_post_kernel,
        out_shape=jax.ShapeDtypeStruct((DEC_BATCH, D_RWKV), F32),
        compiler_params=pltpu.CompilerParams(vmem_limit_bytes=VMEM_LIMIT),
        name="sample_rwkv_post",
    )(y, r, k, v, g, prm["r_k"], prm["gn_g"], prm["gn_b"])


def _project_mix(attn_ref, rwkv_ref, wo_ref):
    return (_dot(attn_ref[...].astype(BF16), wo_ref[0:D_ATTN, :])
            + _dot(rwkv_ref[...].astype(BF16), wo_ref[D_ATTN:D_ATTN + D_RWKV, :]))


def _norm_and_route(mix, x_ref, g_ref, b_ref, wr_ref, br_ref, x1_ref, x1b_ref, route_ref):
    x1 = _layer_norm(ALPHA * x_ref[...] + mix, g_ref[...], b_ref[...])
    x1_ref[...] = x1
    x1b = x1.astype(BF16)
    x1b_ref[...] = _pack_bf16_halves(x1b)
    logits = _dot(x1b, wr_ref[...].astype(BF16)) + br_ref[...]
    tm = logits.shape[0]
    lane = lax.broadcasted_iota(jnp.int32, (tm, LANES), 1).astype(F32)
    big = float(2 * LANES)
    neg = -jnp.inf
    lc = jnp.where(lane < N_GROUPS, logits, neg)
    mc = jnp.max(lc, axis=-1, keepdims=True)
    g_sel = jnp.min(jnp.where(lc == mc, lane, big), axis=-1, keepdims=True)
    p_group = 1.0 / jnp.sum(jnp.exp(lc - mc), axis=-1, keepdims=True)
    lo = ROUTE_FINE_OFF + g_sel * EXPERTS_PER_GROUP
    lf = jnp.where((lane >= lo) & (lane < lo + EXPERTS_PER_GROUP), logits, neg)
    v1 = jnp.max(lf, axis=-1, keepdims=True)
    i1 = jnp.min(jnp.where(lf == v1, lane, big), axis=-1, keepdims=True)
    lf2 = jnp.where(lane == i1, neg, lf)
    v2 = jnp.max(lf2, axis=-1, keepdims=True)
    i2 = jnp.min(jnp.where(lf2 == v2, lane, big), axis=-1, keepdims=True)
    e21 = jnp.exp(v2 - v1)
    gate1 = p_group / (1.0 + e21)
    gate2 = p_group * e21 / (1.0 + e21)
    route = jnp.where(lane == 0, i1 - ROUTE_FINE_OFF,
                      jnp.where(lane == 1, i2 - ROUTE_FINE_OFF,
                                jnp.where(lane == 2, gate1, jnp.where(lane == 3, gate2, 0.0))))
    route_ref[...] = route


N_ROUTER_OUTS = 3


def _outproj_router_kernel(n_tiles, n_aliased, attn_ref, rwkv_ref, x_ref, wo_ref, g_ref, b_ref, wr_ref, br_ref,
                           *rest):
    outs = rest[n_aliased:n_aliased + N_ROUTER_OUTS]
    mix_ref = rest[-1]
    i = pl.program_id(0)
    finish = lambda mix: _norm_and_route(mix, x_ref, g_ref, b_ref, wr_ref, br_ref, *outs)

    @pl.when(i == 0)
    def _():
        mix_ref[...] = _project_mix(attn_ref, rwkv_ref, wo_ref)

    @pl.when((i >= 1) & (i < n_tiles))
    def _():
        finish(mix_ref[...])
        mix_ref[...] = _project_mix(attn_ref, rwkv_ref, wo_ref)

    @pl.when(i == n_tiles)
    def _():
        finish(mix_ref[...])

    @pl.when(i > n_tiles)
    def _():
        for out_ref in outs:
            out_ref[...] = jnp.zeros_like(out_ref)


def _outproj_router(attn, rwkv, x, wo_bf16, ln_g, ln_b, w_route, b_route, tm, n_total, row_block, into, name):
    m = x.shape[0]
    n_tiles = m // tm
    const = lambda shape: pl.BlockSpec(shape, lambda i: (0, 0))
    ahead = lambda width: pl.BlockSpec((tm, width), lambda i: (jnp.minimum(i, n_tiles - 1), 0))
    behind = lambda width: pl.BlockSpec((tm, width), lambda i: (jnp.clip(i - 1, 0, n_tiles - 1), 0))
    in_specs = [ahead(D_ATTN), ahead(D_RWKV), behind(D_MODEL),
                const((D_MODEL, D_MODEL)), const((1, D_MODEL)), const((1, D_MODEL)),
                const((D_MODEL, LANES)), const((1, LANES))]
    args = [attn, rwkv, x, wo_bf16, ln_g, ln_b, w_route, b_route]
    aliases, n_aliased, fill_steps = {}, 0, pl.cdiv(n_total - m, tm)
    if into is not None:
        n_aliased, fill_steps = N_ROUTER_OUTS, 0
        in_specs += [pl.BlockSpec(memory_space=pl.ANY)] * N_ROUTER_OUTS
        aliases = {len(args) + k: k for k in range(N_ROUTER_OUTS)}
        args += list(into)
    out_rows = lambda width: pl.BlockSpec((tm, width), lambda i: (jnp.maximum(i - 1, 0) + row_block, 0))
    return pl.pallas_call(
        functools.partial(_outproj_router_kernel, n_tiles, n_aliased),
        out_shape=(jax.ShapeDtypeStruct((n_total, D_MODEL), F32),
                   jax.ShapeDtypeStruct((n_total, D_MODEL // 2), jnp.uint32),
                   jax.ShapeDtypeStruct((n_total, LANES), F32)),
        grid=(n_tiles + 1 + fill_steps,),
        in_specs=in_specs,
        out_specs=(out_rows(D_MODEL), out_rows(D_MODEL // 2), out_rows(LANES)),
        scratch_shapes=[pltpu.VMEM((tm, D_MODEL), F32)],
        input_output_aliases=aliases,
        compiler_params=_cparams(("arbitrary",)),
        name=name,
    )(*args)


DISPATCH_TILE = 128


def _dispatch_kernel(zoff_ref, dest_ref, x_ref, o_hbm, zbuf, ring, zsem, sem):
    i = pl.program_id(0)
    n_blocks = o_hbm.shape[0] // MOE_BLOCK
    n_used = zoff_ref[N_EXPERTS]

    def zero_fill(start_row):
        start_row = pl.multiple_of(start_row, MOE_BLOCK)
        return pltpu.make_async_copy(zbuf, o_hbm.at[pl.ds(start_row, MOE_BLOCK)], zsem)

    def zero_fills(action):
        for e in range(N_EXPERTS):
            @pl.when(zoff_ref[e] >= 0)
            def _():
                action(zero_fill(zoff_ref[e]))
        for b in range(n_blocks):
            @pl.when(b >= n_used)
            def _():
                action(zero_fill(b * MOE_BLOCK))

    @pl.when(i == 0)
    def _():
        zbuf[...] = jnp.zeros_like(zbuf)
        zero_fills(lambda copy: copy.start())
        zero_fills(lambda copy: copy.wait())

    cur = lax.rem(i, 2)

    def wait_rows(slot):
        for k in range(2):
            pltpu.make_async_copy(ring.at[slot], o_hbm.at[pl.ds(0, DISPATCH_TILE)], sem.at[slot]).wait()

    @pl.when(i >= 2)
    def _():
        wait_rows(cur)

    ring[cur] = x_ref[...]
    for t in range(DISPATCH_TILE):
        for k in range(2):
            pltpu.make_async_copy(ring.at[cur, pl.ds(t, 1)], o_hbm.at[pl.ds(dest_ref[0, 0, 2 * t + k], 1)],
                                  sem.at[cur]).start()

    @pl.when(i == pl.num_programs(0) - 1)
    def _():
        wait_rows(cur)

        @pl.when(i >= 1)
        def _():
            wait_rows(1 - cur)


def _dispatch(zero_offsets, dest, x_packed, n_blocks):
    n_tokens, width = x_packed.shape
    grid_spec = pltpu.PrefetchScalarGridSpec(
        num_scalar_prefetch=1,
        grid=(n_tokens // DISPATCH_TILE,),
        in_specs=[pl.BlockSpec((1, 1, 2 * DISPATCH_TILE), lambda i, z: (i, 0, 0), memory_space=pltpu.SMEM),
                  pl.BlockSpec((DISPATCH_TILE, width), lambda i, z: (i, 0))],
        out_specs=pl.BlockSpec(memory_space=pl.ANY),
        scratch_shapes=[pltpu.VMEM((MOE_BLOCK, width), x_packed.dtype),
                        pltpu.VMEM((2, DISPATCH_TILE, width), x_packed.dtype),
                        pltpu.SemaphoreType.DMA, pltpu.SemaphoreType.DMA((2,))],
    )
    return pl.pallas_call(
        _dispatch_kernel,
        out_shape=jax.ShapeDtypeStruct((n_blocks * MOE_BLOCK, width), x_packed.dtype),
        grid_spec=grid_spec,
        compiler_params=_cparams(("arbitrary",)),
        name="moe_dispatch",
    )(zero_offsets, dest.reshape(-1, 1, 2 * DISPATCH_TILE), x_packed)


def _expert_kernel(be_ref, nb_ref, x_ref, wg_hbm, wu_hbm, wd_hbm, o_ref, wg_buf, wu_buf, wd_buf, slot_ref, sem):
    blk = pl.program_id(0)
    n_used = nb_ref[0]
    expert = be_ref[blk]
    is_first = (blk == 0) | (be_ref[jnp.maximum(blk - 1, 0)] != expert)

    def fetch(e, slot):
        return [pltpu.make_async_copy(hbm.at[e], buf.at[slot], sem.at[slot, i])
                for i, (hbm, buf) in enumerate(((wg_hbm, wg_buf), (wu_hbm, wu_buf), (wd_hbm, wd_buf)))]

    @pl.when((blk < n_used) & is_first)
    def _():
        @pl.when(blk == 0)
        def _():
            slot_ref[0] = 1
            for copy in fetch(expert, 0):
                copy.start()

        slot = 1 - slot_ref[0]
        slot_ref[0] = slot
        for copy in fetch(expert, slot):
            copy.wait()
        nxt = lax.while_loop(lambda j: (j < n_used) & (be_ref[jnp.minimum(j, n_used - 1)] == expert),
                             lambda j: j + 1, blk + 1)

        @pl.when(nxt < n_used)
        def _():
            for copy in fetch(be_ref[jnp.minimum(nxt, n_used - 1)], 1 - slot):
                copy.start()

    @pl.when(blk < n_used)
    def _():
        slot = slot_ref[0]
        half = D_MODEL // 2
        x_head, x_tail = _unpack_bf16_halves(x_ref[...])
        proj = lambda w_buf: (_dot(x_head, w_buf[slot, 0:half, :].astype(BF16))
                              + _dot(x_tail, w_buf[slot, half:D_MODEL, :].astype(BF16)))
        gate = proj(wg_buf)
        up = proj(wu_buf)
        h = gate * _sigmoid(gate) * up
        o_ref[...] = _dot(h.astype(BF16), wd_buf[slot].astype(BF16))

    @pl.when(blk >= n_used)
    def _():
        o_ref[...] = jnp.zeros_like(o_ref)


def _expert_mlp(block_expert, n_used, x_sorted, w_gate, w_up, w_down, n_blocks):
    grid_spec = pltpu.PrefetchScalarGridSpec(
        num_scalar_prefetch=2,
        grid=(n_blocks,),
        in_specs=[pl.BlockSpec((MOE_BLOCK, D_MODEL // 2), lambda b, be, nb: (jnp.minimum(b, nb[0] - 1), 0)),
                  pl.BlockSpec(memory_space=pl.ANY), pl.BlockSpec(memory_space=pl.ANY),
                  pl.BlockSpec(memory_space=pl.ANY)],
        out_specs=pl.BlockSpec((MOE_BLOCK, D_MODEL), lambda b, be, nb: (b, 0)),
        scratch_shapes=[pltpu.VMEM((2, D_MODEL, D_EXPERT), F32), pltpu.VMEM((2, D_MODEL, D_EXPERT), F32),
                        pltpu.VMEM((2, D_EXPERT, D_MODEL), F32), pltpu.SMEM((1,), jnp.int32),
                        pltpu.SemaphoreType.DMA((2, 3))],
    )
    return pl.pallas_call(
        _expert_kernel,
        out_shape=jax.ShapeDtypeStruct((n_blocks * MOE_BLOCK, D_MODEL), F32),
        grid_spec=grid_spec,
        compiler_params=_cparams(("arbitrary",)),
        name="expert_mlp",
    )(block_expert, n_used, x_sorted, w_gate, w_up, w_down)


COMBINE_TILE = 256


def _combine_kernel(dest_ref, dest_next_ref, y_hbm, x1_ref, route_ref, g_ref, b_ref, o_ref, ybuf, sem):
    i = pl.program_id(0)
    cur = lax.rem(i, 2)
    tile = x1_ref.shape[0]
    n_rows = 2 * tile

    def gather(table_ref, buf):
        for slot in range(n_rows):
            pltpu.make_async_copy(y_hbm.at[pl.ds(table_ref[0, 0, slot], 1)], ybuf.at[buf, pl.ds(slot, 1)],
                                  sem.at[buf]).start()

    def wait_gather(buf):
        pltpu.make_async_copy(y_hbm.at[pl.ds(0, n_rows)], ybuf.at[buf], sem.at[buf]).wait()

    @pl.when(i == 0)
    def _():
        gather(dest_ref, 0)

    gather(dest_next_ref, 1 - cur)
    wait_gather(cur)
    route = route_ref[...]
    yb = ybuf[cur]
    moe = route[:, 2:3] * yb[0:tile, :] + route[:, 3:4] * yb[tile:n_rows, :]
    o_ref[...] = _layer_norm(ALPHA * x1_ref[...] + moe, g_ref[...], b_ref[...])

    @pl.when(i == pl.num_programs(0) - 1)
    def _():
        wait_gather(1 - cur)


def _combine(dest, y_slots, x1_all, route_all, m, tm, row_block, ln_g, ln_b, name):
    return pl.pallas_call(
        _combine_kernel,
        out_shape=jax.ShapeDtypeStruct((m, D_MODEL), F32),
        grid=(m // tm,),
        in_specs=[pl.BlockSpec((1, 1, 2 * tm), lambda i: (i, 0, 0), memory_space=pltpu.SMEM),
                  pl.BlockSpec((1, 1, 2 * tm), lambda i: (i + 1, 0, 0), memory_space=pltpu.SMEM),
                  pl.BlockSpec(memory_space=pl.ANY),
                  pl.BlockSpec((tm, D_MODEL), lambda i: (i + row_block, 0)),
                  pl.BlockSpec((tm, LANES), lambda i: (i + row_block, 0)),
                  pl.BlockSpec((1, D_MODEL), lambda i: (0, 0)),
                  pl.BlockSpec((1, D_MODEL), lambda i: (0, 0))],
        out_specs=pl.BlockSpec((tm, D_MODEL), lambda i: (i, 0)),
        scratch_shapes=[pltpu.VMEM((2, 2 * tm, D_MODEL), F32), pltpu.SemaphoreType.DMA((2,))],
        compiler_params=_cparams(("arbitrary",)),
        name=name,
    )(dest, dest, y_slots, x1_all, route_all, ln_g, ln_b)


def _dispatch_plan(route_all, n_blocks):
    flat_e = route_all[:, 0:2].astype(jnp.int32).reshape(-1)
    onehot = (flat_e[:, None] == jnp.arange(N_EXPERTS, dtype=jnp.int32)[None, :]).astype(jnp.int32)
    csum = jnp.cumsum(onehot, axis=0)
    rank = jnp.sum(onehot * csum, axis=1) - 1
    counts = csum[-1]
    padded = (counts + MOE_BLOCK - 1) // MOE_BLOCK * MOE_BLOCK
    pend = jnp.cumsum(padded)
    pstart = pend - padded
    dest = (pstart[flat_e] + rank).astype(jnp.int32)
    zero_offsets = jnp.where(counts > 0, pend - MOE_BLOCK, -1).astype(jnp.int32)
    n_used = (pend[-1] // MOE_BLOCK).astype(jnp.int32)
    block_start = jnp.minimum(jnp.arange(n_blocks, dtype=jnp.int32), n_used - 1) * MOE_BLOCK
    block_e = jnp.minimum(jnp.searchsorted(pend, block_start, side="right"), N_EXPERTS - 1).astype(jnp.int32)
    return dest, jnp.concatenate([zero_offsets, n_used.reshape(1)]), block_e, n_used.reshape(1)


def kernel(x_prompt, x_sample, cache_k_win, cache_v_win, state_wkv, state_shift, w_in, attn_sinks, shift_mu, w0,
           w_decay_up, a0, w_a_up, w_g_up, k_k, k_a, r_k, gn_g, gn_b, w_out, ln1_g, ln1_b, w_coarse, b_coarse,
           w_fine, b_fine, w_exp_gate, w_exp_up, w_exp_down, ln2_g, ln2_b):
    xp = x_prompt[0]
    xs = x_sample[:, 0]
    row = lambda a: a.reshape(1, -1)

    w_in_t = jnp.swapaxes(w_in[0], 0, 1)
    prm = dict(mu=row(shift_mu[0, :D_RKV]), mu_tail=row(shift_mu[0, D_RKV:]), w_tail=w_in_t[D_MAIN:].astype(BF16),
               w0=row(w0[0]), a0=row(a0[0]), k_k=row(k_k[0]), k_a=row(k_a[0]),
               r_k=row(r_k[0]), gn_g=row(gn_g[0]), gn_b=row(gn_b[0]),
               wd=w_decay_up[0], wa=w_a_up[0], wg=w_g_up[0])
    sinks = attn_sinks[0]
    wo_bf16 = w_out[0].astype(BF16)
    w_route = jnp.pad(jnp.concatenate([w_coarse[0], w_fine[0]], axis=1), ((0, 0), (0, LANES - N_GROUPS - N_EXPERTS)))
    b_route = jnp.pad(jnp.concatenate([b_coarse[0], b_fine[0]]), (0, LANES - N_GROUPS - N_EXPERTS)).reshape(1, LANES)

    hp = _matmul(xp, w_in_t, D_MAIN, MAIN_TM, MAIN_TN, "in_proj_prompt")
    tail_p = _matmul(xp, prm["w_tail"], D_TAIL, TAIL_TM, D_TAIL, "in_proj_prompt_tail")
    hs = _matmul(xs, w_in_t, D_MAIN, DEC_BATCH, MAIN_TN, "in_proj_sample")

    attn_p = _prompt_attention(hp, sinks)
    rwkv_p, state_p = _prompt_rwkv(hp, tail_p, prm)

    q_s = hs[:, :D_ATTN].reshape(DEC_BATCH, N_Q_HEADS, HEAD_DIM)
    k_s = hs[:, D_ATTN:D_ATTN + D_KV].reshape(DEC_BATCH, 1, D_KV)
    v_s = hs[:, D_ATTN + D_KV:D_QKV].reshape(DEC_BATCH, 1, D_KV)
    window_t = lambda c: jnp.transpose(c, (0, 2, 3, 1)).reshape(DEC_BATCH, D_KV, WINDOW)
    attn_s, kwin_s, vwin_s = _sample_attention(
        q_s, k_s, v_s, window_t(cache_k_win[0]), window_t(cache_v_win[0]), sinks.reshape(N_Q_HEADS, 1))
    r_s, k2_s, vv_s, g_s, tail_s, r_t, w_t, k_t, v_t, a_t, b_t = _sample_prep(hs, xs, state_shift[0], prm)
    y_t, state_s = _sample_step(jnp.transpose(state_wkv[0], (1, 2, 3, 0)), r_t, w_t, k_t, a_t, b_t, v_t)
    state_s = state_s.reshape(DEC_BATCH, N_RWKV_HEADS, HEAD_DIM, HEAD_DIM)
    rwkv_s = _sample_post(y_t, r_s, k2_s, vv_s, g_s, prm)

    n_tokens = SEQ + DEC_BATCH
    outs_pr = _outproj_router(attn_p, rwkv_p, xp, wo_bf16, row(ln1_g[0]), row(ln1_b[0]), w_route, b_route,
                              OUTPROJ_TM, n_tokens, 0, None, "outproj_router_prompt")
    x1_all, x1b_all, route_all = _outproj_router(attn_s.reshape(DEC_BATCH, D_ATTN), rwkv_s, xs,
                                                 wo_bf16, row(ln1_g[0]), row(ln1_b[0]), w_route, b_route,
                                                 DEC_BATCH, n_tokens, SEQ // DEC_BATCH, outs_pr,
                                                 "outproj_router_sample")

    n_assign = 2 * n_tokens
    n_blocks = -(-(n_assign + N_EXPERTS * (MOE_BLOCK - 1)) // MOE_BLOCK)
    dest, zero_offsets, block_e, n_used = _dispatch_plan(route_all, n_blocks)
    x_sorted = _dispatch(zero_offsets, dest, x1b_all, n_blocks)
    y_slots = _expert_mlp(block_e, n_used, x_sorted, w_exp_gate[0], w_exp_up[0], w_exp_down[0], n_blocks)

    def dest_tiles(d, tile):
        d = d.reshape(-1, tile, 2)
        d = jnp.concatenate([d[:, :, 0], d[:, :, 1]], axis=1)
        return jnp.pad(d, ((0, 1), (0, 0))).reshape(-1, 1, 2 * tile)

    y_p = _combine(dest_tiles(dest[:2 * SEQ], COMBINE_TILE), y_slots, x1_all, route_all, SEQ, COMBINE_TILE, 0,
                   row(ln2_g[0]), row(ln2_b[0]), "combine_prompt")
    y_s = _combine(dest_tiles(dest[2 * SEQ:], DEC_BATCH), y_slots, x1_all, route_all, DEC_BATCH, DEC_BATCH,
                   SEQ // DEC_BATCH, row(ln2_g[0]), row(ln2_b[0]), "combine_sample")

    kv4 = lambda a: a.reshape(a.shape[0], N_KV_HEADS, HEAD_DIM)
    k_win_p = kv4(hp[SEQ - WINDOW:, D_ATTN:D_ATTN + D_KV])[None, None]
    v_win_p = kv4(hp[SEQ - WINDOW:, D_ATTN + D_KV:D_QKV])[None, None]
    sp = state_p.reshape(N_PAIRS, HEADS_PER_TILE, HEAD_DIM, HEADS_PER_TILE, HEAD_DIM)
    wkv_p = jnp.stack([sp[:, i, :, i, :] for i in range(HEADS_PER_TILE)], axis=1)
    wkv_p = wkv_p.reshape(N_RWKV_HEADS, HEAD_DIM, HEAD_DIM).transpose(0, 2, 1)[None, None]
    shift_p = jnp.concatenate([hp[SEQ - 1:SEQ, D_QKV:], tail_p[SEQ - 1:SEQ]], axis=1)[None]
    shift_s = jnp.concatenate([hs[:, D_QKV:], tail_s], axis=1)[None]
    return (y_p[None], y_s[:, None, :], k_win_p, v_win_p, wkv_p, shift_p,
            kwin_s.reshape(1, DEC_BATCH, WINDOW, N_KV_HEADS, HEAD_DIM),
            vwin_s.reshape(1, DEC_BATCH, WINDOW, N_KV_HEADS, HEAD_DIM),
            state_s[None], shift_s)
```

```python
import functools
import math

import jax
import jax.numpy as jnp
from jax import lax
from jax.experimental import pallas as pl
from jax.experimental.pallas import tpu as pltpu

F32 = jnp.float32
BF16 = jnp.bfloat16

D_MODEL = 2048
SEQ = 8192
DEC_BATCH = 128
HEAD_DIM = 64
D_ATTN = 1024
D_RWKV = 1024
N_Q_HEADS = 16
N_KV_HEADS = 4
Q_PER_KV = 4
D_KV = 256
WINDOW = 128
ATTN_SCALE = HEAD_DIM ** -0.5
N_RWKV_HEADS = 16
W_LORA = 64
A_LORA = 64
G_LORA = 160
D_SHIFT = 3 * D_RWKV + W_LORA + A_LORA + G_LORA
D_QKV = D_ATTN + 2 * D_KV
N_GROUPS = 4
EXPERTS_PER_GROUP = 8
N_EXPERTS = 32
D_EXPERT = 512
ALPHA = 2.0 ** 0.25
LN_EPS = 1e-5
GN_EPS = 64e-5

SUBLANES = 8
LANES = 128
VMEM_LIMIT = 52 * 1024 * 1024

D_RKV = 3 * D_RWKV
D_TAIL = W_LORA + A_LORA + G_LORA
D_MAIN = D_QKV + D_RKV
MAIN_TN = 1536
MAIN_TM = 512
OUTPROJ_TM = 256

CHUNK = 64
HEADS_PER_TILE = LANES // HEAD_DIM
N_PAIRS = N_RWKV_HEADS // HEADS_PER_TILE
SOLVE_LEVELS = int(math.log2(CHUNK))
PAIR_GROUP = 8

MOE_BLOCK = 256
ROUTE_FINE_OFF = N_GROUPS

NN = (((1,), (0,)), ((), ()))
NT = (((1,), (1,)), ((), ()))


def _dot(a, b, dims=NN):
    return lax.dot_general(a, b, dims, preferred_element_type=F32)


def _dot1(a, b, dims=NN):
    return _dot(a.astype(BF16), b.astype(BF16), dims)


def _split(x):
    hi = x.astype(BF16)
    lo = (x - hi.astype(F32)).astype(BF16)
    return hi, lo


def _dot_exact_lhs(a_bf16, b, dims=NN):
    bh, bl = _split(b)
    return _dot(a_bf16, bh, dims) + _dot(a_bf16, bl, dims)


def _dot_exact_rhs(a, b_bf16, dims=NN):
    ah, al = _split(a)
    return _dot(ah, b_bf16, dims) + _dot(al, b_bf16, dims)


def _div_pow2(x, d):
    return lax.shift_right_logical(x, jnp.int32(int(math.log2(d))))


def _mod_pow2(x, d):
    return lax.bitwise_and(x, jnp.int32(d - 1))


def _pack_bf16_halves(x_bf16):
    n = x_bf16.shape[1] // 2
    bits = lax.bitcast_convert_type(x_bf16.astype(F32), jnp.uint32)
    return lax.bitwise_or(bits[:, 0:n], lax.shift_right_logical(bits[:, n:2 * n], jnp.uint32(16)))


def _unpack_bf16_halves(packed):
    hi = lax.bitcast_convert_type(lax.bitwise_and(packed, jnp.uint32(0xFFFF0000)), F32)
    lo = lax.bitcast_convert_type(lax.shift_left(packed, jnp.uint32(16)), F32)
    return hi.astype(BF16), lo.astype(BF16)


def _sigmoid(x):
    return 1.0 / (1.0 + jnp.exp(-x))


def _softplus(x):
    return jnp.maximum(x, 0.0) + jnp.log(1.0 + jnp.exp(-jnp.abs(x)))


def _layer_norm(z, g, b):
    mu = jnp.mean(z, axis=-1, keepdims=True)
    d = z - mu
    var = jnp.mean(d * d, axis=-1, keepdims=True)
    return d * lax.rsqrt(var + LN_EPS) * g + b


def _cparams(sem):
    return pltpu.CompilerParams(dimension_semantics=sem, vmem_limit_bytes=VMEM_LIMIT)


def _matmul_kernel(x_ref, wt_ref, o_ref):
    o_ref[...] = _dot(x_ref[...].astype(BF16), wt_ref[...].astype(BF16), NT)


def _matmul(x, w_t, n_out, tm, tn, name):
    m, k = x.shape
    tm = min(tm, m)
    return pl.pallas_call(
        _matmul_kernel,
        out_shape=jax.ShapeDtypeStruct((m, n_out), F32),
        grid=(n_out // tn, m // tm),
        in_specs=[pl.BlockSpec((tm, k), lambda j, i: (i, 0)),
                  pl.BlockSpec((tn, k), lambda j, i: (j, 0))],
        out_specs=pl.BlockSpec((tm, tn), lambda j, i: (i, j)),
        compiler_params=_cparams(("arbitrary", "arbitrary")),
        name=name,
    )(x, w_t)


def _matmul_with_tail_kernel(x_ref, wt_ref, wtail_ref, o_ref, tail_ref):
    xb = x_ref[...].astype(BF16)
    o_ref[...] = _dot(xb, wt_ref[...].astype(BF16), NT)

    @pl.when(pl.program_id(0) == 0)
    def _():
        tail_ref[...] = _dot(xb, wtail_ref[...], NT)

    @pl.when(pl.program_id(0) > 0)
    def _():
        tail_ref[...] = jnp.zeros_like(tail_ref)


def _matmul_with_tail(x, w_t, n_out, wtail_t, tm, tn, name):
    m, k = x.shape
    n_rows = m // tm
    n_tail = wtail_t.shape[0]
    return pl.pallas_call(
        _matmul_with_tail_kernel,
        out_shape=(jax.ShapeDtypeStruct((m, n_out), F32), jax.ShapeDtypeStruct((m + tm, n_tail), F32)),
        grid=(n_out // tn, n_rows),
        in_specs=[pl.BlockSpec((tm, k), lambda j, i: (i, 0)),
                  pl.BlockSpec((tn, k), lambda j, i: (j, 0)),
                  pl.BlockSpec((n_tail, k), lambda j, i: (0, 0))],
        out_specs=(pl.BlockSpec((tm, tn), lambda j, i: (i, j)),
                   pl.BlockSpec((tm, n_tail), lambda j, i: (jnp.where(j == 0, i, n_rows), 0))),
        compiler_params=_cparams(("arbitrary", "arbitrary")),
        name=name,
    )(x, w_t, wtail_t)


def _band_bias():
    qi = jnp.arange(Q_PER_KV * WINDOW)[:, None] % WINDOW
    kj = jnp.arange(2 * WINDOW)[None, :]
    diff = qi + WINDOW - kj
    band = (diff >= 0) & (diff <= WINDOW)
    keep = jnp.stack([band & (kj >= WINDOW), band])
    return jnp.where(keep, 0.0, -jnp.inf).astype(F32)


def _prompt_attn_kernel(q_ref, kvp_ref, kvc_ref, bias_ref, sink_ref, o_ref):
    q = q_ref[...]
    kv_prev = kvp_ref[...]
    kv_cur = kvc_ref[...]
    bias = bias_ref[0]
    row_head = _div_pow2(lax.broadcasted_iota(jnp.int32, (Q_PER_KV * WINDOW, 1), 0), WINDOW)
    groups = range(N_KV_HEADS)
    kv_cols = lambda off, g: jnp.concatenate([kv_prev[:, off + g * HEAD_DIM:off + (g + 1) * HEAD_DIM],
                                              kv_cur[:, off + g * HEAD_DIM:off + (g + 1) * HEAD_DIM]],
                                             axis=0).astype(BF16)
    q_rows = lambda g: jnp.concatenate(
        [q[:, (g * Q_PER_KV + h) * HEAD_DIM:(g * Q_PER_KV + h + 1) * HEAD_DIM] for h in range(Q_PER_KV)],
        axis=0).astype(BF16)
    s = [_dot(q_rows(g), kv_cols(0, g), NT) * ATTN_SCALE + bias for g in groups]
    sink = []
    for g in groups:
        col = jnp.zeros((Q_PER_KV * WINDOW, 1), F32)
        for h in range(Q_PER_KV):
            col = jnp.where(row_head == h, sink_ref[g * Q_PER_KV + h], col)
        sink.append(col)
    m = [jnp.maximum(jnp.max(s[g], axis=-1, keepdims=True), sink[g]) for g in groups]
    p = [jnp.exp(s[g] - m[g]) for g in groups]
    denom = [jnp.sum(p[g], axis=-1, keepdims=True) + jnp.exp(sink[g] - m[g]) for g in groups]
    o = [_dot((p[g] / denom[g]).astype(BF16), kv_cols(D_KV, g)) for g in groups]
    o_ref[...] = jnp.concatenate([o[g][h * WINDOW:(h + 1) * WINDOW, :] for g in groups for h in range(Q_PER_KV)],
                                 axis=1)


def _prompt_attention(h_attn, sinks):
    nb = SEQ // WINDOW
    return pl.pallas_call(
        _prompt_attn_kernel,
        out_shape=jax.ShapeDtypeStruct((SEQ, D_ATTN), F32),
        grid=(nb,),
        in_specs=[pl.BlockSpec((WINDOW, D_ATTN), lambda i: (i, 0)),
                  pl.BlockSpec((WINDOW, 2 * D_KV), lambda i: (jnp.maximum(i - 1, 0), 2)),
                  pl.BlockSpec((WINDOW, 2 * D_KV), lambda i: (i, 2)),
                  pl.BlockSpec((1, Q_PER_KV * WINDOW, 2 * WINDOW), lambda i: (jnp.minimum(i, 1), 0, 0)),
                  pl.BlockSpec(memory_space=pltpu.SMEM)],
        out_specs=pl.BlockSpec((WINDOW, D_ATTN), lambda i: (i, 0)),
        compiler_params=_cparams(("arbitrary",)),
        name="prompt_attention",
    )(h_attn, h_attn, h_attn, _band_bias(), sinks)


SAMPLE_ATTN_TILE = 8


def _sample_attn_kernel(q_ref, knew_ref, vnew_ref, ck_ref, cv_ref, sink_ref, o_ref, kwin_ref, vwin_ref):
    lane = lax.broadcasted_iota(jnp.int32, (N_Q_HEADS, D_KV), 1)
    head = lax.broadcasted_iota(jnp.int32, (N_Q_HEADS, D_KV), 0)
    group_mask = _div_pow2(lane, HEAD_DIM) == _div_pow2(head, Q_PER_KV)
    sink = sink_ref[...]
    row = lax.broadcasted_iota(jnp.int32, (WINDOW, D_KV), 0)
    seqs = range(SAMPLE_ATTN_TILE)
    qbd = [jnp.where(group_mask, jnp.concatenate([q_ref[b]] * N_KV_HEADS, axis=1), 0.0).astype(BF16) for b in seqs]
    s = [_dot1(qbd[b], ck_ref[b]) * ATTN_SCALE for b in seqs]
    s_new = [jnp.sum(qbd[b].astype(F32) * knew_ref[b].astype(BF16).astype(F32), axis=-1, keepdims=True) * ATTN_SCALE
             for b in seqs]
    m = [jnp.maximum(jnp.maximum(jnp.max(s[b], axis=-1, keepdims=True), s_new[b]), sink) for b in seqs]
    p = [jnp.exp(s[b] - m[b]) for b in seqs]
    p_new = [jnp.exp(s_new[b] - m[b]) for b in seqs]
    denom = [jnp.sum(p[b], axis=-1, keepdims=True) + p_new[b] + jnp.exp(sink - m[b]) for b in seqs]
    for b in seqs:
        kb = ck_ref[b].T
        vb = cv_ref[b].T
        kn = knew_ref[b]
        vn = vnew_ref[b]
        o_full = (_dot1(p[b] / denom[b], vb)
                  + (p_new[b] / denom[b]).astype(BF16).astype(F32) * vn.astype(BF16).astype(F32))
        o_full = jnp.where(group_mask, o_full, 0.0)
        o = o_full[:, 0:HEAD_DIM]
        for g in range(1, N_KV_HEADS):
            o = o + o_full[:, g * HEAD_DIM:(g + 1) * HEAD_DIM]
        o_ref[b] = o
        kwin_ref[b] = jnp.where(row == WINDOW - 1, kn, pltpu.roll(kb, WINDOW - 1, axis=0))
        vwin_ref[b] = jnp.where(row == WINDOW - 1, vn, pltpu.roll(vb, WINDOW - 1, axis=0))


def _sample_attention(q, k_new, v_new, cache_k_t, cache_v_t, sinks):
    bt = SAMPLE_ATTN_TILE
    win_spec = pl.BlockSpec((bt, WINDOW, D_KV), lambda i: (i, 0, 0))
    win_t_spec = pl.BlockSpec((bt, D_KV, WINDOW), lambda i: (i, 0, 0))
    new_spec = pl.BlockSpec((bt, 1, D_KV), lambda i: (i, 0, 0))
    return pl.pallas_call(
        _sample_attn_kernel,
        out_shape=(jax.ShapeDtypeStruct((DEC_BATCH, N_Q_HEADS, HEAD_DIM), F32),
                   jax.ShapeDtypeStruct((DEC_BATCH, WINDOW, D_KV), F32),
                   jax.ShapeDtypeStruct((DEC_BATCH, WINDOW, D_KV), F32)),
        grid=(DEC_BATCH // bt,),
        in_specs=[pl.BlockSpec((bt, N_Q_HEADS, HEAD_DIM), lambda i: (i, 0, 0)),
                  new_spec, new_spec, win_t_spec, win_t_spec,
                  pl.BlockSpec((N_Q_HEADS, 1), lambda i: (0, 0))],
        out_specs=(pl.BlockSpec((bt, N_Q_HEADS, HEAD_DIM), lambda i: (i, 0, 0)), win_spec, win_spec),
        compiler_params=_cparams(("arbitrary",)),
        name="sample_attention",
    )(q, k_new, v_new, cache_k_t, cache_v_t, sinks)


def _head_ones():
    r = _div_pow2(lax.broadcasted_iota(jnp.int32, (LANES, LANES), 0), HEAD_DIM)
    c = _div_pow2(lax.broadcasted_iota(jnp.int32, (LANES, LANES), 1), HEAD_DIM)
    return jnp.where(r == c, 1.0, 0.0).astype(BF16)


def _head_sum(x, ones, passes=2):
    dot = _dot_exact_rhs if passes == 2 else _dot1
    parts = [dot(x[:, p * LANES:(p + 1) * LANES], ones) for p in range(x.shape[1] // LANES)]
    return jnp.concatenate(parts, axis=1)


def _token_mix(feat, shifted, mu):
    return feat + (shifted - feat) * mu


def _rwkv_prep(mixed, mixed_tail, w0, a0, k_k, k_a, wd, wa, wg, ones):
    r = mixed[:, 0:D_RWKV]
    k = mixed[:, D_RWKV:2 * D_RWKV]
    v = mixed[:, 2 * D_RWKV:3 * D_RWKV]
    xw = mixed_tail[:, 0:W_LORA]
    xa = mixed_tail[:, W_LORA:W_LORA + A_LORA]
    xg = mixed_tail[:, W_LORA + A_LORA:D_TAIL]
    w_log = -_softplus(-(w0 + _dot1(jnp.tanh(xw), wd))) - 0.5
    log_decay = -jnp.exp(w_log)
    a = _sigmoid(a0 + _dot1(xa, wa))
    g = _dot1(_sigmoid(xg), wg)
    kk = k * k_k
    kk = kk * lax.rsqrt(jnp.maximum(_head_sum(kk * kk, ones), 1e-24))
    k2 = k * (1.0 + (a - 1.0) * k_a)
    return r, log_decay, k2, v, -kk, kk * a, g


def _rwkv_post(y, r, k2, v, g, r_k, gn_g, gn_b, ones, passes=2):
    inv_n = 1.0 / HEAD_DIM
    mu = _head_sum(y, ones, passes) * inv_n
    d = y - mu
    var = _head_sum(d * d, ones, passes) * inv_n
    yn = d * lax.rsqrt(var + GN_EPS) * gn_g + gn_b
    bonus = _head_sum(r * k2 * r_k, ones, passes) * v
    return (yn + bonus) * g


(OP_AABS, OP_RABS, OP_AN, OP_RN, OP_BN, OP_KN, OP_BH, OP_KH, OP_V) = range(9)
N_OPS = 9


def _prompt_rwkv_kernel(f1_ref, f2_ref, tail_in_ref, mu_ref, mut_ref, w0_ref, a0_ref, kk_ref, ka_ref, rk_ref,
                        gng_ref, gnb_ref, wd_ref, wa_ref, wg_ref, out_ref, state_ref,
                        prev_ref, prevt_ref, s_ref, ops_ref, pc_ref, y_ref):
    c = pl.program_id(0)
    C = CHUNK

    @pl.when(c == 0)
    def _():
        prev_ref[...] = jnp.zeros_like(prev_ref)
        prevt_ref[...] = jnp.zeros_like(prevt_ref)
        s_ref[...] = jnp.zeros_like(s_ref)

    ones = _head_ones()
    row = lax.broadcasted_iota(jnp.int32, (C, 1), 0)

    def token_shift(feat, carry_ref):
        shifted = jnp.where(row == 0, carry_ref[0:1, :], pltpu.roll(feat, 1, axis=0))
        carry_ref[0:1, :] = feat[C - 1:C, :]
        return shifted

    feat = jnp.concatenate([f1_ref[...], f2_ref[...]], axis=1)
    tail = tail_in_ref[...]
    mixed = _token_mix(feat, token_shift(feat, prev_ref), mu_ref[...])
    mixed_tail = _token_mix(tail, token_shift(tail, prevt_ref), mut_ref[...])
    r, ld, k2, v, av, bv, g = _rwkv_prep(mixed, mixed_tail, w0_ref[...], a0_ref[...], kk_ref[...], ka_ref[...],
                                         wd_ref[...], wa_ref[...], wg_ref[...], ones)

    ti = lax.broadcasted_iota(jnp.int32, (C, C), 0)
    tj = lax.broadcasted_iota(jnp.int32, (C, C), 1)
    tri_incl = jnp.where(tj <= ti, 1.0, 0.0).astype(BF16)
    cs = _dot_exact_lhs(tri_incl, ld)
    cs_ref = cs[C // 2 - 1:C // 2, :]
    cs_end = cs[C - 1:C, :]
    e_prev = jnp.exp(cs - ld)
    e_cur = jnp.exp(cs)
    n_prev = jnp.exp(cs - ld - cs_ref)
    n_cur = jnp.exp(cs - cs_ref)
    n_inv = jnp.exp(cs_ref - cs)
    e_tail = jnp.exp(cs_end - cs)
    ops = {OP_AABS: av * e_prev, OP_RABS: r * e_cur, OP_AN: av * n_prev, OP_RN: r * n_cur,
           OP_BN: bv * n_inv, OP_KN: k2 * n_inv, OP_BH: bv * e_tail, OP_KH: k2 * e_tail, OP_V: v}
    p_end = jnp.exp(cs_end)
    for p in range(N_PAIRS):
        sl = slice(p * LANES, (p + 1) * LANES)
        for idx, val in ops.items():
            ops_ref[p, idx] = val[:, sl]
        pc_ref[p] = jnp.broadcast_to(p_end[:, sl], (SUBLANES, LANES))

    lane1 = lax.broadcasted_iota(jnp.int32, (C, LANES), 1)
    head0 = lane1 < HEAD_DIM
    r2 = lax.broadcasted_iota(jnp.int32, (2 * C, 2 * C), 0)
    c2 = lax.broadcasted_iota(jnp.int32, (2 * C, 2 * C), 1)
    tq = _mod_pow2(r2, C)
    tk = _mod_pow2(c2, C)
    band = (tk < tq) | ((tk == tq) & (r2 >= C))
    blockdiag = _div_pow2(r2, HEAD_DIM) == _div_pow2(c2, HEAD_DIM)

    op = lambda p, idx: ops_ref[p, idx]
    zero_half = jnp.zeros((C, LANES), F32)
    for pairs in [range(g, g + PAIR_GROUP) for g in range(0, N_PAIRS, PAIR_GROUP)]:
        gy = {p: _dot1(jnp.concatenate([op(p, OP_AABS), op(p, OP_RABS)], axis=0), s_ref[p]) for p in pairs}

        am0, am1 = {}, {}
        for p in pairs:
            a_n, r_n = op(p, OP_AN), op(p, OP_RN)
            b0, k0 = jnp.where(head0, op(p, OP_BN), 0.0), jnp.where(head0, op(p, OP_KN), 0.0)
            b1, k1 = jnp.where(head0, 0.0, op(p, OP_BN)), jnp.where(head0, 0.0, op(p, OP_KN))
            am = _dot1(jnp.concatenate([a_n, r_n], axis=0), jnp.concatenate([k0, b0, b1, k1], axis=0), NT)
            am0[p] = jnp.where(band, am[:, 0:2 * C], 0.0)
            am1[p] = jnp.where(band, am[:, 2 * C:4 * C], 0.0)

        w0, w1 = {}, {}
        for p in pairs:
            top0, top1 = am0[p][0:C], am1[p][0:C]
            ak = jnp.concatenate([jnp.where(head0, top0, 0.0), jnp.where(head0, 0.0, top1)], axis=0)
            vv = op(p, OP_V)
            g0 = gy[p][0:C]
            m = jnp.concatenate([g0, g0], axis=0) + _dot1(ak, jnp.concatenate([vv, vv], axis=0))
            w0[p] = jnp.where(head0, m[0:C], top0)
            w1[p] = jnp.where(head0, top1, m[C:2 * C])

        for lvl in range(SOLVE_LEVELS):
            prod0 = {p: _dot1(w0[p], jnp.concatenate([zero_half, w0[p]], axis=0)) for p in pairs}
            prod1 = {p: _dot1(w1[p], jnp.concatenate([w1[p], zero_half], axis=0)) for p in pairs}
            w0 = {p: jnp.where(head0, w0[p] + prod0[p], prod0[p]) for p in pairs}
            w1 = {p: jnp.where(head0, prod1[p], w1[p] + prod1[p]) for p in pairs}
        u = {p: jnp.where(head0, w0[p], w1[p]) for p in pairs}

        for p in pairs:
            vv = op(p, OP_V)
            y_lhs = jnp.concatenate([am0[p][C:2 * C], am1[p][C:2 * C]], axis=1)
            y_rhs = jnp.concatenate([jnp.where(head0, vv, 0.0), jnp.where(head0, u[p], 0.0),
                                     jnp.where(head0, 0.0, u[p]), jnp.where(head0, 0.0, vv)], axis=0)
            y_ref[p] = gy[p][C:2 * C] + _dot1(y_lhs, y_rhs)

        for p in pairs:
            decay_rows = jnp.broadcast_to(pc_ref[p][0:1, :], (LANES, LANES)).T
            upd_lhs = jnp.concatenate([op(p, OP_BH), op(p, OP_KH)], axis=0).T
            upd_rhs = jnp.concatenate([u[p], op(p, OP_V)], axis=0)
            s_ref[p] = s_ref[p] * decay_rows + jnp.where(blockdiag, _dot1(upd_lhs, upd_rhs), 0.0)

    y = jnp.concatenate([y_ref[p] for p in range(N_PAIRS)], axis=1)
    out_ref[...] = _rwkv_post(y, r, k2, v, g, rk_ref[...], gng_ref[...], gnb_ref[...], ones, passes=1)

    @pl.when(c == pl.num_programs(0) - 1)
    def _():
        state_ref[...] = s_ref[...]


def _prompt_rwkv(h_main, tail, prm):
    n_chunks = SEQ // CHUNK
    half = D_RKV // 2
    assert D_QKV == half
    vec = pl.BlockSpec((1, D_RWKV), lambda c: (0, 0))
    full = lambda a: pl.BlockSpec(a.shape, lambda c: (0,) * a.ndim)
    return pl.pallas_call(
        _prompt_rwkv_kernel,
        out_shape=(jax.ShapeDtypeStruct((SEQ, D_RWKV), F32),
                   jax.ShapeDtypeStruct((N_PAIRS, LANES, LANES), F32)),
        grid=(n_chunks,),
        in_specs=[pl.BlockSpec((CHUNK, half), lambda c: (c, 1)),
                  pl.BlockSpec((CHUNK, half), lambda c: (c, 2)),
                  pl.BlockSpec((CHUNK, D_TAIL), lambda c: (c, 0)),
                  full(prm["mu"]), full(prm["mu_tail"]),
                  vec, vec, vec, vec, vec, vec, vec,
                  full(prm["wd"]), full(prm["wa"]), full(prm["wg"])],
        out_specs=(pl.BlockSpec((CHUNK, D_RWKV), lambda c: (c, 0)),
                   pl.BlockSpec((N_PAIRS, LANES, LANES), lambda c: (0, 0, 0))),
        scratch_shapes=[pltpu.VMEM((SUBLANES, D_RKV), F32),
                        pltpu.VMEM((SUBLANES, D_TAIL), F32),
                        pltpu.VMEM((N_PAIRS, LANES, LANES), F32),
                        pltpu.VMEM((N_PAIRS, N_OPS, CHUNK, LANES), F32),
                        pltpu.VMEM((N_PAIRS, SUBLANES, LANES), F32),
                        pltpu.VMEM((N_PAIRS, CHUNK, LANES), F32)],
        compiler_params=_cparams(("arbitrary",)),
        name="prompt_rwkv",
    )(h_main, h_main, tail, prm["mu"], prm["mu_tail"], prm["w0"], prm["a0"], prm["k_k"], prm["k_a"],
      prm["r_k"], prm["gn_g"], prm["gn_b"], prm["wd"], prm["wa"], prm["wg"])


def _sample_prep_kernel(h_ref, x_ref, wt_ref, shift_ref, mu_ref, mut_ref, w0_ref, a0_ref, kk_ref, ka_ref,
                        wd_ref, wa_ref, wg_ref, r_ref, k_ref, v_ref, g_ref, tail_ref,
                        rt_ref, wtr_ref, kt_ref, vt_ref, at_ref, bt_ref):
    ones = _head_ones()
    feat = h_ref[:, D_QKV:D_MAIN]
    tail = _dot1(x_ref[...], wt_ref[...], NT)
    tail_ref[...] = tail
    mixed = _token_mix(feat, shift_ref[:, 0:D_RKV], mu_ref[...])
    mixed_tail = _token_mix(tail, shift_ref[:, D_RKV:D_SHIFT], mut_ref[...])
    r, ld, k2, v, av, bv, g = _rwkv_prep(mixed, mixed_tail, w0_ref[...], a0_ref[...], kk_ref[...], ka_ref[...],
                                         wd_ref[...], wa_ref[...], wg_ref[...], ones)
    r_ref[...] = r
    k_ref[...] = k2
    v_ref[...] = v
    g_ref[...] = g
    rt_ref[...] = r.T
    wtr_ref[...] = jnp.exp(ld).T
    kt_ref[...] = k2.T
    vt_ref[...] = v.T
    at_ref[...] = av.T
    bt_ref[...] = bv.T


def _sample_prep(h_main, x, shift, prm):
    tok = jax.ShapeDtypeStruct((DEC_BATCH, D_RWKV), F32)
    chan = jax.ShapeDtypeStruct((D_RWKV, DEC_BATCH), F32)
    return pl.pallas_call(
        _sample_prep_kernel,
        out_shape=(tok,) * 4 + (jax.ShapeDtypeStruct((DEC_BATCH, D_TAIL), F32),) + (chan,) * 6,
        compiler_params=pltpu.CompilerParams(vmem_limit_bytes=VMEM_LIMIT),
        name="sample_rwkv_prep",
    )(h_main, x, prm["w_tail"], shift, prm["mu"], prm["mu_tail"], prm["w0"], prm["a0"], prm["k_k"], prm["k_a"],
      prm["wd"], prm["wa"], prm["wg"])


STEP_GROUP = 4


def _sample_step_kernel(s_ref, r_ref, w_ref, k_ref, a_ref, b_ref, v_ref, y_ref, snew_ref):
    r, w, k, a, b = r_ref[...], w_ref[...], k_ref[...], a_ref[...], b_ref[...]
    for g0 in range(0, HEAD_DIM, 2 * STEP_GROUP):
        chans = range(g0, g0 + 2 * STEP_GROUP)
        sa = {i: jnp.sum(s_ref[0, i] * a, axis=0, keepdims=True) for i in chans}
        s_new = {i: s_ref[0, i] * w + sa[i] * b + v_ref[i:i + 1, :] * k for i in chans}
        for i in chans:
            y_ref[i:i + 1, :] = jnp.sum(s_new[i] * r, axis=0, keepdims=True)
        for i in range(g0, g0 + 2 * STEP_GROUP, 2):
            pair = jnp.concatenate([s_new[i], s_new[i + 1]], axis=0)
            snew_ref[:, i * HEAD_DIM:(i + 2) * HEAD_DIM] = pair.T


def _sample_step(state_t, r_t, w_t, k_t, a_t, b_t, v_t):
    head_rows = pl.BlockSpec((HEAD_DIM, DEC_BATCH), lambda h: (h, 0))
    return pl.pallas_call(
        _sample_step_kernel,
        out_shape=(jax.ShapeDtypeStruct((D_RWKV, DEC_BATCH), F32),
                   jax.ShapeDtypeStruct((DEC_BATCH, N_RWKV_HEADS * HEAD_DIM * HEAD_DIM), F32)),
        grid=(N_RWKV_HEADS,),
        in_specs=[pl.BlockSpec((1, HEAD_DIM, HEAD_DIM, DEC_BATCH), lambda h: (h, 0, 0, 0))] + [head_rows] * 6,
        out_specs=(head_rows, pl.BlockSpec((DEC_BATCH, HEAD_DIM * HEAD_DIM), lambda h: (0, h))),
        compiler_params=_cparams(("arbitrary",)),
        name="sample_rwkv_step",
    )(state_t, r_t, w_t, k_t, a_t, b_t, v_t)


def _sample_post_kernel(yt_ref, r_ref, k_ref, v_ref, g_ref, rk_ref, gng_ref, gnb_ref, o_ref):
    o_ref[...] = _rwkv_post(yt_ref[...].T, r_ref[...], k_ref[...], v_ref[...], g_ref[...], rk_ref[...],
                            gng_ref[...], gnb_ref[...], _head_ones())


def _sample_post(y, r, k, v, g, prm):
    return pl.pallas_call(
        _sample_post_kernel,
        out_shape=jax.ShapeDtypeStruct((DEC_BATCH, D_RWKV), F32),
        compiler_params=pltpu.CompilerParams(vmem_limit_bytes=VMEM_LIMIT),
        name="sample_rwkv_post",
    )(y, r, k, v, g, prm["r_k"], prm["gn_g"], prm["gn_b"])


def _project_mix(attn_ref, rwkv_ref, wo_ref):
    return (_dot(attn_ref[...].astype(BF16), wo_ref[0:D_ATTN, :])
            + _dot(rwkv_ref[...].astype(BF16), wo_ref[D_ATTN:D_ATTN + D_RWKV, :]))


def _norm_and_route(mix, x_ref, g_ref, b_ref, wr_ref, br_ref, x1_ref, x1b_ref, route_ref):
    x1 = _layer_norm(ALPHA * x_ref[...] + mix, g_ref[...], b_ref[...])
    x1_ref[...] = x1
    x1b = x1.astype(BF16)
    x1b_ref[...] = _pack_bf16_halves(x1b)
    logits = _dot(x1b, wr_ref[...].astype(BF16)) + br_ref[...]
    tm = logits.shape[0]
    lane = lax.broadcasted_iota(jnp.int32, (tm, LANES), 1).astype(F32)
    big = float(2 * LANES)
    neg = -jnp.inf
    lc = jnp.where(lane < N_GROUPS, logits, neg)
    mc = jnp.max(lc, axis=-1, keepdims=True)
    g_sel = jnp.min(jnp.where(lc == mc, lane, big), axis=-1, keepdims=True)
    p_group = 1.0 / jnp.sum(jnp.exp(lc - mc), axis=-1, keepdims=True)
    lo = ROUTE_FINE_OFF + g_sel * EXPERTS_PER_GROUP
    lf = jnp.where((lane >= lo) & (lane < lo + EXPERTS_PER_GROUP), logits, neg)
    v1 = jnp.max(lf, axis=-1, keepdims=True)
    i1 = jnp.min(jnp.where(lf == v1, lane, big), axis=-1, keepdims=True)
    lf2 = jnp.where(lane == i1, neg, lf)
    v2 = jnp.max(lf2, axis=-1, keepdims=True)
    i2 = jnp.min(jnp.where(lf2 == v2, lane, big), axis=-1, keepdims=True)
    e21 = jnp.exp(v2 - v1)
    gate1 = p_group / (1.0 + e21)
    gate2 = p_group * e21 / (1.0 + e21)
    route = jnp.where(lane == 0, i1 - ROUTE_FINE_OFF,
                      jnp.where(lane == 1, i2 - ROUTE_FINE_OFF,
                                jnp.where(lane == 2, gate1, jnp.where(lane == 3, gate2, 0.0))))
    route_ref[...] = route


N_ROUTER_OUTS = 3


def _outproj_router_kernel(n_tiles, n_aliased, attn_ref, rwkv_ref, x_ref, wo_ref, g_ref, b_ref, wr_ref, br_ref,
                           *rest):
    outs = rest[n_aliased:n_aliased + N_ROUTER_OUTS]
    mix_ref = rest[-1]
    i = pl.program_id(0)
    finish = lambda mix: _norm_and_route(mix, x_ref, g_ref, b_ref, wr_ref, br_ref, *outs)

    @pl.when(i == 0)
    def _():
        mix_ref[...] = _project_mix(attn_ref, rwkv_ref, wo_ref)

    @pl.when((i >= 1) & (i < n_tiles))
    def _():
        finish(mix_ref[...])
        mix_ref[...] = _project_mix(attn_ref, rwkv_ref, wo_ref)

    @pl.when(i == n_tiles)
    def _():
        finish(mix_ref[...])

    @pl.when(i > n_tiles)
    def _():
        for out_ref in outs:
            out_ref[...] = jnp.zeros_like(out_ref)


def _outproj_router(attn, rwkv, x, wo_bf16, ln_g, ln_b, w_route, b_route, tm, n_total, row_block, into, name):
    m = x.shape[0]
    n_tiles = m // tm
    const = lambda shape: pl.BlockSpec(shape, lambda i: (0, 0))
    ahead = lambda width: pl.BlockSpec((tm, width), lambda i: (jnp.minimum(i, n_tiles - 1), 0))
    behind = lambda width: pl.BlockSpec((tm, width), lambda i: (jnp.clip(i - 1, 0, n_tiles - 1), 0))
    in_specs = [ahead(D_ATTN), ahead(D_RWKV), behind(D_MODEL),
                const((D_MODEL, D_MODEL)), const((1, D_MODEL)), const((1, D_MODEL)),
                const((D_MODEL, LANES)), const((1, LANES))]
    args = [attn, rwkv, x, wo_bf16, ln_g, ln_b, w_route, b_route]
    aliases, n_aliased, fill_steps = {}, 0, pl.cdiv(n_total - m, tm)
    if into is not None:
        n_aliased, fill_steps = N_ROUTER_OUTS, 0
        in_specs += [pl.BlockSpec(memory_space=pl.ANY)] * N_ROUTER_OUTS
        aliases = {len(args) + k: k for k in range(N_ROUTER_OUTS)}
        args += list(into)
    out_rows = lambda width: pl.BlockSpec((tm, width), lambda i: (jnp.maximum(i - 1, 0) + row_block, 0))
    return pl.pallas_call(
        functools.partial(_outproj_router_kernel, n_tiles, n_aliased),
        out_shape=(jax.ShapeDtypeStruct((n_total, D_MODEL), F32),
                   jax.ShapeDtypeStruct((n_total, D_MODEL // 2), jnp.uint32),
                   jax.ShapeDtypeStruct((n_total, LANES), F32)),
        grid=(n_tiles + 1 + fill_steps,),
        in_specs=in_specs,
        out_specs=(out_rows(D_MODEL), out_rows(D_MODEL // 2), out_rows(LANES)),
        scratch_shapes=[pltpu.VMEM((tm, D_MODEL), F32)],
        input_output_aliases=aliases,
        compiler_params=_cparams(("arbitrary",)),
        name=name,
    )(*args)


DISPATCH_TILE = 128


def _dispatch_kernel(zoff_ref, dest_ref, x_ref, o_hbm, zbuf, ring, zsem, sem):
    i = pl.program_id(0)
    n_blocks = o_hbm.shape[0] // MOE_BLOCK
    n_used = zoff_ref[N_EXPERTS]

    def zero_fill(start_row):
        start_row = pl.multiple_of(start_row, MOE_BLOCK)
        return pltpu.make_async_copy(zbuf, o_hbm.at[pl.ds(start_row, MOE_BLOCK)], zsem)

    def zero_fills(action):
        for e in range(N_EXPERTS):
            @pl.when(zoff_ref[e] >= 0)
            def _():
                action(zero_fill(zoff_ref[e]))
        for b in range(n_blocks):
            @pl.when(b >= n_used)
            def _():
                action(zero_fill(b * MOE_BLOCK))

    @pl.when(i == 0)
    def _():
        zbuf[...] = jnp.zeros_like(zbuf)
        zero_fills(lambda copy: copy.start())
        zero_fills(lambda copy: copy.wait())

    cur = lax.rem(i, 2)

    def wait_rows(slot):
        for k in range(2):
            pltpu.make_async_copy(ring.at[slot], o_hbm.at[pl.ds(0, DISPATCH_TILE)], sem.at[slot]).wait()

    @pl.when(i >= 2)
    def _():
        wait_rows(cur)

    ring[cur] = x_ref[...]
    for t in range(DISPATCH_TILE):
        for k in range(2):
            pltpu.make_async_copy(ring.at[cur, pl.ds(t, 1)], o_hbm.at[pl.ds(dest_ref[0, 0, 2 * t + k], 1)],
                                  sem.at[cur]).start()

    @pl.when(i == pl.num_programs(0) - 1)
    def _():
        wait_rows(cur)

        @pl.when(i >= 1)
        def _():
            wait_rows(1 - cur)


def _dispatch(zero_offsets, dest, x_packed, n_blocks):
    n_tokens, width = x_packed.shape
    grid_spec = pltpu.PrefetchScalarGridSpec(
        num_scalar_prefetch=1,
        grid=(n_tokens // DISPATCH_TILE,),
        in_specs=[pl.BlockSpec((1, 1, 2 * DISPATCH_TILE), lambda i, z: (i, 0, 0), memory_space=pltpu.SMEM),
                  pl.BlockSpec((DISPATCH_TILE, width), lambda i, z: (i, 0))],
        out_specs=pl.BlockSpec(memory_space=pl.ANY),
        scratch_shapes=[pltpu.VMEM((MOE_BLOCK, width), x_packed.dtype),
                        pltpu.VMEM((2, DISPATCH_TILE, width), x_packed.dtype),
                        pltpu.SemaphoreType.DMA, pltpu.SemaphoreType.DMA((2,))],
    )
    return pl.pallas_call(
        _dispatch_kernel,
        out_shape=jax.ShapeDtypeStruct((n_blocks * MOE_BLOCK, width), x_packed.dtype),
        grid_spec=grid_spec,
        compiler_params=_cparams(("arbitrary",)),
        name="moe_dispatch",
    )(zero_offsets, dest.reshape(-1, 1, 2 * DISPATCH_TILE), x_packed)


def _expert_kernel(be_ref, nb_ref, x_ref, wg_hbm, wu_hbm, wd_hbm, o_ref, wg_buf, wu_buf, wd_buf, slot_ref, sem):
    blk = pl.program_id(0)
    n_used = nb_ref[0]
    expert = be_ref[blk]
    is_first = (blk == 0) | (be_ref[jnp.maximum(blk - 1, 0)] != expert)

    def fetch(e, slot):
        return [pltpu.make_async_copy(hbm.at[e], buf.at[slot], sem.at[slot, i])
                for i, (hbm, buf) in enumerate(((wg_hbm, wg_buf), (wu_hbm, wu_buf), (wd_hbm, wd_buf)))]

    @pl.when((blk < n_used) & is_first)
    def _():
        @pl.when(blk == 0)
        def _():
            slot_ref[0] = 1
            for copy in fetch(expert, 0):
                copy.start()

        slot = 1 - slot_ref[0]
        slot_ref[0] = slot
        for copy in fetch(expert, slot):
            copy.wait()
        nxt = lax.while_loop(lambda j: (j < n_used) & (be_ref[jnp.minimum(j, n_used - 1)] == expert),
                             lambda j: j + 1, blk + 1)

        @pl.when(nxt < n_used)
        def _():
            for copy in fetch(be_ref[jnp.minimum(nxt, n_used - 1)], 1 - slot):
                copy.start()

    @pl.when(blk < n_used)
    def _():
        slot = slot_ref[0]
        half = D_MODEL // 2
        x_head, x_tail = _unpack_bf16_halves(x_ref[...])
        proj = lambda w_buf: (_dot(x_head, w_buf[slot, 0:half, :].astype(BF16))
                              + _dot(x_tail, w_buf[slot, half:D_MODEL, :].astype(BF16)))
        gate = proj(wg_buf)
        up = proj(wu_buf)
        h = gate * _sigmoid(gate) * up
        o_ref[...] = _dot(h.astype(BF16), wd_buf[slot].astype(BF16))

    @pl.when(blk >= n_used)
    def _():
        o_ref[...] = jnp.zeros_like(o_ref)


def _expert_mlp(block_expert, n_used, x_sorted, w_gate, w_up, w_down, n_blocks):
    grid_spec = pltpu.PrefetchScalarGridSpec(
        num_scalar_prefetch=2,
        grid=(n_blocks,),
        in_specs=[pl.BlockSpec((MOE_BLOCK, D_MODEL // 2), lambda b, be, nb: (jnp.minimum(b, nb[0] - 1), 0)),
                  pl.BlockSpec(memory_space=pl.ANY), pl.BlockSpec(memory_space=pl.ANY),
                  pl.BlockSpec(memory_space=pl.ANY)],
        out_specs=pl.BlockSpec((MOE_BLOCK, D_MODEL), lambda b, be, nb: (b, 0)),
        scratch_shapes=[pltpu.VMEM((2, D_MODEL, D_EXPERT), F32), pltpu.VMEM((2, D_MODEL, D_EXPERT), F32),
                        pltpu.VMEM((2, D_EXPERT, D_MODEL), F32), pltpu.SMEM((1,), jnp.int32),
                        pltpu.SemaphoreType.DMA((2, 3))],
    )
    return pl.pallas_call(
        _expert_kernel,
        out_shape=jax.ShapeDtypeStruct((n_blocks * MOE_BLOCK, D_MODEL), F32),
        grid_spec=grid_spec,
        compiler_params=_cparams(("arbitrary",)),
        name="expert_mlp",
    )(block_expert, n_used, x_sorted, w_gate, w_up, w_down)


COMBINE_TILE = 256


def _combine_kernel(dest_ref, dest_next_ref, y_hbm, x1_ref, route_ref, g_ref, b_ref, o_ref, ybuf, sem):
    i = pl.program_id(0)
    cur = lax.rem(i, 2)
    tile = x1_ref.shape[0]
    n_rows = 2 * tile

    def gather(table_ref, buf):
        for slot in range(n_rows):
            pltpu.make_async_copy(y_hbm.at[pl.ds(table_ref[0, 0, slot], 1)], ybuf.at[buf, pl.ds(slot, 1)],
                                  sem.at[buf]).start()

    def wait_gather(buf):
        pltpu.make_async_copy(y_hbm.at[pl.ds(0, n_rows)], ybuf.at[buf], sem.at[buf]).wait()

    @pl.when(i == 0)
    def _():
        gather(dest_ref, 0)

    gather(dest_next_ref, 1 - cur)
    wait_gather(cur)
    route = route_ref[...]
    yb = ybuf[cur]
    moe = route[:, 2:3] * yb[0:tile, :] + route[:, 3:4] * yb[tile:n_rows, :]
    o_ref[...] = _layer_norm(ALPHA * x1_ref[...] + moe, g_ref[...], b_ref[...])

    @pl.when(i == pl.num_programs(0) - 1)
    def _():
        wait_gather(1 - cur)


def _combine(dest, y_slots, x1_all, route_all, m, tm, row_block, ln_g, ln_b, name):
    return pl.pallas_call(
        _combine_kernel,
        out_shape=jax.ShapeDtypeStruct((m, D_MODEL), F32),
        grid=(m // tm,),
        in_specs=[pl.BlockSpec((1, 1, 2 * tm), lambda i: (i, 0, 0), memory_space=pltpu.SMEM),
                  pl.BlockSpec((1, 1, 2 * tm), lambda i: (i + 1, 0, 0), memory_space=pltpu.SMEM),
                  pl.BlockSpec(memory_space=pl.ANY),
                  pl.BlockSpec((tm, D_MODEL), lambda i: (i + row_block, 0)),
                  pl.BlockSpec((tm, LANES), lambda i: (i + row_block, 0)),
                  pl.BlockSpec((1, D_MODEL), lambda i: (0, 0)),
                  pl.BlockSpec((1, D_MODEL), lambda i: (0, 0))],
        out_specs=pl.BlockSpec((tm, D_MODEL), lambda i: (i, 0)),
        scratch_shapes=[pltpu.VMEM((2, 2 * tm, D_MODEL), F32), pltpu.SemaphoreType.DMA((2,))],
        compiler_params=_cparams(("arbitrary",)),
        name=name,
    )(dest, dest, y_slots, x1_all, route_all, ln_g, ln_b)


def _dispatch_plan(route_all, n_blocks):
    flat_e = route_all[:, 0:2].astype(jnp.int32).reshape(-1)
    onehot = (flat_e[:, None] == jnp.arange(N_EXPERTS, dtype=jnp.int32)[None, :]).astype(jnp.int32)
    csum = jnp.cumsum(onehot, axis=0)
    rank = jnp.sum(onehot * csum, axis=1) - 1
    counts = csum[-1]
    padded = (counts + MOE_BLOCK - 1) // MOE_BLOCK * MOE_BLOCK
    pend = jnp.cumsum(padded)
    pstart = pend - padded
    dest = (pstart[flat_e] + rank).astype(jnp.int32)
    zero_offsets = jnp.where(counts > 0, pend - MOE_BLOCK, -1).astype(jnp.int32)
    n_used = (pend[-1] // MOE_BLOCK).astype(jnp.int32)
    block_start = jnp.minimum(jnp.arange(n_blocks, dtype=jnp.int32), n_used - 1) * MOE_BLOCK
    block_e = jnp.minimum(jnp.searchsorted(pend, block_start, side="right"), N_EXPERTS - 1).astype(jnp.int32)
    return dest, jnp.concatenate([zero_offsets, n_used.reshape(1)]), block_e, n_used.reshape(1)


def kernel(x_prompt, x_sample, cache_k_win, cache_v_win, state_wkv, state_shift, w_in, attn_sinks, shift_mu, w0,
           w_decay_up, a0, w_a_up, w_g_up, k_k, k_a, r_k, gn_g, gn_b, w_out, ln1_g, ln1_b, w_coarse, b_coarse,
           w_fine, b_fine, w_exp_gate, w_exp_up, w_exp_down, ln2_g, ln2_b):
    xp = x_prompt[0]
    xs = x_sample[:, 0]
    row = lambda a: a.reshape(1, -1)

    w_in_t = jnp.swapaxes(w_in[0], 0, 1)
    prm = dict(mu=row(shift_mu[0, :D_RKV]), mu_tail=row(shift_mu[0, D_RKV:]), w_tail=w_in_t[D_MAIN:].astype(BF16),
               w0=row(w0[0]), a0=row(a0[0]), k_k=row(k_k[0]), k_a=row(k_a[0]),
               r_k=row(r_k[0]), gn_g=row(gn_g[0]), gn_b=row(gn_b[0]),
               wd=w_decay_up[0], wa=w_a_up[0], wg=w_g_up[0])
    sinks = attn_sinks[0]
    wo_bf16 = w_out[0].astype(BF16)
    w_route = jnp.pad(jnp.concatenate([w_coarse[0], w_fine[0]], axis=1), ((0, 0), (0, LANES - N_GROUPS - N_EXPERTS)))
    b_route = jnp.pad(jnp.concatenate([b_coarse[0], b_fine[0]]), (0, LANES - N_GROUPS - N_EXPERTS)).reshape(1, LANES)

    hp, tail_p = _matmul_with_tail(xp, w_in_t, D_MAIN, prm["w_tail"], MAIN_TM, MAIN_TN, "in_proj_prompt")
    hs = _matmul(xs, w_in_t, D_MAIN, DEC_BATCH, MAIN_TN, "in_proj_sample")

    attn_p = _prompt_attention(hp, sinks)
    rwkv_p, state_p = _prompt_rwkv(hp, tail_p, prm)

    q_s = hs[:, :D_ATTN].reshape(DEC_BATCH, N_Q_HEADS, HEAD_DIM)
    k_s = hs[:, D_ATTN:D_ATTN + D_KV].reshape(DEC_BATCH, 1, D_KV)
    v_s = hs[:, D_ATTN + D_KV:D_QKV].reshape(DEC_BATCH, 1, D_KV)
    window_t = lambda c: jnp.transpose(c, (0, 2, 3, 1)).reshape(DEC_BATCH, D_KV, WINDOW)
    attn_s, kwin_s, vwin_s = _sample_attention(
        q_s, k_s, v_s, window_t(cache_k_win[0]), window_t(cache_v_win[0]), sinks.reshape(N_Q_HEADS, 1))
    r_s, k2_s, vv_s, g_s, tail_s, r_t, w_t, k_t, v_t, a_t, b_t = _sample_prep(hs, xs, state_shift[0], prm)
    y_t, state_s = _sample_step(jnp.transpose(state_wkv[0], (1, 2, 3, 0)), r_t, w_t, k_t, a_t, b_t, v_t)
    state_s = state_s.reshape(DEC_BATCH, N_RWKV_HEADS, HEAD_DIM, HEAD_DIM)
    rwkv_s = _sample_post(y_t, r_s, k2_s, vv_s, g_s, prm)

    n_tokens = SEQ + DEC_BATCH
    outs_pr = _outproj_router(attn_p, rwkv_p, xp, wo_bf16, row(ln1_g[0]), row(ln1_b[0]), w_route, b_route,
                              OUTPROJ_TM, n_tokens, 0, None, "outproj_router_prompt")
    x1_all, x1b_all, route_all = _outproj_router(attn_s.reshape(DEC_BATCH, D_ATTN), rwkv_s, xs,
                                                 wo_bf16, row(ln1_g[0]), row(ln1_b[0]), w_route, b_route,
                                                 DEC_BATCH, n_tokens, SEQ // DEC_BATCH, outs_pr,
                                                 "outproj_router_sample")

    n_assign = 2 * n_tokens
    n_blocks = -(-(n_assign + N_EXPERTS * (MOE_BLOCK - 1)) // MOE_BLOCK)
    dest, zero_offsets, block_e, n_used = _dispatch_plan(route_all, n_blocks)
    x_sorted = _dispatch(zero_offsets, dest, x1b_all, n_blocks)
    y_slots = _expert_mlp(block_e, n_used, x_sorted, w_exp_gate[0], w_exp_up[0], w_exp_down[0], n_blocks)

    def dest_tiles(d, tile):
        d = d.reshape(-1, tile, 2)
        d = jnp.concatenate([d[:, :, 0], d[:, :, 1]], axis=1)
        return jnp.pad(d, ((0, 1), (0, 0))).reshape(-1, 1, 2 * tile)

    y_p = _combine(dest_tiles(dest[:2 * SEQ], COMBINE_TILE), y_slots, x1_all, route_all, SEQ, COMBINE_TILE, 0,
                   row(ln2_g[0]), row(ln2_b[0]), "combine_prompt")
    y_s = _combine(dest_tiles(dest[2 * SEQ:], DEC_BATCH), y_slots, x1_all, route_all, DEC_BATCH, DEC_BATCH,
                   SEQ // DEC_BATCH, row(ln2_g[0]), row(ln2_b[0]), "combine_sample")

    kv4 = lambda a: a.reshape(a.shape[0], N_KV_HEADS, HEAD_DIM)
    k_win_p = kv4(hp[SEQ - WINDOW:, D_ATTN:D_ATTN + D_KV])[None, None]
    v_win_p = kv4(hp[SEQ - WINDOW:, D_ATTN + D_KV:D_QKV])[None, None]
    sp = state_p.reshape(N_PAIRS, HEADS_PER_TILE, HEAD_DIM, HEADS_PER_TILE, HEAD_DIM)
    wkv_p = jnp.stack([sp[:, i, :, i, :] for i in range(HEADS_PER_TILE)], axis=1)
    wkv_p = wkv_p.reshape(N_RWKV_HEADS, HEAD_DIM, HEAD_DIM).transpose(0, 2, 1)[None, None]
    shift_p = jnp.concatenate([hp[SEQ - 1:SEQ, D_QKV:], tail_p[SEQ - 1:SEQ]], axis=1)[None]
    shift_s = jnp.concatenate([hs[:, D_QKV:], tail_s], axis=1)[None]
    return (y_p[None], y_s[:, None, :], k_win_p, v_win_p, wkv_p, shift_p,
            kwin_s.reshape(1, DEC_BATCH, WINDOW, N_KV_HEADS, HEAD_DIM),
            vwin_s.reshape(1, DEC_BATCH, WINDOW, N_KV_HEADS, HEAD_DIM),
            state_s[None], shift_s)
```

```python
import functools
import math

import jax
import jax.numpy as jnp
from jax import lax
from jax.experimental import pallas as pl
from jax.experimental.pallas import tpu as pltpu

F32 = jnp.float32
BF16 = jnp.bfloat16

D_MODEL = 2048
SEQ = 8192
DEC_BATCH = 128
HEAD_DIM = 64
D_ATTN = 1024
D_RWKV = 1024
N_Q_HEADS = 16
N_KV_HEADS = 4
Q_PER_KV = 4
D_KV = 256
WINDOW = 128
ATTN_SCALE = HEAD_DIM ** -0.5
N_RWKV_HEADS = 16
W_LORA = 64
A_LORA = 64
G_LORA = 160
D_SHIFT = 3 * D_RWKV + W_LORA + A_LORA + G_LORA
D_QKV = D_ATTN + 2 * D_KV
N_GROUPS = 4
EXPERTS_PER_GROUP = 8
N_EXPERTS = 32
D_EXPERT = 512
ALPHA = 2.0 ** 0.25
LN_EPS = 1e-5
GN_EPS = 64e-5

SUBLANES = 8
LANES = 128
VMEM_LIMIT = 52 * 1024 * 1024

D_RKV = 3 * D_RWKV
D_TAIL = W_LORA + A_LORA + G_LORA
D_MAIN = D_QKV + D_RKV
MAIN_TN = 1536
MAIN_TM = 512
OUTPROJ_TM = 256

CHUNK = 64
HEADS_PER_TILE = LANES // HEAD_DIM
N_PAIRS = N_RWKV_HEADS // HEADS_PER_TILE
SOLVE_LEVELS = int(math.log2(CHUNK))
PAIR_GROUP = 8

MOE_BLOCK = 256
ROUTE_FINE_OFF = N_GROUPS

NN = (((1,), (0,)), ((), ()))
NT = (((1,), (1,)), ((), ()))


def _dot(a, b, dims=NN):
    return lax.dot_general(a, b, dims, preferred_element_type=F32)


def _dot1(a, b, dims=NN):
    return _dot(a.astype(BF16), b.astype(BF16), dims)


def _split(x):
    hi = x.astype(BF16)
    lo = (x - hi.astype(F32)).astype(BF16)
    return hi, lo


def _dot_exact_lhs(a_bf16, b, dims=NN):
    bh, bl = _split(b)
    return _dot(a_bf16, bh, dims) + _dot(a_bf16, bl, dims)


def _dot_exact_rhs(a, b_bf16, dims=NN):
    ah, al = _split(a)
    return _dot(ah, b_bf16, dims) + _dot(al, b_bf16, dims)


def _div_pow2(x, d):
    return lax.shift_right_logical(x, jnp.int32(int(math.log2(d))))


def _mod_pow2(x, d):
    return lax.bitwise_and(x, jnp.int32(d - 1))


def _pack_bf16_halves(x_bf16):
    n = x_bf16.shape[1] // 2
    bits = lax.bitcast_convert_type(x_bf16.astype(F32), jnp.uint32)
    return lax.bitwise_or(bits[:, 0:n], lax.shift_right_logical(bits[:, n:2 * n], jnp.uint32(16)))


def _unpack_bf16_halves(packed):
    hi = lax.bitcast_convert_type(lax.bitwise_and(packed, jnp.uint32(0xFFFF0000)), F32)
    lo = lax.bitcast_convert_type(lax.shift_left(packed, jnp.uint32(16)), F32)
    return hi.astype(BF16), lo.astype(BF16)


def _sigmoid(x):
    return 1.0 / (1.0 + jnp.exp(-x))


def _softplus(x):
    return jnp.maximum(x, 0.0) + jnp.log(1.0 + jnp.exp(-jnp.abs(x)))


def _layer_norm(z, g, b):
    mu = jnp.mean(z, axis=-1, keepdims=True)
    d = z - mu
    var = jnp.mean(d * d, axis=-1, keepdims=True)
    return d * lax.rsqrt(var + LN_EPS) * g + b


def _cparams(sem):
    return pltpu.CompilerParams(dimension_semantics=sem, vmem_limit_bytes=VMEM_LIMIT)


def _matmul_kernel(x_ref, wt_ref, o_ref):
    o_ref[...] = _dot(x_ref[...].astype(BF16), wt_ref[...].astype(BF16), NT)


def _matmul(x, w_t, n_out, tm, tn, name):
    m, k = x.shape
    tm = min(tm, m)
    return pl.pallas_call(
        _matmul_kernel,
        out_shape=jax.ShapeDtypeStruct((m, n_out), F32),
        grid=(n_out // tn, m // tm),
        in_specs=[pl.BlockSpec((tm, k), lambda j, i: (i, 0)),
                  pl.BlockSpec((tn, k), lambda j, i: (j, 0))],
        out_specs=pl.BlockSpec((tm, tn), lambda j, i: (i, j)),
        compiler_params=_cparams(("arbitrary", "arbitrary")),
        name=name,
    )(x, w_t)


def _matmul_with_tail_kernel(x_ref, wt_ref, wtail_ref, o_ref, tail_ref, wb_ref):
    @pl.when(pl.program_id(1) == 0)
    def _():
        wb_ref[...] = wt_ref[...].astype(BF16)

    xb = x_ref[...].astype(BF16)
    o_ref[...] = _dot(xb, wb_ref[...], NT)

    @pl.when(pl.program_id(0) == 0)
    def _():
        tail_ref[...] = _dot(xb, wtail_ref[...], NT)

    @pl.when(pl.program_id(0) > 0)
    def _():
        tail_ref[...] = jnp.zeros_like(tail_ref)


def _matmul_with_tail(x, w_t, n_out, wtail_t, tm, tn, name):
    m, k = x.shape
    n_rows = m // tm
    n_tail = wtail_t.shape[0]
    return pl.pallas_call(
        _matmul_with_tail_kernel,
        out_shape=(jax.ShapeDtypeStruct((m, n_out), F32), jax.ShapeDtypeStruct((m + tm, n_tail), F32)),
        grid=(n_out // tn, n_rows),
        in_specs=[pl.BlockSpec((tm, k), lambda j, i: (i, 0)),
                  pl.BlockSpec((tn, k), lambda j, i: (j, 0)),
                  pl.BlockSpec((n_tail, k), lambda j, i: (0, 0))],
        out_specs=(pl.BlockSpec((tm, tn), lambda j, i: (i, j)),
                   pl.BlockSpec((tm, n_tail), lambda j, i: (jnp.where(j == 0, i, n_rows), 0))),
        scratch_shapes=[pltpu.VMEM((tn, k), BF16)],
        compiler_params=_cparams(("arbitrary", "arbitrary")),
        name=name,
    )(x, w_t, wtail_t)


def _band_bias():
    qi = jnp.arange(Q_PER_KV * WINDOW)[:, None] % WINDOW
    kj = jnp.arange(2 * WINDOW)[None, :]
    diff = qi + WINDOW - kj
    band = (diff >= 0) & (diff <= WINDOW)
    keep = jnp.stack([band & (kj >= WINDOW), band])
    return jnp.where(keep, 0.0, -jnp.inf).astype(F32)


def _prompt_attn_kernel(q_ref, kvp_ref, kvc_ref, bias_ref, sink_ref, o_ref):
    q = q_ref[...]
    kv_prev = kvp_ref[...]
    kv_cur = kvc_ref[...]
    bias = bias_ref[0]
    row_head = _div_pow2(lax.broadcasted_iota(jnp.int32, (Q_PER_KV * WINDOW, 1), 0), WINDOW)
    groups = range(N_KV_HEADS)
    kv_cols = lambda off, g: jnp.concatenate([kv_prev[:, off + g * HEAD_DIM:off + (g + 1) * HEAD_DIM],
                                              kv_cur[:, off + g * HEAD_DIM:off + (g + 1) * HEAD_DIM]],
                                             axis=0).astype(BF16)
    q_rows = lambda g: jnp.concatenate(
        [q[:, (g * Q_PER_KV + h) * HEAD_DIM:(g * Q_PER_KV + h + 1) * HEAD_DIM] for h in range(Q_PER_KV)],
        axis=0).astype(BF16)
    s = [_dot(q_rows(g), kv_cols(0, g), NT) * ATTN_SCALE + bias for g in groups]
    sink = []
    for g in groups:
        col = jnp.zeros((Q_PER_KV * WINDOW, 1), F32)
        for h in range(Q_PER_KV):
            col = jnp.where(row_head == h, sink_ref[g * Q_PER_KV + h], col)
        sink.append(col)
    m = [jnp.maximum(jnp.max(s[g], axis=-1, keepdims=True), sink[g]) for g in groups]
    p = [jnp.exp(s[g] - m[g]) for g in groups]
    denom = [jnp.sum(p[g], axis=-1, keepdims=True) + jnp.exp(sink[g] - m[g]) for g in groups]
    o = [_dot((p[g] / denom[g]).astype(BF16), kv_cols(D_KV, g)) for g in groups]
    o_ref[...] = jnp.concatenate([o[g][h * WINDOW:(h + 1) * WINDOW, :] for g in groups for h in range(Q_PER_KV)],
                                 axis=1)


def _prompt_attention(h_attn, sinks):
    nb = SEQ // WINDOW
    return pl.pallas_call(
        _prompt_attn_kernel,
        out_shape=jax.ShapeDtypeStruct((SEQ, D_ATTN), F32),
        grid=(nb,),
        in_specs=[pl.BlockSpec((WINDOW, D_ATTN), lambda i: (i, 0)),
                  pl.BlockSpec((WINDOW, 2 * D_KV), lambda i: (jnp.maximum(i - 1, 0), 2)),
                  pl.BlockSpec((WINDOW, 2 * D_KV), lambda i: (i, 2)),
                  pl.BlockSpec((1, Q_PER_KV * WINDOW, 2 * WINDOW), lambda i: (jnp.minimum(i, 1), 0, 0)),
                  pl.BlockSpec(memory_space=pltpu.SMEM)],
        out_specs=pl.BlockSpec((WINDOW, D_ATTN), lambda i: (i, 0)),
        compiler_params=_cparams(("arbitrary",)),
        name="prompt_attention",
    )(h_attn, h_attn, h_attn, _band_bias(), sinks)


SAMPLE_ATTN_TILE = 8


def _sample_attn_kernel(q_ref, knew_ref, vnew_ref, ck_ref, cv_ref, sink_ref, o_ref, kwin_ref, vwin_ref):
    lane = lax.broadcasted_iota(jnp.int32, (N_Q_HEADS, D_KV), 1)
    head = lax.broadcasted_iota(jnp.int32, (N_Q_HEADS, D_KV), 0)
    group_mask = _div_pow2(lane, HEAD_DIM) == _div_pow2(head, Q_PER_KV)
    sink = sink_ref[...]
    row = lax.broadcasted_iota(jnp.int32, (WINDOW, D_KV), 0)
    seqs = range(SAMPLE_ATTN_TILE)
    qbd = [jnp.where(group_mask, jnp.concatenate([q_ref[b]] * N_KV_HEADS, axis=1), 0.0).astype(BF16) for b in seqs]
    s = [_dot1(qbd[b], ck_ref[b]) * ATTN_SCALE for b in seqs]
    s_new = [jnp.sum(qbd[b].astype(F32) * knew_ref[b].astype(BF16).astype(F32), axis=-1, keepdims=True) * ATTN_SCALE
             for b in seqs]
    m = [jnp.maximum(jnp.maximum(jnp.max(s[b], axis=-1, keepdims=True), s_new[b]), sink) for b in seqs]
    p = [jnp.exp(s[b] - m[b]) for b in seqs]
    p_new = [jnp.exp(s_new[b] - m[b]) for b in seqs]
    denom = [jnp.sum(p[b], axis=-1, keepdims=True) + p_new[b] + jnp.exp(sink - m[b]) for b in seqs]
    for b in seqs:
        kb = ck_ref[b].T
        vb = cv_ref[b].T
        kn = knew_ref[b]
        vn = vnew_ref[b]
        o_full = (_dot1(p[b] / denom[b], vb)
                  + (p_new[b] / denom[b]).astype(BF16).astype(F32) * vn.astype(BF16).astype(F32))
        o_full = jnp.where(group_mask, o_full, 0.0)
        o = o_full[:, 0:HEAD_DIM]
        for g in range(1, N_KV_HEADS):
            o = o + o_full[:, g * HEAD_DIM:(g + 1) * HEAD_DIM]
        o_ref[b] = o
        kwin_ref[b] = jnp.where(row == WINDOW - 1, kn, pltpu.roll(kb, WINDOW - 1, axis=0))
        vwin_ref[b] = jnp.where(row == WINDOW - 1, vn, pltpu.roll(vb, WINDOW - 1, axis=0))


def _sample_attention(q, k_new, v_new, cache_k_t, cache_v_t, sinks):
    bt = SAMPLE_ATTN_TILE
    win_spec = pl.BlockSpec((bt, WINDOW, D_KV), lambda i: (i, 0, 0))
    win_t_spec = pl.BlockSpec((bt, D_KV, WINDOW), lambda i: (i, 0, 0))
    new_spec = pl.BlockSpec((bt, 1, D_KV), lambda i: (i, 0, 0))
    return pl.pallas_call(
        _sample_attn_kernel,
        out_shape=(jax.ShapeDtypeStruct((DEC_BATCH, N_Q_HEADS, HEAD_DIM), F32),
                   jax.ShapeDtypeStruct((DEC_BATCH, WINDOW, D_KV), F32),
                   jax.ShapeDtypeStruct((DEC_BATCH, WINDOW, D_KV), F32)),
        grid=(DEC_BATCH // bt,),
        in_specs=[pl.BlockSpec((bt, N_Q_HEADS, HEAD_DIM), lambda i: (i, 0, 0)),
                  new_spec, new_spec, win_t_spec, win_t_spec,
                  pl.BlockSpec((N_Q_HEADS, 1), lambda i: (0, 0))],
        out_specs=(pl.BlockSpec((bt, N_Q_HEADS, HEAD_DIM), lambda i: (i, 0, 0)), win_spec, win_spec),
        compiler_params=_cparams(("arbitrary",)),
        name="sample_attention",
    )(q, k_new, v_new, cache_k_t, cache_v_t, sinks)


def _head_ones():
    r = _div_pow2(lax.broadcasted_iota(jnp.int32, (LANES, LANES), 0), HEAD_DIM)
    c = _div_pow2(lax.broadcasted_iota(jnp.int32, (LANES, LANES), 1), HEAD_DIM)
    return jnp.where(r == c, 1.0, 0.0).astype(BF16)


def _head_sum(x, ones, passes=2):
    dot = _dot_exact_rhs if passes == 2 else _dot1
    parts = [dot(x[:, p * LANES:(p + 1) * LANES], ones) for p in range(x.shape[1] // LANES)]
    return jnp.concatenate(parts, axis=1)


def _token_mix(feat, shifted, mu):
    return feat + (shifted - feat) * mu


def _rwkv_prep(mixed, mixed_tail, w0, a0, k_k, k_a, wd, wa, wg, ones):
    r = mixed[:, 0:D_RWKV]
    k = mixed[:, D_RWKV:2 * D_RWKV]
    v = mixed[:, 2 * D_RWKV:3 * D_RWKV]
    xw = mixed_tail[:, 0:W_LORA]
    xa = mixed_tail[:, W_LORA:W_LORA + A_LORA]
    xg = mixed_tail[:, W_LORA + A_LORA:D_TAIL]
    w_log = -_softplus(-(w0 + _dot1(jnp.tanh(xw), wd))) - 0.5
    log_decay = -jnp.exp(w_log)
    a = _sigmoid(a0 + _dot1(xa, wa))
    g = _dot1(_sigmoid(xg), wg)
    kk = k * k_k
    kk = kk * lax.rsqrt(jnp.maximum(_head_sum(kk * kk, ones), 1e-24))
    k2 = k * (1.0 + (a - 1.0) * k_a)
    return r, log_decay, k2, v, -kk, kk * a, g


def _rwkv_post(y, r, k2, v, g, r_k, gn_g, gn_b, ones, passes=2):
    inv_n = 1.0 / HEAD_DIM
    mu = _head_sum(y, ones, passes) * inv_n
    d = y - mu
    var = _head_sum(d * d, ones, passes) * inv_n
    yn = d * lax.rsqrt(var + GN_EPS) * gn_g + gn_b
    bonus = _head_sum(r * k2 * r_k, ones, passes) * v
    return (yn + bonus) * g


(OP_AABS, OP_RABS, OP_AN, OP_RN, OP_BN, OP_KN, OP_BH, OP_KH, OP_V) = range(9)
N_OPS = 9


def _prompt_rwkv_kernel(f1_ref, f2_ref, tail_in_ref, mu_ref, mut_ref, w0_ref, a0_ref, kk_ref, ka_ref, rk_ref,
                        gng_ref, gnb_ref, wd_ref, wa_ref, wg_ref, out_ref, state_ref,
                        prev_ref, prevt_ref, s_ref, ops_ref, pc_ref, y_ref):
    c = pl.program_id(0)
    C = CHUNK

    @pl.when(c == 0)
    def _():
        prev_ref[...] = jnp.zeros_like(prev_ref)
        prevt_ref[...] = jnp.zeros_like(prevt_ref)
        s_ref[...] = jnp.zeros_like(s_ref)

    ones = _head_ones()
    row = lax.broadcasted_iota(jnp.int32, (C, 1), 0)

    def token_shift(feat, carry_ref):
        shifted = jnp.where(row == 0, carry_ref[0:1, :], pltpu.roll(feat, 1, axis=0))
        carry_ref[0:1, :] = feat[C - 1:C, :]
        return shifted

    feat = jnp.concatenate([f1_ref[...], f2_ref[...]], axis=1)
    tail = tail_in_ref[...]
    mixed = _token_mix(feat, token_shift(feat, prev_ref), mu_ref[...])
    mixed_tail = _token_mix(tail, token_shift(tail, prevt_ref), mut_ref[...])
    r, ld, k2, v, av, bv, g = _rwkv_prep(mixed, mixed_tail, w0_ref[...], a0_ref[...], kk_ref[...], ka_ref[...],
                                         wd_ref[...], wa_ref[...], wg_ref[...], ones)

    ti = lax.broadcasted_iota(jnp.int32, (C, C), 0)
    tj = lax.broadcasted_iota(jnp.int32, (C, C), 1)
    tri_incl = jnp.where(tj <= ti, 1.0, 0.0).astype(BF16)
    cs = _dot_exact_lhs(tri_incl, ld)
    cs_ref = cs[C // 2 - 1:C // 2, :]
    cs_end = cs[C - 1:C, :]
    e_prev = jnp.exp(cs - ld)
    e_cur = jnp.exp(cs)
    n_prev = jnp.exp(cs - ld - cs_ref)
    n_cur = jnp.exp(cs - cs_ref)
    n_inv = jnp.exp(cs_ref - cs)
    e_tail = jnp.exp(cs_end - cs)
    ops = {OP_AABS: av * e_prev, OP_RABS: r * e_cur, OP_AN: av * n_prev, OP_RN: r * n_cur,
           OP_BN: bv * n_inv, OP_KN: k2 * n_inv, OP_BH: bv * e_tail, OP_KH: k2 * e_tail, OP_V: v}
    p_end = jnp.exp(cs_end)
    for p in range(N_PAIRS):
        sl = slice(p * LANES, (p + 1) * LANES)
        for idx, val in ops.items():
            ops_ref[p, idx] = val[:, sl]
        pc_ref[p] = jnp.broadcast_to(p_end[:, sl], (SUBLANES, LANES))

    lane1 = lax.broadcasted_iota(jnp.int32, (C, LANES), 1)
    head0 = lane1 < HEAD_DIM
    r2 = lax.broadcasted_iota(jnp.int32, (2 * C, 2 * C), 0)
    c2 = lax.broadcasted_iota(jnp.int32, (2 * C, 2 * C), 1)
    tq = _mod_pow2(r2, C)
    tk = _mod_pow2(c2, C)
    band = (tk < tq) | ((tk == tq) & (r2 >= C))
    blockdiag = _div_pow2(r2, HEAD_DIM) == _div_pow2(c2, HEAD_DIM)

    op = lambda p, idx: ops_ref[p, idx]
    zero_half = jnp.zeros((C, LANES), F32)
    for pairs in [range(g, g + PAIR_GROUP) for g in range(0, N_PAIRS, PAIR_GROUP)]:
        gy = {p: _dot1(jnp.concatenate([op(p, OP_AABS), op(p, OP_RABS)], axis=0), s_ref[p]) for p in pairs}

        am0, am1 = {}, {}
        for p in pairs:
            a_n, r_n = op(p, OP_AN), op(p, OP_RN)
            b0, k0 = jnp.where(head0, op(p, OP_BN), 0.0), jnp.where(head0, op(p, OP_KN), 0.0)
            b1, k1 = jnp.where(head0, 0.0, op(p, OP_BN)), jnp.where(head0, 0.0, op(p, OP_KN))
            am = _dot1(jnp.concatenate([a_n, r_n], axis=0), jnp.concatenate([k0, b0, b1, k1], axis=0), NT)
            am0[p] = jnp.where(band, am[:, 0:2 * C], 0.0)
            am1[p] = jnp.where(band, am[:, 2 * C:4 * C], 0.0)

        w0, w1 = {}, {}
        for p in pairs:
            top0, top1 = am0[p][0:C], am1[p][0:C]
            ak = jnp.concatenate([jnp.where(head0, top0, 0.0), jnp.where(head0, 0.0, top1)], axis=0)
            vv = op(p, OP_V)
            g0 = gy[p][0:C]
            m = jnp.concatenate([g0, g0], axis=0) + _dot1(ak, jnp.concatenate([vv, vv], axis=0))
            w0[p] = jnp.where(head0, m[0:C], top0)
            w1[p] = jnp.where(head0, top1, m[C:2 * C])

        for lvl in range(SOLVE_LEVELS):
            prod0 = {p: _dot1(w0[p], jnp.concatenate([zero_half, w0[p]], axis=0)) for p in pairs}
            prod1 = {p: _dot1(w1[p], jnp.concatenate([w1[p], zero_half], axis=0)) for p in pairs}
            w0 = {p: jnp.where(head0, w0[p] + prod0[p], prod0[p]) for p in pairs}
            w1 = {p: jnp.where(head0, prod1[p], w1[p] + prod1[p]) for p in pairs}
        u = {p: jnp.where(head0, w0[p], w1[p]) for p in pairs}

        for p in pairs:
            vv = op(p, OP_V)
            y_lhs = jnp.concatenate([am0[p][C:2 * C], am1[p][C:2 * C]], axis=1)
            y_rhs = jnp.concatenate([jnp.where(head0, vv, 0.0), jnp.where(head0, u[p], 0.0),
                                     jnp.where(head0, 0.0, u[p]), jnp.where(head0, 0.0, vv)], axis=0)
            y_ref[p] = gy[p][C:2 * C] + _dot1(y_lhs, y_rhs)

        for p in pairs:
            decay_rows = jnp.broadcast_to(pc_ref[p][0:1, :], (LANES, LANES)).T
            upd_lhs = jnp.concatenate([op(p, OP_BH), op(p, OP_KH)], axis=0).T
            upd_rhs = jnp.concatenate([u[p], op(p, OP_V)], axis=0)
            s_ref[p] = s_ref[p] * decay_rows + jnp.where(blockdiag, _dot1(upd_lhs, upd_rhs), 0.0)

    y = jnp.concatenate([y_ref[p] for p in range(N_PAIRS)], axis=1)
    out_ref[...] = _rwkv_post(y, r, k2, v, g, rk_ref[...], gng_ref[...], gnb_ref[...], ones, passes=1)

    @pl.when(c == pl.num_programs(0) - 1)
    def _():
        state_ref[...] = s_ref[...]


def _prompt_rwkv(h_main, tail, prm):
    n_chunks = SEQ // CHUNK
    half = D_RKV // 2
    assert D_QKV == half
    vec = pl.BlockSpec((1, D_RWKV), lambda c: (0, 0))
    full = lambda a: pl.BlockSpec(a.shape, lambda c: (0,) * a.ndim)
    return pl.pallas_call(
        _prompt_rwkv_kernel,
        out_shape=(jax.ShapeDtypeStruct((SEQ, D_RWKV), F32),
                   jax.ShapeDtypeStruct((N_PAIRS, LANES, LANES), F32)),
        grid=(n_chunks,),
        in_specs=[pl.BlockSpec((CHUNK, half), lambda c: (c, 1)),
                  pl.BlockSpec((CHUNK, half), lambda c: (c, 2)),
                  pl.BlockSpec((CHUNK, D_TAIL), lambda c: (c, 0)),
                  full(prm["mu"]), full(prm["mu_tail"]),
                  vec, vec, vec, vec, vec, vec, vec,
                  full(prm["wd"]), full(prm["wa"]), full(prm["wg"])],
        out_specs=(pl.BlockSpec((CHUNK, D_RWKV), lambda c: (c, 0)),
                   pl.BlockSpec((N_PAIRS, LANES, LANES), lambda c: (0, 0, 0))),
        scratch_shapes=[pltpu.VMEM((SUBLANES, D_RKV), F32),
                        pltpu.VMEM((SUBLANES, D_TAIL), F32),
                        pltpu.VMEM((N_PAIRS, LANES, LANES), F32),
                        pltpu.VMEM((N_PAIRS, N_OPS, CHUNK, LANES), F32),
                        pltpu.VMEM((N_PAIRS, SUBLANES, LANES), F32),
                        pltpu.VMEM((N_PAIRS, CHUNK, LANES), F32)],
        compiler_params=_cparams(("arbitrary",)),
        name="prompt_rwkv",
    )(h_main, h_main, tail, prm["mu"], prm["mu_tail"], prm["w0"], prm["a0"], prm["k_k"], prm["k_a"],
      prm["r_k"], prm["gn_g"], prm["gn_b"], prm["wd"], prm["wa"], prm["wg"])


def _sample_prep_kernel(h_ref, x_ref, wt_ref, shift_ref, mu_ref, mut_ref, w0_ref, a0_ref, kk_ref, ka_ref,
                        wd_ref, wa_ref, wg_ref, r_ref, k_ref, v_ref, g_ref, tail_ref,
                        rt_ref, wtr_ref, kt_ref, vt_ref, at_ref, bt_ref):
    ones = _head_ones()
    feat = h_ref[:, D_QKV:D_MAIN]
    tail = _dot1(x_ref[...], wt_ref[...], NT)
    tail_ref[...] = tail
    mixed = _token_mix(feat, shift_ref[:, 0:D_RKV], mu_ref[...])
    mixed_tail = _token_mix(tail, shift_ref[:, D_RKV:D_SHIFT], mut_ref[...])
    r, ld, k2, v, av, bv, g = _rwkv_prep(mixed, mixed_tail, w0_ref[...], a0_ref[...], kk_ref[...], ka_ref[...],
                                         wd_ref[...], wa_ref[...], wg_ref[...], ones)
    r_ref[...] = r
    k_ref[...] = k2
    v_ref[...] = v
    g_ref[...] = g
    rt_ref[...] = r.T
    wtr_ref[...] = jnp.exp(ld).T
    kt_ref[...] = k2.T
    vt_ref[...] = v.T
    at_ref[...] = av.T
    bt_ref[...] = bv.T


def _sample_prep(h_main, x, shift, prm):
    tok = jax.ShapeDtypeStruct((DEC_BATCH, D_RWKV), F32)
    chan = jax.ShapeDtypeStruct((D_RWKV, DEC_BATCH), F32)
    return pl.pallas_call(
        _sample_prep_kernel,
        out_shape=(tok,) * 4 + (jax.ShapeDtypeStruct((DEC_BATCH, D_TAIL), F32),) + (chan,) * 6,
        compiler_params=pltpu.CompilerParams(vmem_limit_bytes=VMEM_LIMIT),
        name="sample_rwkv_prep",
    )(h_main, x, prm["w_tail"], shift, prm["mu"], prm["mu_tail"], prm["w0"], prm["a0"], prm["k_k"], prm["k_a"],
      prm["wd"], prm["wa"], prm["wg"])


STEP_GROUP = 4


def _sample_step_kernel(s_ref, r_ref, w_ref, k_ref, a_ref, b_ref, v_ref, y_ref, snew_ref):
    r, w, k, a, b = r_ref[...], w_ref[...], k_ref[...], a_ref[...], b_ref[...]
    for g0 in range(0, HEAD_DIM, 2 * STEP_GROUP):
        chans = range(g0, g0 + 2 * STEP_GROUP)
        sa = {i: jnp.sum(s_ref[0, i] * a, axis=0, keepdims=True) for i in chans}
        s_new = {i: s_ref[0, i] * w + sa[i] * b + v_ref[i:i + 1, :] * k for i in chans}
        for i in chans:
            y_ref[i:i + 1, :] = jnp.sum(s_new[i] * r, axis=0, keepdims=True)
        for i in range(g0, g0 + 2 * STEP_GROUP, 2):
            pair = jnp.concatenate([s_new[i], s_new[i + 1]], axis=0)
            snew_ref[:, i * HEAD_DIM:(i + 2) * HEAD_DIM] = pair.T


def _sample_step(state_t, r_t, w_t, k_t, a_t, b_t, v_t):
    head_rows = pl.BlockSpec((HEAD_DIM, DEC_BATCH), lambda h: (h, 0))
    return pl.pallas_call(
        _sample_step_kernel,
        out_shape=(jax.ShapeDtypeStruct((D_RWKV, DEC_BATCH), F32),
                   jax.ShapeDtypeStruct((DEC_BATCH, N_RWKV_HEADS * HEAD_DIM * HEAD_DIM), F32)),
        grid=(N_RWKV_HEADS,),
        in_specs=[pl.BlockSpec((1, HEAD_DIM, HEAD_DIM, DEC_BATCH), lambda h: (h, 0, 0, 0))] + [head_rows] * 6,
        out_specs=(head_rows, pl.BlockSpec((DEC_BATCH, HEAD_DIM * HEAD_DIM), lambda h: (0, h))),
        compiler_params=_cparams(("arbitrary",)),
        name="sample_rwkv_step",
    )(state_t, r_t, w_t, k_t, a_t, b_t, v_t)


def _sample_post_kernel(yt_ref, r_ref, k_ref, v_ref, g_ref, rk_ref, gng_ref, gnb_ref, o_ref):
    o_ref[...] = _rwkv_post(yt_ref[...].T, r_ref[...], k_ref[...], v_ref[...], g_ref[...], rk_ref[...],
                            gng_ref[...], gnb_ref[...], _head_ones())


def _sample_post(y, r, k, v, g, prm):
    return pl.pallas_call(
        _sample_post_kernel,
        out_shape=jax.ShapeDtypeStruct((DEC_BATCH, D_RWKV), F32),
        compiler_params=pltpu.CompilerParams(vmem_limit_bytes=VMEM_LIMIT),
        name="sample_rwkv_post",
    )(y, r, k, v, g, prm["r_k"], prm["gn_g"], prm["gn_b"])


def _project_mix(attn_ref, rwkv_ref, wo_ref):
    return (_dot(attn_ref[...].astype(BF16), wo_ref[0:D_ATTN, :])
            + _dot(rwkv_ref[...].astype(BF16), wo_ref[D_ATTN:D_ATTN + D_RWKV, :]))


def _norm_and_route(mix, x_ref, g_ref, b_ref, wr_ref, br_ref, x1_ref, x1b_ref, route_ref):
    x1 = _layer_norm(ALPHA * x_ref[...] + mix, g_ref[...], b_ref[...])
    x1_ref[...] = x1
    x1b = x1.astype(BF16)
    x1b_ref[...] = _pack_bf16_halves(x1b)
    logits = _dot(x1b, wr_ref[...].astype(BF16)) + br_ref[...]
    tm = logits.shape[0]
    lane = lax.broadcasted_iota(jnp.int32, (tm, LANES), 1).astype(F32)
    big = float(2 * LANES)
    neg = -jnp.inf
    lc = jnp.where(lane < N_GROUPS, logits, neg)
    mc = jnp.max(lc, axis=-1, keepdims=True)
    g_sel = jnp.min(jnp.where(lc == mc, lane, big), axis=-1, keepdims=True)
    p_group = 1.0 / jnp.sum(jnp.exp(lc - mc), axis=-1, keepdims=True)
    lo = ROUTE_FINE_OFF + g_sel * EXPERTS_PER_GROUP
    lf = jnp.where((lane >= lo) & (lane < lo + EXPERTS_PER_GROUP), logits, neg)
    v1 = jnp.max(lf, axis=-1, keepdims=True)
    i1 = jnp.min(jnp.where(lf == v1, lane, big), axis=-1, keepdims=True)
    lf2 = jnp.where(lane == i1, neg, lf)
    v2 = jnp.max(lf2, axis=-1, keepdims=True)
    i2 = jnp.min(jnp.where(lf2 == v2, lane, big), axis=-1, keepdims=True)
    e21 = jnp.exp(v2 - v1)
    gate1 = p_group / (1.0 + e21)
    gate2 = p_group * e21 / (1.0 + e21)
    route = jnp.where(lane == 0, i1 - ROUTE_FINE_OFF,
                      jnp.where(lane == 1, i2 - ROUTE_FINE_OFF,
                                jnp.where(lane == 2, gate1, jnp.where(lane == 3, gate2, 0.0))))
    route_ref[...] = route


N_ROUTER_OUTS = 3


def _outproj_router_kernel(n_tiles, n_aliased, attn_ref, rwkv_ref, x_ref, wo_ref, g_ref, b_ref, wr_ref, br_ref,
                           *rest):
    outs = rest[n_aliased:n_aliased + N_ROUTER_OUTS]
    mix_ref = rest[-1]
    i = pl.program_id(0)
    finish = lambda mix: _norm_and_route(mix, x_ref, g_ref, b_ref, wr_ref, br_ref, *outs)

    @pl.when(i == 0)
    def _():
        mix_ref[...] = _project_mix(attn_ref, rwkv_ref, wo_ref)

    @pl.when((i >= 1) & (i < n_tiles))
    def _():
        finish(mix_ref[...])
        mix_ref[...] = _project_mix(attn_ref, rwkv_ref, wo_ref)

    @pl.when(i == n_tiles)
    def _():
        finish(mix_ref[...])

    @pl.when(i > n_tiles)
    def _():
        for out_ref in outs:
            out_ref[...] = jnp.zeros_like(out_ref)


def _outproj_router(attn, rwkv, x, wo_bf16, ln_g, ln_b, w_route, b_route, tm, n_total, row_block, into, name):
    m = x.shape[0]
    n_tiles = m // tm
    const = lambda shape: pl.BlockSpec(shape, lambda i: (0, 0))
    ahead = lambda width: pl.BlockSpec((tm, width), lambda i: (jnp.minimum(i, n_tiles - 1), 0))
    behind = lambda width: pl.BlockSpec((tm, width), lambda i: (jnp.clip(i - 1, 0, n_tiles - 1), 0))
    in_specs = [ahead(D_ATTN), ahead(D_RWKV), behind(D_MODEL),
                const((D_MODEL, D_MODEL)), const((1, D_MODEL)), const((1, D_MODEL)),
                const((D_MODEL, LANES)), const((1, LANES))]
    args = [attn, rwkv, x, wo_bf16, ln_g, ln_b, w_route, b_route]
    aliases, n_aliased, fill_steps = {}, 0, pl.cdiv(n_total - m, tm)
    if into is not None:
        n_aliased, fill_steps = N_ROUTER_OUTS, 0
        in_specs += [pl.BlockSpec(memory_space=pl.ANY)] * N_ROUTER_OUTS
        aliases = {len(args) + k: k for k in range(N_ROUTER_OUTS)}
        args += list(into)
    out_rows = lambda width: pl.BlockSpec((tm, width), lambda i: (jnp.maximum(i - 1, 0) + row_block, 0))
    return pl.pallas_call(
        functools.partial(_outproj_router_kernel, n_tiles, n_aliased),
        out_shape=(jax.ShapeDtypeStruct((n_total, D_MODEL), F32),
                   jax.ShapeDtypeStruct((n_total, D_MODEL // 2), jnp.uint32),
                   jax.ShapeDtypeStruct((n_total, LANES), F32)),
        grid=(n_tiles + 1 + fill_steps,),
        in_specs=in_specs,
        out_specs=(out_rows(D_MODEL), out_rows(D_MODEL // 2), out_rows(LANES)),
        scratch_shapes=[pltpu.VMEM((tm, D_MODEL), F32)],
        input_output_aliases=aliases,
        compiler_params=_cparams(("arbitrary",)),
        name=name,
    )(*args)


DISPATCH_TILE = 128


def _dispatch_kernel(zoff_ref, dest_ref, x_ref, o_hbm, zbuf, ring, zsem, sem):
    i = pl.program_id(0)
    n_blocks = o_hbm.shape[0] // MOE_BLOCK
    n_used = zoff_ref[N_EXPERTS]

    def zero_fill(start_row):
        start_row = pl.multiple_of(start_row, MOE_BLOCK)
        return pltpu.make_async_copy(zbuf, o_hbm.at[pl.ds(start_row, MOE_BLOCK)], zsem)

    def zero_fills(action):
        for e in range(N_EXPERTS):
            @pl.when(zoff_ref[e] >= 0)
            def _():
                action(zero_fill(zoff_ref[e]))
        for b in range(n_blocks):
            @pl.when(b >= n_used)
            def _():
                action(zero_fill(b * MOE_BLOCK))

    @pl.when(i == 0)
    def _():
        zbuf[...] = jnp.zeros_like(zbuf)
        zero_fills(lambda copy: copy.start())
        zero_fills(lambda copy: copy.wait())

    cur = lax.rem(i, 2)

    def wait_rows(slot):
        for k in range(2):
            pltpu.make_async_copy(ring.at[slot], o_hbm.at[pl.ds(0, DISPATCH_TILE)], sem.at[slot]).wait()

    @pl.when(i >= 2)
    def _():
        wait_rows(cur)

    ring[cur] = x_ref[...]
    for t in range(DISPATCH_TILE):
        for k in range(2):
            pltpu.make_async_copy(ring.at[cur, pl.ds(t, 1)], o_hbm.at[pl.ds(dest_ref[0, 0, 2 * t + k], 1)],
                                  sem.at[cur]).start()

    @pl.when(i == pl.num_programs(0) - 1)
    def _():
        wait_rows(cur)

        @pl.when(i >= 1)
        def _():
            wait_rows(1 - cur)


def _dispatch(zero_offsets, dest, x_packed, n_blocks):
    n_tokens, width = x_packed.shape
    grid_spec = pltpu.PrefetchScalarGridSpec(
        num_scalar_prefetch=1,
        grid=(n_tokens // DISPATCH_TILE,),
        in_specs=[pl.BlockSpec((1, 1, 2 * DISPATCH_TILE), lambda i, z: (i, 0, 0), memory_space=pltpu.SMEM),
                  pl.BlockSpec((DISPATCH_TILE, width), lambda i, z: (i, 0))],
        out_specs=pl.BlockSpec(memory_space=pl.ANY),
        scratch_shapes=[pltpu.VMEM((MOE_BLOCK, width), x_packed.dtype),
                        pltpu.VMEM((2, DISPATCH_TILE, width), x_packed.dtype),
                        pltpu.SemaphoreType.DMA, pltpu.SemaphoreType.DMA((2,))],
    )
    return pl.pallas_call(
        _dispatch_kernel,
        out_shape=jax.ShapeDtypeStruct((n_blocks * MOE_BLOCK, width), x_packed.dtype),
        grid_spec=grid_spec,
        compiler_params=_cparams(("arbitrary",)),
        name="moe_dispatch",
    )(zero_offsets, dest.reshape(-1, 1, 2 * DISPATCH_TILE), x_packed)


def _expert_kernel(be_ref, nb_ref, x_ref, wg_hbm, wu_hbm, wd_hbm, o_ref, wg_buf, wu_buf, wd_buf, slot_ref, sem):
    blk = pl.program_id(0)
    n_used = nb_ref[0]
    expert = be_ref[blk]
    is_first = (blk == 0) | (be_ref[jnp.maximum(blk - 1, 0)] != expert)

    def fetch(e, slot):
        return [pltpu.make_async_copy(hbm.at[e], buf.at[slot], sem.at[slot, i])
                for i, (hbm, buf) in enumerate(((wg_hbm, wg_buf), (wu_hbm, wu_buf), (wd_hbm, wd_buf)))]

    @pl.when((blk < n_used) & is_first)
    def _():
        @pl.when(blk == 0)
        def _():
            slot_ref[0] = 1
            for copy in fetch(expert, 0):
                copy.start()

        slot = 1 - slot_ref[0]
        slot_ref[0] = slot
        for copy in fetch(expert, slot):
            copy.wait()
        nxt = lax.while_loop(lambda j: (j < n_used) & (be_ref[jnp.minimum(j, n_used - 1)] == expert),
                             lambda j: j + 1, blk + 1)

        @pl.when(nxt < n_used)
        def _():
            for copy in fetch(be_ref[jnp.minimum(nxt, n_used - 1)], 1 - slot):
                copy.start()

    @pl.when(blk < n_used)
    def _():
        slot = slot_ref[0]
        half = D_MODEL // 2
        x_head, x_tail = _unpack_bf16_halves(x_ref[...])
        proj = lambda w_buf: (_dot(x_head, w_buf[slot, 0:half, :].astype(BF16))
                              + _dot(x_tail, w_buf[slot, half:D_MODEL, :].astype(BF16)))
        gate = proj(wg_buf)
        up = proj(wu_buf)
        h = gate * _sigmoid(gate) * up
        o_ref[...] = _dot(h.astype(BF16), wd_buf[slot].astype(BF16))

    @pl.when(blk >= n_used)
    def _():
        o_ref[...] = jnp.zeros_like(o_ref)


def _expert_mlp(block_expert, n_used, x_sorted, w_gate, w_up, w_down, n_blocks):
    grid_spec = pltpu.PrefetchScalarGridSpec(
        num_scalar_prefetch=2,
        grid=(n_blocks,),
        in_specs=[pl.BlockSpec((MOE_BLOCK, D_MODEL // 2), lambda b, be, nb: (jnp.minimum(b, nb[0] - 1), 0)),
                  pl.BlockSpec(memory_space=pl.ANY), pl.BlockSpec(memory_space=pl.ANY),
                  pl.BlockSpec(memory_space=pl.ANY)],
        out_specs=pl.BlockSpec((MOE_BLOCK, D_MODEL), lambda b, be, nb: (b, 0)),
        scratch_shapes=[pltpu.VMEM((2, D_MODEL, D_EXPERT), F32), pltpu.VMEM((2, D_MODEL, D_EXPERT), F32),
                        pltpu.VMEM((2, D_EXPERT, D_MODEL), F32), pltpu.SMEM((1,), jnp.int32),
                        pltpu.SemaphoreType.DMA((2, 3))],
    )
    return pl.pallas_call(
        _expert_kernel,
        out_shape=jax.ShapeDtypeStruct((n_blocks * MOE_BLOCK, D_MODEL), F32),
        grid_spec=grid_spec,
        compiler_params=_cparams(("arbitrary",)),
        name="expert_mlp",
    )(block_expert, n_used, x_sorted, w_gate, w_up, w_down)


COMBINE_TILE = 256


def _combine_kernel(dest_ref, dest_next_ref, y_hbm, x1_ref, route_ref, g_ref, b_ref, o_ref, ybuf, sem):
    i = pl.program_id(0)
    cur = lax.rem(i, 2)
    tile = x1_ref.shape[0]
    n_rows = 2 * tile

    def gather(table_ref, buf):
        for slot in range(n_rows):
            pltpu.make_async_copy(y_hbm.at[pl.ds(table_ref[0, 0, slot], 1)], ybuf.at[buf, pl.ds(slot, 1)],
                                  sem.at[buf]).start()

    def wait_gather(buf):
        pltpu.make_async_copy(y_hbm.at[pl.ds(0, n_rows)], ybuf.at[buf], sem.at[buf]).wait()

    @pl.when(i == 0)
    def _():
        gather(dest_ref, 0)

    gather(dest_next_ref, 1 - cur)
    wait_gather(cur)
    route = route_ref[...]
    yb = ybuf[cur]
    moe = route[:, 2:3] * yb[0:tile, :] + route[:, 3:4] * yb[tile:n_rows, :]
    o_ref[...] = _layer_norm(ALPHA * x1_ref[...] + moe, g_ref[...], b_ref[...])

    @pl.when(i == pl.num_programs(0) - 1)
    def _():
        wait_gather(1 - cur)


def _combine(dest, y_slots, x1_all, route_all, m, tm, row_block, ln_g, ln_b, name):
    return pl.pallas_call(
        _combine_kernel,
        out_shape=jax.ShapeDtypeStruct((m, D_MODEL), F32),
        grid=(m // tm,),
        in_specs=[pl.BlockSpec((1, 1, 2 * tm), lambda i: (i, 0, 0), memory_space=pltpu.SMEM),
                  pl.BlockSpec((1, 1, 2 * tm), lambda i: (i + 1, 0, 0), memory_space=pltpu.SMEM),
                  pl.BlockSpec(memory_space=pl.ANY),
                  pl.BlockSpec((tm, D_MODEL), lambda i: (i + row_block, 0)),
                  pl.BlockSpec((tm, LANES), lambda i: (i + row_block, 0)),
                  pl.BlockSpec((1, D_MODEL), lambda i: (0, 0)),
                  pl.BlockSpec((1, D_MODEL), lambda i: (0, 0))],
        out_specs=pl.BlockSpec((tm, D_MODEL), lambda i: (i, 0)),
        scratch_shapes=[pltpu.VMEM((2, 2 * tm, D_MODEL), F32), pltpu.SemaphoreType.DMA((2,))],
        compiler_params=_cparams(("arbitrary",)),
        name=name,
    )(dest, dest, y_slots, x1_all, route_all, ln_g, ln_b)


def _dispatch_plan(route_all, n_blocks):
    flat_e = route_all[:, 0:2].astype(jnp.int32).reshape(-1)
    onehot = (flat_e[:, None] == jnp.arange(N_EXPERTS, dtype=jnp.int32)[None, :]).astype(jnp.int32)
    csum = jnp.cumsum(onehot, axis=0)
    rank = jnp.sum(onehot * csum, axis=1) - 1
    counts = csum[-1]
    padded = (counts + MOE_BLOCK - 1) // MOE_BLOCK * MOE_BLOCK
    pend = jnp.cumsum(padded)
    pstart = pend - padded
    dest = (pstart[flat_e] + rank).astype(jnp.int32)
    zero_offsets = jnp.where(counts > 0, pend - MOE_BLOCK, -1).astype(jnp.int32)
    n_used = (pend[-1] // MOE_BLOCK).astype(jnp.int32)
    block_start = jnp.minimum(jnp.arange(n_blocks, dtype=jnp.int32), n_used - 1) * MOE_BLOCK
    block_e = jnp.minimum(jnp.searchsorted(pend, block_start, side="right"), N_EXPERTS - 1).astype(jnp.int32)
    return dest, jnp.concatenate([zero_offsets, n_used.reshape(1)]), block_e, n_used.reshape(1)


def kernel(x_prompt, x_sample, cache_k_win, cache_v_win, state_wkv, state_shift, w_in, attn_sinks, shift_mu, w0,
           w_decay_up, a0, w_a_up, w_g_up, k_k, k_a, r_k, gn_g, gn_b, w_out, ln1_g, ln1_b, w_coarse, b_coarse,
           w_fine, b_fine, w_exp_gate, w_exp_up, w_exp_down, ln2_g, ln2_b):
    xp = x_prompt[0]
    xs = x_sample[:, 0]
    row = lambda a: a.reshape(1, -1)

    w_in_t = jnp.swapaxes(w_in[0], 0, 1)
    prm = dict(mu=row(shift_mu[0, :D_RKV]), mu_tail=row(shift_mu[0, D_RKV:]), w_tail=w_in_t[D_MAIN:].astype(BF16),
               w0=row(w0[0]), a0=row(a0[0]), k_k=row(k_k[0]), k_a=row(k_a[0]),
               r_k=row(r_k[0]), gn_g=row(gn_g[0]), gn_b=row(gn_b[0]),
               wd=w_decay_up[0], wa=w_a_up[0], wg=w_g_up[0])
    sinks = attn_sinks[0]
    wo_bf16 = w_out[0].astype(BF16)
    w_route = jnp.pad(jnp.concatenate([w_coarse[0], w_fine[0]], axis=1), ((0, 0), (0, LANES - N_GROUPS - N_EXPERTS)))
    b_route = jnp.pad(jnp.concatenate([b_coarse[0], b_fine[0]]), (0, LANES - N_GROUPS - N_EXPERTS)).reshape(1, LANES)

    hp, tail_p = _matmul_with_tail(xp, w_in_t, D_MAIN, prm["w_tail"], MAIN_TM, MAIN_TN, "in_proj_prompt")
    hs = _matmul(xs, w_in_t, D_MAIN, DEC_BATCH, MAIN_TN, "in_proj_sample")

    attn_p = _prompt_attention(hp, sinks)
    rwkv_p, state_p = _prompt_rwkv(hp, tail_p, prm)

    q_s = hs[:, :D_ATTN].reshape(DEC_BATCH, N_Q_HEADS, HEAD_DIM)
    k_s = hs[:, D_ATTN:D_ATTN + D_KV].reshape(DEC_BATCH, 1, D_KV)
    v_s = hs[:, D_ATTN + D_KV:D_QKV].reshape(DEC_BATCH, 1, D_KV)
    window_t = lambda c: jnp.transpose(c, (0, 2, 3, 1)).reshape(DEC_BATCH, D_KV, WINDOW)
    attn_s, kwin_s, vwin_s = _sample_attention(
        q_s, k_s, v_s, window_t(cache_k_win[0]), window_t(cache_v_win[0]), sinks.reshape(N_Q_HEADS, 1))
    r_s, k2_s, vv_s, g_s, tail_s, r_t, w_t, k_t, v_t, a_t, b_t = _sample_prep(hs, xs, state_shift[0], prm)
    y_t, state_s = _sample_step(jnp.transpose(state_wkv[0], (1, 2, 3, 0)), r_t, w_t, k_t, a_t, b_t, v_t)
    state_s = state_s.reshape(DEC_BATCH, N_RWKV_HEADS, HEAD_DIM, HEAD_DIM)
    rwkv_s = _sample_post(y_t, r_s, k2_s, vv_s, g_s, prm)

    n_tokens = SEQ + DEC_BATCH
    outs_pr = _outproj_router(attn_p, rwkv_p, xp, wo_bf16, row(ln1_g[0]), row(ln1_b[0]), w_route, b_route,
                              OUTPROJ_TM, n_tokens, 0, None, "outproj_router_prompt")
    x1_all, x1b_all, route_all = _outproj_router(attn_s.reshape(DEC_BATCH, D_ATTN), rwkv_s, xs,
                                                 wo_bf16, row(ln1_g[0]), row(ln1_b[0]), w_route, b_route,
                                                 DEC_BATCH, n_tokens, SEQ // DEC_BATCH, outs_pr,
                                                 "outproj_router_sample")

    n_assign = 2 * n_tokens
    n_blocks = -(-(n_assign + N_EXPERTS * (MOE_BLOCK - 1)) // MOE_BLOCK)
    dest, zero_offsets, block_e, n_used = _dispatch_plan(route_all, n_blocks)
    x_sorted = _dispatch(zero_offsets, dest, x1b_all, n_blocks)
    y_slots = _expert_mlp(block_e, n_used, x_sorted, w_exp_gate[0], w_exp_up[0], w_exp_down[0], n_blocks)

    def dest_tiles(d, tile):
        d = d.reshape(-1, tile, 2)
        d = jnp.concatenate([d[:, :, 0], d[:, :, 1]], axis=1)
        return jnp.pad(d, ((0, 1), (0, 0))).reshape(-1, 1, 2 * tile)

    y_p = _combine(dest_tiles(dest[:2 * SEQ], COMBINE_TILE), y_slots, x1_all, route_all, SEQ, COMBINE_TILE, 0,
                   row(ln2_g[0]), row(ln2_b[0]), "combine_prompt")
    y_s = _combine(dest_tiles(dest[2 * SEQ:], DEC_BATCH), y_slots, x1_all, route_all, DEC_BATCH, DEC_BATCH,
                   SEQ // DEC_BATCH, row(ln2_g[0]), row(ln2_b[0]), "combine_sample")

    kv4 = lambda a: a.reshape(a.shape[0], N_KV_HEADS, HEAD_DIM)
    k_win_p = kv4(hp[SEQ - WINDOW:, D_ATTN:D_ATTN + D_KV])[None, None]
    v_win_p = kv4(hp[SEQ - WINDOW:, D_ATTN + D_KV:D_QKV])[None, None]
    sp = state_p.reshape(N_PAIRS, HEADS_PER_TILE, HEAD_DIM, HEADS_PER_TILE, HEAD_DIM)
    wkv_p = jnp.stack([sp[:, i, :, i, :] for i in range(HEADS_PER_TILE)], axis=1)
    wkv_p = wkv_p.reshape(N_RWKV_HEADS, HEAD_DIM, HEAD_DIM).transpose(0, 2, 1)[None, None]
    shift_p = jnp.concatenate([hp[SEQ - 1:SEQ, D_QKV:], tail_p[SEQ - 1:SEQ]], axis=1)[None]
    shift_s = jnp.concatenate([hs[:, D_QKV:], tail_s], axis=1)[None]
    return (y_p[None], y_s[:, None, :], k_win_p, v_win_p, wkv_p, shift_p,
            kwin_s.reshape(1, DEC_BATCH, WINDOW, N_KV_HEADS, HEAD_DIM),
            vwin_s.reshape(1, DEC_BATCH, WINDOW, N_KV_HEADS, HEAD_DIM),
            state_s[None], shift_s)
```

```python
import functools
import math

import jax
import jax.numpy as jnp
from jax import lax
from jax.experimental import pallas as pl
from jax.experimental.pallas import tpu as pltpu

F32 = jnp.float32
BF16 = jnp.bfloat16

D_MODEL = 2048
SEQ = 8192
DEC_BATCH = 128
HEAD_DIM = 64
D_ATTN = 1024
D_RWKV = 1024
N_Q_HEADS = 16
N_KV_HEADS = 4
Q_PER_KV = 4
D_KV = 256
WINDOW = 128
ATTN_SCALE = HEAD_DIM ** -0.5
N_RWKV_HEADS = 16
W_LORA = 64
A_LORA = 64
G_LORA = 160
D_SHIFT = 3 * D_RWKV + W_LORA + A_LORA + G_LORA
D_QKV = D_ATTN + 2 * D_KV
N_GROUPS = 4
EXPERTS_PER_GROUP = 8
N_EXPERTS = 32
D_EXPERT = 512
ALPHA = 2.0 ** 0.25
LN_EPS = 1e-5
GN_EPS = 64e-5

SUBLANES = 8
LANES = 128
VMEM_LIMIT = 52 * 1024 * 1024

D_RKV = 3 * D_RWKV
D_TAIL = W_LORA + A_LORA + G_LORA
D_MAIN = D_QKV + D_RKV
MAIN_TN = 1536
MAIN_TM = 512
OUTPROJ_TM = 256

CHUNK = 64
HEADS_PER_TILE = LANES // HEAD_DIM
N_PAIRS = N_RWKV_HEADS // HEADS_PER_TILE
SOLVE_LEVELS = int(math.log2(CHUNK))
PAIR_GROUP = 8

MOE_BLOCK = 256
ROUTE_FINE_OFF = N_GROUPS

NN = (((1,), (0,)), ((), ()))
NT = (((1,), (1,)), ((), ()))


def _dot(a, b, dims=NN):
    return lax.dot_general(a, b, dims, preferred_element_type=F32)


def _dot1(a, b, dims=NN):
    return _dot(a.astype(BF16), b.astype(BF16), dims)


def _split(x):
    hi = x.astype(BF16)
    lo = (x - hi.astype(F32)).astype(BF16)
    return hi, lo


def _dot_exact_lhs(a_bf16, b, dims=NN):
    bh, bl = _split(b)
    return _dot(a_bf16, bh, dims) + _dot(a_bf16, bl, dims)


def _dot_exact_rhs(a, b_bf16, dims=NN):
    ah, al = _split(a)
    return _dot(ah, b_bf16, dims) + _dot(al, b_bf16, dims)


def _div_pow2(x, d):
    return lax.shift_right_logical(x, jnp.int32(int(math.log2(d))))


def _mod_pow2(x, d):
    return lax.bitwise_and(x, jnp.int32(d - 1))


def _pack_bf16_halves(x_bf16):
    n = x_bf16.shape[1] // 2
    bits = lax.bitcast_convert_type(x_bf16.astype(F32), jnp.uint32)
    return lax.bitwise_or(bits[:, 0:n], lax.shift_right_logical(bits[:, n:2 * n], jnp.uint32(16)))


def _unpack_bf16_halves(packed):
    hi = lax.bitcast_convert_type(lax.bitwise_and(packed, jnp.uint32(0xFFFF0000)), F32)
    lo = lax.bitcast_convert_type(lax.shift_left(packed, jnp.uint32(16)), F32)
    return hi.astype(BF16), lo.astype(BF16)


def _sigmoid(x):
    return 1.0 / (1.0 + jnp.exp(-x))


def _softplus(x):
    return jnp.maximum(x, 0.0) + jnp.log(1.0 + jnp.exp(-jnp.abs(x)))


def _layer_norm(z, g, b):
    mu = jnp.mean(z, axis=-1, keepdims=True)
    d = z - mu
    var = jnp.mean(d * d, axis=-1, keepdims=True)
    return d * lax.rsqrt(var + LN_EPS) * g + b


def _cparams(sem):
    return pltpu.CompilerParams(dimension_semantics=sem, vmem_limit_bytes=VMEM_LIMIT)


def _matmul_kernel(x_ref, wt_ref, o_ref):
    o_ref[...] = _dot(x_ref[...].astype(BF16), wt_ref[...].astype(BF16), NT)


def _matmul(x, w_t, n_out, tm, tn, name):
    m, k = x.shape
    tm = min(tm, m)
    return pl.pallas_call(
        _matmul_kernel,
        out_shape=jax.ShapeDtypeStruct((m, n_out), F32),
        grid=(n_out // tn, m // tm),
        in_specs=[pl.BlockSpec((tm, k), lambda j, i: (i, 0)),
                  pl.BlockSpec((tn, k), lambda j, i: (j, 0))],
        out_specs=pl.BlockSpec((tm, tn), lambda j, i: (i, j)),
        compiler_params=_cparams(("arbitrary", "arbitrary")),
        name=name,
    )(x, w_t)


def _matmul_with_tail_kernel(x_ref, wt_ref, wtail_ref, o_ref, tail_ref):
    xb = x_ref[...].astype(BF16)
    o_ref[...] = _dot(xb, wt_ref[...].astype(BF16), NT)

    @pl.when(pl.program_id(0) == 0)
    def _():
        tail_ref[...] = _dot(xb, wtail_ref[...], NT)

    @pl.when(pl.program_id(0) > 0)
    def _():
        tail_ref[...] = jnp.zeros_like(tail_ref)


def _matmul_with_tail(x, w_t, n_out, wtail_t, tm, tn, name):
    m, k = x.shape
    n_rows = m // tm
    n_tail = wtail_t.shape[0]
    return pl.pallas_call(
        _matmul_with_tail_kernel,
        out_shape=(jax.ShapeDtypeStruct((m, n_out), F32), jax.ShapeDtypeStruct((m + tm, n_tail), F32)),
        grid=(n_out // tn, n_rows),
        in_specs=[pl.BlockSpec((tm, k), lambda j, i: (i, 0)),
                  pl.BlockSpec((tn, k), lambda j, i: (j, 0)),
                  pl.BlockSpec((n_tail, k), lambda j, i: (0, 0))],
        out_specs=(pl.BlockSpec((tm, tn), lambda j, i: (i, j)),
                   pl.BlockSpec((tm, n_tail), lambda j, i: (jnp.where(j == 0, i, n_rows), 0))),
        compiler_params=_cparams(("arbitrary", "arbitrary")),
        name=name,
    )(x, w_t, wtail_t)


def _band_bias():
    qi = jnp.arange(Q_PER_KV * WINDOW)[:, None] % WINDOW
    kj = jnp.arange(2 * WINDOW)[None, :]
    diff = qi + WINDOW - kj
    band = (diff >= 0) & (diff <= WINDOW)
    keep = jnp.stack([band & (kj >= WINDOW), band])
    return jnp.where(keep, 0.0, -jnp.inf).astype(F32)


def _prompt_attn_kernel(q_ref, kvp_ref, kvc_ref, bias_ref, sink_ref, o_ref):
    q = q_ref[...]
    kv_prev = kvp_ref[...]
    kv_cur = kvc_ref[...]
    bias = bias_ref[0]
    row_head = _div_pow2(lax.broadcasted_iota(jnp.int32, (Q_PER_KV * WINDOW, 1), 0), WINDOW)
    groups = range(N_KV_HEADS)
    kv_cols = lambda off, g: jnp.concatenate([kv_prev[:, off + g * HEAD_DIM:off + (g + 1) * HEAD_DIM],
                                              kv_cur[:, off + g * HEAD_DIM:off + (g + 1) * HEAD_DIM]],
                                             axis=0).astype(BF16)
    q_rows = lambda g: jnp.concatenate(
        [q[:, (g * Q_PER_KV + h) * HEAD_DIM:(g * Q_PER_KV + h + 1) * HEAD_DIM] for h in range(Q_PER_KV)],
        axis=0).astype(BF16)
    s = [_dot(q_rows(g), kv_cols(0, g), NT) * ATTN_SCALE + bias for g in groups]
    sink = []
    for g in groups:
        col = jnp.zeros((Q_PER_KV * WINDOW, 1), F32)
        for h in range(Q_PER_KV):
            col = jnp.where(row_head == h, sink_ref[g * Q_PER_KV + h], col)
        sink.append(col)
    m = [jnp.maximum(jnp.max(s[g], axis=-1, keepdims=True), sink[g]) for g in groups]
    p = [jnp.exp(s[g] - m[g]) for g in groups]
    denom = [jnp.sum(p[g], axis=-1, keepdims=True) + jnp.exp(sink[g] - m[g]) for g in groups]
    o = [_dot((p[g] / denom[g]).astype(BF16), kv_cols(D_KV, g)) for g in groups]
    o_ref[...] = jnp.concatenate([o[g][h * WINDOW:(h + 1) * WINDOW, :] for g in groups for h in range(Q_PER_KV)],
                                 axis=1)


def _prompt_attention(h_attn, sinks):
    nb = SEQ // WINDOW
    return pl.pallas_call(
        _prompt_attn_kernel,
        out_shape=jax.ShapeDtypeStruct((SEQ, D_ATTN), F32),
        grid=(nb,),
        in_specs=[pl.BlockSpec((WINDOW, D_ATTN), lambda i: (i, 0)),
                  pl.BlockSpec((WINDOW, 2 * D_KV), lambda i: (jnp.maximum(i - 1, 0), 2)),
                  pl.BlockSpec((WINDOW, 2 * D_KV), lambda i: (i, 2)),
                  pl.BlockSpec((1, Q_PER_KV * WINDOW, 2 * WINDOW), lambda i: (jnp.minimum(i, 1), 0, 0)),
                  pl.BlockSpec(memory_space=pltpu.SMEM)],
        out_specs=pl.BlockSpec((WINDOW, D_ATTN), lambda i: (i, 0)),
        compiler_params=_cparams(("arbitrary",)),
        name="prompt_attention",
    )(h_attn, h_attn, h_attn, _band_bias(), sinks)


SAMPLE_ATTN_TILE = 8


def _sample_attn_kernel(q_ref, knew_ref, vnew_ref, ck_ref, cv_ref, sink_ref, o_ref, kwin_ref, vwin_ref):
    lane = lax.broadcasted_iota(jnp.int32, (N_Q_HEADS, D_KV), 1)
    head = lax.broadcasted_iota(jnp.int32, (N_Q_HEADS, D_KV), 0)
    group_mask = _div_pow2(lane, HEAD_DIM) == _div_pow2(head, Q_PER_KV)
    sink = sink_ref[...]
    row = lax.broadcasted_iota(jnp.int32, (WINDOW, D_KV), 0)
    seqs = range(SAMPLE_ATTN_TILE)
    qbd = [jnp.where(group_mask, jnp.concatenate([q_ref[b]] * N_KV_HEADS, axis=1), 0.0).astype(BF16) for b in seqs]
    s = [_dot1(qbd[b], ck_ref[b]) * ATTN_SCALE for b in seqs]
    s_new = [jnp.sum(qbd[b].astype(F32) * knew_ref[b].astype(BF16).astype(F32), axis=-1, keepdims=True) * ATTN_SCALE
             for b in seqs]
    m = [jnp.maximum(jnp.maximum(jnp.max(s[b], axis=-1, keepdims=True), s_new[b]), sink) for b in seqs]
    p = [jnp.exp(s[b] - m[b]) for b in seqs]
    p_new = [jnp.exp(s_new[b] - m[b]) for b in seqs]
    denom = [jnp.sum(p[b], axis=-1, keepdims=True) + p_new[b] + jnp.exp(sink - m[b]) for b in seqs]
    for b in seqs:
        kb = ck_ref[b].T
        vb = cv_ref[b].T
        kn = knew_ref[b]
        vn = vnew_ref[b]
        o_full = (_dot1(p[b] / denom[b], vb)
                  + (p_new[b] / denom[b]).astype(BF16).astype(F32) * vn.astype(BF16).astype(F32))
        o_full = jnp.where(group_mask, o_full, 0.0)
        o = o_full[:, 0:HEAD_DIM]
        for g in range(1, N_KV_HEADS):
            o = o + o_full[:, g * HEAD_DIM:(g + 1) * HEAD_DIM]
        o_ref[b] = o
        kwin_ref[b] = jnp.where(row == WINDOW - 1, kn, pltpu.roll(kb, WINDOW - 1, axis=0))
        vwin_ref[b] = jnp.where(row == WINDOW - 1, vn, pltpu.roll(vb, WINDOW - 1, axis=0))


def _sample_attention(q, k_new, v_new, cache_k_t, cache_v_t, sinks):
    bt = SAMPLE_ATTN_TILE
    win_spec = pl.BlockSpec((bt, WINDOW, D_KV), lambda i: (i, 0, 0))
    win_t_spec = pl.BlockSpec((bt, D_KV, WINDOW), lambda i: (i, 0, 0))
    new_spec = pl.BlockSpec((bt, 1, D_KV), lambda i: (i, 0, 0))
    return pl.pallas_call(
        _sample_attn_kernel,
        out_shape=(jax.ShapeDtypeStruct((DEC_BATCH, N_Q_HEADS, HEAD_DIM), F32),
                   jax.ShapeDtypeStruct((DEC_BATCH, WINDOW, D_KV), F32),
                   jax.ShapeDtypeStruct((DEC_BATCH, WINDOW, D_KV), F32)),
        grid=(DEC_BATCH // bt,),
        in_specs=[pl.BlockSpec((bt, N_Q_HEADS, HEAD_DIM), lambda i: (i, 0, 0)),
                  new_spec, new_spec, win_t_spec, win_t_spec,
                  pl.BlockSpec((N_Q_HEADS, 1), lambda i: (0, 0))],
        out_specs=(pl.BlockSpec((bt, N_Q_HEADS, HEAD_DIM), lambda i: (i, 0, 0)), win_spec, win_spec),
        compiler_params=_cparams(("arbitrary",)),
        name="sample_attention",
    )(q, k_new, v_new, cache_k_t, cache_v_t, sinks)


def _head_ones():
    r = _div_pow2(lax.broadcasted_iota(jnp.int32, (LANES, LANES), 0), HEAD_DIM)
    c = _div_pow2(lax.broadcasted_iota(jnp.int32, (LANES, LANES), 1), HEAD_DIM)
    return jnp.where(r == c, 1.0, 0.0).astype(BF16)


def _head_sum(x, ones, passes=2):
    dot = _dot_exact_rhs if passes == 2 else _dot1
    parts = [dot(x[:, p * LANES:(p + 1) * LANES], ones) for p in range(x.shape[1] // LANES)]
    return jnp.concatenate(parts, axis=1)


def _token_mix(feat, shifted, mu):
    return feat + (shifted - feat) * mu


def _rwkv_prep(mixed, mixed_tail, w0, a0, k_k, k_a, wd, wa, wg, ones):
    r = mixed[:, 0:D_RWKV]
    k = mixed[:, D_RWKV:2 * D_RWKV]
    v = mixed[:, 2 * D_RWKV:3 * D_RWKV]
    xw = mixed_tail[:, 0:W_LORA]
    xa = mixed_tail[:, W_LORA:W_LORA + A_LORA]
    xg = mixed_tail[:, W_LORA + A_LORA:D_TAIL]
    w_log = -_softplus(-(w0 + _dot1(jnp.tanh(xw), wd))) - 0.5
    log_decay = -jnp.exp(w_log)
    a = _sigmoid(a0 + _dot1(xa, wa))
    g = _dot1(_sigmoid(xg), wg)
    kk = k * k_k
    kk = kk * lax.rsqrt(jnp.maximum(_head_sum(kk * kk, ones), 1e-24))
    k2 = k * (1.0 + (a - 1.0) * k_a)
    return r, log_decay, k2, v, -kk, kk * a, g


def _rwkv_post(y, r, k2, v, g, r_k, gn_g, gn_b, ones, passes=2):
    inv_n = 1.0 / HEAD_DIM
    mu = _head_sum(y, ones, passes) * inv_n
    d = y - mu
    var = _head_sum(d * d, ones, passes) * inv_n
    yn = d * lax.rsqrt(var + GN_EPS) * gn_g + gn_b
    bonus = _head_sum(r * k2 * r_k, ones, passes) * v
    return (yn + bonus) * g


(OP_AABS, OP_RABS, OP_AN, OP_RN, OP_BN, OP_KN, OP_BH, OP_KH, OP_V) = range(9)
N_OPS = 9


def _prompt_rwkv_kernel(f1_ref, f2_ref, tail_in_ref, mu_ref, mut_ref, w0_ref, a0_ref, kk_ref, ka_ref, rk_ref,
                        gng_ref, gnb_ref, wd_ref, wa_ref, wg_ref, out_ref, state_ref,
                        prev_ref, prevt_ref, s_ref, ops_ref, pc_ref, y_ref):
    c = pl.program_id(0)
    C = CHUNK

    @pl.when(c == 0)
    def _():
        prev_ref[...] = jnp.zeros_like(prev_ref)
        prevt_ref[...] = jnp.zeros_like(prevt_ref)
        s_ref[...] = jnp.zeros_like(s_ref)

    ones = _head_ones()
    row = lax.broadcasted_iota(jnp.int32, (C, 1), 0)

    def token_shift(feat, carry_ref):
        shifted = jnp.where(row == 0, carry_ref[0:1, :], pltpu.roll(feat, 1, axis=0))
        carry_ref[0:1, :] = feat[C - 1:C, :]
        return shifted

    feat = jnp.concatenate([f1_ref[...], f2_ref[...]], axis=1)
    tail = tail_in_ref[...]
    mixed = _token_mix(feat, token_shift(feat, prev_ref), mu_ref[...])
    mixed_tail = _token_mix(tail, token_shift(tail, prevt_ref), mut_ref[...])
    r, ld, k2, v, av, bv, g = _rwkv_prep(mixed, mixed_tail, w0_ref[...], a0_ref[...], kk_ref[...], ka_ref[...],
                                         wd_ref[...], wa_ref[...], wg_ref[...], ones)

    ti = lax.broadcasted_iota(jnp.int32, (C, C), 0)
    tj = lax.broadcasted_iota(jnp.int32, (C, C), 1)
    tri_incl = jnp.where(tj <= ti, 1.0, 0.0).astype(BF16)
    cs = _dot_exact_lhs(tri_incl, ld)
    cs_ref = cs[C // 2 - 1:C // 2, :]
    cs_end = cs[C - 1:C, :]
    e_prev = jnp.exp(cs - ld)
    e_cur = jnp.exp(cs)
    n_prev = jnp.exp(cs - ld - cs_ref)
    n_cur = jnp.exp(cs - cs_ref)
    n_inv = jnp.exp(cs_ref - cs)
    e_tail = jnp.exp(cs_end - cs)
    ops = {OP_AABS: av * e_prev, OP_RABS: r * e_cur, OP_AN: av * n_prev, OP_RN: r * n_cur,
           OP_BN: bv * n_inv, OP_KN: k2 * n_inv, OP_BH: bv * e_tail, OP_KH: k2 * e_tail, OP_V: v}
    p_end = jnp.exp(cs_end)
    for p in range(N_PAIRS):
        sl = slice(p * LANES, (p + 1) * LANES)
        for idx, val in ops.items():
            ops_ref[p, idx] = val[:, sl]
        pc_ref[p] = jnp.broadcast_to(p_end[:, sl], (SUBLANES, LANES))

    lane1 = lax.broadcasted_iota(jnp.int32, (C, LANES), 1)
    head0 = lane1 < HEAD_DIM
    r2 = lax.broadcasted_iota(jnp.int32, (2 * C, 2 * C), 0)
    c2 = lax.broadcasted_iota(jnp.int32, (2 * C, 2 * C), 1)
    tq = _mod_pow2(r2, C)
    tk = _mod_pow2(c2, C)
    band = (tk < tq) | ((tk == tq) & (r2 >= C))
    blockdiag = _div_pow2(r2, HEAD_DIM) == _div_pow2(c2, HEAD_DIM)

    op = lambda p, idx: ops_ref[p, idx]
    zero_half = jnp.zeros((C, LANES), F32)
    for pairs in [range(g, g + PAIR_GROUP) for g in range(0, N_PAIRS, PAIR_GROUP)]:
        gy = {p: _dot1(jnp.concatenate([op(p, OP_AABS), op(p, OP_RABS)], axis=0), s_ref[p]) for p in pairs}

        am0, am1 = {}, {}
        for p in pairs:
            a_n, r_n = op(p, OP_AN), op(p, OP_RN)
            b0, k0 = jnp.where(head0, op(p, OP_BN), 0.0), jnp.where(head0, op(p, OP_KN), 0.0)
            b1, k1 = jnp.where(head0, 0.0, op(p, OP_BN)), jnp.where(head0, 0.0, op(p, OP_KN))
            am = _dot1(jnp.concatenate([a_n, r_n], axis=0), jnp.concatenate([k0, b0, b1, k1], axis=0), NT)
            am0[p] = jnp.where(band, am[:, 0:2 * C], 0.0)
            am1[p] = jnp.where(band, am[:, 2 * C:4 * C], 0.0)

        w0, w1 = {}, {}
        for p in pairs:
            top0, top1 = am0[p][0:C], am1[p][0:C]
            ak = jnp.concatenate([jnp.where(head0, top0, 0.0), jnp.where(head0, 0.0, top1)], axis=0)
            vv = op(p, OP_V)
            g0 = gy[p][0:C]
            m = jnp.concatenate([g0, g0], axis=0) + _dot1(ak, jnp.concatenate([vv, vv], axis=0))
            w0[p] = jnp.where(head0, m[0:C], top0)
            w1[p] = jnp.where(head0, top1, m[C:2 * C])

        for lvl in range(SOLVE_LEVELS):
            prod0 = {p: _dot1(w0[p], jnp.concatenate([zero_half, w0[p]], axis=0)) for p in pairs}
            prod1 = {p: _dot1(w1[p], jnp.concatenate([w1[p], zero_half], axis=0)) for p in pairs}
            w0 = {p: jnp.where(head0, w0[p] + prod0[p], prod0[p]) for p in pairs}
            w1 = {p: jnp.where(head0, prod1[p], w1[p] + prod1[p]) for p in pairs}
        u = {p: jnp.where(head0, w0[p], w1[p]) for p in pairs}

        for p in pairs:
            vv = op(p, OP_V)
            y_lhs = jnp.concatenate([am0[p][C:2 * C], am1[p][C:2 * C]], axis=1)
            y_rhs = jnp.concatenate([jnp.where(head0, vv, 0.0), jnp.where(head0, u[p], 0.0),
                                     jnp.where(head0, 0.0, u[p]), jnp.where(head0, 0.0, vv)], axis=0)
            y_ref[p] = gy[p][C:2 * C] + _dot1(y_lhs, y_rhs)

        for p in pairs:
            decay_rows = jnp.broadcast_to(pc_ref[p][0:1, :], (LANES, LANES)).T
            upd_lhs = jnp.concatenate([op(p, OP_BH), op(p, OP_KH)], axis=0).T
            upd_rhs = jnp.concatenate([u[p], op(p, OP_V)], axis=0)
            s_ref[p] = s_ref[p] * decay_rows + jnp.where(blockdiag, _dot1(upd_lhs, upd_rhs), 0.0)

    y = jnp.concatenate([y_ref[p] for p in range(N_PAIRS)], axis=1)
    out_ref[...] = _rwkv_post(y, r, k2, v, g, rk_ref[...], gng_ref[...], gnb_ref[...], ones, passes=1)

    @pl.when(c == pl.num_programs(0) - 1)
    def _():
        state_ref[...] = s_ref[...]


def _prompt_rwkv(h_main, tail, prm):
    n_chunks = SEQ // CHUNK
    half = D_RKV // 2
    assert D_QKV == half
    vec = pl.BlockSpec((1, D_RWKV), lambda c: (0, 0))
    full = lambda a: pl.BlockSpec(a.shape, lambda c: (0,) * a.ndim)
    return pl.pallas_call(
        _prompt_rwkv_kernel,
        out_shape=(jax.ShapeDtypeStruct((SEQ, D_RWKV), F32),
                   jax.ShapeDtypeStruct((N_PAIRS, LANES, LANES), F32)),
        grid=(n_chunks,),
        in_specs=[pl.BlockSpec((CHUNK, half), lambda c: (c, 1)),
                  pl.BlockSpec((CHUNK, half), lambda c: (c, 2)),
                  pl.BlockSpec((CHUNK, D_TAIL), lambda c: (c, 0)),
                  full(prm["mu"]), full(prm["mu_tail"]),
                  vec, vec, vec, vec, vec, vec, vec,
                  full(prm["wd"]), full(prm["wa"]), full(prm["wg"])],
        out_specs=(pl.BlockSpec((CHUNK, D_RWKV), lambda c: (c, 0)),
                   pl.BlockSpec((N_PAIRS, LANES, LANES), lambda c: (0, 0, 0))),
        scratch_shapes=[pltpu.VMEM((SUBLANES, D_RKV), F32),
                        pltpu.VMEM((SUBLANES, D_TAIL), F32),
                        pltpu.VMEM((N_PAIRS, LANES, LANES), F32),
                        pltpu.VMEM((N_PAIRS, N_OPS, CHUNK, LANES), F32),
                        pltpu.VMEM((N_PAIRS, SUBLANES, LANES), F32),
                        pltpu.VMEM((N_PAIRS, CHUNK, LANES), F32)],
        compiler_params=_cparams(("arbitrary",)),
        name="prompt_rwkv",
    )(h_main, h_main, tail, prm["mu"], prm["mu_tail"], prm["w0"], prm["a0"], prm["k_k"], prm["k_a"],
      prm["r_k"], prm["gn_g"], prm["gn_b"], prm["wd"], prm["wa"], prm["wg"])


def _sample_prep_kernel(h_ref, x_ref, wt_ref, shift_ref, mu_ref, mut_ref, w0_ref, a0_ref, kk_ref, ka_ref,
                        wd_ref, wa_ref, wg_ref, r_ref, k_ref, v_ref, g_ref, tail_ref,
                        rt_ref, wtr_ref, kt_ref, vt_ref, at_ref, bt_ref):
    ones = _head_ones()
    feat = h_ref[:, D_QKV:D_MAIN]
    tail = _dot1(x_ref[...], wt_ref[...], NT)
    tail_ref[...] = tail
    mixed = _token_mix(feat, shift_ref[:, 0:D_RKV], mu_ref[...])
    mixed_tail = _token_mix(tail, shift_ref[:, D_RKV:D_SHIFT], mut_ref[...])
    r, ld, k2, v, av, bv, g = _rwkv_prep(mixed, mixed_tail, w0_ref[...], a0_ref[...], kk_ref[...], ka_ref[...],
                                         wd_ref[...], wa_ref[...], wg_ref[...], ones)
    r_ref[...] = r
    k_ref[...] = k2
    v_ref[...] = v
    g_ref[...] = g
    rt_ref[...] = r.T
    wtr_ref[...] = jnp.exp(ld).T
    kt_ref[...] = k2.T
    vt_ref[...] = v.T
    at_ref[...] = av.T
    bt_ref[...] = bv.T


def _sample_prep(h_main, x, shift, prm):
    tok = jax.ShapeDtypeStruct((DEC_BATCH, D_RWKV), F32)
    chan = jax.ShapeDtypeStruct((D_RWKV, DEC_BATCH), F32)
    return pl.pallas_call(
        _sample_prep_kernel,
        out_shape=(tok,) * 4 + (jax.ShapeDtypeStruct((DEC_BATCH, D_TAIL), F32),) + (chan,) * 6,
        compiler_params=pltpu.CompilerParams(vmem_limit_bytes=VMEM_LIMIT),
        name="sample_rwkv_prep",
    )(h_main, x, prm["w_tail"], shift, prm["mu"], prm["mu_tail"], prm["w0"], prm["a0"], prm["k_k"], prm["k_a"],
      prm["wd"], prm["wa"], prm["wg"])


STEP_GROUP = 4


def _sample_step_kernel(s_ref, r_ref, w_ref, k_ref, a_ref, b_ref, v_ref, y_ref, snew_ref):
    r, w, k, a, b = r_ref[...], w_ref[...], k_ref[...], a_ref[...], b_ref[...]
    for g0 in range(0, HEAD_DIM, 2 * STEP_GROUP):
        chans = range(g0, g0 + 2 * STEP_GROUP)
        sa = {i: jnp.sum(s_ref[0, i] * a, axis=0, keepdims=True) for i in chans}
        s_new = {i: s_ref[0, i] * w + sa[i] * b + v_ref[i:i + 1, :] * k for i in chans}
        for i in chans:
            y_ref[i:i + 1, :] = jnp.sum(s_new[i] * r, axis=0, keepdims=True)
        for i in range(g0, g0 + 2 * STEP_GROUP, 2):
            pair = jnp.concatenate([s_new[i], s_new[i + 1]], axis=0)
            snew_ref[:, i * HEAD_DIM:(i + 2) * HEAD_DIM] = pair.T


def _sample_step(state_t, r_t, w_t, k_t, a_t, b_t, v_t):
    head_rows = pl.BlockSpec((HEAD_DIM, DEC_BATCH), lambda h: (h, 0))
    return pl.pallas_call(
        _sample_step_kernel,
        out_shape=(jax.ShapeDtypeStruct((D_RWKV, DEC_BATCH), F32),
                   jax.ShapeDtypeStruct((DEC_BATCH, N_RWKV_HEADS * HEAD_DIM * HEAD_DIM), F32)),
        grid=(N_RWKV_HEADS,),
        in_specs=[pl.BlockSpec((1, HEAD_DIM, HEAD_DIM, DEC_BATCH), lambda h: (h, 0, 0, 0))] + [head_rows] * 6,
        out_specs=(head_rows, pl.BlockSpec((DEC_BATCH, HEAD_DIM * HEAD_DIM), lambda h: (0, h))),
        compiler_params=_cparams(("arbitrary",)),
        name="sample_rwkv_step",
    )(state_t, r_t, w_t, k_t, a_t, b_t, v_t)


def _sample_post_kernel(yt_ref, r_ref, k_ref, v_ref, g_ref, rk_ref, gng_ref, gnb_ref, o_ref):
    o_ref[...] = _rwkv_post(yt_ref[...].T, r_ref[...], k_ref[...], v_ref[...], g_ref[...], rk_ref[...],
                            gng_ref[...], gnb_ref[...], _head_ones())


def _sample_post(y, r, k, v, g, prm):
    return pl.pallas_call(
        _sample_post_kernel,
        out_shape=jax.ShapeDtypeStruct((DEC_BATCH, D_RWKV), F32),
        compiler_params=pltpu.CompilerParams(vmem_limit_bytes=VMEM_LIMIT),
        name="sample_rwkv_post",
    )(y, r, k, v, g, prm["r_k"], prm["gn_g"], prm["gn_b"])


def _project_mix(attn_ref, rwkv_ref, wo_ref):
    return (_dot(attn_ref[...].astype(BF16), wo_ref[0:D_ATTN, :])
            + _dot(rwkv_ref[...].astype(BF16), wo_ref[D_ATTN:D_ATTN + D_RWKV, :]))


def _norm_and_route(mix, x_ref, g_ref, b_ref, wr_ref, br_ref, x1_ref, x1b_ref, route_ref):
    x1 = _layer_norm(ALPHA * x_ref[...] + mix, g_ref[...], b_ref[...])
    x1_ref[...] = x1
    x1b = x1.astype(BF16)
    x1b_ref[...] = _pack_bf16_halves(x1b)
    logits = _dot(x1b, wr_ref[...].astype(BF16)) + br_ref[...]
    tm = logits.shape[0]
    lane = lax.broadcasted_iota(jnp.int32, (tm, LANES), 1).astype(F32)
    big = float(2 * LANES)
    neg = -jnp.inf
    lc = jnp.where(lane < N_GROUPS, logits, neg)
    mc = jnp.max(lc, axis=-1, keepdims=True)
    g_sel = jnp.min(jnp.where(lc == mc, lane, big), axis=-1, keepdims=True)
    p_group = 1.0 / jnp.sum(jnp.exp(lc - mc), axis=-1, keepdims=True)
    lo = ROUTE_FINE_OFF + g_sel * EXPERTS_PER_GROUP
    lf = jnp.where((lane >= lo) & (lane < lo + EXPERTS_PER_GROUP), logits, neg)
    v1 = jnp.max(lf, axis=-1, keepdims=True)
    i1 = jnp.min(jnp.where(lf == v1, lane, big), axis=-1, keepdims=True)
    lf2 = jnp.where(lane == i1, neg, lf)
    v2 = jnp.max(lf2, axis=-1, keepdims=True)
    i2 = jnp.min(jnp.where(lf2 == v2, lane, big), axis=-1, keepdims=True)
    e21 = jnp.exp(v2 - v1)
    gate1 = p_group / (1.0 + e21)
    gate2 = p_group * e21 / (1.0 + e21)
    route = jnp.where(lane == 0, i1 - ROUTE_FINE_OFF,
                      jnp.where(lane == 1, i2 - ROUTE_FINE_OFF,
                                jnp.where(lane == 2, gate1, jnp.where(lane == 3, gate2, 0.0))))
    route_ref[...] = route


N_ROUTER_OUTS = 3


def _outproj_router_kernel(n_tiles, n_aliased, attn_ref, rwkv_ref, x_ref, wo_ref, g_ref, b_ref, wr_ref, br_ref,
                           *rest):
    outs = rest[n_aliased:n_aliased + N_ROUTER_OUTS]
    mix_ref = rest[-1]
    i = pl.program_id(0)
    finish = lambda mix: _norm_and_route(mix, x_ref, g_ref, b_ref, wr_ref, br_ref, *outs)

    @pl.when(i == 0)
    def _():
        mix_ref[...] = _project_mix(attn_ref, rwkv_ref, wo_ref)

    @pl.when((i >= 1) & (i < n_tiles))
    def _():
        finish(mix_ref[...])
        mix_ref[...] = _project_mix(attn_ref, rwkv_ref, wo_ref)

    @pl.when(i == n_tiles)
    def _():
        finish(mix_ref[...])

    @pl.when(i > n_tiles)
    def _():
        for out_ref in outs:
            out_ref[...] = jnp.zeros_like(out_ref)


def _outproj_router(attn, rwkv, x, wo_bf16, ln_g, ln_b, w_route, b_route, tm, n_total, row_block, into, name):
    m = x.shape[0]
    n_tiles = m // tm
    const = lambda shape: pl.BlockSpec(shape, lambda i: (0, 0))
    ahead = lambda width: pl.BlockSpec((tm, width), lambda i: (jnp.minimum(i, n_tiles - 1), 0))
    behind = lambda width: pl.BlockSpec((tm, width), lambda i: (jnp.clip(i - 1, 0, n_tiles - 1), 0))
    in_specs = [ahead(D_ATTN), ahead(D_RWKV), behind(D_MODEL),
                const((D_MODEL, D_MODEL)), const((1, D_MODEL)), const((1, D_MODEL)),
                const((D_MODEL, LANES)), const((1, LANES))]
    args = [attn, rwkv, x, wo_bf16, ln_g, ln_b, w_route, b_route]
    aliases, n_aliased, fill_steps = {}, 0, pl.cdiv(n_total - m, tm)
    if into is not None:
        n_aliased, fill_steps = N_ROUTER_OUTS, 0
        in_specs += [pl.BlockSpec(memory_space=pl.ANY)] * N_ROUTER_OUTS
        aliases = {len(args) + k: k for k in range(N_ROUTER_OUTS)}
        args += list(into)
    out_rows = lambda width: pl.BlockSpec((tm, width), lambda i: (jnp.maximum(i - 1, 0) + row_block, 0))
    return pl.pallas_call(
        functools.partial(_outproj_router_kernel, n_tiles, n_aliased),
        out_shape=(jax.ShapeDtypeStruct((n_total, D_MODEL), F32),
                   jax.ShapeDtypeStruct((n_total, D_MODEL // 2), jnp.uint32),
                   jax.ShapeDtypeStruct((n_total, LANES), F32)),
        grid=(n_tiles + 1 + fill_steps,),
        in_specs=in_specs,
        out_specs=(out_rows(D_MODEL), out_rows(D_MODEL // 2), out_rows(LANES)),
        scratch_shapes=[pltpu.VMEM((tm, D_MODEL), F32)],
        input_output_aliases=aliases,
        compiler_params=_cparams(("arbitrary",)),
        name=name,
    )(*args)


DISPATCH_TILE = 128


def _dispatch_kernel(zoff_ref, dest_ref, x_ref, o_hbm, zbuf, ring, zsem, sem):
    i = pl.program_id(0)
    n_blocks = o_hbm.shape[0] // MOE_BLOCK
    n_used = zoff_ref[N_EXPERTS]

    def zero_fill(start_row):
        start_row = pl.multiple_of(start_row, MOE_BLOCK)
        return pltpu.make_async_copy(zbuf, o_hbm.at[pl.ds(start_row, MOE_BLOCK)], zsem)

    def zero_fills(action):
        for e in range(N_EXPERTS):
            @pl.when(zoff_ref[e] >= 0)
            def _():
                action(zero_fill(zoff_ref[e]))
        for b in range(n_blocks):
            @pl.when(b >= n_used)
            def _():
                action(zero_fill(b * MOE_BLOCK))

    @pl.when(i == 0)
    def _():
        zbuf[...] = jnp.zeros_like(zbuf)
        zero_fills(lambda copy: copy.start())
        zero_fills(lambda copy: copy.wait())

    cur = lax.rem(i, 2)

    def wait_rows(slot):
        for k in range(2):
            pltpu.make_async_copy(ring.at[slot], o_hbm.at[pl.ds(0, DISPATCH_TILE)], sem.at[slot]).wait()

    @pl.when(i >= 2)
    def _():
        wait_rows(cur)

    ring[cur] = x_ref[...]
    for t in range(DISPATCH_TILE):
        for k in range(2):
            pltpu.make_async_copy(ring.at[cur, pl.ds(t, 1)], o_hbm.at[pl.ds(dest_ref[0, 0, 2 * t + k], 1)],
                                  sem.at[cur]).start()

    @pl.when(i == pl.num_programs(0) - 1)
    def _():
        wait_rows(cur)

        @pl.when(i >= 1)
        def _():
            wait_rows(1 - cur)


def _dispatch(zero_offsets, dest, x_packed, n_blocks):
    n_tokens, width = x_packed.shape
    grid_spec = pltpu.PrefetchScalarGridSpec(
        num_scalar_prefetch=1,
        grid=(n_tokens // DISPATCH_TILE,),
        in_specs=[pl.BlockSpec((1, 1, 2 * DISPATCH_TILE), lambda i, z: (i, 0, 0), memory_space=pltpu.SMEM),
                  pl.BlockSpec((DISPATCH_TILE, width), lambda i, z: (i, 0))],
        out_specs=pl.BlockSpec(memory_space=pl.ANY),
        scratch_shapes=[pltpu.VMEM((MOE_BLOCK, width), x_packed.dtype),
                        pltpu.VMEM((2, DISPATCH_TILE, width), x_packed.dtype),
                        pltpu.SemaphoreType.DMA, pltpu.SemaphoreType.DMA((2,))],
    )
    return pl.pallas_call(
        _dispatch_kernel,
        out_shape=jax.ShapeDtypeStruct((n_blocks * MOE_BLOCK, width), x_packed.dtype),
        grid_spec=grid_spec,
        compiler_params=_cparams(("arbitrary",)),
        name="moe_dispatch",
    )(zero_offsets, dest.reshape(-1, 1, 2 * DISPATCH_TILE), x_packed)


def _expert_kernel(be_ref, nb_ref, x_ref, wg_hbm, wu_hbm, wd_hbm, o_ref, wg_buf, wu_buf, wd_buf, slot_ref, sem):
    blk = pl.program_id(0)
    n_used = nb_ref[0]
    expert = be_ref[blk]
    is_first = (blk == 0) | (be_ref[jnp.maximum(blk - 1, 0)] != expert)

    def fetch(e, slot):
        return [pltpu.make_async_copy(hbm.at[e], buf.at[slot], sem.at[slot, i])
                for i, (hbm, buf) in enumerate(((wg_hbm, wg_buf), (wu_hbm, wu_buf), (wd_hbm, wd_buf)))]

    @pl.when((blk < n_used) & is_first)
    def _():
        @pl.when(blk == 0)
        def _():
            slot_ref[0] = 1
            for copy in fetch(expert, 0):
                copy.start()

        slot = 1 - slot_ref[0]
        slot_ref[0] = slot
        for copy in fetch(expert, slot):
            copy.wait()
        nxt = lax.while_loop(lambda j: (j < n_used) & (be_ref[jnp.minimum(j, n_used - 1)] == expert),
                             lambda j: j + 1, blk + 1)

        @pl.when(nxt < n_used)
        def _():
            for copy in fetch(be_ref[jnp.minimum(nxt, n_used - 1)], 1 - slot):
                copy.start()

    @pl.when(blk < n_used)
    def _():
        slot = slot_ref[0]
        half = D_MODEL // 2
        x_head, x_tail = _unpack_bf16_halves(x_ref[...])
        proj = lambda w_buf: (_dot(x_head, w_buf[slot, 0:half, :].astype(BF16))
                              + _dot(x_tail, w_buf[slot, half:D_MODEL, :].astype(BF16)))
        gate = proj(wg_buf)
        up = proj(wu_buf)
        h = gate * _sigmoid(gate) * up
        o_ref[...] = _dot(h.astype(BF16), wd_buf[slot].astype(BF16))

    @pl.when(blk >= n_used)
    def _():
        o_ref[...] = jnp.zeros_like(o_ref)


def _expert_mlp(block_expert, n_used, x_sorted, w_gate, w_up, w_down, n_blocks):
    grid_spec = pltpu.PrefetchScalarGridSpec(
        num_scalar_prefetch=2,
        grid=(n_blocks,),
        in_specs=[pl.BlockSpec((MOE_BLOCK, D_MODEL // 2), lambda b, be, nb: (jnp.minimum(b, nb[0] - 1), 0)),
                  pl.BlockSpec(memory_space=pl.ANY), pl.BlockSpec(memory_space=pl.ANY),
                  pl.BlockSpec(memory_space=pl.ANY)],
        out_specs=pl.BlockSpec((MOE_BLOCK, D_MODEL), lambda b, be, nb: (b, 0)),
        scratch_shapes=[pltpu.VMEM((2, D_MODEL, D_EXPERT), F32), pltpu.VMEM((2, D_MODEL, D_EXPERT), F32),
                        pltpu.VMEM((2, D_EXPERT, D_MODEL), F32), pltpu.SMEM((1,), jnp.int32),
                        pltpu.SemaphoreType.DMA((2, 3))],
    )
    return pl.pallas_call(
        _expert_kernel,
        out_shape=jax.ShapeDtypeStruct((n_blocks * MOE_BLOCK, D_MODEL), F32),
        grid_spec=grid_spec,
        compiler_params=_cparams(("arbitrary",)),
        name="expert_mlp",
    )(block_expert, n_used, x_sorted, w_gate, w_up, w_down)


COMBINE_TILE = 256


def _combine_kernel(dest_ref, dest_next_ref, y_hbm, x1_ref, route_ref, g_ref, b_ref, o_ref, ybuf, sem):
    i = pl.program_id(0)
    cur = lax.rem(i, 2)
    tile = x1_ref.shape[0]
    n_rows = 2 * tile

    def gather(table_ref, buf):
        for slot in range(n_rows):
            pltpu.make_async_copy(y_hbm.at[pl.ds(table_ref[0, 0, slot], 1)], ybuf.at[buf, pl.ds(slot, 1)],
                                  sem.at[buf]).start()

    def wait_gather(buf):
        pltpu.make_async_copy(y_hbm.at[pl.ds(0, n_rows)], ybuf.at[buf], sem.at[buf]).wait()

    @pl.when(i == 0)
    def _():
        gather(dest_ref, 0)

    gather(dest_next_ref, 1 - cur)
    wait_gather(cur)
    route = route_ref[...]
    yb = ybuf[cur]
    moe = route[:, 2:3] * yb[0:tile, :] + route[:, 3:4] * yb[tile:n_rows, :]
    o_ref[...] = _layer_norm(ALPHA * x1_ref[...] + moe, g_ref[...], b_ref[...])

    @pl.when(i == pl.num_programs(0) - 1)
    def _():
        wait_gather(1 - cur)


def _combine(dest, y_slots, x1_all, route_all, m, tm, row_block, ln_g, ln_b, name):
    return pl.pallas_call(
        _combine_kernel,
        out_shape=jax.ShapeDtypeStruct((m, D_MODEL), F32),
        grid=(m // tm,),
        in_specs=[pl.BlockSpec((1, 1, 2 * tm), lambda i: (i, 0, 0), memory_space=pltpu.SMEM),
                  pl.BlockSpec((1, 1, 2 * tm), lambda i: (i + 1, 0, 0), memory_space=pltpu.SMEM),
                  pl.BlockSpec(memory_space=pl.ANY),
                  pl.BlockSpec((tm, D_MODEL), lambda i: (i + row_block, 0)),
                  pl.BlockSpec((tm, LANES), lambda i: (i + row_block, 0)),
                  pl.BlockSpec((1, D_MODEL), lambda i: (0, 0)),
                  pl.BlockSpec((1, D_MODEL), lambda i: (0, 0))],
        out_specs=pl.BlockSpec((tm, D_MODEL), lambda i: (i, 0)),
        scratch_shapes=[pltpu.VMEM((2, 2 * tm, D_MODEL), F32), pltpu.SemaphoreType.DMA((2,))],
        compiler_params=_cparams(("arbitrary",)),
        name=name,
    )(dest, dest, y_slots, x1_all, route_all, ln_g, ln_b)


def _dispatch_plan(route_all, n_blocks):
    experts = route_all[:, 0:2].astype(jnp.int32)
    onehot = (experts[:, :, None] == jnp.arange(N_EXPERTS, dtype=jnp.int32)).astype(jnp.int32)
    per_token = jnp.sum(onehot, axis=1)
    csum = jnp.cumsum(per_token, axis=0)
    rank = jnp.sum(onehot * (csum - per_token)[:, None, :], axis=2)
    rank = rank.at[:, 1].add((experts[:, 0] == experts[:, 1]).astype(jnp.int32))
    counts = csum[-1]
    flat_e, rank = experts.reshape(-1), rank.reshape(-1)
    padded = (counts + MOE_BLOCK - 1) // MOE_BLOCK * MOE_BLOCK
    pend = jnp.cumsum(padded)
    pstart = pend - padded
    dest = (pstart[flat_e] + rank).astype(jnp.int32)
    zero_offsets = jnp.where(counts > 0, pend - MOE_BLOCK, -1).astype(jnp.int32)
    n_used = (pend[-1] // MOE_BLOCK).astype(jnp.int32)
    block_start = jnp.minimum(jnp.arange(n_blocks, dtype=jnp.int32), n_used - 1) * MOE_BLOCK
    block_e = jnp.minimum(jnp.searchsorted(pend, block_start, side="right"), N_EXPERTS - 1).astype(jnp.int32)
    return dest, jnp.concatenate([zero_offsets, n_used.reshape(1)]), block_e, n_used.reshape(1)


def kernel(x_prompt, x_sample, cache_k_win, cache_v_win, state_wkv, state_shift, w_in, attn_sinks, shift_mu, w0,
           w_decay_up, a0, w_a_up, w_g_up, k_k, k_a, r_k, gn_g, gn_b, w_out, ln1_g, ln1_b, w_coarse, b_coarse,
           w_fine, b_fine, w_exp_gate, w_exp_up, w_exp_down, ln2_g, ln2_b):
    xp = x_prompt[0]
    xs = x_sample[:, 0]
    row = lambda a: a.reshape(1, -1)

    w_in_t = jnp.swapaxes(w_in[0], 0, 1)
    prm = dict(mu=row(shift_mu[0, :D_RKV]), mu_tail=row(shift_mu[0, D_RKV:]), w_tail=w_in_t[D_MAIN:].astype(BF16),
               w0=row(w0[0]), a0=row(a0[0]), k_k=row(k_k[0]), k_a=row(k_a[0]),
               r_k=row(r_k[0]), gn_g=row(gn_g[0]), gn_b=row(gn_b[0]),
               wd=w_decay_up[0], wa=w_a_up[0], wg=w_g_up[0])
    sinks = attn_sinks[0]
    wo_bf16 = w_out[0].astype(BF16)
    w_route = jnp.pad(jnp.concatenate([w_coarse[0], w_fine[0]], axis=1), ((0, 0), (0, LANES - N_GROUPS - N_EXPERTS)))
    b_route = jnp.pad(jnp.concatenate([b_coarse[0], b_fine[0]]), (0, LANES - N_GROUPS - N_EXPERTS)).reshape(1, LANES)

    hp, tail_p = _matmul_with_tail(xp, w_in_t, D_MAIN, prm["w_tail"], MAIN_TM, MAIN_TN, "in_proj_prompt")
    hs = _matmul(xs, w_in_t, D_MAIN, DEC_BATCH, MAIN_TN, "in_proj_sample")

    attn_p = _prompt_attention(hp, sinks)
    rwkv_p, state_p = _prompt_rwkv(hp, tail_p, prm)

    q_s = hs[:, :D_ATTN].reshape(DEC_BATCH, N_Q_HEADS, HEAD_DIM)
    k_s = hs[:, D_ATTN:D_ATTN + D_KV].reshape(DEC_BATCH, 1, D_KV)
    v_s = hs[:, D_ATTN + D_KV:D_QKV].reshape(DEC_BATCH, 1, D_KV)
    window_t = lambda c: jnp.transpose(c, (0, 2, 3, 1)).reshape(DEC_BATCH, D_KV, WINDOW)
    attn_s, kwin_s, vwin_s = _sample_attention(
        q_s, k_s, v_s, window_t(cache_k_win[0]), window_t(cache_v_win[0]), sinks.reshape(N_Q_HEADS, 1))
    r_s, k2_s, vv_s, g_s, tail_s, r_t, w_t, k_t, v_t, a_t, b_t = _sample_prep(hs, xs, state_shift[0], prm)
    y_t, state_s = _sample_step(jnp.transpose(state_wkv[0], (1, 2, 3, 0)), r_t, w_t, k_t, a_t, b_t, v_t)
    state_s = state_s.reshape(DEC_BATCH, N_RWKV_HEADS, HEAD_DIM, HEAD_DIM)
    rwkv_s = _sample_post(y_t, r_s, k2_s, vv_s, g_s, prm)

    n_tokens = SEQ + DEC_BATCH
    outs_pr = _outproj_router(attn_p, rwkv_p, xp, wo_bf16, row(ln1_g[0]), row(ln1_b[0]), w_route, b_route,
                              OUTPROJ_TM, n_tokens, 0, None, "outproj_router_prompt")
    x1_all, x1b_all, route_all = _outproj_router(attn_s.reshape(DEC_BATCH, D_ATTN), rwkv_s, xs,
                                                 wo_bf16, row(ln1_g[0]), row(ln1_b[0]), w_route, b_route,
                                                 DEC_BATCH, n_tokens, SEQ // DEC_BATCH, outs_pr,
                                                 "outproj_router_sample")

    n_assign = 2 * n_tokens
    n_blocks = -(-(n_assign + N_EXPERTS * (MOE_BLOCK - 1)) // MOE_BLOCK)
    dest, zero_offsets, block_e, n_used = _dispatch_plan(route_all, n_blocks)
    x_sorted = _dispatch(zero_offsets, dest, x1b_all, n_blocks)
    y_slots = _expert_mlp(block_e, n_used, x_sorted, w_exp_gate[0], w_exp_up[0], w_exp_down[0], n_blocks)

    def dest_tiles(d, tile):
        d = d.reshape(-1, tile, 2)
        d = jnp.concatenate([d[:, :, 0], d[:, :, 1]], axis=1)
        return jnp.pad(d, ((0, 1), (0, 0))).reshape(-1, 1, 2 * tile)

    y_p = _combine(dest_tiles(dest[:2 * SEQ], COMBINE_TILE), y_slots, x1_all, route_all, SEQ, COMBINE_TILE, 0,
                   row(ln2_g[0]), row(ln2_b[0]), "combine_prompt")
    y_s = _combine(dest_tiles(dest[2 * SEQ:], DEC_BATCH), y_slots, x1_all, route_all, DEC_BATCH, DEC_BATCH,
                   SEQ // DEC_BATCH, row(ln2_g[0]), row(ln2_b[0]), "combine_sample")

    kv4 = lambda a: a.reshape(a.shape[0], N_KV_HEADS, HEAD_DIM)
    k_win_p = kv4(hp[SEQ - WINDOW:, D_ATTN:D_ATTN + D_KV])[None, None]
    v_win_p = kv4(hp[SEQ - WINDOW:, D_ATTN + D_KV:D_QKV])[None, None]
    sp = state_p.reshape(N_PAIRS, HEADS_PER_TILE, HEAD_DIM, HEADS_PER_TILE, HEAD_DIM)
    wkv_p = jnp.stack([sp[:, i, :, i, :] for i in range(HEADS_PER_TILE)], axis=1)
    wkv_p = wkv_p.reshape(N_RWKV_HEADS, HEAD_DIM, HEAD_DIM).transpose(0, 2, 1)[None, None]
    shift_p = jnp.concatenate([hp[SEQ - 1:SEQ, D_QKV:], tail_p[SEQ - 1:SEQ]], axis=1)[None]
    shift_s = jnp.concatenate([hs[:, D_QKV:], tail_s], axis=1)[None]
    return (y_p[None], y_s[:, None, :], k_win_p, v_win_p, wkv_p, shift_p,
            kwin_s.reshape(1, DEC_BATCH, WINDOW, N_KV_HEADS, HEAD_DIM),
            vwin_s.reshape(1, DEC_BATCH, WINDOW, N_KV_HEADS, HEAD_DIM),
            state_s[None], shift_s)
```

```python
import functools
import math

import jax
import jax.numpy as jnp
from jax import lax
from jax.experimental import pallas as pl
from jax.experimental.pallas import tpu as pltpu

F32 = jnp.float32
BF16 = jnp.bfloat16

D_MODEL = 2048
SEQ = 8192
DEC_BATCH = 128
HEAD_DIM = 64
D_ATTN = 1024
D_RWKV = 1024
N_Q_HEADS = 16
N_KV_HEADS = 4
Q_PER_KV = 4
D_KV = 256
WINDOW = 128
ATTN_SCALE = HEAD_DIM ** -0.5
N_RWKV_HEADS = 16
W_LORA = 64
A_LORA = 64
G_LORA = 160
D_SHIFT = 3 * D_RWKV + W_LORA + A_LORA + G_LORA
D_QKV = D_ATTN + 2 * D_KV
N_GROUPS = 4
EXPERTS_PER_GROUP = 8
N_EXPERTS = 32
D_EXPERT = 512
ALPHA = 2.0 ** 0.25
LN_EPS = 1e-5
GN_EPS = 64e-5

SUBLANES = 8
LANES = 128
VMEM_LIMIT = 52 * 1024 * 1024

D_RKV = 3 * D_RWKV
D_TAIL = W_LORA + A_LORA + G_LORA
D_MAIN = D_QKV + D_RKV
MAIN_TN = 1536
MAIN_TM = 512
OUTPROJ_TM = 256

CHUNK = 64
HEADS_PER_TILE = LANES // HEAD_DIM
N_PAIRS = N_RWKV_HEADS // HEADS_PER_TILE
SOLVE_LEVELS = int(math.log2(CHUNK))
PAIR_GROUP = 8

MOE_BLOCK = 256
ROUTE_FINE_OFF = N_GROUPS

NN = (((1,), (0,)), ((), ()))
NT = (((1,), (1,)), ((), ()))


def _dot(a, b, dims=NN):
    return lax.dot_general(a, b, dims, preferred_element_type=F32)


def _dot1(a, b, dims=NN):
    return _dot(a.astype(BF16), b.astype(BF16), dims)


def _split(x):
    hi = x.astype(BF16)
    lo = (x - hi.astype(F32)).astype(BF16)
    return hi, lo


def _dot_exact_lhs(a_bf16, b, dims=NN):
    bh, bl = _split(b)
    return _dot(a_bf16, bh, dims) + _dot(a_bf16, bl, dims)


def _dot_exact_rhs(a, b_bf16, dims=NN):
    ah, al = _split(a)
    return _dot(ah, b_bf16, dims) + _dot(al, b_bf16, dims)


def _div_pow2(x, d):
    return lax.shift_right_logical(x, jnp.int32(int(math.log2(d))))


def _mod_pow2(x, d):
    return lax.bitwise_and(x, jnp.int32(d - 1))


def _pack_bf16_halves(x_bf16):
    n = x_bf16.shape[1] // 2
    bits = lax.bitcast_convert_type(x_bf16.astype(F32), jnp.uint32)
    return lax.bitwise_or(bits[:, 0:n], lax.shift_right_logical(bits[:, n:2 * n], jnp.uint32(16)))


def _unpack_bf16_halves(packed):
    hi = lax.bitcast_convert_type(lax.bitwise_and(packed, jnp.uint32(0xFFFF0000)), F32)
    lo = lax.bitcast_convert_type(lax.shift_left(packed, jnp.uint32(16)), F32)
    return hi.astype(BF16), lo.astype(BF16)


def _sigmoid(x):
    return 1.0 / (1.0 + jnp.exp(-x))


def _softplus(x):
    return jnp.maximum(x, 0.0) + jnp.log(1.0 + jnp.exp(-jnp.abs(x)))


def _layer_norm(z, g, b):
    mu = jnp.mean(z, axis=-1, keepdims=True)
    d = z - mu
    var = jnp.mean(d * d, axis=-1, keepdims=True)
    return d * lax.rsqrt(var + LN_EPS) * g + b


def _cparams(sem):
    return pltpu.CompilerParams(dimension_semantics=sem, vmem_limit_bytes=VMEM_LIMIT)


def _matmul_kernel(x_ref, wt_ref, o_ref):
    o_ref[...] = _dot(x_ref[...].astype(BF16), wt_ref[...].astype(BF16), NT)


def _matmul(x, w_t, n_out, tm, tn, name):
    m, k = x.shape
    tm = min(tm, m)
    return pl.pallas_call(
        _matmul_kernel,
        out_shape=jax.ShapeDtypeStruct((m, n_out), F32),
        grid=(n_out // tn, m // tm),
        in_specs=[pl.BlockSpec((tm, k), lambda j, i: (i, 0)),
                  pl.BlockSpec((tn, k), lambda j, i: (j, 0))],
        out_specs=pl.BlockSpec((tm, tn), lambda j, i: (i, j)),
        compiler_params=_cparams(("arbitrary", "arbitrary")),
        name=name,
    )(x, w_t)


def _matmul_with_tail_kernel(x_ref, wt_ref, wtail_ref, o_ref, tail_ref):
    xb = x_ref[...].astype(BF16)
    o_ref[...] = _dot(xb, wt_ref[...].astype(BF16), NT)

    @pl.when(pl.program_id(0) == 0)
    def _():
        tail_ref[...] = _dot(xb, wtail_ref[...], NT)

    @pl.when(pl.program_id(0) > 0)
    def _():
        tail_ref[...] = jnp.zeros_like(tail_ref)


def _matmul_with_tail(x, w_t, n_out, wtail_t, tm, tn, name):
    m, k = x.shape
    n_rows = m // tm
    n_tail = wtail_t.shape[0]
    return pl.pallas_call(
        _matmul_with_tail_kernel,
        out_shape=(jax.ShapeDtypeStruct((m, n_out), F32), jax.ShapeDtypeStruct((m + tm, n_tail), F32)),
        grid=(n_out // tn, n_rows),
        in_specs=[pl.BlockSpec((tm, k), lambda j, i: (i, 0)),
                  pl.BlockSpec((tn, k), lambda j, i: (j, 0)),
                  pl.BlockSpec((n_tail, k), lambda j, i: (0, 0))],
        out_specs=(pl.BlockSpec((tm, tn), lambda j, i: (i, j)),
                   pl.BlockSpec((tm, n_tail), lambda j, i: (jnp.where(j == 0, i, n_rows), 0))),
        compiler_params=_cparams(("arbitrary", "arbitrary")),
        name=name,
    )(x, w_t, wtail_t)


def _band_bias():
    qi = jnp.arange(Q_PER_KV * WINDOW)[:, None] % WINDOW
    kj = jnp.arange(2 * WINDOW)[None, :]
    diff = qi + WINDOW - kj
    band = (diff >= 0) & (diff <= WINDOW)
    keep = jnp.stack([band & (kj >= WINDOW), band])
    return jnp.where(keep, 0.0, -jnp.inf).astype(F32)


def _prompt_attn_kernel(q_ref, kvp_ref, kvc_ref, bias_ref, sink_ref, o_ref):
    q = q_ref[...]
    kv_prev = kvp_ref[...]
    kv_cur = kvc_ref[...]
    bias = bias_ref[0]
    row_head = _div_pow2(lax.broadcasted_iota(jnp.int32, (Q_PER_KV * WINDOW, 1), 0), WINDOW)
    groups = range(N_KV_HEADS)
    kv_cols = lambda off, g: jnp.concatenate([kv_prev[:, off + g * HEAD_DIM:off + (g + 1) * HEAD_DIM],
                                              kv_cur[:, off + g * HEAD_DIM:off + (g + 1) * HEAD_DIM]],
                                             axis=0).astype(BF16)
    q_rows = lambda g: jnp.concatenate(
        [q[:, (g * Q_PER_KV + h) * HEAD_DIM:(g * Q_PER_KV + h + 1) * HEAD_DIM] for h in range(Q_PER_KV)],
        axis=0).astype(BF16)
    s = [_dot(q_rows(g), kv_cols(0, g), NT) * ATTN_SCALE + bias for g in groups]
    sink = []
    for g in groups:
        col = jnp.zeros((Q_PER_KV * WINDOW, 1), F32)
        for h in range(Q_PER_KV):
            col = jnp.where(row_head == h, sink_ref[g * Q_PER_KV + h], col)
        sink.append(col)
    m = [jnp.maximum(jnp.max(s[g], axis=-1, keepdims=True), sink[g]) for g in groups]
    p = [jnp.exp(s[g] - m[g]) for g in groups]
    denom = [jnp.sum(p[g], axis=-1, keepdims=True) + jnp.exp(sink[g] - m[g]) for g in groups]
    o = [_dot((p[g] / denom[g]).astype(BF16), kv_cols(D_KV, g)) for g in groups]
    o_ref[...] = jnp.concatenate([o[g][h * WINDOW:(h + 1) * WINDOW, :] for g in groups for h in range(Q_PER_KV)],
                                 axis=1)


def _prompt_attention(h_attn, sinks):
    nb = SEQ // WINDOW
    return pl.pallas_call(
        _prompt_attn_kernel,
        out_shape=jax.ShapeDtypeStruct((SEQ, D_ATTN), F32),
        grid=(nb,),
        in_specs=[pl.BlockSpec((WINDOW, D_ATTN), lambda i: (i, 0)),
                  pl.BlockSpec((WINDOW, 2 * D_KV), lambda i: (jnp.maximum(i - 1, 0), 2)),
                  pl.BlockSpec((WINDOW, 2 * D_KV), lambda i: (i, 2)),
                  pl.BlockSpec((1, Q_PER_KV * WINDOW, 2 * WINDOW), lambda i: (jnp.minimum(i, 1), 0, 0)),
                  pl.BlockSpec(memory_space=pltpu.SMEM)],
        out_specs=pl.BlockSpec((WINDOW, D_ATTN), lambda i: (i, 0)),
        compiler_params=_cparams(("arbitrary",)),
        name="prompt_attention",
    )(h_attn, h_attn, h_attn, _band_bias(), sinks)


SAMPLE_ATTN_TILE = 8


def _sample_attn_kernel(q_ref, knew_ref, vnew_ref, ck_ref, cv_ref, sink_ref, o_ref, kwin_ref, vwin_ref):
    lane = lax.broadcasted_iota(jnp.int32, (N_Q_HEADS, D_KV), 1)
    head = lax.broadcasted_iota(jnp.int32, (N_Q_HEADS, D_KV), 0)
    group_mask = _div_pow2(lane, HEAD_DIM) == _div_pow2(head, Q_PER_KV)
    sink = sink_ref[...]
    row = lax.broadcasted_iota(jnp.int32, (WINDOW, D_KV), 0)
    seqs = range(SAMPLE_ATTN_TILE)
    qbd = [jnp.where(group_mask, jnp.concatenate([q_ref[b]] * N_KV_HEADS, axis=1), 0.0).astype(BF16) for b in seqs]
    s = [_dot1(qbd[b], ck_ref[b]) * ATTN_SCALE for b in seqs]
    s_new = [jnp.sum(qbd[b].astype(F32) * knew_ref[b].astype(BF16).astype(F32), axis=-1, keepdims=True) * ATTN_SCALE
             for b in seqs]
    m = [jnp.maximum(jnp.maximum(jnp.max(s[b], axis=-1, keepdims=True), s_new[b]), sink) for b in seqs]
    p = [jnp.exp(s[b] - m[b]) for b in seqs]
    p_new = [jnp.exp(s_new[b] - m[b]) for b in seqs]
    denom = [jnp.sum(p[b], axis=-1, keepdims=True) + p_new[b] + jnp.exp(sink - m[b]) for b in seqs]
    for b in seqs:
        kb = ck_ref[b].T
        vb = cv_ref[b].T
        kn = knew_ref[b]
        vn = vnew_ref[b]
        o_full = (_dot1(p[b] / denom[b], vb)
                  + (p_new[b] / denom[b]).astype(BF16).astype(F32) * vn.astype(BF16).astype(F32))
        o_full = jnp.where(group_mask, o_full, 0.0)
        o = o_full[:, 0:HEAD_DIM]
        for g in range(1, N_KV_HEADS):
            o = o + o_full[:, g * HEAD_DIM:(g + 1) * HEAD_DIM]
        o_ref[b] = o
        kwin_ref[b] = jnp.where(row == WINDOW - 1, kn, pltpu.roll(kb, WINDOW - 1, axis=0))
        vwin_ref[b] = jnp.where(row == WINDOW - 1, vn, pltpu.roll(vb, WINDOW - 1, axis=0))


def _sample_attention(q, k_new, v_new, cache_k_t, cache_v_t, sinks):
    bt = SAMPLE_ATTN_TILE
    win_spec = pl.BlockSpec((bt, WINDOW, D_KV), lambda i: (i, 0, 0))
    win_t_spec = pl.BlockSpec((bt, D_KV, WINDOW), lambda i: (i, 0, 0))
    new_spec = pl.BlockSpec((bt, 1, D_KV), lambda i: (i, 0, 0))
    return pl.pallas_call(
        _sample_attn_kernel,
        out_shape=(jax.ShapeDtypeStruct((DEC_BATCH, N_Q_HEADS, HEAD_DIM), F32),
                   jax.ShapeDtypeStruct((DEC_BATCH, WINDOW, D_KV), F32),
                   jax.ShapeDtypeStruct((DEC_BATCH, WINDOW, D_KV), F32)),
        grid=(DEC_BATCH // bt,),
        in_specs=[pl.BlockSpec((bt, N_Q_HEADS, HEAD_DIM), lambda i: (i, 0, 0)),
                  new_spec, new_spec, win_t_spec, win_t_spec,
                  pl.BlockSpec((N_Q_HEADS, 1), lambda i: (0, 0))],
        out_specs=(pl.BlockSpec((bt, N_Q_HEADS, HEAD_DIM), lambda i: (i, 0, 0)), win_spec, win_spec),
        compiler_params=_cparams(("arbitrary",)),
        name="sample_attention",
    )(q, k_new, v_new, cache_k_t, cache_v_t, sinks)


def _head_ones():
    r = _div_pow2(lax.broadcasted_iota(jnp.int32, (LANES, LANES), 0), HEAD_DIM)
    c = _div_pow2(lax.broadcasted_iota(jnp.int32, (LANES, LANES), 1), HEAD_DIM)
    return jnp.where(r == c, 1.0, 0.0).astype(BF16)


def _head_sum(x, ones, passes=2):
    dot = _dot_exact_rhs if passes == 2 else _dot1
    parts = [dot(x[:, p * LANES:(p + 1) * LANES], ones) for p in range(x.shape[1] // LANES)]
    return jnp.concatenate(parts, axis=1)


def _token_mix(feat, shifted, mu):
    return feat + (shifted - feat) * mu


def _rwkv_prep(mixed, mixed_tail, w0, a0, k_k, k_a, wd, wa, wg, ones):
    r = mixed[:, 0:D_RWKV]
    k = mixed[:, D_RWKV:2 * D_RWKV]
    v = mixed[:, 2 * D_RWKV:3 * D_RWKV]
    xw = mixed_tail[:, 0:W_LORA]
    xa = mixed_tail[:, W_LORA:W_LORA + A_LORA]
    xg = mixed_tail[:, W_LORA + A_LORA:D_TAIL]
    w_log = -_softplus(-(w0 + _dot1(jnp.tanh(xw), wd))) - 0.5
    log_decay = -jnp.exp(w_log)
    a = _sigmoid(a0 + _dot1(xa, wa))
    g = _dot1(_sigmoid(xg), wg)
    kk = k * k_k
    kk = kk * lax.rsqrt(jnp.maximum(_head_sum(kk * kk, ones), 1e-24))
    k2 = k * (1.0 + (a - 1.0) * k_a)
    return r, log_decay, k2, v, -kk, kk * a, g


def _rwkv_post(y, r, k2, v, g, r_k, gn_g, gn_b, ones, passes=2):
    inv_n = 1.0 / HEAD_DIM
    mu = _head_sum(y, ones, passes) * inv_n
    d = y - mu
    var = _head_sum(d * d, ones, passes) * inv_n
    yn = d * lax.rsqrt(var + GN_EPS) * gn_g + gn_b
    bonus = _head_sum(r * k2 * r_k, ones, passes) * v
    return (yn + bonus) * g


(OP_AABS, OP_RABS, OP_AN, OP_RN, OP_BN, OP_KN, OP_BH, OP_KH, OP_V) = range(9)
N_OPS = 9


def _prompt_rwkv_kernel(f1_ref, f2_ref, tail_in_ref, mu_ref, mut_ref, w0_ref, a0_ref, kk_ref, ka_ref, rk_ref,
                        gng_ref, gnb_ref, wd_ref, wa_ref, wg_ref, out_ref, state_ref,
                        prev_ref, prevt_ref, s_ref, ops_ref, pc_ref, y_ref):
    c = pl.program_id(0)
    C = CHUNK

    @pl.when(c == 0)
    def _():
        prev_ref[...] = jnp.zeros_like(prev_ref)
        prevt_ref[...] = jnp.zeros_like(prevt_ref)
        s_ref[...] = jnp.zeros_like(s_ref)

    ones = _head_ones()
    row = lax.broadcasted_iota(jnp.int32, (C, 1), 0)

    def token_shift(feat, carry_ref):
        shifted = jnp.where(row == 0, carry_ref[0:1, :], pltpu.roll(feat, 1, axis=0))
        carry_ref[0:1, :] = feat[C - 1:C, :]
        return shifted

    feat = jnp.concatenate([f1_ref[...], f2_ref[...]], axis=1)
    tail = tail_in_ref[...]
    mixed = _token_mix(feat, token_shift(feat, prev_ref), mu_ref[...])
    mixed_tail = _token_mix(tail, token_shift(tail, prevt_ref), mut_ref[...])
    r, ld, k2, v, av, bv, g = _rwkv_prep(mixed, mixed_tail, w0_ref[...], a0_ref[...], kk_ref[...], ka_ref[...],
                                         wd_ref[...], wa_ref[...], wg_ref[...], ones)

    ti = lax.broadcasted_iota(jnp.int32, (C, C), 0)
    tj = lax.broadcasted_iota(jnp.int32, (C, C), 1)
    tri_incl = jnp.where(tj <= ti, 1.0, 0.0).astype(BF16)
    cs = _dot_exact_lhs(tri_incl, ld)
    cs_ref = cs[C // 2 - 1:C // 2, :]
    cs_end = cs[C - 1:C, :]
    e_prev = jnp.exp(cs - ld)
    e_cur = jnp.exp(cs)
    n_prev = jnp.exp(cs - ld - cs_ref)
    n_cur = jnp.exp(cs - cs_ref)
    n_inv = jnp.exp(cs_ref - cs)
    e_tail = jnp.exp(cs_end - cs)
    ops = {OP_AABS: av * e_prev, OP_RABS: r * e_cur, OP_AN: av * n_prev, OP_RN: r * n_cur,
           OP_BN: bv * n_inv, OP_KN: k2 * n_inv, OP_BH: bv * e_tail, OP_KH: k2 * e_tail, OP_V: v}
    p_end = jnp.exp(cs_end)
    for p in range(N_PAIRS):
        sl = slice(p * LANES, (p + 1) * LANES)
        for idx, val in ops.items():
            ops_ref[p, idx] = val[:, sl]
        pc_ref[p] = jnp.broadcast_to(p_end[:, sl], (SUBLANES, LANES))

    lane1 = lax.broadcasted_iota(jnp.int32, (C, LANES), 1)
    head0 = lane1 < HEAD_DIM
    r2 = lax.broadcasted_iota(jnp.int32, (2 * C, 2 * C), 0)
    c2 = lax.broadcasted_iota(jnp.int32, (2 * C, 2 * C), 1)
    tq = _mod_pow2(r2, C)
    tk = _mod_pow2(c2, C)
    band = (tk < tq) | ((tk == tq) & (r2 >= C))
    blockdiag = _div_pow2(r2, HEAD_DIM) == _div_pow2(c2, HEAD_DIM)

    op = lambda p, idx: ops_ref[p, idx]
    zero_half = jnp.zeros((C, LANES), F32)
    for pairs in [range(g, g + PAIR_GROUP) for g in range(0, N_PAIRS, PAIR_GROUP)]:
        gy = {p: _dot1(jnp.concatenate([op(p, OP_AABS), op(p, OP_RABS)], axis=0), s_ref[p]) for p in pairs}

        am0, am1 = {}, {}
        for p in pairs:
            a_n, r_n = op(p, OP_AN), op(p, OP_RN)
            b0, k0 = jnp.where(head0, op(p, OP_BN), 0.0), jnp.where(head0, op(p, OP_KN), 0.0)
            b1, k1 = jnp.where(head0, 0.0, op(p, OP_BN)), jnp.where(head0, 0.0, op(p, OP_KN))
            am = _dot1(jnp.concatenate([a_n, r_n], axis=0), jnp.concatenate([k0, b0, b1, k1], axis=0), NT)
            am0[p] = jnp.where(band, am[:, 0:2 * C], 0.0)
            am1[p] = jnp.where(band, am[:, 2 * C:4 * C], 0.0)

        w0, w1 = {}, {}
        for p in pairs:
            top0, top1 = am0[p][0:C], am1[p][0:C]
            ak = jnp.concatenate([jnp.where(head0, top0, 0.0), jnp.where(head0, 0.0, top1)], axis=0)
            vv = op(p, OP_V)
            g0 = gy[p][0:C]
            m = jnp.concatenate([g0, g0], axis=0) + _dot1(ak, jnp.concatenate([vv, vv], axis=0))
            w0[p] = jnp.where(head0, m[0:C], top0)
            w1[p] = jnp.where(head0, top1, m[C:2 * C])

        for lvl in range(SOLVE_LEVELS):
            prod0 = {p: _dot1(w0[p], jnp.concatenate([zero_half, w0[p]], axis=0)) for p in pairs}
            prod1 = {p: _dot1(w1[p], jnp.concatenate([w1[p], zero_half], axis=0)) for p in pairs}
            w0 = {p: jnp.where(head0, w0[p] + prod0[p], prod0[p]) for p in pairs}
            w1 = {p: jnp.where(head0, prod1[p], w1[p] + prod1[p]) for p in pairs}
        u = {p: jnp.where(head0, w0[p], w1[p]) for p in pairs}

        for p in pairs:
            vv = op(p, OP_V)
            y_lhs = jnp.concatenate([am0[p][C:2 * C], am1[p][C:2 * C]], axis=1)
            y_rhs = jnp.concatenate([jnp.where(head0, vv, 0.0), jnp.where(head0, u[p], 0.0),
                                     jnp.where(head0, 0.0, u[p]), jnp.where(head0, 0.0, vv)], axis=0)
            y_ref[p] = gy[p][C:2 * C] + _dot1(y_lhs, y_rhs)

        for p in pairs:
            decay_rows = jnp.broadcast_to(pc_ref[p][0:1, :], (LANES, LANES)).T
            upd_lhs = jnp.concatenate([op(p, OP_BH), op(p, OP_KH)], axis=0).T
            upd_rhs = jnp.concatenate([u[p], op(p, OP_V)], axis=0)
            s_ref[p] = s_ref[p] * decay_rows + jnp.where(blockdiag, _dot1(upd_lhs, upd_rhs), 0.0)

    y = jnp.concatenate([y_ref[p] for p in range(N_PAIRS)], axis=1)
    out_ref[...] = _rwkv_post(y, r, k2, v, g, rk_ref[...], gng_ref[...], gnb_ref[...], ones, passes=1)

    @pl.when(c == pl.num_programs(0) - 1)
    def _():
        state_ref[...] = s_ref[...]


def _prompt_rwkv(h_main, tail, prm):
    n_chunks = SEQ // CHUNK
    half = D_RKV // 2
    assert D_QKV == half
    vec = pl.BlockSpec((1, D_RWKV), lambda c: (0, 0))
    full = lambda a: pl.BlockSpec(a.shape, lambda c: (0,) * a.ndim)
    return pl.pallas_call(
        _prompt_rwkv_kernel,
        out_shape=(jax.ShapeDtypeStruct((SEQ, D_RWKV), F32),
                   jax.ShapeDtypeStruct((N_PAIRS, LANES, LANES), F32)),
        grid=(n_chunks,),
        in_specs=[pl.BlockSpec((CHUNK, half), lambda c: (c, 1)),
                  pl.BlockSpec((CHUNK, half), lambda c: (c, 2)),
                  pl.BlockSpec((CHUNK, D_TAIL), lambda c: (c, 0)),
                  full(prm["mu"]), full(prm["mu_tail"]),
                  vec, vec, vec, vec, vec, vec, vec,
                  full(prm["wd"]), full(prm["wa"]), full(prm["wg"])],
        out_specs=(pl.BlockSpec((CHUNK, D_RWKV), lambda c: (c, 0)),
                   pl.BlockSpec((N_PAIRS, LANES, LANES), lambda c: (0, 0, 0))),
        scratch_shapes=[pltpu.VMEM((SUBLANES, D_RKV), F32),
                        pltpu.VMEM((SUBLANES, D_TAIL), F32),
                        pltpu.VMEM((N_PAIRS, LANES, LANES), F32),
                        pltpu.VMEM((N_PAIRS, N_OPS, CHUNK, LANES), F32),
                        pltpu.VMEM((N_PAIRS, SUBLANES, LANES), F32),
                        pltpu.VMEM((N_PAIRS, CHUNK, LANES), F32)],
        compiler_params=_cparams(("arbitrary",)),
        name="prompt_rwkv",
    )(h_main, h_main, tail, prm["mu"], prm["mu_tail"], prm["w0"], prm["a0"], prm["k_k"], prm["k_a"],
      prm["r_k"], prm["gn_g"], prm["gn_b"], prm["wd"], prm["wa"], prm["wg"])


def _sample_prep_kernel(h_ref, x_ref, wt_ref, shift_ref, mu_ref, mut_ref, w0_ref, a0_ref, kk_ref, ka_ref,
                        wd_ref, wa_ref, wg_ref, r_ref, k_ref, v_ref, g_ref, tail_ref,
                        rt_ref, wtr_ref, kt_ref, vt_ref, at_ref, bt_ref):
    ones = _head_ones()
    feat = h_ref[:, D_QKV:D_MAIN]
    tail = _dot1(x_ref[...], wt_ref[...], NT)
    tail_ref[...] = tail
    mixed = _token_mix(feat, shift_ref[:, 0:D_RKV], mu_ref[...])
    mixed_tail = _token_mix(tail, shift_ref[:, D_RKV:D_SHIFT], mut_ref[...])
    r, ld, k2, v, av, bv, g = _rwkv_prep(mixed, mixed_tail, w0_ref[...], a0_ref[...], kk_ref[...], ka_ref[...],
                                         wd_ref[...], wa_ref[...], wg_ref[...], ones)
    r_ref[...] = r
    k_ref[...] = k2
    v_ref[...] = v
    g_ref[...] = g
    rt_ref[...] = r.T
    wtr_ref[...] = jnp.exp(ld).T
    kt_ref[...] = k2.T
    vt_ref[...] = v.T
    at_ref[...] = av.T
    bt_ref[...] = bv.T


def _sample_prep(h_main, x, shift, prm):
    tok = jax.ShapeDtypeStruct((DEC_BATCH, D_RWKV), F32)
    chan = jax.ShapeDtypeStruct((D_RWKV, DEC_BATCH), F32)
    return pl.pallas_call(
        _sample_prep_kernel,
        out_shape=(tok,) * 4 + (jax.ShapeDtypeStruct((DEC_BATCH, D_TAIL), F32),) + (chan,) * 6,
        compiler_params=pltpu.CompilerParams(vmem_limit_bytes=VMEM_LIMIT),
        name="sample_rwkv_prep",
    )(h_main, x, prm["w_tail"], shift, prm["mu"], prm["mu_tail"], prm["w0"], prm["a0"], prm["k_k"], prm["k_a"],
      prm["wd"], prm["wa"], prm["wg"])


STEP_GROUP = 4


def _sample_step_kernel(s_ref, r_ref, w_ref, k_ref, a_ref, b_ref, v_ref, y_ref, snew_ref):
    r, w, k, a, b = r_ref[...], w_ref[...], k_ref[...], a_ref[...], b_ref[...]
    for g0 in range(0, HEAD_DIM, 2 * STEP_GROUP):
        chans = range(g0, g0 + 2 * STEP_GROUP)
        sa = {i: jnp.sum(s_ref[0, i] * a, axis=0, keepdims=True) for i in chans}
        s_new = {i: s_ref[0, i] * w + sa[i] * b + v_ref[i:i + 1, :] * k for i in chans}
        for i in chans:
            y_ref[i:i + 1, :] = jnp.sum(s_new[i] * r, axis=0, keepdims=True)
        for i in range(g0, g0 + 2 * STEP_GROUP, 2):
            pair = jnp.concatenate([s_new[i], s_new[i + 1]], axis=0)
            snew_ref[:, i * HEAD_DIM:(i + 2) * HEAD_DIM] = pair.T


def _sample_step(state_t, r_t, w_t, k_t, a_t, b_t, v_t):
    head_rows = pl.BlockSpec((HEAD_DIM, DEC_BATCH), lambda h: (h, 0))
    return pl.pallas_call(
        _sample_step_kernel,
        out_shape=(jax.ShapeDtypeStruct((D_RWKV, DEC_BATCH), F32),
                   jax.ShapeDtypeStruct((DEC_BATCH, N_RWKV_HEADS * HEAD_DIM * HEAD_DIM), F32)),
        grid=(N_RWKV_HEADS,),
        in_specs=[pl.BlockSpec((1, HEAD_DIM, HEAD_DIM, DEC_BATCH), lambda h: (h, 0, 0, 0))] + [head_rows] * 6,
        out_specs=(head_rows, pl.BlockSpec((DEC_BATCH, HEAD_DIM * HEAD_DIM), lambda h: (0, h))),
        compiler_params=_cparams(("arbitrary",)),
        name="sample_rwkv_step",
    )(state_t, r_t, w_t, k_t, a_t, b_t, v_t)


def _sample_post_kernel(yt_ref, r_ref, k_ref, v_ref, g_ref, rk_ref, gng_ref, gnb_ref, o_ref):
    o_ref[...] = _rwkv_post(yt_ref[...].T, r_ref[...], k_ref[...], v_ref[...], g_ref[...], rk_ref[...],
                            gng_ref[...], gnb_ref[...], _head_ones())


def _sample_post(y, r, k, v, g, prm):
    return pl.pallas_call(
        _sample_post_kernel,
        out_shape=jax.ShapeDtypeStruct((DEC_BATCH, D_RWKV), F32),
        compiler_params=pltpu.CompilerParams(vmem_limit_bytes=VMEM_LIMIT),
        name="sample_rwkv_post",
    )(y, r, k, v, g, prm["r_k"], prm["gn_g"], prm["gn_b"])


def _project_mix(attn_ref, rwkv_ref, wo_ref):
    return (_dot(attn_ref[...].astype(BF16), wo_ref[0:D_ATTN, :])
            + _dot(rwkv_ref[...].astype(BF16), wo_ref[D_ATTN:D_ATTN + D_RWKV, :]))


def _norm_and_route(mix, x_ref, g_ref, b_ref, wr_ref, br_ref, x1_ref, x1b_ref, route_ref):
    x1 = _layer_norm(ALPHA * x_ref[...] + mix, g_ref[...], b_ref[...])
    x1_ref[...] = x1
    x1b = x1.astype(BF16)
    x1b_ref[...] = _pack_bf16_halves(x1b)
    logits = _dot(x1b, wr_ref[...].astype(BF16)) + br_ref[...]
    tm = logits.shape[0]
    lane = lax.broadcasted_iota(jnp.int32, (tm, LANES), 1).astype(F32)
    big = float(2 * LANES)
    neg = -jnp.inf
    lc = jnp.where(lane < N_GROUPS, logits, neg)
    mc = jnp.max(lc, axis=-1, keepdims=True)
    g_sel = jnp.min(jnp.where(lc == mc, lane, big), axis=-1, keepdims=True)
    p_group = 1.0 / jnp.sum(jnp.exp(lc - mc), axis=-1, keepdims=True)
    lo = ROUTE_FINE_OFF + g_sel * EXPERTS_PER_GROUP
    lf = jnp.where((lane >= lo) & (lane < lo + EXPERTS_PER_GROUP), logits, neg)
    v1 = jnp.max(lf, axis=-1, keepdims=True)
    i1 = jnp.min(jnp.where(lf == v1, lane, big), axis=-1, keepdims=True)
    lf2 = jnp.where(lane == i1, neg, lf)
    v2 = jnp.max(lf2, axis=-1, keepdims=True)
    i2 = jnp.min(jnp.where(lf2 == v2, lane, big), axis=-1, keepdims=True)
    e21 = jnp.exp(v2 - v1)
    gate1 = p_group / (1.0 + e21)
    gate2 = p_group * e21 / (1.0 + e21)
    route = jnp.where(lane == 0, i1 - ROUTE_FINE_OFF,
                      jnp.where(lane == 1, i2 - ROUTE_FINE_OFF,
                                jnp.where(lane == 2, gate1, jnp.where(lane == 3, gate2, 0.0))))
    route_ref[...] = route


N_ROUTER_OUTS = 3


def _outproj_router_kernel(n_tiles, n_aliased, attn_ref, rwkv_ref, x_ref, wo_ref, g_ref, b_ref, wr_ref, br_ref,
                           *rest):
    outs = rest[n_aliased:n_aliased + N_ROUTER_OUTS]
    mix_ref = rest[-1]
    i = pl.program_id(0)
    finish = lambda mix: _norm_and_route(mix, x_ref, g_ref, b_ref, wr_ref, br_ref, *outs)

    @pl.when(i == 0)
    def _():
        mix_ref[...] = _project_mix(attn_ref, rwkv_ref, wo_ref)

    @pl.when((i >= 1) & (i < n_tiles))
    def _():
        finish(mix_ref[...])
        mix_ref[...] = _project_mix(attn_ref, rwkv_ref, wo_ref)

    @pl.when(i == n_tiles)
    def _():
        finish(mix_ref[...])

    @pl.when(i > n_tiles)
    def _():
        for out_ref in outs:
            out_ref[...] = jnp.zeros_like(out_ref)


def _outproj_router(attn, rwkv, x, wo_bf16, ln_g, ln_b, w_route, b_route, tm, n_total, row_block, into, name):
    m = x.shape[0]
    n_tiles = m // tm
    const = lambda shape: pl.BlockSpec(shape, lambda i: (0, 0))
    ahead = lambda width: pl.BlockSpec((tm, width), lambda i: (jnp.minimum(i, n_tiles - 1), 0))
    behind = lambda width: pl.BlockSpec((tm, width), lambda i: (jnp.clip(i - 1, 0, n_tiles - 1), 0))
    in_specs = [ahead(D_ATTN), ahead(D_RWKV), behind(D_MODEL),
                const((D_MODEL, D_MODEL)), const((1, D_MODEL)), const((1, D_MODEL)),
                const((D_MODEL, LANES)), const((1, LANES))]
    args = [attn, rwkv, x, wo_bf16, ln_g, ln_b, w_route, b_route]
    aliases, n_aliased, fill_steps = {}, 0, pl.cdiv(n_total - m, tm)
    if into is not None:
        n_aliased, fill_steps = N_ROUTER_OUTS, 0
        in_specs += [pl.BlockSpec(memory_space=pl.ANY)] * N_ROUTER_OUTS
        aliases = {len(args) + k: k for k in range(N_ROUTER_OUTS)}
        args += list(into)
    out_rows = lambda width: pl.BlockSpec((tm, width), lambda i: (jnp.maximum(i - 1, 0) + row_block, 0))
    return pl.pallas_call(
        functools.partial(_outproj_router_kernel, n_tiles, n_aliased),
        out_shape=(jax.ShapeDtypeStruct((n_total, D_MODEL), F32),
                   jax.ShapeDtypeStruct((n_total, D_MODEL // 2), jnp.uint32),
                   jax.ShapeDtypeStruct((n_total, LANES), F32)),
        grid=(n_tiles + 1 + fill_steps,),
        in_specs=in_specs,
        out_specs=(out_rows(D_MODEL), out_rows(D_MODEL // 2), out_rows(LANES)),
        scratch_shapes=[pltpu.VMEM((tm, D_MODEL), F32)],
        input_output_aliases=aliases,
        compiler_params=_cparams(("arbitrary",)),
        name=name,
    )(*args)


DISPATCH_TILE = 128


def _dispatch_kernel(zoff_ref, dest_ref, x_ref, o_hbm, zbuf, ring, zsem, sem):
    i = pl.program_id(0)
    n_blocks = o_hbm.shape[0] // MOE_BLOCK
    n_used = zoff_ref[N_EXPERTS]

    def zero_fill(start_row):
        start_row = pl.multiple_of(start_row, MOE_BLOCK)
        return pltpu.make_async_copy(zbuf, o_hbm.at[pl.ds(start_row, MOE_BLOCK)], zsem)

    def zero_fills(action):
        for e in range(N_EXPERTS):
            @pl.when(zoff_ref[e] >= 0)
            def _():
                action(zero_fill(zoff_ref[e]))
        for b in range(n_blocks):
            @pl.when(b >= n_used)
            def _():
                action(zero_fill(b * MOE_BLOCK))

    @pl.when(i == 0)
    def _():
        zbuf[...] = jnp.zeros_like(zbuf)
        zero_fills(lambda copy: copy.start())
        zero_fills(lambda copy: copy.wait())

    cur = lax.rem(i, 2)

    def wait_rows(slot):
        for k in range(2):
            pltpu.make_async_copy(ring.at[slot], o_hbm.at[pl.ds(0, DISPATCH_TILE)], sem.at[slot]).wait()

    @pl.when(i >= 2)
    def _():
        wait_rows(cur)

    ring[cur] = x_ref[...]
    for t in range(DISPATCH_TILE):
        for k in range(2):
            pltpu.make_async_copy(ring.at[cur, pl.ds(t, 1)], o_hbm.at[pl.ds(dest_ref[0, 0, 2 * t + k], 1)],
                                  sem.at[cur]).start()

    @pl.when(i == pl.num_programs(0) - 1)
    def _():
        wait_rows(cur)

        @pl.when(i >= 1)
        def _():
            wait_rows(1 - cur)


def _dispatch(zero_offsets, dest, x_packed, n_blocks):
    n_tokens, width = x_packed.shape
    grid_spec = pltpu.PrefetchScalarGridSpec(
        num_scalar_prefetch=1,
        grid=(n_tokens // DISPATCH_TILE,),
        in_specs=[pl.BlockSpec((1, 1, 2 * DISPATCH_TILE), lambda i, z: (i, 0, 0), memory_space=pltpu.SMEM),
                  pl.BlockSpec((DISPATCH_TILE, width), lambda i, z: (i, 0))],
        out_specs=pl.BlockSpec(memory_space=pl.ANY),
        scratch_shapes=[pltpu.VMEM((MOE_BLOCK, width), x_packed.dtype),
                        pltpu.VMEM((2, DISPATCH_TILE, width), x_packed.dtype),
                        pltpu.SemaphoreType.DMA, pltpu.SemaphoreType.DMA((2,))],
    )
    return pl.pallas_call(
        _dispatch_kernel,
        out_shape=jax.ShapeDtypeStruct((n_blocks * MOE_BLOCK, width), x_packed.dtype),
        grid_spec=grid_spec,
        compiler_params=_cparams(("arbitrary",)),
        name="moe_dispatch",
    )(zero_offsets, dest.reshape(-1, 1, 2 * DISPATCH_TILE), x_packed)


def _expert_kernel(be_ref, nb_ref, x_ref, wg_hbm, wu_hbm, wd_hbm, o_ref, wg_buf, wu_buf, wd_buf, slot_ref, sem):
    blk = pl.program_id(0)
    n_used = nb_ref[0]
    expert = be_ref[blk]
    is_first = (blk == 0) | (be_ref[jnp.maximum(blk - 1, 0)] != expert)

    def fetch(e, slot):
        return [pltpu.make_async_copy(hbm.at[e], buf.at[slot], sem.at[slot, i])
                for i, (hbm, buf) in enumerate(((wg_hbm, wg_buf), (wu_hbm, wu_buf), (wd_hbm, wd_buf)))]

    @pl.when((blk < n_used) & is_first)
    def _():
        @pl.when(blk == 0)
        def _():
            slot_ref[0] = 1
            for copy in fetch(expert, 0):
                copy.start()

        slot = 1 - slot_ref[0]
        slot_ref[0] = slot
        for copy in fetch(expert, slot):
            copy.wait()
        nxt = lax.while_loop(lambda j: (j < n_used) & (be_ref[jnp.minimum(j, n_used - 1)] == expert),
                             lambda j: j + 1, blk + 1)

        @pl.when(nxt < n_used)
        def _():
            for copy in fetch(be_ref[jnp.minimum(nxt, n_used - 1)], 1 - slot):
                copy.start()

    @pl.when(blk < n_used)
    def _():
        slot = slot_ref[0]
        half = D_MODEL // 2
        x_head, x_tail = _unpack_bf16_halves(x_ref[...])
        proj = lambda w_buf: (_dot(x_head, w_buf[slot, 0:half, :].astype(BF16))
                              + _dot(x_tail, w_buf[slot, half:D_MODEL, :].astype(BF16)))
        gate = proj(wg_buf)
        up = proj(wu_buf)
        h = gate * _sigmoid(gate) * up
        o_ref[...] = _dot(h.astype(BF16), wd_buf[slot].astype(BF16))

    @pl.when(blk >= n_used)
    def _():
        o_ref[...] = jnp.zeros_like(o_ref)


def _expert_mlp(block_expert, n_used, x_sorted, w_gate, w_up, w_down, n_blocks):
    grid_spec = pltpu.PrefetchScalarGridSpec(
        num_scalar_prefetch=2,
        grid=(n_blocks,),
        in_specs=[pl.BlockSpec((MOE_BLOCK, D_MODEL // 2), lambda b, be, nb: (jnp.minimum(b, nb[0] - 1), 0)),
                  pl.BlockSpec(memory_space=pl.ANY), pl.BlockSpec(memory_space=pl.ANY),
                  pl.BlockSpec(memory_space=pl.ANY)],
        out_specs=pl.BlockSpec((MOE_BLOCK, D_MODEL), lambda b, be, nb: (b, 0)),
        scratch_shapes=[pltpu.VMEM((2, D_MODEL, D_EXPERT), F32), pltpu.VMEM((2, D_MODEL, D_EXPERT), F32),
                        pltpu.VMEM((2, D_EXPERT, D_MODEL), F32), pltpu.SMEM((1,), jnp.int32),
                        pltpu.SemaphoreType.DMA((2, 3))],
    )
    return pl.pallas_call(
        _expert_kernel,
        out_shape=jax.ShapeDtypeStruct((n_blocks * MOE_BLOCK, D_MODEL), F32),
        grid_spec=grid_spec,
        compiler_params=_cparams(("arbitrary",)),
        name="expert_mlp",
    )(block_expert, n_used, x_sorted, w_gate, w_up, w_down)


COMBINE_TILE = 256


def _combine_kernel(dest_ref, dest_next_ref, y_hbm, x1_ref, route_ref, g_ref, b_ref, o_ref, ybuf, sem):
    i = pl.program_id(0)
    cur = lax.rem(i, 2)
    tile = x1_ref.shape[0]
    n_rows = 2 * tile

    def gather(table_ref, buf):
        for slot in range(n_rows):
            pltpu.make_async_copy(y_hbm.at[pl.ds(table_ref[0, 0, slot], 1)], ybuf.at[buf, pl.ds(slot, 1)],
                                  sem.at[buf]).start()

    def wait_gather(buf):
        pltpu.make_async_copy(y_hbm.at[pl.ds(0, n_rows)], ybuf.at[buf], sem.at[buf]).wait()

    @pl.when(i == 0)
    def _():
        gather(dest_ref, 0)

    gather(dest_next_ref, 1 - cur)
    wait_gather(cur)
    route = route_ref[...]
    yb = ybuf[cur]
    moe = route[:, 2:3] * yb[0:tile, :] + route[:, 3:4] * yb[tile:n_rows, :]
    o_ref[...] = _layer_norm(ALPHA * x1_ref[...] + moe, g_ref[...], b_ref[...])

    @pl.when(i == pl.num_programs(0) - 1)
    def _():
        wait_gather(1 - cur)


def _combine(dest, y_slots, x1_all, route_all, m, tm, row_block, ln_g, ln_b, name):
    return pl.pallas_call(
        _combine_kernel,
        out_shape=jax.ShapeDtypeStruct((m, D_MODEL), F32),
        grid=(m // tm,),
        in_specs=[pl.BlockSpec((1, 1, 2 * tm), lambda i: (i, 0, 0), memory_space=pltpu.SMEM),
                  pl.BlockSpec((1, 1, 2 * tm), lambda i: (i + 1, 0, 0), memory_space=pltpu.SMEM),
                  pl.BlockSpec(memory_space=pl.ANY),
                  pl.BlockSpec((tm, D_MODEL), lambda i: (i + row_block, 0)),
                  pl.BlockSpec((tm, LANES), lambda i: (i + row_block, 0)),
                  pl.BlockSpec((1, D_MODEL), lambda i: (0, 0)),
                  pl.BlockSpec((1, D_MODEL), lambda i: (0, 0))],
        out_specs=pl.BlockSpec((tm, D_MODEL), lambda i: (i, 0)),
        scratch_shapes=[pltpu.VMEM((2, 2 * tm, D_MODEL), F32), pltpu.SemaphoreType.DMA((2,))],
        compiler_params=_cparams(("arbitrary",)),
        name=name,
    )(dest, dest, y_slots, x1_all, route_all, ln_g, ln_b)


def _dispatch_plan(route_all, n_blocks):
    flat_e = route_all[:, 0:2].astype(jnp.int32).reshape(-1)
    onehot = (flat_e[:, None] == jnp.arange(N_EXPERTS, dtype=jnp.int32)[None, :]).astype(jnp.int32)
    csum = jnp.cumsum(onehot, axis=0)
    rank = jnp.sum(onehot * csum, axis=1) - 1
    counts = csum[-1]
    padded = (counts + MOE_BLOCK - 1) // MOE_BLOCK * MOE_BLOCK
    pend = jnp.cumsum(padded)
    pstart = pend - padded
    dest = (pstart[flat_e] + rank).astype(jnp.int32)
    zero_offsets = jnp.where(counts > 0, pend - MOE_BLOCK, -1).astype(jnp.int32)
    n_used = (pend[-1] // MOE_BLOCK).astype(jnp.int32)
    block_start = jnp.minimum(jnp.arange(n_blocks, dtype=jnp.int32), n_used - 1) * MOE_BLOCK
    block_e = jnp.minimum(jnp.searchsorted(pend, block_start, side="right"), N_EXPERTS - 1).astype(jnp.int32)
    return dest, jnp.concatenate([zero_offsets, n_used.reshape(1)]), block_e, n_used.reshape(1)


def kernel(x_prompt, x_sample, cache_k_win, cache_v_win, state_wkv, state_shift, w_in, attn_sinks, shift_mu, w0,
           w_decay_up, a0, w_a_up, w_g_up, k_k, k_a, r_k, gn_g, gn_b, w_out, ln1_g, ln1_b, w_coarse, b_coarse,
           w_fine, b_fine, w_exp_gate, w_exp_up, w_exp_down, ln2_g, ln2_b):
    xp = x_prompt[0]
    xs = x_sample[:, 0]
    row = lambda a: a.reshape(1, -1)

    w_in_t = jnp.swapaxes(w_in[0], 0, 1)
    prm = dict(mu=row(shift_mu[0, :D_RKV]), mu_tail=row(shift_mu[0, D_RKV:]), w_tail=w_in_t[D_MAIN:].astype(BF16),
               w0=row(w0[0]), a0=row(a0[0]), k_k=row(k_k[0]), k_a=row(k_a[0]),
               r_k=row(r_k[0]), gn_g=row(gn_g[0]), gn_b=row(gn_b[0]),
               wd=w_decay_up[0], wa=w_a_up[0], wg=w_g_up[0])
    sinks = attn_sinks[0]
    wo_bf16 = w_out[0].astype(BF16)
    w_route = jnp.pad(jnp.concatenate([w_coarse[0], w_fine[0]], axis=1), ((0, 0), (0, LANES - N_GROUPS - N_EXPERTS)))
    b_route = jnp.pad(jnp.concatenate([b_coarse[0], b_fine[0]]), (0, LANES - N_GROUPS - N_EXPERTS)).reshape(1, LANES)

    hp, tail_p = _matmul_with_tail(xp, w_in_t, D_MAIN, prm["w_tail"], MAIN_TM, MAIN_TN, "in_proj_prompt")
    hs = _matmul(xs, w_in_t, D_MAIN, DEC_BATCH, MAIN_TN, "in_proj_sample")

    attn_p = _prompt_attention(hp, sinks)
    rwkv_p, state_p = _prompt_rwkv(hp, tail_p, prm)

    q_s = hs[:, :D_ATTN].reshape(DEC_BATCH, N_Q_HEADS, HEAD_DIM)
    k_s = hs[:, D_ATTN:D_ATTN + D_KV].reshape(DEC_BATCH, 1, D_KV)
    v_s = hs[:, D_ATTN + D_KV:D_QKV].reshape(DEC_BATCH, 1, D_KV)
    window_t = lambda c: jnp.transpose(c, (0, 2, 3, 1)).reshape(DEC_BATCH, D_KV, WINDOW)
    attn_s, kwin_s, vwin_s = _sample_attention(
        q_s, k_s, v_s, window_t(cache_k_win[0]), window_t(cache_v_win[0]), sinks.reshape(N_Q_HEADS, 1))
    r_s, k2_s, vv_s, g_s, tail_s, r_t, w_t, k_t, v_t, a_t, b_t = _sample_prep(hs, xs, state_shift[0], prm)
    y_t, state_s = _sample_step(jnp.transpose(state_wkv[0], (1, 2, 3, 0)), r_t, w_t, k_t, a_t, b_t, v_t)
    state_s = state_s.reshape(DEC_BATCH, N_RWKV_HEADS, HEAD_DIM, HEAD_DIM)
    rwkv_s = _sample_post(y_t, r_s, k2_s, vv_s, g_s, prm)

    n_tokens = SEQ + DEC_BATCH
    outs_pr = _outproj_router(attn_p, rwkv_p, xp, wo_bf16, row(ln1_g[0]), row(ln1_b[0]), w_route, b_route,
                              OUTPROJ_TM, n_tokens, 0, None, "outproj_router_prompt")
    x1_all, x1b_all, route_all = _outproj_router(attn_s.reshape(DEC_BATCH, D_ATTN), rwkv_s, xs,
                                                 wo_bf16, row(ln1_g[0]), row(ln1_b[0]), w_route, b_route,
                                                 DEC_BATCH, n_tokens, SEQ // DEC_BATCH, outs_pr,
                                                 "outproj_router_sample")

    n_assign = 2 * n_tokens
    n_blocks = -(-(n_assign + N_EXPERTS * (MOE_BLOCK - 1)) // MOE_BLOCK)
    dest, zero_offsets, block_e, n_used = _dispatch_plan(route_all, n_blocks)
    x_sorted = _dispatch(zero_offsets, dest, x1b_all, n_blocks)
    y_slots = _expert_mlp(block_e, n_used, x_sorted, w_exp_gate[0], w_exp_up[0], w_exp_down[0], n_blocks)

    def dest_tiles(d, tile):
        d = d.reshape(-1, tile, 2)
        d = jnp.concatenate([d[:, :, 0], d[:, :, 1]], axis=1)
        return jnp.pad(d, ((0, 1), (0, 0))).reshape(-1, 1, 2 * tile)

    y_p = _combine(dest_tiles(dest[:2 * SEQ], COMBINE_TILE), y_slots, x1_all, route_all, SEQ, COMBINE_TILE, 0,
                   row(ln2_g[0]), row(ln2_b[0]), "combine_prompt")
    y_s = _combine(dest_tiles(dest[2 * SEQ:], DEC_BATCH), y_slots, x1_all, route_all, DEC_BATCH, DEC_BATCH,
                   SEQ // DEC_BATCH, row(ln2_g[0]), row(ln2_b[0]), "combine_sample")

    kv4 = lambda a: a.reshape(a.shape[0], N_KV_HEADS, HEAD_DIM)
    k_win_p = kv4(hp[SEQ - WINDOW:, D_ATTN:D_ATTN + D_KV])[None, None]
    v_win_p = kv4(hp[SEQ - WINDOW:, D_ATTN + D_KV:D_QKV])[None, None]
    sp = state_p.reshape(N_PAIRS, HEADS_PER_TILE, HEAD_DIM, HEADS_PER_TILE, HEAD_DIM)
    wkv_p = jnp.stack([sp[:, i, :, i, :] for i in range(HEADS_PER_TILE)], axis=1)
    wkv_p = wkv_p.reshape(N_RWKV_HEADS, HEAD_DIM, HEAD_DIM).transpose(0, 2, 1)[None, None]
    shift_p = jnp.concatenate([hp[SEQ - 1:SEQ, D_QKV:], tail_p[SEQ - 1:SEQ]], axis=1)[None]
    shift_s = jnp.concatenate([hs[:, D_QKV:], tail_s], axis=1)[None]
    return (y_p[None], y_s[:, None, :], k_win_p, v_win_p, wkv_p, shift_p,
            kwin_s.reshape(1, DEC_BATCH, WINDOW, N_KV_HEADS, HEAD_DIM),
            vwin_s.reshape(1, DEC_BATCH, WINDOW, N_KV_HEADS, HEAD_DIM),
            state_s[None], shift_s)
```

```python
import functools
import math

import jax
import jax.numpy as jnp
from jax import lax
from jax.experimental import pallas as pl
from jax.experimental.pallas import tpu as pltpu

F32 = jnp.float32
BF16 = jnp.bfloat16

D_MODEL = 2048
SEQ = 8192
DEC_BATCH = 128
HEAD_DIM = 64
D_ATTN = 1024
D_RWKV = 1024
N_Q_HEADS = 16
N_KV_HEADS = 4
Q_PER_KV = 4
D_KV = 256
WINDOW = 128
ATTN_SCALE = HEAD_DIM ** -0.5
N_RWKV_HEADS = 16
W_LORA = 64
A_LORA = 64
G_LORA = 160
D_SHIFT = 3 * D_RWKV + W_LORA + A_LORA + G_LORA
D_QKV = D_ATTN + 2 * D_KV
N_GROUPS = 4
EXPERTS_PER_GROUP = 8
N_EXPERTS = 32
D_EXPERT = 512
ALPHA = 2.0 ** 0.25
LN_EPS = 1e-5
GN_EPS = 64e-5

SUBLANES = 8
LANES = 128
VMEM_LIMIT = 52 * 1024 * 1024

D_RKV = 3 * D_RWKV
D_TAIL = W_LORA + A_LORA + G_LORA
D_MAIN = D_QKV + D_RKV
MAIN_TN = 1536
MAIN_TM = 512
OUTPROJ_TM = 256

CHUNK = 64
HEADS_PER_TILE = LANES // HEAD_DIM
N_PAIRS = N_RWKV_HEADS // HEADS_PER_TILE
SOLVE_LEVELS = int(math.log2(CHUNK))
PAIR_GROUP = 8

MOE_BLOCK = 256
ROUTE_FINE_OFF = N_GROUPS

NN = (((1,), (0,)), ((), ()))
NT = (((1,), (1,)), ((), ()))


def _dot(a, b, dims=NN):
    return lax.dot_general(a, b, dims, preferred_element_type=F32)


def _dot1(a, b, dims=NN):
    return _dot(a.astype(BF16), b.astype(BF16), dims)


def _split(x):
    hi = x.astype(BF16)
    lo = (x - hi.astype(F32)).astype(BF16)
    return hi, lo


def _dot_exact_lhs(a_bf16, b, dims=NN):
    bh, bl = _split(b)
    return _dot(a_bf16, bh, dims) + _dot(a_bf16, bl, dims)


def _dot_exact_rhs(a, b_bf16, dims=NN):
    ah, al = _split(a)
    return _dot(ah, b_bf16, dims) + _dot(al, b_bf16, dims)


def _div_pow2(x, d):
    return lax.shift_right_logical(x, jnp.int32(int(math.log2(d))))


def _mod_pow2(x, d):
    return lax.bitwise_and(x, jnp.int32(d - 1))


def _pack_bf16_halves(x_bf16):
    n = x_bf16.shape[1] // 2
    bits = lax.bitcast_convert_type(x_bf16.astype(F32), jnp.uint32)
    return lax.bitwise_or(bits[:, 0:n], lax.shift_right_logical(bits[:, n:2 * n], jnp.uint32(16)))


def _unpack_bf16_halves(packed):
    hi = lax.bitcast_convert_type(lax.bitwise_and(packed, jnp.uint32(0xFFFF0000)), F32)
    lo = lax.bitcast_convert_type(lax.shift_left(packed, jnp.uint32(16)), F32)
    return hi.astype(BF16), lo.astype(BF16)


def _sigmoid(x):
    return 1.0 / (1.0 + jnp.exp(-x))


def _softplus(x):
    return jnp.maximum(x, 0.0) + jnp.log(1.0 + jnp.exp(-jnp.abs(x)))


def _layer_norm(z, g, b):
    mu = jnp.mean(z, axis=-1, keepdims=True)
    d = z - mu
    var = jnp.mean(d * d, axis=-1, keepdims=True)
    return d * lax.rsqrt(var + LN_EPS) * g + b


def _cparams(sem):
    return pltpu.CompilerParams(dimension_semantics=sem, vmem_limit_bytes=VMEM_LIMIT)


def _matmul_kernel(x_ref, wt_ref, o_ref):
    o_ref[...] = _dot(x_ref[...].astype(BF16), wt_ref[...].astype(BF16), NT)


def _matmul(x, w_t, n_out, tm, tn, name):
    m, k = x.shape
    tm = min(tm, m)
    return pl.pallas_call(
        _matmul_kernel,
        out_shape=jax.ShapeDtypeStruct((m, n_out), F32),
        grid=(n_out // tn, m // tm),
        in_specs=[pl.BlockSpec((tm, k), lambda j, i: (i, 0)),
                  pl.BlockSpec((tn, k), lambda j, i: (j, 0))],
        out_specs=pl.BlockSpec((tm, tn), lambda j, i: (i, j)),
        compiler_params=_cparams(("arbitrary", "arbitrary")),
        name=name,
    )(x, w_t)


def _matmul_with_tail_kernel(x_ref, wt_ref, wtail_ref, o_ref, tail_ref):
    xb = x_ref[...].astype(BF16)
    o_ref[...] = _dot(xb, wt_ref[...].astype(BF16), NT)

    @pl.when(pl.program_id(0) == 0)
    def _():
        tail_ref[...] = _dot(xb, wtail_ref[...], NT)

    @pl.when(pl.program_id(0) > 0)
    def _():
        tail_ref[...] = jnp.zeros_like(tail_ref)


def _matmul_with_tail(x, w_t, n_out, wtail_t, tm, tn, name):
    m, k = x.shape
    n_rows = m // tm
    n_tail = wtail_t.shape[0]
    return pl.pallas_call(
        _matmul_with_tail_kernel,
        out_shape=(jax.ShapeDtypeStruct((m, n_out), F32), jax.ShapeDtypeStruct((m + tm, n_tail), F32)),
        grid=(n_out // tn, n_rows),
        in_specs=[pl.BlockSpec((tm, k), lambda j, i: (i, 0)),
                  pl.BlockSpec((tn, k), lambda j, i: (j, 0)),
                  pl.BlockSpec((n_tail, k), lambda j, i: (0, 0))],
        out_specs=(pl.BlockSpec((tm, tn), lambda j, i: (i, j)),
                   pl.BlockSpec((tm, n_tail), lambda j, i: (jnp.where(j == 0, i, n_rows), 0))),
        compiler_params=_cparams(("arbitrary", "arbitrary")),
        name=name,
    )(x, w_t, wtail_t)


def _band_bias():
    qi = jnp.arange(Q_PER_KV * WINDOW)[:, None] % WINDOW
    kj = jnp.arange(2 * WINDOW)[None, :]
    diff = qi + WINDOW - kj
    band = (diff >= 0) & (diff <= WINDOW)
    keep = jnp.stack([band & (kj >= WINDOW), band])
    return jnp.where(keep, 0.0, -jnp.inf).astype(F32)


def _prompt_attn_kernel(q_ref, kvp_ref, kvc_ref, bias_ref, sink_ref, o_ref):
    q = q_ref[...]
    kv_prev = kvp_ref[...]
    kv_cur = kvc_ref[...]
    bias = bias_ref[0]
    row_head = _div_pow2(lax.broadcasted_iota(jnp.int32, (Q_PER_KV * WINDOW, 1), 0), WINDOW)
    groups = range(N_KV_HEADS)
    kv_cols = lambda off, g: jnp.concatenate([kv_prev[:, off + g * HEAD_DIM:off + (g + 1) * HEAD_DIM],
                                              kv_cur[:, off + g * HEAD_DIM:off + (g + 1) * HEAD_DIM]],
                                             axis=0).astype(BF16)
    q_rows = lambda g: jnp.concatenate(
        [q[:, (g * Q_PER_KV + h) * HEAD_DIM:(g * Q_PER_KV + h + 1) * HEAD_DIM] for h in range(Q_PER_KV)],
        axis=0).astype(BF16)
    s = [_dot(q_rows(g), kv_cols(0, g), NT) * ATTN_SCALE + bias for g in groups]
    sink = []
    for g in groups:
        col = jnp.zeros((Q_PER_KV * WINDOW, 1), F32)
        for h in range(Q_PER_KV):
            col = jnp.where(row_head == h, sink_ref[g * Q_PER_KV + h], col)
        sink.append(col)
    m = [jnp.maximum(jnp.max(s[g], axis=-1, keepdims=True), sink[g]) for g in groups]
    p = [jnp.exp(s[g] - m[g]) for g in groups]
    denom = [jnp.sum(p[g], axis=-1, keepdims=True) + jnp.exp(sink[g] - m[g]) for g in groups]
    o = [_dot((p[g] / denom[g]).astype(BF16), kv_cols(D_KV, g)) for g in groups]
    o_ref[...] = jnp.concatenate([o[g][h * WINDOW:(h + 1) * WINDOW, :] for g in groups for h in range(Q_PER_KV)],
                                 axis=1)


def _prompt_attention(h_attn, sinks):
    nb = SEQ // WINDOW
    return pl.pallas_call(
        _prompt_attn_kernel,
        out_shape=jax.ShapeDtypeStruct((SEQ, D_ATTN), F32),
        grid=(nb,),
        in_specs=[pl.BlockSpec((WINDOW, D_ATTN), lambda i: (i, 0)),
                  pl.BlockSpec((WINDOW, 2 * D_KV), lambda i: (jnp.maximum(i - 1, 0), 2)),
                  pl.BlockSpec((WINDOW, 2 * D_KV), lambda i: (i, 2)),
                  pl.BlockSpec((1, Q_PER_KV * WINDOW, 2 * WINDOW), lambda i: (jnp.minimum(i, 1), 0, 0)),
                  pl.BlockSpec(memory_space=pltpu.SMEM)],
        out_specs=pl.BlockSpec((WINDOW, D_ATTN), lambda i: (i, 0)),
        compiler_params=_cparams(("arbitrary",)),
        name="prompt_attention",
    )(h_attn, h_attn, h_attn, _band_bias(), sinks)


SAMPLE_ATTN_TILE = 8


def _sample_attn_kernel(q_ref, knew_ref, vnew_ref, ck_ref, cv_ref, sink_ref, o_ref, kwin_ref, vwin_ref):
    lane = lax.broadcasted_iota(jnp.int32, (N_Q_HEADS, D_KV), 1)
    head = lax.broadcasted_iota(jnp.int32, (N_Q_HEADS, D_KV), 0)
    group_mask = _div_pow2(lane, HEAD_DIM) == _div_pow2(head, Q_PER_KV)
    sink = sink_ref[...]
    row = lax.broadcasted_iota(jnp.int32, (WINDOW, D_KV), 0)
    seqs = range(SAMPLE_ATTN_TILE)
    qbd = [jnp.where(group_mask, jnp.concatenate([q_ref[b]] * N_KV_HEADS, axis=1), 0.0).astype(BF16) for b in seqs]
    s = [_dot1(qbd[b], ck_ref[b]) * ATTN_SCALE for b in seqs]
    s_new = [jnp.sum(qbd[b].astype(F32) * knew_ref[b].astype(BF16).astype(F32), axis=-1, keepdims=True) * ATTN_SCALE
             for b in seqs]
    m = [jnp.maximum(jnp.maximum(jnp.max(s[b], axis=-1, keepdims=True), s_new[b]), sink) for b in seqs]
    p = [jnp.exp(s[b] - m[b]) for b in seqs]
    p_new = [jnp.exp(s_new[b] - m[b]) for b in seqs]
    denom = [jnp.sum(p[b], axis=-1, keepdims=True) + p_new[b] + jnp.exp(sink - m[b]) for b in seqs]
    for b in seqs:
        kb = ck_ref[b].T
        vb = cv_ref[b].T
        kn = knew_ref[b]
        vn = vnew_ref[b]
        o_full = (_dot1(p[b] / denom[b], vb)
                  + (p_new[b] / denom[b]).astype(BF16).astype(F32) * vn.astype(BF16).astype(F32))
        o_full = jnp.where(group_mask, o_full, 0.0)
        o = o_full[:, 0:HEAD_DIM]
        for g in range(1, N_KV_HEADS):
            o = o + o_full[:, g * HEAD_DIM:(g + 1) * HEAD_DIM]
        o_ref[b] = o
        kwin_ref[b] = jnp.where(row == WINDOW - 1, kn, pltpu.roll(kb, WINDOW - 1, axis=0))
        vwin_ref[b] = jnp.where(row == WINDOW - 1, vn, pltpu.roll(vb, WINDOW - 1, axis=0))


def _sample_attention(q, k_new, v_new, cache_k_t, cache_v_t, sinks):
    bt = SAMPLE_ATTN_TILE
    win_spec = pl.BlockSpec((bt, WINDOW, D_KV), lambda i: (i, 0, 0))
    win_t_spec = pl.BlockSpec((bt, D_KV, WINDOW), lambda i: (i, 0, 0))
    new_spec = pl.BlockSpec((bt, 1, D_KV), lambda i: (i, 0, 0))
    return pl.pallas_call(
        _sample_attn_kernel,
        out_shape=(jax.ShapeDtypeStruct((DEC_BATCH, N_Q_HEADS, HEAD_DIM), F32),
                   jax.ShapeDtypeStruct((DEC_BATCH, WINDOW, D_KV), F32),
                   jax.ShapeDtypeStruct((DEC_BATCH, WINDOW, D_KV), F32)),
        grid=(DEC_BATCH // bt,),
        in_specs=[pl.BlockSpec((bt, N_Q_HEADS, HEAD_DIM), lambda i: (i, 0, 0)),
                  new_spec, new_spec, win_t_spec, win_t_spec,
                  pl.BlockSpec((N_Q_HEADS, 1), lambda i: (0, 0))],
        out_specs=(pl.BlockSpec((bt, N_Q_HEADS, HEAD_DIM), lambda i: (i, 0, 0)), win_spec, win_spec),
        compiler_params=_cparams(("arbitrary",)),
        name="sample_attention",
    )(q, k_new, v_new, cache_k_t, cache_v_t, sinks)


def _head_ones():
    r = _div_pow2(lax.broadcasted_iota(jnp.int32, (LANES, LANES), 0), HEAD_DIM)
    c = _div_pow2(lax.broadcasted_iota(jnp.int32, (LANES, LANES), 1), HEAD_DIM)
    return jnp.where(r == c, 1.0, 0.0).astype(BF16)


def _head_sum(x, ones, passes=2):
    dot = _dot_exact_rhs if passes == 2 else _dot1
    parts = [dot(x[:, p * LANES:(p + 1) * LANES], ones) for p in range(x.shape[1] // LANES)]
    return jnp.concatenate(parts, axis=1)


def _token_mix(feat, shifted, mu):
    return feat + (shifted - feat) * mu


def _rwkv_prep(mixed, mixed_tail, w0, a0, k_k, k_a, wd, wa, wg, ones):
    r = mixed[:, 0:D_RWKV]
    k = mixed[:, D_RWKV:2 * D_RWKV]
    v = mixed[:, 2 * D_RWKV:3 * D_RWKV]
    xw = mixed_tail[:, 0:W_LORA]
    xa = mixed_tail[:, W_LORA:W_LORA + A_LORA]
    xg = mixed_tail[:, W_LORA + A_LORA:D_TAIL]
    w_log = -_softplus(-(w0 + _dot1(jnp.tanh(xw), wd))) - 0.5
    log_decay = -jnp.exp(w_log)
    a = _sigmoid(a0 + _dot1(xa, wa))
    g = _dot1(_sigmoid(xg), wg)
    kk = k * k_k
    kk = kk * lax.rsqrt(jnp.maximum(_head_sum(kk * kk, ones), 1e-24))
    k2 = k * (1.0 + (a - 1.0) * k_a)
    return r, log_decay, k2, v, -kk, kk * a, g


def _rwkv_post(y, r, k2, v, g, r_k, gn_g, gn_b, ones, passes=2):
    inv_n = 1.0 / HEAD_DIM
    mu = _head_sum(y, ones, passes) * inv_n
    d = y - mu
    var = _head_sum(d * d, ones, passes) * inv_n
    yn = d * lax.rsqrt(var + GN_EPS) * gn_g + gn_b
    bonus = _head_sum(r * k2 * r_k, ones, passes) * v
    return (yn + bonus) * g


(OP_AABS, OP_RABS, OP_AN, OP_RN, OP_BN, OP_KN, OP_BH, OP_KH, OP_V) = range(9)
N_OPS = 9


def _prompt_rwkv_kernel(f1_ref, f2_ref, tail_in_ref, mu_ref, mut_ref, w0_ref, a0_ref, kk_ref, ka_ref, rk_ref,
                        gng_ref, gnb_ref, wd_ref, wa_ref, wg_ref, out_ref, state_ref,
                        prev_ref, prevt_ref, s_ref, ops_ref, pc_ref, y_ref):
    c = pl.program_id(0)
    C = CHUNK

    @pl.when(c == 0)
    def _():
        prev_ref[...] = jnp.zeros_like(prev_ref)
        prevt_ref[...] = jnp.zeros_like(prevt_ref)
        s_ref[...] = jnp.zeros_like(s_ref)

    ones = _head_ones()
    row = lax.broadcasted_iota(jnp.int32, (C, 1), 0)

    def token_shift(feat, carry_ref):
        shifted = jnp.where(row == 0, carry_ref[0:1, :], pltpu.roll(feat, 1, axis=0))
        carry_ref[0:1, :] = feat[C - 1:C, :]
        return shifted

    feat = jnp.concatenate([f1_ref[...], f2_ref[...]], axis=1)
    tail = tail_in_ref[...]
    mixed = _token_mix(feat, token_shift(feat, prev_ref), mu_ref[...])
    mixed_tail = _token_mix(tail, token_shift(tail, prevt_ref), mut_ref[...])
    r, ld, k2, v, av, bv, g = _rwkv_prep(mixed, mixed_tail, w0_ref[...], a0_ref[...], kk_ref[...], ka_ref[...],
                                         wd_ref[...], wa_ref[...], wg_ref[...], ones)

    ti = lax.broadcasted_iota(jnp.int32, (C, C), 0)
    tj = lax.broadcasted_iota(jnp.int32, (C, C), 1)
    tri_incl = jnp.where(tj <= ti, 1.0, 0.0).astype(BF16)
    cs = _dot_exact_lhs(tri_incl, ld)
    cs_ref = cs[C // 2 - 1:C // 2, :]
    cs_end = cs[C - 1:C, :]
    e_prev = jnp.exp(cs - ld)
    e_cur = jnp.exp(cs)
    n_prev = jnp.exp(cs - ld - cs_ref)
    n_cur = jnp.exp(cs - cs_ref)
    n_inv = jnp.exp(cs_ref - cs)
    e_tail = jnp.exp(cs_end - cs)
    ops = {OP_AABS: av * e_prev, OP_RABS: r * e_cur, OP_AN: av * n_prev, OP_RN: r * n_cur,
           OP_BN: bv * n_inv, OP_KN: k2 * n_inv, OP_BH: bv * e_tail, OP_KH: k2 * e_tail, OP_V: v}
    p_end = jnp.exp(cs_end)
    for p in range(N_PAIRS):
        sl = slice(p * LANES, (p + 1) * LANES)
        for idx, val in ops.items():
            ops_ref[p, idx] = val[:, sl]
        pc_ref[p] = jnp.broadcast_to(p_end[:, sl], (SUBLANES, LANES))

    lane1 = lax.broadcasted_iota(jnp.int32, (C, LANES), 1)
    head0 = lane1 < HEAD_DIM
    r2 = lax.broadcasted_iota(jnp.int32, (2 * C, 2 * C), 0)
    c2 = lax.broadcasted_iota(jnp.int32, (2 * C, 2 * C), 1)
    tq = _mod_pow2(r2, C)
    tk = _mod_pow2(c2, C)
    band = (tk < tq) | ((tk == tq) & (r2 >= C))
    blockdiag = _div_pow2(r2, HEAD_DIM) == _div_pow2(c2, HEAD_DIM)

    op = lambda p, idx: ops_ref[p, idx]
    zero_half = jnp.zeros((C, LANES), F32)
    for pairs in [range(g, g + PAIR_GROUP) for g in range(0, N_PAIRS, PAIR_GROUP)]:
        gy = {p: _dot1(jnp.concatenate([op(p, OP_AABS), op(p, OP_RABS)], axis=0), s_ref[p]) for p in pairs}

        am0, am1 = {}, {}
        for p in pairs:
            a_n, r_n = op(p, OP_AN), op(p, OP_RN)
            b0, k0 = jnp.where(head0, op(p, OP_BN), 0.0), jnp.where(head0, op(p, OP_KN), 0.0)
            b1, k1 = jnp.where(head0, 0.0, op(p, OP_BN)), jnp.where(head0, 0.0, op(p, OP_KN))
            am = _dot1(jnp.concatenate([a_n, r_n], axis=0), jnp.concatenate([k0, b0, b1, k1], axis=0), NT)
            am0[p] = jnp.where(band, am[:, 0:2 * C], 0.0)
            am1[p] = jnp.where(band, am[:, 2 * C:4 * C], 0.0)

        w0, w1 = {}, {}
        for p in pairs:
            top0, top1 = am0[p][0:C], am1[p][0:C]
            ak = jnp.concatenate([jnp.where(head0, top0, 0.0), jnp.where(head0, 0.0, top1)], axis=0)
            vv = op(p, OP_V)
            g0 = gy[p][0:C]
            m = jnp.concatenate([g0, g0], axis=0) + _dot1(ak, jnp.concatenate([vv, vv], axis=0))
            w0[p] = jnp.where(head0, m[0:C], top0)
            w1[p] = jnp.where(head0, top1, m[C:2 * C])

        for lvl in range(SOLVE_LEVELS):
            prod0 = {p: _dot1(w0[p], jnp.concatenate([zero_half, w0[p]], axis=0)) for p in pairs}
            prod1 = {p: _dot1(w1[p], jnp.concatenate([w1[p], zero_half], axis=0)) for p in pairs}
            w0 = {p: jnp.where(head0, w0[p] + prod0[p], prod0[p]) for p in pairs}
            w1 = {p: jnp.where(head0, prod1[p], w1[p] + prod1[p]) for p in pairs}
        u = {p: jnp.where(head0, w0[p], w1[p]) for p in pairs}

        for p in pairs:
            vv = op(p, OP_V)
            y_lhs = jnp.concatenate([am0[p][C:2 * C], am1[p][C:2 * C]], axis=1)
            y_rhs = jnp.concatenate([jnp.where(head0, vv, 0.0), jnp.where(head0, u[p], 0.0),
                                     jnp.where(head0, 0.0, u[p]), jnp.where(head0, 0.0, vv)], axis=0)
            y_ref[p] = gy[p][C:2 * C] + _dot1(y_lhs, y_rhs)

        for p in pairs:
            decay_rows = jnp.broadcast_to(pc_ref[p][0:1, :], (LANES, LANES)).T
            upd_lhs = jnp.concatenate([op(p, OP_BH), op(p, OP_KH)], axis=0).T
            upd_rhs = jnp.concatenate([u[p], op(p, OP_V)], axis=0)
            s_ref[p] = s_ref[p] * decay_rows + jnp.where(blockdiag, _dot1(upd_lhs, upd_rhs), 0.0)

    y = jnp.concatenate([y_ref[p] for p in range(N_PAIRS)], axis=1)
    out_ref[...] = _rwkv_post(y, r, k2, v, g, rk_ref[...], gng_ref[...], gnb_ref[...], ones, passes=1)

    @pl.when(c == pl.num_programs(0) - 1)
    def _():
        state_ref[...] = s_ref[...]


def _prompt_rwkv(h_main, tail, prm):
    n_chunks = SEQ // CHUNK
    half = D_RKV // 2
    assert D_QKV == half
    vec = pl.BlockSpec((1, D_RWKV), lambda c: (0, 0))
    full = lambda a: pl.BlockSpec(a.shape, lambda c: (0,) * a.ndim)
    return pl.pallas_call(
        _prompt_rwkv_kernel,
        out_shape=(jax.ShapeDtypeStruct((SEQ, D_RWKV), F32),
                   jax.ShapeDtypeStruct((N_PAIRS, LANES, LANES), F32)),
        grid=(n_chunks,),
        in_specs=[pl.BlockSpec((CHUNK, half), lambda c: (c, 1)),
                  pl.BlockSpec((CHUNK, half), lambda c: (c, 2)),
                  pl.BlockSpec((CHUNK, D_TAIL), lambda c: (c, 0)),
                  full(prm["mu"]), full(prm["mu_tail"]),
                  vec, vec, vec, vec, vec, vec, vec,
                  full(prm["wd"]), full(prm["wa"]), full(prm["wg"])],
        out_specs=(pl.BlockSpec((CHUNK, D_RWKV), lambda c: (c, 0)),
                   pl.BlockSpec((N_PAIRS, LANES, LANES), lambda c: (0, 0, 0))),
        scratch_shapes=[pltpu.VMEM((SUBLANES, D_RKV), F32),
                        pltpu.VMEM((SUBLANES, D_TAIL), F32),
                        pltpu.VMEM((N_PAIRS, LANES, LANES), F32),
                        pltpu.VMEM((N_PAIRS, N_OPS, CHUNK, LANES), F32),
                        pltpu.VMEM((N_PAIRS, SUBLANES, LANES), F32),
                        pltpu.VMEM((N_PAIRS, CHUNK, LANES), F32)],
        compiler_params=_cparams(("arbitrary",)),
        name="prompt_rwkv",
    )(h_main, h_main, tail, prm["mu"], prm["mu_tail"], prm["w0"], prm["a0"], prm["k_k"], prm["k_a"],
      prm["r_k"], prm["gn_g"], prm["gn_b"], prm["wd"], prm["wa"], prm["wg"])


def _sample_prep_kernel(h_ref, x_ref, wt_ref, shift_ref, mu_ref, mut_ref, w0_ref, a0_ref, kk_ref, ka_ref,
                        wd_ref, wa_ref, wg_ref, r_ref, k_ref, v_ref, g_ref, tail_ref,
                        rt_ref, wtr_ref, kt_ref, vt_ref, at_ref, bt_ref):
    ones = _head_ones()
    feat = h_ref[:, D_QKV:D_MAIN]
    tail = _dot1(x_ref[...], wt_ref[...], NT)
    tail_ref[...] = tail
    mixed = _token_mix(feat, shift_ref[:, 0:D_RKV], mu_ref[...])
    mixed_tail = _token_mix(tail, shift_ref[:, D_RKV:D_SHIFT], mut_ref[...])
    r, ld, k2, v, av, bv, g = _rwkv_prep(mixed, mixed_tail, w0_ref[...], a0_ref[...], kk_ref[...], ka_ref[...],
                                         wd_ref[...], wa_ref[...], wg_ref[...], ones)
    r_ref[...] = r
    k_ref[...] = k2
    v_ref[...] = v
    g_ref[...] = g
    rt_ref[...] = r.T
    wtr_ref[...] = jnp.exp(ld).T
    kt_ref[...] = k2.T
    vt_ref[...] = v.T
    at_ref[...] = av.T
    bt_ref[...] = bv.T


def _sample_prep(h_main, x, shift, prm):
    tok = jax.ShapeDtypeStruct((DEC_BATCH, D_RWKV), F32)
    chan = jax.ShapeDtypeStruct((D_RWKV, DEC_BATCH), F32)
    return pl.pallas_call(
        _sample_prep_kernel,
        out_shape=(tok,) * 4 + (jax.ShapeDtypeStruct((DEC_BATCH, D_TAIL), F32),) + (chan,) * 6,
        compiler_params=pltpu.CompilerParams(vmem_limit_bytes=VMEM_LIMIT),
        name="sample_rwkv_prep",
    )(h_main, x, prm["w_tail"], shift, prm["mu"], prm["mu_tail"], prm["w0"], prm["a0"], prm["k_k"], prm["k_a"],
      prm["wd"], prm["wa"], prm["wg"])


STEP_GROUP = 4


def _sample_step_kernel(s_ref, r_ref, w_ref, k_ref, a_ref, b_ref, v_ref, y_ref, snew_ref):
    r, w, k, a, b = r_ref[...], w_ref[...], k_ref[...], a_ref[...], b_ref[...]
    for g0 in range(0, HEAD_DIM, 2 * STEP_GROUP):
        chans = range(g0, g0 + 2 * STEP_GROUP)
        sa = {i: jnp.sum(s_ref[0, i] * a, axis=0, keepdims=True) for i in chans}
        s_new = {i: s_ref[0, i] * w + sa[i] * b + v_ref[i:i + 1, :] * k for i in chans}
        for i in chans:
            y_ref[i:i + 1, :] = jnp.sum(s_new[i] * r, axis=0, keepdims=True)
        for i in range(g0, g0 + 2 * STEP_GROUP, 2):
            pair = jnp.concatenate([s_new[i], s_new[i + 1]], axis=0)
            snew_ref[:, i * HEAD_DIM:(i + 2) * HEAD_DIM] = pair.T


def _sample_step(state_t, r_t, w_t, k_t, a_t, b_t, v_t):
    head_rows = pl.BlockSpec((HEAD_DIM, DEC_BATCH), lambda h: (h, 0))
    return pl.pallas_call(
        _sample_step_kernel,
        out_shape=(jax.ShapeDtypeStruct((D_RWKV, DEC_BATCH), F32),
                   jax.ShapeDtypeStruct((DEC_BATCH, N_RWKV_HEADS * HEAD_DIM * HEAD_DIM), F32)),
        grid=(N_RWKV_HEADS,),
        in_specs=[pl.BlockSpec((1, HEAD_DIM, HEAD_DIM, DEC_BATCH), lambda h: (h, 0, 0, 0))] + [head_rows] * 6,
        out_specs=(head_rows, pl.BlockSpec((DEC_BATCH, HEAD_DIM * HEAD_DIM), lambda h: (0, h))),
        compiler_params=_cparams(("arbitrary",)),
        name="sample_rwkv_step",
    )(state_t, r_t, w_t, k_t, a_t, b_t, v_t)


def _sample_post_kernel(yt_ref, r_ref, k_ref, v_ref, g_ref, rk_ref, gng_ref, gnb_ref, o_ref):
    o_ref[...] = _rwkv_post(yt_ref[...].T, r_ref[...], k_ref[...], v_ref[...], g_ref[...], rk_ref[...],
                            gng_ref[...], gnb_ref[...], _head_ones())


def _sample_post(y, r, k, v, g, prm):
    return pl.pallas_call(
        _sample_post_kernel,
        out_shape=jax.ShapeDtypeStruct((DEC_BATCH, D_RWKV), F32),
        compiler_params=pltpu.CompilerParams(vmem_limit_bytes=VMEM_LIMIT),
        name="sample_rwkv_post",
    )(y, r, k, v, g, prm["r_k"], prm["gn_g"], prm["gn_b"])


def _project_mix(attn_ref, rwkv_ref, wo_ref):
    return (_dot(attn_ref[...].astype(BF16), wo_ref[0:D_ATTN, :])
            + _dot(rwkv_ref[...].astype(BF16), wo_ref[D_ATTN:D_ATTN + D_RWKV, :]))


def _norm_and_route(mix, x_ref, g_ref, b_ref, wr_ref, br_ref, x1_ref, x1b_ref, route_ref):
    x1 = _layer_norm(ALPHA * x_ref[...] + mix, g_ref[...], b_ref[...])
    x1_ref[...] = x1
    x1b = x1.astype(BF16)
    x1b_ref[...] = _pack_bf16_halves(x1b)
    logits = _dot(x1b, wr_ref[...].astype(BF16)) + br_ref[...]
    tm = logits.shape[0]
    lane = lax.broadcasted_iota(jnp.int32, (tm, LANES), 1).astype(F32)
    big = float(2 * LANES)
    neg = -jnp.inf
    lc = jnp.where(lane < N_GROUPS, logits, neg)
    mc = jnp.max(lc, axis=-1, keepdims=True)
    g_sel = jnp.min(jnp.where(lc == mc, lane, big), axis=-1, keepdims=True)
    p_group = 1.0 / jnp.sum(jnp.exp(lc - mc), axis=-1, keepdims=True)
    lo = ROUTE_FINE_OFF + g_sel * EXPERTS_PER_GROUP
    lf = jnp.where((lane >= lo) & (lane < lo + EXPERTS_PER_GROUP), logits, neg)
    v1 = jnp.max(lf, axis=-1, keepdims=True)
    i1 = jnp.min(jnp.where(lf == v1, lane, big), axis=-1, keepdims=True)
    lf2 = jnp.where(lane == i1, neg, lf)
    v2 = jnp.max(lf2, axis=-1, keepdims=True)
    i2 = jnp.min(jnp.where(lf2 == v2, lane, big), axis=-1, keepdims=True)
    e21 = jnp.exp(v2 - v1)
    gate1 = p_group / (1.0 + e21)
    gate2 = p_group * e21 / (1.0 + e21)
    route = jnp.where(lane == 0, i1 - ROUTE_FINE_OFF,
                      jnp.where(lane == 1, i2 - ROUTE_FINE_OFF,
                                jnp.where(lane == 2, gate1, jnp.where(lane == 3, gate2, 0.0))))
    route_ref[...] = route


N_ROUTER_OUTS = 3


def _outproj_router_kernel(n_tiles, n_aliased, attn_ref, rwkv_ref, x_ref, wo_ref, g_ref, b_ref, wr_ref, br_ref,
                           *rest):
    outs = rest[n_aliased:n_aliased + N_ROUTER_OUTS]
    mix_ref = rest[-1]
    i = pl.program_id(0)
    finish = lambda mix: _norm_and_route(mix, x_ref, g_ref, b_ref, wr_ref, br_ref, *outs)

    @pl.when(i == 0)
    def _():
        mix_ref[...] = _project_mix(attn_ref, rwkv_ref, wo_ref)

    @pl.when((i >= 1) & (i < n_tiles))
    def _():
        finish(mix_ref[...])
        mix_ref[...] = _project_mix(attn_ref, rwkv_ref, wo_ref)

    @pl.when(i == n_tiles)
    def _():
        finish(mix_ref[...])

    @pl.when(i > n_tiles)
    def _():
        for out_ref in outs:
            out_ref[...] = jnp.zeros_like(out_ref)


def _outproj_router(attn, rwkv, x, wo_bf16, ln_g, ln_b, w_route, b_route, tm, n_total, row_block, into, name):
    m = x.shape[0]
    n_tiles = m // tm
    const = lambda shape: pl.BlockSpec(shape, lambda i: (0, 0))
    ahead = lambda width: pl.BlockSpec((tm, width), lambda i: (jnp.minimum(i, n_tiles - 1), 0))
    behind = lambda width: pl.BlockSpec((tm, width), lambda i: (jnp.clip(i - 1, 0, n_tiles - 1), 0))
    in_specs = [ahead(D_ATTN), ahead(D_RWKV), behind(D_MODEL),
                const((D_MODEL, D_MODEL)), const((1, D_MODEL)), const((1, D_MODEL)),
                const((D_MODEL, LANES)), const((1, LANES))]
    args = [attn, rwkv, x, wo_bf16, ln_g, ln_b, w_route, b_route]
    aliases, n_aliased, fill_steps = {}, 0, pl.cdiv(n_total - m, tm)
    if into is not None:
        n_aliased, fill_steps = N_ROUTER_OUTS, 0
        in_specs += [pl.BlockSpec(memory_space=pl.ANY)] * N_ROUTER_OUTS
        aliases = {len(args) + k: k for k in range(N_ROUTER_OUTS)}
        args += list(into)
    out_rows = lambda width: pl.BlockSpec((tm, width), lambda i: (jnp.maximum(i - 1, 0) + row_block, 0))
    return pl.pallas_call(
        functools.partial(_outproj_router_kernel, n_tiles, n_aliased),
        out_shape=(jax.ShapeDtypeStruct((n_total, D_MODEL), F32),
                   jax.ShapeDtypeStruct((n_total, D_MODEL // 2), jnp.uint32),
                   jax.ShapeDtypeStruct((n_total, LANES), F32)),
        grid=(n_tiles + 1 + fill_steps,),
        in_specs=in_specs,
        out_specs=(out_rows(D_MODEL), out_rows(D_MODEL // 2), out_rows(LANES)),
        scratch_shapes=[pltpu.VMEM((tm, D_MODEL), F32)],
        input_output_aliases=aliases,
        compiler_params=_cparams(("arbitrary",)),
        name=name,
    )(*args)


DISPATCH_TILE = 128


def _dispatch_kernel(zoff_ref, dest_ref, x_ref, o_hbm, zbuf, ring, zsem, sem):
    i = pl.program_id(0)
    n_blocks = o_hbm.shape[0] // MOE_BLOCK
    n_used = zoff_ref[N_EXPERTS]

    def zero_fill(start_row):
        start_row = pl.multiple_of(start_row, MOE_BLOCK)
        return pltpu.make_async_copy(zbuf, o_hbm.at[pl.ds(start_row, MOE_BLOCK)], zsem)

    def zero_fills(action):
        for e in range(N_EXPERTS):
            @pl.when(zoff_ref[e] >= 0)
            def _():
                action(zero_fill(zoff_ref[e]))
        for b in range(n_blocks):
            @pl.when(b >= n_used)
            def _():
                action(zero_fill(b * MOE_BLOCK))

    @pl.when(i == 0)
    def _():
        zbuf[...] = jnp.zeros_like(zbuf)
        zero_fills(lambda copy: copy.start())
        zero_fills(lambda copy: copy.wait())

    cur = lax.rem(i, 2)

    def wait_rows(slot):
        for k in range(2):
            pltpu.make_async_copy(ring.at[slot], o_hbm.at[pl.ds(0, DISPATCH_TILE)], sem.at[slot]).wait()

    @pl.when(i >= 2)
    def _():
        wait_rows(cur)

    ring[cur] = x_ref[...]
    for t in range(DISPATCH_TILE):
        for k in range(2):
            pltpu.make_async_copy(ring.at[cur, pl.ds(t, 1)], o_hbm.at[pl.ds(dest_ref[0, 0, 2 * t + k], 1)],
                                  sem.at[cur]).start(priority=k)

    @pl.when(i == pl.num_programs(0) - 1)
    def _():
        wait_rows(cur)

        @pl.when(i >= 1)
        def _():
            wait_rows(1 - cur)


def _dispatch(zero_offsets, dest, x_packed, n_blocks):
    n_tokens, width = x_packed.shape
    grid_spec = pltpu.PrefetchScalarGridSpec(
        num_scalar_prefetch=1,
        grid=(n_tokens // DISPATCH_TILE,),
        in_specs=[pl.BlockSpec((1, 1, 2 * DISPATCH_TILE), lambda i, z: (i, 0, 0), memory_space=pltpu.SMEM),
                  pl.BlockSpec((DISPATCH_TILE, width), lambda i, z: (i, 0))],
        out_specs=pl.BlockSpec(memory_space=pl.ANY),
        scratch_shapes=[pltpu.VMEM((MOE_BLOCK, width), x_packed.dtype),
                        pltpu.VMEM((2, DISPATCH_TILE, width), x_packed.dtype),
                        pltpu.SemaphoreType.DMA, pltpu.SemaphoreType.DMA((2,))],
    )
    return pl.pallas_call(
        _dispatch_kernel,
        out_shape=jax.ShapeDtypeStruct((n_blocks * MOE_BLOCK, width), x_packed.dtype),
        grid_spec=grid_spec,
        compiler_params=_cparams(("arbitrary",)),
        name="moe_dispatch",
    )(zero_offsets, dest.reshape(-1, 1, 2 * DISPATCH_TILE), x_packed)


def _expert_kernel(be_ref, nb_ref, x_ref, wg_hbm, wu_hbm, wd_hbm, o_ref, wg_buf, wu_buf, wd_buf, slot_ref, sem):
    blk = pl.program_id(0)
    n_used = nb_ref[0]
    expert = be_ref[blk]
    is_first = (blk == 0) | (be_ref[jnp.maximum(blk - 1, 0)] != expert)

    def fetch(e, slot):
        return [pltpu.make_async_copy(hbm.at[e], buf.at[slot], sem.at[slot, i])
                for i, (hbm, buf) in enumerate(((wg_hbm, wg_buf), (wu_hbm, wu_buf), (wd_hbm, wd_buf)))]

    @pl.when((blk < n_used) & is_first)
    def _():
        @pl.when(blk == 0)
        def _():
            slot_ref[0] = 1
            for copy in fetch(expert, 0):
                copy.start()

        slot = 1 - slot_ref[0]
        slot_ref[0] = slot
        for copy in fetch(expert, slot):
            copy.wait()
        nxt = lax.while_loop(lambda j: (j < n_used) & (be_ref[jnp.minimum(j, n_used - 1)] == expert),
                             lambda j: j + 1, blk + 1)

        @pl.when(nxt < n_used)
        def _():
            for copy in fetch(be_ref[jnp.minimum(nxt, n_used - 1)], 1 - slot):
                copy.start()

    @pl.when(blk < n_used)
    def _():
        slot = slot_ref[0]
        half = D_MODEL // 2
        x_head, x_tail = _unpack_bf16_halves(x_ref[...])
        proj = lambda w_buf: (_dot(x_head, w_buf[slot, 0:half, :].astype(BF16))
                              + _dot(x_tail, w_buf[slot, half:D_MODEL, :].astype(BF16)))
        gate = proj(wg_buf)
        up = proj(wu_buf)
        h = gate * _sigmoid(gate) * up
        o_ref[...] = _dot(h.astype(BF16), wd_buf[slot].astype(BF16))

    @pl.when(blk >= n_used)
    def _():
        o_ref[...] = jnp.zeros_like(o_ref)


def _expert_mlp(block_expert, n_used, x_sorted, w_gate, w_up, w_down, n_blocks):
    grid_spec = pltpu.PrefetchScalarGridSpec(
        num_scalar_prefetch=2,
        grid=(n_blocks,),
        in_specs=[pl.BlockSpec((MOE_BLOCK, D_MODEL // 2), lambda b, be, nb: (jnp.minimum(b, nb[0] - 1), 0)),
                  pl.BlockSpec(memory_space=pl.ANY), pl.BlockSpec(memory_space=pl.ANY),
                  pl.BlockSpec(memory_space=pl.ANY)],
        out_specs=pl.BlockSpec((MOE_BLOCK, D_MODEL), lambda b, be, nb: (b, 0)),
        scratch_shapes=[pltpu.VMEM((2, D_MODEL, D_EXPERT), F32), pltpu.VMEM((2, D_MODEL, D_EXPERT), F32),
                        pltpu.VMEM((2, D_EXPERT, D_MODEL), F32), pltpu.SMEM((1,), jnp.int32),
                        pltpu.SemaphoreType.DMA((2, 3))],
    )
    return pl.pallas_call(
        _expert_kernel,
        out_shape=jax.ShapeDtypeStruct((n_blocks * MOE_BLOCK, D_MODEL), F32),
        grid_spec=grid_spec,
        compiler_params=_cparams(("arbitrary",)),
        name="expert_mlp",
    )(block_expert, n_used, x_sorted, w_gate, w_up, w_down)


COMBINE_TILE = 256


def _combine_kernel(dest_ref, dest_next_ref, y_hbm, x1_ref, route_ref, g_ref, b_ref, o_ref, ybuf, sem):
    i = pl.program_id(0)
    cur = lax.rem(i, 2)
    tile = x1_ref.shape[0]
    n_rows = 2 * tile

    def gather(table_ref, buf):
        for slot in range(n_rows):
            pltpu.make_async_copy(y_hbm.at[pl.ds(table_ref[0, 0, slot], 1)], ybuf.at[buf, pl.ds(slot, 1)],
                                  sem.at[buf]).start(priority=slot % 2)

    def wait_gather(buf):
        pltpu.make_async_copy(y_hbm.at[pl.ds(0, n_rows)], ybuf.at[buf], sem.at[buf]).wait()

    @pl.when(i == 0)
    def _():
        gather(dest_ref, 0)

    gather(dest_next_ref, 1 - cur)
    wait_gather(cur)
    route = route_ref[...]
    yb = ybuf[cur]
    moe = route[:, 2:3] * yb[0:tile, :] + route[:, 3:4] * yb[tile:n_rows, :]
    o_ref[...] = _layer_norm(ALPHA * x1_ref[...] + moe, g_ref[...], b_ref[...])

    @pl.when(i == pl.num_programs(0) - 1)
    def _():
        wait_gather(1 - cur)


def _combine(dest, y_slots, x1_all, route_all, m, tm, row_block, ln_g, ln_b, name):
    return pl.pallas_call(
        _combine_kernel,
        out_shape=jax.ShapeDtypeStruct((m, D_MODEL), F32),
        grid=(m // tm,),
        in_specs=[pl.BlockSpec((1, 1, 2 * tm), lambda i: (i, 0, 0), memory_space=pltpu.SMEM),
                  pl.BlockSpec((1, 1, 2 * tm), lambda i: (i + 1, 0, 0), memory_space=pltpu.SMEM),
                  pl.BlockSpec(memory_space=pl.ANY),
                  pl.BlockSpec((tm, D_MODEL), lambda i: (i + row_block, 0)),
                  pl.BlockSpec((tm, LANES), lambda i: (i + row_block, 0)),
                  pl.BlockSpec((1, D_MODEL), lambda i: (0, 0)),
                  pl.BlockSpec((1, D_MODEL), lambda i: (0, 0))],
        out_specs=pl.BlockSpec((tm, D_MODEL), lambda i: (i, 0)),
        scratch_shapes=[pltpu.VMEM((2, 2 * tm, D_MODEL), F32), pltpu.SemaphoreType.DMA((2,))],
        compiler_params=_cparams(("arbitrary",)),
        name=name,
    )(dest, dest, y_slots, x1_all, route_all, ln_g, ln_b)


def _dispatch_plan(route_all, n_blocks):
    flat_e = route_all[:, 0:2].astype(jnp.int32).reshape(-1)
    onehot = (flat_e[:, None] == jnp.arange(N_EXPERTS, dtype=jnp.int32)[None, :]).astype(jnp.int32)
    csum = jnp.cumsum(onehot, axis=0)
    rank = jnp.sum(onehot * csum, axis=1) - 1
    counts = csum[-1]
    padded = (counts + MOE_BLOCK - 1) // MOE_BLOCK * MOE_BLOCK
    pend = jnp.cumsum(padded)
    pstart = pend - padded
    dest = (pstart[flat_e] + rank).astype(jnp.int32)
    zero_offsets = jnp.where(counts > 0, pend - MOE_BLOCK, -1).astype(jnp.int32)
    n_used = (pend[-1] // MOE_BLOCK).astype(jnp.int32)
    block_start = jnp.minimum(jnp.arange(n_blocks, dtype=jnp.int32), n_used - 1) * MOE_BLOCK
    block_e = jnp.minimum(jnp.searchsorted(pend, block_start, side="right"), N_EXPERTS - 1).astype(jnp.int32)
    return dest, jnp.concatenate([zero_offsets, n_used.reshape(1)]), block_e, n_used.reshape(1)


def kernel(x_prompt, x_sample, cache_k_win, cache_v_win, state_wkv, state_shift, w_in, attn_sinks, shift_mu, w0,
           w_decay_up, a0, w_a_up, w_g_up, k_k, k_a, r_k, gn_g, gn_b, w_out, ln1_g, ln1_b, w_coarse, b_coarse,
           w_fine, b_fine, w_exp_gate, w_exp_up, w_exp_down, ln2_g, ln2_b):
    xp = x_prompt[0]
    xs = x_sample[:, 0]
    row = lambda a: a.reshape(1, -1)

    w_in_t = jnp.swapaxes(w_in[0], 0, 1)
    prm = dict(mu=row(shift_mu[0, :D_RKV]), mu_tail=row(shift_mu[0, D_RKV:]), w_tail=w_in_t[D_MAIN:].astype(BF16),
               w0=row(w0[0]), a0=row(a0[0]), k_k=row(k_k[0]), k_a=row(k_a[0]),
               r_k=row(r_k[0]), gn_g=row(gn_g[0]), gn_b=row(gn_b[0]),
               wd=w_decay_up[0], wa=w_a_up[0], wg=w_g_up[0])
    sinks = attn_sinks[0]
    wo_bf16 = w_out[0].astype(BF16)
    w_route = jnp.pad(jnp.concatenate([w_coarse[0], w_fine[0]], axis=1), ((0, 0), (0, LANES - N_GROUPS - N_EXPERTS)))
    b_route = jnp.pad(jnp.concatenate([b_coarse[0], b_fine[0]]), (0, LANES - N_GROUPS - N_EXPERTS)).reshape(1, LANES)

    hp, tail_p = _matmul_with_tail(xp, w_in_t, D_MAIN, prm["w_tail"], MAIN_TM, MAIN_TN, "in_proj_prompt")
    hs = _matmul(xs, w_in_t, D_MAIN, DEC_BATCH, MAIN_TN, "in_proj_sample")

    attn_p = _prompt_attention(hp, sinks)
    rwkv_p, state_p = _prompt_rwkv(hp, tail_p, prm)

    q_s = hs[:, :D_ATTN].reshape(DEC_BATCH, N_Q_HEADS, HEAD_DIM)
    k_s = hs[:, D_ATTN:D_ATTN + D_KV].reshape(DEC_BATCH, 1, D_KV)
    v_s = hs[:, D_ATTN + D_KV:D_QKV].reshape(DEC_BATCH, 1, D_KV)
    window_t = lambda c: jnp.transpose(c, (0, 2, 3, 1)).reshape(DEC_BATCH, D_KV, WINDOW)
    attn_s, kwin_s, vwin_s = _sample_attention(
        q_s, k_s, v_s, window_t(cache_k_win[0]), window_t(cache_v_win[0]), sinks.reshape(N_Q_HEADS, 1))
    r_s, k2_s, vv_s, g_s, tail_s, r_t, w_t, k_t, v_t, a_t, b_t = _sample_prep(hs, xs, state_shift[0], prm)
    y_t, state_s = _sample_step(jnp.transpose(state_wkv[0], (1, 2, 3, 0)), r_t, w_t, k_t, a_t, b_t, v_t)
    state_s = state_s.reshape(DEC_BATCH, N_RWKV_HEADS, HEAD_DIM, HEAD_DIM)
    rwkv_s = _sample_post(y_t, r_s, k2_s, vv_s, g_s, prm)

    n_tokens = SEQ + DEC_BATCH
    outs_pr = _outproj_router(attn_p, rwkv_p, xp, wo_bf16, row(ln1_g[0]), row(ln1_b[0]), w_route, b_route,
                              OUTPROJ_TM, n_tokens, 0, None, "outproj_router_prompt")
    x1_all, x1b_all, route_all = _outproj_router(attn_s.reshape(DEC_BATCH, D_ATTN), rwkv_s, xs,
                                                 wo_bf16, row(ln1_g[0]), row(ln1_b[0]), w_route, b_route,
                                                 DEC_BATCH, n_tokens, SEQ // DEC_BATCH, outs_pr,
                                                 "outproj_router_sample")

    n_assign = 2 * n_tokens
    n_blocks = -(-(n_assign + N_EXPERTS * (MOE_BLOCK - 1)) // MOE_BLOCK)
    dest, zero_offsets, block_e, n_used = _dispatch_plan(route_all, n_blocks)
    x_sorted = _dispatch(zero_offsets, dest, x1b_all, n_blocks)
    y_slots = _expert_mlp(block_e, n_used, x_sorted, w_exp_gate[0], w_exp_up[0], w_exp_down[0], n_blocks)

    def dest_tiles(d, tile):
        d = d.reshape(-1, tile, 2)
        d = jnp.concatenate([d[:, :, 0], d[:, :, 1]], axis=1)
        return jnp.pad(d, ((0, 1), (0, 0))).reshape(-1, 1, 2 * tile)

    y_p = _combine(dest_tiles(dest[:2 * SEQ], COMBINE_TILE), y_slots, x1_all, route_all, SEQ, COMBINE_TILE, 0,
                   row(ln2_g[0]), row(ln2_b[0]), "combine_prompt")
    y_s = _combine(dest_tiles(dest[2 * SEQ:], DEC_BATCH), y_slots, x1_all, route_all, DEC_BATCH, DEC_BATCH,
                   SEQ // DEC_BATCH, row(ln2_g[0]), row(ln2_b[0]), "combine_sample")

    kv4 = lambda a: a.reshape(a.shape[0], N_KV_HEADS, HEAD_DIM)
    k_win_p = kv4(hp[SEQ - WINDOW:, D_ATTN:D_ATTN + D_KV])[None, None]
    v_win_p = kv4(hp[SEQ - WINDOW:, D_ATTN + D_KV:D_QKV])[None, None]
    sp = state_p.reshape(N_PAIRS, HEADS_PER_TILE, HEAD_DIM, HEADS_PER_TILE, HEAD_DIM)
    wkv_p = jnp.stack([sp[:, i, :, i, :] for i in range(HEADS_PER_TILE)], axis=1)
    wkv_p = wkv_p.reshape(N_RWKV_HEADS, HEAD_DIM, HEAD_DIM).transpose(0, 2, 1)[None, None]
    shift_p = jnp.concatenate([hp[SEQ - 1:SEQ, D_QKV:], tail_p[SEQ - 1:SEQ]], axis=1)[None]
    shift_s = jnp.concatenate([hs[:, D_QKV:], tail_s], axis=1)[None]
    return (y_p[None], y_s[:, None, :], k_win_p, v_win_p, wkv_p, shift_p,
            kwin_s.reshape(1, DEC_BATCH, WINDOW, N_KV_HEADS, HEAD_DIM),
            vwin_s.reshape(1, DEC_BATCH, WINDOW, N_KV_HEADS, HEAD_DIM),
            state_s[None], shift_s)
```

```python
import functools
import math

import jax
import jax.numpy as jnp
from jax import lax
from jax.experimental import pallas as pl
from jax.experimental.pallas import tpu as pltpu

F32 = jnp.float32
BF16 = jnp.bfloat16

D_MODEL = 2048
SEQ = 8192
DEC_BATCH = 128
HEAD_DIM = 64
D_ATTN = 1024
D_RWKV = 1024
N_Q_HEADS = 16
N_KV_HEADS = 4
Q_PER_KV = 4
D_KV = 256
WINDOW = 128
ATTN_SCALE = HEAD_DIM ** -0.5
N_RWKV_HEADS = 16
W_LORA = 64
A_LORA = 64
G_LORA = 160
D_SHIFT = 3 * D_RWKV + W_LORA + A_LORA + G_LORA
D_QKV = D_ATTN + 2 * D_KV
N_GROUPS = 4
EXPERTS_PER_GROUP = 8
N_EXPERTS = 32
D_EXPERT = 512
ALPHA = 2.0 ** 0.25
LN_EPS = 1e-5
GN_EPS = 64e-5

SUBLANES = 8
LANES = 128
VMEM_LIMIT = 52 * 1024 * 1024

D_RKV = 3 * D_RWKV
D_TAIL = W_LORA + A_LORA + G_LORA
D_MAIN = D_QKV + D_RKV
MAIN_TN = 1536
MAIN_TM = 512
OUTPROJ_TM = 256

CHUNK = 64
HEADS_PER_TILE = LANES // HEAD_DIM
N_PAIRS = N_RWKV_HEADS // HEADS_PER_TILE
SOLVE_LEVELS = int(math.log2(CHUNK))
PAIR_GROUP = 8

MOE_BLOCK = 256
ROUTE_FINE_OFF = N_GROUPS

NN = (((1,), (0,)), ((), ()))
NT = (((1,), (1,)), ((), ()))


def _dot(a, b, dims=NN):
    return lax.dot_general(a, b, dims, preferred_element_type=F32)


def _dot1(a, b, dims=NN):
    return _dot(a.astype(BF16), b.astype(BF16), dims)


def _split(x):
    hi = x.astype(BF16)
    lo = (x - hi.astype(F32)).astype(BF16)
    return hi, lo


def _dot_exact_lhs(a_bf16, b, dims=NN):
    bh, bl = _split(b)
    return _dot(a_bf16, bh, dims) + _dot(a_bf16, bl, dims)


def _dot_exact_rhs(a, b_bf16, dims=NN):
    ah, al = _split(a)
    return _dot(ah, b_bf16, dims) + _dot(al, b_bf16, dims)


def _div_pow2(x, d):
    return lax.shift_right_logical(x, jnp.int32(int(math.log2(d))))


def _mod_pow2(x, d):
    return lax.bitwise_and(x, jnp.int32(d - 1))


def _pack_bf16_halves(x_bf16):
    n = x_bf16.shape[1] // 2
    bits = lax.bitcast_convert_type(x_bf16.astype(F32), jnp.uint32)
    return lax.bitwise_or(bits[:, 0:n], lax.shift_right_logical(bits[:, n:2 * n], jnp.uint32(16)))


def _unpack_bf16_halves(packed):
    hi = lax.bitcast_convert_type(lax.bitwise_and(packed, jnp.uint32(0xFFFF0000)), F32)
    lo = lax.bitcast_convert_type(lax.shift_left(packed, jnp.uint32(16)), F32)
    return hi.astype(BF16), lo.astype(BF16)


def _sigmoid(x):
    return 1.0 / (1.0 + jnp.exp(-x))


def _softplus(x):
    return jnp.maximum(x, 0.0) + jnp.log(1.0 + jnp.exp(-jnp.abs(x)))


def _layer_norm(z, g, b):
    mu = jnp.mean(z, axis=-1, keepdims=True)
    d = z - mu
    var = jnp.mean(d * d, axis=-1, keepdims=True)
    return d * lax.rsqrt(var + LN_EPS) * g + b


def _cparams(sem):
    return pltpu.CompilerParams(dimension_semantics=sem, vmem_limit_bytes=VMEM_LIMIT)


def _matmul_kernel(x_ref, wt_ref, o_ref):
    o_ref[...] = _dot(x_ref[...].astype(BF16), wt_ref[...].astype(BF16), NT)


def _matmul(x, w_t, n_out, tm, tn, name):
    m, k = x.shape
    tm = min(tm, m)
    return pl.pallas_call(
        _matmul_kernel,
        out_shape=jax.ShapeDtypeStruct((m, n_out), F32),
        grid=(n_out // tn, m // tm),
        in_specs=[pl.BlockSpec((tm, k), lambda j, i: (i, 0)),
                  pl.BlockSpec((tn, k), lambda j, i: (j, 0))],
        out_specs=pl.BlockSpec((tm, tn), lambda j, i: (i, j)),
        compiler_params=_cparams(("arbitrary", "arbitrary")),
        name=name,
    )(x, w_t)


def _matmul_with_tail_kernel(x_ref, wt_ref, wtail_ref, o_ref, tail_ref):
    xb = x_ref[...].astype(BF16)
    o_ref[...] = _dot(xb, wt_ref[...].astype(BF16), NT)

    @pl.when(pl.program_id(0) == 0)
    def _():
        tail_ref[...] = _dot(xb, wtail_ref[...], NT)

    @pl.when(pl.program_id(0) > 0)
    def _():
        tail_ref[...] = jnp.zeros_like(tail_ref)


def _matmul_with_tail(x, w_t, n_out, wtail_t, tm, tn, name):
    m, k = x.shape
    n_rows = m // tm
    n_tail = wtail_t.shape[0]
    return pl.pallas_call(
        _matmul_with_tail_kernel,
        out_shape=(jax.ShapeDtypeStruct((m, n_out), F32), jax.ShapeDtypeStruct((m + tm, n_tail), F32)),
        grid=(n_out // tn, n_rows),
        in_specs=[pl.BlockSpec((tm, k), lambda j, i: (i, 0)),
                  pl.BlockSpec((tn, k), lambda j, i: (j, 0)),
                  pl.BlockSpec((n_tail, k), lambda j, i: (0, 0))],
        out_specs=(pl.BlockSpec((tm, tn), lambda j, i: (i, j)),
                   pl.BlockSpec((tm, n_tail), lambda j, i: (jnp.where(j == 0, i, n_rows), 0))),
        compiler_params=_cparams(("arbitrary", "arbitrary")),
        name=name,
    )(x, w_t, wtail_t)


def _band_bias():
    qi = jnp.arange(Q_PER_KV * WINDOW)[:, None] % WINDOW
    kj = jnp.arange(2 * WINDOW)[None, :]
    diff = qi + WINDOW - kj
    band = (diff >= 0) & (diff <= WINDOW)
    keep = jnp.stack([band & (kj >= WINDOW), band])
    return jnp.where(keep, 0.0, -jnp.inf).astype(F32)


ATTN_BLOCKS = 2


def _prompt_attn_kernel(q_ref, kvp_ref, kvc_ref, bias_ref, sink_ref, o_ref):
    step = pl.program_id(0)
    kv_prev = kvp_ref[...]
    kv_cur = kvc_ref[...]
    bias_main = bias_ref[1]
    bias_first = jnp.where(step == 0, bias_ref[0], bias_main)
    row_head = _div_pow2(lax.broadcasted_iota(jnp.int32, (Q_PER_KV * WINDOW, 1), 0), WINDOW)
    chains = [(b, g) for b in range(ATTN_BLOCKS) for g in range(N_KV_HEADS)]
    block = lambda a, b: a[b * WINDOW:(b + 1) * WINDOW]

    def kv_cols(off, b, g):
        cols = slice(off + g * HEAD_DIM, off + (g + 1) * HEAD_DIM)
        past = kv_prev if b == 0 else block(kv_cur, b - 1)
        return jnp.concatenate([past[:, cols], block(kv_cur, b)[:, cols]], axis=0).astype(BF16)

    def q_rows(b, g):
        q = q_ref[b * WINDOW:(b + 1) * WINDOW, :]
        return jnp.concatenate(
            [q[:, (g * Q_PER_KV + h) * HEAD_DIM:(g * Q_PER_KV + h + 1) * HEAD_DIM] for h in range(Q_PER_KV)],
            axis=0).astype(BF16)

    sink = []
    for g in range(N_KV_HEADS):
        col = jnp.zeros((Q_PER_KV * WINDOW, 1), F32)
        for h in range(Q_PER_KV):
            col = jnp.where(row_head == h, sink_ref[g * Q_PER_KV + h], col)
        sink.append(col)
    s = {(b, g): _dot(q_rows(b, g), kv_cols(0, b, g), NT) * ATTN_SCALE + (bias_first if b == 0 else bias_main)
         for b, g in chains}
    m = {c: jnp.maximum(jnp.max(s[c], axis=-1, keepdims=True), sink[c[1]]) for c in chains}
    p = {c: jnp.exp(s[c] - m[c]) for c in chains}
    denom = {c: jnp.sum(p[c], axis=-1, keepdims=True) + jnp.exp(sink[c[1]] - m[c]) for c in chains}
    o = {(b, g): _dot((p[b, g] / denom[b, g]).astype(BF16), kv_cols(D_KV, b, g)) for b, g in chains}
    for b in range(ATTN_BLOCKS):
        o_ref[b * WINDOW:(b + 1) * WINDOW, :] = jnp.concatenate(
            [o[b, g][h * WINDOW:(h + 1) * WINDOW, :] for g in range(N_KV_HEADS) for h in range(Q_PER_KV)], axis=1)


def _prompt_attention(h_attn, sinks):
    rows = ATTN_BLOCKS * WINDOW
    bias = _band_bias()
    return pl.pallas_call(
        _prompt_attn_kernel,
        out_shape=jax.ShapeDtypeStruct((SEQ, D_ATTN), F32),
        grid=(SEQ // rows,),
        in_specs=[pl.BlockSpec((rows, D_ATTN), lambda i: (i, 0)),
                  pl.BlockSpec((WINDOW, 2 * D_KV), lambda i: (jnp.maximum(ATTN_BLOCKS * i - 1, 0), 2)),
                  pl.BlockSpec((rows, 2 * D_KV), lambda i: (i, 2)),
                  pl.BlockSpec(bias.shape, lambda i: (0, 0, 0)),
                  pl.BlockSpec(memory_space=pltpu.SMEM)],
        out_specs=pl.BlockSpec((rows, D_ATTN), lambda i: (i, 0)),
        compiler_params=_cparams(("arbitrary",)),
        name="prompt_attention",
    )(h_attn, h_attn, h_attn, bias, sinks)


SAMPLE_ATTN_TILE = 8


def _sample_attn_kernel(q_ref, knew_ref, vnew_ref, ck_ref, cv_ref, sink_ref, o_ref, kwin_ref, vwin_ref):
    lane = lax.broadcasted_iota(jnp.int32, (N_Q_HEADS, D_KV), 1)
    head = lax.broadcasted_iota(jnp.int32, (N_Q_HEADS, D_KV), 0)
    group_mask = _div_pow2(lane, HEAD_DIM) == _div_pow2(head, Q_PER_KV)
    sink = sink_ref[...]
    row = lax.broadcasted_iota(jnp.int32, (WINDOW, D_KV), 0)
    seqs = range(SAMPLE_ATTN_TILE)
    qbd = [jnp.where(group_mask, jnp.concatenate([q_ref[b]] * N_KV_HEADS, axis=1), 0.0).astype(BF16) for b in seqs]
    s = [_dot1(qbd[b], ck_ref[b]) * ATTN_SCALE for b in seqs]
    s_new = [jnp.sum(qbd[b].astype(F32) * knew_ref[b].astype(BF16).astype(F32), axis=-1, keepdims=True) * ATTN_SCALE
             for b in seqs]
    m = [jnp.maximum(jnp.maximum(jnp.max(s[b], axis=-1, keepdims=True), s_new[b]), sink) for b in seqs]
    p = [jnp.exp(s[b] - m[b]) for b in seqs]
    p_new = [jnp.exp(s_new[b] - m[b]) for b in seqs]
    denom = [jnp.sum(p[b], axis=-1, keepdims=True) + p_new[b] + jnp.exp(sink - m[b]) for b in seqs]
    for b in seqs:
        kb = ck_ref[b].T
        vb = cv_ref[b].T
        kn = knew_ref[b]
        vn = vnew_ref[b]
        o_full = (_dot1(p[b] / denom[b], vb)
                  + (p_new[b] / denom[b]).astype(BF16).astype(F32) * vn.astype(BF16).astype(F32))
        o_full = jnp.where(group_mask, o_full, 0.0)
        o = o_full[:, 0:HEAD_DIM]
        for g in range(1, N_KV_HEADS):
            o = o + o_full[:, g * HEAD_DIM:(g + 1) * HEAD_DIM]
        o_ref[b] = o
        kwin_ref[b] = jnp.where(row == WINDOW - 1, kn, pltpu.roll(kb, WINDOW - 1, axis=0))
        vwin_ref[b] = jnp.where(row == WINDOW - 1, vn, pltpu.roll(vb, WINDOW - 1, axis=0))


def _sample_attention(q, k_new, v_new, cache_k_t, cache_v_t, sinks):
    bt = SAMPLE_ATTN_TILE
    win_spec = pl.BlockSpec((bt, WINDOW, D_KV), lambda i: (i, 0, 0))
    win_t_spec = pl.BlockSpec((bt, D_KV, WINDOW), lambda i: (i, 0, 0))
    new_spec = pl.BlockSpec((bt, 1, D_KV), lambda i: (i, 0, 0))
    return pl.pallas_call(
        _sample_attn_kernel,
        out_shape=(jax.ShapeDtypeStruct((DEC_BATCH, N_Q_HEADS, HEAD_DIM), F32),
                   jax.ShapeDtypeStruct((DEC_BATCH, WINDOW, D_KV), F32),
                   jax.ShapeDtypeStruct((DEC_BATCH, WINDOW, D_KV), F32)),
        grid=(DEC_BATCH // bt,),
        in_specs=[pl.BlockSpec((bt, N_Q_HEADS, HEAD_DIM), lambda i: (i, 0, 0)),
                  new_spec, new_spec, win_t_spec, win_t_spec,
                  pl.BlockSpec((N_Q_HEADS, 1), lambda i: (0, 0))],
        out_specs=(pl.BlockSpec((bt, N_Q_HEADS, HEAD_DIM), lambda i: (i, 0, 0)), win_spec, win_spec),
        compiler_params=_cparams(("arbitrary",)),
        name="sample_attention",
    )(q, k_new, v_new, cache_k_t, cache_v_t, sinks)


def _head_ones():
    r = _div_pow2(lax.broadcasted_iota(jnp.int32, (LANES, LANES), 0), HEAD_DIM)
    c = _div_pow2(lax.broadcasted_iota(jnp.int32, (LANES, LANES), 1), HEAD_DIM)
    return jnp.where(r == c, 1.0, 0.0).astype(BF16)


def _head_sum(x, ones, passes=2):
    dot = _dot_exact_rhs if passes == 2 else _dot1
    parts = [dot(x[:, p * LANES:(p + 1) * LANES], ones) for p in range(x.shape[1] // LANES)]
    return jnp.concatenate(parts, axis=1)


def _token_mix(feat, shifted, mu):
    return feat + (shifted - feat) * mu


def _rwkv_prep(mixed, mixed_tail, w0, a0, k_k, k_a, wd, wa, wg, ones):
    r = mixed[:, 0:D_RWKV]
    k = mixed[:, D_RWKV:2 * D_RWKV]
    v = mixed[:, 2 * D_RWKV:3 * D_RWKV]
    xw = mixed_tail[:, 0:W_LORA]
    xa = mixed_tail[:, W_LORA:W_LORA + A_LORA]
    xg = mixed_tail[:, W_LORA + A_LORA:D_TAIL]
    w_log = -_softplus(-(w0 + _dot1(jnp.tanh(xw), wd))) - 0.5
    log_decay = -jnp.exp(w_log)
    a = _sigmoid(a0 + _dot1(xa, wa))
    g = _dot1(_sigmoid(xg), wg)
    kk = k * k_k
    kk = kk * lax.rsqrt(jnp.maximum(_head_sum(kk * kk, ones), 1e-24))
    k2 = k * (1.0 + (a - 1.0) * k_a)
    return r, log_decay, k2, v, -kk, kk * a, g


def _rwkv_post(y, r, k2, v, g, r_k, gn_g, gn_b, ones, passes=2):
    inv_n = 1.0 / HEAD_DIM
    mu = _head_sum(y, ones, passes) * inv_n
    d = y - mu
    var = _head_sum(d * d, ones, passes) * inv_n
    yn = d * lax.rsqrt(var + GN_EPS) * gn_g + gn_b
    bonus = _head_sum(r * k2 * r_k, ones, passes) * v
    return (yn + bonus) * g


(OP_AABS, OP_RABS, OP_AN, OP_RN, OP_BN, OP_KN, OP_BH, OP_KH, OP_V) = range(9)
N_OPS = 9


def _prompt_rwkv_kernel(f1_ref, f2_ref, tail_in_ref, mu_ref, mut_ref, w0_ref, a0_ref, kk_ref, ka_ref, rk_ref,
                        gng_ref, gnb_ref, wd_ref, wa_ref, wg_ref, out_ref, state_ref,
                        prev_ref, prevt_ref, s_ref, ops_ref, pc_ref, y_ref):
    c = pl.program_id(0)
    C = CHUNK

    @pl.when(c == 0)
    def _():
        prev_ref[...] = jnp.zeros_like(prev_ref)
        prevt_ref[...] = jnp.zeros_like(prevt_ref)
        s_ref[...] = jnp.zeros_like(s_ref)

    ones = _head_ones()
    row = lax.broadcasted_iota(jnp.int32, (C, 1), 0)

    def token_shift(feat, carry_ref):
        shifted = jnp.where(row == 0, carry_ref[0:1, :], pltpu.roll(feat, 1, axis=0))
        carry_ref[0:1, :] = feat[C - 1:C, :]
        return shifted

    feat = jnp.concatenate([f1_ref[...], f2_ref[...]], axis=1)
    tail = tail_in_ref[...]
    mixed = _token_mix(feat, token_shift(feat, prev_ref), mu_ref[...])
    mixed_tail = _token_mix(tail, token_shift(tail, prevt_ref), mut_ref[...])
    r, ld, k2, v, av, bv, g = _rwkv_prep(mixed, mixed_tail, w0_ref[...], a0_ref[...], kk_ref[...], ka_ref[...],
                                         wd_ref[...], wa_ref[...], wg_ref[...], ones)

    ti = lax.broadcasted_iota(jnp.int32, (C, C), 0)
    tj = lax.broadcasted_iota(jnp.int32, (C, C), 1)
    tri_incl = jnp.where(tj <= ti, 1.0, 0.0).astype(BF16)
    cs = _dot_exact_lhs(tri_incl, ld)
    cs_ref = cs[C // 2 - 1:C // 2, :]
    cs_end = cs[C - 1:C, :]
    e_prev = jnp.exp(cs - ld)
    e_cur = jnp.exp(cs)
    n_prev = jnp.exp(cs - ld - cs_ref)
    n_cur = jnp.exp(cs - cs_ref)
    n_inv = jnp.exp(cs_ref - cs)
    e_tail = jnp.exp(cs_end - cs)
    ops = {OP_AABS: av * e_prev, OP_RABS: r * e_cur, OP_AN: av * n_prev, OP_RN: r * n_cur,
           OP_BN: bv * n_inv, OP_KN: k2 * n_inv, OP_BH: bv * e_tail, OP_KH: k2 * e_tail, OP_V: v}
    p_end = jnp.exp(cs_end)
    for p in range(N_PAIRS):
        sl = slice(p * LANES, (p + 1) * LANES)
        for idx, val in ops.items():
            ops_ref[p, idx] = val[:, sl]
        pc_ref[p] = jnp.broadcast_to(p_end[:, sl], (SUBLANES, LANES))

    lane1 = lax.broadcasted_iota(jnp.int32, (C, LANES), 1)
    head0 = lane1 < HEAD_DIM
    r2 = lax.broadcasted_iota(jnp.int32, (2 * C, 2 * C), 0)
    c2 = lax.broadcasted_iota(jnp.int32, (2 * C, 2 * C), 1)
    tq = _mod_pow2(r2, C)
    tk = _mod_pow2(c2, C)
    band = (tk < tq) | ((tk == tq) & (r2 >= C))
    blockdiag = _div_pow2(r2, HEAD_DIM) == _div_pow2(c2, HEAD_DIM)

    op = lambda p, idx: ops_ref[p, idx]
    zero_half = jnp.zeros((C, LANES), F32)
    for pairs in [range(g, g + PAIR_GROUP) for g in range(0, N_PAIRS, PAIR_GROUP)]:
        gy = {p: _dot1(jnp.concatenate([op(p, OP_AABS), op(p, OP_RABS)], axis=0), s_ref[p]) for p in pairs}

        am0, am1 = {}, {}
        for p in pairs:
            a_n, r_n = op(p, OP_AN), op(p, OP_RN)
            b0, k0 = jnp.where(head0, op(p, OP_BN), 0.0), jnp.where(head0, op(p, OP_KN), 0.0)
            b1, k1 = jnp.where(head0, 0.0, op(p, OP_BN)), jnp.where(head0, 0.0, op(p, OP_KN))
            am = _dot1(jnp.concatenate([a_n, r_n], axis=0), jnp.concatenate([k0, b0, b1, k1], axis=0), NT)
            am0[p] = jnp.where(band, am[:, 0:2 * C], 0.0)
            am1[p] = jnp.where(band, am[:, 2 * C:4 * C], 0.0)

        w0, w1 = {}, {}
        for p in pairs:
            top0, top1 = am0[p][0:C], am1[p][0:C]
            ak = jnp.concatenate([jnp.where(head0, top0, 0.0), jnp.where(head0, 0.0, top1)], axis=0)
            vv = op(p, OP_V)
            g0 = gy[p][0:C]
            m = jnp.concatenate([g0, g0], axis=0) + _dot1(ak, jnp.concatenate([vv, vv], axis=0))
            w0[p] = jnp.where(head0, m[0:C], top0)
            w1[p] = jnp.where(head0, top1, m[C:2 * C])

        for lvl in range(SOLVE_LEVELS):
            prod0 = {p: _dot1(w0[p], jnp.concatenate([zero_half, w0[p]], axis=0)) for p in pairs}
            prod1 = {p: _dot1(w1[p], jnp.concatenate([w1[p], zero_half], axis=0)) for p in pairs}
            w0 = {p: jnp.where(head0, w0[p] + prod0[p], prod0[p]) for p in pairs}
            w1 = {p: jnp.where(head0, prod1[p], w1[p] + prod1[p]) for p in pairs}
        u = {p: jnp.where(head0, w0[p], w1[p]) for p in pairs}

        for p in pairs:
            vv = op(p, OP_V)
            y_lhs = jnp.concatenate([am0[p][C:2 * C], am1[p][C:2 * C]], axis=1)
            y_rhs = jnp.concatenate([jnp.where(head0, vv, 0.0), jnp.where(head0, u[p], 0.0),
                                     jnp.where(head0, 0.0, u[p]), jnp.where(head0, 0.0, vv)], axis=0)
            y_ref[p] = gy[p][C:2 * C] + _dot1(y_lhs, y_rhs)

        for p in pairs:
            decay_rows = jnp.broadcast_to(pc_ref[p][0:1, :], (LANES, LANES)).T
            upd_lhs = jnp.concatenate([op(p, OP_BH), op(p, OP_KH)], axis=0).T
            upd_rhs = jnp.concatenate([u[p], op(p, OP_V)], axis=0)
            s_ref[p] = s_ref[p] * decay_rows + jnp.where(blockdiag, _dot1(upd_lhs, upd_rhs), 0.0)

    y = jnp.concatenate([y_ref[p] for p in range(N_PAIRS)], axis=1)
    out_ref[...] = _rwkv_post(y, r, k2, v, g, rk_ref[...], gng_ref[...], gnb_ref[...], ones, passes=1)

    @pl.when(c == pl.num_programs(0) - 1)
    def _():
        state_ref[...] = s_ref[...]


def _prompt_rwkv(h_main, tail, prm):
    n_chunks = SEQ // CHUNK
    half = D_RKV // 2
    assert D_QKV == half
    vec = pl.BlockSpec((1, D_RWKV), lambda c: (0, 0))
    full = lambda a: pl.BlockSpec(a.shape, lambda c: (0,) * a.ndim)
    return pl.pallas_call(
        _prompt_rwkv_kernel,
        out_shape=(jax.ShapeDtypeStruct((SEQ, D_RWKV), F32),
                   jax.ShapeDtypeStruct((N_PAIRS, LANES, LANES), F32)),
        grid=(n_chunks,),
        in_specs=[pl.BlockSpec((CHUNK, half), lambda c: (c, 1)),
                  pl.BlockSpec((CHUNK, half), lambda c: (c, 2)),
                  pl.BlockSpec((CHUNK, D_TAIL), lambda c: (c, 0)),
                  full(prm["mu"]), full(prm["mu_tail"]),
                  vec, vec, vec, vec, vec, vec, vec,
                  full(prm["wd"]), full(prm["wa"]), full(prm["wg"])],
        out_specs=(pl.BlockSpec((CHUNK, D_RWKV), lambda c: (c, 0)),
                   pl.BlockSpec((N_PAIRS, LANES, LANES), lambda c: (0, 0, 0))),
        scratch_shapes=[pltpu.VMEM((SUBLANES, D_RKV), F32),
                        pltpu.VMEM((SUBLANES, D_TAIL), F32),
                        pltpu.VMEM((N_PAIRS, LANES, LANES), F32),
                        pltpu.VMEM((N_PAIRS, N_OPS, CHUNK, LANES), F32),
                        pltpu.VMEM((N_PAIRS, SUBLANES, LANES), F32),
                        pltpu.VMEM((N_PAIRS, CHUNK, LANES), F32)],
        compiler_params=_cparams(("arbitrary",)),
        name="prompt_rwkv",
    )(h_main, h_main, tail, prm["mu"], prm["mu_tail"], prm["w0"], prm["a0"], prm["k_k"], prm["k_a"],
      prm["r_k"], prm["gn_g"], prm["gn_b"], prm["wd"], prm["wa"], prm["wg"])


def _sample_prep_kernel(h_ref, x_ref, wt_ref, shift_ref, mu_ref, mut_ref, w0_ref, a0_ref, kk_ref, ka_ref,
                        wd_ref, wa_ref, wg_ref, r_ref, k_ref, v_ref, g_ref, tail_ref,
                        rt_ref, wtr_ref, kt_ref, vt_ref, at_ref, bt_ref):
    ones = _head_ones()
    feat = h_ref[:, D_QKV:D_MAIN]
    tail = _dot1(x_ref[...], wt_ref[...], NT)
    tail_ref[...] = tail
    mixed = _token_mix(feat, shift_ref[:, 0:D_RKV], mu_ref[...])
    mixed_tail = _token_mix(tail, shift_ref[:, D_RKV:D_SHIFT], mut_ref[...])
    r, ld, k2, v, av, bv, g = _rwkv_prep(mixed, mixed_tail, w0_ref[...], a0_ref[...], kk_ref[...], ka_ref[...],
                                         wd_ref[...], wa_ref[...], wg_ref[...], ones)
    r_ref[...] = r
    k_ref[...] = k2
    v_ref[...] = v
    g_ref[...] = g
    rt_ref[...] = r.T
    wtr_ref[...] = jnp.exp(ld).T
    kt_ref[...] = k2.T
    vt_ref[...] = v.T
    at_ref[...] = av.T
    bt_ref[...] = bv.T


def _sample_prep(h_main, x, shift, prm):
    tok = jax.ShapeDtypeStruct((DEC_BATCH, D_RWKV), F32)
    chan = jax.ShapeDtypeStruct((D_RWKV, DEC_BATCH), F32)
    return pl.pallas_call(
        _sample_prep_kernel,
        out_shape=(tok,) * 4 + (jax.ShapeDtypeStruct((DEC_BATCH, D_TAIL), F32),) + (chan,) * 6,
        compiler_params=pltpu.CompilerParams(vmem_limit_bytes=VMEM_LIMIT),
        name="sample_rwkv_prep",
    )(h_main, x, prm["w_tail"], shift, prm["mu"], prm["mu_tail"], prm["w0"], prm["a0"], prm["k_k"], prm["k_a"],
      prm["wd"], prm["wa"], prm["wg"])


STEP_GROUP = 4


def _sample_step_kernel(s_ref, r_ref, w_ref, k_ref, a_ref, b_ref, v_ref, y_ref, snew_ref):
    r, w, k, a, b = r_ref[...], w_ref[...], k_ref[...], a_ref[...], b_ref[...]
    for g0 in range(0, HEAD_DIM, 2 * STEP_GROUP):
        chans = range(g0, g0 + 2 * STEP_GROUP)
        sa = {i: jnp.sum(s_ref[0, i] * a, axis=0, keepdims=True) for i in chans}
        s_new = {i: s_ref[0, i] * w + sa[i] * b + v_ref[i:i + 1, :] * k for i in chans}
        for i in chans:
            y_ref[i:i + 1, :] = jnp.sum(s_new[i] * r, axis=0, keepdims=True)
        for i in range(g0, g0 + 2 * STEP_GROUP, 2):
            pair = jnp.concatenate([s_new[i], s_new[i + 1]], axis=0)
            snew_ref[:, i * HEAD_DIM:(i + 2) * HEAD_DIM] = pair.T


def _sample_step(state_t, r_t, w_t, k_t, a_t, b_t, v_t):
    head_rows = pl.BlockSpec((HEAD_DIM, DEC_BATCH), lambda h: (h, 0))
    return pl.pallas_call(
        _sample_step_kernel,
        out_shape=(jax.ShapeDtypeStruct((D_RWKV, DEC_BATCH), F32),
                   jax.ShapeDtypeStruct((DEC_BATCH, N_RWKV_HEADS * HEAD_DIM * HEAD_DIM), F32)),
        grid=(N_RWKV_HEADS,),
        in_specs=[pl.BlockSpec((1, HEAD_DIM, HEAD_DIM, DEC_BATCH), lambda h: (h, 0, 0, 0))] + [head_rows] * 6,
        out_specs=(head_rows, pl.BlockSpec((DEC_BATCH, HEAD_DIM * HEAD_DIM), lambda h: (0, h))),
        compiler_params=_cparams(("arbitrary",)),
        name="sample_rwkv_step",
    )(state_t, r_t, w_t, k_t, a_t, b_t, v_t)


def _sample_post_kernel(yt_ref, r_ref, k_ref, v_ref, g_ref, rk_ref, gng_ref, gnb_ref, o_ref):
    o_ref[...] = _rwkv_post(yt_ref[...].T, r_ref[...], k_ref[...], v_ref[...], g_ref[...], rk_ref[...],
                            gng_ref[...], gnb_ref[...], _head_ones())


def _sample_post(y, r, k, v, g, prm):
    return pl.pallas_call(
        _sample_post_kernel,
        out_shape=jax.ShapeDtypeStruct((DEC_BATCH, D_RWKV), F32),
        compiler_params=pltpu.CompilerParams(vmem_limit_bytes=VMEM_LIMIT),
        name="sample_rwkv_post",
    )(y, r, k, v, g, prm["r_k"], prm["gn_g"], prm["gn_b"])


def _project_mix(attn_ref, rwkv_ref, wo_ref):
    return (_dot(attn_ref[...].astype(BF16), wo_ref[0:D_ATTN, :])
            + _dot(rwkv_ref[...].astype(BF16), wo_ref[D_ATTN:D_ATTN + D_RWKV, :]))


def _norm_and_route(mix, x_ref, g_ref, b_ref, wr_ref, br_ref, x1_ref, x1b_ref, route_ref):
    x1 = _layer_norm(ALPHA * x_ref[...] + mix, g_ref[...], b_ref[...])
    x1_ref[...] = x1
    x1b = x1.astype(BF16)
    x1b_ref[...] = _pack_bf16_halves(x1b)
    logits = _dot(x1b, wr_ref[...].astype(BF16)) + br_ref[...]
    tm = logits.shape[0]
    lane = lax.broadcasted_iota(jnp.int32, (tm, LANES), 1).astype(F32)
    big = float(2 * LANES)
    neg = -jnp.inf
    lc = jnp.where(lane < N_GROUPS, logits, neg)
    mc = jnp.max(lc, axis=-1, keepdims=True)
    g_sel = jnp.min(jnp.where(lc == mc, lane, big), axis=-1, keepdims=True)
    p_group = 1.0 / jnp.sum(jnp.exp(lc - mc), axis=-1, keepdims=True)
    lo = ROUTE_FINE_OFF + g_sel * EXPERTS_PER_GROUP
    lf = jnp.where((lane >= lo) & (lane < lo + EXPERTS_PER_GROUP), logits, neg)
    v1 = jnp.max(lf, axis=-1, keepdims=True)
    i1 = jnp.min(jnp.where(lf == v1, lane, big), axis=-1, keepdims=True)
    lf2 = jnp.where(lane == i1, neg, lf)
    v2 = jnp.max(lf2, axis=-1, keepdims=True)
    i2 = jnp.min(jnp.where(lf2 == v2, lane, big), axis=-1, keepdims=True)
    e21 = jnp.exp(v2 - v1)
    gate1 = p_group / (1.0 + e21)
    gate2 = p_group * e21 / (1.0 + e21)
    route = jnp.where(lane == 0, i1 - ROUTE_FINE_OFF,
                      jnp.where(lane == 1, i2 - ROUTE_FINE_OFF,
                                jnp.where(lane == 2, gate1, jnp.where(lane == 3, gate2, 0.0))))
    route_ref[...] = route


N_ROUTER_OUTS = 3


def _outproj_router_kernel(n_tiles, n_aliased, attn_ref, rwkv_ref, x_ref, wo_ref, g_ref, b_ref, wr_ref, br_ref,
                           *rest):
    outs = rest[n_aliased:n_aliased + N_ROUTER_OUTS]
    mix_ref = rest[-1]
    i = pl.program_id(0)
    finish = lambda mix: _norm_and_route(mix, x_ref, g_ref, b_ref, wr_ref, br_ref, *outs)

    @pl.when(i == 0)
    def _():
        mix_ref[...] = _project_mix(attn_ref, rwkv_ref, wo_ref)

    @pl.when((i >= 1) & (i < n_tiles))
    def _():
        finish(mix_ref[...])
        mix_ref[...] = _project_mix(attn_ref, rwkv_ref, wo_ref)

    @pl.when(i == n_tiles)
    def _():
        finish(mix_ref[...])

    @pl.when(i > n_tiles)
    def _():
        for out_ref in outs:
            out_ref[...] = jnp.zeros_like(out_ref)


def _outproj_router(attn, rwkv, x, wo_bf16, ln_g, ln_b, w_route, b_route, tm, n_total, row_block, into, name):
    m = x.shape[0]
    n_tiles = m // tm
    const = lambda shape: pl.BlockSpec(shape, lambda i: (0, 0))
    ahead = lambda width: pl.BlockSpec((tm, width), lambda i: (jnp.minimum(i, n_tiles - 1), 0))
    behind = lambda width: pl.BlockSpec((tm, width), lambda i: (jnp.clip(i - 1, 0, n_tiles - 1), 0))
    in_specs = [ahead(D_ATTN), ahead(D_RWKV), behind(D_MODEL),
                const((D_MODEL, D_MODEL)), const((1, D_MODEL)), const((1, D_MODEL)),
                const((D_MODEL, LANES)), const((1, LANES))]
    args = [attn, rwkv, x, wo_bf16, ln_g, ln_b, w_route, b_route]
    aliases, n_aliased, fill_steps = {}, 0, pl.cdiv(n_total - m, tm)
    if into is not None:
        n_aliased, fill_steps = N_ROUTER_OUTS, 0
        in_specs += [pl.BlockSpec(memory_space=pl.ANY)] * N_ROUTER_OUTS
        aliases = {len(args) + k: k for k in range(N_ROUTER_OUTS)}
        args += list(into)
    out_rows = lambda width: pl.BlockSpec((tm, width), lambda i: (jnp.maximum(i - 1, 0) + row_block, 0))
    return pl.pallas_call(
        functools.partial(_outproj_router_kernel, n_tiles, n_aliased),
        out_shape=(jax.ShapeDtypeStruct((n_total, D_MODEL), F32),
                   jax.ShapeDtypeStruct((n_total, D_MODEL // 2), jnp.uint32),
                   jax.ShapeDtypeStruct((n_total, LANES), F32)),
        grid=(n_tiles + 1 + fill_steps,),
        in_specs=in_specs,
        out_specs=(out_rows(D_MODEL), out_rows(D_MODEL // 2), out_rows(LANES)),
        scratch_shapes=[pltpu.VMEM((tm, D_MODEL), F32)],
        input_output_aliases=aliases,
        compiler_params=_cparams(("arbitrary",)),
        name=name,
    )(*args)


DISPATCH_TILE = 128


def _dispatch_kernel(zoff_ref, dest_ref, x_ref, o_hbm, zbuf, ring, zsem, sem):
    i = pl.program_id(0)
    n_blocks = o_hbm.shape[0] // MOE_BLOCK
    n_used = zoff_ref[N_EXPERTS]

    def zero_fill(start_row):
        start_row = pl.multiple_of(start_row, MOE_BLOCK)
        return pltpu.make_async_copy(zbuf, o_hbm.at[pl.ds(start_row, MOE_BLOCK)], zsem)

    def zero_fills(action):
        for e in range(N_EXPERTS):
            @pl.when(zoff_ref[e] >= 0)
            def _():
                action(zero_fill(zoff_ref[e]))
        for b in range(n_blocks):
            @pl.when(b >= n_used)
            def _():
                action(zero_fill(b * MOE_BLOCK))

    @pl.when(i == 0)
    def _():
        zbuf[...] = jnp.zeros_like(zbuf)
        zero_fills(lambda copy: copy.start())
        zero_fills(lambda copy: copy.wait())

    cur = lax.rem(i, 2)

    def wait_rows(slot):
        for k in range(2):
            pltpu.make_async_copy(ring.at[slot], o_hbm.at[pl.ds(0, DISPATCH_TILE)], sem.at[slot]).wait()

    @pl.when(i >= 2)
    def _():
        wait_rows(cur)

    ring[cur] = x_ref[...]
    for t in range(DISPATCH_TILE):
        for k in range(2):
            pltpu.make_async_copy(ring.at[cur, pl.ds(t, 1)], o_hbm.at[pl.ds(dest_ref[0, 0, 2 * t + k], 1)],
                                  sem.at[cur]).start()

    @pl.when(i == pl.num_programs(0) - 1)
    def _():
        wait_rows(cur)

        @pl.when(i >= 1)
        def _():
            wait_rows(1 - cur)


def _dispatch(zero_offsets, dest, x_packed, n_blocks):
    n_tokens, width = x_packed.shape
    grid_spec = pltpu.PrefetchScalarGridSpec(
        num_scalar_prefetch=1,
        grid=(n_tokens // DISPATCH_TILE,),
        in_specs=[pl.BlockSpec((1, 1, 2 * DISPATCH_TILE), lambda i, z: (i, 0, 0), memory_space=pltpu.SMEM),
                  pl.BlockSpec((DISPATCH_TILE, width), lambda i, z: (i, 0))],
        out_specs=pl.BlockSpec(memory_space=pl.ANY),
        scratch_shapes=[pltpu.VMEM((MOE_BLOCK, width), x_packed.dtype),
                        pltpu.VMEM((2, DISPATCH_TILE, width), x_packed.dtype),
                        pltpu.SemaphoreType.DMA, pltpu.SemaphoreType.DMA((2,))],
    )
    return pl.pallas_call(
        _dispatch_kernel,
        out_shape=jax.ShapeDtypeStruct((n_blocks * MOE_BLOCK, width), x_packed.dtype),
        grid_spec=grid_spec,
        compiler_params=_cparams(("arbitrary",)),
        name="moe_dispatch",
    )(zero_offsets, dest.reshape(-1, 1, 2 * DISPATCH_TILE), x_packed)


def _expert_kernel(be_ref, nb_ref, x_ref, wg_hbm, wu_hbm, wd_hbm, o_ref, wg_buf, wu_buf, wd_buf, slot_ref, sem):
    blk = pl.program_id(0)
    n_used = nb_ref[0]
    expert = be_ref[blk]
    is_first = (blk == 0) | (be_ref[jnp.maximum(blk - 1, 0)] != expert)

    def fetch(e, slot):
        return [pltpu.make_async_copy(hbm.at[e], buf.at[slot], sem.at[slot, i])
                for i, (hbm, buf) in enumerate(((wg_hbm, wg_buf), (wu_hbm, wu_buf), (wd_hbm, wd_buf)))]

    @pl.when((blk < n_used) & is_first)
    def _():
        @pl.when(blk == 0)
        def _():
            slot_ref[0] = 1
            for copy in fetch(expert, 0):
                copy.start()

        slot = 1 - slot_ref[0]
        slot_ref[0] = slot
        for copy in fetch(expert, slot):
            copy.wait()
        nxt = lax.while_loop(lambda j: (j < n_used) & (be_ref[jnp.minimum(j, n_used - 1)] == expert),
                             lambda j: j + 1, blk + 1)

        @pl.when(nxt < n_used)
        def _():
            for copy in fetch(be_ref[jnp.minimum(nxt, n_used - 1)], 1 - slot):
                copy.start()

    @pl.when(blk < n_used)
    def _():
        slot = slot_ref[0]
        half = D_MODEL // 2
        x_head, x_tail = _unpack_bf16_halves(x_ref[...])
        proj = lambda w_buf: (_dot(x_head, w_buf[slot, 0:half, :].astype(BF16))
                              + _dot(x_tail, w_buf[slot, half:D_MODEL, :].astype(BF16)))
        gate = proj(wg_buf)
        up = proj(wu_buf)
        h = gate * _sigmoid(gate) * up
        o_ref[...] = _dot(h.astype(BF16), wd_buf[slot].astype(BF16))

    @pl.when(blk >= n_used)
    def _():
        o_ref[...] = jnp.zeros_like(o_ref)


def _expert_mlp(block_expert, n_used, x_sorted, w_gate, w_up, w_down, n_blocks):
    grid_spec = pltpu.PrefetchScalarGridSpec(
        num_scalar_prefetch=2,
        grid=(n_blocks,),
        in_specs=[pl.BlockSpec((MOE_BLOCK, D_MODEL // 2), lambda b, be, nb: (jnp.minimum(b, nb[0] - 1), 0)),
                  pl.BlockSpec(memory_space=pl.ANY), pl.BlockSpec(memory_space=pl.ANY),
                  pl.BlockSpec(memory_space=pl.ANY)],
        out_specs=pl.BlockSpec((MOE_BLOCK, D_MODEL), lambda b, be, nb: (b, 0)),
        scratch_shapes=[pltpu.VMEM((2, D_MODEL, D_EXPERT), F32), pltpu.VMEM((2, D_MODEL, D_EXPERT), F32),
                        pltpu.VMEM((2, D_EXPERT, D_MODEL), F32), pltpu.SMEM((1,), jnp.int32),
                        pltpu.SemaphoreType.DMA((2, 3))],
    )
    return pl.pallas_call(
        _expert_kernel,
        out_shape=jax.ShapeDtypeStruct((n_blocks * MOE_BLOCK, D_MODEL), F32),
        grid_spec=grid_spec,
        compiler_params=_cparams(("arbitrary",)),
        name="expert_mlp",
    )(block_expert, n_used, x_sorted, w_gate, w_up, w_down)


COMBINE_TILE = 256


def _combine_kernel(dest_ref, dest_next_ref, y_hbm, x1_ref, route_ref, g_ref, b_ref, o_ref, ybuf, sem):
    i = pl.program_id(0)
    cur = lax.rem(i, 2)
    tile = x1_ref.shape[0]
    n_rows = 2 * tile

    def gather(table_ref, buf):
        for slot in range(n_rows):
            pltpu.make_async_copy(y_hbm.at[pl.ds(table_ref[0, 0, slot], 1)], ybuf.at[buf, pl.ds(slot, 1)],
                                  sem.at[buf]).start()

    def wait_gather(buf):
        pltpu.make_async_copy(y_hbm.at[pl.ds(0, n_rows)], ybuf.at[buf], sem.at[buf]).wait()

    @pl.when(i == 0)
    def _():
        gather(dest_ref, 0)

    gather(dest_next_ref, 1 - cur)
    wait_gather(cur)
    route = route_ref[...]
    yb = ybuf[cur]
    moe = route[:, 2:3] * yb[0:tile, :] + route[:, 3:4] * yb[tile:n_rows, :]
    o_ref[...] = _layer_norm(ALPHA * x1_ref[...] + moe, g_ref[...], b_ref[...])

    @pl.when(i == pl.num_programs(0) - 1)
    def _():
        wait_gather(1 - cur)


def _combine(dest, y_slots, x1_all, route_all, m, tm, row_block, ln_g, ln_b, name):
    return pl.pallas_call(
        _combine_kernel,
        out_shape=jax.ShapeDtypeStruct((m, D_MODEL), F32),
        grid=(m // tm,),
        in_specs=[pl.BlockSpec((1, 1, 2 * tm), lambda i: (i, 0, 0), memory_space=pltpu.SMEM),
                  pl.BlockSpec((1, 1, 2 * tm), lambda i: (i + 1, 0, 0), memory_space=pltpu.SMEM),
                  pl.BlockSpec(memory_space=pl.ANY),
                  pl.BlockSpec((tm, D_MODEL), lambda i: (i + row_block, 0)),
                  pl.BlockSpec((tm, LANES), lambda i: (i + row_block, 0)),
                  pl.BlockSpec((1, D_MODEL), lambda i: (0, 0)),
                  pl.BlockSpec((1, D_MODEL), lambda i: (0, 0))],
        out_specs=pl.BlockSpec((tm, D_MODEL), lambda i: (i, 0)),
        scratch_shapes=[pltpu.VMEM((2, 2 * tm, D_MODEL), F32), pltpu.SemaphoreType.DMA((2,))],
        compiler_params=_cparams(("arbitrary",)),
        name=name,
    )(dest, dest, y_slots, x1_all, route_all, ln_g, ln_b)


def _dispatch_plan(route_all, n_blocks):
    flat_e = route_all[:, 0:2].astype(jnp.int32).reshape(-1)
    onehot = (flat_e[:, None] == jnp.arange(N_EXPERTS, dtype=jnp.int32)[None, :]).astype(jnp.int32)
    csum = jnp.cumsum(onehot, axis=0)
    rank = jnp.sum(onehot * csum, axis=1) - 1
    counts = csum[-1]
    padded = (counts + MOE_BLOCK - 1) // MOE_BLOCK * MOE_BLOCK
    pend = jnp.cumsum(padded)
    pstart = pend - padded
    dest = (pstart[flat_e] + rank).astype(jnp.int32)
    zero_offsets = jnp.where(counts > 0, pend - MOE_BLOCK, -1).astype(jnp.int32)
    n_used = (pend[-1] // MOE_BLOCK).astype(jnp.int32)
    block_start = jnp.minimum(jnp.arange(n_blocks, dtype=jnp.int32), n_used - 1) * MOE_BLOCK
    block_e = jnp.minimum(jnp.searchsorted(pend, block_start, side="right"), N_EXPERTS - 1).astype(jnp.int32)
    return dest, jnp.concatenate([zero_offsets, n_used.reshape(1)]), block_e, n_used.reshape(1)


def kernel(x_prompt, x_sample, cache_k_win, cache_v_win, state_wkv, state_shift, w_in, attn_sinks, shift_mu, w0,
           w_decay_up, a0, w_a_up, w_g_up, k_k, k_a, r_k, gn_g, gn_b, w_out, ln1_g, ln1_b, w_coarse, b_coarse,
           w_fine, b_fine, w_exp_gate, w_exp_up, w_exp_down, ln2_g, ln2_b):
    xp = x_prompt[0]
    xs = x_sample[:, 0]
    row = lambda a: a.reshape(1, -1)

    w_in_t = jnp.swapaxes(w_in[0], 0, 1)
    prm = dict(mu=row(shift_mu[0, :D_RKV]), mu_tail=row(shift_mu[0, D_RKV:]), w_tail=w_in_t[D_MAIN:].astype(BF16),
               w0=row(w0[0]), a0=row(a0[0]), k_k=row(k_k[0]), k_a=row(k_a[0]),
               r_k=row(r_k[0]), gn_g=row(gn_g[0]), gn_b=row(gn_b[0]),
               wd=w_decay_up[0], wa=w_a_up[0], wg=w_g_up[0])
    sinks = attn_sinks[0]
    wo_bf16 = w_out[0].astype(BF16)
    w_route = jnp.pad(jnp.concatenate([w_coarse[0], w_fine[0]], axis=1), ((0, 0), (0, LANES - N_GROUPS - N_EXPERTS)))
    b_route = jnp.pad(jnp.concatenate([b_coarse[0], b_fine[0]]), (0, LANES - N_GROUPS - N_EXPERTS)).reshape(1, LANES)

    hp, tail_p = _matmul_with_tail(xp, w_in_t, D_MAIN, prm["w_tail"], MAIN_TM, MAIN_TN, "in_proj_prompt")
    hs = _matmul(xs, w_in_t, D_MAIN, DEC_BATCH, MAIN_TN, "in_proj_sample")

    attn_p = _prompt_attention(hp, sinks)
    rwkv_p, state_p = _prompt_rwkv(hp, tail_p, prm)

    q_s = hs[:, :D_ATTN].reshape(DEC_BATCH, N_Q_HEADS, HEAD_DIM)
    k_s = hs[:, D_ATTN:D_ATTN + D_KV].reshape(DEC_BATCH, 1, D_KV)
    v_s = hs[:, D_ATTN + D_KV:D_QKV].reshape(DEC_BATCH, 1, D_KV)
    window_t = lambda c: jnp.transpose(c, (0, 2, 3, 1)).reshape(DEC_BATCH, D_KV, WINDOW)
    attn_s, kwin_s, vwin_s = _sample_attention(
        q_s, k_s, v_s, window_t(cache_k_win[0]), window_t(cache_v_win[0]), sinks.reshape(N_Q_HEADS, 1))
    r_s, k2_s, vv_s, g_s, tail_s, r_t, w_t, k_t, v_t, a_t, b_t = _sample_prep(hs, xs, state_shift[0], prm)
    y_t, state_s = _sample_step(jnp.transpose(state_wkv[0], (1, 2, 3, 0)), r_t, w_t, k_t, a_t, b_t, v_t)
    state_s = state_s.reshape(DEC_BATCH, N_RWKV_HEADS, HEAD_DIM, HEAD_DIM)
    rwkv_s = _sample_post(y_t, r_s, k2_s, vv_s, g_s, prm)

    n_tokens = SEQ + DEC_BATCH
    outs_pr = _outproj_router(attn_p, rwkv_p, xp, wo_bf16, row(ln1_g[0]), row(ln1_b[0]), w_route, b_route,
                              OUTPROJ_TM, n_tokens, 0, None, "outproj_router_prompt")
    x1_all, x1b_all, route_all = _outproj_router(attn_s.reshape(DEC_BATCH, D_ATTN), rwkv_s, xs,
                                                 wo_bf16, row(ln1_g[0]), row(ln1_b[0]), w_route, b_route,
                                                 DEC_BATCH, n_tokens, SEQ // DEC_BATCH, outs_pr,
                                                 "outproj_router_sample")

    n_assign = 2 * n_tokens
    n_blocks = -(-(n_assign + N_EXPERTS * (MOE_BLOCK - 1)) // MOE_BLOCK)
    dest, zero_offsets, block_e, n_used = _dispatch_plan(route_all, n_blocks)
    x_sorted = _dispatch(zero_offsets, dest, x1b_all, n_blocks)
    y_slots = _expert_mlp(block_e, n_used, x_sorted, w_exp_gate[0], w_exp_up[0], w_exp_down[0], n_blocks)

    def dest_tiles(d, tile):
        d = d.reshape(-1, tile, 2)
        d = jnp.concatenate([d[:, :, 0], d[:, :, 1]], axis=1)
        return jnp.pad(d, ((0, 1), (0, 0))).reshape(-1, 1, 2 * tile)

    y_p = _combine(dest_tiles(dest[:2 * SEQ], COMBINE_TILE), y_slots, x1_all, route_all, SEQ, COMBINE_TILE, 0,
                   row(ln2_g[0]), row(ln2_b[0]), "combine_prompt")
    y_s = _combine(dest_tiles(dest[2 * SEQ:], DEC_BATCH), y_slots, x1_all, route_all, DEC_BATCH, DEC_BATCH,
                   SEQ // DEC_BATCH, row(ln2_g[0]), row(ln2_b[0]), "combine_sample")

    kv4 = lambda a: a.reshape(a.shape[0], N_KV_HEADS, HEAD_DIM)
    k_win_p = kv4(hp[SEQ - WINDOW:, D_ATTN:D_ATTN + D_KV])[None, None]
    v_win_p = kv4(hp[SEQ - WINDOW:, D_ATTN + D_KV:D_QKV])[None, None]
    sp = state_p.reshape(N_PAIRS, HEADS_PER_TILE, HEAD_DIM, HEADS_PER_TILE, HEAD_DIM)
    wkv_p = jnp.stack([sp[:, i, :, i, :] for i in range(HEADS_PER_TILE)], axis=1)
    wkv_p = wkv_p.reshape(N_RWKV_HEADS, HEAD_DIM, HEAD_DIM).transpose(0, 2, 1)[None, None]
    shift_p = jnp.concatenate([hp[SEQ - 1:SEQ, D_QKV:], tail_p[SEQ - 1:SEQ]], axis=1)[None]
    shift_s = jnp.concatenate([hs[:, D_QKV:], tail_s], axis=1)[None]
    return (y_p[None], y_s[:, None, :], k_win_p, v_win_p, wkv_p, shift_p,
            kwin_s.reshape(1, DEC_BATCH, WINDOW, N_KV_HEADS, HEAD_DIM),
            vwin_s.reshape(1, DEC_BATCH, WINDOW, N_KV_HEADS, HEAD_DIM),
            state_s[None], shift_s)
```
